```python
import jax, jax.numpy as jnp
from jax import lax
import numpy as np

D_MODEL = 2048
BATCH = 8
SEQ = 4096
DEPTH = 2

HEAD_DIM = 64
ATTN_WIDTH = D_MODEL // 2
ATTN_HEADS = ATTN_WIDTH // HEAD_DIM
DILATED_BRANCHES = ((128, 1), (512, 4), (2048, 16))
ATTN_BLOCK = 128

SSD_WIDTH = D_MODEL // 2
SSD_HEAD_DIM = 64
SSD_HEADS = SSD_WIDTH // SSD_HEAD_DIM
SSD_GROUPS = 2
SSD_STATE = 128
SSD_CONV = 4
SSD_CHUNK = 128
CONV_CH = SSD_WIDTH + 2 * SSD_GROUPS * SSD_STATE

MIX_WIDTH = ATTN_WIDTH + SSD_WIDTH
IN_PROJ = 3 * ATTN_WIDTH + SSD_WIDTH + CONV_CH + SSD_HEADS
D_FF = 4 * D_MODEL
NORM_EPS = 1e-5

kernel_name = "hybrid_ssd_dilated_alibi_block"


def alibi_slopes(n_heads):
    return jnp.asarray(2.0 ** (-8.0 * (np.arange(n_heads) + 1) / n_heads), dtype=jnp.float32)


def rmsnorm(x, g):
    x32 = x.astype(jnp.float32)
    y = x32 * lax.rsqrt(jnp.mean(x32 * x32, axis=-1, keepdims=True) + NORM_EPS)
    return (y * g.astype(jnp.float32)).astype(x.dtype)


def dilated_branch(q, k, v, slopes, window, dilation):
    b, s, h, dh = q.shape
    L = s // dilation
    nb = -(-L // ATTN_BLOCK)
    Lp = nb * ATTN_BLOCK
    steps = window // dilation

    def to_blocks(t):
        t = t.astype(jnp.float32).reshape(b, L, dilation, h, dh).transpose(0, 2, 3, 1, 4)
        t = jnp.pad(t, ((0, 0), (0, 0), (0, 0), (0, Lp - L), (0, 0)))
        return t.reshape(b, dilation, h, nb, ATTN_BLOCK, dh)

    def with_prev(t):
        prev = jnp.pad(t, ((0, 0), (0, 0), (0, 0), (1, 0), (0, 0), (0, 0)))[:, :, :, :-1]
        return jnp.concatenate([prev, t], axis=4)

    qb, kb, vb = to_blocks(q), to_blocks(k), to_blocks(v)
    kc, vc = with_prev(kb), with_prev(vb)
    scores = jnp.einsum('brhnid,brhnjd->brhnij', qb, kc) * (dh ** -0.5)

    i = jnp.arange(ATTN_BLOCK)[:, None]
    j = jnp.arange(2 * ATTN_BLOCK)[None, :]
    delta = i - j + ATTN_BLOCK
    key_pos = jnp.arange(nb)[:, None, None] * ATTN_BLOCK - ATTN_BLOCK + j
    valid = (delta >= 0) & (delta <= steps) & (key_pos >= 0)
    bias = -slopes[:, None, None, None] * (delta * dilation).astype(jnp.float32)
    scores = jnp.where(valid, scores + bias, -jnp.inf)

    m = jnp.max(scores, axis=-1, keepdims=True)
    p = jnp.exp(scores - m)
    den = jnp.sum(p, axis=-1)
    out = jnp.einsum('brhnij,brhnjd->brhnid', p, vc) / den[..., None]
    lse = m[..., 0] + jnp.log(den)

    out = out.reshape(b, dilation, h, Lp, dh)[:, :, :, :L].transpose(0, 3, 1, 2, 4).reshape(b, s, h, dh)
    lse = lse.reshape(b, dilation, h, Lp)[:, :, :, :L].transpose(0, 3, 1, 2).reshape(b, s, h)
    return out, lse


def dilated_attention(q, k, v, slopes):
    outs, lses = [], []
    for window, dilation in DILATED_BRANCHES:
        o, l = dilated_branch(q, k, v, slopes, window, dilation)
        outs.append(o)
        lses.append(l)
    w = jax.nn.softmax(jnp.stack(lses, axis=0), axis=0)
    out = jnp.sum(w[..., None] * jnp.stack(outs, axis=0), axis=0)
    return out.astype(q.dtype)


def causal_depthwise_conv(u, w, bias):
    out = lax.conv_general_dilated(
        u, w[:, None, :], window_strides=(1,), padding=((SSD_CONV - 1, 0),),
        dimension_numbers=('NWC', 'WIO', 'NWC'), feature_group_count=u.shape[-1])
    return out + bias


def segsum_exp(a):
    cum = jnp.cumsum(a, axis=-1)
    diff = cum[..., :, None] - cum[..., None, :]
    T = a.shape[-1]
    mask = jnp.tril(jnp.ones((T, T), dtype=bool))
    return jnp.exp(jnp.where(mask, diff, -jnp.inf))


def ssd_scan(x, dt, a, b_in, c_in):
    bs, s, h, p = x.shape
    g, n, Q = SSD_GROUPS, SSD_STATE, SSD_CHUNK
    e = h // g
    nc = s // Q
    x = x.astype(jnp.float32)
    X = (x * dt[..., None]).reshape(bs, nc, Q, g, e, p)
    dA = (dt * a).reshape(bs, nc, Q, g, e).transpose(0, 3, 4, 1, 2)
    Bc = b_in.astype(jnp.float32).reshape(bs, nc, Q, g, n)
    Cc = c_in.astype(jnp.float32).reshape(bs, nc, Q, g, n)
    a_cum = jnp.cumsum(dA, axis=-1)

    Lmat = segsum_exp(dA)
    cb = jnp.einsum('bclgn,bcsgn->bgcls', Cc, Bc)
    y_diag = jnp.einsum('bgecls,bcsgep->bclgep', cb[:, :, None] * Lmat, X)

    decay_states = jnp.exp(a_cum[..., -1:] - a_cum)
    states = jnp.einsum('bcsgn,bgecs,bcsgep->bcgepn', Bc, decay_states, X)
    states = jnp.concatenate([jnp.zeros_like(states[:, :1]), states], axis=1)
    decay_chunk = segsum_exp(jnp.pad(a_cum[..., -1], ((0, 0), (0, 0), (0, 0), (1, 0))))
    states = jnp.einsum('bgezc,bcgepn->bzgepn', decay_chunk, states)[:, :-1]

    y_off = jnp.einsum('bclgn,bcgepn,bgecl->bclgep', Cc, states, jnp.exp(a_cum))
    return (y_diag + y_off).reshape(bs, s, h, p)


def hybrid_layer(x, ln1_g, w_in, conv_w, conv_b, dt_bias, a_log, d_skip,
                 attn_norm_g, ssd_norm_g, w_out, ln2_g, w_mlp_in, w_mlp_out, slopes):
    b, s, _ = x.shape
    h = rmsnorm(x, ln1_g)
    proj = h @ w_in
    cuts = [ATTN_WIDTH, 2 * ATTN_WIDTH, 3 * ATTN_WIDTH,
            3 * ATTN_WIDTH + SSD_WIDTH, 3 * ATTN_WIDTH + SSD_WIDTH + CONV_CH]
    q, k, v, z, xbc, dt_raw = jnp.split(proj, cuts, axis=-1)

    q = q.reshape(b, s, ATTN_HEADS, HEAD_DIM)
    k = k.reshape(b, s, ATTN_HEADS, HEAD_DIM)
    v = v.reshape(b, s, ATTN_HEADS, HEAD_DIM)
    attn = dilated_attention(q, k, v, slopes).reshape(b, s, ATTN_WIDTH)
    attn = rmsnorm(attn, attn_norm_g)

    xbc = jax.nn.silu(causal_depthwise_conv(xbc, conv_w, conv_b))
    xs, bm, cm = jnp.split(xbc, [SSD_WIDTH, SSD_WIDTH + SSD_GROUPS * SSD_STATE], axis=-1)
    xs = xs.reshape(b, s, SSD_HEADS, SSD_HEAD_DIM)
    dt = jax.nn.softplus(dt_raw.astype(jnp.float32) + dt_bias.astype(jnp.float32))
    a = -jnp.exp(a_log.astype(jnp.float32))
    y = ssd_scan(xs, dt, a, bm.reshape(b, s, SSD_GROUPS, SSD_STATE),
                 cm.reshape(b, s, SSD_GROUPS, SSD_STATE)).astype(x.dtype)
    y = y + d_skip[:, None] * xs
    y = y.reshape(b, s, SSD_WIDTH) * jax.nn.silu(z)
    y = rmsnorm(y.reshape(b, s, SSD_GROUPS, SSD_WIDTH // SSD_GROUPS),
                ssd_norm_g.reshape(SSD_GROUPS, SSD_WIDTH // SSD_GROUPS)).reshape(b, s, SSD_WIDTH)

    x = x + jnp.concatenate([attn, y], axis=-1) @ w_out

    h = rmsnorm(x, ln2_g)
    x = x + jnp.square(jax.nn.relu(h @ w_mlp_in)) @ w_mlp_out
    return x


def _fwd_setup_inputs(seed: int = 0) -> dict:
    key = jax.random.key(seed)
    ks = jax.random.split(key, 16)
    f32 = jnp.float32
    dt0 = jnp.exp(jax.random.uniform(ks[5], (DEPTH, SSD_HEADS), f32,
                                     minval=float(np.log(1e-3)), maxval=float(np.log(1e-1))))
    return {
        "x": jax.random.normal(ks[0], (BATCH, SEQ, D_MODEL), f32),
        "ln1_g": 1.0 + 0.02 * jax.random.normal(ks[1], (DEPTH, D_MODEL), f32),
        "w_in": jax.random.normal(ks[2], (DEPTH, D_MODEL, IN_PROJ), f32) * D_MODEL ** -0.5,
        "conv_w": jax.random.normal(ks[3], (DEPTH, SSD_CONV, CONV_CH), f32) * SSD_CONV ** -0.5,
        "conv_b": 0.02 * jax.random.normal(ks[4], (DEPTH, CONV_CH), f32),
        "dt_bias": dt0 + jnp.log(-jnp.expm1(-dt0)),
        "a_log": jnp.log(jax.random.uniform(ks[6], (DEPTH, SSD_HEADS), f32, minval=1.0, maxval=16.0)),
        "d_skip": 1.0 + 0.1 * jax.random.normal(ks[7], (DEPTH, SSD_HEADS), f32),
        "attn_norm_g": 1.0 + 0.02 * jax.random.normal(ks[8], (DEPTH, ATTN_WIDTH), f32),
        "ssd_norm_g": 1.0 + 0.02 * jax.random.normal(ks[9], (DEPTH, SSD_WIDTH), f32),
        "w_out": jax.random.normal(ks[10], (DEPTH, MIX_WIDTH, D_MODEL), f32) * MIX_WIDTH ** -0.5,
        "ln2_g": 1.0 + 0.02 * jax.random.normal(ks[11], (DEPTH, D_MODEL), f32),
        "w_mlp_in": jax.random.normal(ks[12], (DEPTH, D_MODEL, D_FF), f32) * D_MODEL ** -0.5,
        "w_mlp_out": jax.random.normal(ks[13], (DEPTH, D_FF, D_MODEL), f32) * D_FF ** -0.5,
        "final_norm_g": 1.0 + 0.02 * jax.random.normal(ks[14], (D_MODEL,), f32),
    }


def _fwd_reference(x, ln1_g, w_in, conv_w, conv_b, dt_bias, a_log, d_skip,
              attn_norm_g, ssd_norm_g, w_out, ln2_g, w_mlp_in, w_mlp_out, final_norm_g):
    slopes = alibi_slopes(ATTN_HEADS)
    for l in range(DEPTH):
        x = hybrid_layer(x, ln1_g[l], w_in[l], conv_w[l], conv_b[l], dt_bias[l], a_log[l],
                         d_skip[l], attn_norm_g[l], ssd_norm_g[l], w_out[l], ln2_g[l],
                         w_mlp_in[l], w_mlp_out[l], slopes)
    return rmsnorm(x, final_norm_g)


import jax as _jax
import jax.numpy as _jnp

TWIN_FORMAT = 'train_step'
FWD_PARAMS = ['x', 'ln1_g', 'w_in', 'conv_w', 'conv_b', 'dt_bias', 'a_log', 'd_skip', 'attn_norm_g', 'ssd_norm_g', 'w_out', 'ln2_g', 'w_mlp_in', 'w_mlp_out', 'final_norm_g']
TWIN_WEIGHTS = ['ln1_g', 'w_in', 'conv_w', 'conv_b', 'dt_bias', 'a_log', 'd_skip', 'attn_norm_g', 'ssd_norm_g', 'w_out', 'ln2_g', 'w_mlp_in', 'w_mlp_out', 'final_norm_g']
TWIN_DIFF_INPUT = 'x'
TWIN_INPUTS = ['x', 'ln1_g', 'w_in', 'conv_w', 'conv_b', 'dt_bias', 'a_log', 'd_skip', 'attn_norm_g', 'ssd_norm_g', 'w_out', 'ln2_g', 'w_mlp_in', 'w_mlp_out', 'final_norm_g', 'loss_target', 'm_ln1_g', 'm_w_in', 'm_conv_w', 'm_conv_b', 'm_dt_bias', 'm_a_log', 'm_d_skip', 'm_attn_norm_g', 'm_ssd_norm_g', 'm_w_out', 'm_ln2_g', 'm_w_mlp_in', 'm_w_mlp_out', 'm_final_norm_g', 'v_ln1_g', 'v_w_in', 'v_conv_w', 'v_conv_b', 'v_dt_bias', 'v_a_log', 'v_d_skip', 'v_attn_norm_g', 'v_ssd_norm_g', 'v_w_out', 'v_ln2_g', 'v_w_mlp_in', 'v_w_mlp_out', 'v_final_norm_g']
TWIN_OUTPUTS = ['loss', 'grad_x', 'grad_ln1_g', 'grad_w_in', 'grad_conv_w', 'grad_conv_b', 'grad_dt_bias', 'grad_a_log', 'grad_d_skip', 'grad_attn_norm_g', 'grad_ssd_norm_g', 'grad_w_out', 'grad_ln2_g', 'grad_w_mlp_in', 'grad_w_mlp_out', 'grad_final_norm_g', 'delta_ln1_g', 'delta_w_in', 'delta_conv_w', 'delta_conv_b', 'delta_dt_bias', 'delta_a_log', 'delta_d_skip', 'delta_attn_norm_g', 'delta_ssd_norm_g', 'delta_w_out', 'delta_ln2_g', 'delta_w_mlp_in', 'delta_w_mlp_out', 'delta_final_norm_g', 'new_m_ln1_g', 'new_m_w_in', 'new_m_conv_w', 'new_m_conv_b', 'new_m_dt_bias', 'new_m_a_log', 'new_m_d_skip', 'new_m_attn_norm_g', 'new_m_ssd_norm_g', 'new_m_w_out', 'new_m_ln2_g', 'new_m_w_mlp_in', 'new_m_w_mlp_out', 'new_m_final_norm_g', 'new_v_ln1_g', 'new_v_w_in', 'new_v_conv_w', 'new_v_conv_b', 'new_v_dt_bias', 'new_v_a_log', 'new_v_d_skip', 'new_v_attn_norm_g', 'new_v_ssd_norm_g', 'new_v_w_out', 'new_v_ln2_g', 'new_v_w_mlp_in', 'new_v_w_mlp_out', 'new_v_final_norm_g']
TWIN_LEAF_KINDS = {'loss': 'loss', 'grad_x': 'grad_x', 'grad_ln1_g': 'grad_w', 'grad_w_in': 'grad_w', 'grad_conv_w': 'grad_w', 'grad_conv_b': 'grad_w', 'grad_dt_bias': 'grad_w', 'grad_a_log': 'grad_w', 'grad_d_skip': 'grad_w', 'grad_attn_norm_g': 'grad_w', 'grad_ssd_norm_g': 'grad_w', 'grad_w_out': 'grad_w', 'grad_ln2_g': 'grad_w', 'grad_w_mlp_in': 'grad_w', 'grad_w_mlp_out': 'grad_w', 'grad_final_norm_g': 'grad_w', 'delta_ln1_g': 'delta_w', 'delta_w_in': 'delta_w', 'delta_conv_w': 'delta_w', 'delta_conv_b': 'delta_w', 'delta_dt_bias': 'delta_w', 'delta_a_log': 'delta_w', 'delta_d_skip': 'delta_w', 'delta_attn_norm_g': 'delta_w', 'delta_ssd_norm_g': 'delta_w', 'delta_w_out': 'delta_w', 'delta_ln2_g': 'delta_w', 'delta_w_mlp_in': 'delta_w', 'delta_w_mlp_out': 'delta_w', 'delta_final_norm_g': 'delta_w', 'new_m_ln1_g': 'new_m', 'new_m_w_in': 'new_m', 'new_m_conv_w': 'new_m', 'new_m_conv_b': 'new_m', 'new_m_dt_bias': 'new_m', 'new_m_a_log': 'new_m', 'new_m_d_skip': 'new_m', 'new_m_attn_norm_g': 'new_m', 'new_m_ssd_norm_g': 'new_m', 'new_m_w_out': 'new_m', 'new_m_ln2_g': 'new_m', 'new_m_w_mlp_in': 'new_m', 'new_m_w_mlp_out': 'new_m', 'new_m_final_norm_g': 'new_m', 'new_v_ln1_g': 'new_v', 'new_v_w_in': 'new_v', 'new_v_conv_w': 'new_v', 'new_v_conv_b': 'new_v', 'new_v_dt_bias': 'new_v', 'new_v_a_log': 'new_v', 'new_v_d_skip': 'new_v', 'new_v_attn_norm_g': 'new_v', 'new_v_ssd_norm_g': 'new_v', 'new_v_w_out': 'new_v', 'new_v_ln2_g': 'new_v', 'new_v_w_mlp_in': 'new_v', 'new_v_w_mlp_out': 'new_v', 'new_v_final_norm_g': 'new_v'}


def _forward(args):
    return _fwd_reference(*[args[k] for k in FWD_PARAMS])


def _output_shape():
    out = _jax.eval_shape(lambda: _forward(_fwd_setup_inputs(0)))
    return out.shape, out.dtype

N_MICROBATCH = 1
ADAM_LR = 0.001
ADAM_B1 = 0.9
ADAM_B2 = 0.999
ADAM_EPS = 1e-08
ADAM_WD = 0.01
ADAM_STEP = 10
PER_EXAMPLE_BATCH_AXIS = {'x': 0, 'loss_target': 0}
SHARED_INPUTS = []
_WEIGHT_DTYPES = {'ln1_g': _jnp.float32, 'w_in': _jnp.float32, 'conv_w': _jnp.float32, 'conv_b': _jnp.float32, 'dt_bias': _jnp.float32, 'a_log': _jnp.float32, 'd_skip': _jnp.float32, 'attn_norm_g': _jnp.float32, 'ssd_norm_g': _jnp.float32, 'w_out': _jnp.float32, 'ln2_g': _jnp.float32, 'w_mlp_in': _jnp.float32, 'w_mlp_out': _jnp.float32, 'final_norm_g': _jnp.float32}
MOMENT_SCALE = {'ln1_g': 8.994497e-02, 'w_in': 5.273862e-02, 'conv_w': 5.444267e-02, 'conv_b': 7.542824e-02, 'dt_bias': 1.584013e-01, 'a_log': 2.611256e-01, 'd_skip': 4.982295e-01, 'attn_norm_g': 6.366182e-02, 'ssd_norm_g': 6.985500e-02, 'w_out': 6.370920e-02, 'ln2_g': 6.328711e-02, 'w_mlp_in': 3.216092e-02, 'w_mlp_out': 6.639317e-02, 'final_norm_g': 1.627118e+01}


def _to_microbatches(a, axis):
    t = _jnp.moveaxis(a, axis, 0)
    t = t.reshape((N_MICROBATCH, t.shape[0] // N_MICROBATCH) + t.shape[1:])
    return _jnp.moveaxis(t, 1, axis + 1)


def setup_inputs(seed: int = 0) -> dict:
    inp = _fwd_setup_inputs(seed)
    key = _jax.random.fold_in(_jax.random.key(seed), 7919)
    shape, _ = _output_shape()
    out = dict(inp)
    out["loss_target"] = _jax.random.normal(_jax.random.fold_in(key, 0), shape, _jnp.float32)
    for i, name in enumerate(TWIN_WEIGHTS):
        w = inp[name].astype(_jnp.float32)
        if MOMENT_SCALE is None:
            s = _jnp.sqrt(_jnp.mean(_jnp.square(w)) + 1e-30)
        else:
            s = MOMENT_SCALE[name]
        km, kv = _jax.random.split(_jax.random.fold_in(key, i + 1))
        out[name] = w
        out["m_" + name] = s * _jax.random.normal(km, w.shape, _jnp.float32)
        out["v_" + name] = (s * s) * _jax.random.uniform(kv, w.shape, _jnp.float32, 0.5, 1.5)
    if N_MICROBATCH > 1:
        for name, axis in PER_EXAMPLE_BATCH_AXIS.items():
            out[name] = _to_microbatches(out[name], axis)
    return {'x': out['x'], 'ln1_g': out['ln1_g'], 'w_in': out['w_in'], 'conv_w': out['conv_w'], 'conv_b': out['conv_b'], 'dt_bias': out['dt_bias'], 'a_log': out['a_log'], 'd_skip': out['d_skip'], 'attn_norm_g': out['attn_norm_g'], 'ssd_norm_g': out['ssd_norm_g'], 'w_out': out['w_out'], 'ln2_g': out['ln2_g'], 'w_mlp_in': out['w_mlp_in'], 'w_mlp_out': out['w_mlp_out'], 'final_norm_g': out['final_norm_g'], 'loss_target': out['loss_target'], 'm_ln1_g': out['m_ln1_g'], 'm_w_in': out['m_w_in'], 'm_conv_w': out['m_conv_w'], 'm_conv_b': out['m_conv_b'], 'm_dt_bias': out['m_dt_bias'], 'm_a_log': out['m_a_log'], 'm_d_skip': out['m_d_skip'], 'm_attn_norm_g': out['m_attn_norm_g'], 'm_ssd_norm_g': out['m_ssd_norm_g'], 'm_w_out': out['m_w_out'], 'm_ln2_g': out['m_ln2_g'], 'm_w_mlp_in': out['m_w_mlp_in'], 'm_w_mlp_out': out['m_w_mlp_out'], 'm_final_norm_g': out['m_final_norm_g'], 'v_ln1_g': out['v_ln1_g'], 'v_w_in': out['v_w_in'], 'v_conv_w': out['v_conv_w'], 'v_conv_b': out['v_conv_b'], 'v_dt_bias': out['v_dt_bias'], 'v_a_log': out['v_a_log'], 'v_d_skip': out['v_d_skip'], 'v_attn_norm_g': out['v_attn_norm_g'], 'v_ssd_norm_g': out['v_ssd_norm_g'], 'v_w_out': out['v_w_out'], 'v_ln2_g': out['v_ln2_g'], 'v_w_mlp_in': out['v_w_mlp_in'], 'v_w_mlp_out': out['v_w_mlp_out'], 'v_final_norm_g': out['v_final_norm_g']}


def _loss(weights, diff, rest, loss_target):
    with _jax.named_scope("forward"):
        args = {**rest, TWIN_DIFF_INPUT: diff, **{k: w.astype(_WEIGHT_DTYPES[k]) for k, w in weights.items()}}
        y = _forward(args)
    with _jax.named_scope("loss_head"):
        err = _jnp.square(y.astype(_jnp.float32) - loss_target)
        return 0.5 * _jnp.sum(_jnp.mean(err, axis=-1)) if err.ndim else 0.5 * err


def _adamw(w, g, m, v):
    m = ADAM_B1 * m + (1.0 - ADAM_B1) * g
    v = ADAM_B2 * v + (1.0 - ADAM_B2) * _jnp.square(g)
    m_hat = m / (1.0 - ADAM_B1 ** ADAM_STEP)
    v_hat = v / (1.0 - ADAM_B2 ** ADAM_STEP)
    delta = -ADAM_LR * (m_hat / (_jnp.sqrt(v_hat) + ADAM_EPS) + ADAM_WD * w)
    return delta, m, v


def reference(x, ln1_g, w_in, conv_w, conv_b, dt_bias, a_log, d_skip, attn_norm_g, ssd_norm_g, w_out, ln2_g, w_mlp_in, w_mlp_out, final_norm_g, loss_target, m_ln1_g, m_w_in, m_conv_w, m_conv_b, m_dt_bias, m_a_log, m_d_skip, m_attn_norm_g, m_ssd_norm_g, m_w_out, m_ln2_g, m_w_mlp_in, m_w_mlp_out, m_final_norm_g, v_ln1_g, v_w_in, v_conv_w, v_conv_b, v_dt_bias, v_a_log, v_d_skip, v_attn_norm_g, v_ssd_norm_g, v_w_out, v_ln2_g, v_w_mlp_in, v_w_mlp_out, v_final_norm_g):
    given = dict(x=x, ln1_g=ln1_g, w_in=w_in, conv_w=conv_w, conv_b=conv_b, dt_bias=dt_bias, a_log=a_log, d_skip=d_skip, attn_norm_g=attn_norm_g, ssd_norm_g=ssd_norm_g, w_out=w_out, ln2_g=ln2_g, w_mlp_in=w_mlp_in, w_mlp_out=w_mlp_out, final_norm_g=final_norm_g, loss_target=loss_target, m_ln1_g=m_ln1_g, m_w_in=m_w_in, m_conv_w=m_conv_w, m_conv_b=m_conv_b, m_dt_bias=m_dt_bias, m_a_log=m_a_log, m_d_skip=m_d_skip, m_attn_norm_g=m_attn_norm_g, m_ssd_norm_g=m_ssd_norm_g, m_w_out=m_w_out, m_ln2_g=m_ln2_g, m_w_mlp_in=m_w_mlp_in, m_w_mlp_out=m_w_mlp_out, m_final_norm_g=m_final_norm_g, v_ln1_g=v_ln1_g, v_w_in=v_w_in, v_conv_w=v_conv_w, v_conv_b=v_conv_b, v_dt_bias=v_dt_bias, v_a_log=v_a_log, v_d_skip=v_d_skip, v_attn_norm_g=v_attn_norm_g, v_ssd_norm_g=v_ssd_norm_g, v_w_out=v_w_out, v_ln2_g=v_ln2_g, v_w_mlp_in=v_w_mlp_in, v_w_mlp_out=v_w_mlp_out, v_final_norm_g=v_final_norm_g)
    weights = {n: given[n] for n in TWIN_WEIGHTS}
    shared = {n: given[n] for n in SHARED_INPUTS}
    per_example = {n: given[n] for n in ['x']}
    grad_fn = _jax.value_and_grad(_loss, argnums=(0, 1))

    def one_microbatch(ex, loss_target):
        ex = dict(ex)
        diff = ex.pop(TWIN_DIFF_INPUT)
        return grad_fn(weights, diff, {**shared, **ex}, loss_target)

    if N_MICROBATCH == 1:
        loss, (grad_w, grad_x) = one_microbatch(per_example, given["loss_target"])
    else:
        def body(carry, xs):
            loss_sum, grad_sum = carry
            l_k, (gw_k, gx_k) = one_microbatch(xs[0], xs[1])
            with _jax.named_scope("update"):
                return (loss_sum + l_k, _jax.tree.map(_jnp.add, grad_sum, gw_k)), gx_k

        init = (_jnp.zeros((), _jnp.float32), _jax.tree.map(_jnp.zeros_like, weights))
        (loss, grad_w), grad_x = _jax.lax.scan(body, init, (per_example, given["loss_target"]))
    with _jax.named_scope("update"):
        delta_w, new_m, new_v = {}, {}, {}
        for n in TWIN_WEIGHTS:
            delta_w[n], new_m[n], new_v[n] = _adamw(weights[n], grad_w[n], given["m_" + n], given["v_" + n])
    return (loss, grad_x, *[grad_w[n] for n in TWIN_WEIGHTS], *[delta_w[n] for n in TWIN_WEIGHTS],
            *[new_m[n] for n in TWIN_WEIGHTS], *[new_v[n] for n in TWIN_WEIGHTS])
```

```python
import functools

import numpy as np
import jax
import jax.numpy as jnp
from jax import lax
from jax.experimental import pallas as pl
from jax.experimental.pallas import tpu as pltpu

F32 = jnp.float32
_BF = jnp.bfloat16
_NEG = -1e30
_EPS = 1e-5
_HEADS = 16
_HDIM = 64
_ABLK = 128
_DILATIONS = (1, 4, 16)
_CHUNK = 128
_NSTATE = 128
_GROUPS = 2
_HPG = _HEADS // _GROUPS
_CONV_K = 4
_LR, _B1, _B2, _AEPS, _WD, _STEP = 0.001, 0.9, 0.999, 1e-08, 0.01, 10
_VMEM_CAP = 56 * 1024 * 1024
_MESH = pl.DeviceIdType.MESH
_SDS = jax.ShapeDtypeStruct


def _params(sem, est_bytes):
    lim = int(min(max(2 * est_bytes + (4 << 20), 16 << 20), _VMEM_CAP))
    return pltpu.CompilerParams(dimension_semantics=sem, vmem_limit_bytes=lim)


def _nbytes(shape, dtype):
    return int(np.prod(shape)) * jnp.dtype(dtype).itemsize


def _rows(fn, ins, consts, outs, sums=(), *, tile, name):
    rows = ins[0].shape[0]
    n_in, n_c, n_o, n_s = len(ins), len(consts), len(outs), len(sums)

    def body(*refs):
        vals = [r[...] for r in refs[:n_in + n_c]]
        res = fn(*vals)
        res = res if isinstance(res, tuple) else (res,)
        orefs = refs[n_in + n_c:n_in + n_c + n_o]
        srefs = refs[n_in + n_c + n_o:]
        for r, v in zip(orefs, res[:n_o]):
            r[...] = v.astype(r.dtype)
        if n_s:
            @pl.when(pl.program_id(0) == 0)
            def _():
                for r in srefs:
                    r[...] = jnp.zeros_like(r)
            for r, v in zip(srefs, res[n_o:]):
                r[...] += v.reshape(tile // 8, 8, v.shape[-1]).sum(axis=0)

    in_specs = [pl.BlockSpec((tile, a.shape[1]), lambda i: (i, 0)) for a in ins]
    in_specs += [pl.BlockSpec(c.shape, lambda i, nd=c.ndim: (0,) * nd) for c in consts]
    out_shape = [_SDS((rows, w), dt) for w, dt in outs] + [_SDS((8, w), F32) for w in sums]
    out_specs = [pl.BlockSpec((tile, w), lambda i: (i, 0)) for w, _ in outs]
    out_specs += [pl.BlockSpec((8, w), lambda i: (0, 0)) for w in sums]
    est = sum(_nbytes((tile, a.shape[1]), a.dtype) for a in ins) + sum(_nbytes((tile, w), dt) for w, dt in outs)
    res = pl.pallas_call(body, grid=(rows // tile,), in_specs=in_specs, out_specs=out_specs, out_shape=out_shape,
                         name=name, compiler_params=_params(("arbitrary",), 3 * est))(*ins, *consts)
    return res


def _tile_for(width):
    return max(c for c in (256, 128, 64, 32) if c * width <= (1 << 18) or c == 32)


def _rstd(x):
    return lax.rsqrt(jnp.mean(x * x, axis=-1, keepdims=True) + _EPS)


def _split(x, groups):
    w = x.shape[-1] // groups
    return [x[:, g * w:(g + 1) * w] for g in range(groups)]


def _rms_fwd(x, g, *, groups=1, out_dtype=_BF, name):
    def fn(x, g):
        ys = [xs * _rstd(xs) * gs for xs, gs in zip(_split(x, groups), _split(g, groups))]
        return ys[0] if groups == 1 else jnp.concatenate(ys, axis=-1)
    w = x.shape[1]
    return _rows(fn, [x], [g.reshape(1, w)], [(w, out_dtype)], tile=_tile_for(w), name=name)[0]


def _rms_bwd(x, dy, g, res=None, *, groups=1, name):
    def fn(x, dy, *rest):
        g = rest[-1]
        dxs, dgs = [], []
        for xs, ds, gs in zip(_split(x, groups), _split(dy.astype(F32), groups), _split(g, groups)):
            r = _rstd(xs)
            xh = xs * r
            gd = ds * gs
            dxs.append(r * (gd - xh * jnp.mean(gd * xh, axis=-1, keepdims=True)))
            dgs.append(ds * xh)
        dx = dxs[0] if groups == 1 else jnp.concatenate(dxs, axis=-1)
        dg = dgs[0] if groups == 1 else jnp.concatenate(dgs, axis=-1)
        if res is not None:
            dx = dx + rest[0]
        return dx, dg
    w = x.shape[1]
    ins = [x, dy] + ([res] if res is not None else [])
    dx, dg = _rows(fn, ins, [g.reshape(1, w)], [(w, F32)], [w], tile=_tile_for(w), name=name)
    return dx, dg.sum(axis=0)


def _pick(n, cands):
    for c in cands:
        if n % c == 0:
            return c
    raise ValueError(f"no block size for {n}")


def _mm(a, b, *, ta=False, tb=False, extra=(), epi=None, outs=(F32,), name):
    m, k = (a.shape[1], a.shape[0]) if ta else a.shape
    n = b.shape[0] if tb else b.shape[1]
    assert k == (b.shape[1] if tb else b.shape[0])
    bm = _pick(m, (1024, 640, 512))
    bn = _pick(n, (1024, 640, 512))
    bk = _pick(k, (1024, 640, 512))
    nk = k // bk
    n_e, n_o = len(extra), len(outs)
    dims = (((0 if ta else 1,), (1 if tb else 0,)), ((), ()))

    def body(a_ref, b_ref, *rest):
        ex, orefs, acc = rest[:n_e], rest[n_e:n_e + n_o], rest[-1]
        kk = pl.program_id(2)

        @pl.when(kk == 0)
        def _():
            acc[...] = jnp.zeros_like(acc)

        acc[...] += lax.dot_general(a_ref[...].astype(_BF), b_ref[...].astype(_BF), dims, preferred_element_type=F32)

        @pl.when(kk == nk - 1)
        def _():
            r = acc[...]
            res = epi(r, *[e[...] for e in ex]) if epi is not None else (r,)
            for o, v in zip(orefs, res):
                o[...] = v.astype(o.dtype)

    a_spec = pl.BlockSpec((bk, bm), lambda i, j, kk: (kk, i)) if ta else pl.BlockSpec((bm, bk), lambda i, j, kk: (i, kk))
    b_spec = pl.BlockSpec((bn, bk), lambda i, j, kk: (j, kk)) if tb else pl.BlockSpec((bk, bn), lambda i, j, kk: (kk, j))
    t_spec = pl.BlockSpec((bm, bn), lambda i, j, kk: (i, j))
    est = (_nbytes((bm, bk), a.dtype) + _nbytes((bk, bn), b.dtype) + sum(_nbytes((bm, bn), e.dtype) for e in extra)
           + sum(_nbytes((bm, bn), o) for o in outs)) * 2 + 2 * _nbytes((bm, bn), F32)
    res = pl.pallas_call(
        body, grid=(m // bm, n // bn, nk), in_specs=[a_spec, b_spec] + [t_spec] * n_e, out_specs=[t_spec] * n_o,
        out_shape=[_SDS((m, n), o) for o in outs], scratch_shapes=[pltpu.VMEM((bm, bn), F32)], name=name,
        compiler_params=_params(("parallel", "parallel", "arbitrary"), est))(a, b, *extra)
    return res[0] if n_o == 1 else res


def _to_hm(a, d):
    t = a.shape[0]
    return a.reshape(t // d, d, _HEADS, _HDIM).transpose(1, 2, 0, 3)


def _from_hm(a):
    d, h, l, dh = a.shape
    return a.transpose(2, 0, 1, 3).reshape(l * d, h * dh)


def _alibi_bias(dilation):
    slopes = 2.0 ** (-8.0 * (np.arange(_HEADS) + 1) / _HEADS)
    i = np.arange(_ABLK)[:, None]
    j = np.arange(_ABLK)[None, :]
    cur = np.where(i - j >= 0, -slopes[:, None, None] * ((i - j) * dilation), _NEG)
    prev = np.where(j >= i, -slopes[:, None, None] * ((i - j + _ABLK) * dilation), _NEG)
    return jnp.asarray(cur, F32), jnp.asarray(prev, F32)


def _scores(q, kc, kp, bc, bp, first):
    scale = _HDIM ** -0.5
    sc = jnp.einsum('hqd,hkd->hqk', q, kc, preferred_element_type=F32) * scale + bc
    sp = jnp.einsum('hqd,hkd->hqk', q, kp, preferred_element_type=F32) * scale + bp
    return sc, jnp.where(first, _NEG, sp)


def _attn_fwd(q, k, v, dilation, *, name):
    s, h, l, dh = q.shape
    nb = l // _ABLK
    bc, bp = _alibi_bias(dilation)

    def body(q_ref, kc_ref, kp_ref, vc_ref, vp_ref, bc_ref, bp_ref, o_ref, l_ref):
        sc, sp = _scores(q_ref[0], kc_ref[0], kp_ref[0], bc_ref[...], bp_ref[...], pl.program_id(1) == 0)
        m = jnp.maximum(jnp.max(sc, axis=-1, keepdims=True), jnp.max(sp, axis=-1, keepdims=True))
        pc, pp = jnp.exp(sc - m), jnp.exp(sp - m)
        den = jnp.sum(pc, axis=-1, keepdims=True) + jnp.sum(pp, axis=-1, keepdims=True)
        o = (jnp.einsum('hqk,hkd->hqd', pc.astype(_BF), vc_ref[0], preferred_element_type=F32)
             + jnp.einsum('hqk,hkd->hqd', pp.astype(_BF), vp_ref[0], preferred_element_type=F32))
        o_ref[0] = o / den
        l_ref[0] = jnp.broadcast_to(m + jnp.log(den), (h, _ABLK, dh))

    blk = (1, h, _ABLK, dh)
    cur = pl.BlockSpec(blk, lambda r, j: (r, 0, j, 0))
    prev = pl.BlockSpec(blk, lambda r, j: (r, 0, jnp.maximum(j - 1, 0), 0))
    bias = pl.BlockSpec((h, _ABLK, _ABLK), lambda r, j: (0, 0, 0))
    return pl.pallas_call(
        body, grid=(s, nb), in_specs=[cur, cur, prev, cur, prev, bias, bias], out_specs=[cur, cur],
        out_shape=[_SDS(q.shape, F32), _SDS(q.shape, F32)], name=name,
        compiler_params=_params(("parallel", "arbitrary"), 24 << 20))(q, k, k, v, v, bc, bp)


def _attn_bwd(q, k, v, do, o, lse, dilation, *, name):
    s, h, l, dh = q.shape
    nb = l // _ABLK
    bc, bp = _alibi_bias(dilation)
    scale = _HDIM ** -0.5

    def body(q_ref, kc_ref, kp_ref, vc_ref, vp_ref, do_ref, o_ref, l_ref, bc_ref, bp_ref, dq_ref, dk_ref, dv_ref, ck, cv):
        n = pl.program_id(1)

        @pl.when(n == 0)
        def _():
            ck[...] = jnp.zeros_like(ck)
            cv[...] = jnp.zeros_like(cv)

        @pl.when(n < nb)
        def _():
            qv, kc, kp, vc, vp = q_ref[0], kc_ref[0], kp_ref[0], vc_ref[0], vp_ref[0]
            dov = do_ref[0]
            dob = dov.astype(_BF)
            dsum = jnp.sum(dov * o_ref[0], axis=-1, keepdims=True)
            lrow = jnp.max(l_ref[0], axis=-1, keepdims=True)
            sc, sp = _scores(qv, kc, kp, bc_ref[...], bp_ref[...], n == 0)
            pc, pp = jnp.exp(sc - lrow), jnp.exp(sp - lrow)
            dsc = pc * (jnp.einsum('hqd,hkd->hqk', dob, vc, preferred_element_type=F32) - dsum)
            dsp = pp * (jnp.einsum('hqd,hkd->hqk', dob, vp, preferred_element_type=F32) - dsum)
            dscb, dspb = dsc.astype(_BF), dsp.astype(_BF)
            dq_ref[0] = (jnp.einsum('hqk,hkd->hqd', dscb, kc, preferred_element_type=F32)
                         + jnp.einsum('hqk,hkd->hqd', dspb, kp, preferred_element_type=F32)) * scale
            dk_ref[0] = ck[...] + jnp.einsum('hqk,hqd->hkd', dspb, qv, preferred_element_type=F32) * scale
            dv_ref[0] = cv[...] + jnp.einsum('hqk,hqd->hkd', pp.astype(_BF), dob, preferred_element_type=F32)
            ck[...] = jnp.einsum('hqk,hqd->hkd', dscb, qv, preferred_element_type=F32) * scale
            cv[...] = jnp.einsum('hqk,hqd->hkd', pc.astype(_BF), dob, preferred_element_type=F32)

        @pl.when(n == nb)
        def _():
            dk_ref[0] = ck[...]
            dv_ref[0] = cv[...]

    blk = (1, h, _ABLK, dh)
    cur = pl.BlockSpec(blk, lambda r, j: (r, 0, jnp.minimum(j, nb - 1), 0))
    prev = pl.BlockSpec(blk, lambda r, j: (r, 0, jnp.clip(j - 1, 0, nb - 1), 0))
    bias = pl.BlockSpec((h, _ABLK, _ABLK), lambda r, j: (0, 0, 0))
    return pl.pallas_call(
        body, grid=(s, nb + 1), in_specs=[cur, cur, prev, cur, prev, cur, cur, cur, bias, bias], out_specs=[cur, prev, prev],
        out_shape=[_SDS(q.shape, F32)] * 3, scratch_shapes=[pltpu.VMEM((h, _ABLK, dh), F32)] * 2, name=name,
        compiler_params=_params(("parallel", "arbitrary"), 24 << 20))(q, k, k, v, v, do, o, lse, bc, bp)


def _decay(a128, at, transposed):
    shape = (a128.shape[0], _CHUNK, _CHUNK)
    i1 = lax.broadcasted_iota(jnp.int32, shape, 1)
    i2 = lax.broadcasted_iota(jnp.int32, shape, 2)
    if transposed:
        return jnp.where(i2 >= i1, jnp.exp(at - a128), 0.0)
    return jnp.where(i1 >= i2, jnp.exp(a128 - at), 0.0)


def _ssd_specs(nc, reverse):
    ch = (lambda c: nc - 1 - c) if reverse else (lambda c: c)
    hq = _HPG, _CHUNK
    return dict(
        h64=pl.BlockSpec((*hq, _HDIM), lambda g, c: (g, ch(c), 0)),
        h128=pl.BlockSpec((*hq, 128), lambda g, c: (g, ch(c), 0)),
        hrow=pl.BlockSpec((_HPG, 1, _CHUNK), lambda g, c: (g, 0, ch(c))),
        gq=pl.BlockSpec((1, _CHUNK, _NSTATE), lambda g, c: (g, ch(c), 0)),
        gt=pl.BlockSpec((1, _NSTATE, _CHUNK), lambda g, c: (g, 0, ch(c))),
        st=pl.BlockSpec((1, _HPG, _NSTATE, _HDIM), lambda g, c: (ch(c), g, 0, 0)),
    )


def _ssd_fwd(xs, dt64, ac64, ac128, act, bg, btg, cg, *, name):
    t = xs.shape[1]
    nc = t // _CHUNK
    sp = _ssd_specs(nc, False)

    def body(xs_ref, dt_ref, a64_ref, a128_ref, at_ref, b_ref, bt_ref, c_ref, y_ref, sall_ref, st):
        @pl.when(pl.program_id(1) == 0)
        def _():
            st[...] = jnp.zeros_like(st)

        xb = (xs_ref[...] * dt_ref[...]).astype(_BF)
        btv, cb_ = bt_ref[0], c_ref[0].astype(_BF)
        a64, a128, at = a64_ref[...], a128_ref[...], at_ref[...]
        cbm = jnp.dot(cb_, btv.astype(_BF), preferred_element_type=F32)
        gmat = cbm[None] * _decay(a128, at, False)
        sv = st[...]
        sall_ref[0] = sv
        yd = jnp.einsum('hls,hsp->hlp', gmat.astype(_BF), xb, preferred_element_type=F32)
        cbb = jnp.broadcast_to(cb_[None], (_HPG, _CHUNK, _NSTATE))
        yo = jnp.exp(a64) * jnp.einsum('hln,hnp->hlp', cbb, sv.astype(_BF), preferred_element_type=F32)
        y_ref[...] = yd + yo
        w = jnp.exp(jnp.min(a128, axis=1, keepdims=True) - at)
        bwt = (btv[None] * w).astype(_BF)
        st[...] = (jnp.exp(jnp.min(a64, axis=1, keepdims=True)) * sv
                   + jnp.einsum('hns,hsp->hnp', bwt, xb, preferred_element_type=F32))

    return pl.pallas_call(
        body, grid=(_GROUPS, nc),
        in_specs=[sp['h64'], sp['h64'], sp['h64'], sp['h128'], sp['hrow'], sp['gq'], sp['gt'], sp['gq']],
        out_specs=[sp['h64'], sp['st']],
        out_shape=[_SDS(xs.shape, F32), _SDS((nc, _HEADS, _NSTATE, _HDIM), F32)],
        scratch_shapes=[pltpu.VMEM((_HPG, _NSTATE, _HDIM), F32)], name=name,
        compiler_params=_params(("parallel", "arbitrary"), 16 << 20))(xs, dt64, ac64, ac128, act, bg, btg, cg)


def _ssd_bwd(xs, dt64, ac64, ac128, act, bg, btg, cg, ctg, sall, dy, *, name):
    t = xs.shape[1]
    nc = t // _CHUNK
    sp = _ssd_specs(nc, True)

    def rsum(v):
        return jnp.sum(v, axis=-1, keepdims=True)

    def body(xs_ref, dt_ref, a64_ref, a128_ref, at_ref, b_ref, bt_ref, c_ref, ct_ref, sall_ref, dy_ref,
             dxs_ref, ddt_ref, da_ref, db_ref, dc_ref, dst):
        @pl.when(pl.program_id(1) == 0)
        def _():
            dst[...] = jnp.zeros_like(dst)

        xsv, dtv = xs_ref[...], dt_ref[...]
        x = xsv * dtv
        xb = x.astype(_BF)
        dyv = dy_ref[...]
        dyb = dyv.astype(_BF)
        bv, btv, cv, ctv = b_ref[0], bt_ref[0].astype(_BF), c_ref[0], ct_ref[0].astype(_BF)
        bvb, cvb = bv.astype(_BF), cv.astype(_BF)
        a64, a128, at = a64_ref[...], a128_ref[...], at_ref[...]
        lm, lmt = _decay(a128, at, False), _decay(a128, at, True)
        gmat = jnp.dot(cvb, btv, preferred_element_type=F32)[None] * lm
        gmt = jnp.dot(bvb, ctv, preferred_element_type=F32)[None] * lmt
        sv, dsv = sall_ref[0], dst[...]
        svb, dsb = sv.astype(_BF), dsv.astype(_BF)
        e64 = jnp.exp(a64)
        alast = jnp.min(a64, axis=1, keepdims=True)
        wcol = jnp.exp(jnp.min(a128, axis=1, keepdims=True) - a128)
        bw = (bv[None] * wcol).astype(_BF)
        bwds = jnp.einsum('hsn,hnp->hsp', bw, dsb, preferred_element_type=F32)
        dx = jnp.einsum('hsl,hlp->hsp', gmt.astype(_BF), dyb, preferred_element_type=F32) + bwds
        dg = jnp.einsum('hlp,hsp->hls', dyb, xb, preferred_element_type=F32)
        dgt = jnp.einsum('hsp,hlp->hsl', xb, dyb, preferred_element_type=F32)
        edy = e64 * dyv
        edyb = edy.astype(_BF)
        dc_ref[0] = (jnp.dot(jnp.sum(dg * lm, axis=0).astype(_BF), bvb, preferred_element_type=F32)
                     + jnp.sum(jnp.einsum('hlp,hnp->hln', edyb, svb, preferred_element_type=F32), axis=0))
        xds = jnp.einsum('hsp,hnp->hsn', xb, dsb, preferred_element_type=F32)
        db_ref[0] = (jnp.dot(jnp.sum(dgt * lmt, axis=0).astype(_BF), cvb, preferred_element_type=F32)
                     + jnp.sum(wcol * xds, axis=0))
        ctb = jnp.broadcast_to(ctv[None], (_HPG, _NSTATE, _CHUNK))
        cbb = jnp.broadcast_to(cvb[None], (_HPG, _CHUNK, _NSTATE))
        yo = e64 * jnp.einsum('hln,hnp->hlp', cbb, svb, preferred_element_type=F32)
        z = rsum(x * bwds)
        da = rsum(dg * gmat) - rsum(dgt * gmt) + rsum(dyv * yo) - z
        ealast = jnp.exp(jnp.max(alast, axis=-1, keepdims=True))
        da_last = jnp.sum(z, axis=1, keepdims=True) + ealast * jnp.sum(rsum(sv * dsv), axis=1, keepdims=True)
        row = lax.broadcasted_iota(jnp.int32, (_HPG, _CHUNK, 1), 1)
        da = da + jnp.where(row == _CHUNK - 1, da_last, 0.0)
        da_ref[...] = jnp.broadcast_to(da, (_HPG, _CHUNK, _HDIM))
        ddt_ref[...] = jnp.broadcast_to(rsum(dx * xsv), (_HPG, _CHUNK, _HDIM))
        dxs_ref[...] = dx * dtv
        dst[...] = jnp.exp(alast) * dsv + jnp.einsum('hnl,hlp->hnp', ctb, edyb, preferred_element_type=F32)

    return pl.pallas_call(
        body, grid=(_GROUPS, nc),
        in_specs=[sp['h64'], sp['h64'], sp['h64'], sp['h128'], sp['hrow'], sp['gq'], sp['gt'], sp['gq'], sp['gt'],
                  sp['st'], sp['h64']],
        out_specs=[sp['h64'], sp['h64'], sp['h64'], sp['gq'], sp['gq']],
        out_shape=[_SDS(xs.shape, F32)] * 3 + [_SDS(bg.shape, F32)] * 2,
        scratch_shapes=[pltpu.VMEM((_HPG, _NSTATE, _HDIM), F32)], name=name,
        compiler_params=_params(("parallel", "arbitrary"), 24 << 20))(xs, dt64, ac64, ac128, act, bg, btg, cg, ctg, sall, dy)


def _scan_rows(v, reverse):
    r = lax.broadcasted_iota(jnp.int32, v.shape, 0)
    for s in (1, 2, 4, 8, 16, 32, 64):
        if reverse:
            v = v + jnp.where(r < _CHUNK - s, pltpu.roll(v, _CHUNK - s, 0), 0.0)
        else:
            v = v + jnp.where(r >= s, pltpu.roll(v, s, 0), 0.0)
    return v


def _softplus(x):
    return jnp.maximum(x, 0.0) + jnp.log(1.0 + jnp.exp(-jnp.abs(x)))


def _sigmoid(x):
    return 1.0 / (1.0 + jnp.exp(-x))


def _silu(x):
    return x * _sigmoid(x)


def _dsilu(x):
    s = _sigmoid(x)
    return s * (1.0 + x * (1.0 - s))


def _shift(a, j):
    if j == 0:
        return a
    if j > 0:
        return jnp.pad(a, ((j, 0), (0, 0)))[:-j]
    return jnp.pad(a, ((0, -j), (0, 0)))[-j:]


def _rep(a, n):
    return jnp.broadcast_to(a.T[:, :, None], (a.shape[1], a.shape[0], n))


def _layer_fwd(x, p, l):
    t, d = x.shape
    aw = _HEADS * _HDIM
    sv = {}
    h1 = _rms_fwd(x, p['ln1_g'], name=f"ln1_fwd_{l}")
    proj = _mm(h1, p['w_in'], name=f"in_proj_{l}")
    q, k, v, z = (proj[:, i * aw:(i + 1) * aw] for i in range(4))
    xbc = proj[:, 4 * aw:4 * aw + p['conv_w'].shape[1]]
    dt_raw = proj[:, 4 * aw + xbc.shape[1]:4 * aw + xbc.shape[1] + _HEADS]

    outs = []
    for dil in _DILATIONS:
        qh, kh, vh = (_to_hm(a.astype(_BF), dil) for a in (q, k, v))
        sv[f'qkv{dil}'] = (qh, kh, vh)
        ob, lb = _attn_fwd(qh, kh, vh, dil, name=f"attn_fwd_d{dil}_{l}")
        outs += [_from_hm(ob), _from_hm(lb)]

    def combine(o1, l1, o2, l2, o3, l3):
        m = jnp.maximum(jnp.maximum(l1, l2), l3)
        e1, e2, e3 = jnp.exp(l1 - m), jnp.exp(l2 - m), jnp.exp(l3 - m)
        tot = e1 + e2 + e3
        return (e1 * o1 + e2 * o2 + e3 * o3) / tot, m + jnp.log(tot)
    attn, lse = _rows(combine, outs, [], [(aw, F32), (aw, F32)], tile=_tile_for(aw), name=f"attn_combine_{l}")
    attn_n = _rms_fwd(attn, p['attn_norm_g'], name=f"attn_norm_fwd_{l}")

    us = [_shift(xbc, j) for j in range(_CONV_K)]

    def conv(u0, u1, u2, u3, w, b):
        return _silu(w[0:1] * u3 + w[1:2] * u2 + w[2:3] * u1 + w[3:4] * u0 + b)
    cch = xbc.shape[1]
    act = _rows(conv, us, [p['conv_w'], p['conv_b'].reshape(1, cch)], [(cch, F32)], tile=_tile_for(cch), name=f"conv_fwd_{l}")[0]
    xs, bm, cm = act[:, :aw], act[:, aw:aw + _GROUPS * _NSTATE], act[:, aw + _GROUPS * _NSTATE:]

    def dtf(raw, bias, alog):
        dt = _softplus(raw + bias)
        return dt, _scan_rows(dt * -jnp.exp(alog), False)
    dt, acum = _rows(dtf, [dt_raw], [p['dt_bias'].reshape(1, _HEADS), p['a_log'].reshape(1, _HEADS)],
                     [(_HEADS, F32), (_HEADS, F32)], tile=_CHUNK, name=f"dt_fwd_{l}")
    xs_hm = _to_hm(xs, 1)[0]
    bg = bm.reshape(t, _GROUPS, _NSTATE).transpose(1, 0, 2)
    cg = cm.reshape(t, _GROUPS, _NSTATE).transpose(1, 0, 2)
    ssd_in = (xs_hm, _rep(dt, _HDIM), _rep(acum, _HDIM), _rep(acum, 128), acum.T[:, None, :], bg, bg.transpose(0, 2, 1), cg)
    y_hm, sall = _ssd_fwd(*ssd_in, name=f"ssd_fwd_{l}")
    y_ssd = _from_hm(y_hm[None])
    dskip = jnp.repeat(p['d_skip'], _HDIM).reshape(1, aw)

    def gate(y, xs, z, dsk):
        return (y + dsk * xs) * _silu(z)
    y2 = _rows(gate, [y_ssd, xs, z], [dskip], [(aw, F32)], tile=_tile_for(aw), name=f"gate_fwd_{l}")[0]
    y_n = _rms_fwd(y2, p['ssd_norm_g'], groups=_GROUPS, name=f"ssd_norm_fwd_{l}")

    mix = jnp.concatenate([attn_n, y_n], axis=1)
    x2 = _mm(mix, p['w_out'], extra=(x,), epi=lambda acc, r: (acc + r,), name=f"out_proj_{l}")
    h2 = _rms_fwd(x2, p['ln2_g'], name=f"ln2_fwd_{l}")
    u, a = _mm(h2, p['w_mlp_in'], epi=lambda acc: (acc, jnp.square(jnp.maximum(acc, 0.0))), outs=(F32, _BF), name=f"mlp_in_{l}")
    x3 = _mm(a, p['w_mlp_out'], extra=(x2,), epi=lambda acc, r: (acc + r,), name=f"mlp_out_{l}")
    sv.update(x=x, h1=h1, z=z, xbc=xbc, us=us, dt_raw=dt_raw, attn=attn, lse=lse, xs=xs, dt=dt, ssd_in=ssd_in, cg=cg,
              sall=sall, y_ssd=y_ssd, dskip=dskip, y2=y2, mix=mix, x2=x2, h2=h2, u=u, a=a)
    return x3, sv


def _layer_bwd(dx3, p, sv, l):
    t = dx3.shape[0]
    aw = _HEADS * _HDIM
    g = {}
    dx3b = dx3.astype(_BF)
    du = _mm(dx3b, p['w_mlp_out'], tb=True, extra=(sv['u'],), outs=(_BF,),
             epi=lambda acc, u: (acc * 2.0 * jnp.maximum(u, 0.0),), name=f"mlp_out_dx_{l}")
    g['w_mlp_out'] = _mm(sv['a'], dx3b, ta=True, name=f"mlp_out_dw_{l}")
    g['w_mlp_in'] = _mm(sv['h2'], du, ta=True, name=f"mlp_in_dw_{l}")
    dh2 = _mm(du, p['w_mlp_in'], tb=True, name=f"mlp_in_dx_{l}")
    dx2, g['ln2_g'] = _rms_bwd(sv['x2'], dh2, p['ln2_g'], dx3, name=f"ln2_bwd_{l}")
    dx2b = dx2.astype(_BF)
    dmix = _mm(dx2b, p['w_out'], tb=True, name=f"out_proj_dx_{l}")
    g['w_out'] = _mm(sv['mix'], dx2b, ta=True, name=f"out_proj_dw_{l}")

    dattn, g['attn_norm_g'] = _rms_bwd(sv['attn'], dmix[:, :aw], p['attn_norm_g'], name=f"attn_norm_bwd_{l}")
    parts = []
    for dil in _DILATIONS:
        qh, kh, vh = sv[f'qkv{dil}']
        do, oh, lh = (_to_hm(a, dil) for a in (dattn, sv['attn'], sv['lse']))
        parts.append([_from_hm(a) for a in _attn_bwd(qh, kh, vh, do, oh, lh, dil, name=f"attn_bwd_d{dil}_{l}")])
    dqkv = [_rows(lambda a, b, c: a + b + c, [parts[0][i], parts[1][i], parts[2][i]], [], [(aw, _BF)],
                  tile=_tile_for(aw), name=f"attn_sum_{'qkv'[i]}_{l}")[0] for i in range(3)]

    def gate_bwd(x, dy, y, xs, z, dsk, gn):
        dy2 = _rms_bwd_tile(x, dy, gn)
        y1 = y + dsk * xs
        dy1 = dy2 * _silu(z)
        return dy1, dsk * dy1, (dy2 * y1 * _dsilu(z)), dy1 * xs, _rms_dg_tile(x, dy)

    def _rms_bwd_tile(x, dy, gn):
        outs_ = []
        for xs_, ds_, gs_ in zip(_split(x, _GROUPS), _split(dy, _GROUPS), _split(gn, _GROUPS)):
            r = _rstd(xs_)
            xh = xs_ * r
            gd = ds_ * gs_
            outs_.append(r * (gd - xh * jnp.mean(gd * xh, axis=-1, keepdims=True)))
        return jnp.concatenate(outs_, axis=-1)

    def _rms_dg_tile(x, dy):
        return jnp.concatenate([ds_ * xs_ * _rstd(xs_) for xs_, ds_ in zip(_split(x, _GROUPS), _split(dy, _GROUPS))], axis=-1)

    dy1, dxs_skip, dz, dsk_sum, gn_sum = _rows(
        gate_bwd, [sv['y2'], dmix[:, aw:], sv['y_ssd'], sv['xs'], sv['z']], [sv['dskip'], p['ssd_norm_g'].reshape(1, aw)],
        [(aw, F32), (aw, F32), (aw, _BF)], [aw, aw], tile=128, name=f"gate_bwd_{l}")
    g['ssd_norm_g'] = gn_sum.sum(axis=0)
    g['d_skip'] = dsk_sum.sum(axis=0).reshape(_HEADS, _HDIM).sum(axis=1)
    ssd_in = sv['ssd_in']
    dxs_hm, ddt_hm, da_hm, dbg, dcg = _ssd_bwd(*ssd_in, sv['cg'].transpose(0, 2, 1), sv['sall'], _to_hm(dy1, 1)[0],
                                                name=f"ssd_bwd_{l}")

    def dtb(da, ddtx, raw, dt, bias, alog):
        a = -jnp.exp(alog)
        dda = _scan_rows(da, True)
        draw = (dda * a + ddtx) * _sigmoid(raw + bias)
        return draw, draw, dda * dt * a
    draw, dbias, dalog = _rows(dtb, [da_hm[:, :, 0].T, ddt_hm[:, :, 0].T, sv['dt_raw'], sv['dt']],
                               [p['dt_bias'].reshape(1, _HEADS), p['a_log'].reshape(1, _HEADS)],
                               [(_HEADS, F32)], [_HEADS, _HEADS], tile=_CHUNK, name=f"dt_bwd_{l}")
    g['dt_bias'], g['a_log'] = dbias.sum(axis=0), dalog.sum(axis=0)
    dact = jnp.concatenate([_from_hm(dxs_hm[None]), dbg.transpose(1, 0, 2).reshape(t, -1), dcg.transpose(1, 0, 2).reshape(t, -1)], axis=1)
    us = sv['us']
    cch = dact.shape[1]

    def conv_bwd1(u0, u1, u2, u3, da, dxk, w, b):
        pre = w[0:1] * u3 + w[1:2] * u2 + w[2:3] * u1 + w[3:4] * u0 + b
        da = da + jnp.concatenate([dxk, jnp.zeros((dxk.shape[0], cch - aw), F32)], axis=1)
        dp = da * _dsilu(pre)
        return dp, dp * u3, dp * u2, dp * u1, dp * u0, dp
    dpre, *dws = _rows(conv_bwd1, [*us, dact, dxs_skip], [p['conv_w'], p['conv_b'].reshape(1, cch)], [(cch, F32)], [cch] * 5,
                       tile=128, name=f"conv_bwd_pre_{l}")
    g['conv_w'] = jnp.stack([dws[i].sum(axis=0) for i in range(_CONV_K)])
    g['conv_b'] = dws[4].sum(axis=0)

    def conv_bwd2(p0, p1, p2, p3, w):
        return w[3:4] * p0 + w[2:3] * p1 + w[1:2] * p2 + w[0:1] * p3
    dxbc = _rows(conv_bwd2, [_shift(dpre, -j) for j in range(_CONV_K)], [p['conv_w']], [(cch, _BF)], tile=_tile_for(cch),
                 name=f"conv_bwd_in_{l}")[0]
    pad = p['w_in'].shape[1] - (4 * aw + cch + _HEADS)
    dproj = jnp.concatenate([*dqkv, dz, dxbc, draw.astype(_BF), jnp.zeros((t, pad), _BF)], axis=1)
    g['w_in'] = _mm(sv['h1'], dproj, ta=True, name=f"in_proj_dw_{l}")
    dh1 = _mm(dproj, p['w_in'], tb=True, name=f"in_proj_dx_{l}")
    dx, g['ln1_g'] = _rms_bwd(sv['x'], dh1, p['ln1_g'], dx2, name=f"ln1_bwd_{l}")
    return dx, g


def _loss_bwd(x, g, tgt):
    w = x.shape[1]
    tile = _tile_for(w)

    def fn(x, tgt, g):
        r = _rstd(x)
        xh = x * r
        e = xh * g - tgt
        gd = e * (g / w)
        dx = r * (gd - xh * jnp.mean(gd * xh, axis=-1, keepdims=True))
        rowloss = 0.5 * jnp.mean(e * e, axis=-1, keepdims=True)
        return dx, (e / w) * xh, jnp.broadcast_to(rowloss, (tile, 128))
    dx, dg, ls = _rows(fn, [x, tgt], [g.reshape(1, w)], [(w, F32)], [w, 128], tile=tile, name="loss_head")
    return dx, dg.sum(axis=0), ls[:, 0].sum()


def _adamw(w, g, m, v, *, name):
    def fn(w, g, m, v):
        m2 = _B1 * m + (1.0 - _B1) * g
        v2 = _B2 * v + (1.0 - _B2) * jnp.square(g)
        m_hat = m2 / (1.0 - _B1 ** _STEP)
        v_hat = v2 / (1.0 - _B2 ** _STEP)
        return -_LR * (m_hat / (jnp.sqrt(v_hat) + _AEPS) + _WD * w), m2, v2
    width = w.shape[-1]
    flat = [a.reshape(-1, width) for a in (w, g, m, v)]
    tile = _pick(flat[0].shape[0], (_tile_for(width), 32, 8))
    res = _rows(fn, flat, [], [(width, F32)] * 3, tile=tile, name=name)
    return [r.reshape(w.shape) for r in res]


_HBM = pl.BlockSpec(memory_space=pltpu.HBM)


def _place():
    x, y, c = lax.axis_index("x"), lax.axis_index("y"), lax.axis_index("c")
    other_chips = [(1 - x, y), (x, 1 - y), (1 - x, 1 - y)]
    return x, y, c, other_chips


def _remote(src, dst, sems, i, dev):
    return pltpu.make_async_remote_copy(src_ref=src, dst_ref=dst, send_sem=sems[0].at[i], recv_sem=sems[1].at[i],
                                        device_id=dev, device_id_type=_MESH)


def _exchange8(v, *, reduce, name):
    r, w = v.shape

    def body(v_ref, all_ref, *rest):
        sems = rest[-2:]
        x, y, c, _ = _place()
        me = 4 * x + 2 * y + c
        all_ref[me] = v_ref[...]
        flips = [((d >> 2) & 1, (d >> 1) & 1, d & 1) for d in range(1, 8)]
        sends = [_remote(v_ref, all_ref.at[me], sems, i, (x ^ fx, y ^ fy, c ^ fc)) for i, (fx, fy, fc) in enumerate(flips)]
        for cp in sends:
            cp.start()
        for i, (fx, fy, fc) in enumerate(flips):
            _remote(v_ref, all_ref.at[me ^ (4 * fx + 2 * fy + fc)], sems, i, (x ^ fx, y ^ fy, c ^ fc)).wait_recv()
        for cp in sends:
            cp.wait_send()
        if reduce:
            acc = all_ref[0]
            for s in range(1, 8):
                acc = acc + all_ref[s]
            rest[0][...] = acc

    vm = pl.BlockSpec(memory_space=pltpu.VMEM)
    out_shape = [_SDS((8, r, w), v.dtype)] + ([_SDS((r, w), v.dtype)] if reduce else [])
    res = pl.pallas_call(body, in_specs=[vm], out_specs=[vm] * len(out_shape), out_shape=out_shape, name=name,
                         scratch_shapes=[pltpu.SemaphoreType.DMA((7,)), pltpu.SemaphoreType.DMA((7,))],
                         compiler_params=pltpu.CompilerParams(vmem_limit_bytes=int(32 << 20)))(v)
    return res[1] if reduce else res[0]


def _gather_weights(shards, *, name):
    n = len(shards)

    def body(*refs):
        ins, outs, sems, lsem = refs[:n], refs[n:2 * n], refs[2 * n:2 * n + 2], refs[2 * n + 2]
        x, y, c, chips = _place()
        k = 2 * x + y
        sibling = (x, y, 1 - c)
        own = [pltpu.make_async_copy(ins[t], outs[t].at[k], lsem.at[t]) for t in range(n)]
        for cp in own:
            cp.start()
        first = [_remote(ins[t].at[c], outs[t].at[k, c], sems, 6 * t + j, (px, py, c))
                 for t in range(n) for j, (px, py) in enumerate(chips)]
        for cp in first:
            cp.start()
        passed = []
        for t in range(n):
            for j, (px, py) in enumerate(chips):
                landed = outs[t].at[2 * px + py, c]
                _remote(ins[t].at[c], landed, sems, 6 * t + j, (px, py, c)).wait_recv()
                passed.append(_remote(landed, landed, sems, 6 * t + 3 + j, sibling))
                passed[-1].start()
        for t in range(n):
            for j, (px, py) in enumerate(chips):
                theirs = outs[t].at[2 * px + py, 1 - c]
                _remote(theirs, theirs, sems, 6 * t + 3 + j, sibling).wait_recv()
        for cp in first + passed:
            cp.wait_send()
        for cp in own:
            cp.wait()

    return pl.pallas_call(
        body, in_specs=[_HBM] * n, out_specs=[_HBM] * n, out_shape=[_SDS((4, *s.shape), s.dtype) for s in shards], name=name,
        scratch_shapes=[pltpu.SemaphoreType.DMA((6 * n,)), pltpu.SemaphoreType.DMA((6 * n,)), pltpu.SemaphoreType.DMA((n,))])(*shards)


def _swap(name, srcs, out_shapes, plan):
    n = len(srcs)

    def body(*refs):
        ins, outs = refs[:n], refs[n:n + len(out_shapes)]
        sems, lsem = refs[-3:-1], refs[-1]
        x, y, c, chips = _place()
        sends, landings, local = plan(x, y, c, chips, ins, outs)
        own = [pltpu.make_async_copy(s, d, lsem.at[i]) for i, (s, d) in enumerate(local)]
        for cp in own:
            cp.start()
        out = [_remote(s, d, sems, i, dev) for i, (s, d, dev) in enumerate(sends)]
        for cp in out:
            cp.start()
        for i, d in enumerate(landings):
            _remote(d, d, sems, i, sends[i][2]).wait_recv()
        for cp in out:
            cp.wait_send()
        for cp in own:
            cp.wait()

    return body, n


def _run_swap(name, srcs, out_shapes, plan, n_sends, n_local):
    body, n = _swap(name, srcs, out_shapes, plan)
    return pl.pallas_call(
        body, in_specs=[_HBM] * n, out_specs=[_HBM] * len(out_shapes), out_shape=out_shapes, name=name,
        scratch_shapes=[pltpu.SemaphoreType.DMA((n_sends,)), pltpu.SemaphoreType.DMA((n_sends,)),
                        pltpu.SemaphoreType.DMA((max(n_local, 1),))])(*srcs)


def _reduce_grads(grads, c, k):
    n = len(grads)

    def plan_a(x, y, c_, chips, ins, outs):
        sends = [(ins[t].at[kk, 1 - c_], outs[t].at[kk], (x, y, 1 - c_)) for t in range(n) for kk in range(4)]
        return sends, [outs[t].at[kk] for t in range(n) for kk in range(4)], []
    got = _run_swap("grad_swap_cores", grads, [_SDS((4, *g.shape[2:]), g.dtype) for g in grads], plan_a, 4 * n, 0)
    part = []
    for t, g in enumerate(grads):
        a, b = g.shape[2:]
        mine = lax.dynamic_index_in_dim(g, c, axis=1, keepdims=False)
        s = _rows(lambda p, q: p.astype(F32) + q.astype(F32), [mine.reshape(4 * a, b), got[t].reshape(4 * a, b)], [], [(b, _BF)],
                  tile=_pick(4 * a, (_tile_for(b), 32)), name=f"grad_sum_cores_{t}")[0]
        part.append(s.reshape(4, a, b))

    def plan_b(x, y, c_, chips, ins, outs):
        sends = [(ins[t].at[2 * px + py], outs[t].at[j], (px, py, c_)) for t in range(n) for j, (px, py) in enumerate(chips)]
        return sends, [outs[t].at[j] for t in range(n) for j in range(3)], []
    got = _run_swap("grad_swap_chips", part, [_SDS((3, *p.shape[1:]), p.dtype) for p in part], plan_b, 3 * n, 0)
    half = []
    for t, p in enumerate(part):
        a, b = p.shape[1:]
        mine = lax.dynamic_index_in_dim(p, k, axis=0, keepdims=False)
        half.append(_rows(lambda p0, p1, p2, p3: ((p0.astype(F32) + p1.astype(F32)) + p2.astype(F32)) + p3.astype(F32),
                          [mine, got[t][0], got[t][1], got[t][2]], [], [(b, F32)], tile=_pick(a, (_tile_for(b), 32)),
                          name=f"grad_sum_chips_{t}")[0])

    def plan_c(x, y, c_, chips, ins, outs):
        sends = [(ins[t], outs[t].at[c_], (x, y, 1 - c_)) for t in range(n)]
        return sends, [outs[t].at[1 - c_] for t in range(n)], [(ins[t], outs[t].at[c_]) for t in range(n)]
    return _run_swap("grad_share_cores", half, [_SDS((2, *h.shape), F32) for h in half], plan_c, n, n)


_BIG = ("w_in", "w_out", "w_mlp_in", "w_mlp_out")
_SMALL = ("ln1_g", "conv_b", "dt_bias", "a_log", "d_skip", "attn_norm_g", "ssd_norm_g", "ln2_g", "final_norm_g")
_ORDER = ("ln1_g", "w_in", "conv_w", "conv_b", "dt_bias", "a_log", "d_skip", "attn_norm_g", "ssd_norm_g", "w_out", "ln2_g",
          "w_mlp_in", "w_mlp_out", "final_norm_g")


def _pack(parts, rows):
    flat = jnp.concatenate([p.reshape(-1) for p in parts])
    return jnp.pad(flat, (0, rows * 128 - flat.shape[0])).reshape(rows, 128)


def _unpack(buf, like):
    flat, out, o = buf.reshape(-1), [], 0
    for p in like:
        out.append(flat[o:o + p.size].reshape(p.shape))
        o += p.size
    return out


def kernel(x, ln1_g, w_in, conv_w, conv_b, dt_bias, a_log, d_skip, attn_norm_g, ssd_norm_g, w_out, ln2_g, w_mlp_in, w_mlp_out, final_norm_g, loss_target, m_ln1_g, m_w_in, m_conv_w, m_conv_b, m_dt_bias, m_a_log, m_d_skip, m_attn_norm_g, m_ssd_norm_g, m_w_out, m_ln2_g, m_w_mlp_in, m_w_mlp_out, m_final_norm_g, v_ln1_g, v_w_in, v_conv_w, v_conv_b, v_dt_bias, v_a_log, v_d_skip, v_attn_norm_g, v_ssd_norm_g, v_w_out, v_ln2_g, v_w_mlp_in, v_w_mlp_out, v_final_norm_g):
    w = dict(ln1_g=ln1_g, w_in=w_in, conv_w=conv_w, conv_b=conv_b, dt_bias=dt_bias, a_log=a_log, d_skip=d_skip,
             attn_norm_g=attn_norm_g, ssd_norm_g=ssd_norm_g, w_out=w_out, ln2_g=ln2_g, w_mlp_in=w_mlp_in, w_mlp_out=w_mlp_out,
             final_norm_g=final_norm_g)
    m = dict(ln1_g=m_ln1_g, w_in=m_w_in, conv_w=m_conv_w, conv_b=m_conv_b, dt_bias=m_dt_bias, a_log=m_a_log, d_skip=m_d_skip,
             attn_norm_g=m_attn_norm_g, ssd_norm_g=m_ssd_norm_g, w_out=m_w_out, ln2_g=m_ln2_g, w_mlp_in=m_w_mlp_in,
             w_mlp_out=m_w_mlp_out, final_norm_g=m_final_norm_g)
    v = dict(ln1_g=v_ln1_g, w_in=v_w_in, conv_w=v_conv_w, conv_b=v_conv_b, dt_bias=v_dt_bias, a_log=v_a_log, d_skip=v_d_skip,
             attn_norm_g=v_attn_norm_g, ssd_norm_g=v_ssd_norm_g, w_out=v_w_out, ln2_g=v_ln2_g, w_mlp_in=v_w_mlp_in,
             w_mlp_out=v_w_mlp_out, final_norm_g=v_final_norm_g)
    depth, d_model = ln1_g.shape
    n_chips = 4
    c = lax.axis_index("c")
    chip = 2 * lax.axis_index("x") + lax.axis_index("y")
    in_proj = w_in.shape[2] * n_chips
    in_pad = -(-in_proj // 640) * 640
    cch = conv_w.shape[2] * n_chips

    cw = _exchange8(conv_w.reshape(depth * _CONV_K, -1), reduce=False, name="gather_conv_w")[0::2]
    conv_full = cw.reshape(n_chips, depth, _CONV_K, -1).transpose(1, 2, 0, 3).reshape(depth, _CONV_K, cch)
    g_in, g_out, g_mi, g_mo = _gather_weights([w[n].astype(_BF) for n in _BIG], name="gather_weights")
    layers = []
    for l in range(depth):
        p = {n: w[n][l] for n in _SMALL[:-1]}
        p['conv_w'] = conv_full[l]
        p['w_in'] = jnp.pad(g_in[:, l].transpose(1, 0, 2).reshape(d_model, in_proj), ((0, 0), (0, in_pad - in_proj)))
        p['w_out'] = g_out[:, l].reshape(-1, d_model)
        p['w_mlp_in'] = g_mi[:, l].transpose(1, 0, 2).reshape(d_model, -1)
        p['w_mlp_out'] = g_mo[:, l].reshape(-1, d_model)
        layers.append(p)

    h, saved = x[0], []
    for l, p in enumerate(layers):
        h, sv = _layer_fwd(h, p, l)
        saved.append(sv)
    dx, g_final, loss_part = _loss_bwd(h, final_norm_g, loss_target[0])
    grads = [None] * depth
    for l in reversed(range(depth)):
        dx, grads[l] = _layer_bwd(dx, layers[l], saved[l], l)

    def by_chip(g, name):
        if name == "w_in":
            return g[:, :in_proj].reshape(d_model, n_chips, -1).transpose(1, 0, 2)
        if name == "w_mlp_in":
            return g.reshape(d_model, n_chips, -1).transpose(1, 0, 2)
        return g.reshape(n_chips, -1, d_model)
    big = [jnp.stack([by_chip(grads[l][n], n).astype(_BF) for l in range(depth)], axis=1) for n in _BIG]
    red = dict(zip(_BIG, _reduce_grads(big, c, chip)))

    small = {n: jnp.stack([grads[l][n] for l in range(depth)]) for n in _SMALL[:-1] + ("conv_w",)}
    small["final_norm_g"] = g_final
    parts = [loss_part.reshape(1)] + [small[n] for n in _SMALL + ("conv_w",)]
    rows = -(-sum(p.size for p in parts) // 1024) * 8
    tot = _unpack(_exchange8(_pack(parts, rows), reduce=True, name="allreduce_small"), parts)
    loss = tot[0][0]
    red.update(zip(_SMALL + ("conv_w",), tot[1:]))
    red["conv_w"] = lax.dynamic_index_in_dim(red["conv_w"].reshape(depth, _CONV_K, n_chips, -1), chip, axis=2, keepdims=False)

    delta, new_m, new_v = {}, {}, {}
    for n in _BIG:
        delta[n], new_m[n], new_v[n] = _adamw(w[n], red[n], m[n], v[n], name=f"adamw_{n}")
    names = _SMALL + ("conv_w",)
    like = [w[n] for n in names]
    srows = -(-sum(p.size for p in like) // 1024) * 8
    res = _adamw(*[_pack([d[n] for n in names], srows) for d in (w, red, m, v)], name="adamw_small")
    for dst, buf in zip((delta, new_m, new_v), res):
        dst.update(zip(names, _unpack(buf, like)))
    return (loss, dx[None], *[red[n] for n in _ORDER], *[delta[n] for n in _ORDER], *[new_m[n] for n in _ORDER],
            *[new_v[n] for n in _ORDER])
```

```python
import numpy as np
import jax
import jax.numpy as jnp
from jax import lax
from jax.experimental import pallas as pl
from jax.experimental.pallas import tpu as pltpu

F32 = jnp.float32
_BF = jnp.bfloat16
_NEG = -1e30
_EPS = 1e-5
_HEADS = 16
_HDIM = 64
_AW = _HEADS * _HDIM
_ABLK = 128
_DILATIONS = (1, 4, 16)
_CHUNK = 128
_NSTATE = 128
_GROUPS = 2
_HPG = _HEADS // _GROUPS
_CONV_K = 4
_LANES = 128
_LR, _B1, _B2, _AEPS, _WD, _STEP = 0.001, 0.9, 0.999, 1e-08, 0.01, 10
_VMEM_CAP = 56 * 1024 * 1024
_MESH = pl.DeviceIdType.MESH
_SDS = jax.ShapeDtypeStruct
_NT = (((1,), (1,)), ((), ()))
_TN = (((0,), (0,)), ((), ()))


def _params(sem, est_bytes):
    lim = int(min(max(2 * est_bytes + (4 << 20), 16 << 20), _VMEM_CAP))
    return pltpu.CompilerParams(dimension_semantics=sem, vmem_limit_bytes=lim)


def _nbytes(shape, dtype):
    return int(np.prod(shape)) * jnp.dtype(dtype).itemsize


def _dot(a, b, dims=(((1,), (0,)), ((), ()))):
    return lax.dot_general(a.astype(_BF), b.astype(_BF), dims, preferred_element_type=F32)


def _rows(fn, ins, consts, outs, sums=(), *, tile, name):
    ins = [a if isinstance(a, tuple) else (a, a.shape[1], 0) for a in ins]
    rows = ins[0][0].shape[0]
    n_in, n_c, n_o, n_s = len(ins), len(consts), len(outs), len(sums)

    def body(*refs):
        vals = [r[...] for r in refs[:n_in + n_c]]
        res = fn(*vals)
        res = res if isinstance(res, tuple) else (res,)
        orefs = refs[n_in + n_c:n_in + n_c + n_o]
        srefs = refs[n_in + n_c + n_o:]
        for r, v in zip(orefs, res[:n_o]):
            r[...] = v.astype(r.dtype)
        if n_s:
            @pl.when(pl.program_id(0) == 0)
            def _():
                for r in srefs:
                    r[...] = jnp.zeros_like(r)
            for r, v in zip(srefs, res[n_o:]):
                r[...] += v.reshape(tile // 8, 8, v.shape[-1]).sum(axis=0)

    in_specs = [pl.BlockSpec((tile, w), lambda i, j=j: (i, j)) for _, w, j in ins]
    in_specs += [pl.BlockSpec(c.shape, lambda i, nd=c.ndim: (0,) * nd) for c in consts]
    out_shape = [_SDS((rows, w), dt) for w, dt in outs] + [_SDS((8, w), F32) for w in sums]
    out_specs = [pl.BlockSpec((tile, w), lambda i: (i, 0)) for w, _ in outs]
    out_specs += [pl.BlockSpec((8, w), lambda i: (0, 0)) for w in sums]
    est = sum(_nbytes((tile, w), a.dtype) for a, w, _ in ins) + sum(_nbytes((tile, w), dt) for w, dt in outs)
    return pl.pallas_call(body, grid=(rows // tile,), in_specs=in_specs, out_specs=out_specs, out_shape=out_shape,
                          name=name, compiler_params=_params(("arbitrary",), 3 * est))(*[a for a, _, _ in ins], *consts)


def _tile_for(width):
    return max(c for c in (256, 128, 64, 32) if c * width <= (1 << 18) or c == 32)


def _rstd(x):
    return lax.rsqrt(jnp.mean(x * x, axis=-1, keepdims=True) + _EPS)


def _split(x, groups):
    w = x.shape[-1] // groups
    return [x[:, g * w:(g + 1) * w] for g in range(groups)]


def _cat(parts):
    return parts[0] if len(parts) == 1 else jnp.concatenate(parts, axis=-1)


def _rms_bwd_tile(x, dy, g, groups):
    dxs, dgs = [], []
    for xs, ds, gs in zip(_split(x, groups), _split(dy.astype(F32), groups), _split(g, groups)):
        r = _rstd(xs)
        xh = xs * r
        gd = ds * gs
        dxs.append(r * (gd - xh * jnp.mean(gd * xh, axis=-1, keepdims=True)))
        dgs.append(ds * xh)
    return _cat(dxs), _cat(dgs)


def _rms_fwd(x, g, *, groups=1, name):
    def fn(x, g):
        return _cat([xs * _rstd(xs) * gs for xs, gs in zip(_split(x, groups), _split(g, groups))])
    w = x.shape[1]
    return _rows(fn, [x], [g.reshape(1, w)], [(w, _BF)], tile=_tile_for(w), name=name)[0]


def _rms_bwd(x, dy, g, res=None, *, name):
    def fn(x, dy, *rest):
        dx, dg = _rms_bwd_tile(x, dy, rest[-1], 1)
        return (dx + rest[0] if res is not None else dx), dg
    w = x.shape[1]
    ins = [x, dy] + ([res] if res is not None else [])
    dx, dg = _rows(fn, ins, [g.reshape(1, w)], [(w, F32)], [w], tile=_tile_for(w), name=name)
    return dx, dg.sum(axis=0)


def _pick(n, cands):
    for c in cands:
        if n % c == 0:
            return c
    raise ValueError(f"no block size for {n}")


_MM_BLOCKS = (1024, 640, 512, 384)


def _mm(a, b, *, ta=False, tb=False, extra=(), epi=None, outs=(F32,), name):
    m, k = (a.shape[1], a.shape[0]) if ta else a.shape
    n = b.shape[0] if tb else b.shape[1]
    assert k == (b.shape[1] if tb else b.shape[0])
    bm, bn, bk = _pick(m, _MM_BLOCKS), _pick(n, _MM_BLOCKS), _pick(k, _MM_BLOCKS)
    nk = k // bk
    n_e, n_o = len(extra), len(outs)
    dims = (((0 if ta else 1,), (1 if tb else 0,)), ((), ()))

    def body(a_ref, b_ref, *rest):
        ex, orefs, acc = rest[:n_e], rest[n_e:n_e + n_o], rest[-1]
        kk = pl.program_id(2)

        @pl.when(kk == 0)
        def _():
            acc[...] = jnp.zeros_like(acc)

        acc[...] += _dot(a_ref[...], b_ref[...], dims)

        @pl.when(kk == nk - 1)
        def _():
            r = acc[...]
            res = epi(r, *[e[...] for e in ex]) if epi is not None else (r,)
            for o, v in zip(orefs, res):
                o[...] = v.astype(o.dtype)

    a_spec = pl.BlockSpec((bk, bm), lambda i, j, kk: (kk, i)) if ta else pl.BlockSpec((bm, bk), lambda i, j, kk: (i, kk))
    b_spec = pl.BlockSpec((bn, bk), lambda i, j, kk: (j, kk)) if tb else pl.BlockSpec((bk, bn), lambda i, j, kk: (kk, j))
    t_spec = pl.BlockSpec((bm, bn), lambda i, j, kk: (i, j))
    est = (_nbytes((bm, bk), a.dtype) + _nbytes((bk, bn), b.dtype) + sum(_nbytes((bm, bn), e.dtype) for e in extra)
           + sum(_nbytes((bm, bn), o) for o in outs)) * 2 + 2 * _nbytes((bm, bn), F32)
    res = pl.pallas_call(
        body, grid=(m // bm, n // bn, nk), in_specs=[a_spec, b_spec] + [t_spec] * n_e, out_specs=[t_spec] * n_o,
        out_shape=[_SDS((m, n), o) for o in outs], scratch_shapes=[pltpu.VMEM((bm, bn), F32)], name=name,
        compiler_params=_params(("parallel", "parallel", "arbitrary"), est))(a, b, *extra)
    return res[0] if n_o == 1 else res


def _add_to(acc, r):
    return (acc + r,)


def _alibi_bias(dilation):
    slopes = 2.0 ** (-8.0 * (np.arange(_HEADS) + 1) / _HEADS)
    i = np.arange(_ABLK)[:, None]
    j = np.arange(_ABLK)[None, :]
    cur = np.where(i - j >= 0, -slopes[:, None, None] * ((i - j) * dilation), _NEG)
    prev = np.where(j >= i, -slopes[:, None, None] * ((i - j + _ABLK) * dilation), _NEG)
    return jnp.asarray(cur, F32), jnp.asarray(prev, F32)


def _strided(a, d):
    return a.reshape(a.shape[0] // d, d * a.shape[1])


def _head(h):
    return slice(h * _HDIM, (h + 1) * _HDIM)


def _attn_specs(nb, n_parts):
    def cur(p):
        return pl.BlockSpec((_ABLK, _AW), lambda r, j: (jnp.minimum(j, nb - 1), r * n_parts + p))

    def prev(p):
        return pl.BlockSpec((_ABLK, _AW), lambda r, j: (jnp.clip(j - 1, 0, nb - 1), r * n_parts + p))
    return cur, prev


def _attn_fwd(qkv, dilation, *, name):
    t = qkv.shape[0]
    nb = t // dilation // _ABLK
    bc, bp = _alibi_bias(dilation)
    scale = _HDIM ** -0.5

    def body(q_ref, kc_ref, kp_ref, vc_ref, vp_ref, bc_ref, bp_ref, o_ref, l_ref):
        first = pl.program_id(1) == 0
        for h in range(_HEADS):
            sl = _head(h)
            q = q_ref[:, sl]
            sc = _dot(q, kc_ref[:, sl], _NT) * scale + bc_ref[h]
            sp = jnp.where(first, _NEG, _dot(q, kp_ref[:, sl], _NT) * scale + bp_ref[h])
            m = jnp.maximum(jnp.max(sc, axis=-1, keepdims=True), jnp.max(sp, axis=-1, keepdims=True))
            pc, pp = jnp.exp(sc - m), jnp.exp(sp - m)
            den = jnp.sum(pc, axis=-1, keepdims=True) + jnp.sum(pp, axis=-1, keepdims=True)
            o_ref[:, sl] = (_dot(pc, vc_ref[:, sl]) + _dot(pp, vp_ref[:, sl])) / den
            l_ref[:, sl] = jnp.broadcast_to(m + jnp.log(den), (_ABLK, _HDIM))

    cur, prev = _attn_specs(nb, 3)
    cur1, _ = _attn_specs(nb, 1)
    bias = pl.BlockSpec((_HEADS, _ABLK, _ABLK), lambda r, j: (0, 0, 0))
    sv = _strided(qkv, dilation)
    out =_SDS((t // dilation, dilation * _AW), F32)
    o, l = pl.pallas_call(
        body, grid=(dilation, nb), in_specs=[cur(0), cur(1), prev(1), cur(2), prev(2), bias, bias],
        out_specs=[cur1(0), cur1(0)], out_shape=[out, out], name=name,
        compiler_params=_params(("parallel", "arbitrary"), 16 << 20))(sv, sv, sv, sv, sv, bc, bp)
    return o.reshape(t, _AW), l.reshape(t, _AW)


def _attn_bwd(qkv, do, o, lse, dilation, *, name):
    t = qkv.shape[0]
    nb = t // dilation // _ABLK
    bc, bp = _alibi_bias(dilation)
    scale = _HDIM ** -0.5

    def body(q_ref, kc_ref, kp_ref, vc_ref, vp_ref, do_ref, o_ref, l_ref, bc_ref, bp_ref, dq_ref, dk_ref, dv_ref, ck, cv):
        n = pl.program_id(1)

        @pl.when(n == 0)
        def _():
            ck[...] = jnp.zeros_like(ck)
            cv[...] = jnp.zeros_like(cv)

        @pl.when(n < nb)
        def _():
            for h in range(_HEADS):
                sl = _head(h)
                q, kc, kp, vc, vp = q_ref[:, sl], kc_ref[:, sl], kp_ref[:, sl], vc_ref[:, sl], vp_ref[:, sl]
                dov = do_ref[:, sl]
                dsum = jnp.sum(dov * o_ref[:, sl], axis=-1, keepdims=True)
                lrow = jnp.max(l_ref[:, sl], axis=-1, keepdims=True)
                pc = jnp.exp(_dot(q, kc, _NT) * scale + bc_ref[h] - lrow)
                pp = jnp.exp(jnp.where(n == 0, _NEG, _dot(q, kp, _NT) * scale + bp_ref[h]) - lrow)
                dsc = pc * (_dot(dov, vc, _NT) - dsum)
                dsp = pp * (_dot(dov, vp, _NT) - dsum)
                dq_ref[:, sl] = (_dot(dsc, kc) + _dot(dsp, kp)) * scale
                dk_ref[:, sl] = ck[:, sl] + _dot(dsp, q, _TN) * scale
                dv_ref[:, sl] = cv[:, sl] + _dot(pp, dov, _TN)
                ck[:, sl] = _dot(dsc, q, _TN) * scale
                cv[:, sl] = _dot(pc, dov, _TN)

        @pl.when(n == nb)
        def _():
            dk_ref[...] = ck[...]
            dv_ref[...] = cv[...]

    cur, prev = _attn_specs(nb, 3)
    cur1, prev1 = _attn_specs(nb, 1)
    bias = pl.BlockSpec((_HEADS, _ABLK, _ABLK), lambda r, j: (0, 0, 0))
    sv = _strided(qkv, dilation)
    s1 = [_strided(a, dilation) for a in (do, o, lse)]
    dqkv = pl.pallas_call(
        body, grid=(dilation, nb + 1),
        in_specs=[cur(0), cur(1), prev(1), cur(2), prev(2), cur1(0), cur1(0), cur1(0), bias, bias],
        out_specs=[cur1(0), prev1(0), prev1(0)], out_shape=[_SDS(s1[0].shape, F32)] * 3, name=name,
        scratch_shapes=[pltpu.VMEM((_ABLK, _AW), F32)] * 2,
        compiler_params=_params(("parallel", "arbitrary"), 16 << 20))(sv, sv, sv, sv, sv, *s1, bc, bp)
    return [a.reshape(t, _AW) for a in dqkv]


def _ssd_in_specs(ch):
    return dict(
        xs=pl.BlockSpec((_CHUNK, _AW), lambda c: (ch(c), 0)),
        bc=pl.BlockSpec((_CHUNK, 2 * _GROUPS * _NSTATE), lambda c: (ch(c), _AW // (2 * _GROUPS * _NSTATE))),
        lane=pl.BlockSpec((_CHUNK, _LANES), lambda c: (ch(c), 0)),
        arow=pl.BlockSpec((_HEADS, 1, _CHUNK), lambda c: (0, 0, ch(c))),
        st=pl.BlockSpec((1, _HEADS, _NSTATE, _HDIM), lambda c: (ch(c), 0, 0, 0)),
    )


def _decay(a_col, a_row):
    i0 = lax.broadcasted_iota(jnp.int32, (_CHUNK, _CHUNK), 0)
    i1 = lax.broadcasted_iota(jnp.int32, (_CHUNK, _CHUNK), 1)
    return jnp.where(i0 >= i1, jnp.exp(a_col - a_row), 0.0), jnp.where(i1 >= i0, jnp.exp(a_row - a_col), 0.0)


def _ssd_fwd(act, dt, acum, a_row, *, name):
    t = act.shape[0]
    nc = t // _CHUNK
    sp = _ssd_in_specs(lambda c: c)
    gw = _GROUPS * _NSTATE

    def body(xs_ref, bc_ref, dt_ref, ac_ref, ar_ref, y_ref, sall_ref, st):
        @pl.when(pl.program_id(0) == 0)
        def _():
            st[...] = jnp.zeros_like(st)

        for g in range(_GROUPS):
            bg = bc_ref[:, g * _NSTATE:(g + 1) * _NSTATE]
            cg = bc_ref[:, gw + g * _NSTATE:gw + (g + 1) * _NSTATE].astype(_BF)
            cb = _dot(cg, bg, _NT)
            for h in range(g * _HPG, (g + 1) * _HPG):
                a_col = ac_ref[:, h:h + 1]
                x = (xs_ref[:, _head(h)] * dt_ref[:, h:h + 1]).astype(_BF)
                lm, _ = _decay(a_col, ar_ref[h])
                sv = st[h]
                sall_ref[0, h] = sv
                y_ref[:, _head(h)] = _dot(cb * lm, x) + jnp.exp(a_col) * _dot(cg, sv)
                a_last = jnp.min(a_col, axis=0, keepdims=True)
                st[h] = jnp.exp(a_last) * sv + _dot(bg * jnp.exp(a_last - a_col), x, _TN)

    return pl.pallas_call(
        body, grid=(nc,), in_specs=[sp['xs'], sp['bc'], sp['lane'], sp['lane'], sp['arow']],
        out_specs=[sp['xs'], sp['st']], out_shape=[_SDS((t, _AW), F32), _SDS((nc, _HEADS, _NSTATE, _HDIM), F32)],
        scratch_shapes=[pltpu.VMEM((_HEADS, _NSTATE, _HDIM), F32)], name=name,
        compiler_params=_params(("arbitrary",), 16 << 20))(act, act, dt, acum, a_row)


def _ssd_bwd(act, dt, acum, a_row, sall, dy, *, name):
    t = act.shape[0]
    nc = t // _CHUNK
    sp = _ssd_in_specs(lambda c: nc - 1 - c)
    gw = _GROUPS * _NSTATE

    def rsum(v):
        return jnp.sum(v, axis=-1, keepdims=True)

    def body(xs_ref, bc_ref, dt_ref, ac_ref, ar_ref, sall_ref, dy_ref, dxs_ref, dbc_ref, ddt_ref, da_ref, dst):
        @pl.when(pl.program_id(0) == 0)
        def _():
            dst[...] = jnp.zeros_like(dst)

        ddt_ref[...] = jnp.zeros_like(ddt_ref)
        da_ref[...] = jnp.zeros_like(da_ref)
        row = lax.broadcasted_iota(jnp.int32, (_CHUNK, 1), 0)
        for g in range(_GROUPS):
            bg = bc_ref[:, g * _NSTATE:(g + 1) * _NSTATE]
            bgb = bg.astype(_BF)
            cg = bc_ref[:, gw + g * _NSTATE:gw + (g + 1) * _NSTATE].astype(_BF)
            cb, cbt = _dot(cg, bgb, _NT), _dot(bgb, cg, _NT)
            dcb = jnp.zeros((_CHUNK, _CHUNK), F32)
            dbg = jnp.zeros((_CHUNK, _NSTATE), F32)
            dcg = jnp.zeros((_CHUNK, _NSTATE), F32)
            for h in range(g * _HPG, (g + 1) * _HPG):
                a_col, dt_col = ac_ref[:, h:h + 1], dt_ref[:, h:h + 1]
                xsv = xs_ref[:, _head(h)]
                x = xsv * dt_col
                xb = x.astype(_BF)
                dyv = dy_ref[:, _head(h)]
                dyb = dyv.astype(_BF)
                lm, lmt = _decay(a_col, ar_ref[h])
                gm, gmt = cb * lm, cbt * lmt
                sv, dsv = sall_ref[0, h], dst[h]
                svb, dsb = sv.astype(_BF), dsv.astype(_BF)
                e_col = jnp.exp(a_col)
                a_last = jnp.min(a_col, axis=0, keepdims=True)
                w_col = jnp.exp(a_last - a_col)
                bwds = _dot(bg * w_col, dsb)
                dx = _dot(gm, dyb, _TN) + bwds
                dg, dgt = _dot(dyb, xb, _NT), _dot(xb, dyb, _NT)
                dcb = dcb + dg * lm
                edy = (e_col * dyv).astype(_BF)
                dcg = dcg + _dot(edy, svb, _NT)
                dbg = dbg + w_col * _dot(xb, dsb, _NT)
                z = rsum(x * bwds)
                da = rsum(dg * gm) - rsum(dgt * gmt) + rsum(dyv * (e_col * _dot(cg, svb))) - z
                da_last = jnp.sum(z, axis=0, keepdims=True) + jnp.exp(a_last) * jnp.sum(rsum(sv * dsv), axis=0, keepdims=True)
                da_ref[:, h:h + 1] = da + jnp.where(row == _CHUNK - 1, da_last, 0.0)
                ddt_ref[:, h:h + 1] = rsum(dx * xsv)
                dxs_ref[:, _head(h)] = dx * dt_col
                dst[h] = jnp.exp(a_last) * dsv + _dot(cg, edy, _TN)
            dbc_ref[:, g * _NSTATE:(g + 1) * _NSTATE] = dbg + _dot(dcb, cg, _TN)
            dbc_ref[:, gw + g * _NSTATE:gw + (g + 1) * _NSTATE] = dcg + _dot(dcb, bgb)

    ch = lambda c: nc - 1 - c
    wide = pl.BlockSpec((_CHUNK, 2 * gw), lambda c: (ch(c), 0))
    return pl.pallas_call(
        body, grid=(nc,), in_specs=[sp['xs'], sp['bc'], sp['lane'], sp['lane'], sp['arow'], sp['st'], sp['xs']],
        out_specs=[sp['xs'], wide, sp['lane'], sp['lane']],
        out_shape=[_SDS((t, _AW), F32), _SDS((t, 2 * gw), F32), _SDS((t, _LANES), F32), _SDS((t, _LANES), F32)],
        scratch_shapes=[pltpu.VMEM((_HEADS, _NSTATE, _HDIM), F32)], name=name,
        compiler_params=_params(("arbitrary",), 16 << 20))(act, act, dt, acum, a_row, sall, dy)


def _scan_rows(v, reverse):
    r = lax.broadcasted_iota(jnp.int32, v.shape, 0)
    for s in (1, 2, 4, 8, 16, 32, 64):
        if reverse:
            v = v + jnp.where(r < _CHUNK - s, pltpu.roll(v, _CHUNK - s, 0), 0.0)
        else:
            v = v + jnp.where(r >= s, pltpu.roll(v, s, 0), 0.0)
    return v


def _softplus(x):
    return jnp.maximum(x, 0.0) + jnp.log(1.0 + jnp.exp(-jnp.abs(x)))


def _sigmoid(x):
    return 1.0 / (1.0 + jnp.exp(-x))


def _silu(x):
    return x * _sigmoid(x)


def _dsilu(x):
    s = _sigmoid(x)
    return s * (1.0 + x * (1.0 - s))


def _shift(a, j):
    if j == 0:
        return a
    if j > 0:
        return jnp.pad(a, ((j, 0), (0, 0)))[:-j]
    return jnp.pad(a, ((0, -j), (0, 0)))[-j:]


def _lanes(a):
    return jnp.pad(a, (0, _LANES - a.shape[0])).reshape(1, _LANES)


def _layer_fwd(x, p, l):
    cch = p['conv_w'].shape[1]
    sv = {}
    h1 = _rms_fwd(x, p['ln1_g'], name=f"ln1_fwd_{l}")
    qkv = _mm(h1, p['w_qkv'], outs=(_BF,), name=f"in_proj_qkv_{l}")
    xbc = _mm(h1, p['w_xbc'], name=f"in_proj_xbc_{l}")
    zdt = _mm(h1, p['w_zdt'], name=f"in_proj_zdt_{l}")
    z, dt_raw = (zdt, _AW, 0), (zdt, _LANES, _AW // _LANES)

    outs = []
    for dil in _DILATIONS:
        outs += _attn_fwd(qkv, dil, name=f"attn_fwd_d{dil}_{l}")

    def combine(o1, l1, o2, l2, o3, l3):
        m = jnp.maximum(jnp.maximum(l1, l2), l3)
        e1, e2, e3 = jnp.exp(l1 - m), jnp.exp(l2 - m), jnp.exp(l3 - m)
        tot = e1 + e2 + e3
        return (e1 * o1 + e2 * o2 + e3 * o3) / tot, m + jnp.log(tot)
    attn, lse = _rows(combine, outs, [], [(_AW, F32), (_AW, F32)], tile=_tile_for(_AW), name=f"attn_combine_{l}")
    attn_n = _rms_fwd(attn, p['attn_norm_g'], name=f"attn_norm_fwd_{l}")

    us = [_shift(xbc, j) for j in range(_CONV_K)]

    def conv(u0, u1, u2, u3, w, b):
        return _silu(w[0:1] * u3 + w[1:2] * u2 + w[2:3] * u1 + w[3:4] * u0 + b)
    act = _rows(conv, us, [p['conv_w'], p['conv_b'].reshape(1, cch)], [(cch, F32)], tile=_tile_for(cch), name=f"conv_fwd_{l}")[0]

    def dtf(raw, bias, alog):
        dt = _softplus(raw + bias)
        return dt, _scan_rows(dt * -jnp.exp(alog), False)
    dt, acum = _rows(dtf, [dt_raw], [_lanes(p['dt_bias']), _lanes(p['a_log'])], [(_LANES, F32), (_LANES, F32)],
                     tile=_CHUNK, name=f"dt_fwd_{l}")
    a_row = acum[:, :_HEADS].T[:, None, :]
    y_ssd, sall = _ssd_fwd(act, dt, acum, a_row, name=f"ssd_fwd_{l}")
    dskip = jnp.repeat(p['d_skip'], _HDIM).reshape(1, _AW)
    xs = (act, _AW, 0)

    def gate(y, xs, z, dsk):
        return (y + dsk * xs) * _silu(z)
    y2 = _rows(gate, [y_ssd, xs, z], [dskip], [(_AW, F32)], tile=_tile_for(_AW), name=f"gate_fwd_{l}")[0]
    y_n = _rms_fwd(y2, p['ssd_norm_g'], groups=_GROUPS, name=f"ssd_norm_fwd_{l}")

    mix = jnp.concatenate([attn_n, y_n], axis=1)
    x2 = _mm(mix, p['w_out'], extra=(x,), epi=_add_to, name=f"out_proj_{l}")
    h2 = _rms_fwd(x2, p['ln2_g'], name=f"ln2_fwd_{l}")
    u, a = _mm(h2, p['w_mlp_in'], epi=lambda acc: (acc, jnp.square(jnp.maximum(acc, 0.0))), outs=(F32, _BF), name=f"mlp_in_{l}")
    x3 = _mm(a, p['w_mlp_out'], extra=(x2,), epi=_add_to, name=f"mlp_out_{l}")
    sv.update(x=x, h1=h1, qkv=qkv, zdt=zdt, us=us, attn=attn, lse=lse, act=act, dt=dt, acum=acum, a_row=a_row,
              sall=sall, y_ssd=y_ssd, dskip=dskip, y2=y2, mix=mix, x2=x2, h2=h2, u=u, a=a)
    return x3, sv


def _layer_bwd(dx3, p, sv, l):
    cch = p['conv_w'].shape[1]
    g = {}
    dx3b = dx3.astype(_BF)
    du = _mm(dx3b, p['w_mlp_out'], tb=True, extra=(sv['u'],), outs=(_BF,),
             epi=lambda acc, u: (acc * 2.0 * jnp.maximum(u, 0.0),), name=f"mlp_out_dx_{l}")
    g['w_mlp_out'] = _mm(sv['a'], dx3b, ta=True, outs=(_BF,), name=f"mlp_out_dw_{l}")
    g['w_mlp_in'] = _mm(sv['h2'], du, ta=True, outs=(_BF,), name=f"mlp_in_dw_{l}")
    dh2 = _mm(du, p['w_mlp_in'], tb=True, name=f"mlp_in_dx_{l}")
    dx2, g['ln2_g'] = _rms_bwd(sv['x2'], dh2, p['ln2_g'], dx3, name=f"ln2_bwd_{l}")
    dx2b = dx2.astype(_BF)
    dmix = _mm(dx2b, p['w_out'], tb=True, name=f"out_proj_dx_{l}")
    g['w_out'] = _mm(sv['mix'], dx2b, ta=True, outs=(_BF,), name=f"out_proj_dw_{l}")

    dattn, g['attn_norm_g'] = _rms_bwd(sv['attn'], (dmix, _AW, 0), p['attn_norm_g'], name=f"attn_norm_bwd_{l}")
    parts = [_attn_bwd(sv['qkv'], dattn, sv['attn'], sv['lse'], dil, name=f"attn_bwd_d{dil}_{l}") for dil in _DILATIONS]
    def branch_sum(*t):
        return jnp.concatenate([t[i] + t[3 + i] + t[6 + i] for i in range(3)], axis=1)
    dqkv = _rows(branch_sum, [a for pr in parts for a in pr], [], [(3 * _AW, _BF)], tile=128, name=f"attn_bwd_sum_{l}")[0]

    xs, z, dt_raw = (sv['act'], _AW, 0), (sv['zdt'], _AW, 0), (sv['zdt'], _LANES, _AW // _LANES)

    def gate_bwd(y2, dy, y, xs, z, dsk, gn):
        dy2, dgn = _rms_bwd_tile(y2, dy, gn, _GROUPS)
        dy1 = dy2 * _silu(z)
        return dy1, dsk * dy1, dy2 * (y + dsk * xs) * _dsilu(z), dy1 * xs, dgn
    dy1, dxs_skip, dz, dsk_sum, gn_sum = _rows(
        gate_bwd, [sv['y2'], (dmix, _AW, 1), sv['y_ssd'], xs, z], [sv['dskip'], p['ssd_norm_g'].reshape(1, _AW)],
        [(_AW, F32), (_AW, F32), (_AW, _BF)], [_AW, _AW], tile=128, name=f"gate_bwd_{l}")
    g['ssd_norm_g'] = gn_sum.sum(axis=0)
    g['d_skip'] = dsk_sum.sum(axis=0).reshape(_HEADS, _HDIM).sum(axis=1)
    dxs, dbc, ddt, da = _ssd_bwd(sv['act'], sv['dt'], sv['acum'], sv['a_row'], sv['sall'], dy1, name=f"ssd_bwd_{l}")

    def dtb(da, ddtx, raw, dt, dz, bias, alog):
        a = -jnp.exp(alog)
        dda = _scan_rows(da, True)
        draw = (dda * a + ddtx) * _sigmoid(raw + bias)
        return jnp.concatenate([dz, draw.astype(dz.dtype)], axis=1), draw, dda * dt * a
    dzdt, dbias, dalog = _rows(dtb, [da, ddt, dt_raw, sv['dt'], dz], [_lanes(p['dt_bias']), _lanes(p['a_log'])],
                               [(_AW + _LANES, _BF)], [_LANES, _LANES], tile=_CHUNK, name=f"dt_bwd_{l}")
    g['dt_bias'], g['a_log'] = dbias.sum(axis=0)[:_HEADS], dalog.sum(axis=0)[:_HEADS]
    us = sv['us']

    def conv_bwd1(u0, u1, u2, u3, dxs, dbc, dxk, w, b):
        pre = w[0:1] * u3 + w[1:2] * u2 + w[2:3] * u1 + w[3:4] * u0 + b
        dp = jnp.concatenate([dxs + dxk, dbc], axis=1) * _dsilu(pre)
        return dp, dp * u3, dp * u2, dp * u1, dp * u0, dp
    dpre, *dws = _rows(conv_bwd1, [*us, dxs, dbc, dxs_skip], [p['conv_w'], p['conv_b'].reshape(1, cch)], [(cch, F32)], [cch] * 5,
                       tile=128, name=f"conv_bwd_pre_{l}")
    g['conv_w'] = jnp.stack([dws[i].sum(axis=0) for i in range(_CONV_K)])
    g['conv_b'] = dws[4].sum(axis=0)

    def conv_bwd2(p0, p1, p2, p3, w):
        return w[3:4] * p0 + w[2:3] * p1 + w[1:2] * p2 + w[0:1] * p3
    dxbc = _rows(conv_bwd2, [_shift(dpre, -j) for j in range(_CONV_K)], [p['conv_w']], [(cch, _BF)], tile=_tile_for(cch),
                 name=f"conv_bwd_in_{l}")[0]
    h1 = sv['h1']
    g_qkv = _mm(h1, dqkv, ta=True, outs=(_BF,), name=f"in_proj_qkv_dw_{l}")
    g_xbc = _mm(h1, dxbc, ta=True, outs=(_BF,), name=f"in_proj_xbc_dw_{l}")
    g_zdt = _mm(h1, dzdt, ta=True, outs=(_BF,), name=f"in_proj_zdt_dw_{l}")
    g['w_in'] = jnp.concatenate([g_qkv, g_zdt[:, :_AW], g_xbc, g_zdt[:, _AW:_AW + _HEADS]], axis=1)
    dh1 = _mm(dqkv, p['w_qkv'], tb=True, name=f"in_proj_qkv_dx_{l}")
    dh1 = _mm(dxbc, p['w_xbc'], tb=True, extra=(dh1,), epi=_add_to, name=f"in_proj_xbc_dx_{l}")
    dh1 = _mm(dzdt, p['w_zdt'], tb=True, extra=(dh1,), epi=_add_to, name=f"in_proj_zdt_dx_{l}")
    dx, g['ln1_g'] = _rms_bwd(sv['x'], dh1, p['ln1_g'], dx2, name=f"ln1_bwd_{l}")
    return dx, g


def _loss_bwd(x, g, tgt):
    w = x.shape[1]
    tile = _tile_for(w)

    def fn(x, tgt, g):
        r = _rstd(x)
        xh = x * r
        e = xh * g - tgt
        gd = e * (g / w)
        dx = r * (gd - xh * jnp.mean(gd * xh, axis=-1, keepdims=True))
        rowloss = 0.5 * jnp.mean(e * e, axis=-1, keepdims=True)
        return dx, (e / w) * xh, jnp.broadcast_to(rowloss, (tile, _LANES))
    dx, dg, ls = _rows(fn, [x, tgt], [g.reshape(1, w)], [(w, F32)], [w, _LANES], tile=tile, name="loss_head")
    return dx, dg.sum(axis=0), ls[:, 0].sum()


def _adamw(w, g, m, v, *, name):
    def fn(w, g, m, v):
        m2 = _B1 * m + (1.0 - _B1) * g
        v2 = _B2 * v + (1.0 - _B2) * jnp.square(g)
        m_hat = m2 / (1.0 - _B1 ** _STEP)
        v_hat = v2 / (1.0 - _B2 ** _STEP)
        return -_LR * (m_hat / (jnp.sqrt(v_hat) + _AEPS) + _WD * w), m2, v2
    width = w.shape[-1]
    flat = [a.reshape(-1, width) for a in (w, g, m, v)]
    tile = _pick(flat[0].shape[0], (_tile_for(width), 32, 8))
    res = _rows(fn, flat, [], [(width, F32)] * 3, tile=tile, name=name)
    return [r.reshape(w.shape) for r in res]


_HBM = pl.BlockSpec(memory_space=pltpu.HBM)


def _place():
    x, y, c = lax.axis_index("x"), lax.axis_index("y"), lax.axis_index("c")
    other_chips = [(1 - x, y), (x, 1 - y), (1 - x, 1 - y)]
    return x, y, c, other_chips


def _remote(src, dst, sems, i, dev):
    return pltpu.make_async_remote_copy(src_ref=src, dst_ref=dst, send_sem=sems[0].at[i], recv_sem=sems[1].at[i],
                                        device_id=dev, device_id_type=_MESH)


def _exchange8(v, *, reduce, name):
    r, w = v.shape

    def body(v_ref, all_ref, *rest):
        sems = rest[-2:]
        x, y, c, _ = _place()
        me = 4 * x + 2 * y + c
        all_ref[me] = v_ref[...]
        flips = [((d >> 2) & 1, (d >> 1) & 1, d & 1) for d in range(1, 8)]
        sends = [_remote(v_ref, all_ref.at[me], sems, i, (x ^ fx, y ^ fy, c ^ fc)) for i, (fx, fy, fc) in enumerate(flips)]
        for cp in sends:
            cp.start()
        for i, (fx, fy, fc) in enumerate(flips):
            _remote(v_ref, all_ref.at[me ^ (4 * fx + 2 * fy + fc)], sems, i, (x ^ fx, y ^ fy, c ^ fc)).wait_recv()
        for cp in sends:
            cp.wait_send()
        if reduce:
            acc = all_ref[0]
            for s in range(1, 8):
                acc = acc + all_ref[s]
            rest[0][...] = acc

    vm = pl.BlockSpec(memory_space=pltpu.VMEM)
    out_shape = [_SDS((8, r, w), v.dtype)] + ([_SDS((r, w), v.dtype)] if reduce else [])
    res = pl.pallas_call(body, in_specs=[vm], out_specs=[vm] * len(out_shape), out_shape=out_shape, name=name,
                         scratch_shapes=[pltpu.SemaphoreType.DMA((7,)), pltpu.SemaphoreType.DMA((7,))],
                         compiler_params=pltpu.CompilerParams(vmem_limit_bytes=int(32 << 20)))(v)
    return res[1] if reduce else res[0]


def _gather_weights(shards, *, name):
    n = len(shards)

    def body(*refs):
        ins, outs, sems = refs[:n], refs[n:2 * n], refs[2 * n:2 * n + 2]
        x, y, c, chips = _place()
        k = 2 * x + y
        sibling = (x, y, 1 - c)
        first = [_remote(ins[t].at[c], outs[t].at[k, c], sems, 6 * t + j, (px, py, c))
                 for t in range(n) for j, (px, py) in enumerate(chips)]
        for cp in first:
            cp.start()
        passed = []
        for t in range(n):
            for j, (px, py) in enumerate(chips):
                landed = outs[t].at[2 * px + py, c]
                _remote(ins[t].at[c], landed, sems, 6 * t + j, (px, py, c)).wait_recv()
                passed.append(_remote(landed, landed, sems, 6 * t + 3 + j, sibling))
                passed[-1].start()
        for t in range(n):
            for j, (px, py) in enumerate(chips):
                theirs = outs[t].at[2 * px + py, 1 - c]
                _remote(theirs, theirs, sems, 6 * t + 3 + j, sibling).wait_recv()
        for cp in first + passed:
            cp.wait_send()

    return pl.pallas_call(
        body, in_specs=[_HBM] * n, out_specs=[_HBM] * n, out_shape=[_SDS((4, *s.shape), s.dtype) for s in shards], name=name,
        scratch_shapes=[pltpu.SemaphoreType.DMA((6 * n,)), pltpu.SemaphoreType.DMA((6 * n,))])(*shards)


def _swap(name, srcs, out_shapes, plan, n_sends):
    n = len(srcs)

    def body(*refs):
        ins, outs, sems = refs[:n], refs[n:n + len(out_shapes)], refs[-2:]
        x, y, c, chips = _place()
        sends, landings = plan(x, y, c, chips, ins, outs)
        out = [_remote(s, d, sems, i, dev) for i, (s, d, dev) in enumerate(sends)]
        for cp in out:
            cp.start()
        for i, d in enumerate(landings):
            _remote(d, d, sems, i, sends[i][2]).wait_recv()
        for cp in out:
            cp.wait_send()

    return pl.pallas_call(
        body, in_specs=[_HBM] * n, out_specs=[_HBM] * len(out_shapes), out_shape=out_shapes, name=name,
        scratch_shapes=[pltpu.SemaphoreType.DMA((n_sends,)), pltpu.SemaphoreType.DMA((n_sends,))])(*srcs)


def _reduce_grads(grads, c, k):
    n = len(grads)
    half = [g.shape[1] // 2 for g in grads]

    def plan_a(x, y, c_, chips, ins, outs):
        sends = [(ins[t].at[kk, pl.ds((1 - c_) * half[t], half[t])], outs[t].at[kk], (x, y, 1 - c_))
                 for t in range(n) for kk in range(4)]
        return sends, [outs[t].at[kk] for t in range(n) for kk in range(4)]
    got = _swap("grad_swap_cores", grads, [_SDS((4, half[t], g.shape[2]), g.dtype) for t, g in enumerate(grads)], plan_a, 4 * n)
    part = []
    for t, g in enumerate(grads):
        b = g.shape[2]
        mine = lax.dynamic_slice_in_dim(g, c * half[t], half[t], axis=1)
        s = _rows(lambda p, q: p.astype(F32) + q.astype(F32), [mine.reshape(4 * half[t], b), got[t].reshape(4 * half[t], b)], [],
                  [(b, _BF)], tile=_pick(4 * half[t], (_tile_for(b), 32)), name=f"grad_sum_cores_{t}")[0]
        part.append(s.reshape(4, half[t], b))

    def plan_b(x, y, c_, chips, ins, outs):
        sends = [(ins[t].at[2 * px + py], outs[t].at[j], (px, py, c_)) for t in range(n) for j, (px, py) in enumerate(chips)]
        return sends, [outs[t].at[j] for t in range(n) for j in range(3)]
    got = _swap("grad_swap_chips", part, [_SDS((3, *p.shape[1:]), p.dtype) for p in part], plan_b, 3 * n)
    mine = []
    for t, p in enumerate(part):
        b = p.shape[2]
        own = lax.dynamic_index_in_dim(p, k, axis=0, keepdims=False)
        mine.append(_rows(lambda p0, p1, p2, p3: ((p0.astype(F32) + p1.astype(F32)) + p2.astype(F32)) + p3.astype(F32),
                          [own, got[t][0], got[t][1], got[t][2]], [], [(b, F32)], tile=_pick(half[t], (_tile_for(b), 32)),
                          name=f"grad_sum_chips_{t}")[0])

    def plan_c(x, y, c_, chips, ins, outs):
        return [(ins[t], outs[t], (x, y, 1 - c_)) for t in range(n)], [outs[t] for t in range(n)]
    theirs = _swap("grad_share_cores", mine, [_SDS(h.shape, F32) for h in mine], plan_c, n)
    return [jnp.where(c == 0, jnp.concatenate([a, b], axis=0), jnp.concatenate([b, a], axis=0)) for a, b in zip(mine, theirs)]


_BIG = ("w_in", "w_out", "w_mlp_in", "w_mlp_out")
_SMALL = ("ln1_g", "conv_b", "dt_bias", "a_log", "d_skip", "attn_norm_g", "ssd_norm_g", "ln2_g", "final_norm_g")
_ORDER = ("ln1_g", "w_in", "conv_w", "conv_b", "dt_bias", "a_log", "d_skip", "attn_norm_g", "ssd_norm_g", "w_out", "ln2_g",
          "w_mlp_in", "w_mlp_out", "final_norm_g")


def _pack(parts, rows):
    flat = jnp.concatenate([p.reshape(-1) for p in parts])
    return jnp.pad(flat, (0, rows * _LANES - flat.shape[0])).reshape(rows, _LANES)


def _unpack(buf, like):
    flat, out, o = buf.reshape(-1), [], 0
    for p in like:
        out.append(flat[o:o + p.size].reshape(p.shape))
        o += p.size
    return out


def kernel(x, ln1_g, w_in, conv_w, conv_b, dt_bias, a_log, d_skip, attn_norm_g, ssd_norm_g, w_out, ln2_g, w_mlp_in, w_mlp_out, final_norm_g, loss_target, m_ln1_g, m_w_in, m_conv_w, m_conv_b, m_dt_bias, m_a_log, m_d_skip, m_attn_norm_g, m_ssd_norm_g, m_w_out, m_ln2_g, m_w_mlp_in, m_w_mlp_out, m_final_norm_g, v_ln1_g, v_w_in, v_conv_w, v_conv_b, v_dt_bias, v_a_log, v_d_skip, v_attn_norm_g, v_ssd_norm_g, v_w_out, v_ln2_g, v_w_mlp_in, v_w_mlp_out, v_final_norm_g):
    w = dict(ln1_g=ln1_g, w_in=w_in, conv_w=conv_w, conv_b=conv_b, dt_bias=dt_bias, a_log=a_log, d_skip=d_skip,
             attn_norm_g=attn_norm_g, ssd_norm_g=ssd_norm_g, w_out=w_out, ln2_g=ln2_g, w_mlp_in=w_mlp_in, w_mlp_out=w_mlp_out,
             final_norm_g=final_norm_g)
    m = dict(ln1_g=m_ln1_g, w_in=m_w_in, conv_w=m_conv_w, conv_b=m_conv_b, dt_bias=m_dt_bias, a_log=m_a_log, d_skip=m_d_skip,
             attn_norm_g=m_attn_norm_g, ssd_norm_g=m_ssd_norm_g, w_out=m_w_out, ln2_g=m_ln2_g, w_mlp_in=m_w_mlp_in,
             w_mlp_out=m_w_mlp_out, final_norm_g=m_final_norm_g)
    v = dict(ln1_g=v_ln1_g, w_in=v_w_in, conv_w=v_conv_w, conv_b=v_conv_b, dt_bias=v_dt_bias, a_log=v_a_log, d_skip=v_d_skip,
             attn_norm_g=v_attn_norm_g, ssd_norm_g=v_ssd_norm_g, w_out=v_w_out, ln2_g=v_ln2_g, w_mlp_in=v_w_mlp_in,
             w_mlp_out=v_w_mlp_out, final_norm_g=v_final_norm_g)
    depth, d_model = ln1_g.shape
    n_chips = 4
    c = lax.axis_index("c")
    chip = 2 * lax.axis_index("x") + lax.axis_index("y")
    in_proj = w_in.shape[2] * n_chips
    cch = conv_w.shape[2] * n_chips
    zdt_pad = _LANES - _HEADS

    cw = _exchange8(conv_w.reshape(depth * _CONV_K, -1), reduce=False, name="gather_conv_w")[0::2]
    conv_full = cw.reshape(n_chips, depth, _CONV_K, -1).transpose(1, 2, 0, 3).reshape(depth, _CONV_K, cch)
    own = [w[n].astype(_BF) for n in _BIG]
    is_own = (jnp.arange(n_chips) == chip).reshape(n_chips, 1, 1, 1)
    g_in, g_out, g_mi, g_mo = (jnp.where(is_own, o[None], g) for o, g in zip(own, _gather_weights(own, name="gather_weights")))
    layers = []
    for l in range(depth):
        p = {n: w[n][l] for n in _SMALL[:-1]}
        p['conv_w'] = conv_full[l]
        full_in = g_in[:, l].transpose(1, 0, 2).reshape(d_model, in_proj)
        p['w_qkv'] = full_in[:, :3 * _AW]
        p['w_xbc'] = full_in[:, 4 * _AW:4 * _AW + cch]
        p['w_zdt'] = jnp.concatenate([full_in[:, 3 * _AW:4 * _AW], full_in[:, 4 * _AW + cch:], jnp.zeros((d_model, zdt_pad), _BF)], axis=1)
        p['w_out'] = g_out[:, l].reshape(-1, d_model)
        p['w_mlp_in'] = g_mi[:, l].transpose(1, 0, 2).reshape(d_model, -1)
        p['w_mlp_out'] = g_mo[:, l].reshape(-1, d_model)
        layers.append(p)

    h, saved = x[0], []
    for l, p in enumerate(layers):
        h, sv = _layer_fwd(h, p, l)
        saved.append(sv)
    dx, g_final, loss_part = _loss_bwd(h, final_norm_g, loss_target[0])
    grads = [None] * depth
    for l in reversed(range(depth)):
        dx, grads[l] = _layer_bwd(dx, layers[l], saved[l], l)

    def by_chip(g, name):
        if name in ("w_in", "w_mlp_in"):
            return g.reshape(d_model, n_chips, -1).transpose(1, 0, 2)
        return g.reshape(n_chips, -1, d_model)
    red_list = _reduce_grads([by_chip(grads[l][n], n) for n in _BIG for l in range(depth)], c, chip)
    red = {n: jnp.stack(red_list[depth * i:depth * (i + 1)]) for i, n in enumerate(_BIG)}

    small = {n: jnp.stack([grads[l][n] for l in range(depth)]) for n in _SMALL[:-1] + ("conv_w",)}
    small["final_norm_g"] = g_final
    parts = [loss_part.reshape(1)] + [small[n] for n in _SMALL + ("conv_w",)]
    rows = -(-sum(p.size for p in parts) // 1024) * 8
    tot = _unpack(_exchange8(_pack(parts, rows), reduce=True, name="allreduce_small"), parts)
    loss = tot[0][0]
    red.update(zip(_SMALL + ("conv_w",), tot[1:]))
    red["conv_w"] = lax.dynamic_index_in_dim(red["conv_w"].reshape(depth, _CONV_K, n_chips, -1), chip, axis=2, keepdims=False)

    delta, new_m, new_v = {}, {}, {}
    for n in _BIG:
        delta[n], new_m[n], new_v[n] = _adamw(w[n], red[n], m[n], v[n], name=f"adamw_{n}")
    names = _SMALL + ("conv_w",)
    like = [w[n] for n in names]
    srows = -(-sum(p.size for p in like) // 1024) * 8
    res = _adamw(*[_pack([d[n] for n in names], srows) for d in (w, red, m, v)], name="adamw_small")
    for dst, buf in zip((delta, new_m, new_v), res):
        dst.update(zip(names, _unpack(buf, like)))
    return (loss, dx[None], *[red[n] for n in _ORDER], *[delta[n] for n in _ORDER], *[new_m[n] for n in _ORDER],
            *[new_v[n] for n in _ORDER])
```

```python
import numpy as np
import jax
import jax.numpy as jnp
from jax import lax
from jax.experimental import pallas as pl
from jax.experimental.pallas import tpu as pltpu

F32 = jnp.float32
_BF = jnp.bfloat16
_NEG = -1e30
_EPS = 1e-5
_HEADS = 16
_HDIM = 64
_AW = _HEADS * _HDIM
_ABLK = 128
_DILATIONS = (1, 4, 16)
_CHUNK = 128
_NSTATE = 128
_GROUPS = 2
_HPG = _HEADS // _GROUPS
_CONV_K = 4
_LANES = 128
_LR, _B1, _B2, _AEPS, _WD, _STEP = 0.001, 0.9, 0.999, 1e-08, 0.01, 10
_VMEM_CAP = 56 * 1024 * 1024
_MESH = pl.DeviceIdType.MESH
_SDS = jax.ShapeDtypeStruct
_NT = (((1,), (1,)), ((), ()))
_TN = (((0,), (0,)), ((), ()))


def _params(sem, est_bytes):
    lim = int(min(max(2 * est_bytes + (4 << 20), 16 << 20), _VMEM_CAP))
    return pltpu.CompilerParams(dimension_semantics=sem, vmem_limit_bytes=lim)


def _nbytes(shape, dtype):
    return int(np.prod(shape)) * jnp.dtype(dtype).itemsize


def _dot(a, b, dims=(((1,), (0,)), ((), ()))):
    return lax.dot_general(a.astype(_BF), b.astype(_BF), dims, preferred_element_type=F32)


def _rows(fn, ins, consts, outs, sums=(), *, tile, name):
    ins = [a if isinstance(a, tuple) else (a, a.shape[1], 0) for a in ins]
    rows = ins[0][0].shape[0]
    n_in, n_c, n_o, n_s = len(ins), len(consts), len(outs), len(sums)

    def body(*refs):
        vals = [r[...] for r in refs[:n_in + n_c]]
        res = fn(*vals)
        res = res if isinstance(res, tuple) else (res,)
        orefs = refs[n_in + n_c:n_in + n_c + n_o]
        srefs = refs[n_in + n_c + n_o:]
        for r, v in zip(orefs, res[:n_o]):
            r[...] = v.astype(r.dtype)
        if n_s:
            @pl.when(pl.program_id(0) == 0)
            def _():
                for r in srefs:
                    r[...] = jnp.zeros_like(r)
            for r, v in zip(srefs, res[n_o:]):
                r[...] += v.reshape(tile // 8, 8, v.shape[-1]).sum(axis=0)

    in_specs = [pl.BlockSpec((tile, w), lambda i, j=j: (i, j)) for _, w, j in ins]
    in_specs += [pl.BlockSpec(c.shape, lambda i, nd=c.ndim: (0,) * nd) for c in consts]
    out_shape = [_SDS((rows, w), dt) for w, dt in outs] + [_SDS((8, w), F32) for w in sums]
    out_specs = [pl.BlockSpec((tile, w), lambda i: (i, 0)) for w, _ in outs]
    out_specs += [pl.BlockSpec((8, w), lambda i: (0, 0)) for w in sums]
    est = sum(_nbytes((tile, w), a.dtype) for a, w, _ in ins) + sum(_nbytes((tile, w), dt) for w, dt in outs)
    return pl.pallas_call(body, grid=(rows // tile,), in_specs=in_specs, out_specs=out_specs, out_shape=out_shape,
                          name=name, compiler_params=_params(("arbitrary",), 3 * est))(*[a for a, _, _ in ins], *consts)


def _tile_for(width):
    return max(c for c in (256, 128, 64, 32) if c * width <= (1 << 18) or c == 32)


def _rstd(x):
    return lax.rsqrt(jnp.mean(x * x, axis=-1, keepdims=True) + _EPS)


def _split(x, groups):
    w = x.shape[-1] // groups
    return [x[:, g * w:(g + 1) * w] for g in range(groups)]


def _cat(parts):
    return parts[0] if len(parts) == 1 else jnp.concatenate(parts, axis=-1)


def _rms_bwd_tile(x, dy, g, groups):
    dxs, dgs = [], []
    for xs, ds, gs in zip(_split(x, groups), _split(dy.astype(F32), groups), _split(g, groups)):
        r = _rstd(xs)
        xh = xs * r
        gd = ds * gs
        dxs.append(r * (gd - xh * jnp.mean(gd * xh, axis=-1, keepdims=True)))
        dgs.append(ds * xh)
    return _cat(dxs), _cat(dgs)


def _rms_fwd(x, g, *, groups=1, name):
    def fn(x, g):
        return _cat([xs * _rstd(xs) * gs for xs, gs in zip(_split(x, groups), _split(g, groups))])
    w = x.shape[1]
    return _rows(fn, [x], [g.reshape(1, w)], [(w, _BF)], tile=_tile_for(w), name=name)[0]


def _rms_bwd(x, dy, g, res=None, *, name):
    def fn(x, dy, *rest):
        dx, dg = _rms_bwd_tile(x, dy, rest[-1], 1)
        return (dx + rest[0] if res is not None else dx), dg
    w = x.shape[1]
    ins = [x, dy] + ([res] if res is not None else [])
    dx, dg = _rows(fn, ins, [g.reshape(1, w)], [(w, F32)], [w], tile=_tile_for(w), name=name)
    return dx, dg.sum(axis=0)


def _pick(n, cands):
    for c in cands:
        if n % c == 0:
            return c
    raise ValueError(f"no block size for {n}")


_MM_BLOCKS = (1024, 640, 512, 384)


def _mm(a, b, *, ta=False, tb=False, extra=(), epi=None, outs=(F32,), after=None, name):
    m, k = (a.shape[1], a.shape[0]) if ta else a.shape
    n = b.shape[0] if tb else b.shape[1]
    assert k == (b.shape[1] if tb else b.shape[0])
    bm, bn, bk = _pick(m, _MM_BLOCKS), _pick(n, _MM_BLOCKS), _pick(k, _MM_BLOCKS)
    nk = k // bk
    n_e, n_o = len(extra), len(outs)
    behind = [] if after is None else [after]
    dims = (((0 if ta else 1,), (1 if tb else 0,)), ((), ()))

    def body(a_ref, b_ref, *rest):
        ex, orefs, acc = rest[:n_e], rest[n_e + len(behind):n_e + len(behind) + n_o], rest[-1]
        kk = pl.program_id(2)

        @pl.when(kk == 0)
        def _():
            acc[...] = jnp.zeros_like(acc)

        acc[...] += _dot(a_ref[...], b_ref[...], dims)

        @pl.when(kk == nk - 1)
        def _():
            r = acc[...]
            res = epi(r, *[e[...] for e in ex]) if epi is not None else (r,)
            for o, v in zip(orefs, res):
                o[...] = v.astype(o.dtype)

    a_spec = pl.BlockSpec((bk, bm), lambda i, j, kk: (kk, i)) if ta else pl.BlockSpec((bm, bk), lambda i, j, kk: (i, kk))
    b_spec = pl.BlockSpec((bn, bk), lambda i, j, kk: (j, kk)) if tb else pl.BlockSpec((bk, bn), lambda i, j, kk: (kk, j))
    t_spec = pl.BlockSpec((bm, bn), lambda i, j, kk: (i, j))
    est = (_nbytes((bm, bk), a.dtype) + _nbytes((bk, bn), b.dtype) + sum(_nbytes((bm, bn), e.dtype) for e in extra)
           + sum(_nbytes((bm, bn), o) for o in outs)) * 2 + 2 * _nbytes((bm, bn), F32)
    res = pl.pallas_call(
        body, grid=(m // bm, n // bn, nk), in_specs=[a_spec, b_spec] + [t_spec] * n_e + [pl.BlockSpec(memory_space=pl.ANY)] * len(behind),
        out_specs=[t_spec] * n_o, out_shape=[_SDS((m, n), o) for o in outs], scratch_shapes=[pltpu.VMEM((bm, bn), F32)], name=name,
        compiler_params=_params(("parallel", "parallel", "arbitrary"), est))(a, b, *extra, *behind)
    return res[0] if n_o == 1 else res


def _add_to(acc, r):
    return (acc + r,)


def _alibi_bias(dilation):
    slopes = 2.0 ** (-8.0 * (np.arange(_HEADS) + 1) / _HEADS)
    i = np.arange(_ABLK)[:, None]
    j = np.arange(_ABLK)[None, :]
    cur = np.where(i - j >= 0, -slopes[:, None, None] * ((i - j) * dilation), _NEG)
    prev = np.where(j >= i, -slopes[:, None, None] * ((i - j + _ABLK) * dilation), _NEG)
    return jnp.asarray(cur, F32), jnp.asarray(prev, F32)


def _strided(a, d):
    return a.reshape(a.shape[0] // d, d * a.shape[1])


def _head(h):
    return slice(h * _HDIM, (h + 1) * _HDIM)


def _attn_specs(nb, n_parts):
    def cur(p):
        return pl.BlockSpec((_ABLK, _AW), lambda r, j: (jnp.minimum(j, nb - 1), r * n_parts + p))

    def prev(p):
        return pl.BlockSpec((_ABLK, _AW), lambda r, j: (jnp.clip(j - 1, 0, nb - 1), r * n_parts + p))
    return cur, prev


def _attn_fwd(qkv, dilation, *, name):
    t = qkv.shape[0]
    nb = t // dilation // _ABLK
    bc, bp = _alibi_bias(dilation)
    scale = _HDIM ** -0.5

    def body(q_ref, kc_ref, kp_ref, vc_ref, vp_ref, bc_ref, bp_ref, o_ref, l_ref):
        first = pl.program_id(1) == 0
        for h in range(_HEADS):
            sl = _head(h)
            q = q_ref[:, sl]
            sc = _dot(q, kc_ref[:, sl], _NT) * scale + bc_ref[h]
            sp = jnp.where(first, _NEG, _dot(q, kp_ref[:, sl], _NT) * scale + bp_ref[h])
            m = jnp.maximum(jnp.max(sc, axis=-1, keepdims=True), jnp.max(sp, axis=-1, keepdims=True))
            pc, pp = jnp.exp(sc - m), jnp.exp(sp - m)
            den = jnp.sum(pc, axis=-1, keepdims=True) + jnp.sum(pp, axis=-1, keepdims=True)
            o_ref[:, sl] = (_dot(pc, vc_ref[:, sl]) + _dot(pp, vp_ref[:, sl])) / den
            l_ref[:, sl] = jnp.broadcast_to(m + jnp.log(den), (_ABLK, _HDIM))

    cur, prev = _attn_specs(nb, 3)
    cur1, _ = _attn_specs(nb, 1)
    bias = pl.BlockSpec((_HEADS, _ABLK, _ABLK), lambda r, j: (0, 0, 0))
    sv = _strided(qkv, dilation)
    out =_SDS((t // dilation, dilation * _AW), F32)
    o, l = pl.pallas_call(
        body, grid=(dilation, nb), in_specs=[cur(0), cur(1), prev(1), cur(2), prev(2), bias, bias],
        out_specs=[cur1(0), cur1(0)], out_shape=[out, out], name=name,
        compiler_params=_params(("parallel", "arbitrary"), 16 << 20))(sv, sv, sv, sv, sv, bc, bp)
    return o.reshape(t, _AW), l.reshape(t, _AW)


def _attn_bwd(qkv, do, o, lse, dilation, *, name):
    t = qkv.shape[0]
    nb = t // dilation // _ABLK
    bc, bp = _alibi_bias(dilation)
    scale = _HDIM ** -0.5

    def body(q_ref, kc_ref, kp_ref, vc_ref, vp_ref, do_ref, o_ref, l_ref, bc_ref, bp_ref, dq_ref, dk_ref, dv_ref, ck, cv):
        n = pl.program_id(1)

        @pl.when(n == 0)
        def _():
            ck[...] = jnp.zeros_like(ck)
            cv[...] = jnp.zeros_like(cv)

        @pl.when(n < nb)
        def _():
            for h in range(_HEADS):
                sl = _head(h)
                q, kc, kp, vc, vp = q_ref[:, sl], kc_ref[:, sl], kp_ref[:, sl], vc_ref[:, sl], vp_ref[:, sl]
                dov = do_ref[:, sl]
                dsum = jnp.sum(dov * o_ref[:, sl], axis=-1, keepdims=True)
                lrow = jnp.max(l_ref[:, sl], axis=-1, keepdims=True)
                pc = jnp.exp(_dot(q, kc, _NT) * scale + bc_ref[h] - lrow)
                pp = jnp.exp(jnp.where(n == 0, _NEG, _dot(q, kp, _NT) * scale + bp_ref[h]) - lrow)
                dsc = pc * (_dot(dov, vc, _NT) - dsum)
                dsp = pp * (_dot(dov, vp, _NT) - dsum)
                dq_ref[:, sl] = (_dot(dsc, kc) + _dot(dsp, kp)) * scale
                dk_ref[:, sl] = ck[:, sl] + _dot(dsp, q, _TN) * scale
                dv_ref[:, sl] = cv[:, sl] + _dot(pp, dov, _TN)
                ck[:, sl] = _dot(dsc, q, _TN) * scale
                cv[:, sl] = _dot(pc, dov, _TN)

        @pl.when(n == nb)
        def _():
            dk_ref[...] = ck[...]
            dv_ref[...] = cv[...]

    cur, prev = _attn_specs(nb, 3)
    cur1, prev1 = _attn_specs(nb, 1)
    bias = pl.BlockSpec((_HEADS, _ABLK, _ABLK), lambda r, j: (0, 0, 0))
    sv = _strided(qkv, dilation)
    s1 = [_strided(a, dilation) for a in (do, o, lse)]
    dqkv = pl.pallas_call(
        body, grid=(dilation, nb + 1),
        in_specs=[cur(0), cur(1), prev(1), cur(2), prev(2), cur1(0), cur1(0), cur1(0), bias, bias],
        out_specs=[cur1(0), prev1(0), prev1(0)], out_shape=[_SDS(s1[0].shape, F32)] * 3, name=name,
        scratch_shapes=[pltpu.VMEM((_ABLK, _AW), F32)] * 2,
        compiler_params=_params(("parallel", "arbitrary"), 16 << 20))(sv, sv, sv, sv, sv, *s1, bc, bp)
    return [a.reshape(t, _AW) for a in dqkv]


def _ssd_in_specs(ch):
    return dict(
        xs=pl.BlockSpec((_CHUNK, _AW), lambda c: (ch(c), 0)),
        bc=pl.BlockSpec((_CHUNK, 2 * _GROUPS * _NSTATE), lambda c: (ch(c), _AW // (2 * _GROUPS * _NSTATE))),
        lane=pl.BlockSpec((_CHUNK, _LANES), lambda c: (ch(c), 0)),
        arow=pl.BlockSpec((_HEADS, 1, _CHUNK), lambda c: (0, 0, ch(c))),
        st=pl.BlockSpec((1, _HEADS, _NSTATE, _HDIM), lambda c: (ch(c), 0, 0, 0)),
    )


def _decay(a_col, a_row):
    i0 = lax.broadcasted_iota(jnp.int32, (_CHUNK, _CHUNK), 0)
    i1 = lax.broadcasted_iota(jnp.int32, (_CHUNK, _CHUNK), 1)
    return jnp.where(i0 >= i1, jnp.exp(a_col - a_row), 0.0), jnp.where(i1 >= i0, jnp.exp(a_row - a_col), 0.0)


def _ssd_fwd(act, dt, acum, a_row, *, name):
    t = act.shape[0]
    nc = t // _CHUNK
    sp = _ssd_in_specs(lambda c: c)
    gw = _GROUPS * _NSTATE

    def body(xs_ref, bc_ref, dt_ref, ac_ref, ar_ref, y_ref, sall_ref, st):
        @pl.when(pl.program_id(0) == 0)
        def _():
            st[...] = jnp.zeros_like(st)

        for g in range(_GROUPS):
            bg = bc_ref[:, g * _NSTATE:(g + 1) * _NSTATE]
            cg = bc_ref[:, gw + g * _NSTATE:gw + (g + 1) * _NSTATE].astype(_BF)
            cb = _dot(cg, bg, _NT)
            for h in range(g * _HPG, (g + 1) * _HPG):
                a_col = ac_ref[:, h:h + 1]
                x = (xs_ref[:, _head(h)] * dt_ref[:, h:h + 1]).astype(_BF)
                lm, _ = _decay(a_col, ar_ref[h])
                sv = st[h]
                sall_ref[0, h] = sv
                y_ref[:, _head(h)] = _dot(cb * lm, x) + jnp.exp(a_col) * _dot(cg, sv)
                a_last = jnp.min(a_col, axis=0, keepdims=True)
                st[h] = jnp.exp(a_last) * sv + _dot(bg * jnp.exp(a_last - a_col), x, _TN)

    return pl.pallas_call(
        body, grid=(nc,), in_specs=[sp['xs'], sp['bc'], sp['lane'], sp['lane'], sp['arow']],
        out_specs=[sp['xs'], sp['st']], out_shape=[_SDS((t, _AW), F32), _SDS((nc, _HEADS, _NSTATE, _HDIM), F32)],
        scratch_shapes=[pltpu.VMEM((_HEADS, _NSTATE, _HDIM), F32)], name=name,
        compiler_params=_params(("arbitrary",), 16 << 20))(act, act, dt, acum, a_row)


def _ssd_bwd(act, dt, acum, a_row, sall, dy, *, name):
    t = act.shape[0]
    nc = t // _CHUNK
    sp = _ssd_in_specs(lambda c: nc - 1 - c)
    gw = _GROUPS * _NSTATE

    def rsum(v):
        return jnp.sum(v, axis=-1, keepdims=True)

    def body(xs_ref, bc_ref, dt_ref, ac_ref, ar_ref, sall_ref, dy_ref, dxs_ref, dbc_ref, ddt_ref, da_ref, dst):
        @pl.when(pl.program_id(0) == 0)
        def _():
            dst[...] = jnp.zeros_like(dst)

        ddt_ref[...] = jnp.zeros_like(ddt_ref)
        da_ref[...] = jnp.zeros_like(da_ref)
        row = lax.broadcasted_iota(jnp.int32, (_CHUNK, 1), 0)
        for g in range(_GROUPS):
            bg = bc_ref[:, g * _NSTATE:(g + 1) * _NSTATE]
            bgb = bg.astype(_BF)
            cg = bc_ref[:, gw + g * _NSTATE:gw + (g + 1) * _NSTATE].astype(_BF)
            cb, cbt = _dot(cg, bgb, _NT), _dot(bgb, cg, _NT)
            dcb = jnp.zeros((_CHUNK, _CHUNK), F32)
            dbg = jnp.zeros((_CHUNK, _NSTATE), F32)
            dcg = jnp.zeros((_CHUNK, _NSTATE), F32)
            for h in range(g * _HPG, (g + 1) * _HPG):
                a_col, dt_col = ac_ref[:, h:h + 1], dt_ref[:, h:h + 1]
                xsv = xs_ref[:, _head(h)]
                x = xsv * dt_col
                xb = x.astype(_BF)
                dyv = dy_ref[:, _head(h)]
                dyb = dyv.astype(_BF)
                lm, lmt = _decay(a_col, ar_ref[h])
                gm, gmt = cb * lm, cbt * lmt
                sv, dsv = sall_ref[0, h], dst[h]
                svb, dsb = sv.astype(_BF), dsv.astype(_BF)
                e_col = jnp.exp(a_col)
                a_last = jnp.min(a_col, axis=0, keepdims=True)
                w_col = jnp.exp(a_last - a_col)
                bwds = _dot(bg * w_col, dsb)
                dx = _dot(gm, dyb, _TN) + bwds
                dg, dgt = _dot(dyb, xb, _NT), _dot(xb, dyb, _NT)
                dcb = dcb + dg * lm
                edy = (e_col * dyv).astype(_BF)
                dcg = dcg + _dot(edy, svb, _NT)
                dbg = dbg + w_col * _dot(xb, dsb, _NT)
                z = rsum(x * bwds)
                da = rsum(dg * gm) - rsum(dgt * gmt) + rsum(dyv * (e_col * _dot(cg, svb))) - z
                da_last = jnp.sum(z, axis=0, keepdims=True) + jnp.exp(a_last) * jnp.sum(rsum(sv * dsv), axis=0, keepdims=True)
                da_ref[:, h:h + 1] = da + jnp.where(row == _CHUNK - 1, da_last, 0.0)
                ddt_ref[:, h:h + 1] = rsum(dx * xsv)
                dxs_ref[:, _head(h)] = dx * dt_col
                dst[h] = jnp.exp(a_last) * dsv + _dot(cg, edy, _TN)
            dbc_ref[:, g * _NSTATE:(g + 1) * _NSTATE] = dbg + _dot(dcb, cg, _TN)
            dbc_ref[:, gw + g * _NSTATE:gw + (g + 1) * _NSTATE] = dcg + _dot(dcb, bgb)

    ch = lambda c: nc - 1 - c
    wide = pl.BlockSpec((_CHUNK, 2 * gw), lambda c: (ch(c), 0))
    return pl.pallas_call(
        body, grid=(nc,), in_specs=[sp['xs'], sp['bc'], sp['lane'], sp['lane'], sp['arow'], sp['st'], sp['xs']],
        out_specs=[sp['xs'], wide, sp['lane'], sp['lane']],
        out_shape=[_SDS((t, _AW), F32), _SDS((t, 2 * gw), F32), _SDS((t, _LANES), F32), _SDS((t, _LANES), F32)],
        scratch_shapes=[pltpu.VMEM((_HEADS, _NSTATE, _HDIM), F32)], name=name,
        compiler_params=_params(("arbitrary",), 16 << 20))(act, act, dt, acum, a_row, sall, dy)


def _scan_rows(v, reverse):
    r = lax.broadcasted_iota(jnp.int32, v.shape, 0)
    for s in (1, 2, 4, 8, 16, 32, 64):
        if reverse:
            v = v + jnp.where(r < _CHUNK - s, pltpu.roll(v, _CHUNK - s, 0), 0.0)
        else:
            v = v + jnp.where(r >= s, pltpu.roll(v, s, 0), 0.0)
    return v


def _softplus(x):
    return jnp.maximum(x, 0.0) + jnp.log(1.0 + jnp.exp(-jnp.abs(x)))


def _sigmoid(x):
    return 1.0 / (1.0 + jnp.exp(-x))


def _silu(x):
    return x * _sigmoid(x)


def _dsilu(x):
    s = _sigmoid(x)
    return s * (1.0 + x * (1.0 - s))


def _shift(a, j):
    if j == 0:
        return a
    if j > 0:
        return jnp.pad(a, ((j, 0), (0, 0)))[:-j]
    return jnp.pad(a, ((0, -j), (0, 0)))[-j:]


def _lanes(a):
    return jnp.pad(a, (0, _LANES - a.shape[0])).reshape(1, _LANES)


def _layer_fwd(x, p, l):
    cch = p['conv_w'].shape[1]
    sv = {}
    h1 = _rms_fwd(x, p['ln1_g'], name=f"ln1_fwd_{l}")
    qkv = _mm(h1, p['w_qkv'], outs=(_BF,), name=f"in_proj_qkv_{l}")
    xbc = _mm(h1, p['w_xbc'], name=f"in_proj_xbc_{l}")
    zdt = _mm(h1, p['w_zdt'], name=f"in_proj_zdt_{l}")
    z, dt_raw = (zdt, _AW, 0), (zdt, _LANES, _AW // _LANES)

    outs = []
    for dil in _DILATIONS:
        outs += _attn_fwd(qkv, dil, name=f"attn_fwd_d{dil}_{l}")

    def combine(o1, l1, o2, l2, o3, l3):
        m = jnp.maximum(jnp.maximum(l1, l2), l3)
        e1, e2, e3 = jnp.exp(l1 - m), jnp.exp(l2 - m), jnp.exp(l3 - m)
        tot = e1 + e2 + e3
        return (e1 * o1 + e2 * o2 + e3 * o3) / tot, m + jnp.log(tot)
    attn, lse = _rows(combine, outs, [], [(_AW, F32), (_AW, F32)], tile=_tile_for(_AW), name=f"attn_combine_{l}")
    attn_n = _rms_fwd(attn, p['attn_norm_g'], name=f"attn_norm_fwd_{l}")

    us = [_shift(xbc, j) for j in range(_CONV_K)]

    def conv(u0, u1, u2, u3, w, b):
        return _silu(w[0:1] * u3 + w[1:2] * u2 + w[2:3] * u1 + w[3:4] * u0 + b)
    act = _rows(conv, us, [p['conv_w'], p['conv_b'].reshape(1, cch)], [(cch, F32)], tile=_tile_for(cch), name=f"conv_fwd_{l}")[0]

    def dtf(raw, bias, alog):
        dt = _softplus(raw + bias)
        return dt, _scan_rows(dt * -jnp.exp(alog), False)
    dt, acum = _rows(dtf, [dt_raw], [_lanes(p['dt_bias']), _lanes(p['a_log'])], [(_LANES, F32), (_LANES, F32)],
                     tile=_CHUNK, name=f"dt_fwd_{l}")
    a_row = acum[:, :_HEADS].T[:, None, :]
    y_ssd, sall = _ssd_fwd(act, dt, acum, a_row, name=f"ssd_fwd_{l}")
    dskip = jnp.repeat(p['d_skip'], _HDIM).reshape(1, _AW)
    xs = (act, _AW, 0)

    def gate(y, xs, z, dsk):
        return (y + dsk * xs) * _silu(z)
    y2 = _rows(gate, [y_ssd, xs, z], [dskip], [(_AW, F32)], tile=_tile_for(_AW), name=f"gate_fwd_{l}")[0]
    y_n = _rms_fwd(y2, p['ssd_norm_g'], groups=_GROUPS, name=f"ssd_norm_fwd_{l}")

    mix = jnp.concatenate([attn_n, y_n], axis=1)
    sv.update(x=x, h1=h1, qkv=qkv, zdt=zdt, us=us, attn=attn, lse=lse, act=act, dt=dt, acum=acum, a_row=a_row,
              sall=sall, y_ssd=y_ssd, dskip=dskip, y2=y2, mix=mix)
    return mix, sv


def _layer_fwd_mlp(p, sv, l):
    x2 = _mm(sv['mix'], p['w_out'], extra=(sv['x'],), epi=_add_to, name=f"out_proj_{l}")
    h2 = _rms_fwd(x2, p['ln2_g'], name=f"ln2_fwd_{l}")
    u, a = _mm(h2, p['w_mlp_in'], epi=lambda acc: (acc, jnp.square(jnp.maximum(acc, 0.0))), outs=(F32, _BF), name=f"mlp_in_{l}")
    x3 = _mm(a, p['w_mlp_out'], extra=(x2,), epi=_add_to, name=f"mlp_out_{l}")
    sv.update(x2=x2, h2=h2, u=u, a=a)
    return x3


def _layer_bwd(dx3, p, sv, l, send, after):
    cch = p['conv_w'].shape[1]
    g = {}
    dx3b = dx3.astype(_BF)
    du = _mm(dx3b, p['w_mlp_out'], tb=True, extra=(sv['u'],), outs=(_BF,), after=after,
             epi=lambda acc, u: (acc * 2.0 * jnp.maximum(u, 0.0),), name=f"mlp_out_dx_{l}")
    g['w_mlp_out'] = _mm(sv['a'], dx3b, ta=True, outs=(_BF,), name=f"mlp_out_dw_{l}")
    g['w_mlp_in'] = _mm(sv['h2'], du, ta=True, outs=(_BF,), name=f"mlp_in_dw_{l}")
    sent = send(('w_mlp_out', 'w_mlp_in'), g)
    dh2 = _mm(du, p['w_mlp_in'], tb=True, after=sent, name=f"mlp_in_dx_{l}")
    dx2, g['ln2_g'] = _rms_bwd(sv['x2'], dh2, p['ln2_g'], dx3, name=f"ln2_bwd_{l}")
    dx2b = dx2.astype(_BF)
    dmix = _mm(dx2b, p['w_out'], tb=True, name=f"out_proj_dx_{l}")
    g['w_out'] = _mm(sv['mix'], dx2b, ta=True, outs=(_BF,), name=f"out_proj_dw_{l}")

    dattn, g['attn_norm_g'] = _rms_bwd(sv['attn'], (dmix, _AW, 0), p['attn_norm_g'], name=f"attn_norm_bwd_{l}")
    parts = [_attn_bwd(sv['qkv'], dattn, sv['attn'], sv['lse'], dil, name=f"attn_bwd_d{dil}_{l}") for dil in _DILATIONS]
    def branch_sum(*t):
        return jnp.concatenate([t[i] + t[3 + i] + t[6 + i] for i in range(3)], axis=1)
    dqkv = _rows(branch_sum, [a for pr in parts for a in pr], [], [(3 * _AW, _BF)], tile=128, name=f"attn_bwd_sum_{l}")[0]

    xs, z, dt_raw = (sv['act'], _AW, 0), (sv['zdt'], _AW, 0), (sv['zdt'], _LANES, _AW // _LANES)

    def gate_bwd(y2, dy, y, xs, z, dsk, gn):
        dy2, dgn = _rms_bwd_tile(y2, dy, gn, _GROUPS)
        dy1 = dy2 * _silu(z)
        return dy1, dsk * dy1, dy2 * (y + dsk * xs) * _dsilu(z), dy1 * xs, dgn
    dy1, dxs_skip, dz, dsk_sum, gn_sum = _rows(
        gate_bwd, [sv['y2'], (dmix, _AW, 1), sv['y_ssd'], xs, z], [sv['dskip'], p['ssd_norm_g'].reshape(1, _AW)],
        [(_AW, F32), (_AW, F32), (_AW, _BF)], [_AW, _AW], tile=128, name=f"gate_bwd_{l}")
    g['ssd_norm_g'] = gn_sum.sum(axis=0)
    g['d_skip'] = dsk_sum.sum(axis=0).reshape(_HEADS, _HDIM).sum(axis=1)
    dxs, dbc, ddt, da = _ssd_bwd(sv['act'], sv['dt'], sv['acum'], sv['a_row'], sv['sall'], dy1, name=f"ssd_bwd_{l}")

    def dtb(da, ddtx, raw, dt, dz, bias, alog):
        a = -jnp.exp(alog)
        dda = _scan_rows(da, True)
        draw = (dda * a + ddtx) * _sigmoid(raw + bias)
        return jnp.concatenate([dz, draw.astype(dz.dtype)], axis=1), draw, dda * dt * a
    dzdt, dbias, dalog = _rows(dtb, [da, ddt, dt_raw, sv['dt'], dz], [_lanes(p['dt_bias']), _lanes(p['a_log'])],
                               [(_AW + _LANES, _BF)], [_LANES, _LANES], tile=_CHUNK, name=f"dt_bwd_{l}")
    g['dt_bias'], g['a_log'] = dbias.sum(axis=0)[:_HEADS], dalog.sum(axis=0)[:_HEADS]
    us = sv['us']

    def conv_bwd1(u0, u1, u2, u3, dxs, dbc, dxk, w, b):
        pre = w[0:1] * u3 + w[1:2] * u2 + w[2:3] * u1 + w[3:4] * u0 + b
        dp = jnp.concatenate([dxs + dxk, dbc], axis=1) * _dsilu(pre)
        return dp, dp * u3, dp * u2, dp * u1, dp * u0, dp
    dpre, *dws = _rows(conv_bwd1, [*us, dxs, dbc, dxs_skip], [p['conv_w'], p['conv_b'].reshape(1, cch)], [(cch, F32)], [cch] * 5,
                       tile=128, name=f"conv_bwd_pre_{l}")
    g['conv_w'] = jnp.stack([dws[i].sum(axis=0) for i in range(_CONV_K)])
    g['conv_b'] = dws[4].sum(axis=0)

    def conv_bwd2(p0, p1, p2, p3, w):
        return w[3:4] * p0 + w[2:3] * p1 + w[1:2] * p2 + w[0:1] * p3
    dxbc = _rows(conv_bwd2, [_shift(dpre, -j) for j in range(_CONV_K)], [p['conv_w']], [(cch, _BF)], tile=_tile_for(cch),
                 name=f"conv_bwd_in_{l}")[0]
    h1 = sv['h1']
    g_qkv = _mm(h1, dqkv, ta=True, outs=(_BF,), name=f"in_proj_qkv_dw_{l}")
    g_xbc = _mm(h1, dxbc, ta=True, outs=(_BF,), name=f"in_proj_xbc_dw_{l}")
    g_zdt = _mm(h1, dzdt, ta=True, outs=(_BF,), name=f"in_proj_zdt_dw_{l}")
    g['w_in'] = jnp.concatenate([g_qkv, g_zdt[:, :_AW], g_xbc, g_zdt[:, _AW:_AW + _HEADS]], axis=1)
    sent = send(('w_out', 'w_in'), g)
    for n in _BIG:
        del g[n]
    dh1 = _mm(dqkv, p['w_qkv'], tb=True, after=sent, name=f"in_proj_qkv_dx_{l}")
    dh1 = _mm(dxbc, p['w_xbc'], tb=True, extra=(dh1,), epi=_add_to, name=f"in_proj_xbc_dx_{l}")
    dh1 = _mm(dzdt, p['w_zdt'], tb=True, extra=(dh1,), epi=_add_to, name=f"in_proj_zdt_dx_{l}")
    dx, g['ln1_g'] = _rms_bwd(sv['x'], dh1, p['ln1_g'], dx2, name=f"ln1_bwd_{l}")
    return dx, g


def _loss_bwd(x, g, tgt):
    w = x.shape[1]
    tile = _tile_for(w)

    def fn(x, tgt, g):
        r = _rstd(x)
        xh = x * r
        e = xh * g - tgt
        gd = e * (g / w)
        dx = r * (gd - xh * jnp.mean(gd * xh, axis=-1, keepdims=True))
        rowloss = 0.5 * jnp.mean(e * e, axis=-1, keepdims=True)
        return dx, (e / w) * xh, jnp.broadcast_to(rowloss, (tile, _LANES))
    dx, dg, ls = _rows(fn, [x, tgt], [g.reshape(1, w)], [(w, F32)], [w, _LANES], tile=tile, name="loss_head")
    return dx, dg.sum(axis=0), ls[:, 0].sum()


def _adamw(w, g, m, v, *, name):
    def fn(w, g, m, v):
        m2 = _B1 * m + (1.0 - _B1) * g
        v2 = _B2 * v + (1.0 - _B2) * jnp.square(g)
        m_hat = m2 / (1.0 - _B1 ** _STEP)
        v_hat = v2 / (1.0 - _B2 ** _STEP)
        return -_LR * (m_hat / (jnp.sqrt(v_hat) + _AEPS) + _WD * w), m2, v2
    width = w.shape[-1]
    flat = [a.reshape(-1, width) for a in (w, g, m, v)]
    tile = _pick(flat[0].shape[0], (_tile_for(width), 32, 8))
    res = _rows(fn, flat, [], [(width, F32)] * 3, tile=tile, name=name)
    return [r.reshape(w.shape) for r in res]


_HBM = pl.BlockSpec(memory_space=pltpu.HBM)


def _place():
    x, y, c = lax.axis_index("x"), lax.axis_index("y"), lax.axis_index("c")
    other_chips = [(1 - x, y), (x, 1 - y), (1 - x, 1 - y)]
    return x, y, c, other_chips


def _remote(src, dst, sems, i, dev):
    return pltpu.make_async_remote_copy(src_ref=src, dst_ref=dst, send_sem=sems[0].at[i], recv_sem=sems[1].at[i],
                                        device_id=dev, device_id_type=_MESH)


def _exchange8(v, *, reduce, name):
    r, w = v.shape

    def body(v_ref, all_ref, *rest):
        sems = rest[-2:]
        x, y, c, _ = _place()
        me = 4 * x + 2 * y + c
        all_ref[me] = v_ref[...]
        flips = [((d >> 2) & 1, (d >> 1) & 1, d & 1) for d in range(1, 8)]
        sends = [_remote(v_ref, all_ref.at[me], sems, i, (x ^ fx, y ^ fy, c ^ fc)) for i, (fx, fy, fc) in enumerate(flips)]
        for cp in sends:
            cp.start()
        for i, (fx, fy, fc) in enumerate(flips):
            _remote(v_ref, all_ref.at[me ^ (4 * fx + 2 * fy + fc)], sems, i, (x ^ fx, y ^ fy, c ^ fc)).wait_recv()
        for cp in sends:
            cp.wait_send()
        if reduce:
            acc = all_ref[0]
            for s in range(1, 8):
                acc = acc + all_ref[s]
            rest[0][...] = acc

    vm = pl.BlockSpec(memory_space=pltpu.VMEM)
    out_shape = [_SDS((8, r, w), v.dtype)] + ([_SDS((r, w), v.dtype)] if reduce else [])
    res = pl.pallas_call(body, in_specs=[vm], out_specs=[vm] * len(out_shape), out_shape=out_shape, name=name,
                         scratch_shapes=[pltpu.SemaphoreType.DMA((7,)), pltpu.SemaphoreType.DMA((7,))],
                         compiler_params=pltpu.CompilerParams(vmem_limit_bytes=int(32 << 20)))(v)
    return res[1] if reduce else res[0]


_SEM = pl.BlockSpec(memory_space=pltpu.SEMAPHORE)
_ANY = pl.BlockSpec(memory_space=pl.ANY)
_EFFECT = pltpu.SideEffectType.DATAFLOW_SIDE_EFFECTING


def _in_hbm(a):
    return pltpu.with_memory_space_constraint(a, pltpu.HBM)


def _send_start(name, srcs, land_shapes, plan, n_sends, after):
    ns, nl = len(srcs), len(land_shapes)

    def body(*refs):
        ins, lands, sems = refs[:ns], refs[ns:ns + nl], refs[ns + nl + 1:ns + nl + 3]
        x, y, c, chips = _place()
        for i, (s, d, dev) in enumerate(plan(x, y, c, chips, ins, lands)[0]):
            _remote(s, d, sems, i, dev).start()
        refs[-1][...] = jnp.zeros_like(refs[-1])

    sem = pltpu.SemaphoreType.DMA((n_sends,))
    res = pl.pallas_call(
        body, name=name, in_specs=[_HBM] * (ns + nl) + [_ANY],
        out_shape=(sem, sem, *[pltpu.HBM(s.shape, s.dtype) for s in land_shapes], _SDS((8, _LANES), F32)),
        out_specs=(_SEM, _SEM, *[_HBM] * nl, pl.BlockSpec(memory_space=pltpu.VMEM)),
        input_output_aliases={ns + i: 2 + i for i in range(nl)},
        compiler_params=pltpu.CompilerParams(has_side_effects=_EFFECT))(
            *[_in_hbm(s) for s in srcs], *[_in_hbm(lax.empty(s.shape, s.dtype)) for s in land_shapes], after)
    return dict(sems=res[:2], srcs=srcs, lands=res[2:2 + nl], plan=plan), res[-1]


def _send_wait(name, h, after):
    ns, nl = len(h['srcs']), len(h['lands'])

    def body(*refs):
        ins, lands, sems = refs[:ns], refs[ns:ns + nl], refs[ns + nl:ns + nl + 2]
        x, y, c, chips = _place()
        sends, landings = h['plan'](x, y, c, chips, ins, lands)
        for i, (s, d, dev) in enumerate(sends):
            _remote(s, d, sems, i, dev).wait_send()
        for i, d in enumerate(landings):
            _remote(d, d, sems, i, sends[i][2]).wait_recv()

    return pl.pallas_call(
        body, name=name, in_specs=[_HBM] * (ns + nl) + [_SEM, _SEM, _ANY],
        out_shape=tuple(pltpu.HBM(a.shape, a.dtype) for a in h['lands']), out_specs=tuple([_HBM] * nl),
        input_output_aliases={ns + i: i for i in range(nl)},
        compiler_params=pltpu.CompilerParams(has_side_effects=_EFFECT))(
            *[_in_hbm(s) for s in h['srcs']], *h['lands'], *h['sems'], after)


def _gather_plan(items):
    def plan(x, y, c, chips, ins, lands):
        k = 2 * x + y
        sends = [(ins[si].at[l], lands[t].at[k], (px, py, c)) for t, (si, l) in enumerate(items) for px, py in chips]
        return sends, [lands[t].at[2 * px + py] for t in range(len(items)) for px, py in chips]
    return plan


_FLIPS = [((d >> 2) & 1, (d >> 1) & 1, d & 1) for d in range(1, 8)]


def _reduce_plan(halves):
    def plan(x, y, c, chips, ins, lands):
        sends, landings = [], []
        for t, hf in enumerate(halves):
            for i, (fx, fy, fc) in enumerate(_FLIPS):
                px, py, pc = x ^ fx, y ^ fy, c ^ fc
                sends.append((ins[t].at[2 * px + py, pl.ds(pc * hf, hf)], lands[t].at[i], (px, py, pc)))
                landings.append(lands[t].at[i])
        return sends, landings
    return plan


def _swap(name, srcs, out_shapes, plan, n_sends):
    n = len(srcs)

    def body(*refs):
        ins, outs, sems = refs[:n], refs[n:n + len(out_shapes)], refs[-2:]
        x, y, c, chips = _place()
        sends, landings = plan(x, y, c, chips, ins, outs)
        out = [_remote(s, d, sems, i, dev) for i, (s, d, dev) in enumerate(sends)]
        for cp in out:
            cp.start()
        for i, d in enumerate(landings):
            _remote(d, d, sems, i, sends[i][2]).wait_recv()
        for cp in out:
            cp.wait_send()

    return pl.pallas_call(
        body, in_specs=[_HBM] * n, out_specs=[_HBM] * len(out_shapes), out_shape=out_shapes, name=name,
        scratch_shapes=[pltpu.SemaphoreType.DMA((n_sends,)), pltpu.SemaphoreType.DMA((n_sends,))])(*srcs)


def _sum_owned(grads, landed, c, k, names):
    def sum8(*parts):
        acc = parts[0].astype(F32)
        for p in parts[1:]:
            acc = acc + p.astype(F32)
        return acc
    outs = []
    for g, got, name in zip(grads, landed, names):
        hf, b = got.shape[1:]
        own = lax.dynamic_slice_in_dim(lax.dynamic_index_in_dim(g, k, axis=0, keepdims=False), c * hf, hf, axis=0)
        outs.append(_rows(sum8, [own] + [got[i] for i in range(len(_FLIPS))], [], [(b, F32)], tile=_pick(hf, (_tile_for(b), 32)),
                          name=f"grad_sum_{name}")[0])
    return outs


def _share_halves(mine, c):
    n = len(mine)

    def plan(x, y, c_, chips, ins, outs):
        return [(ins[t], outs[t], (x, y, 1 - c_)) for t in range(n)], [outs[t] for t in range(n)]
    theirs = _swap("grad_share_cores", mine, [_SDS(h.shape, F32) for h in mine], plan, n)
    return [jnp.where(c == 0, jnp.concatenate([a, b], axis=0), jnp.concatenate([b, a], axis=0)) for a, b in zip(mine, theirs)]


_BIG = ("w_in", "w_out", "w_mlp_in", "w_mlp_out")
_SMALL = ("ln1_g", "conv_b", "dt_bias", "a_log", "d_skip", "attn_norm_g", "ssd_norm_g", "ln2_g", "final_norm_g")
_ORDER = ("ln1_g", "w_in", "conv_w", "conv_b", "dt_bias", "a_log", "d_skip", "attn_norm_g", "ssd_norm_g", "w_out", "ln2_g",
          "w_mlp_in", "w_mlp_out", "final_norm_g")


def _pack(parts, rows):
    flat = jnp.concatenate([p.reshape(-1) for p in parts])
    return jnp.pad(flat, (0, rows * _LANES - flat.shape[0])).reshape(rows, _LANES)


def _unpack(buf, like):
    flat, out, o = buf.reshape(-1), [], 0
    for p in like:
        out.append(flat[o:o + p.size].reshape(p.shape))
        o += p.size
    return out


def kernel(x, ln1_g, w_in, conv_w, conv_b, dt_bias, a_log, d_skip, attn_norm_g, ssd_norm_g, w_out, ln2_g, w_mlp_in, w_mlp_out, final_norm_g, loss_target, m_ln1_g, m_w_in, m_conv_w, m_conv_b, m_dt_bias, m_a_log, m_d_skip, m_attn_norm_g, m_ssd_norm_g, m_w_out, m_ln2_g, m_w_mlp_in, m_w_mlp_out, m_final_norm_g, v_ln1_g, v_w_in, v_conv_w, v_conv_b, v_dt_bias, v_a_log, v_d_skip, v_attn_norm_g, v_ssd_norm_g, v_w_out, v_ln2_g, v_w_mlp_in, v_w_mlp_out, v_final_norm_g):
    w = dict(ln1_g=ln1_g, w_in=w_in, conv_w=conv_w, conv_b=conv_b, dt_bias=dt_bias, a_log=a_log, d_skip=d_skip,
             attn_norm_g=attn_norm_g, ssd_norm_g=ssd_norm_g, w_out=w_out, ln2_g=ln2_g, w_mlp_in=w_mlp_in, w_mlp_out=w_mlp_out,
             final_norm_g=final_norm_g)
    m = dict(ln1_g=m_ln1_g, w_in=m_w_in, conv_w=m_conv_w, conv_b=m_conv_b, dt_bias=m_dt_bias, a_log=m_a_log, d_skip=m_d_skip,
             attn_norm_g=m_attn_norm_g, ssd_norm_g=m_ssd_norm_g, w_out=m_w_out, ln2_g=m_ln2_g, w_mlp_in=m_w_mlp_in,
             w_mlp_out=m_w_mlp_out, final_norm_g=m_final_norm_g)
    v = dict(ln1_g=v_ln1_g, w_in=v_w_in, conv_w=v_conv_w, conv_b=v_conv_b, dt_bias=v_dt_bias, a_log=v_a_log, d_skip=v_d_skip,
             attn_norm_g=v_attn_norm_g, ssd_norm_g=v_ssd_norm_g, w_out=v_w_out, ln2_g=v_ln2_g, w_mlp_in=v_w_mlp_in,
             w_mlp_out=v_w_mlp_out, final_norm_g=v_final_norm_g)
    depth, d_model = ln1_g.shape
    n_chips = 4
    c = lax.axis_index("c")
    chip = 2 * lax.axis_index("x") + lax.axis_index("y")
    in_proj = w_in.shape[2] * n_chips
    cch = conv_w.shape[2] * n_chips
    zdt_pad = _LANES - _HEADS

    cw = _exchange8(conv_w.reshape(depth * _CONV_K, -1), reduce=False, name="gather_conv_w")[0::2]
    conv_full = cw.reshape(n_chips, depth, _CONV_K, -1).transpose(1, 2, 0, 3).reshape(depth, _CONV_K, cch)
    own = [w[n].astype(_BF) for n in _BIG]
    is_own = (jnp.arange(n_chips) == chip).reshape(n_chips, 1, 1)

    def start_gather(tag, items, after):
        lands = [_SDS((n_chips, *own[i].shape[1:]), _BF) for i, _ in items]
        return _send_start(f"gather_start_{tag}", own, lands, _gather_plan(items), 3 * len(items), after)

    def finish_gather(tag, handle, items, after):
        landed = _send_wait(f"gather_wait_{tag}", handle, after)
        return {_BIG[i]: jnp.where(is_own, own[i][l][None], g) for (i, l), g in zip(items, landed)}

    def layer_weights(l, blocks):
        p = {}
        if 'w_in' in blocks:
            full_in = blocks['w_in'].transpose(1, 0, 2).reshape(d_model, in_proj)
            p['w_qkv'] = full_in[:, :3 * _AW]
            p['w_xbc'] = full_in[:, 4 * _AW:4 * _AW + cch]
            p['w_zdt'] = jnp.concatenate([full_in[:, 3 * _AW:4 * _AW], full_in[:, 4 * _AW + cch:], jnp.zeros((d_model, zdt_pad), _BF)], axis=1)
        if 'w_out' in blocks:
            p['w_out'] = blocks['w_out'].reshape(-1, d_model)
            p['w_mlp_in'] = blocks['w_mlp_in'].transpose(1, 0, 2).reshape(d_model, -1)
            p['w_mlp_out'] = blocks['w_mlp_out'].reshape(-1, d_model)
        return p

    groups = dict(a=[(0, 0)], b=[(1, 0), (2, 0), (3, 0)], c=[(0, 1), (1, 1), (2, 1), (3, 1)])
    handles, token = {}, jnp.zeros((8, _LANES), F32)
    for tag, items in groups.items():
        handles[tag], token = start_gather(tag, items, token)
    layers = [{n: w[n][l] for n in _SMALL[:-1]} for l in range(depth)]
    for l in range(depth):
        layers[l]['conv_w'] = conv_full[l]

    layers[0].update(layer_weights(0, finish_gather("a", handles["a"], groups["a"], token)))
    mix, sv0 = _layer_fwd(x[0], layers[0], 0)
    layers[0].update(layer_weights(0, finish_gather("b", handles["b"], groups["b"], mix)))
    h = _layer_fwd_mlp(layers[0], sv0, 0)
    layers[1].update(layer_weights(1, finish_gather("c", handles["c"], groups["c"], h)))
    mix, sv1 = _layer_fwd(h, layers[1], 1)
    h = _layer_fwd_mlp(layers[1], sv1, 1)
    saved = [sv0, sv1]

    def by_chip(g, name):
        if name in ("w_in", "w_mlp_in"):
            return g.reshape(d_model, n_chips, -1).transpose(1, 0, 2)
        return g.reshape(n_chips, -1, d_model)

    pending = []

    def sender(l):
        def send(names, g):
            srcs = [by_chip(g[n], n) for n in names]
            halves = [s.shape[1] // 2 for s in srcs]
            lands = [_SDS((len(_FLIPS), hf, s.shape[2]), _BF) for s, hf in zip(srcs, halves)]
            handle, tok = _send_start(f"grad_start_{names[-1]}_{l}", srcs, lands, _reduce_plan(halves), len(_FLIPS) * len(srcs), srcs[0])
            pending.append((l, names, srcs, handle))
            return tok
        return send

    dx, g_final, loss_part = _loss_bwd(h, final_norm_g, loss_target[0])
    grads, after = [None] * depth, None
    for l in reversed(range(depth)):
        dx, grads[l] = _layer_bwd(dx, layers[l], saved[l], l, sender(l), after)
        after = dx
    owned = {}
    for l, names, srcs, handle in pending:
        landed = _send_wait(f"grad_wait_{names[-1]}_{l}", handle, dx)
        owned.update(zip([(n, l) for n in names], _sum_owned(srcs, landed, c, chip, [f"{n}_{l}" for n in names])))
    keys = [(n, l) for n in _BIG for l in range(depth)]
    full = dict(zip(keys, _share_halves([owned[k] for k in keys], c)))
    red = {n: jnp.stack([full[(n, l)] for l in range(depth)]) for n in _BIG}

    small = {n: jnp.stack([grads[l][n] for l in range(depth)]) for n in _SMALL[:-1] + ("conv_w",)}
    small["final_norm_g"] = g_final
    parts = [loss_part.reshape(1)] + [small[n] for n in _SMALL + ("conv_w",)]
    rows = -(-sum(p.size for p in parts) // 1024) * 8
    tot = _unpack(_exchange8(_pack(parts, rows), reduce=True, name="allreduce_small"), parts)
    loss = tot[0][0]
    red.update(zip(_SMALL + ("conv_w",), tot[1:]))
    red["conv_w"] = lax.dynamic_index_in_dim(red["conv_w"].reshape(depth, _CONV_K, n_chips, -1), chip, axis=2, keepdims=False)

    delta, new_m, new_v = {}, {}, {}
    for n in _BIG:
        delta[n], new_m[n], new_v[n] = _adamw(w[n], red[n], m[n], v[n], name=f"adamw_{n}")
    names = _SMALL + ("conv_w",)
    like = [w[n] for n in names]
    srows = -(-sum(p.size for p in like) // 1024) * 8
    res = _adamw(*[_pack([d[n] for n in names], srows) for d in (w, red, m, v)], name="adamw_small")
    for dst, buf in zip((delta, new_m, new_v), res):
        dst.update(zip(names, _unpack(buf, like)))
    return (loss, dx[None], *[red[n] for n in _ORDER], *[delta[n] for n in _ORDER], *[new_m[n] for n in _ORDER],
            *[new_v[n] for n in _ORDER])
```

```python
import numpy as np
import jax
import jax.numpy as jnp
from jax import lax
from jax.experimental import pallas as pl
from jax.experimental.pallas import tpu as pltpu

F32 = jnp.float32
_BF = jnp.bfloat16
_NEG = -1e30
_EPS = 1e-5
_HEADS = 16
_HDIM = 64
_AW = _HEADS * _HDIM
_ABLK = 128
_DILATIONS = (1, 4, 16)
_CHUNK = 128
_NSTATE = 128
_GROUPS = 2
_HPG = _HEADS // _GROUPS
_CONV_K = 4
_LANES = 128
_LR, _B1, _B2, _AEPS, _WD, _STEP = 0.001, 0.9, 0.999, 1e-08, 0.01, 10
_VMEM_CAP = 56 * 1024 * 1024
_MESH = pl.DeviceIdType.MESH
_SDS = jax.ShapeDtypeStruct
_NT = (((1,), (1,)), ((), ()))
_TN = (((0,), (0,)), ((), ()))


def _params(sem, est_bytes):
    lim = int(min(max(2 * est_bytes + (4 << 20), 16 << 20), _VMEM_CAP))
    return pltpu.CompilerParams(dimension_semantics=sem, vmem_limit_bytes=lim)


def _nbytes(shape, dtype):
    return int(np.prod(shape)) * jnp.dtype(dtype).itemsize


def _dot(a, b, dims=(((1,), (0,)), ((), ()))):
    return lax.dot_general(a.astype(_BF), b.astype(_BF), dims, preferred_element_type=F32)


def _rows(fn, ins, consts, outs, sums=(), *, tile, name):
    ins = [a if isinstance(a, tuple) else (a, a.shape[1], 0) for a in ins]
    rows = ins[0][0].shape[0]
    n_in, n_c, n_o, n_s = len(ins), len(consts), len(outs), len(sums)

    def body(*refs):
        vals = [r[...] for r in refs[:n_in + n_c]]
        res = fn(*vals)
        res = res if isinstance(res, tuple) else (res,)
        orefs = refs[n_in + n_c:n_in + n_c + n_o]
        srefs = refs[n_in + n_c + n_o:]
        for r, v in zip(orefs, res[:n_o]):
            r[...] = v.astype(r.dtype)
        if n_s:
            @pl.when(pl.program_id(0) == 0)
            def _():
                for r in srefs:
                    r[...] = jnp.zeros_like(r)
            for r, v in zip(srefs, res[n_o:]):
                r[...] += v.reshape(tile // 8, 8, v.shape[-1]).sum(axis=0)

    in_specs = [pl.BlockSpec((tile, w), lambda i, j=j: (i, j)) for _, w, j in ins]
    in_specs += [pl.BlockSpec(c.shape, lambda i, nd=c.ndim: (0,) * nd) for c in consts]
    out_shape = [_SDS((rows, w), dt) for w, dt in outs] + [_SDS((8, w), F32) for w in sums]
    out_specs = [pl.BlockSpec((tile, w), lambda i: (i, 0)) for w, _ in outs]
    out_specs += [pl.BlockSpec((8, w), lambda i: (0, 0)) for w in sums]
    est = sum(_nbytes((tile, w), a.dtype) for a, w, _ in ins) + sum(_nbytes((tile, w), dt) for w, dt in outs)
    return pl.pallas_call(body, grid=(rows // tile,), in_specs=in_specs, out_specs=out_specs, out_shape=out_shape,
                          name=name, compiler_params=_params(("arbitrary",), 3 * est))(*[a for a, _, _ in ins], *consts)


def _tile_for(width):
    return max(c for c in (256, 128, 64, 32) if c * width <= (1 << 18) or c == 32)


def _rstd(x):
    return lax.rsqrt(jnp.mean(x * x, axis=-1, keepdims=True) + _EPS)


def _split(x, groups):
    w = x.shape[-1] // groups
    return [x[:, g * w:(g + 1) * w] for g in range(groups)]


def _cat(parts):
    return parts[0] if len(parts) == 1 else jnp.concatenate(parts, axis=-1)


def _rms_bwd_tile(x, dy, g, groups):
    dxs, dgs = [], []
    for xs, ds, gs in zip(_split(x, groups), _split(dy.astype(F32), groups), _split(g, groups)):
        r = _rstd(xs)
        xh = xs * r
        gd = ds * gs
        dxs.append(r * (gd - xh * jnp.mean(gd * xh, axis=-1, keepdims=True)))
        dgs.append(ds * xh)
    return _cat(dxs), _cat(dgs)


def _rms_fwd(x, g, *, groups=1, name):
    def fn(x, g):
        return _cat([xs * _rstd(xs) * gs for xs, gs in zip(_split(x, groups), _split(g, groups))])
    w = x.shape[1]
    return _rows(fn, [x], [g.reshape(1, w)], [(w, _BF)], tile=_tile_for(w), name=name)[0]


def _rms_bwd(x, dy, g, res=None, *, name):
    def fn(x, dy, *rest):
        dx, dg = _rms_bwd_tile(x, dy, rest[-1], 1)
        return (dx + rest[0] if res is not None else dx), dg
    w = x.shape[1]
    ins = [x, dy] + ([res] if res is not None else [])
    dx, dg = _rows(fn, ins, [g.reshape(1, w)], [(w, F32)], [w], tile=_tile_for(w), name=name)
    return dx, dg.sum(axis=0)


def _pick(n, cands):
    for c in cands:
        if n % c == 0:
            return c
    raise ValueError(f"no block size for {n}")


_MM_BLOCKS = (1024, 640, 512, 384)


def _mm(a, b, *, ta=False, tb=False, extra=(), epi=None, outs=(F32,), after=None, name):
    m, k = (a.shape[1], a.shape[0]) if ta else a.shape
    n = b.shape[0] if tb else b.shape[1]
    assert k == (b.shape[1] if tb else b.shape[0])
    bm, bn, bk = _pick(m, _MM_BLOCKS), _pick(n, _MM_BLOCKS), _pick(k, _MM_BLOCKS)
    nk = k // bk
    n_e, n_o = len(extra), len(outs)
    behind = [] if after is None else [after]
    dims = (((0 if ta else 1,), (1 if tb else 0,)), ((), ()))

    def body(a_ref, b_ref, *rest):
        ex, orefs, acc = rest[:n_e], rest[n_e + len(behind):n_e + len(behind) + n_o], rest[-1]
        kk = pl.program_id(2)

        @pl.when(kk == 0)
        def _():
            acc[...] = jnp.zeros_like(acc)

        acc[...] += _dot(a_ref[...], b_ref[...], dims)

        @pl.when(kk == nk - 1)
        def _():
            r = acc[...]
            res = epi(r, *[e[...] for e in ex]) if epi is not None else (r,)
            for o, v in zip(orefs, res):
                o[...] = v.astype(o.dtype)

    a_spec = pl.BlockSpec((bk, bm), lambda i, j, kk: (kk, i)) if ta else pl.BlockSpec((bm, bk), lambda i, j, kk: (i, kk))
    b_spec = pl.BlockSpec((bn, bk), lambda i, j, kk: (j, kk)) if tb else pl.BlockSpec((bk, bn), lambda i, j, kk: (kk, j))
    t_spec = pl.BlockSpec((bm, bn), lambda i, j, kk: (i, j))
    est = (_nbytes((bm, bk), a.dtype) + _nbytes((bk, bn), b.dtype) + sum(_nbytes((bm, bn), e.dtype) for e in extra)
           + sum(_nbytes((bm, bn), o) for o in outs)) * 2 + 2 * _nbytes((bm, bn), F32)
    res = pl.pallas_call(
        body, grid=(m // bm, n // bn, nk), in_specs=[a_spec, b_spec] + [t_spec] * n_e + [pl.BlockSpec(memory_space=pl.ANY)] * len(behind),
        out_specs=[t_spec] * n_o, out_shape=[_SDS((m, n), o) for o in outs], scratch_shapes=[pltpu.VMEM((bm, bn), F32)], name=name,
        compiler_params=_params(("parallel", "parallel", "arbitrary"), est))(a, b, *extra, *behind)
    return res[0] if n_o == 1 else res


def _add_to(acc, r):
    return (acc + r,)


def _alibi_bias(dilation):
    slopes = 2.0 ** (-8.0 * (np.arange(_HEADS) + 1) / _HEADS)
    i = np.arange(_ABLK)[:, None]
    j = np.arange(_ABLK)[None, :]
    cur = np.where(i - j >= 0, -slopes[:, None, None] * ((i - j) * dilation), _NEG)
    prev = np.where(j >= i, -slopes[:, None, None] * ((i - j + _ABLK) * dilation), _NEG)
    return jnp.asarray(np.concatenate([prev, cur], axis=2), F32)


def _strided(a, d):
    return a.reshape(a.shape[0] // d, d * a.shape[1])


def _head(h):
    return slice(h * _HDIM, (h + 1) * _HDIM)


def _pair(pr):
    return slice(pr * _LANES, (pr + 1) * _LANES)


def _low_lanes(shape):
    return lax.broadcasted_iota(jnp.int32, shape, 1) < _HDIM


def _halves(v, low):
    z = jnp.zeros_like(v)
    return jnp.where(low, v, z), jnp.where(low, z, v)


def _no_prev_mask(first):
    return jnp.logical_and(first, lax.broadcasted_iota(jnp.int32, (_ABLK, 2 * _ABLK), 1) < _ABLK)


def _attn_specs(nb, n_parts):
    def cur(p):
        return pl.BlockSpec((_ABLK, _AW), lambda r, j: (jnp.minimum(j, nb - 1), r * n_parts + p))

    def prev(p):
        return pl.BlockSpec((_ABLK, _AW), lambda r, j: (jnp.clip(j - 1, 0, nb - 1), r * n_parts + p))
    return cur, prev


def _attn_fwd(qkv, dilation, *, name):
    t = qkv.shape[0]
    nb = t // dilation // _ABLK
    bias = _alibi_bias(dilation)
    scale = _HDIM ** -0.5

    def body(q_ref, kc_ref, kp_ref, vc_ref, vp_ref, b_ref, o_ref, l_ref):
        no_prev = _no_prev_mask(pl.program_id(1) == 0)
        low = _low_lanes((_ABLK, _LANES))
        for pr in range(_HEADS // 2):
            sl = _pair(pr)
            k2 = jnp.concatenate([kp_ref[:, sl], kc_ref[:, sl]], axis=0)
            v2 = jnp.concatenate([vp_ref[:, sl], vc_ref[:, sl]], axis=0)
            o2, l2 = [], []
            for h, qh in zip((2 * pr, 2 * pr + 1), _halves(q_ref[:, sl], low)):
                s = jnp.where(no_prev, _NEG, _dot(qh, k2, _NT) * scale + b_ref[h])
                m = jnp.max(s, axis=-1, keepdims=True)
                p = jnp.exp(s - m)
                den = jnp.sum(p, axis=-1, keepdims=True)
                o2.append(_dot(p, v2) / den)
                l2.append(m + jnp.log(den))
            o_ref[:, sl] = jnp.where(low, o2[0], o2[1])
            l_ref[:, sl] = jnp.where(low, l2[0], l2[1])

    cur, prev = _attn_specs(nb, 3)
    cur1, _ = _attn_specs(nb, 1)
    bspec = pl.BlockSpec((_HEADS, _ABLK, 2 * _ABLK), lambda r, j: (0, 0, 0))
    sv = _strided(qkv, dilation)
    out = _SDS((t // dilation, dilation * _AW), F32)
    o, l = pl.pallas_call(
        body, grid=(dilation, nb), in_specs=[cur(0), cur(1), prev(1), cur(2), prev(2), bspec],
        out_specs=[cur1(0), cur1(0)], out_shape=[out, out], name=name,
        compiler_params=_params(("parallel", "arbitrary"), 16 << 20))(sv, sv, sv, sv, sv, bias)
    return o.reshape(t, _AW), l.reshape(t, _AW)


def _attn_bwd(qkv, do, o, lse, dilation, *, name):
    t = qkv.shape[0]
    nb = t // dilation // _ABLK
    bias = _alibi_bias(dilation)
    scale = _HDIM ** -0.5

    def body(q_ref, kc_ref, kp_ref, vc_ref, vp_ref, do_ref, o_ref, l_ref, b_ref, dq_ref, dk_ref, dv_ref, ck, cv):
        n = pl.program_id(1)

        @pl.when(n == 0)
        def _():
            ck[...] = jnp.zeros_like(ck)
            cv[...] = jnp.zeros_like(cv)

        @pl.when(n < nb)
        def _():
            low = _low_lanes((_ABLK, _LANES))
            no_prev = _no_prev_mask(n == 0)
            for pr in range(_HEADS // 2):
                sl = _pair(pr)
                k2 = jnp.concatenate([kp_ref[:, sl], kc_ref[:, sl]], axis=0)
                v2 = jnp.concatenate([vp_ref[:, sl], vc_ref[:, sl]], axis=0)
                dov, lse = do_ref[:, sl], l_ref[:, sl]
                dsums = _halves(dov * o_ref[:, sl], low)
                dos = _halves(dov.astype(_BF), low)
                dq2, dk2, dv2 = [], 0.0, 0.0
                for i, qh in enumerate(_halves(q_ref[:, sl], low)):
                    lrow = lse[:, i * _HDIM:i * _HDIM + 1]
                    dsum = jnp.sum(dsums[i], axis=-1, keepdims=True)
                    p = jnp.exp(jnp.where(no_prev, _NEG, _dot(qh, k2, _NT) * scale + b_ref[2 * pr + i]) - lrow)
                    ds = (p * (_dot(dos[i], v2, _NT) - dsum)).astype(_BF)
                    dq2.append(_dot(ds, k2))
                    dk2, dv2 = dk2 + _dot(ds, qh, _TN), dv2 + _dot(p, dos[i], _TN)
                dq_ref[:, sl] = jnp.where(low, dq2[0], dq2[1]) * scale
                dk_ref[:, sl] = ck[:, sl] + dk2[:_ABLK] * scale
                dv_ref[:, sl] = cv[:, sl] + dv2[:_ABLK]
                ck[:, sl] = dk2[_ABLK:] * scale
                cv[:, sl] = dv2[_ABLK:]

        @pl.when(n == nb)
        def _():
            dk_ref[...] = ck[...]
            dv_ref[...] = cv[...]

    cur, prev = _attn_specs(nb, 3)
    cur1, prev1 = _attn_specs(nb, 1)
    bspec = pl.BlockSpec((_HEADS, _ABLK, 2 * _ABLK), lambda r, j: (0, 0, 0))
    sv = _strided(qkv, dilation)
    s1 = [_strided(a, dilation) for a in (do, o, lse)]
    dqkv = pl.pallas_call(
        body, grid=(dilation, nb + 1),
        in_specs=[cur(0), cur(1), prev(1), cur(2), prev(2), cur1(0), cur1(0), cur1(0), bspec],
        out_specs=[cur1(0), prev1(0), prev1(0)], out_shape=[_SDS(s1[0].shape, F32)] * 3, name=name,
        scratch_shapes=[pltpu.VMEM((_ABLK, _AW), F32)] * 2,
        compiler_params=_params(("parallel", "arbitrary"), 16 << 20))(sv, sv, sv, sv, sv, *s1, bias)
    return [a.reshape(t, _AW) for a in dqkv]


def _ssd_in_specs(ch):
    return dict(
        xs=pl.BlockSpec((_CHUNK, _AW), lambda c: (ch(c), 0)),
        bc=pl.BlockSpec((_CHUNK, 2 * _GROUPS * _NSTATE), lambda c: (ch(c), _AW // (2 * _GROUPS * _NSTATE))),
        lane=pl.BlockSpec((_CHUNK, _LANES), lambda c: (ch(c), 0)),
        arow=pl.BlockSpec((_HEADS, 1, _CHUNK), lambda c: (0, 0, ch(c))),
        st=pl.BlockSpec((1, _HEADS // 2, _NSTATE, _LANES), lambda c: (ch(c), 0, 0, 0)),
    )


def _decay(a_col, a_row):
    i0 = lax.broadcasted_iota(jnp.int32, (_CHUNK, _CHUNK), 0)
    i1 = lax.broadcasted_iota(jnp.int32, (_CHUNK, _CHUNK), 1)
    return jnp.where(i0 >= i1, jnp.exp(a_col - a_row), 0.0), jnp.where(i1 >= i0, jnp.exp(a_row - a_col), 0.0)


def _rsum(v):
    return jnp.sum(v, axis=-1, keepdims=True)


def _ssd_fwd(act, dt, acum, a_row, *, name):
    t = act.shape[0]
    nc = t // _CHUNK
    sp = _ssd_in_specs(lambda c: c)
    gw = _GROUPS * _NSTATE

    def body(xs_ref, bc_ref, dt_ref, ac_ref, ar_ref, y_ref, sall_ref, st):
        @pl.when(pl.program_id(0) == 0)
        def _():
            st[...] = jnp.zeros_like(st)

        low = _low_lanes((_CHUNK, _LANES))
        for g in range(_GROUPS):
            bg = bc_ref[:, g * _NSTATE:(g + 1) * _NSTATE]
            cg = bc_ref[:, gw + g * _NSTATE:gw + (g + 1) * _NSTATE].astype(_BF)
            cb = _dot(cg, bg, _NT)
            for pr in range(g * _HPG // 2, (g + 1) * _HPG // 2):
                ha, hb = 2 * pr, 2 * pr + 1
                a_a, a_b = ac_ref[:, ha:ha + 1], ac_ref[:, hb:hb + 1]
                x = (xs_ref[:, _pair(pr)] * jnp.where(low, dt_ref[:, ha:ha + 1], dt_ref[:, hb:hb + 1])).astype(_BF)
                lm_a, _ = _decay(a_a, ar_ref[ha])
                lm_b, _ = _decay(a_b, ar_ref[hb])
                sv = st[pr]
                sall_ref[0, pr] = sv
                yd = jnp.where(low, _dot(cb * lm_a, x), _dot(cb * lm_b, x))
                y_ref[:, _pair(pr)] = yd + jnp.where(low, jnp.exp(a_a), jnp.exp(a_b)) * _dot(cg, sv)
                al_a, al_b = jnp.min(a_a, axis=0, keepdims=True), jnp.min(a_b, axis=0, keepdims=True)
                st[pr] = (jnp.where(low, jnp.exp(al_a), jnp.exp(al_b)) * sv
                          + jnp.where(low, _dot(bg * jnp.exp(al_a - a_a), x, _TN), _dot(bg * jnp.exp(al_b - a_b), x, _TN)))

    return pl.pallas_call(
        body, grid=(nc,), in_specs=[sp['xs'], sp['bc'], sp['lane'], sp['lane'], sp['arow']],
        out_specs=[sp['xs'], sp['st']], out_shape=[_SDS((t, _AW), F32), _SDS((nc, _HEADS // 2, _NSTATE, _LANES), F32)],
        scratch_shapes=[pltpu.VMEM((_HEADS // 2, _NSTATE, _LANES), F32)], name=name,
        compiler_params=_params(("arbitrary",), 16 << 20))(act, act, dt, acum, a_row)


def _ssd_bwd(act, dt, acum, a_row, sall, dy, *, name):
    t = act.shape[0]
    nc = t // _CHUNK
    sp = _ssd_in_specs(lambda c: nc - 1 - c)
    gw = _GROUPS * _NSTATE

    def body(xs_ref, bc_ref, dt_ref, ac_ref, ar_ref, sall_ref, dy_ref, dxs_ref, dbc_ref, ddt_ref, da_ref, dst):
        @pl.when(pl.program_id(0) == 0)
        def _():
            dst[...] = jnp.zeros_like(dst)

        ddt_ref[...] = jnp.zeros_like(ddt_ref)
        da_ref[...] = jnp.zeros_like(da_ref)
        row = lax.broadcasted_iota(jnp.int32, (_CHUNK, 1), 0)
        low = _low_lanes((_CHUNK, _LANES))
        for g in range(_GROUPS):
            bg = bc_ref[:, g * _NSTATE:(g + 1) * _NSTATE]
            bgb = bg.astype(_BF)
            cg = bc_ref[:, gw + g * _NSTATE:gw + (g + 1) * _NSTATE].astype(_BF)
            cb, cbt = _dot(cg, bgb, _NT), _dot(bgb, cg, _NT)
            dcb = jnp.zeros((_CHUNK, _CHUNK), F32)
            dbg = jnp.zeros((_CHUNK, _NSTATE), F32)
            dcg = jnp.zeros((_CHUNK, _NSTATE), F32)
            for pr in range(g * _HPG // 2, (g + 1) * _HPG // 2):
                heads = (2 * pr, 2 * pr + 1)
                a_cols = [ac_ref[:, h:h + 1] for h in heads]
                dt_pair = jnp.where(low, dt_ref[:, heads[0]:heads[0] + 1], dt_ref[:, heads[1]:heads[1] + 1])
                xsv = xs_ref[:, _pair(pr)]
                x = xsv * dt_pair
                xb = x.astype(_BF)
                xhs = _halves(xb, low)
                dyv = dy_ref[:, _pair(pr)]
                dyb = dyv.astype(_BF)
                dyhs = _halves(dyb, low)
                sv, dsv = sall_ref[0, pr], dst[pr]
                svb, dsb = sv.astype(_BF), dsv.astype(_BF)
                a_lasts = [jnp.min(a, axis=0, keepdims=True) for a in a_cols]
                e_pair = jnp.where(low, jnp.exp(a_cols[0]), jnp.exp(a_cols[1]))
                el_pair = jnp.where(low, jnp.exp(a_lasts[0]), jnp.exp(a_lasts[1]))
                yo = e_pair * _dot(cg, svb)
                dxg, bwds, das = [], [], []
                for i, h in enumerate(heads):
                    lm, lmt = _decay(a_cols[i], ar_ref[h])
                    gm, gmt = cb * lm, cbt * lmt
                    w_col = jnp.exp(a_lasts[i] - a_cols[i])
                    bwds.append(_dot(bg * w_col, dsb))
                    dxg.append(_dot(gm, dyb, _TN))
                    dg, dgt = _dot(dyhs[i], xb, _NT), _dot(xhs[i], dyb, _NT)
                    dcb = dcb + dg * lm
                    dbg = dbg + w_col * _dot(xhs[i], dsb, _NT)
                    das.append(_rsum(dg * gm) - _rsum(dgt * gmt))
                bwd = jnp.where(low, bwds[0], bwds[1])
                dx = jnp.where(low, dxg[0], dxg[1]) + bwd
                edy = (e_pair * dyv).astype(_BF)
                dcg = dcg + _dot(edy, svb, _NT)
                zs, yos, sds, dts = (_halves(v, low) for v in (x * bwd, dyv * yo, sv * dsv, dx * xsv))
                for i, h in enumerate(heads):
                    z = _rsum(zs[i])
                    da_last = jnp.sum(z, axis=0, keepdims=True) + jnp.exp(a_lasts[i]) * jnp.sum(_rsum(sds[i]), axis=0, keepdims=True)
                    da_ref[:, h:h + 1] = das[i] + _rsum(yos[i]) - z + jnp.where(row == _CHUNK - 1, da_last, 0.0)
                    ddt_ref[:, h:h + 1] = _rsum(dts[i])
                dxs_ref[:, _pair(pr)] = dx * dt_pair
                dst[pr] = el_pair * dsv + _dot(cg, edy, _TN)
            dbc_ref[:, g * _NSTATE:(g + 1) * _NSTATE] = dbg + _dot(dcb, cg, _TN)
            dbc_ref[:, gw + g * _NSTATE:gw + (g + 1) * _NSTATE] = dcg + _dot(dcb, bgb)

    ch = lambda c: nc - 1 - c
    wide = pl.BlockSpec((_CHUNK, 2 * gw), lambda c: (ch(c), 0))
    return pl.pallas_call(
        body, grid=(nc,), in_specs=[sp['xs'], sp['bc'], sp['lane'], sp['lane'], sp['arow'], sp['st'], sp['xs']],
        out_specs=[sp['xs'], wide, sp['lane'], sp['lane']],
        out_shape=[_SDS((t, _AW), F32), _SDS((t, 2 * gw), F32), _SDS((t, _LANES), F32), _SDS((t, _LANES), F32)],
        scratch_shapes=[pltpu.VMEM((_HEADS // 2, _NSTATE, _LANES), F32)], name=name,
        compiler_params=_params(("arbitrary",), 16 << 20))(act, act, dt, acum, a_row, sall, dy)


def _scan_rows(v, reverse):
    r = lax.broadcasted_iota(jnp.int32, v.shape, 0)
    for s in (1, 2, 4, 8, 16, 32, 64):
        if reverse:
            v = v + jnp.where(r < _CHUNK - s, pltpu.roll(v, _CHUNK - s, 0), 0.0)
        else:
            v = v + jnp.where(r >= s, pltpu.roll(v, s, 0), 0.0)
    return v


def _softplus(x):
    return jnp.maximum(x, 0.0) + jnp.log(1.0 + jnp.exp(-jnp.abs(x)))


def _sigmoid(x):
    return 1.0 / (1.0 + jnp.exp(-x))


def _silu(x):
    return x * _sigmoid(x)


def _dsilu(x):
    s = _sigmoid(x)
    return s * (1.0 + x * (1.0 - s))


def _shift(a, j):
    if j == 0:
        return a
    if j > 0:
        return jnp.pad(a, ((j, 0), (0, 0)))[:-j]
    return jnp.pad(a, ((0, -j), (0, 0)))[-j:]


def _lanes(a):
    return jnp.pad(a, (0, _LANES - a.shape[0])).reshape(1, _LANES)


def _layer_fwd(x, p, l):
    cch = p['conv_w'].shape[1]
    sv = {}
    h1 = _rms_fwd(x, p['ln1_g'], name=f"ln1_fwd_{l}")
    qkv = _mm(h1, p['w_qkv'], outs=(_BF,), name=f"in_proj_qkv_{l}")
    xbc = _mm(h1, p['w_xbc'], name=f"in_proj_xbc_{l}")
    zdt = _mm(h1, p['w_zdt'], name=f"in_proj_zdt_{l}")
    z, dt_raw = (zdt, _AW, 0), (zdt, _LANES, _AW // _LANES)

    outs = []
    for dil in _DILATIONS:
        outs += _attn_fwd(qkv, dil, name=f"attn_fwd_d{dil}_{l}")

    def combine(o1, l1, o2, l2, o3, l3):
        m = jnp.maximum(jnp.maximum(l1, l2), l3)
        e1, e2, e3 = jnp.exp(l1 - m), jnp.exp(l2 - m), jnp.exp(l3 - m)
        tot = e1 + e2 + e3
        return (e1 * o1 + e2 * o2 + e3 * o3) / tot, m + jnp.log(tot)
    attn, lse = _rows(combine, outs, [], [(_AW, F32), (_AW, F32)], tile=_tile_for(_AW), name=f"attn_combine_{l}")
    attn_n = _rms_fwd(attn, p['attn_norm_g'], name=f"attn_norm_fwd_{l}")

    us = [_shift(xbc, j) for j in range(_CONV_K)]

    def conv(u0, u1, u2, u3, w, b):
        return _silu(w[0:1] * u3 + w[1:2] * u2 + w[2:3] * u1 + w[3:4] * u0 + b)
    act = _rows(conv, us, [p['conv_w'], p['conv_b'].reshape(1, cch)], [(cch, F32)], tile=_tile_for(cch), name=f"conv_fwd_{l}")[0]

    def dtf(raw, bias, alog):
        dt = _softplus(raw + bias)
        return dt, _scan_rows(dt * -jnp.exp(alog), False)
    dt, acum = _rows(dtf, [dt_raw], [_lanes(p['dt_bias']), _lanes(p['a_log'])], [(_LANES, F32), (_LANES, F32)],
                     tile=_CHUNK, name=f"dt_fwd_{l}")
    a_row = acum[:, :_HEADS].T[:, None, :]
    y_ssd, sall = _ssd_fwd(act, dt, acum, a_row, name=f"ssd_fwd_{l}")
    dskip = jnp.repeat(p['d_skip'], _HDIM).reshape(1, _AW)
    xs = (act, _AW, 0)

    def gate(y, xs, z, dsk):
        return (y + dsk * xs) * _silu(z)
    y2 = _rows(gate, [y_ssd, xs, z], [dskip], [(_AW, F32)], tile=_tile_for(_AW), name=f"gate_fwd_{l}")[0]
    y_n = _rms_fwd(y2, p['ssd_norm_g'], groups=_GROUPS, name=f"ssd_norm_fwd_{l}")

    mix = jnp.concatenate([attn_n, y_n], axis=1)
    sv.update(x=x, h1=h1, qkv=qkv, zdt=zdt, us=us, attn=attn, lse=lse, act=act, dt=dt, acum=acum, a_row=a_row,
              sall=sall, y_ssd=y_ssd, dskip=dskip, y2=y2, mix=mix)
    return mix, sv


def _layer_fwd_mlp(p, sv, l):
    x2 = _mm(sv['mix'], p['w_out'], extra=(sv['x'],), epi=_add_to, name=f"out_proj_{l}")
    h2 = _rms_fwd(x2, p['ln2_g'], name=f"ln2_fwd_{l}")
    u, a = _mm(h2, p['w_mlp_in'], epi=lambda acc: (acc, jnp.square(jnp.maximum(acc, 0.0))), outs=(F32, _BF), name=f"mlp_in_{l}")
    x3 = _mm(a, p['w_mlp_out'], extra=(x2,), epi=_add_to, name=f"mlp_out_{l}")
    sv.update(x2=x2, h2=h2, u=u, a=a)
    return x3


def _layer_bwd(dx3, p, sv, l, send, after):
    cch = p['conv_w'].shape[1]
    g = {}
    dx3b = dx3.astype(_BF)
    du = _mm(dx3b, p['w_mlp_out'], tb=True, extra=(sv['u'],), outs=(_BF,), after=after,
             epi=lambda acc, u: (acc * 2.0 * jnp.maximum(u, 0.0),), name=f"mlp_out_dx_{l}")
    g['w_mlp_out'] = _mm(sv['a'], dx3b, ta=True, outs=(_BF,), name=f"mlp_out_dw_{l}")
    g['w_mlp_in'] = _mm(sv['h2'], du, ta=True, outs=(_BF,), name=f"mlp_in_dw_{l}")
    sent = send(('w_mlp_out', 'w_mlp_in'), g)
    dh2 = _mm(du, p['w_mlp_in'], tb=True, after=sent, name=f"mlp_in_dx_{l}")
    dx2, g['ln2_g'] = _rms_bwd(sv['x2'], dh2, p['ln2_g'], dx3, name=f"ln2_bwd_{l}")
    dx2b = dx2.astype(_BF)
    dmix = _mm(dx2b, p['w_out'], tb=True, name=f"out_proj_dx_{l}")
    g['w_out'] = _mm(sv['mix'], dx2b, ta=True, outs=(_BF,), name=f"out_proj_dw_{l}")

    dattn, g['attn_norm_g'] = _rms_bwd(sv['attn'], (dmix, _AW, 0), p['attn_norm_g'], name=f"attn_norm_bwd_{l}")
    parts = [_attn_bwd(sv['qkv'], dattn, sv['attn'], sv['lse'], dil, name=f"attn_bwd_d{dil}_{l}") for dil in _DILATIONS]
    def branch_sum(*t):
        return jnp.concatenate([t[i] + t[3 + i] + t[6 + i] for i in range(3)], axis=1)
    dqkv = _rows(branch_sum, [a for pr in parts for a in pr], [], [(3 * _AW, _BF)], tile=128, name=f"attn_bwd_sum_{l}")[0]

    xs, z, dt_raw = (sv['act'], _AW, 0), (sv['zdt'], _AW, 0), (sv['zdt'], _LANES, _AW // _LANES)

    def gate_bwd(y2, dy, y, xs, z, dsk, gn):
        dy2, dgn = _rms_bwd_tile(y2, dy, gn, _GROUPS)
        dy1 = dy2 * _silu(z)
        return dy1, dsk * dy1, dy2 * (y + dsk * xs) * _dsilu(z), dy1 * xs, dgn
    dy1, dxs_skip, dz, dsk_sum, gn_sum = _rows(
        gate_bwd, [sv['y2'], (dmix, _AW, 1), sv['y_ssd'], xs, z], [sv['dskip'], p['ssd_norm_g'].reshape(1, _AW)],
        [(_AW, F32), (_AW, F32), (_AW, _BF)], [_AW, _AW], tile=128, name=f"gate_bwd_{l}")
    g['ssd_norm_g'] = gn_sum.sum(axis=0)
    g['d_skip'] = dsk_sum.sum(axis=0).reshape(_HEADS, _HDIM).sum(axis=1)
    dxs, dbc, ddt, da = _ssd_bwd(sv['act'], sv['dt'], sv['acum'], sv['a_row'], sv['sall'], dy1, name=f"ssd_bwd_{l}")

    def dtb(da, ddtx, raw, dt, dz, bias, alog):
        a = -jnp.exp(alog)
        dda = _scan_rows(da, True)
        draw = (dda * a + ddtx) * _sigmoid(raw + bias)
        return jnp.concatenate([dz, draw.astype(dz.dtype)], axis=1), draw, dda * dt * a
    dzdt, dbias, dalog = _rows(dtb, [da, ddt, dt_raw, sv['dt'], dz], [_lanes(p['dt_bias']), _lanes(p['a_log'])],
                               [(_AW + _LANES, _BF)], [_LANES, _LANES], tile=_CHUNK, name=f"dt_bwd_{l}")
    g['dt_bias'], g['a_log'] = dbias.sum(axis=0)[:_HEADS], dalog.sum(axis=0)[:_HEADS]
    us = sv['us']

    def conv_bwd1(u0, u1, u2, u3, dxs, dbc, dxk, w, b):
        pre = w[0:1] * u3 + w[1:2] * u2 + w[2:3] * u1 + w[3:4] * u0 + b
        dp = jnp.concatenate([dxs + dxk, dbc], axis=1) * _dsilu(pre)
        return dp, dp * u3, dp * u2, dp * u1, dp * u0, dp
    dpre, *dws = _rows(conv_bwd1, [*us, dxs, dbc, dxs_skip], [p['conv_w'], p['conv_b'].reshape(1, cch)], [(cch, F32)], [cch] * 5,
                       tile=128, name=f"conv_bwd_pre_{l}")
    g['conv_w'] = jnp.stack([dws[i].sum(axis=0) for i in range(_CONV_K)])
    g['conv_b'] = dws[4].sum(axis=0)

    def conv_bwd2(p0, p1, p2, p3, w):
        return w[3:4] * p0 + w[2:3] * p1 + w[1:2] * p2 + w[0:1] * p3
    dxbc = _rows(conv_bwd2, [_shift(dpre, -j) for j in range(_CONV_K)], [p['conv_w']], [(cch, _BF)], tile=_tile_for(cch),
                 name=f"conv_bwd_in_{l}")[0]
    h1 = sv['h1']
    g_qkv = _mm(h1, dqkv, ta=True, outs=(_BF,), name=f"in_proj_qkv_dw_{l}")
    g_xbc = _mm(h1, dxbc, ta=True, outs=(_BF,), name=f"in_proj_xbc_dw_{l}")
    g_zdt = _mm(h1, dzdt, ta=True, outs=(_BF,), name=f"in_proj_zdt_dw_{l}")
    g['w_in'] = jnp.concatenate([g_qkv, g_zdt[:, :_AW], g_xbc, g_zdt[:, _AW:_AW + _HEADS]], axis=1)
    sent = send(('w_out', 'w_in'), g)
    for n in _BIG:
        del g[n]
    dh1 = _mm(dqkv, p['w_qkv'], tb=True, after=sent, name=f"in_proj_qkv_dx_{l}")
    dh1 = _mm(dxbc, p['w_xbc'], tb=True, extra=(dh1,), epi=_add_to, name=f"in_proj_xbc_dx_{l}")
    dh1 = _mm(dzdt, p['w_zdt'], tb=True, extra=(dh1,), epi=_add_to, name=f"in_proj_zdt_dx_{l}")
    dx, g['ln1_g'] = _rms_bwd(sv['x'], dh1, p['ln1_g'], dx2, name=f"ln1_bwd_{l}")
    return dx, g


def _loss_bwd(x, g, tgt):
    w = x.shape[1]
    tile = _tile_for(w)

    def fn(x, tgt, g):
        r = _rstd(x)
        xh = x * r
        e = xh * g - tgt
        gd = e * (g / w)
        dx = r * (gd - xh * jnp.mean(gd * xh, axis=-1, keepdims=True))
        rowloss = 0.5 * jnp.mean(e * e, axis=-1, keepdims=True)
        return dx, (e / w) * xh, jnp.broadcast_to(rowloss, (tile, _LANES))
    dx, dg, ls = _rows(fn, [x, tgt], [g.reshape(1, w)], [(w, F32)], [w, _LANES], tile=tile, name="loss_head")
    return dx, dg.sum(axis=0), ls[:, 0].sum()


def _adamw(w, g, m, v, *, name):
    def fn(w, g, m, v):
        m2 = _B1 * m + (1.0 - _B1) * g
        v2 = _B2 * v + (1.0 - _B2) * jnp.square(g)
        m_hat = m2 / (1.0 - _B1 ** _STEP)
        v_hat = v2 / (1.0 - _B2 ** _STEP)
        return -_LR * (m_hat / (jnp.sqrt(v_hat) + _AEPS) + _WD * w), m2, v2
    width = w.shape[-1]
    flat = [a.reshape(-1, width) for a in (w, g, m, v)]
    tile = _pick(flat[0].shape[0], (_tile_for(width), 32, 8))
    res = _rows(fn, flat, [], [(width, F32)] * 3, tile=tile, name=name)
    return [r.reshape(w.shape) for r in res]


_HBM = pl.BlockSpec(memory_space=pltpu.HBM)


def _place():
    x, y, c = lax.axis_index("x"), lax.axis_index("y"), lax.axis_index("c")
    other_chips = [(1 - x, y), (x, 1 - y), (1 - x, 1 - y)]
    return x, y, c, other_chips


def _remote(src, dst, sems, i, dev):
    return pltpu.make_async_remote_copy(src_ref=src, dst_ref=dst, send_sem=sems[0].at[i], recv_sem=sems[1].at[i],
                                        device_id=dev, device_id_type=_MESH)


def _exchange8(v, *, reduce, after=None, name):
    r, w = v.shape
    behind = [] if after is None else [after]

    def body(v_ref, *rest):
        all_ref, rest = rest[len(behind)], rest[len(behind) + 1:]
        sems = rest[-2:]
        x, y, c, _ = _place()
        me = 4 * x + 2 * y + c
        all_ref[me] = v_ref[...]
        flips = [((d >> 2) & 1, (d >> 1) & 1, d & 1) for d in range(1, 8)]
        sends = [_remote(v_ref, all_ref.at[me], sems, i, (x ^ fx, y ^ fy, c ^ fc)) for i, (fx, fy, fc) in enumerate(flips)]
        for cp in sends:
            cp.start()
        for i, (fx, fy, fc) in enumerate(flips):
            _remote(v_ref, all_ref.at[me ^ (4 * fx + 2 * fy + fc)], sems, i, (x ^ fx, y ^ fy, c ^ fc)).wait_recv()
        for cp in sends:
            cp.wait_send()
        if reduce:
            acc = all_ref[0]
            for s in range(1, 8):
                acc = acc + all_ref[s]
            rest[0][...] = acc

    vm = pl.BlockSpec(memory_space=pltpu.VMEM)
    out_shape = [_SDS((8, r, w), v.dtype)] + ([_SDS((r, w), v.dtype)] if reduce else [])
    res = pl.pallas_call(body, in_specs=[vm] + [_ANY] * len(behind), out_specs=[vm] * len(out_shape), out_shape=out_shape, name=name,
                         scratch_shapes=[pltpu.SemaphoreType.DMA((7,)), pltpu.SemaphoreType.DMA((7,))],
                         compiler_params=pltpu.CompilerParams(vmem_limit_bytes=int(32 << 20)))(v, *behind)
    return res[1] if reduce else res[0]


_SEM = pl.BlockSpec(memory_space=pltpu.SEMAPHORE)
_ANY = pl.BlockSpec(memory_space=pl.ANY)
_EFFECT = pltpu.SideEffectType.DATAFLOW_SIDE_EFFECTING


def _in_hbm(a):
    return pltpu.with_memory_space_constraint(a, pltpu.HBM)


def _send_start(name, srcs, land_shapes, plan, n_sends, after):
    ns, nl = len(srcs), len(land_shapes)

    def body(*refs):
        ins, lands, sems = refs[:ns], refs[ns:ns + nl], refs[ns + nl + 1:ns + nl + 3]
        x, y, c, chips = _place()
        for i, (s, d, dev) in enumerate(plan(x, y, c, chips, ins, lands)[0]):
            _remote(s, d, sems, i, dev).start()
        refs[-1][...] = jnp.zeros_like(refs[-1])

    sem = pltpu.SemaphoreType.DMA((n_sends,))
    res = pl.pallas_call(
        body, name=name, in_specs=[_HBM] * (ns + nl) + [_ANY],
        out_shape=(sem, sem, *[pltpu.HBM(s.shape, s.dtype) for s in land_shapes], _SDS((8, _LANES), F32)),
        out_specs=(_SEM, _SEM, *[_HBM] * nl, pl.BlockSpec(memory_space=pltpu.VMEM)),
        input_output_aliases={ns + i: 2 + i for i in range(nl)},
        compiler_params=pltpu.CompilerParams(has_side_effects=_EFFECT))(
            *[_in_hbm(s) for s in srcs], *[_in_hbm(lax.empty(s.shape, s.dtype)) for s in land_shapes], after)
    return dict(sems=res[:2], srcs=srcs, lands=res[2:2 + nl], plan=plan), res[-1]


def _send_wait(name, h, after):
    ns, nl = len(h['srcs']), len(h['lands'])

    def body(*refs):
        ins, lands, sems = refs[:ns], refs[ns:ns + nl], refs[ns + nl:ns + nl + 2]
        x, y, c, chips = _place()
        sends, landings = h['plan'](x, y, c, chips, ins, lands)
        for i, (s, d, dev) in enumerate(sends):
            _remote(s, d, sems, i, dev).wait_send()
        for i, d in enumerate(landings):
            _remote(d, d, sems, i, sends[i][2]).wait_recv()

    return pl.pallas_call(
        body, name=name, in_specs=[_HBM] * (ns + nl) + [_SEM, _SEM, _ANY],
        out_shape=tuple(pltpu.HBM(a.shape, a.dtype) for a in h['lands']), out_specs=tuple([_HBM] * nl),
        input_output_aliases={ns + i: i for i in range(nl)},
        compiler_params=pltpu.CompilerParams(has_side_effects=_EFFECT))(
            *[_in_hbm(s) for s in h['srcs']], *h['lands'], *h['sems'], after)


def _gather_plan(items):
    def plan(x, y, c, chips, ins, lands):
        k = 2 * x + y
        sends = [(ins[si].at[l], lands[t].at[k], (px, py, c)) for t, (si, l) in enumerate(items) for px, py in chips]
        return sends, [lands[t].at[2 * px + py] for t in range(len(items)) for px, py in chips]
    return plan


_FLIPS = [((d >> 2) & 1, (d >> 1) & 1, d & 1) for d in range(1, 8)]


def _reduce_plan(halves):
    def plan(x, y, c, chips, ins, lands):
        sends, landings = [], []
        for t, hf in enumerate(halves):
            for i, (fx, fy, fc) in enumerate(_FLIPS):
                px, py, pc = x ^ fx, y ^ fy, c ^ fc
                sends.append((ins[t].at[2 * px + py, pl.ds(pc * hf, hf)], lands[t].at[i], (px, py, pc)))
                landings.append(lands[t].at[i])
        return sends, landings
    return plan


def _swap(name, srcs, out_shapes, plan, n_sends):
    n = len(srcs)

    def body(*refs):
        ins, outs, sems = refs[:n], refs[n:n + len(out_shapes)], refs[-2:]
        x, y, c, chips = _place()
        sends, landings = plan(x, y, c, chips, ins, outs)
        out = [_remote(s, d, sems, i, dev) for i, (s, d, dev) in enumerate(sends)]
        for cp in out:
            cp.start()
        for i, d in enumerate(landings):
            _remote(d, d, sems, i, sends[i][2]).wait_recv()
        for cp in out:
            cp.wait_send()

    return pl.pallas_call(
        body, in_specs=[_HBM] * n, out_specs=[_HBM] * len(out_shapes), out_shape=out_shapes, name=name,
        scratch_shapes=[pltpu.SemaphoreType.DMA((n_sends,)), pltpu.SemaphoreType.DMA((n_sends,))])(*srcs)


def _sum_owned(grads, landed, c, k, names):
    def sum8(*parts):
        acc = parts[0].astype(F32)
        for p in parts[1:]:
            acc = acc + p.astype(F32)
        return acc
    outs = []
    for g, got, name in zip(grads, landed, names):
        hf, b = got.shape[1:]
        own = lax.dynamic_slice_in_dim(lax.dynamic_index_in_dim(g, k, axis=0, keepdims=False), c * hf, hf, axis=0)
        outs.append(_rows(sum8, [own] + [got[i] for i in range(len(_FLIPS))], [], [(b, F32)], tile=_pick(hf, (_tile_for(b), 32)),
                          name=f"grad_sum_{name}")[0])
    return outs


def _share_halves(mine, c):
    n = len(mine)

    def plan(x, y, c_, chips, ins, outs):
        return [(ins[t], outs[t], (x, y, 1 - c_)) for t in range(n)], [outs[t] for t in range(n)]
    theirs = _swap("grad_share_cores", mine, [_SDS(h.shape, F32) for h in mine], plan, n)
    return [jnp.where(c == 0, jnp.concatenate([a, b], axis=0), jnp.concatenate([b, a], axis=0)) for a, b in zip(mine, theirs)]


_BIG = ("w_in", "w_out", "w_mlp_in", "w_mlp_out")
_SMALL = ("ln1_g", "conv_b", "dt_bias", "a_log", "d_skip", "attn_norm_g", "ssd_norm_g", "ln2_g", "final_norm_g")
_ORDER = ("ln1_g", "w_in", "conv_w", "conv_b", "dt_bias", "a_log", "d_skip", "attn_norm_g", "ssd_norm_g", "w_out", "ln2_g",
          "w_mlp_in", "w_mlp_out", "final_norm_g")


def _pack(parts, rows):
    flat = jnp.concatenate([p.reshape(-1) for p in parts])
    return jnp.pad(flat, (0, rows * _LANES - flat.shape[0])).reshape(rows, _LANES)


def _unpack(buf, like):
    flat, out, o = buf.reshape(-1), [], 0
    for p in like:
        out.append(flat[o:o + p.size].reshape(p.shape))
        o += p.size
    return out


def kernel(x, ln1_g, w_in, conv_w, conv_b, dt_bias, a_log, d_skip, attn_norm_g, ssd_norm_g, w_out, ln2_g, w_mlp_in, w_mlp_out, final_norm_g, loss_target, m_ln1_g, m_w_in, m_conv_w, m_conv_b, m_dt_bias, m_a_log, m_d_skip, m_attn_norm_g, m_ssd_norm_g, m_w_out, m_ln2_g, m_w_mlp_in, m_w_mlp_out, m_final_norm_g, v_ln1_g, v_w_in, v_conv_w, v_conv_b, v_dt_bias, v_a_log, v_d_skip, v_attn_norm_g, v_ssd_norm_g, v_w_out, v_ln2_g, v_w_mlp_in, v_w_mlp_out, v_final_norm_g):
    w = dict(ln1_g=ln1_g, w_in=w_in, conv_w=conv_w, conv_b=conv_b, dt_bias=dt_bias, a_log=a_log, d_skip=d_skip,
             attn_norm_g=attn_norm_g, ssd_norm_g=ssd_norm_g, w_out=w_out, ln2_g=ln2_g, w_mlp_in=w_mlp_in, w_mlp_out=w_mlp_out,
             final_norm_g=final_norm_g)
    m = dict(ln1_g=m_ln1_g, w_in=m_w_in, conv_w=m_conv_w, conv_b=m_conv_b, dt_bias=m_dt_bias, a_log=m_a_log, d_skip=m_d_skip,
             attn_norm_g=m_attn_norm_g, ssd_norm_g=m_ssd_norm_g, w_out=m_w_out, ln2_g=m_ln2_g, w_mlp_in=m_w_mlp_in,
             w_mlp_out=m_w_mlp_out, final_norm_g=m_final_norm_g)
    v = dict(ln1_g=v_ln1_g, w_in=v_w_in, conv_w=v_conv_w, conv_b=v_conv_b, dt_bias=v_dt_bias, a_log=v_a_log, d_skip=v_d_skip,
             attn_norm_g=v_attn_norm_g, ssd_norm_g=v_ssd_norm_g, w_out=v_w_out, ln2_g=v_ln2_g, w_mlp_in=v_w_mlp_in,
             w_mlp_out=v_w_mlp_out, final_norm_g=v_final_norm_g)
    depth, d_model = ln1_g.shape
    n_chips = 4
    c = lax.axis_index("c")
    chip = 2 * lax.axis_index("x") + lax.axis_index("y")
    in_proj = w_in.shape[2] * n_chips
    cch = conv_w.shape[2] * n_chips
    zdt_pad = _LANES - _HEADS

    cw = _exchange8(conv_w.reshape(depth * _CONV_K, -1), reduce=False, name="gather_conv_w")[0::2]
    conv_full = cw.reshape(n_chips, depth, _CONV_K, -1).transpose(1, 2, 0, 3).reshape(depth, _CONV_K, cch)
    own = [w[n].astype(_BF) for n in _BIG]
    is_own = (jnp.arange(n_chips) == chip).reshape(n_chips, 1, 1)

    def start_gather(tag, items, after):
        lands = [_SDS((n_chips, *own[i].shape[1:]), _BF) for i, _ in items]
        return _send_start(f"gather_start_{tag}", own, lands, _gather_plan(items), 3 * len(items), after)

    def finish_gather(tag, handle, items, after):
        landed = _send_wait(f"gather_wait_{tag}", handle, after)
        return {_BIG[i]: jnp.where(is_own, own[i][l][None], g) for (i, l), g in zip(items, landed)}

    def layer_weights(l, blocks):
        p = {}
        if 'w_in' in blocks:
            full_in = blocks['w_in'].transpose(1, 0, 2).reshape(d_model, in_proj)
            p['w_qkv'] = full_in[:, :3 * _AW]
            p['w_xbc'] = full_in[:, 4 * _AW:4 * _AW + cch]
            p['w_zdt'] = jnp.concatenate([full_in[:, 3 * _AW:4 * _AW], full_in[:, 4 * _AW + cch:], jnp.zeros((d_model, zdt_pad), _BF)], axis=1)
        if 'w_out' in blocks:
            p['w_out'] = blocks['w_out'].reshape(-1, d_model)
            p['w_mlp_in'] = blocks['w_mlp_in'].transpose(1, 0, 2).reshape(d_model, -1)
            p['w_mlp_out'] = blocks['w_mlp_out'].reshape(-1, d_model)
        return p

    groups = dict(a=[(0, 0)], b=[(1, 0), (2, 0), (3, 0)], c=[(0, 1), (1, 1), (2, 1), (3, 1)])
    handles, token = {}, conv_full
    for tag, items in groups.items():
        handles[tag], token = start_gather(tag, items, token)
    layers = [{n: w[n][l] for n in _SMALL[:-1]} for l in range(depth)]
    for l in range(depth):
        layers[l]['conv_w'] = conv_full[l]

    layers[0].update(layer_weights(0, finish_gather("a", handles["a"], groups["a"], token)))
    mix, sv0 = _layer_fwd(x[0], layers[0], 0)
    layers[0].update(layer_weights(0, finish_gather("b", handles["b"], groups["b"], mix)))
    h = _layer_fwd_mlp(layers[0], sv0, 0)
    layers[1].update(layer_weights(1, finish_gather("c", handles["c"], groups["c"], h)))
    mix, sv1 = _layer_fwd(h, layers[1], 1)
    h = _layer_fwd_mlp(layers[1], sv1, 1)
    saved = [sv0, sv1]

    def by_chip(g, name):
        if name in ("w_in", "w_mlp_in"):
            return g.reshape(d_model, n_chips, -1).transpose(1, 0, 2)
        return g.reshape(n_chips, -1, d_model)

    pending = []

    def sender(l):
        def send(names, g):
            srcs = [by_chip(g[n], n) for n in names]
            halves = [s.shape[1] // 2 for s in srcs]
            lands = [_SDS((len(_FLIPS), hf, s.shape[2]), _BF) for s, hf in zip(srcs, halves)]
            handle, tok = _send_start(f"grad_start_{names[-1]}_{l}", srcs, lands, _reduce_plan(halves), len(_FLIPS) * len(srcs), srcs[0])
            pending.append((l, names, srcs, handle))
            return tok
        return send

    dx, g_final, loss_part = _loss_bwd(h, final_norm_g, loss_target[0])
    grads, after = [None] * depth, None
    for l in reversed(range(depth)):
        dx, grads[l] = _layer_bwd(dx, layers[l], saved[l], l, sender(l), after)
        after = dx
    owned = {}
    for l, names, srcs, handle in pending:
        landed = _send_wait(f"grad_wait_{names[-1]}_{l}", handle, dx)
        owned.update(zip([(n, l) for n in names], _sum_owned(srcs, landed, c, chip, [f"{n}_{l}" for n in names])))
    keys = [(n, l) for n in _BIG for l in range(depth)]
    full = dict(zip(keys, _share_halves([owned[k] for k in keys], c)))
    red = {n: jnp.stack([full[(n, l)] for l in range(depth)]) for n in _BIG}

    small = {n: jnp.stack([grads[l][n] for l in range(depth)]) for n in _SMALL[:-1] + ("conv_w",)}
    small["final_norm_g"] = g_final
    parts = [loss_part.reshape(1)] + [small[n] for n in _SMALL + ("conv_w",)]
    rows = -(-sum(p.size for p in parts) // 1024) * 8
    tot = _unpack(_exchange8(_pack(parts, rows), reduce=True, after=red[_BIG[0]], name="allreduce_small"), parts)
    loss = tot[0][0]
    red.update(zip(_SMALL + ("conv_w",), tot[1:]))
    red["conv_w"] = lax.dynamic_index_in_dim(red["conv_w"].reshape(depth, _CONV_K, n_chips, -1), chip, axis=2, keepdims=False)

    delta, new_m, new_v = {}, {}, {}
    for n in _BIG:
        delta[n], new_m[n], new_v[n] = _adamw(w[n], red[n], m[n], v[n], name=f"adamw_{n}")
    names = _SMALL + ("conv_w",)
    like = [w[n] for n in names]
    srows = -(-sum(p.size for p in like) // 1024) * 8
    res = _adamw(*[_pack([d[n] for n in names], srows) for d in (w, red, m, v)], name="adamw_small")
    for dst, buf in zip((delta, new_m, new_v), res):
        dst.update(zip(names, _unpack(buf, like)))
    return (loss, dx[None], *[red[n] for n in _ORDER], *[delta[n] for n in _ORDER], *[new_m[n] for n in _ORDER],
            *[new_v[n] for n in _ORDER])
```

```python
import numpy as np
import jax
import jax.numpy as jnp
from jax import lax
from jax.experimental import pallas as pl
from jax.experimental.pallas import tpu as pltpu

F32 = jnp.float32
_BF = jnp.bfloat16
_NEG = -1e30
_EPS = 1e-5
_HEADS = 16
_HDIM = 64
_AW = _HEADS * _HDIM
_ABLK = 128
_DILATIONS = (1, 4, 16)
_CHUNK = 128
_NSTATE = 128
_GROUPS = 2
_HPG = _HEADS // _GROUPS
_CONV_K = 4
_LANES = 128
_LR, _B1, _B2, _AEPS, _WD, _STEP = 0.001, 0.9, 0.999, 1e-08, 0.01, 10
_VMEM_CAP = 56 * 1024 * 1024
_MESH = pl.DeviceIdType.MESH
_SDS = jax.ShapeDtypeStruct
_NT = (((1,), (1,)), ((), ()))
_TN = (((0,), (0,)), ((), ()))


def _params(sem, est_bytes):
    lim = int(min(max(2 * est_bytes + (4 << 20), 16 << 20), _VMEM_CAP))
    return pltpu.CompilerParams(dimension_semantics=sem, vmem_limit_bytes=lim)


def _nbytes(shape, dtype):
    return int(np.prod(shape)) * jnp.dtype(dtype).itemsize


def _dot(a, b, dims=(((1,), (0,)), ((), ()))):
    return lax.dot_general(a.astype(_BF), b.astype(_BF), dims, preferred_element_type=F32)


def _rows(fn, ins, consts, outs, sums=(), *, tile, name):
    ins = [a if isinstance(a, tuple) else (a, a.shape[1], 0) for a in ins]
    rows = ins[0][0].shape[0]
    n_in, n_c, n_o, n_s = len(ins), len(consts), len(outs), len(sums)

    def body(*refs):
        vals = [r[...] for r in refs[:n_in + n_c]]
        res = fn(*vals)
        res = res if isinstance(res, tuple) else (res,)
        orefs = refs[n_in + n_c:n_in + n_c + n_o]
        srefs = refs[n_in + n_c + n_o:]
        for r, v in zip(orefs, res[:n_o]):
            r[...] = v.astype(r.dtype)
        if n_s:
            @pl.when(pl.program_id(0) == 0)
            def _():
                for r in srefs:
                    r[...] = jnp.zeros_like(r)
            for r, v in zip(srefs, res[n_o:]):
                r[...] += v.reshape(tile // 8, 8, v.shape[-1]).sum(axis=0)

    in_specs = [pl.BlockSpec((tile, w), lambda i, j=j: (i, j)) for _, w, j in ins]
    in_specs += [pl.BlockSpec(c.shape, lambda i, nd=c.ndim: (0,) * nd) for c in consts]
    out_shape = [_SDS((rows, w), dt) for w, dt in outs] + [_SDS((8, w), F32) for w in sums]
    out_specs = [pl.BlockSpec((tile, w), lambda i: (i, 0)) for w, _ in outs]
    out_specs += [pl.BlockSpec((8, w), lambda i: (0, 0)) for w in sums]
    est = sum(_nbytes((tile, w), a.dtype) for a, w, _ in ins) + sum(_nbytes((tile, w), dt) for w, dt in outs)
    return pl.pallas_call(body, grid=(rows // tile,), in_specs=in_specs, out_specs=out_specs, out_shape=out_shape,
                          name=name, compiler_params=_params(("arbitrary",), 3 * est))(*[a for a, _, _ in ins], *consts)


def _tile_for(width):
    return max(c for c in (256, 128, 64, 32) if c * width <= (1 << 18) or c == 32)


def _rstd(x):
    return lax.rsqrt(jnp.mean(x * x, axis=-1, keepdims=True) + _EPS)


def _split(x, groups):
    w = x.shape[-1] // groups
    return [x[:, g * w:(g + 1) * w] for g in range(groups)]


def _cat(parts):
    return parts[0] if len(parts) == 1 else jnp.concatenate(parts, axis=-1)


def _rms_bwd_tile(x, dy, g, groups):
    dxs, dgs = [], []
    for xs, ds, gs in zip(_split(x, groups), _split(dy.astype(F32), groups), _split(g, groups)):
        r = _rstd(xs)
        xh = xs * r
        gd = ds * gs
        dxs.append(r * (gd - xh * jnp.mean(gd * xh, axis=-1, keepdims=True)))
        dgs.append(ds * xh)
    return _cat(dxs), _cat(dgs)


def _rms_fwd(x, g, *, groups=1, name):
    def fn(x, g):
        return _cat([xs * _rstd(xs) * gs for xs, gs in zip(_split(x, groups), _split(g, groups))])
    w = x.shape[1]
    return _rows(fn, [x], [g.reshape(1, w)], [(w, _BF)], tile=_tile_for(w), name=name)[0]


def _rms_bwd(x, dy, g, res=None, *, name):
    def fn(x, dy, *rest):
        dx, dg = _rms_bwd_tile(x, dy, rest[-1], 1)
        return (dx + rest[0] if res is not None else dx), dg
    w = x.shape[1]
    ins = [x, dy] + ([res] if res is not None else [])
    dx, dg = _rows(fn, ins, [g.reshape(1, w)], [(w, F32)], [w], tile=_tile_for(w), name=name)
    return dx, dg.sum(axis=0)


def _pick(n, cands):
    for c in cands:
        if n % c == 0:
            return c
    raise ValueError(f"no block size for {n}")


_MM_BLOCKS = (1024, 640, 512, 384)


def _mm(a, b, *, ta=False, tb=False, extra=(), epi=None, outs=(F32,), after=None, name):
    m, k = (a.shape[1], a.shape[0]) if ta else a.shape
    n = b.shape[0] if tb else b.shape[1]
    assert k == (b.shape[1] if tb else b.shape[0])
    bm, bn, bk = _pick(m, _MM_BLOCKS), _pick(n, _MM_BLOCKS), _pick(k, (2048,) + _MM_BLOCKS)
    nk = k // bk
    n_e, n_o = len(extra), len(outs)
    behind = [] if after is None else [after]
    dims = (((0 if ta else 1,), (1 if tb else 0,)), ((), ()))

    def body(a_ref, b_ref, *rest):
        ex, orefs, acc = rest[:n_e], rest[n_e + len(behind):n_e + len(behind) + n_o], rest[-1]
        kk = pl.program_id(2)

        @pl.when(kk == 0)
        def _():
            acc[...] = jnp.zeros_like(acc)

        acc[...] += _dot(a_ref[...], b_ref[...], dims)

        @pl.when(kk == nk - 1)
        def _():
            r = acc[...]
            res = epi(r, *[e[...] for e in ex]) if epi is not None else (r,)
            for o, v in zip(orefs, res):
                o[...] = v.astype(o.dtype)

    a_spec = pl.BlockSpec((bk, bm), lambda i, j, kk: (kk, i)) if ta else pl.BlockSpec((bm, bk), lambda i, j, kk: (i, kk))
    b_spec = pl.BlockSpec((bn, bk), lambda i, j, kk: (j, kk)) if tb else pl.BlockSpec((bk, bn), lambda i, j, kk: (kk, j))
    t_spec = pl.BlockSpec((bm, bn), lambda i, j, kk: (i, j))
    est = (_nbytes((bm, bk), a.dtype) + _nbytes((bk, bn), b.dtype) + sum(_nbytes((bm, bn), e.dtype) for e in extra)
           + sum(_nbytes((bm, bn), o) for o in outs)) * 2 + 2 * _nbytes((bm, bn), F32)
    res = pl.pallas_call(
        body, grid=(m // bm, n // bn, nk), in_specs=[a_spec, b_spec] + [t_spec] * n_e + [pl.BlockSpec(memory_space=pl.ANY)] * len(behind),
        out_specs=[t_spec] * n_o, out_shape=[_SDS((m, n), o) for o in outs], scratch_shapes=[pltpu.VMEM((bm, bn), F32)], name=name,
        compiler_params=_params(("parallel", "parallel", "arbitrary"), est))(a, b, *extra, *behind)
    return res[0] if n_o == 1 else res


def _add_to(acc, r):
    return (acc + r,)


def _alibi_bias(dilation):
    slopes = 2.0 ** (-8.0 * (np.arange(_HEADS) + 1) / _HEADS)
    i = np.arange(_ABLK)[:, None]
    j = np.arange(_ABLK)[None, :]
    cur = np.where(i - j >= 0, -slopes[:, None, None] * ((i - j) * dilation), _NEG)
    prev = np.where(j >= i, -slopes[:, None, None] * ((i - j + _ABLK) * dilation), _NEG)
    return jnp.asarray(np.concatenate([prev, cur], axis=2), F32)


def _strided(a, d):
    return a.reshape(a.shape[0] // d, d * a.shape[1])


def _head(h):
    return slice(h * _HDIM, (h + 1) * _HDIM)


def _pair(pr):
    return slice(pr * _LANES, (pr + 1) * _LANES)


def _low_lanes(shape):
    return lax.broadcasted_iota(jnp.int32, shape, 1) < _HDIM


def _halves(v, low):
    z = jnp.zeros_like(v)
    return jnp.where(low, v, z), jnp.where(low, z, v)


def _no_prev_mask(first):
    return jnp.logical_and(first, lax.broadcasted_iota(jnp.int32, (_ABLK, 2 * _ABLK), 1) < _ABLK)


def _lane_spec(nb):
    return pl.BlockSpec((_ABLK, _LANES), lambda r, j: (jnp.minimum(j, nb - 1), r))


def _expand_heads(v):
    low = _low_lanes((v.shape[0], _LANES))
    return jnp.concatenate([jnp.where(low, v[:, 2 * pr:2 * pr + 1], v[:, 2 * pr + 1:2 * pr + 2]) for pr in range(_HEADS // 2)], axis=1)


def _attn_specs(nb, n_parts):
    def cur(p):
        return pl.BlockSpec((_ABLK, _AW), lambda r, j: (jnp.minimum(j, nb - 1), r * n_parts + p))

    def prev(p):
        return pl.BlockSpec((_ABLK, _AW), lambda r, j: (jnp.clip(j - 1, 0, nb - 1), r * n_parts + p))
    return cur, prev


def _attn_fwd(qkv, dilation, *, name):
    t = qkv.shape[0]
    nb = t // dilation // _ABLK
    bias = _alibi_bias(dilation)
    scale = _HDIM ** -0.5

    def body(q_ref, kc_ref, kp_ref, vc_ref, vp_ref, b_ref, o_ref, l_ref):
        no_prev = _no_prev_mask(pl.program_id(1) == 0)
        low = _low_lanes((_ABLK, _LANES))
        l_ref[...] = jnp.zeros_like(l_ref)
        for pr in range(_HEADS // 2):
            sl = _pair(pr)
            k2 = jnp.concatenate([kp_ref[:, sl], kc_ref[:, sl]], axis=0)
            v2 = jnp.concatenate([vp_ref[:, sl], vc_ref[:, sl]], axis=0)
            o2 = []
            for h, qh in zip((2 * pr, 2 * pr + 1), _halves(q_ref[:, sl], low)):
                s = jnp.where(no_prev, _NEG, _dot(qh, k2, _NT) * scale + b_ref[h])
                m = jnp.max(s, axis=-1, keepdims=True)
                p = jnp.exp(s - m)
                den = jnp.sum(p, axis=-1, keepdims=True)
                o2.append(_dot(p, v2) / den)
                l_ref[:, h:h + 1] = m + jnp.log(den)
            o_ref[:, sl] = jnp.where(low, o2[0], o2[1]).astype(o_ref.dtype)

    cur, prev = _attn_specs(nb, 3)
    cur1, _ = _attn_specs(nb, 1)
    bspec = pl.BlockSpec((_HEADS, _ABLK, 2 * _ABLK), lambda r, j: (0, 0, 0))
    sv = _strided(qkv, dilation)
    o, l = pl.pallas_call(
        body, grid=(dilation, nb), in_specs=[cur(0), cur(1), prev(1), cur(2), prev(2), bspec],
        out_specs=[cur1(0), _lane_spec(nb)],
        out_shape=[_SDS((t // dilation, dilation * _AW), _BF), _SDS((t // dilation, dilation * _LANES), F32)], name=name,
        compiler_params=_params(("parallel", "arbitrary"), 16 << 20))(sv, sv, sv, sv, sv, bias)
    return o.reshape(t, _AW), l.reshape(t, _LANES)


def _attn_bwd(qkv, do, ld, dilation, *, name):
    t = qkv.shape[0]
    nb = t // dilation // _ABLK
    bias = _alibi_bias(dilation)
    scale = _HDIM ** -0.5

    def body(q_ref, kc_ref, kp_ref, vc_ref, vp_ref, do_ref, ld_ref, b_ref, dq_ref, dk_ref, dv_ref, ck, cv):
        n = pl.program_id(1)

        @pl.when(n == 0)
        def _():
            ck[...] = jnp.zeros_like(ck)
            cv[...] = jnp.zeros_like(cv)

        @pl.when(n < nb)
        def _():
            low = _low_lanes((_ABLK, _LANES))
            no_prev = _no_prev_mask(n == 0)
            for pr in range(_HEADS // 2):
                sl = _pair(pr)
                k2 = jnp.concatenate([kp_ref[:, sl], kc_ref[:, sl]], axis=0)
                v2 = jnp.concatenate([vp_ref[:, sl], vc_ref[:, sl]], axis=0)
                dos = _halves(do_ref[:, sl], low)
                dq2, dk2, dv2 = [], 0.0, 0.0
                for i, qh in enumerate(_halves(q_ref[:, sl], low)):
                    h = 2 * pr + i
                    lrow, dsum = ld_ref[:, h:h + 1], ld_ref[:, _HEADS + h:_HEADS + h + 1]
                    p = jnp.exp(jnp.where(no_prev, _NEG, _dot(qh, k2, _NT) * scale + b_ref[h]) - lrow)
                    ds = (p * (_dot(dos[i], v2, _NT) - dsum)).astype(_BF)
                    dq2.append(_dot(ds, k2))
                    dk2, dv2 = dk2 + _dot(ds, qh, _TN), dv2 + _dot(p, dos[i], _TN)
                dq_ref[:, sl] = (jnp.where(low, dq2[0], dq2[1]) * scale).astype(dq_ref.dtype)
                dk_ref[:, sl] = (ck[:, sl] + dk2[:_ABLK] * scale).astype(dk_ref.dtype)
                dv_ref[:, sl] = (cv[:, sl] + dv2[:_ABLK]).astype(dv_ref.dtype)
                ck[:, sl] = dk2[_ABLK:] * scale
                cv[:, sl] = dv2[_ABLK:]

        @pl.when(n == nb)
        def _():
            dk_ref[...] = ck[...].astype(dk_ref.dtype)
            dv_ref[...] = cv[...].astype(dv_ref.dtype)

    cur, prev = _attn_specs(nb, 3)
    cur1, prev1 = _attn_specs(nb, 1)
    bspec = pl.BlockSpec((_HEADS, _ABLK, 2 * _ABLK), lambda r, j: (0, 0, 0))
    sv, dov, ldv = (_strided(a, dilation) for a in (qkv, do, ld))
    dqkv = pl.pallas_call(
        body, grid=(dilation, nb + 1),
        in_specs=[cur(0), cur(1), prev(1), cur(2), prev(2), cur1(0), _lane_spec(nb), bspec],
        out_specs=[cur1(0), prev1(0), prev1(0)], out_shape=[_SDS(dov.shape, _BF)] * 3, name=name,
        scratch_shapes=[pltpu.VMEM((_ABLK, _AW), F32)] * 2,
        compiler_params=_params(("parallel", "arbitrary"), 16 << 20))(sv, sv, sv, sv, sv, dov, ldv, bias)
    return [a.reshape(t, _AW) for a in dqkv]


def _ssd_in_specs(ch):
    return dict(
        xs=pl.BlockSpec((_CHUNK, _AW), lambda c: (ch(c), 0)),
        bc=pl.BlockSpec((_CHUNK, 2 * _GROUPS * _NSTATE), lambda c: (ch(c), _AW // (2 * _GROUPS * _NSTATE))),
        lane=pl.BlockSpec((_CHUNK, _LANES), lambda c: (ch(c), 0)),
        arow=pl.BlockSpec((_HEADS, 1, _CHUNK), lambda c: (0, 0, ch(c))),
        st=pl.BlockSpec((1, _HEADS // 2, _NSTATE, _LANES), lambda c: (ch(c), 0, 0, 0)),
    )


def _decay(a_col, a_row):
    i0 = lax.broadcasted_iota(jnp.int32, (_CHUNK, _CHUNK), 0)
    i1 = lax.broadcasted_iota(jnp.int32, (_CHUNK, _CHUNK), 1)
    return jnp.where(i0 >= i1, jnp.exp(a_col - a_row), 0.0), jnp.where(i1 >= i0, jnp.exp(a_row - a_col), 0.0)


def _rsum(v):
    return jnp.sum(v, axis=-1, keepdims=True)


def _ssd_fwd(act, dt, acum, a_row, *, name):
    t = act.shape[0]
    nc = t // _CHUNK
    sp = _ssd_in_specs(lambda c: c)
    gw = _GROUPS * _NSTATE

    def body(xs_ref, bc_ref, dt_ref, ac_ref, ar_ref, y_ref, sall_ref, st):
        @pl.when(pl.program_id(0) == 0)
        def _():
            st[...] = jnp.zeros_like(st)

        low = _low_lanes((_CHUNK, _LANES))
        for g in range(_GROUPS):
            bg = bc_ref[:, g * _NSTATE:(g + 1) * _NSTATE]
            cg = bc_ref[:, gw + g * _NSTATE:gw + (g + 1) * _NSTATE].astype(_BF)
            cb = _dot(cg, bg, _NT)
            for pr in range(g * _HPG // 2, (g + 1) * _HPG // 2):
                ha, hb = 2 * pr, 2 * pr + 1
                a_a, a_b = ac_ref[:, ha:ha + 1], ac_ref[:, hb:hb + 1]
                x = (xs_ref[:, _pair(pr)] * jnp.where(low, dt_ref[:, ha:ha + 1], dt_ref[:, hb:hb + 1])).astype(_BF)
                lm_a, _ = _decay(a_a, ar_ref[ha])
                lm_b, _ = _decay(a_b, ar_ref[hb])
                sv = st[pr]
                sall_ref[0, pr] = sv
                yd = jnp.where(low, _dot(cb * lm_a, x), _dot(cb * lm_b, x))
                y_ref[:, _pair(pr)] = yd + jnp.where(low, jnp.exp(a_a), jnp.exp(a_b)) * _dot(cg, sv)
                al_a, al_b = jnp.min(a_a, axis=0, keepdims=True), jnp.min(a_b, axis=0, keepdims=True)
                st[pr] = (jnp.where(low, jnp.exp(al_a), jnp.exp(al_b)) * sv
                          + jnp.where(low, _dot(bg * jnp.exp(al_a - a_a), x, _TN), _dot(bg * jnp.exp(al_b - a_b), x, _TN)))

    return pl.pallas_call(
        body, grid=(nc,), in_specs=[sp['xs'], sp['bc'], sp['lane'], sp['lane'], sp['arow']],
        out_specs=[sp['xs'], sp['st']], out_shape=[_SDS((t, _AW), F32), _SDS((nc, _HEADS // 2, _NSTATE, _LANES), F32)],
        scratch_shapes=[pltpu.VMEM((_HEADS // 2, _NSTATE, _LANES), F32)], name=name,
        compiler_params=_params(("arbitrary",), 16 << 20))(act, act, dt, acum, a_row)


def _ssd_bwd(act, dt, acum, a_row, sall, dy, *, name):
    t = act.shape[0]
    nc = t // _CHUNK
    sp = _ssd_in_specs(lambda c: nc - 1 - c)
    gw = _GROUPS * _NSTATE

    def body(xs_ref, bc_ref, dt_ref, ac_ref, ar_ref, sall_ref, dy_ref, dxs_ref, dbc_ref, ddt_ref, da_ref, dst):
        @pl.when(pl.program_id(0) == 0)
        def _():
            dst[...] = jnp.zeros_like(dst)

        ddt_ref[...] = jnp.zeros_like(ddt_ref)
        da_ref[...] = jnp.zeros_like(da_ref)
        row = lax.broadcasted_iota(jnp.int32, (_CHUNK, 1), 0)
        low = _low_lanes((_CHUNK, _LANES))
        for g in range(_GROUPS):
            bg = bc_ref[:, g * _NSTATE:(g + 1) * _NSTATE]
            bgb = bg.astype(_BF)
            cg = bc_ref[:, gw + g * _NSTATE:gw + (g + 1) * _NSTATE].astype(_BF)
            cb, cbt = _dot(cg, bgb, _NT), _dot(bgb, cg, _NT)
            dcb = jnp.zeros((_CHUNK, _CHUNK), F32)
            dbg = jnp.zeros((_CHUNK, _NSTATE), F32)
            dcg = jnp.zeros((_CHUNK, _NSTATE), F32)
            for pr in range(g * _HPG // 2, (g + 1) * _HPG // 2):
                heads = (2 * pr, 2 * pr + 1)
                a_cols = [ac_ref[:, h:h + 1] for h in heads]
                dt_pair = jnp.where(low, dt_ref[:, heads[0]:heads[0] + 1], dt_ref[:, heads[1]:heads[1] + 1])
                xsv = xs_ref[:, _pair(pr)]
                x = xsv * dt_pair
                xb = x.astype(_BF)
                xhs = _halves(xb, low)
                dyv = dy_ref[:, _pair(pr)]
                dyb = dyv.astype(_BF)
                dyhs = _halves(dyb, low)
                sv, dsv = sall_ref[0, pr], dst[pr]
                svb, dsb = sv.astype(_BF), dsv.astype(_BF)
                a_lasts = [jnp.min(a, axis=0, keepdims=True) for a in a_cols]
                e_pair = jnp.where(low, jnp.exp(a_cols[0]), jnp.exp(a_cols[1]))
                el_pair = jnp.where(low, jnp.exp(a_lasts[0]), jnp.exp(a_lasts[1]))
                yo = e_pair * _dot(cg, svb)
                dxg, bwds, das = [], [], []
                for i, h in enumerate(heads):
                    lm, lmt = _decay(a_cols[i], ar_ref[h])
                    gm, gmt = cb * lm, cbt * lmt
                    w_col = jnp.exp(a_lasts[i] - a_cols[i])
                    bwds.append(_dot(bg * w_col, dsb))
                    dxg.append(_dot(gm, dyb, _TN))
                    dg, dgt = _dot(dyhs[i], xb, _NT), _dot(xhs[i], dyb, _NT)
                    dcb = dcb + dg * lm
                    dbg = dbg + w_col * _dot(xhs[i], dsb, _NT)
                    das.append(_rsum(dg * gm) - _rsum(dgt * gmt))
                bwd = jnp.where(low, bwds[0], bwds[1])
                dx = jnp.where(low, dxg[0], dxg[1]) + bwd
                edy = (e_pair * dyv).astype(_BF)
                dcg = dcg + _dot(edy, svb, _NT)
                zs, yos, sds, dts = (_halves(v, low) for v in (x * bwd, dyv * yo, sv * dsv, dx * xsv))
                for i, h in enumerate(heads):
                    z = _rsum(zs[i])
                    da_last = jnp.sum(z, axis=0, keepdims=True) + jnp.exp(a_lasts[i]) * jnp.sum(_rsum(sds[i]), axis=0, keepdims=True)
                    da_ref[:, h:h + 1] = das[i] + _rsum(yos[i]) - z + jnp.where(row == _CHUNK - 1, da_last, 0.0)
                    ddt_ref[:, h:h + 1] = _rsum(dts[i])
                dxs_ref[:, _pair(pr)] = dx * dt_pair
                dst[pr] = el_pair * dsv + _dot(cg, edy, _TN)
            dbc_ref[:, g * _NSTATE:(g + 1) * _NSTATE] = dbg + _dot(dcb, cg, _TN)
            dbc_ref[:, gw + g * _NSTATE:gw + (g + 1) * _NSTATE] = dcg + _dot(dcb, bgb)

    ch = lambda c: nc - 1 - c
    wide = pl.BlockSpec((_CHUNK, 2 * gw), lambda c: (ch(c), 0))
    return pl.pallas_call(
        body, grid=(nc,), in_specs=[sp['xs'], sp['bc'], sp['lane'], sp['lane'], sp['arow'], sp['st'], sp['xs']],
        out_specs=[sp['xs'], wide, sp['lane'], sp['lane']],
        out_shape=[_SDS((t, _AW), F32), _SDS((t, 2 * gw), F32), _SDS((t, _LANES), F32), _SDS((t, _LANES), F32)],
        scratch_shapes=[pltpu.VMEM((_HEADS // 2, _NSTATE, _LANES), F32)], name=name,
        compiler_params=_params(("arbitrary",), 16 << 20))(act, act, dt, acum, a_row, sall, dy)


def _scan_rows(v, reverse):
    r = lax.broadcasted_iota(jnp.int32, v.shape, 0)
    for s in (1, 2, 4, 8, 16, 32, 64):
        if reverse:
            v = v + jnp.where(r < _CHUNK - s, pltpu.roll(v, _CHUNK - s, 0), 0.0)
        else:
            v = v + jnp.where(r >= s, pltpu.roll(v, s, 0), 0.0)
    return v


def _softplus(x):
    return jnp.maximum(x, 0.0) + jnp.log(1.0 + jnp.exp(-jnp.abs(x)))


def _sigmoid(x):
    return 1.0 / (1.0 + jnp.exp(-x))


def _silu(x):
    return x * _sigmoid(x)


def _dsilu(x):
    s = _sigmoid(x)
    return s * (1.0 + x * (1.0 - s))


def _shift(a, j):
    if j == 0:
        return a
    if j > 0:
        return jnp.pad(a, ((j, 0), (0, 0)))[:-j]
    return jnp.pad(a, ((0, -j), (0, 0)))[-j:]


def _lanes(a):
    return jnp.pad(a, (0, _LANES - a.shape[0])).reshape(1, _LANES)


def _layer_fwd(x, p, l):
    cch = p['conv_w'].shape[1]
    sv = {}
    h1 = _rms_fwd(x, p['ln1_g'], name=f"ln1_fwd_{l}")
    qkv = _mm(h1, p['w_qkv'], outs=(_BF,), name=f"in_proj_qkv_{l}")
    xbc = _mm(h1, p['w_xbc'], name=f"in_proj_xbc_{l}")
    zdt = _mm(h1, p['w_zdt'], name=f"in_proj_zdt_{l}")
    z, dt_raw = (zdt, _AW, 0), (zdt, _LANES, _AW // _LANES)

    outs = []
    for dil in _DILATIONS:
        outs += _attn_fwd(qkv, dil, name=f"attn_fwd_d{dil}_{l}")

    def combine(o1, l1, o2, l2, o3, l3):
        m = jnp.maximum(jnp.maximum(l1, l2), l3)
        e1, e2, e3 = jnp.exp(l1 - m), jnp.exp(l2 - m), jnp.exp(l3 - m)
        tot = e1 + e2 + e3
        mixed = sum(_expand_heads(e / tot) * o.astype(F32) for e, o in ((e1, o1), (e2, o2), (e3, o3)))
        return mixed, m + jnp.log(tot)
    attn, lse = _rows(combine, outs, [], [(_AW, F32), (_LANES, F32)], tile=_tile_for(_AW), name=f"attn_combine_{l}")
    attn_n = _rms_fwd(attn, p['attn_norm_g'], name=f"attn_norm_fwd_{l}")

    us = [_shift(xbc, j) for j in range(_CONV_K)]

    def conv(u0, u1, u2, u3, w, b):
        return _silu(w[0:1] * u3 + w[1:2] * u2 + w[2:3] * u1 + w[3:4] * u0 + b)
    act = _rows(conv, us, [p['conv_w'], p['conv_b'].reshape(1, cch)], [(cch, F32)], tile=_tile_for(cch), name=f"conv_fwd_{l}")[0]

    def dtf(raw, bias, alog):
        dt = _softplus(raw + bias)
        return dt, _scan_rows(dt * -jnp.exp(alog), False)
    dt, acum = _rows(dtf, [dt_raw], [_lanes(p['dt_bias']), _lanes(p['a_log'])], [(_LANES, F32), (_LANES, F32)],
                     tile=_CHUNK, name=f"dt_fwd_{l}")
    a_row = acum[:, :_HEADS].T[:, None, :]
    y_ssd, sall = _ssd_fwd(act, dt, acum, a_row, name=f"ssd_fwd_{l}")
    dskip = jnp.repeat(p['d_skip'], _HDIM).reshape(1, _AW)
    xs = (act, _AW, 0)

    def gate(y, xs, z, dsk):
        return (y + dsk * xs) * _silu(z)
    y2 = _rows(gate, [y_ssd, xs, z], [dskip], [(_AW, F32)], tile=_tile_for(_AW), name=f"gate_fwd_{l}")[0]
    y_n = _rms_fwd(y2, p['ssd_norm_g'], groups=_GROUPS, name=f"ssd_norm_fwd_{l}")

    mix = jnp.concatenate([attn_n, y_n], axis=1)
    sv.update(x=x, h1=h1, qkv=qkv, zdt=zdt, us=us, attn=attn, lse=lse, act=act, dt=dt, acum=acum, a_row=a_row,
              sall=sall, y_ssd=y_ssd, dskip=dskip, y2=y2, mix=mix)
    return mix, sv


def _layer_fwd_mlp(p, sv, l):
    x2 = _mm(sv['mix'], p['w_out'], extra=(sv['x'],), epi=_add_to, name=f"out_proj_{l}")
    h2 = _rms_fwd(x2, p['ln2_g'], name=f"ln2_fwd_{l}")
    a = _mm(h2, p['w_mlp_in'], epi=lambda acc: (jnp.square(jnp.maximum(acc, 0.0)),), outs=(_BF,), name=f"mlp_in_{l}")
    x3 = _mm(a, p['w_mlp_out'], extra=(x2,), epi=_add_to, name=f"mlp_out_{l}")
    sv.update(x2=x2, h2=h2, a=a)
    return x3


def _layer_bwd(dx3, p, sv, l, send, after):
    cch = p['conv_w'].shape[1]
    g = {}
    dx3b = dx3.astype(_BF)
    du = _mm(dx3b, p['w_mlp_out'], tb=True, extra=(sv['a'],), outs=(_BF,), after=after,
             epi=lambda acc, a: (acc * 2.0 * jnp.sqrt(a.astype(F32)),), name=f"mlp_out_dx_{l}")
    g['w_mlp_out'] = _mm(sv['a'], dx3b, ta=True, outs=(_BF,), name=f"mlp_out_dw_{l}")
    g['w_mlp_in'] = _mm(sv['h2'], du, ta=True, outs=(_BF,), name=f"mlp_in_dw_{l}")
    sent = send(('w_mlp_out', 'w_mlp_in'), g)
    dh2 = _mm(du, p['w_mlp_in'], tb=True, after=sent, name=f"mlp_in_dx_{l}")
    dx2, g['ln2_g'] = _rms_bwd(sv['x2'], dh2, p['ln2_g'], dx3, name=f"ln2_bwd_{l}")
    dx2b = dx2.astype(_BF)
    dmix = _mm(dx2b, p['w_out'], tb=True, name=f"out_proj_dx_{l}")
    g['w_out'] = _mm(sv['mix'], dx2b, ta=True, outs=(_BF,), name=f"out_proj_dw_{l}")

    def norm_bwd(attn, dy, lse, gn):
        dattn, dgn = _rms_bwd_tile(attn, dy, gn, 1)
        prod, low = dattn * attn, _low_lanes((attn.shape[0], _LANES))
        lane = lax.broadcasted_iota(jnp.int32, lse.shape, 1)
        ld = jnp.where(lane < _HEADS, lse, 0.0)
        for pr in range(_HEADS // 2):
            for i, part in enumerate(_halves(prod[:, _pair(pr)], low)):
                ld = jnp.where(lane == _HEADS + 2 * pr + i, _rsum(part), ld)
        return dattn, ld, dgn
    dattn, ld, gn_sum = _rows(norm_bwd, [sv['attn'], (dmix, _AW, 0), sv['lse']], [p['attn_norm_g'].reshape(1, _AW)],
                              [(_AW, _BF), (_LANES, F32)], [_AW], tile=128, name=f"attn_norm_bwd_{l}")
    g['attn_norm_g'] = gn_sum.sum(axis=0)
    parts = [_attn_bwd(sv['qkv'], dattn, ld, dil, name=f"attn_bwd_d{dil}_{l}") for dil in _DILATIONS]

    def branch_sum(*t):
        t = [a.astype(F32) for a in t]
        return jnp.concatenate([t[i] + t[3 + i] + t[6 + i] for i in range(3)], axis=1)
    dqkv = _rows(branch_sum, [a for pr in parts for a in pr], [], [(3 * _AW, _BF)], tile=128, name=f"attn_bwd_sum_{l}")[0]

    xs, z, dt_raw = (sv['act'], _AW, 0), (sv['zdt'], _AW, 0), (sv['zdt'], _LANES, _AW // _LANES)

    def gate_bwd(y2, dy, y, xs, z, dsk, gn):
        dy2, dgn = _rms_bwd_tile(y2, dy, gn, _GROUPS)
        dy1 = dy2 * _silu(z)
        return dy1, dsk * dy1, dy2 * (y + dsk * xs) * _dsilu(z), dy1 * xs, dgn
    dy1, dxs_skip, dz, dsk_sum, gn_sum = _rows(
        gate_bwd, [sv['y2'], (dmix, _AW, 1), sv['y_ssd'], xs, z], [sv['dskip'], p['ssd_norm_g'].reshape(1, _AW)],
        [(_AW, F32), (_AW, F32), (_AW, _BF)], [_AW, _AW], tile=128, name=f"gate_bwd_{l}")
    g['ssd_norm_g'] = gn_sum.sum(axis=0)
    g['d_skip'] = dsk_sum.sum(axis=0).reshape(_HEADS, _HDIM).sum(axis=1)
    dxs, dbc, ddt, da = _ssd_bwd(sv['act'], sv['dt'], sv['acum'], sv['a_row'], sv['sall'], dy1, name=f"ssd_bwd_{l}")

    def dtb(da, ddtx, raw, dt, dz, bias, alog):
        a = -jnp.exp(alog)
        dda = _scan_rows(da, True)
        draw = (dda * a + ddtx) * _sigmoid(raw + bias)
        return jnp.concatenate([dz, draw.astype(dz.dtype)], axis=1), draw, dda * dt * a
    dzdt, dbias, dalog = _rows(dtb, [da, ddt, dt_raw, sv['dt'], dz], [_lanes(p['dt_bias']), _lanes(p['a_log'])],
                               [(_AW + _LANES, _BF)], [_LANES, _LANES], tile=_CHUNK, name=f"dt_bwd_{l}")
    g['dt_bias'], g['a_log'] = dbias.sum(axis=0)[:_HEADS], dalog.sum(axis=0)[:_HEADS]
    us = sv['us']

    def conv_bwd1(u0, u1, u2, u3, dxs, dbc, dxk, w, b):
        pre = w[0:1] * u3 + w[1:2] * u2 + w[2:3] * u1 + w[3:4] * u0 + b
        dp = jnp.concatenate([dxs + dxk, dbc], axis=1) * _dsilu(pre)
        return dp, dp * u3, dp * u2, dp * u1, dp * u0, dp
    dpre, *dws = _rows(conv_bwd1, [*us, dxs, dbc, dxs_skip], [p['conv_w'], p['conv_b'].reshape(1, cch)], [(cch, F32)], [cch] * 5,
                       tile=128, name=f"conv_bwd_pre_{l}")
    g['conv_w'] = jnp.stack([dws[i].sum(axis=0) for i in range(_CONV_K)])
    g['conv_b'] = dws[4].sum(axis=0)

    def conv_bwd2(p0, p1, p2, p3, w):
        return w[3:4] * p0 + w[2:3] * p1 + w[1:2] * p2 + w[0:1] * p3
    dxbc = _rows(conv_bwd2, [_shift(dpre, -j) for j in range(_CONV_K)], [p['conv_w']], [(cch, _BF)], tile=_tile_for(cch),
                 name=f"conv_bwd_in_{l}")[0]
    h1 = sv['h1']
    g_qkv = _mm(h1, dqkv, ta=True, outs=(_BF,), name=f"in_proj_qkv_dw_{l}")
    g_xbc = _mm(h1, dxbc, ta=True, outs=(_BF,), name=f"in_proj_xbc_dw_{l}")
    g_zdt = _mm(h1, dzdt, ta=True, outs=(_BF,), name=f"in_proj_zdt_dw_{l}")
    g['w_in'] = jnp.concatenate([g_qkv, g_zdt[:, :_AW], g_xbc, g_zdt[:, _AW:_AW + _HEADS]], axis=1)
    sent = send(('w_out', 'w_in'), g)
    for n in _BIG:
        del g[n]
    dh1 = _mm(dqkv, p['w_qkv'], tb=True, after=sent, name=f"in_proj_qkv_dx_{l}")
    dh1 = _mm(dxbc, p['w_xbc'], tb=True, extra=(dh1,), epi=_add_to, name=f"in_proj_xbc_dx_{l}")
    dh1 = _mm(dzdt, p['w_zdt'], tb=True, extra=(dh1,), epi=_add_to, name=f"in_proj_zdt_dx_{l}")
    dx, g['ln1_g'] = _rms_bwd(sv['x'], dh1, p['ln1_g'], dx2, name=f"ln1_bwd_{l}")
    return dx, g


def _loss_bwd(x, g, tgt):
    w = x.shape[1]
    tile = _tile_for(w)

    def fn(x, tgt, g):
        r = _rstd(x)
        xh = x * r
        e = xh * g - tgt
        gd = e * (g / w)
        dx = r * (gd - xh * jnp.mean(gd * xh, axis=-1, keepdims=True))
        rowloss = 0.5 * jnp.mean(e * e, axis=-1, keepdims=True)
        return dx, (e / w) * xh, jnp.broadcast_to(rowloss, (tile, _LANES))
    dx, dg, ls = _rows(fn, [x, tgt], [g.reshape(1, w)], [(w, F32)], [w, _LANES], tile=tile, name="loss_head")
    return dx, dg.sum(axis=0), ls[:, 0].sum()


def _adamw(w, g, m, v, *, name):
    def fn(w, g, m, v):
        m2 = _B1 * m + (1.0 - _B1) * g
        v2 = _B2 * v + (1.0 - _B2) * jnp.square(g)
        m_hat = m2 / (1.0 - _B1 ** _STEP)
        v_hat = v2 / (1.0 - _B2 ** _STEP)
        return -_LR * (m_hat / (jnp.sqrt(v_hat) + _AEPS) + _WD * w), m2, v2
    width = w.shape[-1]
    flat = [a.reshape(-1, width) for a in (w, g, m, v)]
    tile = _pick(flat[0].shape[0], (_tile_for(width), 32, 8))
    res = _rows(fn, flat, [], [(width, F32)] * 3, tile=tile, name=name)
    return [r.reshape(w.shape) for r in res]


_HBM = pl.BlockSpec(memory_space=pltpu.HBM)


def _place():
    x, y, c = lax.axis_index("x"), lax.axis_index("y"), lax.axis_index("c")
    other_chips = [(1 - x, y), (x, 1 - y), (1 - x, 1 - y)]
    return x, y, c, other_chips


def _remote(src, dst, sems, i, dev):
    return pltpu.make_async_remote_copy(src_ref=src, dst_ref=dst, send_sem=sems[0].at[i], recv_sem=sems[1].at[i],
                                        device_id=dev, device_id_type=_MESH)


def _exchange8(v, *, reduce, after=None, name):
    r, w = v.shape
    behind = [] if after is None else [after]

    def body(v_ref, *rest):
        all_ref, rest = rest[len(behind)], rest[len(behind) + 1:]
        sems = rest[-2:]
        x, y, c, _ = _place()
        me = 4 * x + 2 * y + c
        all_ref[me] = v_ref[...]
        flips = [((d >> 2) & 1, (d >> 1) & 1, d & 1) for d in range(1, 8)]
        sends = [_remote(v_ref, all_ref.at[me], sems, i, (x ^ fx, y ^ fy, c ^ fc)) for i, (fx, fy, fc) in enumerate(flips)]
        for cp in sends:
            cp.start()
        for i, (fx, fy, fc) in enumerate(flips):
            _remote(v_ref, all_ref.at[me ^ (4 * fx + 2 * fy + fc)], sems, i, (x ^ fx, y ^ fy, c ^ fc)).wait_recv()
        for cp in sends:
            cp.wait_send()
        if reduce:
            acc = all_ref[0]
            for s in range(1, 8):
                acc = acc + all_ref[s]
            rest[0][...] = acc

    vm = pl.BlockSpec(memory_space=pltpu.VMEM)
    out_shape = [_SDS((8, r, w), v.dtype)] + ([_SDS((r, w), v.dtype)] if reduce else [])
    res = pl.pallas_call(body, in_specs=[vm] + [_ANY] * len(behind), out_specs=[vm] * len(out_shape), out_shape=out_shape, name=name,
                         scratch_shapes=[pltpu.SemaphoreType.DMA((7,)), pltpu.SemaphoreType.DMA((7,))],
                         compiler_params=pltpu.CompilerParams(vmem_limit_bytes=int(32 << 20)))(v, *behind)
    return res[1] if reduce else res[0]


_SEM = pl.BlockSpec(memory_space=pltpu.SEMAPHORE)
_ANY = pl.BlockSpec(memory_space=pl.ANY)
_EFFECT = pltpu.SideEffectType.DATAFLOW_SIDE_EFFECTING


def _in_hbm(a):
    return pltpu.with_memory_space_constraint(a, pltpu.HBM)


def _send_start(name, srcs, land_shapes, plan, n_sends, after):
    ns, nl = len(srcs), len(land_shapes)

    def body(*refs):
        ins, lands, sems = refs[:ns], refs[ns:ns + nl], refs[ns + nl + 1:ns + nl + 3]
        x, y, c, chips = _place()
        for i, (s, d, dev) in enumerate(plan(x, y, c, chips, ins, lands)[0]):
            _remote(s, d, sems, i, dev).start()
        refs[-1][...] = jnp.zeros_like(refs[-1])

    sem = pltpu.SemaphoreType.DMA((n_sends,))
    res = pl.pallas_call(
        body, name=name, in_specs=[_HBM] * (ns + nl) + [_ANY],
        out_shape=(sem, sem, *[pltpu.HBM(s.shape, s.dtype) for s in land_shapes], _SDS((8, _LANES), F32)),
        out_specs=(_SEM, _SEM, *[_HBM] * nl, pl.BlockSpec(memory_space=pltpu.VMEM)),
        input_output_aliases={ns + i: 2 + i for i in range(nl)},
        compiler_params=pltpu.CompilerParams(has_side_effects=_EFFECT))(
            *[_in_hbm(s) for s in srcs], *[_in_hbm(lax.empty(s.shape, s.dtype)) for s in land_shapes], after)
    return dict(sems=res[:2], srcs=srcs, lands=res[2:2 + nl], plan=plan), res[-1]


def _send_wait(name, h, after):
    ns, nl = len(h['srcs']), len(h['lands'])

    def body(*refs):
        ins, lands, sems = refs[:ns], refs[ns:ns + nl], refs[ns + nl:ns + nl + 2]
        x, y, c, chips = _place()
        sends, landings = h['plan'](x, y, c, chips, ins, lands)
        for i, (s, d, dev) in enumerate(sends):
            _remote(s, d, sems, i, dev).wait_send()
        for i, d in enumerate(landings):
            _remote(d, d, sems, i, sends[i][2]).wait_recv()

    return pl.pallas_call(
        body, name=name, in_specs=[_HBM] * (ns + nl) + [_SEM, _SEM, _ANY],
        out_shape=tuple(pltpu.HBM(a.shape, a.dtype) for a in h['lands']), out_specs=tuple([_HBM] * nl),
        input_output_aliases={ns + i: i for i in range(nl)},
        compiler_params=pltpu.CompilerParams(has_side_effects=_EFFECT))(
            *[_in_hbm(s) for s in h['srcs']], *h['lands'], *h['sems'], after)


def _gather_plan(items):
    def plan(x, y, c, chips, ins, lands):
        k = 2 * x + y
        sends = [(ins[si].at[l], lands[t].at[k], (px, py, c)) for t, (si, l) in enumerate(items) for px, py in chips]
        return sends, [lands[t].at[2 * px + py] for t in range(len(items)) for px, py in chips]
    return plan


_FLIPS = [((d >> 2) & 1, (d >> 1) & 1, d & 1) for d in range(1, 8)]


def _reduce_plan(halves):
    def plan(x, y, c, chips, ins, lands):
        sends, landings = [], []
        for t, hf in enumerate(halves):
            for i, (fx, fy, fc) in enumerate(_FLIPS):
                px, py, pc = x ^ fx, y ^ fy, c ^ fc
                sends.append((ins[t].at[2 * px + py, pl.ds(pc * hf, hf)], lands[t].at[i], (px, py, pc)))
                landings.append(lands[t].at[i])
        return sends, landings
    return plan


def _swap(name, srcs, out_shapes, plan, n_sends):
    n = len(srcs)

    def body(*refs):
        ins, outs, sems = refs[:n], refs[n:n + len(out_shapes)], refs[-2:]
        x, y, c, chips = _place()
        sends, landings = plan(x, y, c, chips, ins, outs)
        out = [_remote(s, d, sems, i, dev) for i, (s, d, dev) in enumerate(sends)]
        for cp in out:
            cp.start()
        for i, d in enumerate(landings):
            _remote(d, d, sems, i, sends[i][2]).wait_recv()
        for cp in out:
            cp.wait_send()

    return pl.pallas_call(
        body, in_specs=[_HBM] * n, out_specs=[_HBM] * len(out_shapes), out_shape=out_shapes, name=name,
        scratch_shapes=[pltpu.SemaphoreType.DMA((n_sends,)), pltpu.SemaphoreType.DMA((n_sends,))])(*srcs)


def _sum_owned(grads, landed, c, k, names):
    def sum8(*parts):
        acc = parts[0].astype(F32)
        for p in parts[1:]:
            acc = acc + p.astype(F32)
        return acc
    outs = []
    for g, got, name in zip(grads, landed, names):
        hf, b = got.shape[1:]
        own = lax.dynamic_slice_in_dim(lax.dynamic_index_in_dim(g, k, axis=0, keepdims=False), c * hf, hf, axis=0)
        outs.append(_rows(sum8, [own] + [got[i] for i in range(len(_FLIPS))], [], [(b, F32)], tile=_pick(hf, (_tile_for(b), 32)),
                          name=f"grad_sum_{name}")[0])
    return outs


def _share_halves(mine, c):
    n = len(mine)

    def plan(x, y, c_, chips, ins, outs):
        return [(ins[t], outs[t], (x, y, 1 - c_)) for t in range(n)], [outs[t] for t in range(n)]
    theirs = _swap("grad_share_cores", mine, [_SDS(h.shape, F32) for h in mine], plan, n)
    return [jnp.where(c == 0, jnp.concatenate([a, b], axis=0), jnp.concatenate([b, a], axis=0)) for a, b in zip(mine, theirs)]


_BIG = ("w_in", "w_out", "w_mlp_in", "w_mlp_out")
_SMALL = ("ln1_g", "conv_b", "dt_bias", "a_log", "d_skip", "attn_norm_g", "ssd_norm_g", "ln2_g", "final_norm_g")
_ORDER = ("ln1_g", "w_in", "conv_w", "conv_b", "dt_bias", "a_log", "d_skip", "attn_norm_g", "ssd_norm_g", "w_out", "ln2_g",
          "w_mlp_in", "w_mlp_out", "final_norm_g")


def _pack(parts, rows):
    flat = jnp.concatenate([p.reshape(-1) for p in parts])
    return jnp.pad(flat, (0, rows * _LANES - flat.shape[0])).reshape(rows, _LANES)


def _unpack(buf, like):
    flat, out, o = buf.reshape(-1), [], 0
    for p in like:
        out.append(flat[o:o + p.size].reshape(p.shape))
        o += p.size
    return out


def kernel(x, ln1_g, w_in, conv_w, conv_b, dt_bias, a_log, d_skip, attn_norm_g, ssd_norm_g, w_out, ln2_g, w_mlp_in, w_mlp_out, final_norm_g, loss_target, m_ln1_g, m_w_in, m_conv_w, m_conv_b, m_dt_bias, m_a_log, m_d_skip, m_attn_norm_g, m_ssd_norm_g, m_w_out, m_ln2_g, m_w_mlp_in, m_w_mlp_out, m_final_norm_g, v_ln1_g, v_w_in, v_conv_w, v_conv_b, v_dt_bias, v_a_log, v_d_skip, v_attn_norm_g, v_ssd_norm_g, v_w_out, v_ln2_g, v_w_mlp_in, v_w_mlp_out, v_final_norm_g):
    w = dict(ln1_g=ln1_g, w_in=w_in, conv_w=conv_w, conv_b=conv_b, dt_bias=dt_bias, a_log=a_log, d_skip=d_skip,
             attn_norm_g=attn_norm_g, ssd_norm_g=ssd_norm_g, w_out=w_out, ln2_g=ln2_g, w_mlp_in=w_mlp_in, w_mlp_out=w_mlp_out,
             final_norm_g=final_norm_g)
    m = dict(ln1_g=m_ln1_g, w_in=m_w_in, conv_w=m_conv_w, conv_b=m_conv_b, dt_bias=m_dt_bias, a_log=m_a_log, d_skip=m_d_skip,
             attn_norm_g=m_attn_norm_g, ssd_norm_g=m_ssd_norm_g, w_out=m_w_out, ln2_g=m_ln2_g, w_mlp_in=m_w_mlp_in,
             w_mlp_out=m_w_mlp_out, final_norm_g=m_final_norm_g)
    v = dict(ln1_g=v_ln1_g, w_in=v_w_in, conv_w=v_conv_w, conv_b=v_conv_b, dt_bias=v_dt_bias, a_log=v_a_log, d_skip=v_d_skip,
             attn_norm_g=v_attn_norm_g, ssd_norm_g=v_ssd_norm_g, w_out=v_w_out, ln2_g=v_ln2_g, w_mlp_in=v_w_mlp_in,
             w_mlp_out=v_w_mlp_out, final_norm_g=v_final_norm_g)
    depth, d_model = ln1_g.shape
    n_chips = 4
    c = lax.axis_index("c")
    chip = 2 * lax.axis_index("x") + lax.axis_index("y")
    in_proj = w_in.shape[2] * n_chips
    cch = conv_w.shape[2] * n_chips
    zdt_pad = _LANES - _HEADS

    cw = _exchange8(conv_w.reshape(depth * _CONV_K, -1), reduce=False, name="gather_conv_w")[0::2]
    conv_full = cw.reshape(n_chips, depth, _CONV_K, -1).transpose(1, 2, 0, 3).reshape(depth, _CONV_K, cch)
    own = [w[n].astype(_BF) for n in _BIG]
    is_own = (jnp.arange(n_chips) == chip).reshape(n_chips, 1, 1)

    def start_gather(tag, items, after):
        lands = [_SDS((n_chips, *own[i].shape[1:]), _BF) for i, _ in items]
        return _send_start(f"gather_start_{tag}", own, lands, _gather_plan(items), 3 * len(items), after)

    def finish_gather(tag, handle, items, after):
        landed = _send_wait(f"gather_wait_{tag}", handle, after)
        return {_BIG[i]: jnp.where(is_own, own[i][l][None], g) for (i, l), g in zip(items, landed)}

    def layer_weights(l, blocks):
        p = {}
        if 'w_in' in blocks:
            full_in = blocks['w_in'].transpose(1, 0, 2).reshape(d_model, in_proj)
            p['w_qkv'] = full_in[:, :3 * _AW]
            p['w_xbc'] = full_in[:, 4 * _AW:4 * _AW + cch]
            p['w_zdt'] = jnp.concatenate([full_in[:, 3 * _AW:4 * _AW], full_in[:, 4 * _AW + cch:], jnp.zeros((d_model, zdt_pad), _BF)], axis=1)
        if 'w_out' in blocks:
            p['w_out'] = blocks['w_out'].reshape(-1, d_model)
            p['w_mlp_in'] = blocks['w_mlp_in'].transpose(1, 0, 2).reshape(d_model, -1)
            p['w_mlp_out'] = blocks['w_mlp_out'].reshape(-1, d_model)
        return p

    groups = dict(a=[(0, 0)], b=[(1, 0), (2, 0), (3, 0)], c=[(0, 1)], d=[(1, 1), (2, 1), (3, 1)])
    handles, token = {}, conv_full
    for tag, items in groups.items():
        handles[tag], token = start_gather(tag, items, token)
    layers = [{n: w[n][l] for n in _SMALL[:-1]} for l in range(depth)]
    for l in range(depth):
        layers[l]['conv_w'] = conv_full[l]

    layers[0].update(layer_weights(0, finish_gather("a", handles["a"], groups["a"], token)))
    mix, sv0 = _layer_fwd(x[0], layers[0], 0)
    layers[0].update(layer_weights(0, finish_gather("b", handles["b"], groups["b"], mix)))
    h = _layer_fwd_mlp(layers[0], sv0, 0)
    layers[1].update(layer_weights(1, finish_gather("c", handles["c"], groups["c"], h)))
    mix, sv1 = _layer_fwd(h, layers[1], 1)
    layers[1].update(layer_weights(1, finish_gather("d", handles["d"], groups["d"], mix)))
    h = _layer_fwd_mlp(layers[1], sv1, 1)
    saved = [sv0, sv1]

    def by_chip(g, name):
        if name in ("w_in", "w_mlp_in"):
            return g.reshape(d_model, n_chips, -1).transpose(1, 0, 2)
        return g.reshape(n_chips, -1, d_model)

    pending = []

    def sender(l):
        def send(names, g):
            srcs = [by_chip(g[n], n) for n in names]
            halves = [s.shape[1] // 2 for s in srcs]
            lands = [_SDS((len(_FLIPS), hf, s.shape[2]), _BF) for s, hf in zip(srcs, halves)]
            handle, tok = _send_start(f"grad_start_{names[-1]}_{l}", srcs, lands, _reduce_plan(halves), len(_FLIPS) * len(srcs), srcs[0])
            pending.append((l, names, srcs, handle))
            return tok
        return send

    dx, g_final, loss_part = _loss_bwd(h, final_norm_g, loss_target[0])
    grads, after = [None] * depth, None
    for l in reversed(range(depth)):
        dx, grads[l] = _layer_bwd(dx, layers[l], saved[l], l, sender(l), after)
        after = dx
    owned = {}
    for l, names, srcs, handle in pending:
        landed = _send_wait(f"grad_wait_{names[-1]}_{l}", handle, dx)
        owned.update(zip([(n, l) for n in names], _sum_owned(srcs, landed, c, chip, [f"{n}_{l}" for n in names])))
    keys = [(n, l) for n in _BIG for l in range(depth)]
    full = dict(zip(keys, _share_halves([owned[k] for k in keys], c)))
    red = {n: jnp.stack([full[(n, l)] for l in range(depth)]) for n in _BIG}

    small = {n: jnp.stack([grads[l][n] for l in range(depth)]) for n in _SMALL[:-1] + ("conv_w",)}
    small["final_norm_g"] = g_final
    parts = [loss_part.reshape(1)] + [small[n] for n in _SMALL + ("conv_w",)]
    rows = -(-sum(p.size for p in parts) // 1024) * 8
    tot = _unpack(_exchange8(_pack(parts, rows), reduce=True, after=red[_BIG[0]], name="allreduce_small"), parts)
    loss = tot[0][0]
    red.update(zip(_SMALL + ("conv_w",), tot[1:]))
    red["conv_w"] = lax.dynamic_index_in_dim(red["conv_w"].reshape(depth, _CONV_K, n_chips, -1), chip, axis=2, keepdims=False)

    delta, new_m, new_v = {}, {}, {}
    for n in _BIG:
        delta[n], new_m[n], new_v[n] = _adamw(w[n], red[n], m[n], v[n], name=f"adamw_{n}")
    names = _SMALL + ("conv_w",)
    like = [w[n] for n in names]
    srows = -(-sum(p.size for p in like) // 1024) * 8
    res = _adamw(*[_pack([d[n] for n in names], srows) for d in (w, red, m, v)], name="adamw_small")
    for dst, buf in zip((delta, new_m, new_v), res):
        dst.update(zip(names, _unpack(buf, like)))
    return (loss, dx[None], *[red[n] for n in _ORDER], *[delta[n] for n in _ORDER], *[new_m[n] for n in _ORDER],
            *[new_v[n] for n in _ORDER])
```

```python
import numpy as np
import jax
import jax.numpy as jnp
from jax import lax
from jax.experimental import pallas as pl
from jax.experimental.pallas import tpu as pltpu

F32 = jnp.float32
_BF = jnp.bfloat16
_NEG = -1e30
_EPS = 1e-5
_HEADS = 16
_HDIM = 64
_AW = _HEADS * _HDIM
_ABLK = 128
_DILATIONS = (1, 4, 16)
_CHUNK = 128
_NSTATE = 128
_GROUPS = 2
_HPG = _HEADS // _GROUPS
_CONV_K = 4
_LANES = 128
_CHIPS = 4
_LR, _B1, _B2, _AEPS, _WD, _STEP = 0.001, 0.9, 0.999, 1e-08, 0.01, 10
_VMEM_CAP = 56 * 1024 * 1024
_MESH = pl.DeviceIdType.MESH
_SDS = jax.ShapeDtypeStruct
_NT = (((1,), (1,)), ((), ()))
_TN = (((0,), (0,)), ((), ()))


def _params(sem, est_bytes):
    lim = int(min(max(2 * est_bytes + (4 << 20), 16 << 20), _VMEM_CAP))
    return pltpu.CompilerParams(dimension_semantics=sem, vmem_limit_bytes=lim)


def _nbytes(shape, dtype):
    return int(np.prod(shape)) * jnp.dtype(dtype).itemsize


def _dot(a, b, dims=(((1,), (0,)), ((), ()))):
    return lax.dot_general(a.astype(_BF), b.astype(_BF), dims, preferred_element_type=F32)


_HALO = 8


def _rows(fn, ins, consts, outs, sums=(), *, halos=(), tile, name):
    ins = [a if isinstance(a, tuple) else (a, a.shape[1], 0) for a in ins]
    rows = ins[0][0].shape[0]
    n_steps = rows // tile
    n_in, n_h, n_c, n_o, n_s = len(ins), len(halos), len(consts), len(outs), len(sums)

    def body(*refs):
        step = pl.program_id(0)
        vals = [r[...] for r in refs[:n_in]]
        for r, (_, side) in zip(refs[n_in:n_in + n_h], halos):
            vals.append(jnp.where(step == (0 if side < 0 else n_steps - 1), 0.0, r[...]))
        vals += [r[...] for r in refs[n_in + n_h:n_in + n_h + n_c]]
        refs = refs[:n_in] + refs[n_in + n_h:]
        res = fn(*vals)
        res = res if isinstance(res, tuple) else (res,)
        orefs = refs[n_in + n_c:n_in + n_c + n_o]
        srefs = refs[n_in + n_c + n_o:]
        for r, v in zip(orefs, res[:n_o]):
            r[...] = v.astype(r.dtype)
        if n_s:
            @pl.when(pl.program_id(0) == 0)
            def _():
                for r in srefs:
                    r[...] = jnp.zeros_like(r)
            for r, v in zip(srefs, res[n_o:]):
                r[...] += v.reshape(tile // 8, 8, v.shape[-1]).sum(axis=0)

    per = tile // _HALO
    in_specs = [pl.BlockSpec((tile, w), j if callable(j) else (lambda i, j=j: (i, j))) for _, w, j in ins]
    in_specs += [pl.BlockSpec((_HALO, a.shape[1]), (lambda i: (jnp.maximum(i * per - 1, 0), 0)) if side < 0
                              else (lambda i: (jnp.minimum((i + 1) * per, rows // _HALO - 1), 0))) for a, side in halos]
    in_specs += [pl.BlockSpec(c.shape, lambda i, nd=c.ndim: (0,) * nd) for c in consts]
    out_shape = [_SDS((rows, w), dt) for w, dt in outs] + [_SDS((8, w), F32) for w in sums]
    out_specs = [pl.BlockSpec((tile, w), lambda i: (i, 0)) for w, _ in outs]
    out_specs += [pl.BlockSpec((8, w), lambda i: (0, 0)) for w in sums]
    est = sum(_nbytes((tile, w), a.dtype) for a, w, _ in ins) + sum(_nbytes((tile, w), dt) for w, dt in outs)
    return pl.pallas_call(body, grid=(n_steps,), in_specs=in_specs, out_specs=out_specs, out_shape=out_shape, name=name,
                          compiler_params=_params(("arbitrary",), 3 * est))(*[a for a, _, _ in ins], *[a for a, _ in halos], *consts)


def _shifted(u, halo, back):
    n = u.shape[0] + _HALO
    if back:
        ext = jnp.concatenate([halo, u], axis=0)
        return [pltpu.roll(ext, j, 0)[_HALO:] for j in (1, 2, 3)]
    ext = jnp.concatenate([u, halo], axis=0)
    return [pltpu.roll(ext, n - j, 0)[:u.shape[0]] for j in (1, 2, 3)]


def _tile_for(width):
    return max(c for c in (256, 128, 64, 32) if c * width <= (1 << 18) or c == 32)


def _rstd(x):
    return lax.rsqrt(jnp.mean(x * x, axis=-1, keepdims=True) + _EPS)


def _split(x, groups):
    w = x.shape[-1] // groups
    return [x[:, g * w:(g + 1) * w] for g in range(groups)]


def _cat(parts):
    return parts[0] if len(parts) == 1 else jnp.concatenate(parts, axis=-1)


def _rms_bwd_tile(x, dy, g, groups):
    dxs, dgs = [], []
    for xs, ds, gs in zip(_split(x, groups), _split(dy.astype(F32), groups), _split(g, groups)):
        r = _rstd(xs)
        xh = xs * r
        gd = ds * gs
        dxs.append(r * (gd - xh * jnp.mean(gd * xh, axis=-1, keepdims=True)))
        dgs.append(ds * xh)
    return _cat(dxs), _cat(dgs)


def _rms_fwd(x, g, *, groups=1, name):
    def fn(x, g):
        return _cat([xs * _rstd(xs) * gs for xs, gs in zip(_split(x, groups), _split(g, groups))])
    w = x.shape[1]
    return _rows(fn, [x], [g.reshape(1, w)], [(w, _BF)], tile=_tile_for(w), name=name)[0]


def _rms_bwd(x, dy, g, res=None, *, name):
    def fn(x, dy, *rest):
        dx, dg = _rms_bwd_tile(x, dy, rest[-1], 1)
        return (dx + rest[0] if res is not None else dx), dg
    w = x.shape[1]
    ins = [x, dy] + ([res] if res is not None else [])
    dx, dg = _rows(fn, ins, [g.reshape(1, w)], [(w, F32)], [w], tile=_tile_for(w), name=name)
    return dx, dg.sum(axis=0)


def _pick(n, cands):
    for c in cands:
        if n % c == 0:
            return c
    raise ValueError(f"no block size for {n}")


_MM_BLOCKS = (1024, 640, 512, 384)


def _mm(a, b, *, ta=False, tb=False, extra=(), epi=None, outs=(F32,), after=None, b_chips=0, out_chips=0, name):
    m, k = (a.shape[1], a.shape[0]) if ta else a.shape
    b_shape = (b.shape[1], b.shape[2] * b_chips) if b_chips else b.shape
    n = b_shape[0] if tb else b_shape[1]
    assert k == (b_shape[1] if tb else b_shape[0])
    n_cap = n // max(out_chips, 1 if tb else b_chips, 1)
    k_cap = k // (b_chips if (b_chips and tb) else 1)
    bm, bn = _pick(m, _MM_BLOCKS), _pick(n_cap, _MM_BLOCKS)
    bk = _pick(k_cap, (2048,) + _MM_BLOCKS)
    nk = k // bk
    n_e, n_o = len(extra), len(outs)
    behind = [] if after is None else [after]
    dims = (((0 if ta else 1,), (1 if tb else 0,)), ((), ()))

    def body(a_ref, b_ref, *rest):
        ex, orefs, acc = rest[:n_e], rest[n_e + len(behind):n_e + len(behind) + n_o], rest[-1]
        kk = pl.program_id(2)

        @pl.when(kk == 0)
        def _():
            acc[...] = jnp.zeros_like(acc)

        acc[...] += _dot(a_ref[...], b_ref[...], dims)

        @pl.when(kk == nk - 1)
        def _():
            r = acc[...]
            res = epi(r, *[e[...] for e in ex]) if epi is not None else (r,)
            for o, v in zip(orefs, res):
                o[...] = v.astype(o.dtype)

    a_spec = pl.BlockSpec((bk, bm), lambda i, j, kk: (kk, i)) if ta else pl.BlockSpec((bm, bk), lambda i, j, kk: (i, kk))
    if b_chips and tb:
        per = k_cap // bk
        b_spec = pl.BlockSpec((None, bn, bk), lambda i, j, kk: (kk // per, j, kk % per))
    elif b_chips:
        per = n_cap // bn
        b_spec = pl.BlockSpec((None, bk, bn), lambda i, j, kk: (j // per, kk, j % per))
    else:
        b_spec = pl.BlockSpec((bn, bk), lambda i, j, kk: (j, kk)) if tb else pl.BlockSpec((bk, bn), lambda i, j, kk: (kk, j))
    t_spec = pl.BlockSpec((bm, bn), lambda i, j, kk: (i, j))
    o_spec, o_shape = t_spec, (m, n)
    if out_chips:
        per_o = n_cap // bn
        o_spec, o_shape = pl.BlockSpec((None, bm, bn), lambda i, j, kk: (j // per_o, i, j % per_o)), (out_chips, m, n_cap)
    est = (_nbytes((bm, bk), a.dtype) + _nbytes((bk, bn), b.dtype) + sum(_nbytes((bm, bn), e.dtype) for e in extra)
           + sum(_nbytes((bm, bn), o) for o in outs)) * 2 + 2 * _nbytes((bm, bn), F32)
    res = pl.pallas_call(
        body, grid=(m // bm, n // bn, nk), in_specs=[a_spec, b_spec] + [t_spec] * n_e + [pl.BlockSpec(memory_space=pl.ANY)] * len(behind),
        out_specs=[o_spec] * n_o, out_shape=[_SDS(o_shape, o) for o in outs], scratch_shapes=[pltpu.VMEM((bm, bn), F32)], name=name,
        compiler_params=_params(("parallel", "parallel", "arbitrary"), est))(a, b, *extra, *behind)
    return res[0] if n_o == 1 else res


def _add_to(acc, r):
    return (acc + r,)


def _alibi_bias(dilation):
    slopes = 2.0 ** (-8.0 * (np.arange(_HEADS) + 1) / _HEADS)
    i = np.arange(_ABLK)[:, None]
    j = np.arange(_ABLK)[None, :]
    cur = np.where(i - j >= 0, -slopes[:, None, None] * ((i - j) * dilation), _NEG)
    prev = np.where(j >= i, -slopes[:, None, None] * ((i - j + _ABLK) * dilation), _NEG)
    return jnp.asarray(np.concatenate([prev, cur], axis=2), F32)


def _strided(a, d):
    return a.reshape(a.shape[0] // d, d * a.shape[1])


def _head(h):
    return slice(h * _HDIM, (h + 1) * _HDIM)


def _pair(pr):
    return slice(pr * _LANES, (pr + 1) * _LANES)


def _low_lanes(shape):
    return lax.broadcasted_iota(jnp.int32, shape, 1) < _HDIM


def _halves(v, low):
    z = jnp.zeros_like(v)
    return jnp.where(low, v, z), jnp.where(low, z, v)


def _no_prev_mask(first):
    return jnp.logical_and(first, lax.broadcasted_iota(jnp.int32, (_ABLK, 2 * _ABLK), 1) < _ABLK)


def _lane_spec(nb):
    return pl.BlockSpec((_ABLK, _LANES), lambda r, j: (jnp.minimum(j, nb - 1), r))


def _expand_heads(v):
    low = _low_lanes((v.shape[0], _LANES))
    return jnp.concatenate([jnp.where(low, v[:, 2 * pr:2 * pr + 1], v[:, 2 * pr + 1:2 * pr + 2]) for pr in range(_HEADS // 2)], axis=1)


def _attn_specs(nb, n_parts):
    def cur(p):
        return pl.BlockSpec((_ABLK, _AW), lambda r, j: (jnp.minimum(j, nb - 1), r * n_parts + p))

    def prev(p):
        return pl.BlockSpec((_ABLK, _AW), lambda r, j: (jnp.clip(j - 1, 0, nb - 1), r * n_parts + p))
    return cur, prev


def _attn_fwd(qkv, dilation, *, name):
    t = qkv.shape[0]
    nb = t // dilation // _ABLK
    bias = _alibi_bias(dilation)
    scale = _HDIM ** -0.5

    def body(q_ref, kc_ref, kp_ref, vc_ref, vp_ref, b_ref, o_ref, l_ref):
        no_prev = _no_prev_mask(pl.program_id(1) == 0)
        low = _low_lanes((_ABLK, _LANES))
        l_ref[...] = jnp.zeros_like(l_ref)
        for pr in range(_HEADS // 2):
            sl = _pair(pr)
            k2 = jnp.concatenate([kp_ref[:, sl], kc_ref[:, sl]], axis=0)
            v2 = jnp.concatenate([vp_ref[:, sl], vc_ref[:, sl]], axis=0)
            o2 = []
            for h, qh in zip((2 * pr, 2 * pr + 1), _halves(q_ref[:, sl], low)):
                s = jnp.where(no_prev, _NEG, _dot(qh, k2, _NT) * scale + b_ref[h])
                m = jnp.max(s, axis=-1, keepdims=True)
                p = jnp.exp(s - m)
                den = jnp.sum(p, axis=-1, keepdims=True)
                o2.append(_dot(p, v2) / den)
                l_ref[:, h:h + 1] = m + jnp.log(den)
            o_ref[:, sl] = jnp.where(low, o2[0], o2[1]).astype(o_ref.dtype)

    cur, prev = _attn_specs(nb, 3)
    cur1, _ = _attn_specs(nb, 1)
    bspec = pl.BlockSpec((_HEADS, _ABLK, 2 * _ABLK), lambda r, j: (0, 0, 0))
    sv = _strided(qkv, dilation)
    o, l = pl.pallas_call(
        body, grid=(dilation, nb), in_specs=[cur(0), cur(1), prev(1), cur(2), prev(2), bspec],
        out_specs=[cur1(0), _lane_spec(nb)],
        out_shape=[_SDS((t // dilation, dilation * _AW), _BF), _SDS((t // dilation, dilation * _LANES), F32)], name=name,
        compiler_params=_params(("parallel", "arbitrary"), 16 << 20))(sv, sv, sv, sv, sv, bias)
    return o.reshape(t, _AW), l.reshape(t, _LANES)


def _attn_bwd(qkv, do, ld, dilation, *, name):
    t = qkv.shape[0]
    nb = t // dilation // _ABLK
    bias = _alibi_bias(dilation)
    scale = _HDIM ** -0.5

    def body(q_ref, kc_ref, kp_ref, vc_ref, vp_ref, do_ref, ld_ref, b_ref, dq_ref, dk_ref, dv_ref, ck, cv):
        n = pl.program_id(1)

        @pl.when(n == 0)
        def _():
            ck[...] = jnp.zeros_like(ck)
            cv[...] = jnp.zeros_like(cv)

        @pl.when(n < nb)
        def _():
            low = _low_lanes((_ABLK, _LANES))
            no_prev = _no_prev_mask(n == 0)
            for pr in range(_HEADS // 2):
                sl = _pair(pr)
                k2 = jnp.concatenate([kp_ref[:, sl], kc_ref[:, sl]], axis=0)
                v2 = jnp.concatenate([vp_ref[:, sl], vc_ref[:, sl]], axis=0)
                dos = _halves(do_ref[:, sl], low)
                dq2, dk2, dv2 = [], 0.0, 0.0
                for i, qh in enumerate(_halves(q_ref[:, sl], low)):
                    h = 2 * pr + i
                    lrow, dsum = ld_ref[:, h:h + 1], ld_ref[:, _HEADS + h:_HEADS + h + 1]
                    p = jnp.exp(jnp.where(no_prev, _NEG, _dot(qh, k2, _NT) * scale + b_ref[h]) - lrow)
                    ds = (p * (_dot(dos[i], v2, _NT) - dsum)).astype(_BF)
                    dq2.append(_dot(ds, k2))
                    dk2, dv2 = dk2 + _dot(ds, qh, _TN), dv2 + _dot(p, dos[i], _TN)
                dq_ref[:, sl] = (jnp.where(low, dq2[0], dq2[1]) * scale).astype(dq_ref.dtype)
                dk_ref[:, sl] = (ck[:, sl] + dk2[:_ABLK] * scale).astype(dk_ref.dtype)
                dv_ref[:, sl] = (cv[:, sl] + dv2[:_ABLK]).astype(dv_ref.dtype)
                ck[:, sl] = dk2[_ABLK:] * scale
                cv[:, sl] = dv2[_ABLK:]

        @pl.when(n == nb)
        def _():
            dk_ref[...] = ck[...].astype(dk_ref.dtype)
            dv_ref[...] = cv[...].astype(dv_ref.dtype)

    cur, prev = _attn_specs(nb, 3)
    cur1, prev1 = _attn_specs(nb, 1)
    bspec = pl.BlockSpec((_HEADS, _ABLK, 2 * _ABLK), lambda r, j: (0, 0, 0))
    sv, dov, ldv = (_strided(a, dilation) for a in (qkv, do, ld))
    dqkv = pl.pallas_call(
        body, grid=(dilation, nb + 1),
        in_specs=[cur(0), cur(1), prev(1), cur(2), prev(2), cur1(0), _lane_spec(nb), bspec],
        out_specs=[cur1(0), prev1(0), prev1(0)], out_shape=[_SDS(dov.shape, _BF)] * 3, name=name,
        scratch_shapes=[pltpu.VMEM((_ABLK, _AW), F32)] * 2,
        compiler_params=_params(("parallel", "arbitrary"), 16 << 20))(sv, sv, sv, sv, sv, dov, ldv, bias)
    return [a.reshape(t, _AW) for a in dqkv]


def _ssd_in_specs(ch):
    return dict(
        xs=pl.BlockSpec((_CHUNK, _AW), lambda c: (ch(c), 0)),
        bc=pl.BlockSpec((_CHUNK, 2 * _GROUPS * _NSTATE), lambda c: (ch(c), _AW // (2 * _GROUPS * _NSTATE))),
        lane=pl.BlockSpec((_CHUNK, _LANES), lambda c: (ch(c), 0)),
        arow=pl.BlockSpec((_HEADS, 1, _CHUNK), lambda c: (0, 0, ch(c))),
        st=pl.BlockSpec((1, _HEADS // 2, _NSTATE, _LANES), lambda c: (ch(c), 0, 0, 0)),
    )


def _decay(a_col, a_row):
    i0 = lax.broadcasted_iota(jnp.int32, (_CHUNK, _CHUNK), 0)
    i1 = lax.broadcasted_iota(jnp.int32, (_CHUNK, _CHUNK), 1)
    return jnp.where(i0 >= i1, jnp.exp(a_col - a_row), 0.0), jnp.where(i1 >= i0, jnp.exp(a_row - a_col), 0.0)


def _rsum(v):
    return jnp.sum(v, axis=-1, keepdims=True)


def _ssd_fwd(act, dt, acum, a_row, *, name):
    t = act.shape[0]
    nc = t // _CHUNK
    sp = _ssd_in_specs(lambda c: c)
    gw = _GROUPS * _NSTATE

    def body(xs_ref, bc_ref, dt_ref, ac_ref, ar_ref, y_ref, sall_ref, st):
        @pl.when(pl.program_id(0) == 0)
        def _():
            st[...] = jnp.zeros_like(st)

        low = _low_lanes((_CHUNK, _LANES))
        for g in range(_GROUPS):
            bg = bc_ref[:, g * _NSTATE:(g + 1) * _NSTATE]
            cg = bc_ref[:, gw + g * _NSTATE:gw + (g + 1) * _NSTATE].astype(_BF)
            cb = _dot(cg, bg, _NT)
            for pr in range(g * _HPG // 2, (g + 1) * _HPG // 2):
                ha, hb = 2 * pr, 2 * pr + 1
                a_a, a_b = ac_ref[:, ha:ha + 1], ac_ref[:, hb:hb + 1]
                x = (xs_ref[:, _pair(pr)] * jnp.where(low, dt_ref[:, ha:ha + 1], dt_ref[:, hb:hb + 1])).astype(_BF)
                lm_a, _ = _decay(a_a, ar_ref[ha])
                lm_b, _ = _decay(a_b, ar_ref[hb])
                sv = st[pr]
                sall_ref[0, pr] = sv
                yd = jnp.where(low, _dot(cb * lm_a, x), _dot(cb * lm_b, x))
                y_ref[:, _pair(pr)] = yd + jnp.where(low, jnp.exp(a_a), jnp.exp(a_b)) * _dot(cg, sv)
                al_a, al_b = jnp.min(a_a, axis=0, keepdims=True), jnp.min(a_b, axis=0, keepdims=True)
                st[pr] = (jnp.where(low, jnp.exp(al_a), jnp.exp(al_b)) * sv
                          + jnp.where(low, _dot(bg * jnp.exp(al_a - a_a), x, _TN), _dot(bg * jnp.exp(al_b - a_b), x, _TN)))

    return pl.pallas_call(
        body, grid=(nc,), in_specs=[sp['xs'], sp['bc'], sp['lane'], sp['lane'], sp['arow']],
        out_specs=[sp['xs'], sp['st']], out_shape=[_SDS((t, _AW), F32), _SDS((nc, _HEADS // 2, _NSTATE, _LANES), F32)],
        scratch_shapes=[pltpu.VMEM((_HEADS // 2, _NSTATE, _LANES), F32)], name=name,
        compiler_params=_params(("arbitrary",), 16 << 20))(act, act, dt, acum, a_row)


def _ssd_bwd(act, dt, acum, a_row, sall, dy, *, name):
    t = act.shape[0]
    nc = t // _CHUNK
    sp = _ssd_in_specs(lambda c: nc - 1 - c)
    gw = _GROUPS * _NSTATE

    def body(xs_ref, bc_ref, dt_ref, ac_ref, ar_ref, sall_ref, dy_ref, dxs_ref, dbc_ref, ddt_ref, da_ref, dst):
        @pl.when(pl.program_id(0) == 0)
        def _():
            dst[...] = jnp.zeros_like(dst)

        ddt_ref[...] = jnp.zeros_like(ddt_ref)
        da_ref[...] = jnp.zeros_like(da_ref)
        row = lax.broadcasted_iota(jnp.int32, (_CHUNK, 1), 0)
        low = _low_lanes((_CHUNK, _LANES))
        for g in range(_GROUPS):
            bg = bc_ref[:, g * _NSTATE:(g + 1) * _NSTATE]
            bgb = bg.astype(_BF)
            cg = bc_ref[:, gw + g * _NSTATE:gw + (g + 1) * _NSTATE].astype(_BF)
            cb, cbt = _dot(cg, bgb, _NT), _dot(bgb, cg, _NT)
            dcb = jnp.zeros((_CHUNK, _CHUNK), F32)
            dbg = jnp.zeros((_CHUNK, _NSTATE), F32)
            dcg = jnp.zeros((_CHUNK, _NSTATE), F32)
            for pr in range(g * _HPG // 2, (g + 1) * _HPG // 2):
                heads = (2 * pr, 2 * pr + 1)
                a_cols = [ac_ref[:, h:h + 1] for h in heads]
                dt_pair = jnp.where(low, dt_ref[:, heads[0]:heads[0] + 1], dt_ref[:, heads[1]:heads[1] + 1])
                xsv = xs_ref[:, _pair(pr)]
                x = xsv * dt_pair
                xb = x.astype(_BF)
                xhs = _halves(xb, low)
                dyv = dy_ref[:, _pair(pr)]
                dyb = dyv.astype(_BF)
                dyhs = _halves(dyb, low)
                sv, dsv = sall_ref[0, pr], dst[pr]
                svb, dsb = sv.astype(_BF), dsv.astype(_BF)
                a_lasts = [jnp.min(a, axis=0, keepdims=True) for a in a_cols]
                e_pair = jnp.where(low, jnp.exp(a_cols[0]), jnp.exp(a_cols[1]))
                el_pair = jnp.where(low, jnp.exp(a_lasts[0]), jnp.exp(a_lasts[1]))
                yo = e_pair * _dot(cg, svb)
                dxg, bwds, das = [], [], []
                for i, h in enumerate(heads):
                    lm, lmt = _decay(a_cols[i], ar_ref[h])
                    gm, gmt = cb * lm, cbt * lmt
                    w_col = jnp.exp(a_lasts[i] - a_cols[i])
                    bwds.append(_dot(bg * w_col, dsb))
                    dxg.append(_dot(gm, dyb, _TN))
                    dg, dgt = _dot(dyhs[i], xb, _NT), _dot(xhs[i], dyb, _NT)
                    dcb = dcb + dg * lm
                    dbg = dbg + w_col * _dot(xhs[i], dsb, _NT)
                    das.append(_rsum(dg * gm) - _rsum(dgt * gmt))
                bwd = jnp.where(low, bwds[0], bwds[1])
                dx = jnp.where(low, dxg[0], dxg[1]) + bwd
                edy = (e_pair * dyv).astype(_BF)
                dcg = dcg + _dot(edy, svb, _NT)
                zs, yos, sds, dts = (_halves(v, low) for v in (x * bwd, dyv * yo, sv * dsv, dx * xsv))
                for i, h in enumerate(heads):
                    z = _rsum(zs[i])
                    da_last = jnp.sum(z, axis=0, keepdims=True) + jnp.exp(a_lasts[i]) * jnp.sum(_rsum(sds[i]), axis=0, keepdims=True)
                    da_ref[:, h:h + 1] = das[i] + _rsum(yos[i]) - z + jnp.where(row == _CHUNK - 1, da_last, 0.0)
                    ddt_ref[:, h:h + 1] = _rsum(dts[i])
                dxs_ref[:, _pair(pr)] = dx * dt_pair
                dst[pr] = el_pair * dsv + _dot(cg, edy, _TN)
            dbc_ref[:, g * _NSTATE:(g + 1) * _NSTATE] = dbg + _dot(dcb, cg, _TN)
            dbc_ref[:, gw + g * _NSTATE:gw + (g + 1) * _NSTATE] = dcg + _dot(dcb, bgb)

    ch = lambda c: nc - 1 - c
    wide = pl.BlockSpec((_CHUNK, 2 * gw), lambda c: (ch(c), 0))
    return pl.pallas_call(
        body, grid=(nc,), in_specs=[sp['xs'], sp['bc'], sp['lane'], sp['lane'], sp['arow'], sp['st'], sp['xs']],
        out_specs=[sp['xs'], wide, sp['lane'], sp['lane']],
        out_shape=[_SDS((t, _AW), F32), _SDS((t, 2 * gw), F32), _SDS((t, _LANES), F32), _SDS((t, _LANES), F32)],
        scratch_shapes=[pltpu.VMEM((_HEADS // 2, _NSTATE, _LANES), F32)], name=name,
        compiler_params=_params(("arbitrary",), 16 << 20))(act, act, dt, acum, a_row, sall, dy)


def _scan_rows(v, reverse):
    r = lax.broadcasted_iota(jnp.int32, v.shape, 0)
    for s in (1, 2, 4, 8, 16, 32, 64):
        if reverse:
            v = v + jnp.where(r < _CHUNK - s, pltpu.roll(v, _CHUNK - s, 0), 0.0)
        else:
            v = v + jnp.where(r >= s, pltpu.roll(v, s, 0), 0.0)
    return v


def _softplus(x):
    return jnp.maximum(x, 0.0) + jnp.log(1.0 + jnp.exp(-jnp.abs(x)))


def _sigmoid(x):
    return 1.0 / (1.0 + jnp.exp(-x))


def _silu(x):
    return x * _sigmoid(x)


def _dsilu(x):
    s = _sigmoid(x)
    return s * (1.0 + x * (1.0 - s))


def _lanes(a):
    return jnp.pad(a, (0, _LANES - a.shape[0])).reshape(1, _LANES)


def _layer_fwd(x, p, l):
    cch = p['conv_w'].shape[1]
    sv = {}
    h1 = _rms_fwd(x, p['ln1_g'], name=f"ln1_fwd_{l}")
    qkv = _mm(h1, p['w_qkv'], outs=(_BF,), name=f"in_proj_qkv_{l}")
    xbc = _mm(h1, p['w_xbc'], name=f"in_proj_xbc_{l}")
    zdt = _mm(h1, p['w_zdt'], name=f"in_proj_zdt_{l}")
    z, dt_raw = (zdt, _AW, 0), (zdt, _LANES, _AW // _LANES)

    outs = []
    for dil in _DILATIONS:
        outs += _attn_fwd(qkv, dil, name=f"attn_fwd_d{dil}_{l}")

    def combine(o1, l1, o2, l2, o3, l3):
        m = jnp.maximum(jnp.maximum(l1, l2), l3)
        e1, e2, e3 = jnp.exp(l1 - m), jnp.exp(l2 - m), jnp.exp(l3 - m)
        tot = e1 + e2 + e3
        mixed = sum(_expand_heads(e / tot) * o.astype(F32) for e, o in ((e1, o1), (e2, o2), (e3, o3)))
        return mixed, m + jnp.log(tot)
    attn, lse = _rows(combine, outs, [], [(_AW, F32), (_LANES, F32)], tile=_tile_for(_AW), name=f"attn_combine_{l}")
    attn_n = _rms_fwd(attn, p['attn_norm_g'], name=f"attn_norm_fwd_{l}")

    def conv(u0, before, w, b):
        u1, u2, u3 = _shifted(u0, before, True)
        return _silu(w[0:1] * u3 + w[1:2] * u2 + w[2:3] * u1 + w[3:4] * u0 + b)
    act = _rows(conv, [xbc], [p['conv_w'], p['conv_b'].reshape(1, cch)], [(cch, F32)], halos=[(xbc, -1)], tile=_tile_for(cch),
                name=f"conv_fwd_{l}")[0]

    def dtf(raw, bias, alog):
        dt = _softplus(raw + bias)
        return dt, _scan_rows(dt * -jnp.exp(alog), False)
    dt, acum = _rows(dtf, [dt_raw], [_lanes(p['dt_bias']), _lanes(p['a_log'])], [(_LANES, F32), (_LANES, F32)],
                     tile=_CHUNK, name=f"dt_fwd_{l}")
    a_row = acum[:, :_HEADS].T[:, None, :]
    y_ssd, sall = _ssd_fwd(act, dt, acum, a_row, name=f"ssd_fwd_{l}")
    dskip = jnp.repeat(p['d_skip'], _HDIM).reshape(1, _AW)
    xs = (act, _AW, 0)

    def gate(y, xs, z, dsk):
        return (y + dsk * xs) * _silu(z)
    y2 = _rows(gate, [y_ssd, xs, z], [dskip], [(_AW, F32)], tile=_tile_for(_AW), name=f"gate_fwd_{l}")[0]
    y_n = _rms_fwd(y2, p['ssd_norm_g'], groups=_GROUPS, name=f"ssd_norm_fwd_{l}")

    mix = jnp.concatenate([attn_n, y_n], axis=1)
    sv.update(x=x, h1=h1, qkv=qkv, zdt=zdt, xbc=xbc, attn=attn, lse=lse, act=act, dt=dt, acum=acum, a_row=a_row,
              sall=sall, y_ssd=y_ssd, dskip=dskip, y2=y2, mix=mix)
    return mix, sv


def _layer_fwd_mlp(p, sv, l):
    x2 = _mm(sv['mix'], p['w_out'], extra=(sv['x'],), epi=_add_to, name=f"out_proj_{l}")
    h2 = _rms_fwd(x2, p['ln2_g'], name=f"ln2_fwd_{l}")
    a = _mm(h2, p['w_mlp_in'], b_chips=_CHIPS, epi=lambda acc: (jnp.square(jnp.maximum(acc, 0.0)),), outs=(_BF,), name=f"mlp_in_{l}")
    x3 = _mm(a, p['w_mlp_out'], extra=(x2,), epi=_add_to, name=f"mlp_out_{l}")
    sv.update(x2=x2, h2=h2, a=a)
    return x3


def _layer_bwd(dx3, p, sv, l, send, after):
    cch = p['conv_w'].shape[1]
    g = {}
    dx3b = dx3.astype(_BF)
    du = _mm(dx3b, p['w_mlp_out'], tb=True, extra=(sv['a'],), outs=(_BF,), after=after,
             epi=lambda acc, a: (acc * 2.0 * jnp.sqrt(a.astype(F32)),), name=f"mlp_out_dx_{l}")
    g['w_mlp_out'] = _mm(sv['a'], dx3b, ta=True, outs=(_BF,), name=f"mlp_out_dw_{l}")
    g['w_mlp_in'] = _mm(sv['h2'], du, ta=True, out_chips=_CHIPS, outs=(_BF,), name=f"mlp_in_dw_{l}")
    sent = send(('w_mlp_out', 'w_mlp_in'), g)
    dh2 = _mm(du, p['w_mlp_in'], tb=True, b_chips=_CHIPS, after=sent, name=f"mlp_in_dx_{l}")
    dx2, g['ln2_g'] = _rms_bwd(sv['x2'], dh2, p['ln2_g'], dx3, name=f"ln2_bwd_{l}")
    dx2b = dx2.astype(_BF)
    dmix = _mm(dx2b, p['w_out'], tb=True, name=f"out_proj_dx_{l}")
    g['w_out'] = _mm(sv['mix'], dx2b, ta=True, outs=(_BF,), name=f"out_proj_dw_{l}")

    def norm_bwd(attn, dy, lse, gn):
        dattn, dgn = _rms_bwd_tile(attn, dy, gn, 1)
        prod, low = dattn * attn, _low_lanes((attn.shape[0], _LANES))
        lane = lax.broadcasted_iota(jnp.int32, lse.shape, 1)
        ld = jnp.where(lane < _HEADS, lse, 0.0)
        for pr in range(_HEADS // 2):
            for i, part in enumerate(_halves(prod[:, _pair(pr)], low)):
                ld = jnp.where(lane == _HEADS + 2 * pr + i, _rsum(part), ld)
        return dattn, ld, dgn
    dattn, ld, gn_sum = _rows(norm_bwd, [sv['attn'], (dmix, _AW, 0), sv['lse']], [p['attn_norm_g'].reshape(1, _AW)],
                              [(_AW, _BF), (_LANES, F32)], [_AW], tile=128, name=f"attn_norm_bwd_{l}")
    g['attn_norm_g'] = gn_sum.sum(axis=0)
    parts = [_attn_bwd(sv['qkv'], dattn, ld, dil, name=f"attn_bwd_d{dil}_{l}") for dil in _DILATIONS]

    def branch_sum(*t):
        t = [a.astype(F32) for a in t]
        return jnp.concatenate([t[i] + t[3 + i] + t[6 + i] for i in range(3)], axis=1)
    dqkv = _rows(branch_sum, [a for pr in parts for a in pr], [], [(3 * _AW, _BF)], tile=128, name=f"attn_bwd_sum_{l}")[0]

    xs, z, dt_raw = (sv['act'], _AW, 0), (sv['zdt'], _AW, 0), (sv['zdt'], _LANES, _AW // _LANES)

    def gate_bwd(y2, dy, y, xs, z, dsk, gn):
        dy2, dgn = _rms_bwd_tile(y2, dy, gn, _GROUPS)
        dy1 = dy2 * _silu(z)
        return dy1, dsk * dy1, dy2 * (y + dsk * xs) * _dsilu(z), dy1 * xs, dgn
    dy1, dxs_skip, dz, dsk_sum, gn_sum = _rows(
        gate_bwd, [sv['y2'], (dmix, _AW, 1), sv['y_ssd'], xs, z], [sv['dskip'], p['ssd_norm_g'].reshape(1, _AW)],
        [(_AW, F32), (_AW, F32), (_AW, _BF)], [_AW, _AW], tile=128, name=f"gate_bwd_{l}")
    g['ssd_norm_g'] = gn_sum.sum(axis=0)
    g['d_skip'] = dsk_sum.sum(axis=0).reshape(_HEADS, _HDIM).sum(axis=1)
    dxs, dbc, ddt, da = _ssd_bwd(sv['act'], sv['dt'], sv['acum'], sv['a_row'], sv['sall'], dy1, name=f"ssd_bwd_{l}")

    def dtb(da, ddtx, raw, dt, dz, bias, alog):
        a = -jnp.exp(alog)
        dda = _scan_rows(da, True)
        draw = (dda * a + ddtx) * _sigmoid(raw + bias)
        return jnp.concatenate([dz, draw.astype(dz.dtype)], axis=1), draw, dda * dt * a
    dzdt, dbias, dalog = _rows(dtb, [da, ddt, dt_raw, sv['dt'], dz], [_lanes(p['dt_bias']), _lanes(p['a_log'])],
                               [(_AW + _LANES, _BF)], [_LANES, _LANES], tile=_CHUNK, name=f"dt_bwd_{l}")
    g['dt_bias'], g['a_log'] = dbias.sum(axis=0)[:_HEADS], dalog.sum(axis=0)[:_HEADS]
    def conv_bwd1(u0, dxs, dbc, dxk, before, w, b):
        u1, u2, u3 = _shifted(u0, before, True)
        pre = w[0:1] * u3 + w[1:2] * u2 + w[2:3] * u1 + w[3:4] * u0 + b
        dp = jnp.concatenate([dxs + dxk, dbc], axis=1) * _dsilu(pre)
        return dp, dp * u3, dp * u2, dp * u1, dp * u0, dp
    dpre, *dws = _rows(conv_bwd1, [sv['xbc'], dxs, dbc, dxs_skip], [p['conv_w'], p['conv_b'].reshape(1, cch)], [(cch, F32)],
                       [cch] * 5, halos=[(sv['xbc'], -1)], tile=128, name=f"conv_bwd_pre_{l}")
    g['conv_w'] = jnp.stack([dws[i].sum(axis=0) for i in range(_CONV_K)])
    g['conv_b'] = dws[4].sum(axis=0)

    def conv_bwd2(p0, after_, w):
        p1, p2, p3 = _shifted(p0, after_, False)
        return w[3:4] * p0 + w[2:3] * p1 + w[1:2] * p2 + w[0:1] * p3
    dxbc = _rows(conv_bwd2, [dpre], [p['conv_w']], [(cch, _BF)], halos=[(dpre, 1)], tile=_tile_for(cch), name=f"conv_bwd_in_{l}")[0]
    h1 = sv['h1']
    g_qkv = _mm(h1, dqkv, ta=True, outs=(_BF,), name=f"in_proj_qkv_dw_{l}")
    g_xbc = _mm(h1, dxbc, ta=True, outs=(_BF,), name=f"in_proj_xbc_dw_{l}")
    g_zdt = _mm(h1, dzdt, ta=True, outs=(_BF,), name=f"in_proj_zdt_dw_{l}")
    g['w_in'] = jnp.concatenate([g_qkv, g_zdt[:, :_AW], g_xbc, g_zdt[:, _AW:_AW + _HEADS]], axis=1)
    sent = send(('w_out', 'w_in'), g)
    for n in _BIG:
        del g[n]
    dh1 = _mm(dqkv, p['w_qkv'], tb=True, after=sent, name=f"in_proj_qkv_dx_{l}")
    dh1 = _mm(dxbc, p['w_xbc'], tb=True, extra=(dh1,), epi=_add_to, name=f"in_proj_xbc_dx_{l}")
    dh1 = _mm(dzdt, p['w_zdt'], tb=True, extra=(dh1,), epi=_add_to, name=f"in_proj_zdt_dx_{l}")
    dx, g['ln1_g'] = _rms_bwd(sv['x'], dh1, p['ln1_g'], dx2, name=f"ln1_bwd_{l}")
    return dx, g


def _loss_bwd(x, g, tgt):
    w = x.shape[1]
    tile = _tile_for(w)

    def fn(x, tgt, g):
        r = _rstd(x)
        xh = x * r
        e = xh * g - tgt
        gd = e * (g / w)
        dx = r * (gd - xh * jnp.mean(gd * xh, axis=-1, keepdims=True))
        rowloss = 0.5 * jnp.mean(e * e, axis=-1, keepdims=True)
        return dx, (e / w) * xh, jnp.broadcast_to(rowloss, (tile, _LANES))
    dx, dg, ls = _rows(fn, [x, tgt], [g.reshape(1, w)], [(w, F32)], [w, _LANES], tile=tile, name="loss_head")
    return dx, dg.sum(axis=0), ls[:, 0].sum()


def _adamw_math(w, g, m, v):
    m2 = _B1 * m + (1.0 - _B1) * g
    v2 = _B2 * v + (1.0 - _B2) * jnp.square(g)
    m_hat = m2 / (1.0 - _B1 ** _STEP)
    v_hat = v2 / (1.0 - _B2 ** _STEP)
    return -_LR * (m_hat / (jnp.sqrt(v_hat) + _AEPS) + _WD * w), m2, v2


def _adamw(w, g, m, v, *, name):
    width = w.shape[-1]
    flat = [a.reshape(-1, width) for a in (w, g, m, v)]
    tile = _pick(flat[0].shape[0], (_tile_for(width), 32, 8))
    res = _rows(_adamw_math, flat, [], [(width, F32)] * 3, tile=tile, name=name)
    return [r.reshape(w.shape) for r in res]


_HBM = pl.BlockSpec(memory_space=pltpu.HBM)


def _place():
    x, y, c = lax.axis_index("x"), lax.axis_index("y"), lax.axis_index("c")
    other_chips = [(1 - x, y), (x, 1 - y), (1 - x, 1 - y)]
    return x, y, c, other_chips


def _remote(src, dst, sems, i, dev):
    return pltpu.make_async_remote_copy(src_ref=src, dst_ref=dst, send_sem=sems[0].at[i], recv_sem=sems[1].at[i],
                                        device_id=dev, device_id_type=_MESH)


def _exchange8(v, *, reduce, after=None, name):
    r, w = v.shape
    behind = [] if after is None else [after]

    def body(v_ref, *rest):
        all_ref, rest = rest[len(behind)], rest[len(behind) + 1:]
        sems = rest[-2:]
        x, y, c, _ = _place()
        me = 4 * x + 2 * y + c
        all_ref[me] = v_ref[...]
        flips = [((d >> 2) & 1, (d >> 1) & 1, d & 1) for d in range(1, 8)]
        sends = [_remote(v_ref, all_ref.at[me], sems, i, (x ^ fx, y ^ fy, c ^ fc)) for i, (fx, fy, fc) in enumerate(flips)]
        for cp in sends:
            cp.start()
        for i, (fx, fy, fc) in enumerate(flips):
            _remote(v_ref, all_ref.at[me ^ (4 * fx + 2 * fy + fc)], sems, i, (x ^ fx, y ^ fy, c ^ fc)).wait_recv()
        for cp in sends:
            cp.wait_send()
        if reduce:
            acc = all_ref[0]
            for s in range(1, 8):
                acc = acc + all_ref[s]
            rest[0][...] = acc

    vm = pl.BlockSpec(memory_space=pltpu.VMEM)
    out_shape = [_SDS((8, r, w), v.dtype)] + ([_SDS((r, w), v.dtype)] if reduce else [])
    res = pl.pallas_call(body, in_specs=[vm] + [_ANY] * len(behind), out_specs=[vm] * len(out_shape), out_shape=out_shape, name=name,
                         scratch_shapes=[pltpu.SemaphoreType.DMA((7,)), pltpu.SemaphoreType.DMA((7,))],
                         compiler_params=pltpu.CompilerParams(vmem_limit_bytes=int(32 << 20)))(v, *behind)
    return res[1] if reduce else res[0]


_SEM = pl.BlockSpec(memory_space=pltpu.SEMAPHORE)
_ANY = pl.BlockSpec(memory_space=pl.ANY)
_EFFECT = pltpu.SideEffectType.DATAFLOW_SIDE_EFFECTING


def _in_hbm(a):
    return pltpu.with_memory_space_constraint(a, pltpu.HBM)


def _send_start(name, srcs, land_shapes, plan, n_sends, after):
    ns, nl = len(srcs), len(land_shapes)

    def body(*refs):
        ins, lands, sems = refs[:ns], refs[ns:ns + nl], refs[ns + nl + 1:ns + nl + 3]
        x, y, c, chips = _place()
        for i, (s, d, dev) in enumerate(plan(x, y, c, chips, ins, lands)[0]):
            _remote(s, d, sems, i, dev).start()
        refs[-1][...] = jnp.zeros_like(refs[-1])

    sem = pltpu.SemaphoreType.DMA((n_sends,))
    res = pl.pallas_call(
        body, name=name, in_specs=[_HBM] * (ns + nl) + [_ANY],
        out_shape=(sem, sem, *[pltpu.HBM(s.shape, s.dtype) for s in land_shapes], _SDS((8, _LANES), F32)),
        out_specs=(_SEM, _SEM, *[_HBM] * nl, pl.BlockSpec(memory_space=pltpu.VMEM)),
        input_output_aliases={ns + i: 2 + i for i in range(nl)},
        compiler_params=pltpu.CompilerParams(has_side_effects=_EFFECT))(
            *[_in_hbm(s) for s in srcs], *[_in_hbm(lax.empty(s.shape, s.dtype)) for s in land_shapes], after)
    return dict(sems=res[:2], srcs=srcs, lands=res[2:2 + nl], plan=plan), res[-1]


def _send_wait(name, h, after):
    ns, nl = len(h['srcs']), len(h['lands'])

    def body(*refs):
        ins, lands, sems = refs[:ns], refs[ns:ns + nl], refs[ns + nl:ns + nl + 2]
        x, y, c, chips = _place()
        sends, landings = h['plan'](x, y, c, chips, ins, lands)
        for i, (s, d, dev) in enumerate(sends):
            _remote(s, d, sems, i, dev).wait_send()
        for i, d in enumerate(landings):
            _remote(d, d, sems, i, sends[i][2]).wait_recv()

    return pl.pallas_call(
        body, name=name, in_specs=[_HBM] * (ns + nl) + [_SEM, _SEM, _ANY],
        out_shape=tuple(pltpu.HBM(a.shape, a.dtype) for a in h['lands']), out_specs=tuple([_HBM] * nl),
        input_output_aliases={ns + i: i for i in range(nl)},
        compiler_params=pltpu.CompilerParams(has_side_effects=_EFFECT))(
            *[_in_hbm(s) for s in h['srcs']], *h['lands'], *h['sems'], after)


def _gather_plan(items):
    def plan(x, y, c, chips, ins, lands):
        k = 2 * x + y
        sends = [(ins[si].at[l], lands[t].at[k], (px, py, c)) for t, (si, l) in enumerate(items) for px, py in chips]
        return sends, [lands[t].at[2 * px + py] for t in range(len(items)) for px, py in chips]
    return plan


_FLIPS = [((d >> 2) & 1, (d >> 1) & 1, d & 1) for d in range(1, 8)]


def _reduce_plan(halves):
    def plan(x, y, c, chips, ins, lands):
        sends, landings = [], []
        for t, hf in enumerate(halves):
            for i, (fx, fy, fc) in enumerate(_FLIPS):
                px, py, pc = x ^ fx, y ^ fy, c ^ fc
                sends.append((ins[t].at[2 * px + py, pl.ds(pc * hf, hf)], lands[t].at[i], (px, py, pc)))
                landings.append(lands[t].at[i])
        return sends, landings
    return plan


def _swap(name, srcs, out_shapes, plan, n_sends):
    n = len(srcs)

    def body(*refs):
        ins, outs, sems = refs[:n], refs[n:n + len(out_shapes)], refs[-2:]
        x, y, c, chips = _place()
        sends, landings = plan(x, y, c, chips, ins, outs)
        out = [_remote(s, d, sems, i, dev) for i, (s, d, dev) in enumerate(sends)]
        for cp in out:
            cp.start()
        for i, d in enumerate(landings):
            _remote(d, d, sems, i, sends[i][2]).wait_recv()
        for cp in out:
            cp.wait_send()

    return pl.pallas_call(
        body, in_specs=[_HBM] * n, out_specs=[_HBM] * len(out_shapes), out_shape=out_shapes, name=name,
        scratch_shapes=[pltpu.SemaphoreType.DMA((n_sends,)), pltpu.SemaphoreType.DMA((n_sends,))])(*srcs)


def _sum_owned(grads, landed, c, k, names):
    def sum8(*parts):
        acc = parts[0].astype(F32)
        for p in parts[1:]:
            acc = acc + p.astype(F32)
        return acc
    outs = []
    for g, got, name in zip(grads, landed, names):
        hf, b = got.shape[1:]
        own = lax.dynamic_slice_in_dim(lax.dynamic_index_in_dim(g, k, axis=0, keepdims=False), c * hf, hf, axis=0)
        outs.append(_rows(sum8, [own] + [got[i] for i in range(len(_FLIPS))], [], [(b, F32)], tile=_pick(hf, (_tile_for(b), 32)),
                          name=f"grad_sum_{name}")[0])
    return outs


def _share_halves(mine):
    n = len(mine)

    def plan(x, y, c_, chips, ins, outs):
        return [(ins[t], outs[t], (x, y, 1 - c_)) for t in range(n)], [outs[t] for t in range(n)]
    return _swap("grad_share_cores", mine, [_SDS(h.shape, F32) for h in mine], plan, n)


def _adamw_owned(w, mine, theirs, m, v, c, *, name):
    depth, a, b = w.shape
    half = a // 2
    tile = _pick(half, (_tile_for(b), 32, 8))
    nh = half // tile

    def blocks_of(l):
        return lambda i: (jnp.clip(i - 2 * nh * l, 0, 2 * nh - 1) % nh, 0)

    def fn(w, m, v, *rest):
        halves, cflag = rest[:-1], rest[-1]
        step = pl.program_id(0)
        is_mine = cflag[0:1, 0:1] == ((step // nh) % 2).astype(F32)
        g = jnp.where(is_mine, halves[0], halves[1])
        for l in range(1, depth):
            g = jnp.where(step >= 2 * nh * l, jnp.where(is_mine, halves[2 * l], halves[2 * l + 1]), g)
        return (g,) + _adamw_math(w, g, m, v)
    ins = [a_.reshape(depth * a, b) for a_ in (w, m, v)]
    ins += [(h, b, blocks_of(l)) for l in range(depth) for h in (mine[l], theirs[l])]
    res = _rows(fn, ins, [jnp.full((1, _LANES), c, F32)], [(b, F32)] * 4, tile=tile, name=name)
    return [r.reshape(w.shape) for r in res]


_BIG = ("w_in", "w_out", "w_mlp_in", "w_mlp_out")
_SMALL = ("ln1_g", "conv_b", "dt_bias", "a_log", "d_skip", "attn_norm_g", "ssd_norm_g", "ln2_g", "final_norm_g")
_ORDER = ("ln1_g", "w_in", "conv_w", "conv_b", "dt_bias", "a_log", "d_skip", "attn_norm_g", "ssd_norm_g", "w_out", "ln2_g",
          "w_mlp_in", "w_mlp_out", "final_norm_g")


def _pack(parts, rows):
    flat = jnp.concatenate([p.reshape(-1) for p in parts])
    return jnp.pad(flat, (0, rows * _LANES - flat.shape[0])).reshape(rows, _LANES)


def _unpack(buf, like):
    flat, out, o = buf.reshape(-1), [], 0
    for p in like:
        out.append(flat[o:o + p.size].reshape(p.shape))
        o += p.size
    return out


def kernel(x, ln1_g, w_in, conv_w, conv_b, dt_bias, a_log, d_skip, attn_norm_g, ssd_norm_g, w_out, ln2_g, w_mlp_in, w_mlp_out, final_norm_g, loss_target, m_ln1_g, m_w_in, m_conv_w, m_conv_b, m_dt_bias, m_a_log, m_d_skip, m_attn_norm_g, m_ssd_norm_g, m_w_out, m_ln2_g, m_w_mlp_in, m_w_mlp_out, m_final_norm_g, v_ln1_g, v_w_in, v_conv_w, v_conv_b, v_dt_bias, v_a_log, v_d_skip, v_attn_norm_g, v_ssd_norm_g, v_w_out, v_ln2_g, v_w_mlp_in, v_w_mlp_out, v_final_norm_g):
    w = dict(ln1_g=ln1_g, w_in=w_in, conv_w=conv_w, conv_b=conv_b, dt_bias=dt_bias, a_log=a_log, d_skip=d_skip,
             attn_norm_g=attn_norm_g, ssd_norm_g=ssd_norm_g, w_out=w_out, ln2_g=ln2_g, w_mlp_in=w_mlp_in, w_mlp_out=w_mlp_out,
             final_norm_g=final_norm_g)
    m = dict(ln1_g=m_ln1_g, w_in=m_w_in, conv_w=m_conv_w, conv_b=m_conv_b, dt_bias=m_dt_bias, a_log=m_a_log, d_skip=m_d_skip,
             attn_norm_g=m_attn_norm_g, ssd_norm_g=m_ssd_norm_g, w_out=m_w_out, ln2_g=m_ln2_g, w_mlp_in=m_w_mlp_in,
             w_mlp_out=m_w_mlp_out, final_norm_g=m_final_norm_g)
    v = dict(ln1_g=v_ln1_g, w_in=v_w_in, conv_w=v_conv_w, conv_b=v_conv_b, dt_bias=v_dt_bias, a_log=v_a_log, d_skip=v_d_skip,
             attn_norm_g=v_attn_norm_g, ssd_norm_g=v_ssd_norm_g, w_out=v_w_out, ln2_g=v_ln2_g, w_mlp_in=v_w_mlp_in,
             w_mlp_out=v_w_mlp_out, final_norm_g=v_final_norm_g)
    depth, d_model = ln1_g.shape
    n_chips = 4
    c = lax.axis_index("c")
    chip = 2 * lax.axis_index("x") + lax.axis_index("y")
    in_proj = w_in.shape[2] * n_chips
    cch = conv_w.shape[2] * n_chips
    zdt_pad = _LANES - _HEADS

    cw = _exchange8(conv_w.reshape(depth * _CONV_K, -1), reduce=False, name="gather_conv_w")[0::2]
    conv_full = cw.reshape(n_chips, depth, _CONV_K, -1).transpose(1, 2, 0, 3).reshape(depth, _CONV_K, cch)
    own = [w[n].astype(_BF) for n in _BIG]
    is_own = (jnp.arange(n_chips) == chip).reshape(n_chips, 1, 1)

    def start_gather(tag, items, after):
        lands = [_SDS((n_chips, *own[i].shape[1:]), _BF) for i, _ in items]
        return _send_start(f"gather_start_{tag}", own, lands, _gather_plan(items), 3 * len(items), after)

    def finish_gather(tag, handle, items, after):
        landed = _send_wait(f"gather_wait_{tag}", handle, after)
        return {_BIG[i]: jnp.where(is_own, own[i][l][None], g) for (i, l), g in zip(items, landed)}

    def layer_weights(l, blocks):
        p = {}
        if 'w_in' in blocks:
            full_in = blocks['w_in'].transpose(1, 0, 2).reshape(d_model, in_proj)
            p['w_qkv'] = full_in[:, :3 * _AW]
            p['w_xbc'] = full_in[:, 4 * _AW:4 * _AW + cch]
            p['w_zdt'] = jnp.concatenate([full_in[:, 3 * _AW:4 * _AW], full_in[:, 4 * _AW + cch:], jnp.zeros((d_model, zdt_pad), _BF)], axis=1)
        if 'w_out' in blocks:
            p['w_out'] = blocks['w_out'].reshape(-1, d_model)
            p['w_mlp_in'] = blocks['w_mlp_in']
            p['w_mlp_out'] = blocks['w_mlp_out'].reshape(-1, d_model)
        return p

    groups = dict(a=[(0, 0)], b=[(1, 0), (2, 0), (3, 0)], c=[(0, 1)], d=[(1, 1), (2, 1), (3, 1)])
    handles, token = {}, conv_full
    for tag, items in groups.items():
        handles[tag], token = start_gather(tag, items, token)
    layers = [{n: w[n][l] for n in _SMALL[:-1]} for l in range(depth)]
    for l in range(depth):
        layers[l]['conv_w'] = conv_full[l]

    layers[0].update(layer_weights(0, finish_gather("a", handles["a"], groups["a"], token)))
    mix, sv0 = _layer_fwd(x[0], layers[0], 0)
    layers[0].update(layer_weights(0, finish_gather("b", handles["b"], groups["b"], mix)))
    h = _layer_fwd_mlp(layers[0], sv0, 0)
    layers[1].update(layer_weights(1, finish_gather("c", handles["c"], groups["c"], h)))
    mix, sv1 = _layer_fwd(h, layers[1], 1)
    layers[1].update(layer_weights(1, finish_gather("d", handles["d"], groups["d"], mix)))
    h = _layer_fwd_mlp(layers[1], sv1, 1)
    saved = [sv0, sv1]

    def by_chip(g, name):
        if name == "w_mlp_in":
            return g
        if name == "w_in":
            return g.reshape(d_model, n_chips, -1).transpose(1, 0, 2)
        return g.reshape(n_chips, -1, d_model)

    pending = []

    def sender(l):
        def send(names, g):
            srcs = [by_chip(g[n], n) for n in names]
            halves = [s.shape[1] // 2 for s in srcs]
            lands = [_SDS((len(_FLIPS), hf, s.shape[2]), _BF) for s, hf in zip(srcs, halves)]
            handle, tok = _send_start(f"grad_start_{names[-1]}_{l}", srcs, lands, _reduce_plan(halves), len(_FLIPS) * len(srcs), srcs[0])
            pending.append((l, names, srcs, handle))
            return tok
        return send

    dx, g_final, loss_part = _loss_bwd(h, final_norm_g, loss_target[0])
    grads, after = [None] * depth, None
    for l in reversed(range(depth)):
        dx, grads[l] = _layer_bwd(dx, layers[l], saved[l], l, sender(l), after)
        after = dx
    owned = {}
    for l, names, srcs, handle in pending:
        landed = _send_wait(f"grad_wait_{names[-1]}_{l}", handle, dx)
        owned.update(zip([(n, l) for n in names], _sum_owned(srcs, landed, c, chip, [f"{n}_{l}" for n in names])))
    keys = [(n, l) for n in _BIG for l in range(depth)]
    theirs = dict(zip(keys, _share_halves([owned[k] for k in keys])))
    red, delta, new_m, new_v = {}, {}, {}, {}
    for n in _BIG:
        red[n], delta[n], new_m[n], new_v[n] = _adamw_owned(
            w[n], [owned[(n, l)] for l in range(depth)], [theirs[(n, l)] for l in range(depth)], m[n], v[n], c, name=f"adamw_{n}")

    small = {n: jnp.stack([grads[l][n] for l in range(depth)]) for n in _SMALL[:-1] + ("conv_w",)}
    small["final_norm_g"] = g_final
    parts = [loss_part.reshape(1)] + [small[n] for n in _SMALL + ("conv_w",)]
    rows = -(-sum(p.size for p in parts) // 1024) * 8
    tot = _unpack(_exchange8(_pack(parts, rows), reduce=True, after=red[_BIG[0]], name="allreduce_small"), parts)
    loss = tot[0][0]
    red.update(zip(_SMALL + ("conv_w",), tot[1:]))
    red["conv_w"] = lax.dynamic_index_in_dim(red["conv_w"].reshape(depth, _CONV_K, n_chips, -1), chip, axis=2, keepdims=False)

    names = _SMALL + ("conv_w",)
    like = [w[n] for n in names]
    srows = -(-sum(p.size for p in like) // 1024) * 8
    res = _adamw(*[_pack([d[n] for n in names], srows) for d in (w, red, m, v)], name="adamw_small")
    for dst, buf in zip((delta, new_m, new_v), res):
        dst.update(zip(names, _unpack(buf, like)))
    return (loss, dx[None], *[red[n] for n in _ORDER], *[delta[n] for n in _ORDER], *[new_m[n] for n in _ORDER],
            *[new_v[n] for n in _ORDER])
```

```python
import numpy as np
import jax
import jax.numpy as jnp
from jax import lax
from jax.experimental import pallas as pl
from jax.experimental.pallas import tpu as pltpu

F32 = jnp.float32
_BF = jnp.bfloat16
_NEG = -1e30
_EPS = 1e-5
_HEADS = 16
_HDIM = 64
_AW = _HEADS * _HDIM
_ABLK = 128
_DILATIONS = (1, 4, 16)
_CHUNK = 128
_NSTATE = 128
_GROUPS = 2
_HPG = _HEADS // _GROUPS
_CONV_K = 4
_LANES = 128
_CHIPS = 4
_LR, _B1, _B2, _AEPS, _WD, _STEP = 0.001, 0.9, 0.999, 1e-08, 0.01, 10
_VMEM_CAP = 56 * 1024 * 1024
_MESH = pl.DeviceIdType.MESH
_SDS = jax.ShapeDtypeStruct
_NT = (((1,), (1,)), ((), ()))
_TN = (((0,), (0,)), ((), ()))


def _params(sem, est_bytes):
    lim = int(min(max(2 * est_bytes + (4 << 20), 16 << 20), _VMEM_CAP))
    return pltpu.CompilerParams(dimension_semantics=sem, vmem_limit_bytes=lim)


def _nbytes(shape, dtype):
    return int(np.prod(shape)) * jnp.dtype(dtype).itemsize


def _dot(a, b, dims=(((1,), (0,)), ((), ()))):
    return lax.dot_general(a.astype(_BF), b.astype(_BF), dims, preferred_element_type=F32)


_HALO = 8


def _rows(fn, ins, consts, outs, sums=(), *, halos=(), tile, name):
    rows = (ins[0][0] if isinstance(ins[0], tuple) else ins[0]).shape[0]
    n_steps = rows // tile

    def norm_in(a):
        if not isinstance(a, tuple):
            return a, tile, a.shape[1], lambda i: (i, 0)
        if isinstance(a[0], str):
            return a[1], tile // a[2], a[1].shape[1], lambda i: (i, 0)
        return a[0], tile, a[1], a[2] if callable(a[2]) else (lambda i, j=a[2]: (i, j))
    ins = [norm_in(a) for a in ins]
    outs = [(w, dt, d[0] if d else 1) for w, dt, *d in outs]
    n_in, n_h, n_c, n_o, n_s = len(ins), len(halos), len(consts), len(outs), len(sums)

    def body(*refs):
        step = pl.program_id(0)
        vals = [r[...] for r in refs[:n_in]]
        for r, (_, side) in zip(refs[n_in:n_in + n_h], halos):
            vals.append(jnp.where(step == (0 if side < 0 else n_steps - 1), 0.0, r[...]))
        vals += [r[...] for r in refs[n_in + n_h:n_in + n_h + n_c]]
        refs = refs[:n_in] + refs[n_in + n_h:]
        res = fn(*vals)
        res = res if isinstance(res, tuple) else (res,)
        orefs = refs[n_in + n_c:n_in + n_c + n_o]
        srefs = refs[n_in + n_c + n_o:]
        for r, v in zip(orefs, res[:n_o]):
            r[...] = v.astype(r.dtype)
        if n_s:
            @pl.when(pl.program_id(0) == 0)
            def _():
                for r in srefs:
                    r[...] = jnp.zeros_like(r)
            for r, v in zip(srefs, res[n_o:]):
                r[...] += v.reshape(tile // 8, 8, v.shape[-1]).sum(axis=0)

    per = tile // _HALO
    in_specs = [pl.BlockSpec((r, w), idx) for _, r, w, idx in ins]
    in_specs += [pl.BlockSpec((_HALO, a.shape[1]), (lambda i: (jnp.maximum(i * per - 1, 0), 0)) if side < 0
                              else (lambda i: (jnp.minimum((i + 1) * per, rows // _HALO - 1), 0))) for a, side in halos]
    in_specs += [pl.BlockSpec(c.shape, lambda i, nd=c.ndim: (0,) * nd) for c in consts]
    out_shape = [_SDS((rows // d, d * w), dt) for w, dt, d in outs] + [_SDS((8, w), F32) for w in sums]
    out_specs = [pl.BlockSpec((tile // d, d * w), lambda i: (i, 0)) for w, _, d in outs]
    out_specs += [pl.BlockSpec((8, w), lambda i: (0, 0)) for w in sums]
    est = sum(_nbytes((r, w), a.dtype) for a, r, w, _ in ins) + sum(_nbytes((tile, w), dt) for w, dt, _ in outs)
    return pl.pallas_call(body, grid=(n_steps,), in_specs=in_specs, out_specs=out_specs, out_shape=out_shape, name=name,
                          compiler_params=_params(("arbitrary",), 3 * est))(*[a[0] for a in ins], *[a for a, _ in halos], *consts)


def _perm(d, tile):
    p = np.zeros((tile, tile), np.float32)
    t = np.arange(tile)
    p[t, (t % d) * (tile // d) + t // d] = 1.0
    return jnp.asarray(p, _BF)


def _unstride(s, p):
    d = p.shape[0] // s.shape[0]
    w = s.shape[1] // d
    return _dot(p, jnp.concatenate([s[:, r * w:(r + 1) * w] for r in range(d)], axis=0))


def _stride(x, p, d):
    z = _dot(p, x, _TN)
    n = x.shape[0] // d
    return jnp.concatenate([z[r * n:(r + 1) * n] for r in range(d)], axis=1)


def _shifted(u, halo, back):
    n = u.shape[0] + _HALO
    if back:
        ext = jnp.concatenate([halo, u], axis=0)
        return [pltpu.roll(ext, j, 0)[_HALO:] for j in (1, 2, 3)]
    ext = jnp.concatenate([u, halo], axis=0)
    return [pltpu.roll(ext, n - j, 0)[:u.shape[0]] for j in (1, 2, 3)]


def _tile_for(width):
    return max(c for c in (256, 128, 64, 32) if c * width <= (1 << 18) or c == 32)


def _rstd(x):
    return lax.rsqrt(jnp.mean(x * x, axis=-1, keepdims=True) + _EPS)


def _split(x, groups):
    w = x.shape[-1] // groups
    return [x[:, g * w:(g + 1) * w] for g in range(groups)]


def _cat(parts):
    return parts[0] if len(parts) == 1 else jnp.concatenate(parts, axis=-1)


def _rms_bwd_tile(x, dy, g, groups):
    dxs, dgs = [], []
    for xs, ds, gs in zip(_split(x, groups), _split(dy.astype(F32), groups), _split(g, groups)):
        r = _rstd(xs)
        xh = xs * r
        gd = ds * gs
        dxs.append(r * (gd - xh * jnp.mean(gd * xh, axis=-1, keepdims=True)))
        dgs.append(ds * xh)
    return _cat(dxs), _cat(dgs)


def _rms_fwd(x, g, *, groups=1, name):
    def fn(x, g):
        return _cat([xs * _rstd(xs) * gs for xs, gs in zip(_split(x, groups), _split(g, groups))])
    w = x.shape[1]
    return _rows(fn, [x], [g.reshape(1, w)], [(w, _BF)], tile=_tile_for(w), name=name)[0]


def _rms_bwd(x, dy, g, res=None, *, name):
    def fn(x, dy, *rest):
        dx, dg = _rms_bwd_tile(x, dy, rest[-1], 1)
        return (dx + rest[0] if res is not None else dx), dg
    w = x.shape[1]
    ins = [x, dy] + ([res] if res is not None else [])
    dx, dg = _rows(fn, ins, [g.reshape(1, w)], [(w, F32)], [w], tile=_tile_for(w), name=name)
    return dx, dg.sum(axis=0)


def _pick(n, cands):
    for c in cands:
        if n % c == 0:
            return c
    raise ValueError(f"no block size for {n}")


_MM_BLOCKS = (1024, 640, 512, 384)


def _mm(a, b, *, ta=False, tb=False, extra=(), epi=None, outs=(F32,), after=None, b_chips=0, out_chips=0, name):
    m, k = (a.shape[1], a.shape[0]) if ta else a.shape
    b_shape = (b.shape[1], b.shape[2] * b_chips) if b_chips else b.shape
    n = b_shape[0] if tb else b_shape[1]
    assert k == (b_shape[1] if tb else b_shape[0])
    n_cap = n // max(out_chips, 1 if tb else b_chips, 1)
    k_cap = k // (b_chips if (b_chips and tb) else 1)
    bm, bn = _pick(m, _MM_BLOCKS), _pick(n_cap, _MM_BLOCKS)
    bk = _pick(k_cap, (2048,) + _MM_BLOCKS)
    nk = k // bk
    n_e, n_o = len(extra), len(outs)
    behind = [] if after is None else [after]
    dims = (((0 if ta else 1,), (1 if tb else 0,)), ((), ()))

    def body(a_ref, b_ref, *rest):
        ex, orefs, acc = rest[:n_e], rest[n_e + len(behind):n_e + len(behind) + n_o], rest[-1]
        kk = pl.program_id(2)

        @pl.when(kk == 0)
        def _():
            acc[...] = jnp.zeros_like(acc)

        acc[...] += _dot(a_ref[...], b_ref[...], dims)

        @pl.when(kk == nk - 1)
        def _():
            r = acc[...]
            res = epi(r, *[e[...] for e in ex]) if epi is not None else (r,)
            for o, v in zip(orefs, res):
                o[...] = v.astype(o.dtype)

    a_spec = pl.BlockSpec((bk, bm), lambda i, j, kk: (kk, i)) if ta else pl.BlockSpec((bm, bk), lambda i, j, kk: (i, kk))
    if b_chips and tb:
        per = k_cap // bk
        b_spec = pl.BlockSpec((None, bn, bk), lambda i, j, kk: (kk // per, j, kk % per))
    elif b_chips:
        per = n_cap // bn
        b_spec = pl.BlockSpec((None, bk, bn), lambda i, j, kk: (j // per, kk, j % per))
    else:
        b_spec = pl.BlockSpec((bn, bk), lambda i, j, kk: (j, kk)) if tb else pl.BlockSpec((bk, bn), lambda i, j, kk: (kk, j))
    t_spec = pl.BlockSpec((bm, bn), lambda i, j, kk: (i, j))
    o_spec, o_shape = t_spec, (m, n)
    if out_chips:
        per_o = n_cap // bn
        o_spec, o_shape = pl.BlockSpec((None, bm, bn), lambda i, j, kk: (j // per_o, i, j % per_o)), (out_chips, m, n_cap)
    est = (_nbytes((bm, bk), a.dtype) + _nbytes((bk, bn), b.dtype) + sum(_nbytes((bm, bn), e.dtype) for e in extra)
           + sum(_nbytes((bm, bn), o) for o in outs)) * 2 + 2 * _nbytes((bm, bn), F32)
    res = pl.pallas_call(
        body, grid=(m // bm, n // bn, nk), in_specs=[a_spec, b_spec] + [t_spec] * n_e + [pl.BlockSpec(memory_space=pl.ANY)] * len(behind),
        out_specs=[o_spec] * n_o, out_shape=[_SDS(o_shape, o) for o in outs], scratch_shapes=[pltpu.VMEM((bm, bn), F32)], name=name,
        compiler_params=_params(("parallel", "parallel", "arbitrary"), est))(a, b, *extra, *behind)
    return res[0] if n_o == 1 else res


def _add_to(acc, r):
    return (acc + r,)


def _alibi_bias(dilation):
    slopes = 2.0 ** (-8.0 * (np.arange(_HEADS) + 1) / _HEADS)
    i = np.arange(_ABLK)[:, None]
    j = np.arange(_ABLK)[None, :]
    cur = np.where(i - j >= 0, -slopes[:, None, None] * ((i - j) * dilation), _NEG)
    prev = np.where(j >= i, -slopes[:, None, None] * ((i - j + _ABLK) * dilation), _NEG)
    both = np.concatenate([prev, cur], axis=2)
    return jnp.asarray(both.reshape(_HEADS // 2, 2 * _ABLK, 2 * _ABLK), F32)


def _strided(a, d):
    return a.reshape(a.shape[0] // d, d * a.shape[1])


def _head(h):
    return slice(h * _HDIM, (h + 1) * _HDIM)


def _pair(pr):
    return slice(pr * _LANES, (pr + 1) * _LANES)


def _low_lanes(shape):
    return lax.broadcasted_iota(jnp.int32, shape, 1) < _HDIM


def _halves(v, low):
    z = jnp.zeros_like(v)
    return jnp.where(low, v, z), jnp.where(low, z, v)


def _no_prev_mask(first):
    return jnp.logical_and(first, lax.broadcasted_iota(jnp.int32, (2 * _ABLK, 2 * _ABLK), 1) < _ABLK)


def _lane_spec(nb):
    return pl.BlockSpec((_ABLK, _LANES), lambda r, j: (jnp.minimum(j, nb - 1), r))


def _expand_heads(v):
    low = _low_lanes((v.shape[0], _LANES))
    return jnp.concatenate([jnp.where(low, v[:, 2 * pr:2 * pr + 1], v[:, 2 * pr + 1:2 * pr + 2]) for pr in range(_HEADS // 2)], axis=1)


def _attn_specs(nb, n_parts):
    def cur(p):
        return pl.BlockSpec((_ABLK, _AW), lambda r, j: (jnp.minimum(j, nb - 1), r * n_parts + p))

    def prev(p):
        return pl.BlockSpec((_ABLK, _AW), lambda r, j: (jnp.clip(j - 1, 0, nb - 1), r * n_parts + p))
    return cur, prev


def _attn_fwd(qkv, dilation, *, name):
    t = qkv.shape[0]
    nb = t // dilation // _ABLK
    bias = _alibi_bias(dilation)
    scale = _HDIM ** -0.5

    def body(q_ref, kc_ref, kp_ref, vc_ref, vp_ref, b_ref, o_ref, l_ref):
        no_prev = _no_prev_mask(pl.program_id(1) == 0)
        low = _low_lanes((_ABLK, _LANES))
        l_ref[...] = jnp.zeros_like(l_ref)
        for pr in range(_HEADS // 2):
            sl = _pair(pr)
            k2 = jnp.concatenate([kp_ref[:, sl], kc_ref[:, sl]], axis=0)
            v2 = jnp.concatenate([vp_ref[:, sl], vc_ref[:, sl]], axis=0)
            q2 = jnp.concatenate(_halves(q_ref[:, sl], low), axis=0)
            s = jnp.where(no_prev, _NEG, _dot(q2, k2, _NT) * scale + b_ref[pr])
            m = jnp.max(s, axis=-1, keepdims=True)
            p = jnp.exp(s - m)
            den = jnp.sum(p, axis=-1, keepdims=True)
            o = _dot(p, v2) / den
            lse = m + jnp.log(den)
            l_ref[:, 2 * pr:2 * pr + 1] = lse[:_ABLK]
            l_ref[:, 2 * pr + 1:2 * pr + 2] = lse[_ABLK:]
            o_ref[:, sl] = jnp.where(low, o[:_ABLK], o[_ABLK:]).astype(o_ref.dtype)

    cur, prev = _attn_specs(nb, 3)
    cur1, _ = _attn_specs(nb, 1)
    bspec = pl.BlockSpec((_HEADS // 2, 2 * _ABLK, 2 * _ABLK), lambda r, j: (0, 0, 0))
    sv = _strided(qkv, dilation)
    o, l = pl.pallas_call(
        body, grid=(dilation, nb), in_specs=[cur(0), cur(1), prev(1), cur(2), prev(2), bspec],
        out_specs=[cur1(0), _lane_spec(nb)],
        out_shape=[_SDS((t // dilation, dilation * _AW), _BF), _SDS((t // dilation, dilation * _LANES), F32)], name=name,
        compiler_params=_params(("parallel", "arbitrary"), 16 << 20))(sv, sv, sv, sv, sv, bias)
    return o, l.reshape(t, _LANES)


def _attn_bwd(qkv, do, ld, dilation, *, name):
    t = qkv.shape[0]
    nb = t // dilation // _ABLK
    bias = _alibi_bias(dilation)
    scale = _HDIM ** -0.5

    def body(q_ref, kc_ref, kp_ref, vc_ref, vp_ref, do_ref, ld_ref, b_ref, dq_ref, dk_ref, dv_ref, ck, cv):
        n = pl.program_id(1)

        @pl.when(n == 0)
        def _():
            ck[...] = jnp.zeros_like(ck)
            cv[...] = jnp.zeros_like(cv)

        @pl.when(n < nb)
        def _():
            low = _low_lanes((_ABLK, _LANES))
            no_prev = _no_prev_mask(n == 0)
            for pr in range(_HEADS // 2):
                sl = _pair(pr)
                k2 = jnp.concatenate([kp_ref[:, sl], kc_ref[:, sl]], axis=0)
                v2 = jnp.concatenate([vp_ref[:, sl], vc_ref[:, sl]], axis=0)
                q2 = jnp.concatenate(_halves(q_ref[:, sl], low), axis=0)
                do2 = jnp.concatenate(_halves(do_ref[:, sl], low), axis=0)
                lrow = jnp.concatenate([ld_ref[:, 2 * pr:2 * pr + 1], ld_ref[:, 2 * pr + 1:2 * pr + 2]], axis=0)
                dsum = jnp.concatenate([ld_ref[:, _HEADS + 2 * pr:_HEADS + 2 * pr + 1],
                                        ld_ref[:, _HEADS + 2 * pr + 1:_HEADS + 2 * pr + 2]], axis=0)
                p = jnp.exp(jnp.where(no_prev, _NEG, _dot(q2, k2, _NT) * scale + b_ref[pr]) - lrow)
                ds = (p * (_dot(do2, v2, _NT) - dsum)).astype(_BF)
                dq = _dot(ds, k2)
                dk2, dv2 = _dot(ds, q2, _TN), _dot(p, do2, _TN)
                dq_ref[:, sl] = (jnp.where(low, dq[:_ABLK], dq[_ABLK:]) * scale).astype(dq_ref.dtype)
                dk_ref[:, sl] = (ck[:, sl] + dk2[:_ABLK] * scale).astype(dk_ref.dtype)
                dv_ref[:, sl] = (cv[:, sl] + dv2[:_ABLK]).astype(dv_ref.dtype)
                ck[:, sl] = dk2[_ABLK:] * scale
                cv[:, sl] = dv2[_ABLK:]

        @pl.when(n == nb)
        def _():
            dk_ref[...] = ck[...].astype(dk_ref.dtype)
            dv_ref[...] = cv[...].astype(dv_ref.dtype)

    cur, prev = _attn_specs(nb, 3)
    cur1, prev1 = _attn_specs(nb, 1)
    bspec = pl.BlockSpec((_HEADS // 2, 2 * _ABLK, 2 * _ABLK), lambda r, j: (0, 0, 0))
    sv, dov, ldv = _strided(qkv, dilation), do, _strided(ld, dilation)
    dqkv = pl.pallas_call(
        body, grid=(dilation, nb + 1),
        in_specs=[cur(0), cur(1), prev(1), cur(2), prev(2), cur1(0), _lane_spec(nb), bspec],
        out_specs=[cur1(0), prev1(0), prev1(0)], out_shape=[_SDS(dov.shape, _BF)] * 3, name=name,
        scratch_shapes=[pltpu.VMEM((_ABLK, _AW), F32)] * 2,
        compiler_params=_params(("parallel", "arbitrary"), 16 << 20))(sv, sv, sv, sv, sv, dov, ldv, bias)
    return dqkv


def _ssd_in_specs(ch):
    return dict(
        xs=pl.BlockSpec((_CHUNK, _AW), lambda c: (ch(c), 0)),
        bc=pl.BlockSpec((_CHUNK, 2 * _GROUPS * _NSTATE), lambda c: (ch(c), _AW // (2 * _GROUPS * _NSTATE))),
        lane=pl.BlockSpec((_CHUNK, _LANES), lambda c: (ch(c), 0)),
        arow=pl.BlockSpec((_HEADS, 1, _CHUNK), lambda c: (0, 0, ch(c))),
        st=pl.BlockSpec((1, _HEADS // 2, _NSTATE, _LANES), lambda c: (ch(c), 0, 0, 0)),
    )


def _decay(a_col, a_row):
    i0 = lax.broadcasted_iota(jnp.int32, (_CHUNK, _CHUNK), 0)
    i1 = lax.broadcasted_iota(jnp.int32, (_CHUNK, _CHUNK), 1)
    return jnp.where(i0 >= i1, jnp.exp(a_col - a_row), 0.0), jnp.where(i1 >= i0, jnp.exp(a_row - a_col), 0.0)


def _rsum(v):
    return jnp.sum(v, axis=-1, keepdims=True)


def _ssd_fwd(act, dt, acum, a_row, *, name):
    t = act.shape[0]
    nc = t // _CHUNK
    sp = _ssd_in_specs(lambda c: c)
    gw = _GROUPS * _NSTATE

    def body(xs_ref, bc_ref, dt_ref, ac_ref, ar_ref, y_ref, sall_ref, st):
        @pl.when(pl.program_id(0) == 0)
        def _():
            st[...] = jnp.zeros_like(st)

        low = _low_lanes((_CHUNK, _LANES))
        for g in range(_GROUPS):
            bg = bc_ref[:, g * _NSTATE:(g + 1) * _NSTATE]
            cg = bc_ref[:, gw + g * _NSTATE:gw + (g + 1) * _NSTATE].astype(_BF)
            cb = _dot(cg, bg, _NT)
            for pr in range(g * _HPG // 2, (g + 1) * _HPG // 2):
                ha, hb = 2 * pr, 2 * pr + 1
                a_a, a_b = ac_ref[:, ha:ha + 1], ac_ref[:, hb:hb + 1]
                x = (xs_ref[:, _pair(pr)] * jnp.where(low, dt_ref[:, ha:ha + 1], dt_ref[:, hb:hb + 1])).astype(_BF)
                lm_a, _ = _decay(a_a, ar_ref[ha])
                lm_b, _ = _decay(a_b, ar_ref[hb])
                sv = st[pr]
                sall_ref[0, pr] = sv
                yd = jnp.where(low, _dot(cb * lm_a, x), _dot(cb * lm_b, x))
                y_ref[:, _pair(pr)] = yd + jnp.where(low, jnp.exp(a_a), jnp.exp(a_b)) * _dot(cg, sv)
                al_a, al_b = jnp.min(a_a, axis=0, keepdims=True), jnp.min(a_b, axis=0, keepdims=True)
                st[pr] = (jnp.where(low, jnp.exp(al_a), jnp.exp(al_b)) * sv
                          + jnp.where(low, _dot(bg * jnp.exp(al_a - a_a), x, _TN), _dot(bg * jnp.exp(al_b - a_b), x, _TN)))

    return pl.pallas_call(
        body, grid=(nc,), in_specs=[sp['xs'], sp['bc'], sp['lane'], sp['lane'], sp['arow']],
        out_specs=[sp['xs'], sp['st']], out_shape=[_SDS((t, _AW), F32), _SDS((nc, _HEADS // 2, _NSTATE, _LANES), F32)],
        scratch_shapes=[pltpu.VMEM((_HEADS // 2, _NSTATE, _LANES), F32)], name=name,
        compiler_params=_params(("arbitrary",), 16 << 20))(act, act, dt, acum, a_row)


def _ssd_bwd(act, dt, acum, a_row, sall, dy, *, name):
    t = act.shape[0]
    nc = t // _CHUNK
    sp = _ssd_in_specs(lambda c: nc - 1 - c)
    gw = _GROUPS * _NSTATE

    def body(xs_ref, bc_ref, dt_ref, ac_ref, ar_ref, sall_ref, dy_ref, dxs_ref, dbc_ref, ddt_ref, da_ref, dst):
        @pl.when(pl.program_id(0) == 0)
        def _():
            dst[...] = jnp.zeros_like(dst)

        ddt_ref[...] = jnp.zeros_like(ddt_ref)
        da_ref[...] = jnp.zeros_like(da_ref)
        row = lax.broadcasted_iota(jnp.int32, (_CHUNK, 1), 0)
        low = _low_lanes((_CHUNK, _LANES))
        for g in range(_GROUPS):
            bg = bc_ref[:, g * _NSTATE:(g + 1) * _NSTATE]
            bgb = bg.astype(_BF)
            cg = bc_ref[:, gw + g * _NSTATE:gw + (g + 1) * _NSTATE].astype(_BF)
            cb, cbt = _dot(cg, bgb, _NT), _dot(bgb, cg, _NT)
            dcb = jnp.zeros((_CHUNK, _CHUNK), F32)
            dbg = jnp.zeros((_CHUNK, _NSTATE), F32)
            dcg = jnp.zeros((_CHUNK, _NSTATE), F32)
            for pr in range(g * _HPG // 2, (g + 1) * _HPG // 2):
                heads = (2 * pr, 2 * pr + 1)
                a_cols = [ac_ref[:, h:h + 1] for h in heads]
                dt_pair = jnp.where(low, dt_ref[:, heads[0]:heads[0] + 1], dt_ref[:, heads[1]:heads[1] + 1])
                xsv = xs_ref[:, _pair(pr)]
                x = xsv * dt_pair
                xb = x.astype(_BF)
                xhs = _halves(xb, low)
                dyv = dy_ref[:, _pair(pr)]
                dyb = dyv.astype(_BF)
                dyhs = _halves(dyb, low)
                sv, dsv = sall_ref[0, pr], dst[pr]
                svb, dsb = sv.astype(_BF), dsv.astype(_BF)
                a_lasts = [jnp.min(a, axis=0, keepdims=True) for a in a_cols]
                e_pair = jnp.where(low, jnp.exp(a_cols[0]), jnp.exp(a_cols[1]))
                el_pair = jnp.where(low, jnp.exp(a_lasts[0]), jnp.exp(a_lasts[1]))
                yo = e_pair * _dot(cg, svb)
                dxg, bwds, das = [], [], []
                for i, h in enumerate(heads):
                    lm, lmt = _decay(a_cols[i], ar_ref[h])
                    gm, gmt = cb * lm, cbt * lmt
                    w_col = jnp.exp(a_lasts[i] - a_cols[i])
                    bwds.append(_dot(bg * w_col, dsb))
                    dxg.append(_dot(gm, dyb, _TN))
                    dg, dgt = _dot(dyhs[i], xb, _NT), _dot(xhs[i], dyb, _NT)
                    dcb = dcb + dg * lm
                    dbg = dbg + w_col * _dot(xhs[i], dsb, _NT)
                    das.append(_rsum(dg * gm) - _rsum(dgt * gmt))
                bwd = jnp.where(low, bwds[0], bwds[1])
                dx = jnp.where(low, dxg[0], dxg[1]) + bwd
                edy = (e_pair * dyv).astype(_BF)
                dcg = dcg + _dot(edy, svb, _NT)
                zs, yos, sds, dts = (_halves(v, low) for v in (x * bwd, dyv * yo, sv * dsv, dx * xsv))
                for i, h in enumerate(heads):
                    z = _rsum(zs[i])
                    da_last = jnp.sum(z, axis=0, keepdims=True) + jnp.exp(a_lasts[i]) * jnp.sum(_rsum(sds[i]), axis=0, keepdims=True)
                    da_ref[:, h:h + 1] = das[i] + _rsum(yos[i]) - z + jnp.where(row == _CHUNK - 1, da_last, 0.0)
                    ddt_ref[:, h:h + 1] = _rsum(dts[i])
                dxs_ref[:, _pair(pr)] = dx * dt_pair
                dst[pr] = el_pair * dsv + _dot(cg, edy, _TN)
            dbc_ref[:, g * _NSTATE:(g + 1) * _NSTATE] = dbg + _dot(dcb, cg, _TN)
            dbc_ref[:, gw + g * _NSTATE:gw + (g + 1) * _NSTATE] = dcg + _dot(dcb, bgb)

    ch = lambda c: nc - 1 - c
    wide = pl.BlockSpec((_CHUNK, 2 * gw), lambda c: (ch(c), 0))
    return pl.pallas_call(
        body, grid=(nc,), in_specs=[sp['xs'], sp['bc'], sp['lane'], sp['lane'], sp['arow'], sp['st'], sp['xs']],
        out_specs=[sp['xs'], wide, sp['lane'], sp['lane']],
        out_shape=[_SDS((t, _AW), F32), _SDS((t, 2 * gw), F32), _SDS((t, _LANES), F32), _SDS((t, _LANES), F32)],
        scratch_shapes=[pltpu.VMEM((_HEADS // 2, _NSTATE, _LANES), F32)], name=name,
        compiler_params=_params(("arbitrary",), 16 << 20))(act, act, dt, acum, a_row, sall, dy)


def _scan_rows(v, reverse):
    r = lax.broadcasted_iota(jnp.int32, v.shape, 0)
    for s in (1, 2, 4, 8, 16, 32, 64):
        if reverse:
            v = v + jnp.where(r < _CHUNK - s, pltpu.roll(v, _CHUNK - s, 0), 0.0)
        else:
            v = v + jnp.where(r >= s, pltpu.roll(v, s, 0), 0.0)
    return v


def _softplus(x):
    return jnp.maximum(x, 0.0) + jnp.log(1.0 + jnp.exp(-jnp.abs(x)))


def _sigmoid(x):
    return 1.0 / (1.0 + jnp.exp(-x))


def _silu(x):
    return x * _sigmoid(x)


def _dsilu(x):
    s = _sigmoid(x)
    return s * (1.0 + x * (1.0 - s))


def _lanes(a):
    return jnp.pad(a, (0, _LANES - a.shape[0])).reshape(1, _LANES)


def _layer_fwd(x, p, l):
    cch = p['conv_w'].shape[1]
    sv = {}
    h1 = _rms_fwd(x, p['ln1_g'], name=f"ln1_fwd_{l}")
    qkv = _mm(h1, p['w_qkv'], outs=(_BF,), name=f"in_proj_qkv_{l}")
    xbc = _mm(h1, p['w_xbc'], name=f"in_proj_xbc_{l}")
    zdt = _mm(h1, p['w_zdt'], name=f"in_proj_zdt_{l}")
    z, dt_raw = (zdt, _AW, 0), (zdt, _LANES, _AW // _LANES)

    outs = []
    for dil in _DILATIONS:
        outs += _attn_fwd(qkv, dil, name=f"attn_fwd_d{dil}_{l}")

    tile = 2 * _ABLK
    perms = [_perm(d, tile) for d in _DILATIONS[1:]]

    def combine(o1, l1, o2, l2, o3, l3, p2, p3):
        m = jnp.maximum(jnp.maximum(l1, l2), l3)
        e1, e2, e3 = jnp.exp(l1 - m), jnp.exp(l2 - m), jnp.exp(l3 - m)
        tot = e1 + e2 + e3
        mixed = sum(_expand_heads(e / tot) * o for e, o in ((e1, o1.astype(F32)), (e2, _unstride(o2, p2)), (e3, _unstride(o3, p3))))
        return mixed, m + jnp.log(tot)
    outs = [a if i % 2 or i == 0 else ("strided", a, _DILATIONS[i // 2]) for i, a in enumerate(outs)]
    attn, lse = _rows(combine, outs, perms, [(_AW, F32), (_LANES, F32)], tile=tile, name=f"attn_combine_{l}")
    attn_n = _rms_fwd(attn, p['attn_norm_g'], name=f"attn_norm_fwd_{l}")

    def conv(u0, before, w, b):
        u1, u2, u3 = _shifted(u0, before, True)
        return _silu(w[0:1] * u3 + w[1:2] * u2 + w[2:3] * u1 + w[3:4] * u0 + b)
    act = _rows(conv, [xbc], [p['conv_w'], p['conv_b'].reshape(1, cch)], [(cch, F32)], halos=[(xbc, -1)], tile=_tile_for(cch),
                name=f"conv_fwd_{l}")[0]

    def dtf(raw, bias, alog):
        dt = _softplus(raw + bias)
        return dt, _scan_rows(dt * -jnp.exp(alog), False)
    dt, acum = _rows(dtf, [dt_raw], [_lanes(p['dt_bias']), _lanes(p['a_log'])], [(_LANES, F32), (_LANES, F32)],
                     tile=_CHUNK, name=f"dt_fwd_{l}")
    a_row = acum[:, :_HEADS].T[:, None, :]
    y_ssd, sall = _ssd_fwd(act, dt, acum, a_row, name=f"ssd_fwd_{l}")
    dskip = jnp.repeat(p['d_skip'], _HDIM).reshape(1, _AW)
    xs = (act, _AW, 0)

    def gate(y, xs, z, dsk):
        return (y + dsk * xs) * _silu(z)
    y2 = _rows(gate, [y_ssd, xs, z], [dskip], [(_AW, F32)], tile=_tile_for(_AW), name=f"gate_fwd_{l}")[0]
    y_n = _rms_fwd(y2, p['ssd_norm_g'], groups=_GROUPS, name=f"ssd_norm_fwd_{l}")

    mix = jnp.concatenate([attn_n, y_n], axis=1)
    sv.update(x=x, h1=h1, qkv=qkv, zdt=zdt, xbc=xbc, attn=attn, lse=lse, act=act, dt=dt, acum=acum, a_row=a_row,
              sall=sall, y_ssd=y_ssd, dskip=dskip, y2=y2, mix=mix)
    return mix, sv


def _layer_fwd_mlp(p, sv, l):
    x2 = _mm(sv['mix'], p['w_out'], extra=(sv['x'],), epi=_add_to, name=f"out_proj_{l}")
    h2 = _rms_fwd(x2, p['ln2_g'], name=f"ln2_fwd_{l}")
    a = _mm(h2, p['w_mlp_in'], b_chips=_CHIPS, epi=lambda acc: (jnp.square(jnp.maximum(acc, 0.0)),), outs=(_BF,), name=f"mlp_in_{l}")
    x3 = _mm(a, p['w_mlp_out'], extra=(x2,), epi=_add_to, name=f"mlp_out_{l}")
    sv.update(x2=x2, h2=h2, a=a)
    return x3


def _layer_bwd(dx3, p, sv, l, send, after):
    cch = p['conv_w'].shape[1]
    g = {}
    dx3b = dx3.astype(_BF)
    du = _mm(dx3b, p['w_mlp_out'], tb=True, extra=(sv['a'],), outs=(_BF,), after=after,
             epi=lambda acc, a: (acc * 2.0 * jnp.sqrt(a.astype(F32)),), name=f"mlp_out_dx_{l}")
    g['w_mlp_out'] = _mm(sv['a'], dx3b, ta=True, outs=(_BF,), name=f"mlp_out_dw_{l}")
    g['w_mlp_in'] = _mm(sv['h2'], du, ta=True, out_chips=_CHIPS, outs=(_BF,), name=f"mlp_in_dw_{l}")
    sent = send(('w_mlp_out', 'w_mlp_in'), g)
    dh2 = _mm(du, p['w_mlp_in'], tb=True, b_chips=_CHIPS, after=sent, name=f"mlp_in_dx_{l}")
    dx2, g['ln2_g'] = _rms_bwd(sv['x2'], dh2, p['ln2_g'], dx3, name=f"ln2_bwd_{l}")
    dx2b = dx2.astype(_BF)
    dmix = _mm(dx2b, p['w_out'], tb=True, name=f"out_proj_dx_{l}")
    g['w_out'] = _mm(sv['mix'], dx2b, ta=True, outs=(_BF,), name=f"out_proj_dw_{l}")

    tile = 2 * _ABLK
    perms = [_perm(d, tile) for d in _DILATIONS[1:]]

    def norm_bwd(attn, dy, lse, gn, p2, p3):
        dattn, dgn = _rms_bwd_tile(attn, dy, gn, 1)
        prod, low = dattn * attn, _low_lanes((attn.shape[0], _LANES))
        lane = lax.broadcasted_iota(jnp.int32, lse.shape, 1)
        ld = jnp.where(lane < _HEADS, lse, 0.0)
        for pr in range(_HEADS // 2):
            for i, part in enumerate(_halves(prod[:, _pair(pr)], low)):
                ld = jnp.where(lane == _HEADS + 2 * pr + i, _rsum(part), ld)
        return dattn, _stride(dattn, p2, _DILATIONS[1]), _stride(dattn, p3, _DILATIONS[2]), ld, dgn
    *dos, ld, gn_sum = _rows(norm_bwd, [sv['attn'], (dmix, _AW, 0), sv['lse']], [p['attn_norm_g'].reshape(1, _AW)] + perms,
                             [(_AW, _BF)] + [(_AW, _BF, d) for d in _DILATIONS[1:]] + [(_LANES, F32)], [_AW], tile=tile,
                             name=f"attn_norm_bwd_{l}")
    g['attn_norm_g'] = gn_sum.sum(axis=0)
    parts = [_attn_bwd(sv['qkv'], do, ld, dil, name=f"attn_bwd_d{dil}_{l}") for do, dil in zip(dos, _DILATIONS)]

    def branch_sum(*t):
        parts_, (p2, p3) = t[:9], t[9:]
        t = [a.astype(F32) for a in parts_[:3]] + [_unstride(a, p2) for a in parts_[3:6]] + [_unstride(a, p3) for a in parts_[6:]]
        return jnp.concatenate([t[i] + t[3 + i] + t[6 + i] for i in range(3)], axis=1)
    branch_ins = list(parts[0]) + [("strided", a, d) for pr, d in zip(parts[1:], _DILATIONS[1:]) for a in pr]
    dqkv = _rows(branch_sum, branch_ins, perms, [(3 * _AW, _BF)], tile=tile, name=f"attn_bwd_sum_{l}")[0]

    xs, z, dt_raw = (sv['act'], _AW, 0), (sv['zdt'], _AW, 0), (sv['zdt'], _LANES, _AW // _LANES)

    def gate_bwd(y2, dy, y, xs, z, dsk, gn):
        dy2, dgn = _rms_bwd_tile(y2, dy, gn, _GROUPS)
        dy1 = dy2 * _silu(z)
        return dy1, dsk * dy1, dy2 * (y + dsk * xs) * _dsilu(z), dy1 * xs, dgn
    dy1, dxs_skip, dz, dsk_sum, gn_sum = _rows(
        gate_bwd, [sv['y2'], (dmix, _AW, 1), sv['y_ssd'], xs, z], [sv['dskip'], p['ssd_norm_g'].reshape(1, _AW)],
        [(_AW, F32), (_AW, F32), (_AW, _BF)], [_AW, _AW], tile=128, name=f"gate_bwd_{l}")
    g['ssd_norm_g'] = gn_sum.sum(axis=0)
    g['d_skip'] = dsk_sum.sum(axis=0).reshape(_HEADS, _HDIM).sum(axis=1)
    dxs, dbc, ddt, da = _ssd_bwd(sv['act'], sv['dt'], sv['acum'], sv['a_row'], sv['sall'], dy1, name=f"ssd_bwd_{l}")

    def dtb(da, ddtx, raw, dt, dz, bias, alog):
        a = -jnp.exp(alog)
        dda = _scan_rows(da, True)
        draw = (dda * a + ddtx) * _sigmoid(raw + bias)
        return jnp.concatenate([dz, draw.astype(dz.dtype)], axis=1), draw, dda * dt * a
    dzdt, dbias, dalog = _rows(dtb, [da, ddt, dt_raw, sv['dt'], dz], [_lanes(p['dt_bias']), _lanes(p['a_log'])],
                               [(_AW + _LANES, _BF)], [_LANES, _LANES], tile=_CHUNK, name=f"dt_bwd_{l}")
    g['dt_bias'], g['a_log'] = dbias.sum(axis=0)[:_HEADS], dalog.sum(axis=0)[:_HEADS]
    def conv_bwd1(u0, dxs, dbc, dxk, before, w, b):
        u1, u2, u3 = _shifted(u0, before, True)
        pre = w[0:1] * u3 + w[1:2] * u2 + w[2:3] * u1 + w[3:4] * u0 + b
        dp = jnp.concatenate([dxs + dxk, dbc], axis=1) * _dsilu(pre)
        return dp, dp * u3, dp * u2, dp * u1, dp * u0, dp
    dpre, *dws = _rows(conv_bwd1, [sv['xbc'], dxs, dbc, dxs_skip], [p['conv_w'], p['conv_b'].reshape(1, cch)], [(cch, F32)],
                       [cch] * 5, halos=[(sv['xbc'], -1)], tile=128, name=f"conv_bwd_pre_{l}")
    g['conv_w'] = jnp.stack([dws[i].sum(axis=0) for i in range(_CONV_K)])
    g['conv_b'] = dws[4].sum(axis=0)

    def conv_bwd2(p0, after_, w):
        p1, p2, p3 = _shifted(p0, after_, False)
        return w[3:4] * p0 + w[2:3] * p1 + w[1:2] * p2 + w[0:1] * p3
    dxbc = _rows(conv_bwd2, [dpre], [p['conv_w']], [(cch, _BF)], halos=[(dpre, 1)], tile=_tile_for(cch), name=f"conv_bwd_in_{l}")[0]
    h1 = sv['h1']
    g_qkv = _mm(h1, dqkv, ta=True, outs=(_BF,), name=f"in_proj_qkv_dw_{l}")
    g_xbc = _mm(h1, dxbc, ta=True, outs=(_BF,), name=f"in_proj_xbc_dw_{l}")
    g_zdt = _mm(h1, dzdt, ta=True, outs=(_BF,), name=f"in_proj_zdt_dw_{l}")
    g['w_in'] = jnp.concatenate([g_qkv, g_zdt[:, :_AW], g_xbc, g_zdt[:, _AW:_AW + _HEADS]], axis=1)
    sent = send(('w_out', 'w_in'), g)
    for n in _BIG:
        del g[n]
    dh1 = _mm(dqkv, p['w_qkv'], tb=True, after=sent, name=f"in_proj_qkv_dx_{l}")
    dh1 = _mm(dxbc, p['w_xbc'], tb=True, extra=(dh1,), epi=_add_to, name=f"in_proj_xbc_dx_{l}")
    dh1 = _mm(dzdt, p['w_zdt'], tb=True, extra=(dh1,), epi=_add_to, name=f"in_proj_zdt_dx_{l}")
    dx, g['ln1_g'] = _rms_bwd(sv['x'], dh1, p['ln1_g'], dx2, name=f"ln1_bwd_{l}")
    return dx, g


def _loss_bwd(x, g, tgt):
    w = x.shape[1]
    tile = _tile_for(w)

    def fn(x, tgt, g):
        r = _rstd(x)
        xh = x * r
        e = xh * g - tgt
        gd = e * (g / w)
        dx = r * (gd - xh * jnp.mean(gd * xh, axis=-1, keepdims=True))
        rowloss = 0.5 * jnp.mean(e * e, axis=-1, keepdims=True)
        return dx, (e / w) * xh, jnp.broadcast_to(rowloss, (tile, _LANES))
    dx, dg, ls = _rows(fn, [x, tgt], [g.reshape(1, w)], [(w, F32)], [w, _LANES], tile=tile, name="loss_head")
    return dx, dg.sum(axis=0), ls[:, 0].sum()


def _adamw_math(w, g, m, v):
    m2 = _B1 * m + (1.0 - _B1) * g
    v2 = _B2 * v + (1.0 - _B2) * jnp.square(g)
    m_hat = m2 / (1.0 - _B1 ** _STEP)
    v_hat = v2 / (1.0 - _B2 ** _STEP)
    return -_LR * (m_hat / (jnp.sqrt(v_hat) + _AEPS) + _WD * w), m2, v2


def _adamw(w, g, m, v, *, name):
    width = w.shape[-1]
    flat = [a.reshape(-1, width) for a in (w, g, m, v)]
    tile = _pick(flat[0].shape[0], (_tile_for(width), 32, 8))
    res = _rows(_adamw_math, flat, [], [(width, F32)] * 3, tile=tile, name=name)
    return [r.reshape(w.shape) for r in res]


_HBM = pl.BlockSpec(memory_space=pltpu.HBM)


def _place():
    x, y, c = lax.axis_index("x"), lax.axis_index("y"), lax.axis_index("c")
    other_chips = [(1 - x, y), (x, 1 - y), (1 - x, 1 - y)]
    return x, y, c, other_chips


def _remote(src, dst, sems, i, dev):
    return pltpu.make_async_remote_copy(src_ref=src, dst_ref=dst, send_sem=sems[0].at[i], recv_sem=sems[1].at[i],
                                        device_id=dev, device_id_type=_MESH)


def _exchange8(v, *, reduce, after=None, name):
    r, w = v.shape
    behind = [] if after is None else [after]

    def body(v_ref, *rest):
        all_ref, rest = rest[len(behind)], rest[len(behind) + 1:]
        sems = rest[-2:]
        x, y, c, _ = _place()
        me = 4 * x + 2 * y + c
        all_ref[me] = v_ref[...]
        flips = [((d >> 2) & 1, (d >> 1) & 1, d & 1) for d in range(1, 8)]
        sends = [_remote(v_ref, all_ref.at[me], sems, i, (x ^ fx, y ^ fy, c ^ fc)) for i, (fx, fy, fc) in enumerate(flips)]
        for cp in sends:
            cp.start()
        for i, (fx, fy, fc) in enumerate(flips):
            _remote(v_ref, all_ref.at[me ^ (4 * fx + 2 * fy + fc)], sems, i, (x ^ fx, y ^ fy, c ^ fc)).wait_recv()
        for cp in sends:
            cp.wait_send()
        if reduce:
            acc = all_ref[0]
            for s in range(1, 8):
                acc = acc + all_ref[s]
            rest[0][...] = acc

    vm = pl.BlockSpec(memory_space=pltpu.VMEM)
    out_shape = [_SDS((8, r, w), v.dtype)] + ([_SDS((r, w), v.dtype)] if reduce else [])
    res = pl.pallas_call(body, in_specs=[vm] + [_ANY] * len(behind), out_specs=[vm] * len(out_shape), out_shape=out_shape, name=name,
                         scratch_shapes=[pltpu.SemaphoreType.DMA((7,)), pltpu.SemaphoreType.DMA((7,))],
                         compiler_params=pltpu.CompilerParams(vmem_limit_bytes=int(32 << 20)))(v, *behind)
    return res[1] if reduce else res[0]


_SEM = pl.BlockSpec(memory_space=pltpu.SEMAPHORE)
_ANY = pl.BlockSpec(memory_space=pl.ANY)
_EFFECT = pltpu.SideEffectType.DATAFLOW_SIDE_EFFECTING


def _in_hbm(a):
    return pltpu.with_memory_space_constraint(a, pltpu.HBM)


def _send_start(name, srcs, land_shapes, plan, n_sends, after):
    ns, nl = len(srcs), len(land_shapes)

    def body(*refs):
        ins, lands, sems = refs[:ns], refs[ns:ns + nl], refs[ns + nl + 1:ns + nl + 3]
        x, y, c, chips = _place()
        for i, (s, d, dev) in enumerate(plan(x, y, c, chips, ins, lands)[0]):
            _remote(s, d, sems, i, dev).start()
        refs[-1][...] = jnp.zeros_like(refs[-1])

    sem = pltpu.SemaphoreType.DMA((n_sends,))
    res = pl.pallas_call(
        body, name=name, in_specs=[_HBM] * (ns + nl) + [_ANY],
        out_shape=(sem, sem, *[pltpu.HBM(s.shape, s.dtype) for s in land_shapes], _SDS((8, _LANES), F32)),
        out_specs=(_SEM, _SEM, *[_HBM] * nl, pl.BlockSpec(memory_space=pltpu.VMEM)),
        input_output_aliases={ns + i: 2 + i for i in range(nl)},
        compiler_params=pltpu.CompilerParams(has_side_effects=_EFFECT))(
            *[_in_hbm(s) for s in srcs], *[_in_hbm(lax.empty(s.shape, s.dtype)) for s in land_shapes], after)
    return dict(sems=res[:2], srcs=srcs, lands=res[2:2 + nl], plan=plan), res[-1]


def _send_wait(name, h, after):
    ns, nl = len(h['srcs']), len(h['lands'])

    def body(*refs):
        ins, lands, sems = refs[:ns], refs[ns:ns + nl], refs[ns + nl:ns + nl + 2]
        x, y, c, chips = _place()
        sends, landings = h['plan'](x, y, c, chips, ins, lands)
        for i, (s, d, dev) in enumerate(sends):
            _remote(s, d, sems, i, dev).wait_send()
        for i, d in enumerate(landings):
            _remote(d, d, sems, i, sends[i][2]).wait_recv()

    return pl.pallas_call(
        body, name=name, in_specs=[_HBM] * (ns + nl) + [_SEM, _SEM, _ANY],
        out_shape=tuple(pltpu.HBM(a.shape, a.dtype) for a in h['lands']), out_specs=tuple([_HBM] * nl),
        input_output_aliases={ns + i: i for i in range(nl)},
        compiler_params=pltpu.CompilerParams(has_side_effects=_EFFECT))(
            *[_in_hbm(s) for s in h['srcs']], *h['lands'], *h['sems'], after)


def _gather_plan(items):
    def plan(x, y, c, chips, ins, lands):
        k = 2 * x + y
        sends = [(ins[si].at[l], lands[t].at[k], (px, py, c)) for t, (si, l) in enumerate(items) for px, py in chips]
        return sends, [lands[t].at[2 * px + py] for t in range(len(items)) for px, py in chips]
    return plan


_FLIPS = [((d >> 2) & 1, (d >> 1) & 1, d & 1) for d in range(1, 8)]


def _reduce_plan(halves):
    def plan(x, y, c, chips, ins, lands):
        sends, landings = [], []
        for t, hf in enumerate(halves):
            for i, (fx, fy, fc) in enumerate(_FLIPS):
                px, py, pc = x ^ fx, y ^ fy, c ^ fc
                sends.append((ins[t].at[2 * px + py, pl.ds(pc * hf, hf)], lands[t].at[i], (px, py, pc)))
                landings.append(lands[t].at[i])
        return sends, landings
    return plan


def _swap(name, srcs, out_shapes, plan, n_sends):
    n = len(srcs)

    def body(*refs):
        ins, outs, sems = refs[:n], refs[n:n + len(out_shapes)], refs[-2:]
        x, y, c, chips = _place()
        sends, landings = plan(x, y, c, chips, ins, outs)
        out = [_remote(s, d, sems, i, dev) for i, (s, d, dev) in enumerate(sends)]
        for cp in out:
            cp.start()
        for i, d in enumerate(landings):
            _remote(d, d, sems, i, sends[i][2]).wait_recv()
        for cp in out:
            cp.wait_send()

    return pl.pallas_call(
        body, in_specs=[_HBM] * n, out_specs=[_HBM] * len(out_shapes), out_shape=out_shapes, name=name,
        scratch_shapes=[pltpu.SemaphoreType.DMA((n_sends,)), pltpu.SemaphoreType.DMA((n_sends,))])(*srcs)


def _sum_owned(grads, landed, c, k, names):
    def sum8(*parts):
        acc = parts[0].astype(F32)
        for p in parts[1:]:
            acc = acc + p.astype(F32)
        return acc
    outs = []
    for g, got, name in zip(grads, landed, names):
        hf, b = got.shape[1:]
        own = lax.dynamic_slice_in_dim(lax.dynamic_index_in_dim(g, k, axis=0, keepdims=False), c * hf, hf, axis=0)
        outs.append(_rows(sum8, [own] + [got[i] for i in range(len(_FLIPS))], [], [(b, F32)], tile=_pick(hf, (_tile_for(b), 32)),
                          name=f"grad_sum_{name}")[0])
    return outs


def _share_halves(mine):
    n = len(mine)

    def plan(x, y, c_, chips, ins, outs):
        return [(ins[t], outs[t], (x, y, 1 - c_)) for t in range(n)], [outs[t] for t in range(n)]
    return _swap("grad_share_cores", mine, [_SDS(h.shape, F32) for h in mine], plan, n)


def _adamw_owned(w, mine, theirs, m, v, c, *, name):
    depth, a, b = w.shape
    half = a // 2
    tile = _pick(half, (_tile_for(b), 32, 8))
    nh = half // tile

    def blocks_of(l):
        return lambda i: (jnp.clip(i - 2 * nh * l, 0, 2 * nh - 1) % nh, 0)

    def fn(w, m, v, *rest):
        halves, cflag = rest[:-1], rest[-1]
        step = pl.program_id(0)
        is_mine = cflag[0:1, 0:1] == ((step // nh) % 2).astype(F32)
        g = jnp.where(is_mine, halves[0], halves[1])
        for l in range(1, depth):
            g = jnp.where(step >= 2 * nh * l, jnp.where(is_mine, halves[2 * l], halves[2 * l + 1]), g)
        return (g,) + _adamw_math(w, g, m, v)
    ins = [a_.reshape(depth * a, b) for a_ in (w, m, v)]
    ins += [(h, b, blocks_of(l)) for l in range(depth) for h in (mine[l], theirs[l])]
    res = _rows(fn, ins, [jnp.full((1, _LANES), c, F32)], [(b, F32)] * 4, tile=tile, name=name)
    return [r.reshape(w.shape) for r in res]


_BIG = ("w_in", "w_out", "w_mlp_in", "w_mlp_out")
_SMALL = ("ln1_g", "conv_b", "dt_bias", "a_log", "d_skip", "attn_norm_g", "ssd_norm_g", "ln2_g", "final_norm_g")
_ORDER = ("ln1_g", "w_in", "conv_w", "conv_b", "dt_bias", "a_log", "d_skip", "attn_norm_g", "ssd_norm_g", "w_out", "ln2_g",
          "w_mlp_in", "w_mlp_out", "final_norm_g")


def _pack(parts, rows):
    flat = jnp.concatenate([p.reshape(-1) for p in parts])
    return jnp.pad(flat, (0, rows * _LANES - flat.shape[0])).reshape(rows, _LANES)


def _unpack(buf, like):
    flat, out, o = buf.reshape(-1), [], 0
    for p in like:
        out.append(flat[o:o + p.size].reshape(p.shape))
        o += p.size
    return out


def kernel(x, ln1_g, w_in, conv_w, conv_b, dt_bias, a_log, d_skip, attn_norm_g, ssd_norm_g, w_out, ln2_g, w_mlp_in, w_mlp_out, final_norm_g, loss_target, m_ln1_g, m_w_in, m_conv_w, m_conv_b, m_dt_bias, m_a_log, m_d_skip, m_attn_norm_g, m_ssd_norm_g, m_w_out, m_ln2_g, m_w_mlp_in, m_w_mlp_out, m_final_norm_g, v_ln1_g, v_w_in, v_conv_w, v_conv_b, v_dt_bias, v_a_log, v_d_skip, v_attn_norm_g, v_ssd_norm_g, v_w_out, v_ln2_g, v_w_mlp_in, v_w_mlp_out, v_final_norm_g):
    w = dict(ln1_g=ln1_g, w_in=w_in, conv_w=conv_w, conv_b=conv_b, dt_bias=dt_bias, a_log=a_log, d_skip=d_skip,
             attn_norm_g=attn_norm_g, ssd_norm_g=ssd_norm_g, w_out=w_out, ln2_g=ln2_g, w_mlp_in=w_mlp_in, w_mlp_out=w_mlp_out,
             final_norm_g=final_norm_g)
    m = dict(ln1_g=m_ln1_g, w_in=m_w_in, conv_w=m_conv_w, conv_b=m_conv_b, dt_bias=m_dt_bias, a_log=m_a_log, d_skip=m_d_skip,
             attn_norm_g=m_attn_norm_g, ssd_norm_g=m_ssd_norm_g, w_out=m_w_out, ln2_g=m_ln2_g, w_mlp_in=m_w_mlp_in,
             w_mlp_out=m_w_mlp_out, final_norm_g=m_final_norm_g)
    v = dict(ln1_g=v_ln1_g, w_in=v_w_in, conv_w=v_conv_w, conv_b=v_conv_b, dt_bias=v_dt_bias, a_log=v_a_log, d_skip=v_d_skip,
             attn_norm_g=v_attn_norm_g, ssd_norm_g=v_ssd_norm_g, w_out=v_w_out, ln2_g=v_ln2_g, w_mlp_in=v_w_mlp_in,
             w_mlp_out=v_w_mlp_out, final_norm_g=v_final_norm_g)
    depth, d_model = ln1_g.shape
    n_chips = 4
    c = lax.axis_index("c")
    chip = 2 * lax.axis_index("x") + lax.axis_index("y")
    in_proj = w_in.shape[2] * n_chips
    cch = conv_w.shape[2] * n_chips
    zdt_pad = _LANES - _HEADS

    cw = _exchange8(conv_w.reshape(depth * _CONV_K, -1), reduce=False, name="gather_conv_w")[0::2]
    conv_full = cw.reshape(n_chips, depth, _CONV_K, -1).transpose(1, 2, 0, 3).reshape(depth, _CONV_K, cch)
    own = [w[n].astype(_BF) for n in _BIG]
    is_own = (jnp.arange(n_chips) == chip).reshape(n_chips, 1, 1)

    def start_gather(tag, items, after):
        lands = [_SDS((n_chips, *own[i].shape[1:]), _BF) for i, _ in items]
        return _send_start(f"gather_start_{tag}", own, lands, _gather_plan(items), 3 * len(items), after)

    def finish_gather(tag, handle, items, after):
        landed = _send_wait(f"gather_wait_{tag}", handle, after)
        return {_BIG[i]: jnp.where(is_own, own[i][l][None], g) for (i, l), g in zip(items, landed)}

    def layer_weights(l, blocks):
        p = {}
        if 'w_in' in blocks:
            full_in = blocks['w_in'].transpose(1, 0, 2).reshape(d_model, in_proj)
            p['w_qkv'] = full_in[:, :3 * _AW]
            p['w_xbc'] = full_in[:, 4 * _AW:4 * _AW + cch]
            p['w_zdt'] = jnp.concatenate([full_in[:, 3 * _AW:4 * _AW], full_in[:, 4 * _AW + cch:], jnp.zeros((d_model, zdt_pad), _BF)], axis=1)
        if 'w_out' in blocks:
            p['w_out'] = blocks['w_out'].reshape(-1, d_model)
            p['w_mlp_in'] = blocks['w_mlp_in']
            p['w_mlp_out'] = blocks['w_mlp_out'].reshape(-1, d_model)
        return p

    groups = dict(a=[(0, 0)], b=[(1, 0), (2, 0), (3, 0)], c=[(0, 1)], d=[(1, 1), (2, 1), (3, 1)])
    handles, token = {}, conv_full
    for tag, items in groups.items():
        handles[tag], token = start_gather(tag, items, token)
    layers = [{n: w[n][l] for n in _SMALL[:-1]} for l in range(depth)]
    for l in range(depth):
        layers[l]['conv_w'] = conv_full[l]

    layers[0].update(layer_weights(0, finish_gather("a", handles["a"], groups["a"], token)))
    mix, sv0 = _layer_fwd(x[0], layers[0], 0)
    layers[0].update(layer_weights(0, finish_gather("b", handles["b"], groups["b"], mix)))
    h = _layer_fwd_mlp(layers[0], sv0, 0)
    layers[1].update(layer_weights(1, finish_gather("c", handles["c"], groups["c"], h)))
    mix, sv1 = _layer_fwd(h, layers[1], 1)
    layers[1].update(layer_weights(1, finish_gather("d", handles["d"], groups["d"], mix)))
    h = _layer_fwd_mlp(layers[1], sv1, 1)
    saved = [sv0, sv1]

    def by_chip(g, name):
        if name == "w_mlp_in":
            return g
        if name == "w_in":
            return g.reshape(d_model, n_chips, -1).transpose(1, 0, 2)
        return g.reshape(n_chips, -1, d_model)

    pending = []

    def sender(l):
        def send(names, g):
            srcs = [by_chip(g[n], n) for n in names]
            halves = [s.shape[1] // 2 for s in srcs]
            lands = [_SDS((len(_FLIPS), hf, s.shape[2]), _BF) for s, hf in zip(srcs, halves)]
            handle, tok = _send_start(f"grad_start_{names[-1]}_{l}", srcs, lands, _reduce_plan(halves), len(_FLIPS) * len(srcs), srcs[0])
            pending.append((l, names, srcs, handle))
            return tok
        return send

    dx, g_final, loss_part = _loss_bwd(h, final_norm_g, loss_target[0])
    grads, after = [None] * depth, None
    for l in reversed(range(depth)):
        dx, grads[l] = _layer_bwd(dx, layers[l], saved[l], l, sender(l), after)
        after = dx
    owned = {}
    for l, names, srcs, handle in pending:
        landed = _send_wait(f"grad_wait_{names[-1]}_{l}", handle, dx)
        owned.update(zip([(n, l) for n in names], _sum_owned(srcs, landed, c, chip, [f"{n}_{l}" for n in names])))
    keys = [(n, l) for n in _BIG for l in range(depth)]
    theirs = dict(zip(keys, _share_halves([owned[k] for k in keys])))
    red, delta, new_m, new_v = {}, {}, {}, {}
    for n in _BIG:
        red[n], delta[n], new_m[n], new_v[n] = _adamw_owned(
            w[n], [owned[(n, l)] for l in range(depth)], [theirs[(n, l)] for l in range(depth)], m[n], v[n], c, name=f"adamw_{n}")

    small = {n: jnp.stack([grads[l][n] for l in range(depth)]) for n in _SMALL[:-1] + ("conv_w",)}
    small["final_norm_g"] = g_final
    parts = [loss_part.reshape(1)] + [small[n] for n in _SMALL + ("conv_w",)]
    rows = -(-sum(p.size for p in parts) // 1024) * 8
    tot = _unpack(_exchange8(_pack(parts, rows), reduce=True, after=red[_BIG[0]], name="allreduce_small"), parts)
    loss = tot[0][0]
    red.update(zip(_SMALL + ("conv_w",), tot[1:]))
    red["conv_w"] = lax.dynamic_index_in_dim(red["conv_w"].reshape(depth, _CONV_K, n_chips, -1), chip, axis=2, keepdims=False)

    names = _SMALL + ("conv_w",)
    like = [w[n] for n in names]
    srows = -(-sum(p.size for p in like) // 1024) * 8
    res = _adamw(*[_pack([d[n] for n in names], srows) for d in (w, red, m, v)], name="adamw_small")
    for dst, buf in zip((delta, new_m, new_v), res):
        dst.update(zip(names, _unpack(buf, like)))
    return (loss, dx[None], *[red[n] for n in _ORDER], *[delta[n] for n in _ORDER], *[new_m[n] for n in _ORDER],
            *[new_v[n] for n in _ORDER])
```

```python
import numpy as np
import jax
import jax.numpy as jnp
from jax import lax
from jax.experimental import pallas as pl
from jax.experimental.pallas import tpu as pltpu

F32 = jnp.float32
_BF = jnp.bfloat16
_NEG = -1e30
_EPS = 1e-5
_HEADS = 16
_HDIM = 64
_AW = _HEADS * _HDIM
_ABLK = 128
_DILATIONS = (1, 4, 16)
_CHUNK = 128
_NSTATE = 128
_GROUPS = 2
_HPG = _HEADS // _GROUPS
_CONV_K = 4
_LANES = 128
_CHIPS = 4
_LR, _B1, _B2, _AEPS, _WD, _STEP = 0.001, 0.9, 0.999, 1e-08, 0.01, 10
_VMEM_CAP = 56 * 1024 * 1024
_MESH = pl.DeviceIdType.MESH
_SDS = jax.ShapeDtypeStruct
_NT = (((1,), (1,)), ((), ()))
_TN = (((0,), (0,)), ((), ()))


def _params(sem, est_bytes):
    lim = int(min(max(2 * est_bytes + (4 << 20), 16 << 20), _VMEM_CAP))
    return pltpu.CompilerParams(dimension_semantics=sem, vmem_limit_bytes=lim)


def _nbytes(shape, dtype):
    return int(np.prod(shape)) * jnp.dtype(dtype).itemsize


def _hbm(a):
    return pltpu.with_memory_space_constraint(a, pltpu.HBM)


def _dot(a, b, dims=(((1,), (0,)), ((), ()))):
    return lax.dot_general(a.astype(_BF), b.astype(_BF), dims, preferred_element_type=F32)


_HALO = 8


def _rows(fn, ins, consts, outs, sums=(), *, halos=(), tile, name):
    rows = (ins[0][0] if isinstance(ins[0], tuple) else ins[0]).shape[0]
    n_steps = rows // tile

    def norm_in(a):
        if not isinstance(a, tuple):
            return a, tile, a.shape[1], lambda i: (i, 0)
        if isinstance(a[0], str):
            return a[1], tile // a[2], a[1].shape[1], lambda i: (i, 0)
        return a[0], tile, a[1], a[2] if callable(a[2]) else (lambda i, j=a[2]: (i, j))
    ins = [norm_in(a) for a in ins]
    outs = [(w, dt, d[0] if d else 1) for w, dt, *d in outs]
    n_in, n_h, n_c, n_o, n_s = len(ins), len(halos), len(consts), len(outs), len(sums)

    def body(*refs):
        step = pl.program_id(0)
        vals = [r[...] for r in refs[:n_in]]
        for r, (_, side) in zip(refs[n_in:n_in + n_h], halos):
            vals.append(jnp.where(step == (0 if side < 0 else n_steps - 1), 0.0, r[...]))
        vals += [r[...] for r in refs[n_in + n_h:n_in + n_h + n_c]]
        refs = refs[:n_in] + refs[n_in + n_h:]
        res = fn(*vals)
        res = res if isinstance(res, tuple) else (res,)
        orefs = refs[n_in + n_c:n_in + n_c + n_o]
        srefs = refs[n_in + n_c + n_o:]
        for r, v in zip(orefs, res[:n_o]):
            r[...] = v.astype(r.dtype)
        if n_s:
            @pl.when(pl.program_id(0) == 0)
            def _():
                for r in srefs:
                    r[...] = jnp.zeros_like(r)
            for r, v in zip(srefs, res[n_o:]):
                r[...] += v.reshape(tile // 8, 8, v.shape[-1]).sum(axis=0)

    per = tile // _HALO
    in_specs = [pl.BlockSpec((r, w), idx) for _, r, w, idx in ins]
    in_specs += [pl.BlockSpec((_HALO, a.shape[1]), (lambda i: (jnp.maximum(i * per - 1, 0), 0)) if side < 0
                              else (lambda i: (jnp.minimum((i + 1) * per, rows // _HALO - 1), 0))) for a, side in halos]
    in_specs += [pl.BlockSpec(c.shape, lambda i, nd=c.ndim: (0,) * nd) for c in consts]
    out_shape = [_SDS((rows // d, d * w), dt) for w, dt, d in outs] + [_SDS((8, w), F32) for w in sums]
    out_specs = [pl.BlockSpec((tile // d, d * w), lambda i: (i, 0)) for w, _, d in outs]
    out_specs += [pl.BlockSpec((8, w), lambda i: (0, 0)) for w in sums]
    est = sum(_nbytes((r, w), a.dtype) for a, r, w, _ in ins) + sum(_nbytes((tile, w), dt) for w, dt, _ in outs)
    return pl.pallas_call(body, grid=(n_steps,), in_specs=in_specs, out_specs=out_specs, out_shape=out_shape, name=name,
                          compiler_params=_params(("arbitrary",), 3 * est))(
                              *[_hbm(a[0]) for a in ins], *[_hbm(a) for a, _ in halos], *consts)


def _perm(d, tile):
    p = np.zeros((tile, tile), np.float32)
    t = np.arange(tile)
    p[t, (t % d) * (tile // d) + t // d] = 1.0
    return jnp.asarray(p, _BF)


def _unstride(s, p):
    d = p.shape[0] // s.shape[0]
    w = s.shape[1] // d
    return _dot(p, jnp.concatenate([s[:, r * w:(r + 1) * w] for r in range(d)], axis=0))


def _stride(x, p, d):
    z = _dot(p, x, _TN)
    n = x.shape[0] // d
    return jnp.concatenate([z[r * n:(r + 1) * n] for r in range(d)], axis=1)


def _shifted(u, halo, back):
    n = u.shape[0] + _HALO
    if back:
        ext = jnp.concatenate([halo, u], axis=0)
        return [pltpu.roll(ext, j, 0)[_HALO:] for j in (1, 2, 3)]
    ext = jnp.concatenate([u, halo], axis=0)
    return [pltpu.roll(ext, n - j, 0)[:u.shape[0]] for j in (1, 2, 3)]


def _tile_for(width):
    return max(c for c in (256, 128, 64, 32) if c * width <= (1 << 18) or c == 32)


def _rstd(x):
    return lax.rsqrt(jnp.mean(x * x, axis=-1, keepdims=True) + _EPS)


def _split(x, groups):
    w = x.shape[-1] // groups
    return [x[:, g * w:(g + 1) * w] for g in range(groups)]


def _cat(parts):
    return parts[0] if len(parts) == 1 else jnp.concatenate(parts, axis=-1)


def _rms_bwd_tile(x, dy, g, groups):
    dxs, dgs = [], []
    for xs, ds, gs in zip(_split(x, groups), _split(dy.astype(F32), groups), _split(g, groups)):
        r = _rstd(xs)
        xh = xs * r
        gd = ds * gs
        dxs.append(r * (gd - xh * jnp.mean(gd * xh, axis=-1, keepdims=True)))
        dgs.append(ds * xh)
    return _cat(dxs), _cat(dgs)


def _rms_fwd(x, g, *, groups=1, name):
    def fn(x, g):
        return _cat([xs * _rstd(xs) * gs for xs, gs in zip(_split(x, groups), _split(g, groups))])
    w = x.shape[1]
    return _rows(fn, [x], [g.reshape(1, w)], [(w, _BF)], tile=_tile_for(w), name=name)[0]


def _rms_bwd(x, dy, g, res=None, *, name):
    def fn(x, dy, *rest):
        dx, dg = _rms_bwd_tile(x, dy, rest[-1], 1)
        return (dx + rest[0] if res is not None else dx), dg
    w = x.shape[1]
    ins = [x, dy] + ([res] if res is not None else [])
    dx, dg = _rows(fn, ins, [g.reshape(1, w)], [(w, F32)], [w], tile=_tile_for(w), name=name)
    return dx, dg.sum(axis=0)


def _pick(n, cands):
    for c in cands:
        if n % c == 0:
            return c
    raise ValueError(f"no block size for {n}")


_MM_BLOCKS = (1024, 640, 512, 384)


def _mm(a, b, *, ta=False, tb=False, extra=(), epi=None, outs=(F32,), after=None, b_chips=0, out_chips=0, name):
    m, k = (a.shape[1], a.shape[0]) if ta else a.shape
    b_shape = (b.shape[1], b.shape[2] * b_chips) if b_chips else b.shape
    n = b_shape[0] if tb else b_shape[1]
    assert k == (b_shape[1] if tb else b_shape[0])
    n_cap = n // max(out_chips, 1 if tb else b_chips, 1)
    k_cap = k // (b_chips if (b_chips and tb) else 1)
    bm, bn = _pick(m, _MM_BLOCKS), _pick(n_cap, _MM_BLOCKS)
    bk = _pick(k_cap, (2048,) + _MM_BLOCKS)
    nk = k // bk
    n_e, n_o = len(extra), len(outs)
    behind = [] if after is None else [after]
    dims = (((0 if ta else 1,), (1 if tb else 0,)), ((), ()))

    def body(a_ref, b_ref, *rest):
        ex, orefs, acc = rest[:n_e], rest[n_e + len(behind):n_e + len(behind) + n_o], rest[-1]
        kk = pl.program_id(2)

        @pl.when(kk == 0)
        def _():
            acc[...] = jnp.zeros_like(acc)

        acc[...] += _dot(a_ref[...], b_ref[...], dims)

        @pl.when(kk == nk - 1)
        def _():
            r = acc[...]
            res = epi(r, *[e[...] for e in ex]) if epi is not None else (r,)
            for o, v in zip(orefs, res):
                o[...] = v.astype(o.dtype)

    a_spec = pl.BlockSpec((bk, bm), lambda i, j, kk: (kk, i)) if ta else pl.BlockSpec((bm, bk), lambda i, j, kk: (i, kk))
    if b_chips and tb:
        per = k_cap // bk
        b_spec = pl.BlockSpec((None, bn, bk), lambda i, j, kk: (kk // per, j, kk % per))
    elif b_chips:
        per = n_cap // bn
        b_spec = pl.BlockSpec((None, bk, bn), lambda i, j, kk: (j // per, kk, j % per))
    else:
        b_spec = pl.BlockSpec((bn, bk), lambda i, j, kk: (j, kk)) if tb else pl.BlockSpec((bk, bn), lambda i, j, kk: (kk, j))
    t_spec = pl.BlockSpec((bm, bn), lambda i, j, kk: (i, j))
    o_spec, o_shape = t_spec, (m, n)
    if out_chips:
        per_o = n_cap // bn
        o_spec, o_shape = pl.BlockSpec((None, bm, bn), lambda i, j, kk: (j // per_o, i, j % per_o)), (out_chips, m, n_cap)
    est = (_nbytes((bm, bk), a.dtype) + _nbytes((bk, bn), b.dtype) + sum(_nbytes((bm, bn), e.dtype) for e in extra)
           + sum(_nbytes((bm, bn), o) for o in outs)) * 2 + 2 * _nbytes((bm, bn), F32)
    res = pl.pallas_call(
        body, grid=(m // bm, n // bn, nk), in_specs=[a_spec, b_spec] + [t_spec] * n_e + [pl.BlockSpec(memory_space=pl.ANY)] * len(behind),
        out_specs=[o_spec] * n_o, out_shape=[_SDS(o_shape, o) for o in outs], scratch_shapes=[pltpu.VMEM((bm, bn), F32)], name=name,
        compiler_params=_params(("parallel", "parallel", "arbitrary"), est))(_hbm(a), _hbm(b), *[_hbm(e) for e in extra], *behind)
    return res[0] if n_o == 1 else res


def _add_to(acc, r):
    return (acc + r,)


def _alibi_bias(dilation):
    slopes = 2.0 ** (-8.0 * (np.arange(_HEADS) + 1) / _HEADS)
    i = np.arange(_ABLK)[:, None]
    j = np.arange(_ABLK)[None, :]
    cur = np.where(i - j >= 0, -slopes[:, None, None] * ((i - j) * dilation), _NEG)
    prev = np.where(j >= i, -slopes[:, None, None] * ((i - j + _ABLK) * dilation), _NEG)
    both = np.concatenate([prev, cur], axis=2)
    return jnp.asarray(both.reshape(_HEADS // 2, 2 * _ABLK, 2 * _ABLK), F32)


def _strided(a, d):
    return a.reshape(a.shape[0] // d, d * a.shape[1])


def _head(h):
    return slice(h * _HDIM, (h + 1) * _HDIM)


def _pair(pr):
    return slice(pr * _LANES, (pr + 1) * _LANES)


def _low_lanes(shape):
    return lax.broadcasted_iota(jnp.int32, shape, 1) < _HDIM


def _halves(v, low):
    z = jnp.zeros_like(v)
    return jnp.where(low, v, z), jnp.where(low, z, v)


def _no_prev_mask(first):
    return jnp.logical_and(first, lax.broadcasted_iota(jnp.int32, (2 * _ABLK, 2 * _ABLK), 1) < _ABLK)


def _lane_spec(nb):
    return pl.BlockSpec((_ABLK, _LANES), lambda r, j: (jnp.minimum(j, nb - 1), r))


def _expand_heads(v):
    low = _low_lanes((v.shape[0], _LANES))
    return jnp.concatenate([jnp.where(low, v[:, 2 * pr:2 * pr + 1], v[:, 2 * pr + 1:2 * pr + 2]) for pr in range(_HEADS // 2)], axis=1)


def _attn_specs(nb, n_parts):
    def cur(p):
        return pl.BlockSpec((_ABLK, _AW), lambda r, j: (jnp.minimum(j, nb - 1), r * n_parts + p))

    def prev(p):
        return pl.BlockSpec((_ABLK, _AW), lambda r, j: (jnp.clip(j - 1, 0, nb - 1), r * n_parts + p))
    return cur, prev


def _attn_fwd(qkv, dilation, *, name):
    t = qkv.shape[0]
    nb = t // dilation // _ABLK
    bias = _alibi_bias(dilation)
    scale = _HDIM ** -0.5

    def body(q_ref, kc_ref, kp_ref, vc_ref, vp_ref, b_ref, o_ref, l_ref):
        no_prev = _no_prev_mask(pl.program_id(1) == 0)
        low = _low_lanes((_ABLK, _LANES))
        l_ref[...] = jnp.zeros_like(l_ref)
        for pr in range(_HEADS // 2):
            sl = _pair(pr)
            k2 = jnp.concatenate([kp_ref[:, sl], kc_ref[:, sl]], axis=0)
            v2 = jnp.concatenate([vp_ref[:, sl], vc_ref[:, sl]], axis=0)
            q2 = jnp.concatenate(_halves(q_ref[:, sl], low), axis=0)
            s = jnp.where(no_prev, _NEG, _dot(q2, k2, _NT) * scale + b_ref[pr])
            m = jnp.max(s, axis=-1, keepdims=True)
            p = jnp.exp(s - m)
            den = jnp.sum(p, axis=-1, keepdims=True)
            o = _dot(p, v2) / den
            lse = m + jnp.log(den)
            l_ref[:, 2 * pr:2 * pr + 1] = lse[:_ABLK]
            l_ref[:, 2 * pr + 1:2 * pr + 2] = lse[_ABLK:]
            o_ref[:, sl] = jnp.where(low, o[:_ABLK], o[_ABLK:]).astype(o_ref.dtype)

    cur, prev = _attn_specs(nb, 3)
    cur1, _ = _attn_specs(nb, 1)
    bspec = pl.BlockSpec((_HEADS // 2, 2 * _ABLK, 2 * _ABLK), lambda r, j: (0, 0, 0))
    sv = _strided(qkv, dilation)
    o, l = pl.pallas_call(
        body, grid=(dilation, nb), in_specs=[cur(0), cur(1), prev(1), cur(2), prev(2), bspec],
        out_specs=[cur1(0), _lane_spec(nb)],
        out_shape=[_SDS((t // dilation, dilation * _AW), _BF), _SDS((t // dilation, dilation * _LANES), F32)], name=name,
        compiler_params=_params(("parallel", "arbitrary"), 16 << 20))(sv, sv, sv, sv, sv, bias)
    return o, l.reshape(t, _LANES)


def _attn_bwd(qkv, do, ld, dilation, *, name):
    t = qkv.shape[0]
    nb = t // dilation // _ABLK
    bias = _alibi_bias(dilation)
    scale = _HDIM ** -0.5

    def body(q_ref, kc_ref, kp_ref, vc_ref, vp_ref, do_ref, ld_ref, b_ref, dq_ref, dk_ref, dv_ref, ck, cv):
        n = pl.program_id(1)

        @pl.when(n == 0)
        def _():
            ck[...] = jnp.zeros_like(ck)
            cv[...] = jnp.zeros_like(cv)

        @pl.when(n < nb)
        def _():
            low = _low_lanes((_ABLK, _LANES))
            no_prev = _no_prev_mask(n == 0)
            for pr in range(_HEADS // 2):
                sl = _pair(pr)
                k2 = jnp.concatenate([kp_ref[:, sl], kc_ref[:, sl]], axis=0)
                v2 = jnp.concatenate([vp_ref[:, sl], vc_ref[:, sl]], axis=0)
                q2 = jnp.concatenate(_halves(q_ref[:, sl], low), axis=0)
                do2 = jnp.concatenate(_halves(do_ref[:, sl], low), axis=0)
                lrow = jnp.concatenate([ld_ref[:, 2 * pr:2 * pr + 1], ld_ref[:, 2 * pr + 1:2 * pr + 2]], axis=0)
                dsum = jnp.concatenate([ld_ref[:, _HEADS + 2 * pr:_HEADS + 2 * pr + 1],
                                        ld_ref[:, _HEADS + 2 * pr + 1:_HEADS + 2 * pr + 2]], axis=0)
                p = jnp.exp(jnp.where(no_prev, _NEG, _dot(q2, k2, _NT) * scale + b_ref[pr]) - lrow)
                ds = (p * (_dot(do2, v2, _NT) - dsum)).astype(_BF)
                dq = _dot(ds, k2)
                dk2, dv2 = _dot(ds, q2, _TN), _dot(p, do2, _TN)
                dq_ref[:, sl] = (jnp.where(low, dq[:_ABLK], dq[_ABLK:]) * scale).astype(dq_ref.dtype)
                dk_ref[:, sl] = (ck[:, sl] + dk2[:_ABLK] * scale).astype(dk_ref.dtype)
                dv_ref[:, sl] = (cv[:, sl] + dv2[:_ABLK]).astype(dv_ref.dtype)
                ck[:, sl] = dk2[_ABLK:] * scale
                cv[:, sl] = dv2[_ABLK:]

        @pl.when(n == nb)
        def _():
            dk_ref[...] = ck[...].astype(dk_ref.dtype)
            dv_ref[...] = cv[...].astype(dv_ref.dtype)

    cur, prev = _attn_specs(nb, 3)
    cur1, prev1 = _attn_specs(nb, 1)
    bspec = pl.BlockSpec((_HEADS // 2, 2 * _ABLK, 2 * _ABLK), lambda r, j: (0, 0, 0))
    sv, dov, ldv = _strided(qkv, dilation), do, _strided(ld, dilation)
    dqkv = pl.pallas_call(
        body, grid=(dilation, nb + 1),
        in_specs=[cur(0), cur(1), prev(1), cur(2), prev(2), cur1(0), _lane_spec(nb), bspec],
        out_specs=[cur1(0), prev1(0), prev1(0)], out_shape=[_SDS(dov.shape, _BF)] * 3, name=name,
        scratch_shapes=[pltpu.VMEM((_ABLK, _AW), F32)] * 2,
        compiler_params=_params(("parallel", "arbitrary"), 16 << 20))(sv, sv, sv, sv, sv, dov, ldv, bias)
    return dqkv


def _ssd_in_specs(ch):
    return dict(
        xs=pl.BlockSpec((_CHUNK, _AW), lambda c: (ch(c), 0)),
        bc=pl.BlockSpec((_CHUNK, 2 * _GROUPS * _NSTATE), lambda c: (ch(c), _AW // (2 * _GROUPS * _NSTATE))),
        lane=pl.BlockSpec((_CHUNK, _LANES), lambda c: (ch(c), 0)),
        arow=pl.BlockSpec((_HEADS, 1, _CHUNK), lambda c: (0, 0, ch(c))),
        st=pl.BlockSpec((1, _HEADS // 2, _NSTATE, _LANES), lambda c: (ch(c), 0, 0, 0)),
    )


def _decay(a_col, a_row):
    i0 = lax.broadcasted_iota(jnp.int32, (_CHUNK, _CHUNK), 0)
    i1 = lax.broadcasted_iota(jnp.int32, (_CHUNK, _CHUNK), 1)
    return jnp.where(i0 >= i1, jnp.exp(a_col - a_row), 0.0), jnp.where(i1 >= i0, jnp.exp(a_row - a_col), 0.0)


def _rsum(v):
    return jnp.sum(v, axis=-1, keepdims=True)


def _ssd_fwd(act, dt, acum, a_row, *, name):
    t = act.shape[0]
    nc = t // _CHUNK
    sp = _ssd_in_specs(lambda c: c)
    gw = _GROUPS * _NSTATE

    def body(xs_ref, bc_ref, dt_ref, ac_ref, ar_ref, y_ref, sall_ref, st):
        @pl.when(pl.program_id(0) == 0)
        def _():
            st[...] = jnp.zeros_like(st)

        low = _low_lanes((_CHUNK, _LANES))
        for g in range(_GROUPS):
            bg = bc_ref[:, g * _NSTATE:(g + 1) * _NSTATE]
            cg = bc_ref[:, gw + g * _NSTATE:gw + (g + 1) * _NSTATE].astype(_BF)
            cb = _dot(cg, bg, _NT)
            for pr in range(g * _HPG // 2, (g + 1) * _HPG // 2):
                ha, hb = 2 * pr, 2 * pr + 1
                a_a, a_b = ac_ref[:, ha:ha + 1], ac_ref[:, hb:hb + 1]
                x = (xs_ref[:, _pair(pr)] * jnp.where(low, dt_ref[:, ha:ha + 1], dt_ref[:, hb:hb + 1])).astype(_BF)
                lm_a, _ = _decay(a_a, ar_ref[ha])
                lm_b, _ = _decay(a_b, ar_ref[hb])
                sv = st[pr]
                sall_ref[0, pr] = sv
                yd = _dot(jnp.concatenate([cb * lm_a, cb * lm_b], axis=0), x)
                yd = jnp.where(low, yd[:_CHUNK], yd[_CHUNK:])
                y_ref[:, _pair(pr)] = yd + jnp.where(low, jnp.exp(a_a), jnp.exp(a_b)) * _dot(cg, sv)
                al_a, al_b = jnp.min(a_a, axis=0, keepdims=True), jnp.min(a_b, axis=0, keepdims=True)
                upd = _dot(jnp.concatenate([bg * jnp.exp(al_a - a_a), bg * jnp.exp(al_b - a_b)], axis=1), x, _TN)
                st[pr] = jnp.where(low, jnp.exp(al_a), jnp.exp(al_b)) * sv + jnp.where(low, upd[:_NSTATE], upd[_NSTATE:])

    return pl.pallas_call(
        body, grid=(nc,), in_specs=[sp['xs'], sp['bc'], sp['lane'], sp['lane'], sp['arow']],
        out_specs=[sp['xs'], sp['st']], out_shape=[_SDS((t, _AW), F32), _SDS((nc, _HEADS // 2, _NSTATE, _LANES), F32)],
        scratch_shapes=[pltpu.VMEM((_HEADS // 2, _NSTATE, _LANES), F32)], name=name,
        compiler_params=_params(("arbitrary",), 16 << 20))(act, act, dt, acum, a_row)


def _ssd_bwd(act, dt, acum, a_row, sall, dy, *, name):
    t = act.shape[0]
    nc = t // _CHUNK
    sp = _ssd_in_specs(lambda c: nc - 1 - c)
    gw = _GROUPS * _NSTATE

    def body(xs_ref, bc_ref, dt_ref, ac_ref, ar_ref, sall_ref, dy_ref, dxs_ref, dbc_ref, ddt_ref, da_ref, dst):
        @pl.when(pl.program_id(0) == 0)
        def _():
            dst[...] = jnp.zeros_like(dst)

        ddt_ref[...] = jnp.zeros_like(ddt_ref)
        da_ref[...] = jnp.zeros_like(da_ref)
        row = lax.broadcasted_iota(jnp.int32, (_CHUNK, 1), 0)
        low = _low_lanes((_CHUNK, _LANES))
        for g in range(_GROUPS):
            bg = bc_ref[:, g * _NSTATE:(g + 1) * _NSTATE]
            bgb = bg.astype(_BF)
            cg = bc_ref[:, gw + g * _NSTATE:gw + (g + 1) * _NSTATE].astype(_BF)
            cb, cbt = _dot(cg, bgb, _NT), _dot(bgb, cg, _NT)
            dcb = jnp.zeros((_CHUNK, _CHUNK), F32)
            dbg = jnp.zeros((_CHUNK, _NSTATE), F32)
            dcg = jnp.zeros((_CHUNK, _NSTATE), F32)
            for pr in range(g * _HPG // 2, (g + 1) * _HPG // 2):
                heads = (2 * pr, 2 * pr + 1)
                a_cols = [ac_ref[:, h:h + 1] for h in heads]
                dt_pair = jnp.where(low, dt_ref[:, heads[0]:heads[0] + 1], dt_ref[:, heads[1]:heads[1] + 1])
                xsv = xs_ref[:, _pair(pr)]
                x = xsv * dt_pair
                xb = x.astype(_BF)
                xhs = _halves(xb, low)
                dyv = dy_ref[:, _pair(pr)]
                dyb = dyv.astype(_BF)
                dyhs = _halves(dyb, low)
                sv, dsv = sall_ref[0, pr], dst[pr]
                svb, dsb = sv.astype(_BF), dsv.astype(_BF)
                a_lasts = [jnp.min(a, axis=0, keepdims=True) for a in a_cols]
                e_pair = jnp.where(low, jnp.exp(a_cols[0]), jnp.exp(a_cols[1]))
                el_pair = jnp.where(low, jnp.exp(a_lasts[0]), jnp.exp(a_lasts[1]))
                yo = e_pair * _dot(cg, svb)
                decays = [_decay(a_cols[i], ar_ref[h]) for i, h in enumerate(heads)]
                gms, gmts = [cb * lm for lm, _ in decays], [cbt * lmt for _, lmt in decays]
                w_cols = [jnp.exp(a_lasts[i] - a_cols[i]) for i in range(2)]
                x2, dy2 = jnp.concatenate(xhs, axis=0), jnp.concatenate(dyhs, axis=0)
                bwd = _dot(jnp.concatenate([bg * w_cols[0], bg * w_cols[1]], axis=0), dsb)
                dxg = _dot(jnp.concatenate(gms, axis=1), dyb, _TN)
                dg2, dgt2, xds2 = _dot(dy2, xb, _NT), _dot(x2, dyb, _NT), _dot(x2, dsb, _NT)
                das = []
                for i in range(2):
                    rows_i = slice(i * _CHUNK, (i + 1) * _CHUNK)
                    dcb = dcb + dg2[rows_i] * decays[i][0]
                    dbg = dbg + w_cols[i] * xds2[rows_i]
                    das.append(_rsum(dg2[rows_i] * gms[i]) - _rsum(dgt2[rows_i] * gmts[i]))
                bwd = jnp.where(low, bwd[:_CHUNK], bwd[_CHUNK:])
                dx = jnp.where(low, dxg[:_CHUNK], dxg[_CHUNK:]) + bwd
                edy = (e_pair * dyv).astype(_BF)
                dcg = dcg + _dot(edy, svb, _NT)
                zs, yos, sds, dts = (_halves(v, low) for v in (x * bwd, dyv * yo, sv * dsv, dx * xsv))
                for i, h in enumerate(heads):
                    z = _rsum(zs[i])
                    da_last = jnp.sum(z, axis=0, keepdims=True) + jnp.exp(a_lasts[i]) * jnp.sum(_rsum(sds[i]), axis=0, keepdims=True)
                    da_ref[:, h:h + 1] = das[i] + _rsum(yos[i]) - z + jnp.where(row == _CHUNK - 1, da_last, 0.0)
                    ddt_ref[:, h:h + 1] = _rsum(dts[i])
                dxs_ref[:, _pair(pr)] = dx * dt_pair
                dst[pr] = el_pair * dsv + _dot(cg, edy, _TN)
            dbc_ref[:, g * _NSTATE:(g + 1) * _NSTATE] = dbg + _dot(dcb, cg, _TN)
            dbc_ref[:, gw + g * _NSTATE:gw + (g + 1) * _NSTATE] = dcg + _dot(dcb, bgb)

    ch = lambda c: nc - 1 - c
    wide = pl.BlockSpec((_CHUNK, 2 * gw), lambda c: (ch(c), 0))
    return pl.pallas_call(
        body, grid=(nc,), in_specs=[sp['xs'], sp['bc'], sp['lane'], sp['lane'], sp['arow'], sp['st'], sp['xs']],
        out_specs=[sp['xs'], wide, sp['lane'], sp['lane']],
        out_shape=[_SDS((t, _AW), F32), _SDS((t, 2 * gw), F32), _SDS((t, _LANES), F32), _SDS((t, _LANES), F32)],
        scratch_shapes=[pltpu.VMEM((_HEADS // 2, _NSTATE, _LANES), F32)], name=name,
        compiler_params=_params(("arbitrary",), 16 << 20))(act, act, dt, acum, a_row, sall, dy)


def _scan_rows(v, reverse):
    r = lax.broadcasted_iota(jnp.int32, v.shape, 0)
    for s in (1, 2, 4, 8, 16, 32, 64):
        if reverse:
            v = v + jnp.where(r < _CHUNK - s, pltpu.roll(v, _CHUNK - s, 0), 0.0)
        else:
            v = v + jnp.where(r >= s, pltpu.roll(v, s, 0), 0.0)
    return v


def _softplus(x):
    return jnp.maximum(x, 0.0) + jnp.log(1.0 + jnp.exp(-jnp.abs(x)))


def _sigmoid(x):
    return 1.0 / (1.0 + jnp.exp(-x))


def _silu(x):
    return x * _sigmoid(x)


def _dsilu(x):
    s = _sigmoid(x)
    return s * (1.0 + x * (1.0 - s))


def _lanes(a):
    return jnp.pad(a, (0, _LANES - a.shape[0])).reshape(1, _LANES)


def _layer_fwd(x, p, l):
    cch = p['conv_w'].shape[1]
    sv = {}
    h1 = _rms_fwd(x, p['ln1_g'], name=f"ln1_fwd_{l}")
    qkv = _mm(h1, p['w_qkv'], outs=(_BF,), name=f"in_proj_qkv_{l}")
    xbc = _mm(h1, p['w_xbc'], name=f"in_proj_xbc_{l}")
    zdt = _mm(h1, p['w_zdt'], name=f"in_proj_zdt_{l}")
    z, dt_raw = (zdt, _AW, 0), (zdt, _LANES, _AW // _LANES)

    outs = []
    for dil in _DILATIONS:
        outs += _attn_fwd(qkv, dil, name=f"attn_fwd_d{dil}_{l}")

    tile = 2 * _ABLK
    perms = [_perm(d, tile) for d in _DILATIONS[1:]]

    def combine(o1, l1, o2, l2, o3, l3, p2, p3):
        m = jnp.maximum(jnp.maximum(l1, l2), l3)
        e1, e2, e3 = jnp.exp(l1 - m), jnp.exp(l2 - m), jnp.exp(l3 - m)
        tot = e1 + e2 + e3
        mixed = sum(_expand_heads(e / tot) * o for e, o in ((e1, o1.astype(F32)), (e2, _unstride(o2, p2)), (e3, _unstride(o3, p3))))
        return mixed, m + jnp.log(tot)
    outs = [a if i % 2 or i == 0 else ("strided", a, _DILATIONS[i // 2]) for i, a in enumerate(outs)]
    attn, lse = _rows(combine, outs, perms, [(_AW, F32), (_LANES, F32)], tile=tile, name=f"attn_combine_{l}")
    attn_n = _rms_fwd(attn, p['attn_norm_g'], name=f"attn_norm_fwd_{l}")

    def conv(u0, before, w, b):
        u1, u2, u3 = _shifted(u0, before, True)
        return _silu(w[0:1] * u3 + w[1:2] * u2 + w[2:3] * u1 + w[3:4] * u0 + b)
    act = _rows(conv, [xbc], [p['conv_w'], p['conv_b'].reshape(1, cch)], [(cch, F32)], halos=[(xbc, -1)], tile=_tile_for(cch),
                name=f"conv_fwd_{l}")[0]

    def dtf(raw, bias, alog):
        dt = _softplus(raw + bias)
        return dt, _scan_rows(dt * -jnp.exp(alog), False)
    dt, acum = _rows(dtf, [dt_raw], [_lanes(p['dt_bias']), _lanes(p['a_log'])], [(_LANES, F32), (_LANES, F32)],
                     tile=_CHUNK, name=f"dt_fwd_{l}")
    a_row = acum[:, :_HEADS].T[:, None, :]
    y_ssd, sall = _ssd_fwd(act, dt, acum, a_row, name=f"ssd_fwd_{l}")
    dskip = jnp.repeat(p['d_skip'], _HDIM).reshape(1, _AW)
    xs = (act, _AW, 0)

    def gate(y, xs, z, dsk):
        return (y + dsk * xs) * _silu(z)
    y2 = _rows(gate, [y_ssd, xs, z], [dskip], [(_AW, F32)], tile=_tile_for(_AW), name=f"gate_fwd_{l}")[0]
    y_n = _rms_fwd(y2, p['ssd_norm_g'], groups=_GROUPS, name=f"ssd_norm_fwd_{l}")

    mix = jnp.concatenate([attn_n, y_n], axis=1)
    sv.update(x=x, h1=h1, qkv=qkv, zdt=zdt, xbc=xbc, attn=attn, lse=lse, act=act, dt=dt, acum=acum, a_row=a_row,
              sall=sall, y_ssd=y_ssd, dskip=dskip, y2=y2, mix=mix)
    return mix, sv


def _layer_fwd_mlp(p, sv, l):
    x2 = _mm(sv['mix'], p['w_out'], extra=(sv['x'],), epi=_add_to, name=f"out_proj_{l}")
    h2 = _rms_fwd(x2, p['ln2_g'], name=f"ln2_fwd_{l}")
    a = _mm(h2, p['w_mlp_in'], b_chips=_CHIPS, epi=lambda acc: (jnp.square(jnp.maximum(acc, 0.0)),), outs=(_BF,), name=f"mlp_in_{l}")
    x3 = _mm(a, p['w_mlp_out'], extra=(x2,), epi=_add_to, name=f"mlp_out_{l}")
    sv.update(x2=x2, h2=h2, a=a)
    return x3


def _layer_bwd(dx3, p, sv, l, send, after):
    cch = p['conv_w'].shape[1]
    g = {}
    dx3b = dx3.astype(_BF)
    du = _mm(dx3b, p['w_mlp_out'], tb=True, extra=(sv['a'],), outs=(_BF,), after=after,
             epi=lambda acc, a: (acc * 2.0 * jnp.sqrt(a.astype(F32)),), name=f"mlp_out_dx_{l}")
    g['w_mlp_out'] = _mm(sv['a'], dx3b, ta=True, outs=(_BF,), name=f"mlp_out_dw_{l}")
    g['w_mlp_in'] = _mm(sv['h2'], du, ta=True, out_chips=_CHIPS, outs=(_BF,), name=f"mlp_in_dw_{l}")
    sent = send(('w_mlp_out', 'w_mlp_in'), g)
    dh2 = _mm(du, p['w_mlp_in'], tb=True, b_chips=_CHIPS, after=sent, name=f"mlp_in_dx_{l}")
    dx2, g['ln2_g'] = _rms_bwd(sv['x2'], dh2, p['ln2_g'], dx3, name=f"ln2_bwd_{l}")
    dx2b = dx2.astype(_BF)
    dmix = _mm(dx2b, p['w_out'], tb=True, name=f"out_proj_dx_{l}")
    g['w_out'] = _mm(sv['mix'], dx2b, ta=True, outs=(_BF,), name=f"out_proj_dw_{l}")

    tile = 2 * _ABLK
    perms = [_perm(d, tile) for d in _DILATIONS[1:]]

    def norm_bwd(attn, dy, lse, gn, p2, p3):
        dattn, dgn = _rms_bwd_tile(attn, dy, gn, 1)
        prod, low = dattn * attn, _low_lanes((attn.shape[0], _LANES))
        lane = lax.broadcasted_iota(jnp.int32, lse.shape, 1)
        ld = jnp.where(lane < _HEADS, lse, 0.0)
        for pr in range(_HEADS // 2):
            for i, part in enumerate(_halves(prod[:, _pair(pr)], low)):
                ld = jnp.where(lane == _HEADS + 2 * pr + i, _rsum(part), ld)
        return dattn, _stride(dattn, p2, _DILATIONS[1]), _stride(dattn, p3, _DILATIONS[2]), ld, dgn
    *dos, ld, gn_sum = _rows(norm_bwd, [sv['attn'], (dmix, _AW, 0), sv['lse']], [p['attn_norm_g'].reshape(1, _AW)] + perms,
                             [(_AW, _BF)] + [(_AW, _BF, d) for d in _DILATIONS[1:]] + [(_LANES, F32)], [_AW], tile=tile,
                             name=f"attn_norm_bwd_{l}")
    g['attn_norm_g'] = gn_sum.sum(axis=0)
    parts = [_attn_bwd(sv['qkv'], do, ld, dil, name=f"attn_bwd_d{dil}_{l}") for do, dil in zip(dos, _DILATIONS)]

    def branch_sum(*t):
        parts_, (p2, p3) = t[:9], t[9:]
        t = [a.astype(F32) for a in parts_[:3]] + [_unstride(a, p2) for a in parts_[3:6]] + [_unstride(a, p3) for a in parts_[6:]]
        return jnp.concatenate([t[i] + t[3 + i] + t[6 + i] for i in range(3)], axis=1)
    branch_ins = list(parts[0]) + [("strided", a, d) for pr, d in zip(parts[1:], _DILATIONS[1:]) for a in pr]
    dqkv = _rows(branch_sum, branch_ins, perms, [(3 * _AW, _BF)], tile=tile, name=f"attn_bwd_sum_{l}")[0]

    xs, z, dt_raw = (sv['act'], _AW, 0), (sv['zdt'], _AW, 0), (sv['zdt'], _LANES, _AW // _LANES)

    def gate_bwd(y2, dy, y, xs, z, dsk, gn):
        dy2, dgn = _rms_bwd_tile(y2, dy, gn, _GROUPS)
        dy1 = dy2 * _silu(z)
        return dy1, dsk * dy1, dy2 * (y + dsk * xs) * _dsilu(z), dy1 * xs, dgn
    dy1, dxs_skip, dz, dsk_sum, gn_sum = _rows(
        gate_bwd, [sv['y2'], (dmix, _AW, 1), sv['y_ssd'], xs, z], [sv['dskip'], p['ssd_norm_g'].reshape(1, _AW)],
        [(_AW, F32), (_AW, F32), (_AW, _BF)], [_AW, _AW], tile=128, name=f"gate_bwd_{l}")
    g['ssd_norm_g'] = gn_sum.sum(axis=0)
    g['d_skip'] = dsk_sum.sum(axis=0).reshape(_HEADS, _HDIM).sum(axis=1)
    dxs, dbc, ddt, da = _ssd_bwd(sv['act'], sv['dt'], sv['acum'], sv['a_row'], sv['sall'], dy1, name=f"ssd_bwd_{l}")

    def dtb(da, ddtx, raw, dt, dz, bias, alog):
        a = -jnp.exp(alog)
        dda = _scan_rows(da, True)
        draw = (dda * a + ddtx) * _sigmoid(raw + bias)
        return jnp.concatenate([dz, draw.astype(dz.dtype)], axis=1), draw, dda * dt * a
    dzdt, dbias, dalog = _rows(dtb, [da, ddt, dt_raw, sv['dt'], dz], [_lanes(p['dt_bias']), _lanes(p['a_log'])],
                               [(_AW + _LANES, _BF)], [_LANES, _LANES], tile=_CHUNK, name=f"dt_bwd_{l}")
    g['dt_bias'], g['a_log'] = dbias.sum(axis=0)[:_HEADS], dalog.sum(axis=0)[:_HEADS]
    def conv_bwd1(u0, dxs, dbc, dxk, before, w, b):
        u1, u2, u3 = _shifted(u0, before, True)
        pre = w[0:1] * u3 + w[1:2] * u2 + w[2:3] * u1 + w[3:4] * u0 + b
        dp = jnp.concatenate([dxs + dxk, dbc], axis=1) * _dsilu(pre)
        return dp, dp * u3, dp * u2, dp * u1, dp * u0, dp
    dpre, *dws = _rows(conv_bwd1, [sv['xbc'], dxs, dbc, dxs_skip], [p['conv_w'], p['conv_b'].reshape(1, cch)], [(cch, F32)],
                       [cch] * 5, halos=[(sv['xbc'], -1)], tile=128, name=f"conv_bwd_pre_{l}")
    g['conv_w'] = jnp.stack([dws[i].sum(axis=0) for i in range(_CONV_K)])
    g['conv_b'] = dws[4].sum(axis=0)

    def conv_bwd2(p0, after_, w):
        p1, p2, p3 = _shifted(p0, after_, False)
        return w[3:4] * p0 + w[2:3] * p1 + w[1:2] * p2 + w[0:1] * p3
    dxbc = _rows(conv_bwd2, [dpre], [p['conv_w']], [(cch, _BF)], halos=[(dpre, 1)], tile=_tile_for(cch), name=f"conv_bwd_in_{l}")[0]
    h1 = sv['h1']
    g_qkv = _mm(h1, dqkv, ta=True, outs=(_BF,), name=f"in_proj_qkv_dw_{l}")
    g_xbc = _mm(h1, dxbc, ta=True, outs=(_BF,), name=f"in_proj_xbc_dw_{l}")
    g_zdt = _mm(h1, dzdt, ta=True, outs=(_BF,), name=f"in_proj_zdt_dw_{l}")
    g['w_in'] = jnp.concatenate([g_qkv, g_zdt[:, :_AW], g_xbc, g_zdt[:, _AW:_AW + _HEADS]], axis=1)
    sent = send(('w_out', 'w_in'), g)
    for n in _BIG:
        del g[n]
    dh1 = _mm(dqkv, p['w_qkv'], tb=True, after=sent, name=f"in_proj_qkv_dx_{l}")
    dh1 = _mm(dxbc, p['w_xbc'], tb=True, extra=(dh1,), epi=_add_to, name=f"in_proj_xbc_dx_{l}")
    dh1 = _mm(dzdt, p['w_zdt'], tb=True, extra=(dh1,), epi=_add_to, name=f"in_proj_zdt_dx_{l}")
    dx, g['ln1_g'] = _rms_bwd(sv['x'], dh1, p['ln1_g'], dx2, name=f"ln1_bwd_{l}")
    return dx, g


def _loss_bwd(x, g, tgt):
    w = x.shape[1]
    tile = _tile_for(w)

    def fn(x, tgt, g):
        r = _rstd(x)
        xh = x * r
        e = xh * g - tgt
        gd = e * (g / w)
        dx = r * (gd - xh * jnp.mean(gd * xh, axis=-1, keepdims=True))
        rowloss = 0.5 * jnp.mean(e * e, axis=-1, keepdims=True)
        return dx, (e / w) * xh, jnp.broadcast_to(rowloss, (tile, _LANES))
    dx, dg, ls = _rows(fn, [x, tgt], [g.reshape(1, w)], [(w, F32)], [w, _LANES], tile=tile, name="loss_head")
    return dx, dg.sum(axis=0), ls[:, 0].sum()


def _adamw_math(w, g, m, v):
    m2 = _B1 * m + (1.0 - _B1) * g
    v2 = _B2 * v + (1.0 - _B2) * jnp.square(g)
    m_hat = m2 / (1.0 - _B1 ** _STEP)
    v_hat = v2 / (1.0 - _B2 ** _STEP)
    return -_LR * (m_hat / (jnp.sqrt(v_hat) + _AEPS) + _WD * w), m2, v2


def _adamw(w, g, m, v, *, name):
    width = w.shape[-1]
    flat = [a.reshape(-1, width) for a in (w, g, m, v)]
    tile = _pick(flat[0].shape[0], (_tile_for(width), 32, 8))
    res = _rows(_adamw_math, flat, [], [(width, F32)] * 3, tile=tile, name=name)
    return [r.reshape(w.shape) for r in res]


_HBM = pl.BlockSpec(memory_space=pltpu.HBM)


def _place():
    x, y, c = lax.axis_index("x"), lax.axis_index("y"), lax.axis_index("c")
    other_chips = [(1 - x, y), (x, 1 - y), (1 - x, 1 - y)]
    return x, y, c, other_chips


def _remote(src, dst, sems, i, dev):
    return pltpu.make_async_remote_copy(src_ref=src, dst_ref=dst, send_sem=sems[0].at[i], recv_sem=sems[1].at[i],
                                        device_id=dev, device_id_type=_MESH)


def _exchange8(v, *, reduce, after=None, name):
    r, w = v.shape
    behind = [] if after is None else [after]

    def body(v_ref, *rest):
        all_ref, rest = rest[len(behind)], rest[len(behind) + 1:]
        sems = rest[-2:]
        x, y, c, _ = _place()
        me = 4 * x + 2 * y + c
        all_ref[me] = v_ref[...]
        flips = [((d >> 2) & 1, (d >> 1) & 1, d & 1) for d in range(1, 8)]
        sends = [_remote(v_ref, all_ref.at[me], sems, i, (x ^ fx, y ^ fy, c ^ fc)) for i, (fx, fy, fc) in enumerate(flips)]
        for cp in sends:
            cp.start()
        for i, (fx, fy, fc) in enumerate(flips):
            _remote(v_ref, all_ref.at[me ^ (4 * fx + 2 * fy + fc)], sems, i, (x ^ fx, y ^ fy, c ^ fc)).wait_recv()
        for cp in sends:
            cp.wait_send()
        if reduce:
            acc = all_ref[0]
            for s in range(1, 8):
                acc = acc + all_ref[s]
            rest[0][...] = acc

    vm = pl.BlockSpec(memory_space=pltpu.VMEM)
    out_shape = [_SDS((8, r, w), v.dtype)] + ([_SDS((r, w), v.dtype)] if reduce else [])
    res = pl.pallas_call(body, in_specs=[vm] + [_ANY] * len(behind), out_specs=[vm] * len(out_shape), out_shape=out_shape, name=name,
                         scratch_shapes=[pltpu.SemaphoreType.DMA((7,)), pltpu.SemaphoreType.DMA((7,))],
                         compiler_params=pltpu.CompilerParams(vmem_limit_bytes=int(32 << 20)))(v, *behind)
    return res[1] if reduce else res[0]


_SEM = pl.BlockSpec(memory_space=pltpu.SEMAPHORE)
_ANY = pl.BlockSpec(memory_space=pl.ANY)
_EFFECT = pltpu.SideEffectType.DATAFLOW_SIDE_EFFECTING


def _in_hbm(a):
    return pltpu.with_memory_space_constraint(a, pltpu.HBM)


def _send_start(name, srcs, land_shapes, plan, n_sends, after):
    ns, nl = len(srcs), len(land_shapes)
    zones = [_in_hbm(lax.empty(s.shape, s.dtype)) if isinstance(s, _SDS) else s for s in land_shapes]

    def body(*refs):
        ins, lands, sems = refs[:ns], refs[ns:ns + nl], refs[ns + nl + 1:ns + nl + 3]
        x, y, c, chips = _place()
        for i, (s, d, dev) in enumerate(plan(x, y, c, chips, ins, lands)[0]):
            _remote(s, d, sems, i, dev).start()
        refs[-1][...] = jnp.zeros_like(refs[-1])

    sem = pltpu.SemaphoreType.DMA((n_sends,))
    res = pl.pallas_call(
        body, name=name, in_specs=[_HBM] * (ns + nl) + [_ANY],
        out_shape=(sem, sem, *[pltpu.HBM(s.shape, s.dtype) for s in land_shapes], _SDS((8, _LANES), F32)),
        out_specs=(_SEM, _SEM, *[_HBM] * nl, pl.BlockSpec(memory_space=pltpu.VMEM)),
        input_output_aliases={ns + i: 2 + i for i in range(nl)},
        compiler_params=pltpu.CompilerParams(has_side_effects=_EFFECT))(
            *[_in_hbm(s) for s in srcs], *zones, after)
    return dict(sems=res[:2], srcs=srcs, lands=res[2:2 + nl], plan=plan), res[-1]


def _send_wait(name, h, after):
    ns, nl = len(h['srcs']), len(h['lands'])

    def body(*refs):
        ins, lands, sems = refs[:ns], refs[ns:ns + nl], refs[ns + nl:ns + nl + 2]
        x, y, c, chips = _place()
        sends, landings = h['plan'](x, y, c, chips, ins, lands)
        for i, (s, d, dev) in enumerate(sends):
            _remote(s, d, sems, i, dev).wait_send()
        for i, d in enumerate(landings):
            _remote(d, d, sems, i, sends[i][2]).wait_recv()

    return pl.pallas_call(
        body, name=name, in_specs=[_HBM] * (ns + nl) + [_SEM, _SEM, _ANY],
        out_shape=tuple(pltpu.HBM(a.shape, a.dtype) for a in h['lands']), out_specs=tuple([_HBM] * nl),
        input_output_aliases={ns + i: i for i in range(nl)},
        compiler_params=pltpu.CompilerParams(has_side_effects=_EFFECT))(
            *[_in_hbm(s) for s in h['srcs']], *h['lands'], *h['sems'], after)


def _gather_plan(items):
    def plan(x, y, c, chips, ins, lands):
        k = 2 * x + y
        sends = [(ins[si].at[l], lands[t].at[k], (px, py, c)) for t, (si, l) in enumerate(items) for px, py in chips]
        return sends, [lands[t].at[2 * px + py] for t in range(len(items)) for px, py in chips]
    return plan


_FLIPS = [((d >> 2) & 1, (d >> 1) & 1, d & 1) for d in range(1, 8)]


def _reduce_plan(halves):
    def plan(x, y, c, chips, ins, lands):
        sends, landings = [], []
        for t, hf in enumerate(halves):
            for i, (fx, fy, fc) in enumerate(_FLIPS):
                px, py, pc = x ^ fx, y ^ fy, c ^ fc
                sends.append((ins[t].at[2 * px + py, pl.ds(pc * hf, hf)], lands[t].at[i], (px, py, pc)))
                landings.append(lands[t].at[i])
        return sends, landings
    return plan


def _swap(name, srcs, out_shapes, plan, n_sends):
    n = len(srcs)

    def body(*refs):
        ins, outs, sems = refs[:n], refs[n:n + len(out_shapes)], refs[-2:]
        x, y, c, chips = _place()
        sends, landings = plan(x, y, c, chips, ins, outs)
        out = [_remote(s, d, sems, i, dev) for i, (s, d, dev) in enumerate(sends)]
        for cp in out:
            cp.start()
        for i, d in enumerate(landings):
            _remote(d, d, sems, i, sends[i][2]).wait_recv()
        for cp in out:
            cp.wait_send()

    return pl.pallas_call(
        body, in_specs=[_HBM] * n, out_specs=[_HBM] * len(out_shapes), out_shape=out_shapes, name=name,
        scratch_shapes=[pltpu.SemaphoreType.DMA((n_sends,)), pltpu.SemaphoreType.DMA((n_sends,))])(*srcs)


def _sum_owned(grads, landed, c, k, names):
    def sum8(*parts):
        acc = parts[0].astype(F32)
        for p in parts[1:]:
            acc = acc + p.astype(F32)
        return acc
    outs = []
    for g, got, name in zip(grads, landed, names):
        hf, b = got.shape[1:]
        own = lax.dynamic_slice_in_dim(lax.dynamic_index_in_dim(g, k, axis=0, keepdims=False), c * hf, hf, axis=0)
        outs.append(_rows(sum8, [own] + [got[i] for i in range(len(_FLIPS))], [], [(b, F32)], tile=_pick(hf, (_tile_for(b), 32)),
                          name=f"grad_sum_{name}")[0])
    return outs


def _share_halves(mine):
    n = len(mine)

    def plan(x, y, c_, chips, ins, outs):
        return [(ins[t], outs[t], (x, y, 1 - c_)) for t in range(n)], [outs[t] for t in range(n)]
    return _swap("grad_share_cores", mine, [_SDS(h.shape, F32) for h in mine], plan, n)


def _adamw_owned(w, mine, theirs, m, v, c, *, name):
    depth, a, b = w.shape
    half = a // 2
    tile = _pick(half, (_tile_for(b), 32, 8))
    nh = half // tile

    def blocks_of(l):
        return lambda i: (jnp.clip(i - 2 * nh * l, 0, 2 * nh - 1) % nh, 0)

    def fn(w, m, v, *rest):
        halves, cflag = rest[:-1], rest[-1]
        step = pl.program_id(0)
        is_mine = cflag[0:1, 0:1] == ((step // nh) % 2).astype(F32)
        g = jnp.where(is_mine, halves[0], halves[1])
        for l in range(1, depth):
            g = jnp.where(step >= 2 * nh * l, jnp.where(is_mine, halves[2 * l], halves[2 * l + 1]), g)
        return (g,) + _adamw_math(w, g, m, v)
    ins = [a_.reshape(depth * a, b) for a_ in (w, m, v)]
    ins += [(h, b, blocks_of(l)) for l in range(depth) for h in (mine[l], theirs[l])]
    res = _rows(fn, ins, [jnp.full((1, _LANES), c, F32)], [(b, F32)] * 4, tile=tile, name=name)
    return [r.reshape(w.shape) for r in res]


_BIG = ("w_in", "w_out", "w_mlp_in", "w_mlp_out")
_SMALL = ("ln1_g", "conv_b", "dt_bias", "a_log", "d_skip", "attn_norm_g", "ssd_norm_g", "ln2_g", "final_norm_g")
_ORDER = ("ln1_g", "w_in", "conv_w", "conv_b", "dt_bias", "a_log", "d_skip", "attn_norm_g", "ssd_norm_g", "w_out", "ln2_g",
          "w_mlp_in", "w_mlp_out", "final_norm_g")


def _pack(parts, rows):
    flat = jnp.concatenate([p.reshape(-1) for p in parts])
    return jnp.pad(flat, (0, rows * _LANES - flat.shape[0])).reshape(rows, _LANES)


def _unpack(buf, like):
    flat, out, o = buf.reshape(-1), [], 0
    for p in like:
        out.append(flat[o:o + p.size].reshape(p.shape))
        o += p.size
    return out


def kernel(x, ln1_g, w_in, conv_w, conv_b, dt_bias, a_log, d_skip, attn_norm_g, ssd_norm_g, w_out, ln2_g, w_mlp_in, w_mlp_out, final_norm_g, loss_target, m_ln1_g, m_w_in, m_conv_w, m_conv_b, m_dt_bias, m_a_log, m_d_skip, m_attn_norm_g, m_ssd_norm_g, m_w_out, m_ln2_g, m_w_mlp_in, m_w_mlp_out, m_final_norm_g, v_ln1_g, v_w_in, v_conv_w, v_conv_b, v_dt_bias, v_a_log, v_d_skip, v_attn_norm_g, v_ssd_norm_g, v_w_out, v_ln2_g, v_w_mlp_in, v_w_mlp_out, v_final_norm_g):
    w = dict(ln1_g=ln1_g, w_in=w_in, conv_w=conv_w, conv_b=conv_b, dt_bias=dt_bias, a_log=a_log, d_skip=d_skip,
             attn_norm_g=attn_norm_g, ssd_norm_g=ssd_norm_g, w_out=w_out, ln2_g=ln2_g, w_mlp_in=w_mlp_in, w_mlp_out=w_mlp_out,
             final_norm_g=final_norm_g)
    m = dict(ln1_g=m_ln1_g, w_in=m_w_in, conv_w=m_conv_w, conv_b=m_conv_b, dt_bias=m_dt_bias, a_log=m_a_log, d_skip=m_d_skip,
             attn_norm_g=m_attn_norm_g, ssd_norm_g=m_ssd_norm_g, w_out=m_w_out, ln2_g=m_ln2_g, w_mlp_in=m_w_mlp_in,
             w_mlp_out=m_w_mlp_out, final_norm_g=m_final_norm_g)
    v = dict(ln1_g=v_ln1_g, w_in=v_w_in, conv_w=v_conv_w, conv_b=v_conv_b, dt_bias=v_dt_bias, a_log=v_a_log, d_skip=v_d_skip,
             attn_norm_g=v_attn_norm_g, ssd_norm_g=v_ssd_norm_g, w_out=v_w_out, ln2_g=v_ln2_g, w_mlp_in=v_w_mlp_in,
             w_mlp_out=v_w_mlp_out, final_norm_g=v_final_norm_g)
    depth, d_model = ln1_g.shape
    n_chips = 4
    c = lax.axis_index("c")
    chip = 2 * lax.axis_index("x") + lax.axis_index("y")
    in_proj = w_in.shape[2] * n_chips
    cch = conv_w.shape[2] * n_chips
    zdt_pad = _LANES - _HEADS

    cw = _exchange8(conv_w.reshape(depth * _CONV_K, -1), reduce=False, name="gather_conv_w")[0::2]
    conv_full = cw.reshape(n_chips, depth, _CONV_K, -1).transpose(1, 2, 0, 3).reshape(depth, _CONV_K, cch)
    own = [w[n].astype(_BF) for n in _BIG]
    is_own = (jnp.arange(n_chips) == chip).reshape(n_chips, 1, 1)

    def start_gather(tag, items, after):
        lands = [_SDS((n_chips, *own[i].shape[1:]), _BF) for i, _ in items]
        return _send_start(f"gather_start_{tag}", own, lands, _gather_plan(items), 3 * len(items), after)

    def finish_gather(tag, handle, items, after):
        landed = _send_wait(f"gather_wait_{tag}", handle, after)
        return {_BIG[i]: jnp.where(is_own, own[i][l][None], g) for (i, l), g in zip(items, landed)}

    def layer_weights(l, blocks):
        p = {}
        if 'w_in' in blocks:
            full_in = blocks['w_in'].transpose(1, 0, 2).reshape(d_model, in_proj)
            p['w_qkv'] = full_in[:, :3 * _AW]
            p['w_xbc'] = full_in[:, 4 * _AW:4 * _AW + cch]
            p['w_zdt'] = jnp.concatenate([full_in[:, 3 * _AW:4 * _AW], full_in[:, 4 * _AW + cch:], jnp.zeros((d_model, zdt_pad), _BF)], axis=1)
        if 'w_out' in blocks:
            p['w_out'] = blocks['w_out'].reshape(-1, d_model)
            p['w_mlp_in'] = blocks['w_mlp_in']
            p['w_mlp_out'] = blocks['w_mlp_out'].reshape(-1, d_model)
        return p

    groups = dict(a=[(0, 0)], b=[(1, 0), (2, 0), (3, 0)], c=[(0, 1)], d=[(1, 1), (2, 1), (3, 1)])
    handles, token = {}, conv_full

    half_in = own[0].shape[1] // 2

    def rows_of(ref, who):
        return ref.at[pl.ds(who * half_in, half_in)]

    def plan_a(x_, y_, c_, chips, ins, lands):
        k = 2 * x_ + y_
        sends = [(rows_of(ins[0].at[0], c_), rows_of(lands[0].at[k], c_), (px, py, c_)) for px, py in chips]
        return sends, [rows_of(lands[0].at[2 * px + py], c_) for px, py in chips]

    def plan_pass(x_, y_, c_, chips, ins, lands):
        sends = [(rows_of(lands[0].at[2 * px + py], c_),) * 2 + ((x_, y_, 1 - c_),) for px, py in chips]
        return sends, [rows_of(lands[0].at[2 * px + py], 1 - c_) for px, py in chips]
    handles["a"], token = _send_start("gather_start_a", own[:1], [_SDS((n_chips, *own[0].shape[1:]), _BF)], plan_a, 3, token)
    for tag, items in list(groups.items())[1:]:
        handles[tag], token = start_gather(tag, items, token)
    landed = _send_wait("gather_land_a", handles["a"], token)
    handles["a"], token = _send_start("gather_pass_a", [], landed, plan_pass, 3, landed[0])
    layers = [{n: w[n][l] for n in _SMALL[:-1]} for l in range(depth)]
    for l in range(depth):
        layers[l]['conv_w'] = conv_full[l]

    layers[0].update(layer_weights(0, finish_gather("a", handles["a"], groups["a"], token)))
    mix, sv0 = _layer_fwd(x[0], layers[0], 0)
    layers[0].update(layer_weights(0, finish_gather("b", handles["b"], groups["b"], mix)))
    h = _layer_fwd_mlp(layers[0], sv0, 0)
    layers[1].update(layer_weights(1, finish_gather("c", handles["c"], groups["c"], h)))
    mix, sv1 = _layer_fwd(h, layers[1], 1)
    layers[1].update(layer_weights(1, finish_gather("d", handles["d"], groups["d"], mix)))
    h = _layer_fwd_mlp(layers[1], sv1, 1)
    saved = [sv0, sv1]

    def by_chip(g, name):
        if name == "w_mlp_in":
            return g
        if name == "w_in":
            return g.reshape(d_model, n_chips, -1).transpose(1, 0, 2)
        return g.reshape(n_chips, -1, d_model)

    pending = []

    def sender(l):
        def send(names, g):
            srcs = [by_chip(g[n], n) for n in names]
            halves = [s.shape[1] // 2 for s in srcs]
            lands = [_SDS((len(_FLIPS), hf, s.shape[2]), _BF) for s, hf in zip(srcs, halves)]
            handle, tok = _send_start(f"grad_start_{names[-1]}_{l}", srcs, lands, _reduce_plan(halves), len(_FLIPS) * len(srcs), srcs[0])
            pending.append((l, names, srcs, handle))
            return tok
        return send

    dx, g_final, loss_part = _loss_bwd(h, final_norm_g, loss_target[0])
    grads, after = [None] * depth, None
    for l in reversed(range(depth)):
        dx, grads[l] = _layer_bwd(dx, layers[l], saved[l], l, sender(l), after)
        after = dx
    owned = {}
    for l, names, srcs, handle in pending:
        landed = _send_wait(f"grad_wait_{names[-1]}_{l}", handle, dx)
        owned.update(zip([(n, l) for n in names], _sum_owned(srcs, landed, c, chip, [f"{n}_{l}" for n in names])))
    keys = [(n, l) for n in _BIG for l in range(depth)]
    theirs = dict(zip(keys, _share_halves([owned[k] for k in keys])))
    red, delta, new_m, new_v = {}, {}, {}, {}
    for n in _BIG:
        red[n], delta[n], new_m[n], new_v[n] = _adamw_owned(
            w[n], [owned[(n, l)] for l in range(depth)], [theirs[(n, l)] for l in range(depth)], m[n], v[n], c, name=f"adamw_{n}")

    small = {n: jnp.stack([grads[l][n] for l in range(depth)]) for n in _SMALL[:-1] + ("conv_w",)}
    small["final_norm_g"] = g_final
    parts = [loss_part.reshape(1)] + [small[n] for n in _SMALL + ("conv_w",)]
    rows = -(-sum(p.size for p in parts) // 1024) * 8
    tot = _unpack(_exchange8(_pack(parts, rows), reduce=True, after=red[_BIG[0]], name="allreduce_small"), parts)
    loss = tot[0][0]
    red.update(zip(_SMALL + ("conv_w",), tot[1:]))
    red["conv_w"] = lax.dynamic_index_in_dim(red["conv_w"].reshape(depth, _CONV_K, n_chips, -1), chip, axis=2, keepdims=False)

    names = _SMALL + ("conv_w",)
    like = [w[n] for n in names]
    srows = -(-sum(p.size for p in like) // 1024) * 8
    res = _adamw(*[_pack([d[n] for n in names], srows) for d in (w, red, m, v)], name="adamw_small")
    for dst, buf in zip((delta, new_m, new_v), res):
        dst.update(zip(names, _unpack(buf, like)))
    return (loss, dx[None], *[red[n] for n in _ORDER], *[delta[n] for n in _ORDER], *[new_m[n] for n in _ORDER],
            *[new_v[n] for n in _ORDER])
```

```python
import numpy as np
import jax
import jax.numpy as jnp
from jax import lax
from jax.experimental import pallas as pl
from jax.experimental.pallas import tpu as pltpu

F32 = jnp.float32
_BF = jnp.bfloat16
_NEG = -1e30
_EPS = 1e-5
_HEADS = 16
_HDIM = 64
_AW = _HEADS * _HDIM
_ABLK = 128
_DILATIONS = (1, 4, 16)
_CHUNK = 128
_NSTATE = 128
_GROUPS = 2
_HPG = _HEADS // _GROUPS
_CONV_K = 4
_LANES = 128
_CHIPS = 4
_LR, _B1, _B2, _AEPS, _WD, _STEP = 0.001, 0.9, 0.999, 1e-08, 0.01, 10
_VMEM_CAP = 56 * 1024 * 1024
_MESH = pl.DeviceIdType.MESH
_SDS = jax.ShapeDtypeStruct
_NT = (((1,), (1,)), ((), ()))
_TN = (((0,), (0,)), ((), ()))


def _params(sem, est_bytes):
    lim = int(min(max(2 * est_bytes + (4 << 20), 16 << 20), _VMEM_CAP))
    return pltpu.CompilerParams(dimension_semantics=sem, vmem_limit_bytes=lim)


def _nbytes(shape, dtype):
    return int(np.prod(shape)) * jnp.dtype(dtype).itemsize


def _hbm(a):
    return pltpu.with_memory_space_constraint(a, pltpu.HBM)


def _dot(a, b, dims=(((1,), (0,)), ((), ()))):
    return lax.dot_general(a.astype(_BF), b.astype(_BF), dims, preferred_element_type=F32)


_HALO = 8


def _rows(fn, ins, consts, outs, sums=(), *, halos=(), into=None, tile, name):
    rows = (ins[0][0] if isinstance(ins[0], tuple) else ins[0]).shape[0]
    n_steps = rows // tile

    def norm_in(a):
        if not isinstance(a, tuple):
            return a, tile, a.shape[1], lambda i: (i, 0)
        if isinstance(a[0], str):
            return a[1], tile // a[2], a[1].shape[1], lambda i: (i, 0)
        return a[0], tile, a[1], a[2] if callable(a[2]) else (lambda i, j=a[2]: (i, j))
    ins = [norm_in(a) for a in ins]
    outs = [(w, dt, d[0] if d else 1) for w, dt, *d in outs]
    n_in, n_h, n_c, n_o, n_s = len(ins), len(halos), len(consts), len(outs), len(sums)
    n_x = int(into is not None and into[0] is not None)

    def body(*refs):
        step = pl.program_id(0)
        vals = [r[...] for r in refs[:n_in]]
        for r, (_, side) in zip(refs[n_in:n_in + n_h], halos):
            vals.append(jnp.where(step == (0 if side < 0 else n_steps - 1), 0.0, r[...]))
        vals += [r[...] for r in refs[n_in + n_h:n_in + n_h + n_c]]
        refs = refs[:n_in] + refs[n_in + n_h:]
        res = fn(*vals)
        res = res if isinstance(res, tuple) else (res,)
        orefs = refs[n_in + n_c + n_x:n_in + n_c + n_x + n_o]
        srefs = refs[n_in + n_c + n_x + n_o:]
        for r, v in zip(orefs, res[:n_o]):
            r[...] = v.astype(r.dtype)
        if n_s:
            @pl.when(pl.program_id(0) == 0)
            def _():
                for r in srefs:
                    r[...] = jnp.zeros_like(r)
            for r, v in zip(srefs, res[n_o:]):
                r[...] += v.reshape(tile // 8, 8, v.shape[-1]).sum(axis=0)

    per = tile // _HALO
    in_specs = [pl.BlockSpec((r, w), idx) for _, r, w, idx in ins]
    in_specs += [pl.BlockSpec((_HALO, a.shape[1]), (lambda i: (jnp.maximum(i * per - 1, 0), 0)) if side < 0
                              else (lambda i: (jnp.minimum((i + 1) * per, rows // _HALO - 1), 0))) for a, side in halos]
    in_specs += [pl.BlockSpec(c.shape, lambda i, nd=c.ndim: (0,) * nd) for c in consts]
    out_shape = [_SDS((rows // d, d * w), dt) for w, dt, d in outs] + [_SDS((8, w), F32) for w in sums]
    out_specs = [pl.BlockSpec((tile // d, d * w), lambda i: (i, 0)) for w, _, d in outs]
    out_specs += [pl.BlockSpec((8, w), lambda i: (0, 0)) for w in sums]
    est = sum(_nbytes((r, w), a.dtype) for a, r, w, _ in ins) + sum(_nbytes((tile, w), dt) for w, dt, _ in outs)
    shared, aliases = [], {}
    if into is not None:
        buf, total, j = into
        out_shape[0] = _SDS((rows, total), outs[0][1])
        out_specs[0] = pl.BlockSpec((tile, outs[0][0]), lambda i: (i, j))
        if buf is not None:
            shared, aliases = [buf], {n_in + n_h + n_c: 0}
            in_specs.append(pl.BlockSpec(memory_space=pl.ANY))
    return pl.pallas_call(body, grid=(n_steps,), in_specs=in_specs, out_specs=out_specs, out_shape=out_shape, name=name,
                          input_output_aliases=aliases, compiler_params=_params(("arbitrary",), 3 * est))(
                              *[_hbm(a[0]) for a in ins], *[_hbm(a) for a, _ in halos], *consts, *shared)


def _perm(d, tile):
    p = np.zeros((tile, tile), np.float32)
    t = np.arange(tile)
    p[t, (t % d) * (tile // d) + t // d] = 1.0
    return jnp.asarray(p, _BF)


def _unstride(s, p):
    d = p.shape[0] // s.shape[0]
    w = s.shape[1] // d
    return _dot(p, jnp.concatenate([s[:, r * w:(r + 1) * w] for r in range(d)], axis=0))


def _stride(x, p, d):
    z = _dot(p, x, _TN)
    n = x.shape[0] // d
    return jnp.concatenate([z[r * n:(r + 1) * n] for r in range(d)], axis=1)


def _shifted(u, halo, back):
    n = u.shape[0] + _HALO
    if back:
        ext = jnp.concatenate([halo, u], axis=0)
        return [pltpu.roll(ext, j, 0)[_HALO:] for j in (1, 2, 3)]
    ext = jnp.concatenate([u, halo], axis=0)
    return [pltpu.roll(ext, n - j, 0)[:u.shape[0]] for j in (1, 2, 3)]


def _tile_for(width):
    return max(c for c in (256, 128, 64, 32) if c * width <= (1 << 18) or c == 32)


def _rstd(x):
    return lax.rsqrt(jnp.mean(x * x, axis=-1, keepdims=True) + _EPS)


def _split(x, groups):
    w = x.shape[-1] // groups
    return [x[:, g * w:(g + 1) * w] for g in range(groups)]


def _cat(parts):
    return parts[0] if len(parts) == 1 else jnp.concatenate(parts, axis=-1)


def _rms_bwd_tile(x, dy, g, groups):
    dxs, dgs = [], []
    for xs, ds, gs in zip(_split(x, groups), _split(dy.astype(F32), groups), _split(g, groups)):
        r = _rstd(xs)
        xh = xs * r
        gd = ds * gs
        dxs.append(r * (gd - xh * jnp.mean(gd * xh, axis=-1, keepdims=True)))
        dgs.append(ds * xh)
    return _cat(dxs), _cat(dgs)


def _rms_fwd(x, g, *, groups=1, name):
    def fn(x, g):
        return _cat([xs * _rstd(xs) * gs for xs, gs in zip(_split(x, groups), _split(g, groups))])
    w = x.shape[1]
    return _rows(fn, [x], [g.reshape(1, w)], [(w, _BF)], tile=_tile_for(w), name=name)[0]


def _rms_bwd(x, dy, g, res=None, *, name):
    def fn(x, dy, *rest):
        dx, dg = _rms_bwd_tile(x, dy, rest[-1], 1)
        return (dx + rest[0] if res is not None else dx), dg
    w = x.shape[1]
    ins = [x, dy] + ([res] if res is not None else [])
    dx, dg = _rows(fn, ins, [g.reshape(1, w)], [(w, F32)], [w], tile=_tile_for(w), name=name)
    return dx, dg.sum(axis=0)


def _pick(n, cands):
    for c in cands:
        if n % c == 0:
            return c
    raise ValueError(f"no block size for {n}")


_MM_BLOCKS = (1024, 640, 512, 384)


def _mm(a, b, *, ta=False, tb=False, extra=(), epi=None, outs=(F32,), after=None, b_chips=0, out_chips=0, b_cols=None, name):
    m, k = (a.shape[1], a.shape[0]) if ta else a.shape
    b_shape = (b.shape[1], b.shape[2] * b_chips) if b_chips else b.shape
    if b_cols is not None:
        b_shape = (b.shape[0], b_cols[1])
    n = b_shape[0] if tb else b_shape[1]
    assert k == (b_shape[1] if tb else b_shape[0])
    n_cap = n // max(out_chips, 1 if tb else b_chips, 1)
    k_cap = k // (b_chips if (b_chips and tb) else 1)
    bm, bn = _pick(m, _MM_BLOCKS), _pick(n_cap, _MM_BLOCKS)
    bk = _pick(k_cap, (2048,) + _MM_BLOCKS)
    nk = k // bk
    n_e, n_o = len(extra), len(outs)
    behind = [] if after is None else [after]
    dims = (((0 if ta else 1,), (1 if tb else 0,)), ((), ()))

    def body(a_ref, b_ref, *rest):
        ex, orefs, acc = rest[:n_e], rest[n_e + len(behind):n_e + len(behind) + n_o], rest[-1]
        kk = pl.program_id(2)

        @pl.when(kk == 0)
        def _():
            acc[...] = jnp.zeros_like(acc)

        acc[...] += _dot(a_ref[...], b_ref[...], dims)

        @pl.when(kk == nk - 1)
        def _():
            r = acc[...]
            res = epi(r, *[e[...] for e in ex]) if epi is not None else (r,)
            for o, v in zip(orefs, res):
                o[...] = v.astype(o.dtype)

    a_spec = pl.BlockSpec((bk, bm), lambda i, j, kk: (kk, i)) if ta else pl.BlockSpec((bm, bk), lambda i, j, kk: (i, kk))
    if b_chips and tb:
        per = k_cap // bk
        b_spec = pl.BlockSpec((None, bn, bk), lambda i, j, kk: (kk // per, j, kk % per))
    elif b_chips:
        per = n_cap // bn
        b_spec = pl.BlockSpec((None, bk, bn), lambda i, j, kk: (j // per, kk, j % per))
    else:
        first = 0 if b_cols is None else b_cols[0] // bn
        assert b_cols is None or (not tb and b_cols[0] % bn == 0)
        b_spec = pl.BlockSpec((bn, bk), lambda i, j, kk: (j, kk)) if tb else pl.BlockSpec((bk, bn), lambda i, j, kk: (kk, first + j))
    t_spec = pl.BlockSpec((bm, bn), lambda i, j, kk: (i, j))
    o_spec, o_shape = t_spec, (m, n)
    if out_chips:
        per_o = n_cap // bn
        o_spec, o_shape = pl.BlockSpec((None, bm, bn), lambda i, j, kk: (j // per_o, i, j % per_o)), (out_chips, m, n_cap)
    est = (_nbytes((bm, bk), a.dtype) + _nbytes((bk, bn), b.dtype) + sum(_nbytes((bm, bn), e.dtype) for e in extra)
           + sum(_nbytes((bm, bn), o) for o in outs)) * 2 + 2 * _nbytes((bm, bn), F32)
    res = pl.pallas_call(
        body, grid=(m // bm, n // bn, nk), in_specs=[a_spec, b_spec] + [t_spec] * n_e + [pl.BlockSpec(memory_space=pl.ANY)] * len(behind),
        out_specs=[o_spec] * n_o, out_shape=[_SDS(o_shape, o) for o in outs], scratch_shapes=[pltpu.VMEM((bm, bn), F32)], name=name,
        compiler_params=_params(("parallel", "parallel", "arbitrary"), est))(_hbm(a), _hbm(b), *[_hbm(e) for e in extra], *behind)
    return res[0] if n_o == 1 else res


def _add_to(acc, r):
    return (acc + r,)


def _alibi_bias(dilation):
    slopes = 2.0 ** (-8.0 * (np.arange(_HEADS) + 1) / _HEADS)
    i = np.arange(_ABLK)[:, None]
    j = np.arange(_ABLK)[None, :]
    cur = np.where(i - j >= 0, -slopes[:, None, None] * ((i - j) * dilation), _NEG)
    prev = np.where(j >= i, -slopes[:, None, None] * ((i - j + _ABLK) * dilation), _NEG)
    both = np.concatenate([prev, cur], axis=2)
    return jnp.asarray(both.reshape(_HEADS // 2, 2 * _ABLK, 2 * _ABLK), F32)


def _strided(a, d):
    return a.reshape(a.shape[0] // d, d * a.shape[1])


def _head(h):
    return slice(h * _HDIM, (h + 1) * _HDIM)


def _pair(pr):
    return slice(pr * _LANES, (pr + 1) * _LANES)


def _low_lanes(shape):
    return lax.broadcasted_iota(jnp.int32, shape, 1) < _HDIM


def _halves(v, low):
    z = jnp.zeros_like(v)
    return jnp.where(low, v, z), jnp.where(low, z, v)


def _no_prev_mask(first):
    return jnp.logical_and(first, lax.broadcasted_iota(jnp.int32, (2 * _ABLK, 2 * _ABLK), 1) < _ABLK)


def _lane_spec(nb):
    return pl.BlockSpec((_ABLK, _LANES), lambda r, j: (jnp.minimum(j, nb - 1), r))


def _expand_heads(v):
    low = _low_lanes((v.shape[0], _LANES))
    return jnp.concatenate([jnp.where(low, v[:, 2 * pr:2 * pr + 1], v[:, 2 * pr + 1:2 * pr + 2]) for pr in range(_HEADS // 2)], axis=1)


def _attn_specs(nb, n_parts):
    def cur(p):
        return pl.BlockSpec((_ABLK, _AW), lambda r, j: (jnp.minimum(j, nb - 1), r * n_parts + p))

    def prev(p):
        return pl.BlockSpec((_ABLK, _AW), lambda r, j: (jnp.clip(j - 1, 0, nb - 1), r * n_parts + p))
    return cur, prev


def _attn_fwd(qkv, dilation, *, name):
    t = qkv.shape[0]
    nb = t // dilation // _ABLK
    bias = _alibi_bias(dilation)
    scale = _HDIM ** -0.5

    def body(q_ref, kc_ref, kp_ref, vc_ref, vp_ref, b_ref, o_ref, l_ref):
        no_prev = _no_prev_mask(pl.program_id(1) == 0)
        low = _low_lanes((_ABLK, _LANES))
        l_ref[...] = jnp.zeros_like(l_ref)
        for pr in range(_HEADS // 2):
            sl = _pair(pr)
            k2 = jnp.concatenate([kp_ref[:, sl], kc_ref[:, sl]], axis=0)
            v2 = jnp.concatenate([vp_ref[:, sl], vc_ref[:, sl]], axis=0)
            q2 = jnp.concatenate(_halves(q_ref[:, sl], low), axis=0)
            s = jnp.where(no_prev, _NEG, _dot(q2, k2, _NT) * scale + b_ref[pr])
            m = jnp.max(s, axis=-1, keepdims=True)
            p = jnp.exp(s - m)
            den = jnp.sum(p, axis=-1, keepdims=True)
            o = _dot(p, v2) / den
            lse = m + jnp.log(den)
            l_ref[:, 2 * pr:2 * pr + 1] = lse[:_ABLK]
            l_ref[:, 2 * pr + 1:2 * pr + 2] = lse[_ABLK:]
            o_ref[:, sl] = jnp.where(low, o[:_ABLK], o[_ABLK:]).astype(o_ref.dtype)

    cur, prev = _attn_specs(nb, 3)
    cur1, _ = _attn_specs(nb, 1)
    bspec = pl.BlockSpec((_HEADS // 2, 2 * _ABLK, 2 * _ABLK), lambda r, j: (0, 0, 0))
    sv = _strided(qkv, dilation)
    o, l = pl.pallas_call(
        body, grid=(dilation, nb), in_specs=[cur(0), cur(1), prev(1), cur(2), prev(2), bspec],
        out_specs=[cur1(0), _lane_spec(nb)],
        out_shape=[_SDS((t // dilation, dilation * _AW), _BF), _SDS((t // dilation, dilation * _LANES), F32)], name=name,
        compiler_params=_params(("parallel", "arbitrary"), 16 << 20))(sv, sv, sv, sv, sv, bias)
    return o, l.reshape(t, _LANES)


def _attn_bwd(qkv, do, ld, dilation, *, name):
    t = qkv.shape[0]
    nb = t // dilation // _ABLK
    bias = _alibi_bias(dilation)
    scale = _HDIM ** -0.5

    def body(q_ref, kc_ref, kp_ref, vc_ref, vp_ref, do_ref, ld_ref, b_ref, dq_ref, dk_ref, dv_ref, ck, cv):
        n = pl.program_id(1)

        @pl.when(n == 0)
        def _():
            ck[...] = jnp.zeros_like(ck)
            cv[...] = jnp.zeros_like(cv)

        @pl.when(n < nb)
        def _():
            low = _low_lanes((_ABLK, _LANES))
            no_prev = _no_prev_mask(n == 0)
            for pr in range(_HEADS // 2):
                sl = _pair(pr)
                k2 = jnp.concatenate([kp_ref[:, sl], kc_ref[:, sl]], axis=0)
                v2 = jnp.concatenate([vp_ref[:, sl], vc_ref[:, sl]], axis=0)
                q2 = jnp.concatenate(_halves(q_ref[:, sl], low), axis=0)
                do2 = jnp.concatenate(_halves(do_ref[:, sl], low), axis=0)
                lrow = jnp.concatenate([ld_ref[:, 2 * pr:2 * pr + 1], ld_ref[:, 2 * pr + 1:2 * pr + 2]], axis=0)
                dsum = jnp.concatenate([ld_ref[:, _HEADS + 2 * pr:_HEADS + 2 * pr + 1],
                                        ld_ref[:, _HEADS + 2 * pr + 1:_HEADS + 2 * pr + 2]], axis=0)
                p = jnp.exp(jnp.where(no_prev, _NEG, _dot(q2, k2, _NT) * scale + b_ref[pr]) - lrow)
                ds = (p * (_dot(do2, v2, _NT) - dsum)).astype(_BF)
                dq = _dot(ds, k2)
                dk2, dv2 = _dot(ds, q2, _TN), _dot(p, do2, _TN)
                dq_ref[:, sl] = (jnp.where(low, dq[:_ABLK], dq[_ABLK:]) * scale).astype(dq_ref.dtype)
                dk_ref[:, sl] = (ck[:, sl] + dk2[:_ABLK] * scale).astype(dk_ref.dtype)
                dv_ref[:, sl] = (cv[:, sl] + dv2[:_ABLK]).astype(dv_ref.dtype)
                ck[:, sl] = dk2[_ABLK:] * scale
                cv[:, sl] = dv2[_ABLK:]

        @pl.when(n == nb)
        def _():
            dk_ref[...] = ck[...].astype(dk_ref.dtype)
            dv_ref[...] = cv[...].astype(dv_ref.dtype)

    cur, prev = _attn_specs(nb, 3)
    cur1, prev1 = _attn_specs(nb, 1)
    bspec = pl.BlockSpec((_HEADS // 2, 2 * _ABLK, 2 * _ABLK), lambda r, j: (0, 0, 0))
    sv, dov, ldv = _strided(qkv, dilation), do, _strided(ld, dilation)
    dqkv = pl.pallas_call(
        body, grid=(dilation, nb + 1),
        in_specs=[cur(0), cur(1), prev(1), cur(2), prev(2), cur1(0), _lane_spec(nb), bspec],
        out_specs=[cur1(0), prev1(0), prev1(0)], out_shape=[_SDS(dov.shape, _BF)] * 3, name=name,
        scratch_shapes=[pltpu.VMEM((_ABLK, _AW), F32)] * 2,
        compiler_params=_params(("parallel", "arbitrary"), 16 << 20))(sv, sv, sv, sv, sv, dov, ldv, bias)
    return dqkv


def _ssd_in_specs(ch):
    return dict(
        xs=pl.BlockSpec((_CHUNK, _AW), lambda c: (ch(c), 0)),
        bc=pl.BlockSpec((_CHUNK, 2 * _GROUPS * _NSTATE), lambda c: (ch(c), _AW // (2 * _GROUPS * _NSTATE))),
        lane=pl.BlockSpec((_CHUNK, _LANES), lambda c: (ch(c), 0)),
        arow=pl.BlockSpec((_HEADS, 1, _CHUNK), lambda c: (0, 0, ch(c))),
        st=pl.BlockSpec((1, _HEADS // 2, _NSTATE, _LANES), lambda c: (ch(c), 0, 0, 0)),
    )


def _decay(a_col, a_row):
    i0 = lax.broadcasted_iota(jnp.int32, (_CHUNK, _CHUNK), 0)
    i1 = lax.broadcasted_iota(jnp.int32, (_CHUNK, _CHUNK), 1)
    return jnp.where(i0 >= i1, jnp.exp(a_col - a_row), 0.0), jnp.where(i1 >= i0, jnp.exp(a_row - a_col), 0.0)


def _rsum(v):
    return jnp.sum(v, axis=-1, keepdims=True)


def _ssd_fwd(act, dt, acum, a_row, *, name):
    t = act.shape[0]
    nc = t // _CHUNK
    sp = _ssd_in_specs(lambda c: c)
    gw = _GROUPS * _NSTATE

    def body(xs_ref, bc_ref, dt_ref, ac_ref, ar_ref, y_ref, sall_ref, st):
        @pl.when(pl.program_id(0) == 0)
        def _():
            st[...] = jnp.zeros_like(st)

        low = _low_lanes((_CHUNK, _LANES))
        for g in range(_GROUPS):
            bg = bc_ref[:, g * _NSTATE:(g + 1) * _NSTATE]
            cg = bc_ref[:, gw + g * _NSTATE:gw + (g + 1) * _NSTATE].astype(_BF)
            cb = _dot(cg, bg, _NT)
            for pr in range(g * _HPG // 2, (g + 1) * _HPG // 2):
                ha, hb = 2 * pr, 2 * pr + 1
                a_a, a_b = ac_ref[:, ha:ha + 1], ac_ref[:, hb:hb + 1]
                x = (xs_ref[:, _pair(pr)] * jnp.where(low, dt_ref[:, ha:ha + 1], dt_ref[:, hb:hb + 1])).astype(_BF)
                lm_a, _ = _decay(a_a, ar_ref[ha])
                lm_b, _ = _decay(a_b, ar_ref[hb])
                sv = st[pr]
                sall_ref[0, pr] = sv
                yd = _dot(jnp.concatenate([cb * lm_a, cb * lm_b], axis=0), x)
                yd = jnp.where(low, yd[:_CHUNK], yd[_CHUNK:])
                y_ref[:, _pair(pr)] = yd + jnp.where(low, jnp.exp(a_a), jnp.exp(a_b)) * _dot(cg, sv)
                al_a, al_b = jnp.min(a_a, axis=0, keepdims=True), jnp.min(a_b, axis=0, keepdims=True)
                upd = _dot(jnp.concatenate([bg * jnp.exp(al_a - a_a), bg * jnp.exp(al_b - a_b)], axis=1), x, _TN)
                st[pr] = jnp.where(low, jnp.exp(al_a), jnp.exp(al_b)) * sv + jnp.where(low, upd[:_NSTATE], upd[_NSTATE:])

    return pl.pallas_call(
        body, grid=(nc,), in_specs=[sp['xs'], sp['bc'], sp['lane'], sp['lane'], sp['arow']],
        out_specs=[sp['xs'], sp['st']], out_shape=[_SDS((t, _AW), F32), _SDS((nc, _HEADS // 2, _NSTATE, _LANES), F32)],
        scratch_shapes=[pltpu.VMEM((_HEADS // 2, _NSTATE, _LANES), F32)], name=name,
        compiler_params=_params(("arbitrary",), 16 << 20))(act, act, dt, acum, a_row)


def _ssd_bwd(act, dt, acum, a_row, sall, dy, *, name):
    t = act.shape[0]
    nc = t // _CHUNK
    sp = _ssd_in_specs(lambda c: nc - 1 - c)
    gw = _GROUPS * _NSTATE

    def body(xs_ref, bc_ref, dt_ref, ac_ref, ar_ref, sall_ref, dy_ref, dxs_ref, dbc_ref, ddt_ref, da_ref, dst):
        @pl.when(pl.program_id(0) == 0)
        def _():
            dst[...] = jnp.zeros_like(dst)

        ddt_ref[...] = jnp.zeros_like(ddt_ref)
        da_ref[...] = jnp.zeros_like(da_ref)
        row = lax.broadcasted_iota(jnp.int32, (_CHUNK, 1), 0)
        low = _low_lanes((_CHUNK, _LANES))
        for g in range(_GROUPS):
            bg = bc_ref[:, g * _NSTATE:(g + 1) * _NSTATE]
            bgb = bg.astype(_BF)
            cg = bc_ref[:, gw + g * _NSTATE:gw + (g + 1) * _NSTATE].astype(_BF)
            cb, cbt = _dot(cg, bgb, _NT), _dot(bgb, cg, _NT)
            dcb = jnp.zeros((_CHUNK, _CHUNK), F32)
            dbg = jnp.zeros((_CHUNK, _NSTATE), F32)
            dcg = jnp.zeros((_CHUNK, _NSTATE), F32)
            for pr in range(g * _HPG // 2, (g + 1) * _HPG // 2):
                heads = (2 * pr, 2 * pr + 1)
                a_cols = [ac_ref[:, h:h + 1] for h in heads]
                dt_pair = jnp.where(low, dt_ref[:, heads[0]:heads[0] + 1], dt_ref[:, heads[1]:heads[1] + 1])
                xsv = xs_ref[:, _pair(pr)]
                x = xsv * dt_pair
                xb = x.astype(_BF)
                xhs = _halves(xb, low)
                dyv = dy_ref[:, _pair(pr)]
                dyb = dyv.astype(_BF)
                dyhs = _halves(dyb, low)
                sv, dsv = sall_ref[0, pr], dst[pr]
                svb, dsb = sv.astype(_BF), dsv.astype(_BF)
                a_lasts = [jnp.min(a, axis=0, keepdims=True) for a in a_cols]
                e_pair = jnp.where(low, jnp.exp(a_cols[0]), jnp.exp(a_cols[1]))
                el_pair = jnp.where(low, jnp.exp(a_lasts[0]), jnp.exp(a_lasts[1]))
                yo = e_pair * _dot(cg, svb)
                decays = [_decay(a_cols[i], ar_ref[h]) for i, h in enumerate(heads)]
                gms, gmts = [cb * lm for lm, _ in decays], [cbt * lmt for _, lmt in decays]
                w_cols = [jnp.exp(a_lasts[i] - a_cols[i]) for i in range(2)]
                x2, dy2 = jnp.concatenate(xhs, axis=0), jnp.concatenate(dyhs, axis=0)
                bwd = _dot(jnp.concatenate([bg * w_cols[0], bg * w_cols[1]], axis=0), dsb)
                dxg = _dot(jnp.concatenate(gms, axis=1), dyb, _TN)
                dg2, dgt2, xds2 = _dot(dy2, xb, _NT), _dot(x2, dyb, _NT), _dot(x2, dsb, _NT)
                das = []
                for i in range(2):
                    rows_i = slice(i * _CHUNK, (i + 1) * _CHUNK)
                    dcb = dcb + dg2[rows_i] * decays[i][0]
                    dbg = dbg + w_cols[i] * xds2[rows_i]
                    das.append(_rsum(dg2[rows_i] * gms[i]) - _rsum(dgt2[rows_i] * gmts[i]))
                bwd = jnp.where(low, bwd[:_CHUNK], bwd[_CHUNK:])
                dx = jnp.where(low, dxg[:_CHUNK], dxg[_CHUNK:]) + bwd
                edy = (e_pair * dyv).astype(_BF)
                dcg = dcg + _dot(edy, svb, _NT)
                zs, yos, sds, dts = (_halves(v, low) for v in (x * bwd, dyv * yo, sv * dsv, dx * xsv))
                for i, h in enumerate(heads):
                    z = _rsum(zs[i])
                    da_last = jnp.sum(z, axis=0, keepdims=True) + jnp.exp(a_lasts[i]) * jnp.sum(_rsum(sds[i]), axis=0, keepdims=True)
                    da_ref[:, h:h + 1] = das[i] + _rsum(yos[i]) - z + jnp.where(row == _CHUNK - 1, da_last, 0.0)
                    ddt_ref[:, h:h + 1] = _rsum(dts[i])
                dxs_ref[:, _pair(pr)] = dx * dt_pair
                dst[pr] = el_pair * dsv + _dot(cg, edy, _TN)
            dbc_ref[:, g * _NSTATE:(g + 1) * _NSTATE] = dbg + _dot(dcb, cg, _TN)
            dbc_ref[:, gw + g * _NSTATE:gw + (g + 1) * _NSTATE] = dcg + _dot(dcb, bgb)

    ch = lambda c: nc - 1 - c
    wide = pl.BlockSpec((_CHUNK, 2 * gw), lambda c: (ch(c), 0))
    return pl.pallas_call(
        body, grid=(nc,), in_specs=[sp['xs'], sp['bc'], sp['lane'], sp['lane'], sp['arow'], sp['st'], sp['xs']],
        out_specs=[sp['xs'], wide, sp['lane'], sp['lane']],
        out_shape=[_SDS((t, _AW), F32), _SDS((t, 2 * gw), F32), _SDS((t, _LANES), F32), _SDS((t, _LANES), F32)],
        scratch_shapes=[pltpu.VMEM((_HEADS // 2, _NSTATE, _LANES), F32)], name=name,
        compiler_params=_params(("arbitrary",), 16 << 20))(act, act, dt, acum, a_row, sall, dy)


def _scan_rows(v, reverse):
    r = lax.broadcasted_iota(jnp.int32, v.shape, 0)
    for s in (1, 2, 4, 8, 16, 32, 64):
        if reverse:
            v = v + jnp.where(r < _CHUNK - s, pltpu.roll(v, _CHUNK - s, 0), 0.0)
        else:
            v = v + jnp.where(r >= s, pltpu.roll(v, s, 0), 0.0)
    return v


def _softplus(x):
    return jnp.maximum(x, 0.0) + jnp.log(1.0 + jnp.exp(-jnp.abs(x)))


def _sigmoid(x):
    return 1.0 / (1.0 + jnp.exp(-x))


def _silu(x):
    return x * _sigmoid(x)


def _dsilu(x):
    s = _sigmoid(x)
    return s * (1.0 + x * (1.0 - s))


def _lanes(a):
    return jnp.pad(a, (0, _LANES - a.shape[0])).reshape(1, _LANES)


def _layer_fwd(x, p, l):
    cch = p['conv_w'].shape[1]
    sv = {}
    h1 = _rms_fwd(x, p['ln1_g'], name=f"ln1_fwd_{l}")
    qkv = _mm(h1, p['w_in'], b_cols=(0, 3 * _AW), outs=(_BF,), name=f"in_proj_qkv_{l}")
    xbc = _mm(h1, p['w_in'], b_cols=(3 * _AW, cch), name=f"in_proj_xbc_{l}")
    zdt = _mm(h1, p['w_in'], b_cols=(3 * _AW + cch, _AW + _LANES), name=f"in_proj_zdt_{l}")
    z, dt_raw = (zdt, _AW, 0), (zdt, _LANES, _AW // _LANES)

    outs = []
    for dil in _DILATIONS:
        outs += _attn_fwd(qkv, dil, name=f"attn_fwd_d{dil}_{l}")

    tile = 2 * _ABLK
    perms = [_perm(d, tile) for d in _DILATIONS[1:]]

    def combine(o1, l1, o2, l2, o3, l3, p2, p3):
        m = jnp.maximum(jnp.maximum(l1, l2), l3)
        e1, e2, e3 = jnp.exp(l1 - m), jnp.exp(l2 - m), jnp.exp(l3 - m)
        tot = e1 + e2 + e3
        mixed = sum(_expand_heads(e / tot) * o for e, o in ((e1, o1.astype(F32)), (e2, _unstride(o2, p2)), (e3, _unstride(o3, p3))))
        return mixed, m + jnp.log(tot)
    outs = [a if i % 2 or i == 0 else ("strided", a, _DILATIONS[i // 2]) for i, a in enumerate(outs)]
    attn, lse = _rows(combine, outs, perms, [(_AW, F32), (_LANES, F32)], tile=tile, name=f"attn_combine_{l}")
    attn_n = _rms_fwd(attn, p['attn_norm_g'], name=f"attn_norm_fwd_{l}")

    def conv(u0, before, w, b):
        u1, u2, u3 = _shifted(u0, before, True)
        return _silu(w[0:1] * u3 + w[1:2] * u2 + w[2:3] * u1 + w[3:4] * u0 + b)
    act = _rows(conv, [xbc], [p['conv_w'], p['conv_b'].reshape(1, cch)], [(cch, F32)], halos=[(xbc, -1)], tile=_tile_for(cch),
                name=f"conv_fwd_{l}")[0]

    def dtf(raw, bias, alog):
        dt = _softplus(raw + bias)
        return dt, _scan_rows(dt * -jnp.exp(alog), False)
    dt, acum = _rows(dtf, [dt_raw], [_lanes(p['dt_bias']), _lanes(p['a_log'])], [(_LANES, F32), (_LANES, F32)],
                     tile=_CHUNK, name=f"dt_fwd_{l}")
    a_row = acum[:, :_HEADS].T[:, None, :]
    y_ssd, sall = _ssd_fwd(act, dt, acum, a_row, name=f"ssd_fwd_{l}")
    dskip = jnp.repeat(p['d_skip'], _HDIM).reshape(1, _AW)
    xs = (act, _AW, 0)

    def gate(y, xs, z, dsk):
        return (y + dsk * xs) * _silu(z)
    y2 = _rows(gate, [y_ssd, xs, z], [dskip], [(_AW, F32)], tile=_tile_for(_AW), name=f"gate_fwd_{l}")[0]
    y_n = _rms_fwd(y2, p['ssd_norm_g'], groups=_GROUPS, name=f"ssd_norm_fwd_{l}")

    mix = jnp.concatenate([attn_n, y_n], axis=1)
    sv.update(x=x, h1=h1, qkv=qkv, zdt=zdt, xbc=xbc, attn=attn, lse=lse, act=act, dt=dt, acum=acum, a_row=a_row,
              sall=sall, y_ssd=y_ssd, dskip=dskip, y2=y2, mix=mix)
    return mix, sv


def _layer_fwd_mlp(p, sv, l):
    x2 = _mm(sv['mix'], p['w_out'], extra=(sv['x'],), epi=_add_to, name=f"out_proj_{l}")
    h2 = _rms_fwd(x2, p['ln2_g'], name=f"ln2_fwd_{l}")
    a = _mm(h2, p['w_mlp_in'], b_chips=_CHIPS, epi=lambda acc: (jnp.square(jnp.maximum(acc, 0.0)),), outs=(_BF,), name=f"mlp_in_{l}")
    x3 = _mm(a, p['w_mlp_out'], extra=(x2,), epi=_add_to, name=f"mlp_out_{l}")
    sv.update(x2=x2, h2=h2, a=a)
    return x3


def _layer_bwd(dx3, p, sv, l, send, after):
    cch = p['conv_w'].shape[1]
    g = {}
    dx3b = dx3.astype(_BF)
    du = _mm(dx3b, p['w_mlp_out'], tb=True, extra=(sv['a'],), outs=(_BF,), after=after,
             epi=lambda acc, a: (acc * 2.0 * jnp.sqrt(a.astype(F32)),), name=f"mlp_out_dx_{l}")
    g['w_mlp_out'] = _mm(sv['a'], dx3b, ta=True, outs=(_BF,), name=f"mlp_out_dw_{l}")
    g['w_mlp_in'] = _mm(sv['h2'], du, ta=True, out_chips=_CHIPS, outs=(_BF,), name=f"mlp_in_dw_{l}")
    sent = send(('w_mlp_out', 'w_mlp_in'), g)
    dh2 = _mm(du, p['w_mlp_in'], tb=True, b_chips=_CHIPS, after=sent, name=f"mlp_in_dx_{l}")
    dx2, g['ln2_g'] = _rms_bwd(sv['x2'], dh2, p['ln2_g'], dx3, name=f"ln2_bwd_{l}")
    dx2b = dx2.astype(_BF)
    dmix = _mm(dx2b, p['w_out'], tb=True, name=f"out_proj_dx_{l}")
    g['w_out'] = _mm(sv['mix'], dx2b, ta=True, outs=(_BF,), name=f"out_proj_dw_{l}")

    tile = 2 * _ABLK
    perms = [_perm(d, tile) for d in _DILATIONS[1:]]

    def norm_bwd(attn, dy, lse, gn, p2, p3):
        dattn, dgn = _rms_bwd_tile(attn, dy, gn, 1)
        prod, low = dattn * attn, _low_lanes((attn.shape[0], _LANES))
        lane = lax.broadcasted_iota(jnp.int32, lse.shape, 1)
        ld = jnp.where(lane < _HEADS, lse, 0.0)
        for pr in range(_HEADS // 2):
            for i, part in enumerate(_halves(prod[:, _pair(pr)], low)):
                ld = jnp.where(lane == _HEADS + 2 * pr + i, _rsum(part), ld)
        return dattn, _stride(dattn, p2, _DILATIONS[1]), _stride(dattn, p3, _DILATIONS[2]), ld, dgn
    *dos, ld, gn_sum = _rows(norm_bwd, [sv['attn'], (dmix, _AW, 0), sv['lse']], [p['attn_norm_g'].reshape(1, _AW)] + perms,
                             [(_AW, _BF)] + [(_AW, _BF, d) for d in _DILATIONS[1:]] + [(_LANES, F32)], [_AW], tile=tile,
                             name=f"attn_norm_bwd_{l}")
    g['attn_norm_g'] = gn_sum.sum(axis=0)
    parts = [_attn_bwd(sv['qkv'], do, ld, dil, name=f"attn_bwd_d{dil}_{l}") for do, dil in zip(dos, _DILATIONS)]

    def branch_sum(*t):
        parts_, (p2, p3) = t[:9], t[9:]
        t = [a.astype(F32) for a in parts_[:3]] + [_unstride(a, p2) for a in parts_[3:6]] + [_unstride(a, p3) for a in parts_[6:]]
        return jnp.concatenate([t[i] + t[3 + i] + t[6 + i] for i in range(3)], axis=1)
    branch_ins = list(parts[0]) + [("strided", a, d) for pr, d in zip(parts[1:], _DILATIONS[1:]) for a in pr]
    w_all = 3 * _AW + cch + _AW + _LANES
    dproj = _rows(branch_sum, branch_ins, perms, [(3 * _AW, _BF)], into=(None, w_all, 0), tile=tile, name=f"attn_bwd_sum_{l}")[0]

    xs, z, dt_raw = (sv['act'], _AW, 0), (sv['zdt'], _AW, 0), (sv['zdt'], _LANES, _AW // _LANES)

    def gate_bwd(y2, dy, y, xs, z, dsk, gn):
        dy2, dgn = _rms_bwd_tile(y2, dy, gn, _GROUPS)
        dy1 = dy2 * _silu(z)
        return dy1, dsk * dy1, dy2 * (y + dsk * xs) * _dsilu(z), dy1 * xs, dgn
    dy1, dxs_skip, dz, dsk_sum, gn_sum = _rows(
        gate_bwd, [sv['y2'], (dmix, _AW, 1), sv['y_ssd'], xs, z], [sv['dskip'], p['ssd_norm_g'].reshape(1, _AW)],
        [(_AW, F32), (_AW, F32), (_AW, _BF)], [_AW, _AW], tile=128, name=f"gate_bwd_{l}")
    g['ssd_norm_g'] = gn_sum.sum(axis=0)
    g['d_skip'] = dsk_sum.sum(axis=0).reshape(_HEADS, _HDIM).sum(axis=1)
    dxs, dbc, ddt, da = _ssd_bwd(sv['act'], sv['dt'], sv['acum'], sv['a_row'], sv['sall'], dy1, name=f"ssd_bwd_{l}")

    def dtb(da, ddtx, raw, dt, dz, bias, alog):
        a = -jnp.exp(alog)
        dda = _scan_rows(da, True)
        draw = (dda * a + ddtx) * _sigmoid(raw + bias)
        return jnp.concatenate([dz, draw.astype(dz.dtype)], axis=1), draw, dda * dt * a
    dproj, dbias, dalog = _rows(dtb, [da, ddt, dt_raw, sv['dt'], dz], [_lanes(p['dt_bias']), _lanes(p['a_log'])],
                                [(_AW + _LANES, _BF)], [_LANES, _LANES], into=(dproj, w_all, (3 * _AW + cch) // (_AW + _LANES)),
                                tile=_CHUNK, name=f"dt_bwd_{l}")
    g['dt_bias'], g['a_log'] = dbias.sum(axis=0)[:_HEADS], dalog.sum(axis=0)[:_HEADS]
    def conv_bwd1(u0, dxs, dbc, dxk, before, w, b):
        u1, u2, u3 = _shifted(u0, before, True)
        pre = w[0:1] * u3 + w[1:2] * u2 + w[2:3] * u1 + w[3:4] * u0 + b
        dp = jnp.concatenate([dxs + dxk, dbc], axis=1) * _dsilu(pre)
        return dp, dp * u3, dp * u2, dp * u1, dp * u0, dp
    dpre, *dws = _rows(conv_bwd1, [sv['xbc'], dxs, dbc, dxs_skip], [p['conv_w'], p['conv_b'].reshape(1, cch)], [(cch, F32)],
                       [cch] * 5, halos=[(sv['xbc'], -1)], tile=128, name=f"conv_bwd_pre_{l}")
    g['conv_w'] = jnp.stack([dws[i].sum(axis=0) for i in range(_CONV_K)])
    g['conv_b'] = dws[4].sum(axis=0)

    def conv_bwd2(p0, after_, w):
        p1, p2, p3 = _shifted(p0, after_, False)
        return w[3:4] * p0 + w[2:3] * p1 + w[1:2] * p2 + w[0:1] * p3
    dproj = _rows(conv_bwd2, [dpre], [p['conv_w']], [(cch, _BF)], halos=[(dpre, 1)], into=(dproj, w_all, 3 * _AW // cch),
                  tile=_tile_for(cch), name=f"conv_bwd_in_{l}")[0]
    g_all = _mm(sv['h1'], dproj, ta=True, outs=(_BF,), name=f"in_proj_dw_{l}")
    z0 = 3 * _AW + cch
    g['w_in'] = jnp.concatenate([g_all[:, :3 * _AW], g_all[:, z0:z0 + _AW], g_all[:, 3 * _AW:z0], g_all[:, z0 + _AW:z0 + _AW + _HEADS]], axis=1)
    sent = send(('w_out', 'w_in'), g)
    for n in _BIG:
        del g[n]
    dh1 = _mm(dproj, p['w_in'], tb=True, after=sent, name=f"in_proj_dx_{l}")
    dx, g['ln1_g'] = _rms_bwd(sv['x'], dh1, p['ln1_g'], dx2, name=f"ln1_bwd_{l}")
    return dx, g


def _loss_bwd(x, g, tgt):
    w = x.shape[1]
    tile = _tile_for(w)

    def fn(x, tgt, g):
        r = _rstd(x)
        xh = x * r
        e = xh * g - tgt
        gd = e * (g / w)
        dx = r * (gd - xh * jnp.mean(gd * xh, axis=-1, keepdims=True))
        rowloss = 0.5 * jnp.mean(e * e, axis=-1, keepdims=True)
        return dx, (e / w) * xh, jnp.broadcast_to(rowloss, (tile, _LANES))
    dx, dg, ls = _rows(fn, [x, tgt], [g.reshape(1, w)], [(w, F32)], [w, _LANES], tile=tile, name="loss_head")
    return dx, dg.sum(axis=0), ls[:, 0].sum()


def _adamw_math(w, g, m, v):
    m2 = _B1 * m + (1.0 - _B1) * g
    v2 = _B2 * v + (1.0 - _B2) * jnp.square(g)
    m_hat = m2 / (1.0 - _B1 ** _STEP)
    v_hat = v2 / (1.0 - _B2 ** _STEP)
    return -_LR * (m_hat / (jnp.sqrt(v_hat) + _AEPS) + _WD * w), m2, v2


def _adamw(w, g, m, v, *, name):
    width = w.shape[-1]
    flat = [a.reshape(-1, width) for a in (w, g, m, v)]
    tile = _pick(flat[0].shape[0], (_tile_for(width), 32, 8))
    res = _rows(_adamw_math, flat, [], [(width, F32)] * 3, tile=tile, name=name)
    return [r.reshape(w.shape) for r in res]


_HBM = pl.BlockSpec(memory_space=pltpu.HBM)


def _place():
    x, y, c = lax.axis_index("x"), lax.axis_index("y"), lax.axis_index("c")
    other_chips = [(1 - x, y), (x, 1 - y), (1 - x, 1 - y)]
    return x, y, c, other_chips


def _remote(src, dst, sems, i, dev):
    return pltpu.make_async_remote_copy(src_ref=src, dst_ref=dst, send_sem=sems[0].at[i], recv_sem=sems[1].at[i],
                                        device_id=dev, device_id_type=_MESH)


def _exchange8(v, *, reduce, after=None, name):
    r, w = v.shape
    behind = [] if after is None else [after]

    def body(v_ref, *rest):
        all_ref, rest = rest[len(behind)], rest[len(behind) + 1:]
        sems = rest[-2:]
        x, y, c, _ = _place()
        me = 4 * x + 2 * y + c
        all_ref[me] = v_ref[...]
        flips = [((d >> 2) & 1, (d >> 1) & 1, d & 1) for d in range(1, 8)]
        sends = [_remote(v_ref, all_ref.at[me], sems, i, (x ^ fx, y ^ fy, c ^ fc)) for i, (fx, fy, fc) in enumerate(flips)]
        for cp in sends:
            cp.start()
        for i, (fx, fy, fc) in enumerate(flips):
            _remote(v_ref, all_ref.at[me ^ (4 * fx + 2 * fy + fc)], sems, i, (x ^ fx, y ^ fy, c ^ fc)).wait_recv()
        for cp in sends:
            cp.wait_send()
        if reduce:
            acc = all_ref[0]
            for s in range(1, 8):
                acc = acc + all_ref[s]
            rest[0][...] = acc

    vm = pl.BlockSpec(memory_space=pltpu.VMEM)
    out_shape = [_SDS((8, r, w), v.dtype)] + ([_SDS((r, w), v.dtype)] if reduce else [])
    res = pl.pallas_call(body, in_specs=[vm] + [_ANY] * len(behind), out_specs=[vm] * len(out_shape), out_shape=out_shape, name=name,
                         scratch_shapes=[pltpu.SemaphoreType.DMA((7,)), pltpu.SemaphoreType.DMA((7,))],
                         compiler_params=pltpu.CompilerParams(vmem_limit_bytes=int(32 << 20)))(v, *behind)
    return res[1] if reduce else res[0]


_SEM = pl.BlockSpec(memory_space=pltpu.SEMAPHORE)
_ANY = pl.BlockSpec(memory_space=pl.ANY)
_EFFECT = pltpu.SideEffectType.DATAFLOW_SIDE_EFFECTING


def _in_hbm(a):
    return pltpu.with_memory_space_constraint(a, pltpu.HBM)


def _send_start(name, srcs, land_shapes, plan, n_sends, after):
    ns, nl = len(srcs), len(land_shapes)
    zones = [_in_hbm(lax.empty(s.shape, s.dtype)) if isinstance(s, _SDS) else s for s in land_shapes]

    def body(*refs):
        ins, lands, sems = refs[:ns], refs[ns:ns + nl], refs[ns + nl + 1:ns + nl + 3]
        x, y, c, chips = _place()
        for i, (s, d, dev) in enumerate(plan(x, y, c, chips, ins, lands)[0]):
            _remote(s, d, sems, i, dev).start()
        refs[-1][...] = jnp.zeros_like(refs[-1])

    sem = pltpu.SemaphoreType.DMA((n_sends,))
    res = pl.pallas_call(
        body, name=name, in_specs=[_HBM] * (ns + nl) + [_ANY],
        out_shape=(sem, sem, *[pltpu.HBM(s.shape, s.dtype) for s in land_shapes], _SDS((8, _LANES), F32)),
        out_specs=(_SEM, _SEM, *[_HBM] * nl, pl.BlockSpec(memory_space=pltpu.VMEM)),
        input_output_aliases={ns + i: 2 + i for i in range(nl)},
        compiler_params=pltpu.CompilerParams(has_side_effects=_EFFECT))(
            *[_in_hbm(s) for s in srcs], *zones, after)
    return dict(sems=res[:2], srcs=srcs, lands=res[2:2 + nl], plan=plan), res[-1]


def _send_wait(name, h, after):
    ns, nl = len(h['srcs']), len(h['lands'])

    def body(*refs):
        ins, lands, sems = refs[:ns], refs[ns:ns + nl], refs[ns + nl:ns + nl + 2]
        x, y, c, chips = _place()
        sends, landings = h['plan'](x, y, c, chips, ins, lands)
        for i, (s, d, dev) in enumerate(sends):
            _remote(s, d, sems, i, dev).wait_send()
        for i, d in enumerate(landings):
            _remote(d, d, sems, i, sends[i][2]).wait_recv()

    return pl.pallas_call(
        body, name=name, in_specs=[_HBM] * (ns + nl) + [_SEM, _SEM, _ANY],
        out_shape=tuple(pltpu.HBM(a.shape, a.dtype) for a in h['lands']), out_specs=tuple([_HBM] * nl),
        input_output_aliases={ns + i: i for i in range(nl)},
        compiler_params=pltpu.CompilerParams(has_side_effects=_EFFECT))(
            *[_in_hbm(s) for s in h['srcs']], *h['lands'], *h['sems'], after)


def _gather_plan(items):
    def plan(x, y, c, chips, ins, lands):
        k = 2 * x + y
        sends = [(ins[si].at[l], lands[t].at[k], (px, py, c)) for t, (si, l) in enumerate(items) for px, py in chips]
        return sends, [lands[t].at[2 * px + py] for t in range(len(items)) for px, py in chips]
    return plan


_FLIPS = [((d >> 2) & 1, (d >> 1) & 1, d & 1) for d in range(1, 8)]


def _reduce_plan(halves):
    def plan(x, y, c, chips, ins, lands):
        sends, landings = [], []
        for t, hf in enumerate(halves):
            for i, (fx, fy, fc) in enumerate(_FLIPS):
                px, py, pc = x ^ fx, y ^ fy, c ^ fc
                sends.append((ins[t].at[2 * px + py, pl.ds(pc * hf, hf)], lands[t].at[i], (px, py, pc)))
                landings.append(lands[t].at[i])
        return sends, landings
    return plan


def _swap(name, srcs, out_shapes, plan, n_sends):
    n = len(srcs)

    def body(*refs):
        ins, outs, sems = refs[:n], refs[n:n + len(out_shapes)], refs[-2:]
        x, y, c, chips = _place()
        sends, landings = plan(x, y, c, chips, ins, outs)
        out = [_remote(s, d, sems, i, dev) for i, (s, d, dev) in enumerate(sends)]
        for cp in out:
            cp.start()
        for i, d in enumerate(landings):
            _remote(d, d, sems, i, sends[i][2]).wait_recv()
        for cp in out:
            cp.wait_send()

    return pl.pallas_call(
        body, in_specs=[_HBM] * n, out_specs=[_HBM] * len(out_shapes), out_shape=out_shapes, name=name,
        scratch_shapes=[pltpu.SemaphoreType.DMA((n_sends,)), pltpu.SemaphoreType.DMA((n_sends,))])(*srcs)


def _sum_owned(grads, landed, c, k, names):
    def sum8(*parts):
        acc = parts[0].astype(F32)
        for p in parts[1:]:
            acc = acc + p.astype(F32)
        return acc
    outs = []
    for g, got, name in zip(grads, landed, names):
        hf, b = got.shape[1:]
        own = lax.dynamic_slice_in_dim(lax.dynamic_index_in_dim(g, k, axis=0, keepdims=False), c * hf, hf, axis=0)
        outs.append(_rows(sum8, [own] + [got[i] for i in range(len(_FLIPS))], [], [(b, F32)], tile=_pick(hf, (_tile_for(b), 32)),
                          name=f"grad_sum_{name}")[0])
    return outs


def _share_halves(mine):
    n = len(mine)

    def plan(x, y, c_, chips, ins, outs):
        return [(ins[t], outs[t], (x, y, 1 - c_)) for t in range(n)], [outs[t] for t in range(n)]
    return _swap("grad_share_cores", mine, [_SDS(h.shape, F32) for h in mine], plan, n)


def _adamw_owned(w, mine, theirs, m, v, c, *, name):
    depth, a, b = w.shape
    half = a // 2
    tile = _pick(half, (_tile_for(b), 32, 8))
    nh = half // tile

    def blocks_of(l):
        return lambda i: (jnp.clip(i - 2 * nh * l, 0, 2 * nh - 1) % nh, 0)

    def fn(w, m, v, *rest):
        halves, cflag = rest[:-1], rest[-1]
        step = pl.program_id(0)
        is_mine = cflag[0:1, 0:1] == ((step // nh) % 2).astype(F32)
        g = jnp.where(is_mine, halves[0], halves[1])
        for l in range(1, depth):
            g = jnp.where(step >= 2 * nh * l, jnp.where(is_mine, halves[2 * l], halves[2 * l + 1]), g)
        return (g,) + _adamw_math(w, g, m, v)
    ins = [a_.reshape(depth * a, b) for a_ in (w, m, v)]
    ins += [(h, b, blocks_of(l)) for l in range(depth) for h in (mine[l], theirs[l])]
    res = _rows(fn, ins, [jnp.full((1, _LANES), c, F32)], [(b, F32)] * 4, tile=tile, name=name)
    return [r.reshape(w.shape) for r in res]


_BIG = ("w_in", "w_out", "w_mlp_in", "w_mlp_out")
_SMALL = ("ln1_g", "conv_b", "dt_bias", "a_log", "d_skip", "attn_norm_g", "ssd_norm_g", "ln2_g", "final_norm_g")
_ORDER = ("ln1_g", "w_in", "conv_w", "conv_b", "dt_bias", "a_log", "d_skip", "attn_norm_g", "ssd_norm_g", "w_out", "ln2_g",
          "w_mlp_in", "w_mlp_out", "final_norm_g")


def _pack(parts, rows):
    flat = jnp.concatenate([p.reshape(-1) for p in parts])
    return jnp.pad(flat, (0, rows * _LANES - flat.shape[0])).reshape(rows, _LANES)


def _unpack(buf, like):
    flat, out, o = buf.reshape(-1), [], 0
    for p in like:
        out.append(flat[o:o + p.size].reshape(p.shape))
        o += p.size
    return out


def kernel(x, ln1_g, w_in, conv_w, conv_b, dt_bias, a_log, d_skip, attn_norm_g, ssd_norm_g, w_out, ln2_g, w_mlp_in, w_mlp_out, final_norm_g, loss_target, m_ln1_g, m_w_in, m_conv_w, m_conv_b, m_dt_bias, m_a_log, m_d_skip, m_attn_norm_g, m_ssd_norm_g, m_w_out, m_ln2_g, m_w_mlp_in, m_w_mlp_out, m_final_norm_g, v_ln1_g, v_w_in, v_conv_w, v_conv_b, v_dt_bias, v_a_log, v_d_skip, v_attn_norm_g, v_ssd_norm_g, v_w_out, v_ln2_g, v_w_mlp_in, v_w_mlp_out, v_final_norm_g):
    w = dict(ln1_g=ln1_g, w_in=w_in, conv_w=conv_w, conv_b=conv_b, dt_bias=dt_bias, a_log=a_log, d_skip=d_skip,
             attn_norm_g=attn_norm_g, ssd_norm_g=ssd_norm_g, w_out=w_out, ln2_g=ln2_g, w_mlp_in=w_mlp_in, w_mlp_out=w_mlp_out,
             final_norm_g=final_norm_g)
    m = dict(ln1_g=m_ln1_g, w_in=m_w_in, conv_w=m_conv_w, conv_b=m_conv_b, dt_bias=m_dt_bias, a_log=m_a_log, d_skip=m_d_skip,
             attn_norm_g=m_attn_norm_g, ssd_norm_g=m_ssd_norm_g, w_out=m_w_out, ln2_g=m_ln2_g, w_mlp_in=m_w_mlp_in,
             w_mlp_out=m_w_mlp_out, final_norm_g=m_final_norm_g)
    v = dict(ln1_g=v_ln1_g, w_in=v_w_in, conv_w=v_conv_w, conv_b=v_conv_b, dt_bias=v_dt_bias, a_log=v_a_log, d_skip=v_d_skip,
             attn_norm_g=v_attn_norm_g, ssd_norm_g=v_ssd_norm_g, w_out=v_w_out, ln2_g=v_ln2_g, w_mlp_in=v_w_mlp_in,
             w_mlp_out=v_w_mlp_out, final_norm_g=v_final_norm_g)
    depth, d_model = ln1_g.shape
    n_chips = 4
    c = lax.axis_index("c")
    chip = 2 * lax.axis_index("x") + lax.axis_index("y")
    in_proj = w_in.shape[2] * n_chips
    cch = conv_w.shape[2] * n_chips
    zdt_pad = _LANES - _HEADS

    cw = _exchange8(conv_w.reshape(depth * _CONV_K, -1), reduce=False, name="gather_conv_w")[0::2]
    conv_full = cw.reshape(n_chips, depth, _CONV_K, -1).transpose(1, 2, 0, 3).reshape(depth, _CONV_K, cch)
    own = [w[n].astype(_BF) for n in _BIG]
    is_own = (jnp.arange(n_chips) == chip).reshape(n_chips, 1, 1)

    def start_gather(tag, items, after):
        lands = [_SDS((n_chips, *own[i].shape[1:]), _BF) for i, _ in items]
        return _send_start(f"gather_start_{tag}", own, lands, _gather_plan(items), 3 * len(items), after)

    def finish_gather(tag, handle, items, after):
        landed = _send_wait(f"gather_wait_{tag}", handle, after)
        return {_BIG[i]: jnp.where(is_own, own[i][l][None], g) for (i, l), g in zip(items, landed)}

    def layer_weights(l, blocks):
        p = {}
        if 'w_in' in blocks:
            full_in = blocks['w_in'].transpose(1, 0, 2).reshape(d_model, in_proj)
            p['w_in'] = jnp.concatenate([full_in[:, :3 * _AW], full_in[:, 4 * _AW:4 * _AW + cch], full_in[:, 3 * _AW:4 * _AW],
                                         full_in[:, 4 * _AW + cch:], jnp.zeros((d_model, zdt_pad), _BF)], axis=1)
        if 'w_out' in blocks:
            p['w_out'] = blocks['w_out'].reshape(-1, d_model)
            p['w_mlp_in'] = blocks['w_mlp_in']
            p['w_mlp_out'] = blocks['w_mlp_out'].reshape(-1, d_model)
        return p

    groups = dict(a=[(0, 0)], b=[(1, 0), (2, 0), (3, 0)], c=[(0, 1)], d=[(1, 1), (2, 1), (3, 1)])
    handles, token = {}, conv_full

    half_in = own[0].shape[1] // 2

    def rows_of(ref, who):
        return ref.at[pl.ds(who * half_in, half_in)]

    def plan_a(x_, y_, c_, chips, ins, lands):
        k = 2 * x_ + y_
        sends = [(rows_of(ins[0].at[0], c_), rows_of(lands[0].at[k], c_), (px, py, c_)) for px, py in chips]
        return sends, [rows_of(lands[0].at[2 * px + py], c_) for px, py in chips]

    def plan_pass(x_, y_, c_, chips, ins, lands):
        sends = [(rows_of(lands[0].at[2 * px + py], c_),) * 2 + ((x_, y_, 1 - c_),) for px, py in chips]
        return sends, [rows_of(lands[0].at[2 * px + py], 1 - c_) for px, py in chips]
    handles["a"], token = _send_start("gather_start_a", own[:1], [_SDS((n_chips, *own[0].shape[1:]), _BF)], plan_a, 3, token)
    for tag, items in list(groups.items())[1:]:
        handles[tag], token = start_gather(tag, items, token)
    landed = _send_wait("gather_land_a", handles["a"], token)
    handles["a"], token = _send_start("gather_pass_a", [], landed, plan_pass, 3, landed[0])
    layers = [{n: w[n][l] for n in _SMALL[:-1]} for l in range(depth)]
    for l in range(depth):
        layers[l]['conv_w'] = conv_full[l]

    layers[0].update(layer_weights(0, finish_gather("a", handles["a"], groups["a"], token)))
    mix, sv0 = _layer_fwd(x[0], layers[0], 0)
    layers[0].update(layer_weights(0, finish_gather("b", handles["b"], groups["b"], mix)))
    h = _layer_fwd_mlp(layers[0], sv0, 0)
    layers[1].update(layer_weights(1, finish_gather("c", handles["c"], groups["c"], h)))
    mix, sv1 = _layer_fwd(h, layers[1], 1)
    layers[1].update(layer_weights(1, finish_gather("d", handles["d"], groups["d"], mix)))
    h = _layer_fwd_mlp(layers[1], sv1, 1)
    saved = [sv0, sv1]

    def by_chip(g, name):
        if name == "w_mlp_in":
            return g
        if name == "w_in":
            return g.reshape(d_model, n_chips, -1).transpose(1, 0, 2)
        return g.reshape(n_chips, -1, d_model)

    pending = []

    def sender(l):
        def send(names, g):
            srcs = [by_chip(g[n], n) for n in names]
            halves = [s.shape[1] // 2 for s in srcs]
            lands = [_SDS((len(_FLIPS), hf, s.shape[2]), _BF) for s, hf in zip(srcs, halves)]
            handle, tok = _send_start(f"grad_start_{names[-1]}_{l}", srcs, lands, _reduce_plan(halves), len(_FLIPS) * len(srcs), srcs[0])
            pending.append((l, names, srcs, handle))
            return tok
        return send

    dx, g_final, loss_part = _loss_bwd(h, final_norm_g, loss_target[0])
    grads, after = [None] * depth, None
    for l in reversed(range(depth)):
        dx, grads[l] = _layer_bwd(dx, layers[l], saved[l], l, sender(l), after)
        after = dx
    owned = {}
    for l, names, srcs, handle in pending:
        landed = _send_wait(f"grad_wait_{names[-1]}_{l}", handle, dx)
        owned.update(zip([(n, l) for n in names], _sum_owned(srcs, landed, c, chip, [f"{n}_{l}" for n in names])))
    keys = [(n, l) for n in _BIG for l in range(depth)]
    theirs = dict(zip(keys, _share_halves([owned[k] for k in keys])))
    red, delta, new_m, new_v = {}, {}, {}, {}
    for n in _BIG:
        red[n], delta[n], new_m[n], new_v[n] = _adamw_owned(
            w[n], [owned[(n, l)] for l in range(depth)], [theirs[(n, l)] for l in range(depth)], m[n], v[n], c, name=f"adamw_{n}")

    small = {n: jnp.stack([grads[l][n] for l in range(depth)]) for n in _SMALL[:-1] + ("conv_w",)}
    small["final_norm_g"] = g_final
    parts = [loss_part.reshape(1)] + [small[n] for n in _SMALL + ("conv_w",)]
    rows = -(-sum(p.size for p in parts) // 1024) * 8
    tot = _unpack(_exchange8(_pack(parts, rows), reduce=True, after=red[_BIG[0]], name="allreduce_small"), parts)
    loss = tot[0][0]
    red.update(zip(_SMALL + ("conv_w",), tot[1:]))
    red["conv_w"] = lax.dynamic_index_in_dim(red["conv_w"].reshape(depth, _CONV_K, n_chips, -1), chip, axis=2, keepdims=False)

    names = _SMALL + ("conv_w",)
    like = [w[n] for n in names]
    srows = -(-sum(p.size for p in like) // 1024) * 8
    res = _adamw(*[_pack([d[n] for n in names], srows) for d in (w, red, m, v)], name="adamw_small")
    for dst, buf in zip((delta, new_m, new_v), res):
        dst.update(zip(names, _unpack(buf, like)))
    return (loss, dx[None], *[red[n] for n in _ORDER], *[delta[n] for n in _ORDER], *[new_m[n] for n in _ORDER],
            *[new_v[n] for n in _ORDER])
```

```python
import numpy as np
import jax
import jax.numpy as jnp
from jax import lax
from jax.experimental import pallas as pl
from jax.experimental.pallas import tpu as pltpu

F32 = jnp.float32
_BF = jnp.bfloat16
_NEG = -1e30
_EPS = 1e-5
_HEADS = 16
_HDIM = 64
_AW = _HEADS * _HDIM
_ABLK = 128
_DILATIONS = (1, 4, 16)
_CHUNK = 128
_NSTATE = 128
_GROUPS = 2
_HPG = _HEADS // _GROUPS
_CONV_K = 4
_LANES = 128
_CHIPS = 4
_LR, _B1, _B2, _AEPS, _WD, _STEP = 0.001, 0.9, 0.999, 1e-08, 0.01, 10
_VMEM_CAP = 56 * 1024 * 1024
_MESH = pl.DeviceIdType.MESH
_SDS = jax.ShapeDtypeStruct
_NT = (((1,), (1,)), ((), ()))
_TN = (((0,), (0,)), ((), ()))


def _params(sem, est_bytes):
    lim = int(min(max(2 * est_bytes + (4 << 20), 16 << 20), _VMEM_CAP))
    return pltpu.CompilerParams(dimension_semantics=sem, vmem_limit_bytes=lim)


def _nbytes(shape, dtype):
    return int(np.prod(shape)) * jnp.dtype(dtype).itemsize


def _hbm(a):
    return pltpu.with_memory_space_constraint(a, pltpu.HBM)


def _dot(a, b, dims=(((1,), (0,)), ((), ()))):
    return lax.dot_general(a.astype(_BF), b.astype(_BF), dims, preferred_element_type=F32)


_HALO = 8


def _rows(fn, ins, consts, outs, sums=(), *, halos=(), into=None, after=None, tile, name):
    rows = (ins[0][0] if isinstance(ins[0], tuple) else ins[0]).shape[0]
    n_steps = rows // tile

    def norm_in(a):
        if not isinstance(a, tuple):
            return a, tile, a.shape[1], lambda i: (i, 0)
        if isinstance(a[0], str):
            return a[1], tile // a[2], a[1].shape[1], lambda i: (i, 0)
        return a[0], tile, a[1], a[2] if callable(a[2]) else (lambda i, j=a[2]: (i, j))
    ins = [norm_in(a) for a in ins]
    outs = [(w, dt, d[0] if d else 1) for w, dt, *d in outs]
    n_in, n_h, n_c, n_o, n_s = len(ins), len(halos), len(consts), len(outs), len(sums)
    n_x = int(into is not None and into[0] is not None) + int(after is not None)

    def body(*refs):
        step = pl.program_id(0)
        vals = [r[...] for r in refs[:n_in]]
        for r, (_, side) in zip(refs[n_in:n_in + n_h], halos):
            vals.append(jnp.where(step == (0 if side < 0 else n_steps - 1), 0.0, r[...]))
        vals += [r[...] for r in refs[n_in + n_h:n_in + n_h + n_c]]
        refs = refs[:n_in] + refs[n_in + n_h:]
        res = fn(*vals)
        res = res if isinstance(res, tuple) else (res,)
        orefs = refs[n_in + n_c + n_x:n_in + n_c + n_x + n_o]
        srefs = refs[n_in + n_c + n_x + n_o:]
        for r, v in zip(orefs, res[:n_o]):
            r[...] = v.astype(r.dtype)
        if n_s:
            @pl.when(pl.program_id(0) == 0)
            def _():
                for r in srefs:
                    r[...] = jnp.zeros_like(r)
            for r, v in zip(srefs, res[n_o:]):
                r[...] += v.reshape(tile // 8, 8, v.shape[-1]).sum(axis=0)

    per = tile // _HALO
    in_specs = [pl.BlockSpec((r, w), idx) for _, r, w, idx in ins]
    in_specs += [pl.BlockSpec((_HALO, a.shape[1]), (lambda i: (jnp.maximum(i * per - 1, 0), 0)) if side < 0
                              else (lambda i: (jnp.minimum((i + 1) * per, rows // _HALO - 1), 0))) for a, side in halos]
    in_specs += [pl.BlockSpec(c.shape, lambda i, nd=c.ndim: (0,) * nd) for c in consts]
    out_shape = [_SDS((rows // d, d * w), dt) for w, dt, d in outs] + [_SDS((8, w), F32) for w in sums]
    out_specs = [pl.BlockSpec((tile // d, d * w), lambda i: (i, 0)) for w, _, d in outs]
    out_specs += [pl.BlockSpec((8, w), lambda i: (0, 0)) for w in sums]
    est = sum(_nbytes((r, w), a.dtype) for a, r, w, _ in ins) + sum(_nbytes((tile, w), dt) for w, dt, _ in outs)
    shared, aliases = [], {}
    if into is not None:
        buf, total, j = into
        out_shape[0] = _SDS((rows, total), outs[0][1])
        out_specs[0] = pl.BlockSpec((tile, outs[0][0]), lambda i: (i, j))
        if buf is not None:
            shared, aliases = [buf], {n_in + n_h + n_c: 0}
    if after is not None:
        shared.append(after)
    in_specs += [pl.BlockSpec(memory_space=pl.ANY)] * len(shared)
    return pl.pallas_call(body, grid=(n_steps,), in_specs=in_specs, out_specs=out_specs, out_shape=out_shape, name=name,
                          input_output_aliases=aliases, compiler_params=_params(("arbitrary",), 3 * est))(
                              *[_hbm(a[0]) for a in ins], *[_hbm(a) for a, _ in halos], *consts, *shared)


def _perm(d, tile):
    p = np.zeros((tile, tile), np.float32)
    t = np.arange(tile)
    p[t, (t % d) * (tile // d) + t // d] = 1.0
    return jnp.asarray(p, _BF)


def _unstride(s, p):
    d = p.shape[0] // s.shape[0]
    w = s.shape[1] // d
    return _dot(p, jnp.concatenate([s[:, r * w:(r + 1) * w] for r in range(d)], axis=0))


def _stride(x, p, d):
    z = _dot(p, x, _TN)
    n = x.shape[0] // d
    return jnp.concatenate([z[r * n:(r + 1) * n] for r in range(d)], axis=1)


def _shifted(u, halo, back):
    n = u.shape[0] + _HALO
    if back:
        ext = jnp.concatenate([halo, u], axis=0)
        return [pltpu.roll(ext, j, 0)[_HALO:] for j in (1, 2, 3)]
    ext = jnp.concatenate([u, halo], axis=0)
    return [pltpu.roll(ext, n - j, 0)[:u.shape[0]] for j in (1, 2, 3)]


def _tile_for(width):
    return max(c for c in (256, 128, 64, 32) if c * width <= (1 << 18) or c == 32)


def _rstd(x):
    return lax.rsqrt(jnp.mean(x * x, axis=-1, keepdims=True) + _EPS)


def _split(x, groups):
    w = x.shape[-1] // groups
    return [x[:, g * w:(g + 1) * w] for g in range(groups)]


def _cat(parts):
    return parts[0] if len(parts) == 1 else jnp.concatenate(parts, axis=-1)


def _rms_bwd_tile(x, dy, g, groups):
    dxs, dgs = [], []
    for xs, ds, gs in zip(_split(x, groups), _split(dy.astype(F32), groups), _split(g, groups)):
        r = _rstd(xs)
        xh = xs * r
        gd = ds * gs
        dxs.append(r * (gd - xh * jnp.mean(gd * xh, axis=-1, keepdims=True)))
        dgs.append(ds * xh)
    return _cat(dxs), _cat(dgs)


def _rms_fwd(x, g, *, groups=1, name):
    def fn(x, g):
        return _cat([xs * _rstd(xs) * gs for xs, gs in zip(_split(x, groups), _split(g, groups))])
    w = x.shape[1]
    return _rows(fn, [x], [g.reshape(1, w)], [(w, _BF)], tile=_tile_for(w), name=name)[0]


def _rms_bwd(x, dy, g, res=None, *, name):
    def fn(x, dy, *rest):
        dx, dg = _rms_bwd_tile(x, dy, rest[-1], 1)
        return (dx + rest[0] if res is not None else dx), dg
    w = x.shape[1]
    ins = [x, dy] + ([res] if res is not None else [])
    dx, dg = _rows(fn, ins, [g.reshape(1, w)], [(w, F32)], [w], tile=_tile_for(w), name=name)
    return dx, dg.sum(axis=0)


def _pick(n, cands):
    for c in cands:
        if n % c == 0:
            return c
    raise ValueError(f"no block size for {n}")


_MM_BLOCKS = (1024, 1152, 512, 384)


def _mm(a, b, *, ta=False, tb=False, extra=(), epi=None, outs=(F32,), after=None, b_chips=0, out_chips=0, b_cols=None, name):
    m, k = (a.shape[1], a.shape[0]) if ta else a.shape
    b_shape = (b.shape[1], b.shape[2] * b_chips) if b_chips else b.shape
    if b_cols is not None:
        b_shape = (b.shape[0], b_cols[1])
    n = b_shape[0] if tb else b_shape[1]
    assert k == (b_shape[1] if tb else b_shape[0])
    n_cap = n // max(out_chips, 1 if tb else b_chips, 1)
    k_cap = k // (b_chips if (b_chips and tb) else 1)
    bm, bn = _pick(m, _MM_BLOCKS), _pick(n_cap, _MM_BLOCKS)
    bk = _pick(k_cap, (2048, 1920) + _MM_BLOCKS)
    nk = k // bk
    n_e, n_o = len(extra), len(outs)
    behind = [] if after is None else [after]
    dims = (((0 if ta else 1,), (1 if tb else 0,)), ((), ()))

    def body(a_ref, b_ref, *rest):
        ex, orefs, acc = rest[:n_e], rest[n_e + len(behind):n_e + len(behind) + n_o], rest[-1]
        kk = pl.program_id(2)

        @pl.when(kk == 0)
        def _():
            acc[...] = jnp.zeros_like(acc)

        acc[...] += _dot(a_ref[...], b_ref[...], dims)

        @pl.when(kk == nk - 1)
        def _():
            r = acc[...]
            res = epi(r, *[e[...] for e in ex]) if epi is not None else (r,)
            for o, v in zip(orefs, res):
                o[...] = v.astype(o.dtype)

    a_spec = pl.BlockSpec((bk, bm), lambda i, j, kk: (kk, i)) if ta else pl.BlockSpec((bm, bk), lambda i, j, kk: (i, kk))
    if b_chips and tb:
        per = k_cap // bk
        b_spec = pl.BlockSpec((None, bn, bk), lambda i, j, kk: (kk // per, j, kk % per))
    elif b_chips:
        per = n_cap // bn
        b_spec = pl.BlockSpec((None, bk, bn), lambda i, j, kk: (j // per, kk, j % per))
    else:
        first = 0 if b_cols is None else b_cols[0] // bn
        assert b_cols is None or (not tb and b_cols[0] % bn == 0)
        b_spec = pl.BlockSpec((bn, bk), lambda i, j, kk: (j, kk)) if tb else pl.BlockSpec((bk, bn), lambda i, j, kk: (kk, first + j))
    t_spec = pl.BlockSpec((bm, bn), lambda i, j, kk: (i, j))
    o_spec, o_shape = t_spec, (m, n)
    if out_chips:
        per_o = n_cap // bn
        o_spec, o_shape = pl.BlockSpec((None, bm, bn), lambda i, j, kk: (j // per_o, i, j % per_o)), (out_chips, m, n_cap)
    est = (_nbytes((bm, bk), a.dtype) + _nbytes((bk, bn), b.dtype) + sum(_nbytes((bm, bn), e.dtype) for e in extra)
           + sum(_nbytes((bm, bn), o) for o in outs)) * 2 + 2 * _nbytes((bm, bn), F32)
    res = pl.pallas_call(
        body, grid=(m // bm, n // bn, nk), in_specs=[a_spec, b_spec] + [t_spec] * n_e + [pl.BlockSpec(memory_space=pl.ANY)] * len(behind),
        out_specs=[o_spec] * n_o, out_shape=[_SDS(o_shape, o) for o in outs], scratch_shapes=[pltpu.VMEM((bm, bn), F32)], name=name,
        compiler_params=_params(("parallel", "parallel", "arbitrary"), est))(_hbm(a), _hbm(b), *[_hbm(e) for e in extra], *behind)
    return res[0] if n_o == 1 else res


def _add_to(acc, r):
    return (acc + r,)


def _alibi_bias(dilation):
    slopes = 2.0 ** (-8.0 * (np.arange(_HEADS) + 1) / _HEADS)
    i = np.arange(_ABLK)[:, None]
    j = np.arange(_ABLK)[None, :]
    cur = np.where(i - j >= 0, -slopes[:, None, None] * ((i - j) * dilation), _NEG)
    prev = np.where(j >= i, -slopes[:, None, None] * ((i - j + _ABLK) * dilation), _NEG)
    both = np.concatenate([prev, cur], axis=2)
    return jnp.asarray(both.reshape(_HEADS // 2, 2 * _ABLK, 2 * _ABLK), F32)


def _strided(a, d):
    return a.reshape(a.shape[0] // d, d * a.shape[1])


def _head(h):
    return slice(h * _HDIM, (h + 1) * _HDIM)


def _pair(pr):
    return slice(pr * _LANES, (pr + 1) * _LANES)


def _low_lanes(shape):
    return lax.broadcasted_iota(jnp.int32, shape, 1) < _HDIM


def _halves(v, low):
    z = jnp.zeros_like(v)
    return jnp.where(low, v, z), jnp.where(low, z, v)


def _no_prev_mask(first):
    return jnp.logical_and(first, lax.broadcasted_iota(jnp.int32, (2 * _ABLK, 2 * _ABLK), 1) < _ABLK)


def _lane_spec(nb):
    return pl.BlockSpec((_ABLK, _LANES), lambda r, j: (jnp.minimum(j, nb - 1), r))


def _expand_heads(v):
    low = _low_lanes((v.shape[0], _LANES))
    return jnp.concatenate([jnp.where(low, v[:, 2 * pr:2 * pr + 1], v[:, 2 * pr + 1:2 * pr + 2]) for pr in range(_HEADS // 2)], axis=1)


def _attn_specs(nb, n_parts):
    def cur(p):
        return pl.BlockSpec((_ABLK, _AW), lambda r, j: (jnp.minimum(j, nb - 1), r * n_parts + p))

    def prev(p):
        return pl.BlockSpec((_ABLK, _AW), lambda r, j: (jnp.clip(j - 1, 0, nb - 1), r * n_parts + p))
    return cur, prev


def _attn_fwd(qkv, dilation, *, name):
    t = qkv.shape[0]
    nb = t // dilation // _ABLK
    bias = _alibi_bias(dilation)
    scale = _HDIM ** -0.5

    def body(q_ref, kc_ref, kp_ref, vc_ref, vp_ref, b_ref, o_ref, l_ref):
        no_prev = _no_prev_mask(pl.program_id(1) == 0)
        low = _low_lanes((_ABLK, _LANES))
        l_ref[...] = jnp.zeros_like(l_ref)
        for pr in range(_HEADS // 2):
            sl = _pair(pr)
            k2 = jnp.concatenate([kp_ref[:, sl], kc_ref[:, sl]], axis=0)
            v2 = jnp.concatenate([vp_ref[:, sl], vc_ref[:, sl]], axis=0)
            q2 = jnp.concatenate(_halves(q_ref[:, sl], low), axis=0)
            s = jnp.where(no_prev, _NEG, _dot(q2, k2, _NT) * scale + b_ref[pr])
            m = jnp.max(s, axis=-1, keepdims=True)
            p = jnp.exp(s - m)
            den = jnp.sum(p, axis=-1, keepdims=True)
            o = _dot(p, v2) / den
            lse = m + jnp.log(den)
            l_ref[:, 2 * pr:2 * pr + 1] = lse[:_ABLK]
            l_ref[:, 2 * pr + 1:2 * pr + 2] = lse[_ABLK:]
            o_ref[:, sl] = jnp.where(low, o[:_ABLK], o[_ABLK:]).astype(o_ref.dtype)

    cur, prev = _attn_specs(nb, 3)
    cur1, _ = _attn_specs(nb, 1)
    bspec = pl.BlockSpec((_HEADS // 2, 2 * _ABLK, 2 * _ABLK), lambda r, j: (0, 0, 0))
    sv = _strided(qkv, dilation)
    o, l = pl.pallas_call(
        body, grid=(dilation, nb), in_specs=[cur(0), cur(1), prev(1), cur(2), prev(2), bspec],
        out_specs=[cur1(0), _lane_spec(nb)],
        out_shape=[_SDS((t // dilation, dilation * _AW), _BF), _SDS((t // dilation, dilation * _LANES), F32)], name=name,
        compiler_params=_params(("parallel", "arbitrary"), 16 << 20))(sv, sv, sv, sv, sv, bias)
    return o, l.reshape(t, _LANES)


def _attn_bwd(qkv, do, ld, dilation, *, name):
    t = qkv.shape[0]
    nb = t // dilation // _ABLK
    bias = _alibi_bias(dilation)
    scale = _HDIM ** -0.5

    def body(q_ref, kc_ref, kp_ref, vc_ref, vp_ref, do_ref, ld_ref, b_ref, dq_ref, dk_ref, dv_ref, ck, cv):
        n = pl.program_id(1)

        @pl.when(n == 0)
        def _():
            ck[...] = jnp.zeros_like(ck)
            cv[...] = jnp.zeros_like(cv)

        @pl.when(n < nb)
        def _():
            low = _low_lanes((_ABLK, _LANES))
            no_prev = _no_prev_mask(n == 0)
            for pr in range(_HEADS // 2):
                sl = _pair(pr)
                k2 = jnp.concatenate([kp_ref[:, sl], kc_ref[:, sl]], axis=0)
                v2 = jnp.concatenate([vp_ref[:, sl], vc_ref[:, sl]], axis=0)
                q2 = jnp.concatenate(_halves(q_ref[:, sl], low), axis=0)
                do2 = jnp.concatenate(_halves(do_ref[:, sl], low), axis=0)
                lrow = jnp.concatenate([ld_ref[:, 2 * pr:2 * pr + 1], ld_ref[:, 2 * pr + 1:2 * pr + 2]], axis=0)
                dsum = jnp.concatenate([ld_ref[:, _HEADS + 2 * pr:_HEADS + 2 * pr + 1],
                                        ld_ref[:, _HEADS + 2 * pr + 1:_HEADS + 2 * pr + 2]], axis=0)
                p = jnp.exp(jnp.where(no_prev, _NEG, _dot(q2, k2, _NT) * scale + b_ref[pr]) - lrow)
                ds = (p * (_dot(do2, v2, _NT) - dsum)).astype(_BF)
                dq = _dot(ds, k2)
                dk2, dv2 = _dot(ds, q2, _TN), _dot(p, do2, _TN)
                dq_ref[:, sl] = (jnp.where(low, dq[:_ABLK], dq[_ABLK:]) * scale).astype(dq_ref.dtype)
                dk_ref[:, sl] = (ck[:, sl] + dk2[:_ABLK] * scale).astype(dk_ref.dtype)
                dv_ref[:, sl] = (cv[:, sl] + dv2[:_ABLK]).astype(dv_ref.dtype)
                ck[:, sl] = dk2[_ABLK:] * scale
                cv[:, sl] = dv2[_ABLK:]

        @pl.when(n == nb)
        def _():
            dk_ref[...] = ck[...].astype(dk_ref.dtype)
            dv_ref[...] = cv[...].astype(dv_ref.dtype)

    cur, prev = _attn_specs(nb, 3)
    cur1, prev1 = _attn_specs(nb, 1)
    bspec = pl.BlockSpec((_HEADS // 2, 2 * _ABLK, 2 * _ABLK), lambda r, j: (0, 0, 0))
    sv, dov, ldv = _strided(qkv, dilation), do, _strided(ld, dilation)
    dqkv = pl.pallas_call(
        body, grid=(dilation, nb + 1),
        in_specs=[cur(0), cur(1), prev(1), cur(2), prev(2), cur1(0), _lane_spec(nb), bspec],
        out_specs=[cur1(0), prev1(0), prev1(0)], out_shape=[_SDS(dov.shape, _BF)] * 3, name=name,
        scratch_shapes=[pltpu.VMEM((_ABLK, _AW), F32)] * 2,
        compiler_params=_params(("parallel", "arbitrary"), 16 << 20))(sv, sv, sv, sv, sv, dov, ldv, bias)
    return dqkv


def _ssd_in_specs(ch):
    return dict(
        xs=pl.BlockSpec((_CHUNK, _AW), lambda c: (ch(c), 0)),
        bc=pl.BlockSpec((_CHUNK, 2 * _GROUPS * _NSTATE), lambda c: (ch(c), _AW // (2 * _GROUPS * _NSTATE))),
        lane=pl.BlockSpec((_CHUNK, _LANES), lambda c: (ch(c), 0)),
        arow=pl.BlockSpec((_HEADS, 1, _CHUNK), lambda c: (0, 0, ch(c))),
        st=pl.BlockSpec((1, _HEADS // 2, _NSTATE, _LANES), lambda c: (ch(c), 0, 0, 0)),
    )


def _decay(a_col, a_row):
    i0 = lax.broadcasted_iota(jnp.int32, (_CHUNK, _CHUNK), 0)
    i1 = lax.broadcasted_iota(jnp.int32, (_CHUNK, _CHUNK), 1)
    return jnp.where(i0 >= i1, jnp.exp(a_col - a_row), 0.0), jnp.where(i1 >= i0, jnp.exp(a_row - a_col), 0.0)


def _rsum(v):
    return jnp.sum(v, axis=-1, keepdims=True)


def _ssd_fwd(act, dt, acum, a_row, *, name):
    t = act.shape[0]
    nc = t // _CHUNK
    sp = _ssd_in_specs(lambda c: c)
    gw = _GROUPS * _NSTATE

    def body(xs_ref, bc_ref, dt_ref, ac_ref, ar_ref, y_ref, sall_ref, st):
        @pl.when(pl.program_id(0) == 0)
        def _():
            st[...] = jnp.zeros_like(st)

        low = _low_lanes((_CHUNK, _LANES))
        for g in range(_GROUPS):
            bg = bc_ref[:, g * _NSTATE:(g + 1) * _NSTATE]
            cg = bc_ref[:, gw + g * _NSTATE:gw + (g + 1) * _NSTATE].astype(_BF)
            cb = _dot(cg, bg, _NT)
            for pr in range(g * _HPG // 2, (g + 1) * _HPG // 2):
                ha, hb = 2 * pr, 2 * pr + 1
                a_a, a_b = ac_ref[:, ha:ha + 1], ac_ref[:, hb:hb + 1]
                x = (xs_ref[:, _pair(pr)] * jnp.where(low, dt_ref[:, ha:ha + 1], dt_ref[:, hb:hb + 1])).astype(_BF)
                lm_a, _ = _decay(a_a, ar_ref[ha])
                lm_b, _ = _decay(a_b, ar_ref[hb])
                sv = st[pr]
                sall_ref[0, pr] = sv
                yd = _dot(jnp.concatenate([cb * lm_a, cb * lm_b], axis=0), x)
                yd = jnp.where(low, yd[:_CHUNK], yd[_CHUNK:])
                y_ref[:, _pair(pr)] = yd + jnp.where(low, jnp.exp(a_a), jnp.exp(a_b)) * _dot(cg, sv)
                al_a, al_b = jnp.min(a_a, axis=0, keepdims=True), jnp.min(a_b, axis=0, keepdims=True)
                upd = _dot(jnp.concatenate([bg * jnp.exp(al_a - a_a), bg * jnp.exp(al_b - a_b)], axis=1), x, _TN)
                st[pr] = jnp.where(low, jnp.exp(al_a), jnp.exp(al_b)) * sv + jnp.where(low, upd[:_NSTATE], upd[_NSTATE:])

    return pl.pallas_call(
        body, grid=(nc,), in_specs=[sp['xs'], sp['bc'], sp['lane'], sp['lane'], sp['arow']],
        out_specs=[sp['xs'], sp['st']], out_shape=[_SDS((t, _AW), F32), _SDS((nc, _HEADS // 2, _NSTATE, _LANES), F32)],
        scratch_shapes=[pltpu.VMEM((_HEADS // 2, _NSTATE, _LANES), F32)], name=name,
        compiler_params=_params(("arbitrary",), 16 << 20))(act, act, dt, acum, a_row)


def _ssd_bwd(act, dt, acum, a_row, sall, dy, *, name):
    t = act.shape[0]
    nc = t // _CHUNK
    sp = _ssd_in_specs(lambda c: nc - 1 - c)
    gw = _GROUPS * _NSTATE

    def body(xs_ref, bc_ref, dt_ref, ac_ref, ar_ref, sall_ref, dy_ref, dxs_ref, dbc_ref, ddt_ref, da_ref, dst):
        @pl.when(pl.program_id(0) == 0)
        def _():
            dst[...] = jnp.zeros_like(dst)

        ddt_ref[...] = jnp.zeros_like(ddt_ref)
        da_ref[...] = jnp.zeros_like(da_ref)
        row = lax.broadcasted_iota(jnp.int32, (_CHUNK, 1), 0)
        low = _low_lanes((_CHUNK, _LANES))
        for g in range(_GROUPS):
            bg = bc_ref[:, g * _NSTATE:(g + 1) * _NSTATE]
            bgb = bg.astype(_BF)
            cg = bc_ref[:, gw + g * _NSTATE:gw + (g + 1) * _NSTATE].astype(_BF)
            cb, cbt = _dot(cg, bgb, _NT), _dot(bgb, cg, _NT)
            dcb = jnp.zeros((_CHUNK, _CHUNK), F32)
            dbg = jnp.zeros((_CHUNK, _NSTATE), F32)
            dcg = jnp.zeros((_CHUNK, _NSTATE), F32)
            for pr in range(g * _HPG // 2, (g + 1) * _HPG // 2):
                heads = (2 * pr, 2 * pr + 1)
                a_cols = [ac_ref[:, h:h + 1] for h in heads]
                dt_pair = jnp.where(low, dt_ref[:, heads[0]:heads[0] + 1], dt_ref[:, heads[1]:heads[1] + 1])
                xsv = xs_ref[:, _pair(pr)]
                x = xsv * dt_pair
                xb = x.astype(_BF)
                xhs = _halves(xb, low)
                dyv = dy_ref[:, _pair(pr)]
                dyb = dyv.astype(_BF)
                dyhs = _halves(dyb, low)
                sv, dsv = sall_ref[0, pr], dst[pr]
                svb, dsb = sv.astype(_BF), dsv.astype(_BF)
                a_lasts = [jnp.min(a, axis=0, keepdims=True) for a in a_cols]
                e_pair = jnp.where(low, jnp.exp(a_cols[0]), jnp.exp(a_cols[1]))
                el_pair = jnp.where(low, jnp.exp(a_lasts[0]), jnp.exp(a_lasts[1]))
                yo = e_pair * _dot(cg, svb)
                decays = [_decay(a_cols[i], ar_ref[h]) for i, h in enumerate(heads)]
                gms, gmts = [cb * lm for lm, _ in decays], [cbt * lmt for _, lmt in decays]
                w_cols = [jnp.exp(a_lasts[i] - a_cols[i]) for i in range(2)]
                x2, dy2 = jnp.concatenate(xhs, axis=0), jnp.concatenate(dyhs, axis=0)
                bwd = _dot(jnp.concatenate([bg * w_cols[0], bg * w_cols[1]], axis=0), dsb)
                dxg = _dot(jnp.concatenate(gms, axis=1), dyb, _TN)
                dg2, dgt2, xds2 = _dot(dy2, xb, _NT), _dot(x2, dyb, _NT), _dot(x2, dsb, _NT)
                das = []
                for i in range(2):
                    rows_i = slice(i * _CHUNK, (i + 1) * _CHUNK)
                    dcb = dcb + dg2[rows_i] * decays[i][0]
                    dbg = dbg + w_cols[i] * xds2[rows_i]
                    das.append(_rsum(dg2[rows_i] * gms[i]) - _rsum(dgt2[rows_i] * gmts[i]))
                bwd = jnp.where(low, bwd[:_CHUNK], bwd[_CHUNK:])
                dx = jnp.where(low, dxg[:_CHUNK], dxg[_CHUNK:]) + bwd
                edy = (e_pair * dyv).astype(_BF)
                dcg = dcg + _dot(edy, svb, _NT)
                zs, yos, sds, dts = (_halves(v, low) for v in (x * bwd, dyv * yo, sv * dsv, dx * xsv))
                for i, h in enumerate(heads):
                    z = _rsum(zs[i])
                    da_last = jnp.sum(z, axis=0, keepdims=True) + jnp.exp(a_lasts[i]) * jnp.sum(_rsum(sds[i]), axis=0, keepdims=True)
                    da_ref[:, h:h + 1] = das[i] + _rsum(yos[i]) - z + jnp.where(row == _CHUNK - 1, da_last, 0.0)
                    ddt_ref[:, h:h + 1] = _rsum(dts[i])
                dxs_ref[:, _pair(pr)] = dx * dt_pair
                dst[pr] = el_pair * dsv + _dot(cg, edy, _TN)
            dbc_ref[:, g * _NSTATE:(g + 1) * _NSTATE] = dbg + _dot(dcb, cg, _TN)
            dbc_ref[:, gw + g * _NSTATE:gw + (g + 1) * _NSTATE] = dcg + _dot(dcb, bgb)

    ch = lambda c: nc - 1 - c
    wide = pl.BlockSpec((_CHUNK, 2 * gw), lambda c: (ch(c), 0))
    return pl.pallas_call(
        body, grid=(nc,), in_specs=[sp['xs'], sp['bc'], sp['lane'], sp['lane'], sp['arow'], sp['st'], sp['xs']],
        out_specs=[sp['xs'], wide, sp['lane'], sp['lane']],
        out_shape=[_SDS((t, _AW), F32), _SDS((t, 2 * gw), F32), _SDS((t, _LANES), F32), _SDS((t, _LANES), F32)],
        scratch_shapes=[pltpu.VMEM((_HEADS // 2, _NSTATE, _LANES), F32)], name=name,
        compiler_params=_params(("arbitrary",), 16 << 20))(act, act, dt, acum, a_row, sall, dy)


def _scan_rows(v, reverse):
    r = lax.broadcasted_iota(jnp.int32, v.shape, 0)
    for s in (1, 2, 4, 8, 16, 32, 64):
        if reverse:
            v = v + jnp.where(r < _CHUNK - s, pltpu.roll(v, _CHUNK - s, 0), 0.0)
        else:
            v = v + jnp.where(r >= s, pltpu.roll(v, s, 0), 0.0)
    return v


def _softplus(x):
    return jnp.maximum(x, 0.0) + jnp.log(1.0 + jnp.exp(-jnp.abs(x)))


def _sigmoid(x):
    return 1.0 / (1.0 + jnp.exp(-x))


def _silu(x):
    return x * _sigmoid(x)


def _dsilu(x):
    s = _sigmoid(x)
    return s * (1.0 + x * (1.0 - s))


def _lanes(a):
    return jnp.pad(a, (0, _LANES - a.shape[0])).reshape(1, _LANES)


def _layer_fwd(x, p, l):
    cch = p['conv_w'].shape[1]
    sv = {}
    h1 = _rms_fwd(x, p['ln1_g'], name=f"ln1_fwd_{l}")
    qkv = _mm(h1, p['w_in'], b_cols=(0, 3 * _AW), outs=(_BF,), name=f"in_proj_qkv_{l}")
    xbc = _mm(h1, p['w_in'], b_cols=(3 * _AW, cch), name=f"in_proj_xbc_{l}")
    zdt = _mm(h1, p['w_in'], b_cols=(3 * _AW + cch, _AW + _LANES), name=f"in_proj_zdt_{l}")
    z, dt_raw = (zdt, _AW, 0), (zdt, _LANES, _AW // _LANES)

    outs = []
    for dil in _DILATIONS:
        outs += _attn_fwd(qkv, dil, name=f"attn_fwd_d{dil}_{l}")

    tile = 2 * _ABLK
    perms = [_perm(d, tile) for d in _DILATIONS[1:]]

    def combine(o1, l1, o2, l2, o3, l3, p2, p3):
        m = jnp.maximum(jnp.maximum(l1, l2), l3)
        e1, e2, e3 = jnp.exp(l1 - m), jnp.exp(l2 - m), jnp.exp(l3 - m)
        tot = e1 + e2 + e3
        mixed = sum(_expand_heads(e / tot) * o for e, o in ((e1, o1.astype(F32)), (e2, _unstride(o2, p2)), (e3, _unstride(o3, p3))))
        return mixed, m + jnp.log(tot)
    outs = [a if i % 2 or i == 0 else ("strided", a, _DILATIONS[i // 2]) for i, a in enumerate(outs)]
    attn, lse = _rows(combine, outs, perms, [(_AW, F32), (_LANES, F32)], tile=tile, name=f"attn_combine_{l}")
    attn_n = _rms_fwd(attn, p['attn_norm_g'], name=f"attn_norm_fwd_{l}")

    def conv(u0, before, w, b):
        u1, u2, u3 = _shifted(u0, before, True)
        return _silu(w[0:1] * u3 + w[1:2] * u2 + w[2:3] * u1 + w[3:4] * u0 + b)
    act = _rows(conv, [xbc], [p['conv_w'], p['conv_b'].reshape(1, cch)], [(cch, F32)], halos=[(xbc, -1)], tile=_tile_for(cch),
                name=f"conv_fwd_{l}")[0]

    def dtf(raw, bias, alog):
        dt = _softplus(raw + bias)
        return dt, _scan_rows(dt * -jnp.exp(alog), False)
    dt, acum = _rows(dtf, [dt_raw], [_lanes(p['dt_bias']), _lanes(p['a_log'])], [(_LANES, F32), (_LANES, F32)],
                     tile=_CHUNK, name=f"dt_fwd_{l}")
    a_row = acum[:, :_HEADS].T[:, None, :]
    y_ssd, sall = _ssd_fwd(act, dt, acum, a_row, name=f"ssd_fwd_{l}")
    dskip = jnp.repeat(p['d_skip'], _HDIM).reshape(1, _AW)
    xs = (act, _AW, 0)

    def gate(y, xs, z, dsk):
        return (y + dsk * xs) * _silu(z)
    y2 = _rows(gate, [y_ssd, xs, z], [dskip], [(_AW, F32)], tile=_tile_for(_AW), name=f"gate_fwd_{l}")[0]
    y_n = _rms_fwd(y2, p['ssd_norm_g'], groups=_GROUPS, name=f"ssd_norm_fwd_{l}")

    mix = jnp.concatenate([attn_n, y_n], axis=1)
    sv.update(x=x, h1=h1, qkv=qkv, zdt=zdt, xbc=xbc, attn=attn, lse=lse, act=act, dt=dt, acum=acum, a_row=a_row,
              sall=sall, y_ssd=y_ssd, dskip=dskip, y2=y2, mix=mix)
    return mix, sv


def _layer_fwd_mlp(p, sv, l):
    x2 = _mm(sv['mix'], p['w_out'], extra=(sv['x'],), epi=_add_to, name=f"out_proj_{l}")
    h2 = _rms_fwd(x2, p['ln2_g'], name=f"ln2_fwd_{l}")
    a = _mm(h2, p['w_mlp_in'], b_chips=_CHIPS, epi=lambda acc: (jnp.square(jnp.maximum(acc, 0.0)),), outs=(_BF,), name=f"mlp_in_{l}")
    x3 = _mm(a, p['w_mlp_out'], extra=(x2,), epi=_add_to, name=f"mlp_out_{l}")
    sv.update(x2=x2, h2=h2, a=a)
    return x3


def _layer_bwd(dx3, p, sv, l, send, after):
    cch = p['conv_w'].shape[1]
    g = {}
    dx3b = dx3.astype(_BF)
    du = _mm(dx3b, p['w_mlp_out'], tb=True, extra=(sv['a'],), outs=(_BF,), after=after,
             epi=lambda acc, a: (acc * 2.0 * jnp.sqrt(a.astype(F32)),), name=f"mlp_out_dx_{l}")
    g['w_mlp_out'] = _mm(sv['a'], dx3b, ta=True, outs=(_BF,), name=f"mlp_out_dw_{l}")
    g['w_mlp_in'] = _mm(sv['h2'], du, ta=True, out_chips=_CHIPS, outs=(_BF,), name=f"mlp_in_dw_{l}")
    sent = send(('w_mlp_out', 'w_mlp_in'), g)
    dh2 = _mm(du, p['w_mlp_in'], tb=True, b_chips=_CHIPS, after=sent, name=f"mlp_in_dx_{l}")
    dx2, g['ln2_g'] = _rms_bwd(sv['x2'], dh2, p['ln2_g'], dx3, name=f"ln2_bwd_{l}")
    dx2b = dx2.astype(_BF)
    dmix = _mm(dx2b, p['w_out'], tb=True, name=f"out_proj_dx_{l}")
    g['w_out'] = _mm(sv['mix'], dx2b, ta=True, outs=(_BF,), name=f"out_proj_dw_{l}")
    after_out = send(('w_out',), g)

    tile = 2 * _ABLK
    perms = [_perm(d, tile) for d in _DILATIONS[1:]]

    def norm_bwd(attn, dy, lse, gn, p2, p3):
        dattn, dgn = _rms_bwd_tile(attn, dy, gn, 1)
        prod, low = dattn * attn, _low_lanes((attn.shape[0], _LANES))
        lane = lax.broadcasted_iota(jnp.int32, lse.shape, 1)
        ld = jnp.where(lane < _HEADS, lse, 0.0)
        for pr in range(_HEADS // 2):
            for i, part in enumerate(_halves(prod[:, _pair(pr)], low)):
                ld = jnp.where(lane == _HEADS + 2 * pr + i, _rsum(part), ld)
        return dattn, _stride(dattn, p2, _DILATIONS[1]), _stride(dattn, p3, _DILATIONS[2]), ld, dgn
    *dos, ld, gn_sum = _rows(norm_bwd, [sv['attn'], (dmix, _AW, 0), sv['lse']], [p['attn_norm_g'].reshape(1, _AW)] + perms,
                             [(_AW, _BF)] + [(_AW, _BF, d) for d in _DILATIONS[1:]] + [(_LANES, F32)], [_AW], after=after_out,
                             tile=tile, name=f"attn_norm_bwd_{l}")
    g['attn_norm_g'] = gn_sum.sum(axis=0)
    parts = [_attn_bwd(sv['qkv'], do, ld, dil, name=f"attn_bwd_d{dil}_{l}") for do, dil in zip(dos, _DILATIONS)]

    def branch_sum(*t):
        parts_, (p2, p3) = t[:9], t[9:]
        t = [a.astype(F32) for a in parts_[:3]] + [_unstride(a, p2) for a in parts_[3:6]] + [_unstride(a, p3) for a in parts_[6:]]
        return jnp.concatenate([t[i] + t[3 + i] + t[6 + i] for i in range(3)], axis=1)
    branch_ins = list(parts[0]) + [("strided", a, d) for pr, d in zip(parts[1:], _DILATIONS[1:]) for a in pr]
    w_all = 3 * _AW + cch + _AW + _LANES
    dproj = _rows(branch_sum, branch_ins, perms, [(3 * _AW, _BF)], into=(None, w_all, 0), tile=tile, name=f"attn_bwd_sum_{l}")[0]

    xs, z, dt_raw = (sv['act'], _AW, 0), (sv['zdt'], _AW, 0), (sv['zdt'], _LANES, _AW // _LANES)

    def gate_bwd(y2, dy, y, xs, z, dsk, gn):
        dy2, dgn = _rms_bwd_tile(y2, dy, gn, _GROUPS)
        dy1 = dy2 * _silu(z)
        return dy1, dsk * dy1, dy2 * (y + dsk * xs) * _dsilu(z), dy1 * xs, dgn
    dy1, dxs_skip, dz, dsk_sum, gn_sum = _rows(
        gate_bwd, [sv['y2'], (dmix, _AW, 1), sv['y_ssd'], xs, z], [sv['dskip'], p['ssd_norm_g'].reshape(1, _AW)],
        [(_AW, F32), (_AW, F32), (_AW, _BF)], [_AW, _AW], tile=128, name=f"gate_bwd_{l}")
    g['ssd_norm_g'] = gn_sum.sum(axis=0)
    g['d_skip'] = dsk_sum.sum(axis=0).reshape(_HEADS, _HDIM).sum(axis=1)
    dxs, dbc, ddt, da = _ssd_bwd(sv['act'], sv['dt'], sv['acum'], sv['a_row'], sv['sall'], dy1, name=f"ssd_bwd_{l}")

    def dtb(da, ddtx, raw, dt, dz, bias, alog):
        a = -jnp.exp(alog)
        dda = _scan_rows(da, True)
        draw = (dda * a + ddtx) * _sigmoid(raw + bias)
        return jnp.concatenate([dz, draw.astype(dz.dtype)], axis=1), draw, dda * dt * a
    dproj, dbias, dalog = _rows(dtb, [da, ddt, dt_raw, sv['dt'], dz], [_lanes(p['dt_bias']), _lanes(p['a_log'])],
                                [(_AW + _LANES, _BF)], [_LANES, _LANES], into=(dproj, w_all, (3 * _AW + cch) // (_AW + _LANES)),
                                tile=_CHUNK, name=f"dt_bwd_{l}")
    g['dt_bias'], g['a_log'] = dbias.sum(axis=0)[:_HEADS], dalog.sum(axis=0)[:_HEADS]
    def conv_bwd1(u0, dxs, dbc, dxk, before, w, b):
        u1, u2, u3 = _shifted(u0, before, True)
        pre = w[0:1] * u3 + w[1:2] * u2 + w[2:3] * u1 + w[3:4] * u0 + b
        dp = jnp.concatenate([dxs + dxk, dbc], axis=1) * _dsilu(pre)
        return dp, dp * u3, dp * u2, dp * u1, dp * u0, dp
    dpre, *dws = _rows(conv_bwd1, [sv['xbc'], dxs, dbc, dxs_skip], [p['conv_w'], p['conv_b'].reshape(1, cch)], [(cch, F32)],
                       [cch] * 5, halos=[(sv['xbc'], -1)], tile=128, name=f"conv_bwd_pre_{l}")
    g['conv_w'] = jnp.stack([dws[i].sum(axis=0) for i in range(_CONV_K)])
    g['conv_b'] = dws[4].sum(axis=0)

    def conv_bwd2(p0, after_, w):
        p1, p2, p3 = _shifted(p0, after_, False)
        return w[3:4] * p0 + w[2:3] * p1 + w[1:2] * p2 + w[0:1] * p3
    dproj = _rows(conv_bwd2, [dpre], [p['conv_w']], [(cch, _BF)], halos=[(dpre, 1)], into=(dproj, w_all, 3 * _AW // cch),
                  tile=_tile_for(cch), name=f"conv_bwd_in_{l}")[0]
    g_all = _mm(sv['h1'], dproj, ta=True, outs=(_BF,), name=f"in_proj_dw_{l}")
    z0 = 3 * _AW + cch
    g['w_in'] = jnp.concatenate([g_all[:, :3 * _AW], g_all[:, z0:z0 + _AW], g_all[:, 3 * _AW:z0], g_all[:, z0 + _AW:z0 + _AW + _HEADS]], axis=1)
    sent = send(('w_in',), g)
    for n in _BIG:
        del g[n]
    dh1 = _mm(dproj, p['w_in'], tb=True, after=sent, name=f"in_proj_dx_{l}")
    dx, g['ln1_g'] = _rms_bwd(sv['x'], dh1, p['ln1_g'], dx2, name=f"ln1_bwd_{l}")
    return dx, g


def _loss_bwd(x, g, tgt):
    w = x.shape[1]
    tile = _tile_for(w)

    def fn(x, tgt, g):
        r = _rstd(x)
        xh = x * r
        e = xh * g - tgt
        gd = e * (g / w)
        dx = r * (gd - xh * jnp.mean(gd * xh, axis=-1, keepdims=True))
        rowloss = 0.5 * jnp.mean(e * e, axis=-1, keepdims=True)
        return dx, (e / w) * xh, jnp.broadcast_to(rowloss, (tile, _LANES))
    dx, dg, ls = _rows(fn, [x, tgt], [g.reshape(1, w)], [(w, F32)], [w, _LANES], tile=tile, name="loss_head")
    return dx, dg.sum(axis=0), ls[:, 0].sum()


def _adamw_math(w, g, m, v):
    m2 = _B1 * m + (1.0 - _B1) * g
    v2 = _B2 * v + (1.0 - _B2) * jnp.square(g)
    m_hat = m2 / (1.0 - _B1 ** _STEP)
    v_hat = v2 / (1.0 - _B2 ** _STEP)
    return -_LR * (m_hat / (jnp.sqrt(v_hat) + _AEPS) + _WD * w), m2, v2


def _adamw(w, g, m, v, *, name):
    width = w.shape[-1]
    flat = [a.reshape(-1, width) for a in (w, g, m, v)]
    tile = _pick(flat[0].shape[0], (_tile_for(width), 32, 8))
    res = _rows(_adamw_math, flat, [], [(width, F32)] * 3, tile=tile, name=name)
    return [r.reshape(w.shape) for r in res]


_HBM = pl.BlockSpec(memory_space=pltpu.HBM)


def _place():
    x, y, c = lax.axis_index("x"), lax.axis_index("y"), lax.axis_index("c")
    other_chips = [(1 - x, y), (x, 1 - y), (1 - x, 1 - y)]
    return x, y, c, other_chips


def _remote(src, dst, sems, i, dev):
    return pltpu.make_async_remote_copy(src_ref=src, dst_ref=dst, send_sem=sems[0].at[i], recv_sem=sems[1].at[i],
                                        device_id=dev, device_id_type=_MESH)


def _exchange8(v, *, reduce, after=None, name):
    r, w = v.shape
    behind = [] if after is None else [after]

    def body(v_ref, *rest):
        all_ref, rest = rest[len(behind)], rest[len(behind) + 1:]
        sems = rest[-2:]
        x, y, c, _ = _place()
        me = 4 * x + 2 * y + c
        all_ref[me] = v_ref[...]
        flips = [((d >> 2) & 1, (d >> 1) & 1, d & 1) for d in range(1, 8)]
        sends = [_remote(v_ref, all_ref.at[me], sems, i, (x ^ fx, y ^ fy, c ^ fc)) for i, (fx, fy, fc) in enumerate(flips)]
        for cp in sends:
            cp.start()
        for i, (fx, fy, fc) in enumerate(flips):
            _remote(v_ref, all_ref.at[me ^ (4 * fx + 2 * fy + fc)], sems, i, (x ^ fx, y ^ fy, c ^ fc)).wait_recv()
        for cp in sends:
            cp.wait_send()
        if reduce:
            acc = all_ref[0]
            for s in range(1, 8):
                acc = acc + all_ref[s]
            rest[0][...] = acc

    vm = pl.BlockSpec(memory_space=pltpu.VMEM)
    out_shape = [_SDS((8, r, w), v.dtype)] + ([_SDS((r, w), v.dtype)] if reduce else [])
    res = pl.pallas_call(body, in_specs=[vm] + [_ANY] * len(behind), out_specs=[vm] * len(out_shape), out_shape=out_shape, name=name,
                         scratch_shapes=[pltpu.SemaphoreType.DMA((7,)), pltpu.SemaphoreType.DMA((7,))],
                         compiler_params=pltpu.CompilerParams(vmem_limit_bytes=int(32 << 20)))(v, *behind)
    return res[1] if reduce else res[0]


_SEM = pl.BlockSpec(memory_space=pltpu.SEMAPHORE)
_ANY = pl.BlockSpec(memory_space=pl.ANY)
_EFFECT = pltpu.SideEffectType.DATAFLOW_SIDE_EFFECTING


def _in_hbm(a):
    return pltpu.with_memory_space_constraint(a, pltpu.HBM)


def _send_start(name, srcs, land_shapes, plan, n_sends, after):
    ns, nl = len(srcs), len(land_shapes)
    zones = [_in_hbm(lax.empty(s.shape, s.dtype)) if isinstance(s, _SDS) else s for s in land_shapes]

    def body(*refs):
        ins, lands, sems = refs[:ns], refs[ns:ns + nl], refs[ns + nl + 1:ns + nl + 3]
        x, y, c, chips = _place()
        for i, (s, d, dev) in enumerate(plan(x, y, c, chips, ins, lands)[0]):
            _remote(s, d, sems, i, dev).start()
        refs[-1][...] = jnp.zeros_like(refs[-1])

    sem = pltpu.SemaphoreType.DMA((n_sends,))
    res = pl.pallas_call(
        body, name=name, in_specs=[_HBM] * (ns + nl) + [_ANY],
        out_shape=(sem, sem, *[pltpu.HBM(s.shape, s.dtype) for s in land_shapes], _SDS((8, _LANES), F32)),
        out_specs=(_SEM, _SEM, *[_HBM] * nl, pl.BlockSpec(memory_space=pltpu.VMEM)),
        input_output_aliases={ns + i: 2 + i for i in range(nl)},
        compiler_params=pltpu.CompilerParams(has_side_effects=_EFFECT))(
            *[_in_hbm(s) for s in srcs], *zones, after)
    return dict(sems=res[:2], srcs=srcs, lands=res[2:2 + nl], plan=plan), res[-1]


def _send_wait(name, h, after):
    ns, nl = len(h['srcs']), len(h['lands'])

    def body(*refs):
        ins, lands, sems = refs[:ns], refs[ns:ns + nl], refs[ns + nl:ns + nl + 2]
        x, y, c, chips = _place()
        sends, landings = h['plan'](x, y, c, chips, ins, lands)
        for i, (s, d, dev) in enumerate(sends):
            _remote(s, d, sems, i, dev).wait_send()
        for i, d in enumerate(landings):
            _remote(d, d, sems, i, sends[i][2]).wait_recv()

    return pl.pallas_call(
        body, name=name, in_specs=[_HBM] * (ns + nl) + [_SEM, _SEM, _ANY],
        out_shape=tuple(pltpu.HBM(a.shape, a.dtype) for a in h['lands']), out_specs=tuple([_HBM] * nl),
        input_output_aliases={ns + i: i for i in range(nl)},
        compiler_params=pltpu.CompilerParams(has_side_effects=_EFFECT))(
            *[_in_hbm(s) for s in h['srcs']], *h['lands'], *h['sems'], after)


def _gather_plan(items):
    def plan(x, y, c, chips, ins, lands):
        k = 2 * x + y
        sends = [(ins[si].at[l], lands[t].at[k], (px, py, c)) for t, (si, l) in enumerate(items) for px, py in chips]
        return sends, [lands[t].at[2 * px + py] for t in range(len(items)) for px, py in chips]
    return plan


_FLIPS = [((d >> 2) & 1, (d >> 1) & 1, d & 1) for d in range(1, 8)]


def _reduce_plan(halves):
    def plan(x, y, c, chips, ins, lands):
        sends, landings = [], []
        for t, hf in enumerate(halves):
            for i, (fx, fy, fc) in enumerate(_FLIPS):
                px, py, pc = x ^ fx, y ^ fy, c ^ fc
                sends.append((ins[t].at[2 * px + py, pl.ds(pc * hf, hf)], lands[t].at[i], (px, py, pc)))
                landings.append(lands[t].at[i])
        return sends, landings
    return plan


def _swap(name, srcs, out_shapes, plan, n_sends):
    n = len(srcs)

    def body(*refs):
        ins, outs, sems = refs[:n], refs[n:n + len(out_shapes)], refs[-2:]
        x, y, c, chips = _place()
        sends, landings = plan(x, y, c, chips, ins, outs)
        out = [_remote(s, d, sems, i, dev) for i, (s, d, dev) in enumerate(sends)]
        for cp in out:
            cp.start()
        for i, d in enumerate(landings):
            _remote(d, d, sems, i, sends[i][2]).wait_recv()
        for cp in out:
            cp.wait_send()

    return pl.pallas_call(
        body, in_specs=[_HBM] * n, out_specs=[_HBM] * len(out_shapes), out_shape=out_shapes, name=name,
        scratch_shapes=[pltpu.SemaphoreType.DMA((n_sends,)), pltpu.SemaphoreType.DMA((n_sends,))])(*srcs)


def _sum_owned(grads, landed, c, k, names):
    def sum8(*parts):
        acc = parts[0].astype(F32)
        for p in parts[1:]:
            acc = acc + p.astype(F32)
        return acc
    outs = []
    for g, got, name in zip(grads, landed, names):
        hf, b = got.shape[1:]
        own = lax.dynamic_slice_in_dim(lax.dynamic_index_in_dim(g, k, axis=0, keepdims=False), c * hf, hf, axis=0)
        outs.append(_rows(sum8, [own] + [got[i] for i in range(len(_FLIPS))], [], [(b, F32)], tile=_pick(hf, (_tile_for(b), 32)),
                          name=f"grad_sum_{name}")[0])
    return outs


def _share_halves(mine):
    n = len(mine)

    def plan(x, y, c_, chips, ins, outs):
        return [(ins[t], outs[t], (x, y, 1 - c_)) for t in range(n)], [outs[t] for t in range(n)]
    return _swap("grad_share_cores", mine, [_SDS(h.shape, F32) for h in mine], plan, n)


def _adamw_owned(w, mine, theirs, m, v, c, *, name):
    depth, a, b = w.shape
    half = a // 2
    tile = _pick(half, (_tile_for(b), 32, 8))
    nh = half // tile

    def blocks_of(l):
        return lambda i: (jnp.clip(i - 2 * nh * l, 0, 2 * nh - 1) % nh, 0)

    def fn(w, m, v, *rest):
        halves, cflag = rest[:-1], rest[-1]
        step = pl.program_id(0)
        is_mine = cflag[0:1, 0:1] == ((step // nh) % 2).astype(F32)
        g = jnp.where(is_mine, halves[0], halves[1])
        for l in range(1, depth):
            g = jnp.where(step >= 2 * nh * l, jnp.where(is_mine, halves[2 * l], halves[2 * l + 1]), g)
        return (g,) + _adamw_math(w, g, m, v)
    ins = [a_.reshape(depth * a, b) for a_ in (w, m, v)]
    ins += [(h, b, blocks_of(l)) for l in range(depth) for h in (mine[l], theirs[l])]
    res = _rows(fn, ins, [jnp.full((1, _LANES), c, F32)], [(b, F32)] * 4, tile=tile, name=name)
    return [r.reshape(w.shape) for r in res]


_BIG = ("w_in", "w_out", "w_mlp_in", "w_mlp_out")
_SMALL = ("ln1_g", "conv_b", "dt_bias", "a_log", "d_skip", "attn_norm_g", "ssd_norm_g", "ln2_g", "final_norm_g")
_ORDER = ("ln1_g", "w_in", "conv_w", "conv_b", "dt_bias", "a_log", "d_skip", "attn_norm_g", "ssd_norm_g", "w_out", "ln2_g",
          "w_mlp_in", "w_mlp_out", "final_norm_g")


def _pack(parts, rows):
    flat = jnp.concatenate([p.reshape(-1) for p in parts])
    return jnp.pad(flat, (0, rows * _LANES - flat.shape[0])).reshape(rows, _LANES)


def _unpack(buf, like):
    flat, out, o = buf.reshape(-1), [], 0
    for p in like:
        out.append(flat[o:o + p.size].reshape(p.shape))
        o += p.size
    return out


def kernel(x, ln1_g, w_in, conv_w, conv_b, dt_bias, a_log, d_skip, attn_norm_g, ssd_norm_g, w_out, ln2_g, w_mlp_in, w_mlp_out, final_norm_g, loss_target, m_ln1_g, m_w_in, m_conv_w, m_conv_b, m_dt_bias, m_a_log, m_d_skip, m_attn_norm_g, m_ssd_norm_g, m_w_out, m_ln2_g, m_w_mlp_in, m_w_mlp_out, m_final_norm_g, v_ln1_g, v_w_in, v_conv_w, v_conv_b, v_dt_bias, v_a_log, v_d_skip, v_attn_norm_g, v_ssd_norm_g, v_w_out, v_ln2_g, v_w_mlp_in, v_w_mlp_out, v_final_norm_g):
    w = dict(ln1_g=ln1_g, w_in=w_in, conv_w=conv_w, conv_b=conv_b, dt_bias=dt_bias, a_log=a_log, d_skip=d_skip,
             attn_norm_g=attn_norm_g, ssd_norm_g=ssd_norm_g, w_out=w_out, ln2_g=ln2_g, w_mlp_in=w_mlp_in, w_mlp_out=w_mlp_out,
             final_norm_g=final_norm_g)
    m = dict(ln1_g=m_ln1_g, w_in=m_w_in, conv_w=m_conv_w, conv_b=m_conv_b, dt_bias=m_dt_bias, a_log=m_a_log, d_skip=m_d_skip,
             attn_norm_g=m_attn_norm_g, ssd_norm_g=m_ssd_norm_g, w_out=m_w_out, ln2_g=m_ln2_g, w_mlp_in=m_w_mlp_in,
             w_mlp_out=m_w_mlp_out, final_norm_g=m_final_norm_g)
    v = dict(ln1_g=v_ln1_g, w_in=v_w_in, conv_w=v_conv_w, conv_b=v_conv_b, dt_bias=v_dt_bias, a_log=v_a_log, d_skip=v_d_skip,
             attn_norm_g=v_attn_norm_g, ssd_norm_g=v_ssd_norm_g, w_out=v_w_out, ln2_g=v_ln2_g, w_mlp_in=v_w_mlp_in,
             w_mlp_out=v_w_mlp_out, final_norm_g=v_final_norm_g)
    depth, d_model = ln1_g.shape
    n_chips = 4
    c = lax.axis_index("c")
    chip = 2 * lax.axis_index("x") + lax.axis_index("y")
    in_proj = w_in.shape[2] * n_chips
    cch = conv_w.shape[2] * n_chips
    zdt_pad = _LANES - _HEADS

    cw = _exchange8(conv_w.reshape(depth * _CONV_K, -1), reduce=False, name="gather_conv_w")[0::2]
    conv_full = cw.reshape(n_chips, depth, _CONV_K, -1).transpose(1, 2, 0, 3).reshape(depth, _CONV_K, cch)
    own = [w[n].astype(_BF) for n in _BIG]
    is_own = (jnp.arange(n_chips) == chip).reshape(n_chips, 1, 1)

    def start_gather(tag, items, after):
        lands = [_SDS((n_chips, *own[i].shape[1:]), _BF) for i, _ in items]
        return _send_start(f"gather_start_{tag}", own, lands, _gather_plan(items), 3 * len(items), after)

    def finish_gather(tag, handle, items, after):
        landed = _send_wait(f"gather_wait_{tag}", handle, after)
        return {_BIG[i]: jnp.where(is_own, own[i][l][None], g) for (i, l), g in zip(items, landed)}

    def layer_weights(l, blocks):
        p = {}
        if 'w_in' in blocks:
            full_in = blocks['w_in'].transpose(1, 0, 2).reshape(d_model, in_proj)
            p['w_in'] = jnp.concatenate([full_in[:, :3 * _AW], full_in[:, 4 * _AW:4 * _AW + cch], full_in[:, 3 * _AW:4 * _AW],
                                         full_in[:, 4 * _AW + cch:], jnp.zeros((d_model, zdt_pad), _BF)], axis=1)
        if 'w_out' in blocks:
            p['w_out'] = blocks['w_out'].reshape(-1, d_model)
            p['w_mlp_in'] = blocks['w_mlp_in']
            p['w_mlp_out'] = blocks['w_mlp_out'].reshape(-1, d_model)
        return p

    groups = dict(a=[(0, 0)], b=[(1, 0), (2, 0), (3, 0)], c=[(0, 1)], d=[(1, 1), (2, 1), (3, 1)])
    handles, token = {}, conv_full

    half_in = own[0].shape[1] // 2

    def rows_of(ref, who):
        return ref.at[pl.ds(who * half_in, half_in)]

    def plan_a(x_, y_, c_, chips, ins, lands):
        k = 2 * x_ + y_
        sends = [(rows_of(ins[0].at[0], c_), rows_of(lands[0].at[k], c_), (px, py, c_)) for px, py in chips]
        return sends, [rows_of(lands[0].at[2 * px + py], c_) for px, py in chips]

    def plan_pass(x_, y_, c_, chips, ins, lands):
        sends = [(rows_of(lands[0].at[2 * px + py], c_),) * 2 + ((x_, y_, 1 - c_),) for px, py in chips]
        return sends, [rows_of(lands[0].at[2 * px + py], 1 - c_) for px, py in chips]
    handles["a"], token = _send_start("gather_start_a", own[:1], [_SDS((n_chips, *own[0].shape[1:]), _BF)], plan_a, 3, token)
    for tag, items in list(groups.items())[1:]:
        handles[tag], token = start_gather(tag, items, token)
    landed = _send_wait("gather_land_a", handles["a"], token)
    handles["a"], token = _send_start("gather_pass_a", [], landed, plan_pass, 3, landed[0])
    layers = [{n: w[n][l] for n in _SMALL[:-1]} for l in range(depth)]
    for l in range(depth):
        layers[l]['conv_w'] = conv_full[l]

    layers[0].update(layer_weights(0, finish_gather("a", handles["a"], groups["a"], token)))
    mix, sv0 = _layer_fwd(x[0], layers[0], 0)
    layers[0].update(layer_weights(0, finish_gather("b", handles["b"], groups["b"], mix)))
    h = _layer_fwd_mlp(layers[0], sv0, 0)
    layers[1].update(layer_weights(1, finish_gather("c", handles["c"], groups["c"], h)))
    mix, sv1 = _layer_fwd(h, layers[1], 1)
    layers[1].update(layer_weights(1, finish_gather("d", handles["d"], groups["d"], mix)))
    h = _layer_fwd_mlp(layers[1], sv1, 1)
    saved = [sv0, sv1]

    def by_chip(g, name):
        if name == "w_mlp_in":
            return g
        if name == "w_in":
            return g.reshape(d_model, n_chips, -1).transpose(1, 0, 2)
        return g.reshape(n_chips, -1, d_model)

    pending = []

    def sender(l):
        def send(names, g):
            srcs = [by_chip(g[n], n) for n in names]
            halves = [s.shape[1] // 2 for s in srcs]
            lands = [_SDS((len(_FLIPS), hf, s.shape[2]), _BF) for s, hf in zip(srcs, halves)]
            handle, tok = _send_start(f"grad_start_{names[-1]}_{l}", srcs, lands, _reduce_plan(halves), len(_FLIPS) * len(srcs), srcs[0])
            pending.append((l, names, srcs, handle))
            return tok
        return send

    dx, g_final, loss_part = _loss_bwd(h, final_norm_g, loss_target[0])
    grads, after = [None] * depth, None
    for l in reversed(range(depth)):
        dx, grads[l] = _layer_bwd(dx, layers[l], saved[l], l, sender(l), after)
        after = dx
    owned = {}
    for l, names, srcs, handle in pending:
        landed = _send_wait(f"grad_wait_{names[-1]}_{l}", handle, dx)
        owned.update(zip([(n, l) for n in names], _sum_owned(srcs, landed, c, chip, [f"{n}_{l}" for n in names])))
    keys = [(n, l) for n in _BIG for l in range(depth)]
    theirs = dict(zip(keys, _share_halves([owned[k] for k in keys])))
    red, delta, new_m, new_v = {}, {}, {}, {}
    for n in _BIG:
        red[n], delta[n], new_m[n], new_v[n] = _adamw_owned(
            w[n], [owned[(n, l)] for l in range(depth)], [theirs[(n, l)] for l in range(depth)], m[n], v[n], c, name=f"adamw_{n}")

    small = {n: jnp.stack([grads[l][n] for l in range(depth)]) for n in _SMALL[:-1] + ("conv_w",)}
    small["final_norm_g"] = g_final
    parts = [loss_part.reshape(1)] + [small[n] for n in _SMALL + ("conv_w",)]
    rows = -(-sum(p.size for p in parts) // 1024) * 8
    tot = _unpack(_exchange8(_pack(parts, rows), reduce=True, after=red[_BIG[0]], name="allreduce_small"), parts)
    loss = tot[0][0]
    red.update(zip(_SMALL + ("conv_w",), tot[1:]))
    red["conv_w"] = lax.dynamic_index_in_dim(red["conv_w"].reshape(depth, _CONV_K, n_chips, -1), chip, axis=2, keepdims=False)

    names = _SMALL + ("conv_w",)
    like = [w[n] for n in names]
    srows = -(-sum(p.size for p in like) // 1024) * 8
    res = _adamw(*[_pack([d[n] for n in names], srows) for d in (w, red, m, v)], name="adamw_small")
    for dst, buf in zip((delta, new_m, new_v), res):
        dst.update(zip(names, _unpack(buf, like)))
    return (loss, dx[None], *[red[n] for n in _ORDER], *[delta[n] for n in _ORDER], *[new_m[n] for n in _ORDER],
            *[new_v[n] for n in _ORDER])
```

```python
import numpy as np
import jax
import jax.numpy as jnp
from jax import lax
from jax.experimental import pallas as pl
from jax.experimental.pallas import tpu as pltpu

F32 = jnp.float32
_BF = jnp.bfloat16
_NEG = -1e30
_EPS = 1e-5
_HEADS = 16
_HDIM = 64
_AW = _HEADS * _HDIM
_ABLK = 128
_DILATIONS = (1, 4, 16)
_CHUNK = 128
_NSTATE = 128
_GROUPS = 2
_HPG = _HEADS // _GROUPS
_CONV_K = 4
_LANES = 128
_CHIPS = 4
_LR, _B1, _B2, _AEPS, _WD, _STEP = 0.001, 0.9, 0.999, 1e-08, 0.01, 10
_VMEM_CAP = 56 * 1024 * 1024
_MESH = pl.DeviceIdType.MESH
_SDS = jax.ShapeDtypeStruct
_NT = (((1,), (1,)), ((), ()))
_TN = (((0,), (0,)), ((), ()))


def _params(sem, est_bytes):
    lim = int(min(max(2 * est_bytes + (4 << 20), 16 << 20), _VMEM_CAP))
    return pltpu.CompilerParams(dimension_semantics=sem, vmem_limit_bytes=lim)


def _nbytes(shape, dtype):
    return int(np.prod(shape)) * jnp.dtype(dtype).itemsize


def _hbm(a):
    return pltpu.with_memory_space_constraint(a, pltpu.HBM)


def _dot(a, b, dims=(((1,), (0,)), ((), ()))):
    return lax.dot_general(a.astype(_BF), b.astype(_BF), dims, preferred_element_type=F32)


_HALO = 8


def _rows(fn, ins, consts, outs, sums=(), *, halos=(), into=None, after=None, tile, name):
    rows = (ins[0][0] if isinstance(ins[0], tuple) else ins[0]).shape[0]
    n_steps = rows // tile

    def norm_in(a):
        if not isinstance(a, tuple):
            return a, tile, a.shape[1], lambda i: (i, 0)
        if isinstance(a[0], str):
            return a[1], tile // a[2], a[1].shape[1], lambda i: (i, 0)
        return a[0], tile, a[1], a[2] if callable(a[2]) else (lambda i, j=a[2]: (i, j))
    ins = [norm_in(a) for a in ins]
    outs = [(w, dt, d[0] if d else 1) for w, dt, *d in outs]
    n_in, n_h, n_c, n_o, n_s = len(ins), len(halos), len(consts), len(outs), len(sums)
    n_x = int(into is not None and into[0] is not None) + int(after is not None)

    def body(*refs):
        step = pl.program_id(0)
        vals = [r[...] for r in refs[:n_in]]
        for r, (_, side) in zip(refs[n_in:n_in + n_h], halos):
            vals.append(jnp.where(step == (0 if side < 0 else n_steps - 1), 0.0, r[...]))
        vals += [r[...] for r in refs[n_in + n_h:n_in + n_h + n_c]]
        refs = refs[:n_in] + refs[n_in + n_h:]
        res = fn(*vals)
        res = res if isinstance(res, tuple) else (res,)
        orefs = refs[n_in + n_c + n_x:n_in + n_c + n_x + n_o]
        srefs = refs[n_in + n_c + n_x + n_o:]
        for r, v in zip(orefs, res[:n_o]):
            r[...] = v.astype(r.dtype)
        if n_s:
            @pl.when(pl.program_id(0) == 0)
            def _():
                for r in srefs:
                    r[...] = jnp.zeros_like(r)
            for r, v in zip(srefs, res[n_o:]):
                r[...] += v.reshape(tile // 8, 8, v.shape[-1]).sum(axis=0)

    per = tile // _HALO
    in_specs = [pl.BlockSpec((r, w), idx) for _, r, w, idx in ins]
    in_specs += [pl.BlockSpec((_HALO, a.shape[1]), (lambda i: (jnp.maximum(i * per - 1, 0), 0)) if side < 0
                              else (lambda i: (jnp.minimum((i + 1) * per, rows // _HALO - 1), 0))) for a, side in halos]
    in_specs += [pl.BlockSpec(c.shape, lambda i, nd=c.ndim: (0,) * nd) for c in consts]
    out_shape = [_SDS((rows // d, d * w), dt) for w, dt, d in outs] + [_SDS((8, w), F32) for w in sums]
    out_specs = [pl.BlockSpec((tile // d, d * w), lambda i: (i, 0)) for w, _, d in outs]
    out_specs += [pl.BlockSpec((8, w), lambda i: (0, 0)) for w in sums]
    est = sum(_nbytes((r, w), a.dtype) for a, r, w, _ in ins) + sum(_nbytes((tile, w), dt) for w, dt, _ in outs)
    shared, aliases = [], {}
    if into is not None:
        buf, total, j = into
        out_shape[0] = _SDS((rows, total), outs[0][1])
        out_specs[0] = pl.BlockSpec((tile, outs[0][0]), lambda i: (i, j))
        if buf is not None:
            shared, aliases = [buf], {n_in + n_h + n_c: 0}
    if after is not None:
        shared.append(after)
    in_specs += [pl.BlockSpec(memory_space=pl.ANY)] * len(shared)
    return pl.pallas_call(body, grid=(n_steps,), in_specs=in_specs, out_specs=out_specs, out_shape=out_shape, name=name,
                          input_output_aliases=aliases, compiler_params=_params(("arbitrary",), 3 * est))(
                              *[_hbm(a[0]) for a in ins], *[_hbm(a) for a, _ in halos], *consts, *shared)


def _perm(d, tile):
    p = np.zeros((tile, tile), np.float32)
    t = np.arange(tile)
    p[t, (t % d) * (tile // d) + t // d] = 1.0
    return jnp.asarray(p, _BF)


def _unstride(s, p):
    d = p.shape[0] // s.shape[0]
    w = s.shape[1] // d
    return _dot(p, jnp.concatenate([s[:, r * w:(r + 1) * w] for r in range(d)], axis=0))


def _stride(x, p, d):
    z = _dot(p, x, _TN)
    n = x.shape[0] // d
    return jnp.concatenate([z[r * n:(r + 1) * n] for r in range(d)], axis=1)


def _shifted(u, halo, back):
    n = u.shape[0] + _HALO
    if back:
        ext = jnp.concatenate([halo, u], axis=0)
        return [pltpu.roll(ext, j, 0)[_HALO:] for j in (1, 2, 3)]
    ext = jnp.concatenate([u, halo], axis=0)
    return [pltpu.roll(ext, n - j, 0)[:u.shape[0]] for j in (1, 2, 3)]


def _tile_for(width):
    return max(c for c in (256, 128, 64, 32) if c * width <= (1 << 18) or c == 32)


def _rstd(x):
    return lax.rsqrt(jnp.mean(x * x, axis=-1, keepdims=True) + _EPS)


def _split(x, groups):
    w = x.shape[-1] // groups
    return [x[:, g * w:(g + 1) * w] for g in range(groups)]


def _cat(parts):
    return parts[0] if len(parts) == 1 else jnp.concatenate(parts, axis=-1)


def _rms_bwd_tile(x, dy, g, groups):
    dxs, dgs = [], []
    for xs, ds, gs in zip(_split(x, groups), _split(dy.astype(F32), groups), _split(g, groups)):
        r = _rstd(xs)
        xh = xs * r
        gd = ds * gs
        dxs.append(r * (gd - xh * jnp.mean(gd * xh, axis=-1, keepdims=True)))
        dgs.append(ds * xh)
    return _cat(dxs), _cat(dgs)


def _rms_fwd(x, g, *, groups=1, name):
    def fn(x, g):
        return _cat([xs * _rstd(xs) * gs for xs, gs in zip(_split(x, groups), _split(g, groups))])
    w = x.shape[1]
    return _rows(fn, [x], [g.reshape(1, w)], [(w, _BF)], tile=_tile_for(w), name=name)[0]


def _rms_bwd(x, dy, g, res=None, *, name):
    def fn(x, dy, *rest):
        dx, dg = _rms_bwd_tile(x, dy, rest[-1], 1)
        return (dx + rest[0] if res is not None else dx), dg
    w = x.shape[1]
    ins = [x, dy] + ([res] if res is not None else [])
    dx, dg = _rows(fn, ins, [g.reshape(1, w)], [(w, F32)], [w], tile=_tile_for(w), name=name)
    return dx, dg.sum(axis=0)


def _pick(n, cands):
    for c in cands:
        if n % c == 0:
            return c
    raise ValueError(f"no block size for {n}")


_MM_BLOCKS = (1024, 1152, 512, 384)


def _mm(a, b, *, ta=False, tb=False, extra=(), epi=None, outs=(F32,), after=None, b_chips=0, out_chips=0, b_cols=None, name):
    m, k = (a.shape[1], a.shape[0]) if ta else a.shape
    b_shape = (b.shape[1], b.shape[2] * b_chips) if b_chips else b.shape
    if b_cols is not None:
        b_shape = (b.shape[0], b_cols[1])
    n = b_shape[0] if tb else b_shape[1]
    assert k == (b_shape[1] if tb else b_shape[0])
    n_cap = n // max(out_chips, 1 if tb else b_chips, 1)
    k_cap = k // (b_chips if (b_chips and tb) else 1)
    bm, bn = _pick(m, _MM_BLOCKS), _pick(n_cap, _MM_BLOCKS)
    bk = _pick(k_cap, (2048, 1920) + _MM_BLOCKS)
    nk = k // bk
    n_e, n_o = len(extra), len(outs)
    behind = [] if after is None else [after]
    dims = (((0 if ta else 1,), (1 if tb else 0,)), ((), ()))

    def body(a_ref, b_ref, *rest):
        ex, orefs, acc = rest[:n_e], rest[n_e + len(behind):n_e + len(behind) + n_o], rest[-1]
        kk = pl.program_id(2)

        @pl.when(kk == 0)
        def _():
            acc[...] = jnp.zeros_like(acc)

        acc[...] += _dot(a_ref[...], b_ref[...], dims)

        @pl.when(kk == nk - 1)
        def _():
            r = acc[...]
            res = epi(r, *[e[...] for e in ex]) if epi is not None else (r,)
            for o, v in zip(orefs, res):
                o[...] = v.astype(o.dtype)

    a_spec = pl.BlockSpec((bk, bm), lambda i, j, kk: (kk, i)) if ta else pl.BlockSpec((bm, bk), lambda i, j, kk: (i, kk))
    if b_chips and tb:
        per = k_cap // bk
        b_spec = pl.BlockSpec((None, bn, bk), lambda i, j, kk: (kk // per, j, kk % per))
    elif b_chips:
        per = n_cap // bn
        b_spec = pl.BlockSpec((None, bk, bn), lambda i, j, kk: (j // per, kk, j % per))
    else:
        first = 0 if b_cols is None else b_cols[0] // bn
        assert b_cols is None or (not tb and b_cols[0] % bn == 0)
        b_spec = pl.BlockSpec((bn, bk), lambda i, j, kk: (j, kk)) if tb else pl.BlockSpec((bk, bn), lambda i, j, kk: (kk, first + j))
    t_spec = pl.BlockSpec((bm, bn), lambda i, j, kk: (i, j))
    o_spec, o_shape = t_spec, (m, n)
    if out_chips:
        per_o = n_cap // bn
        o_spec, o_shape = pl.BlockSpec((None, bm, bn), lambda i, j, kk: (j // per_o, i, j % per_o)), (out_chips, m, n_cap)
    est = (_nbytes((bm, bk), a.dtype) + _nbytes((bk, bn), b.dtype) + sum(_nbytes((bm, bn), e.dtype) for e in extra)
           + sum(_nbytes((bm, bn), o) for o in outs)) * 2 + 2 * _nbytes((bm, bn), F32)
    res = pl.pallas_call(
        body, grid=(m // bm, n // bn, nk), in_specs=[a_spec, b_spec] + [t_spec] * n_e + [pl.BlockSpec(memory_space=pl.ANY)] * len(behind),
        out_specs=[o_spec] * n_o, out_shape=[_SDS(o_shape, o) for o in outs], scratch_shapes=[pltpu.VMEM((bm, bn), F32)], name=name,
        compiler_params=_params(("parallel", "parallel", "arbitrary"), est))(_hbm(a), _hbm(b), *[_hbm(e) for e in extra], *behind)
    return res[0] if n_o == 1 else res


def _add_to(acc, r):
    return (acc + r,)


def _alibi_bias(dilation):
    slopes = 2.0 ** (-8.0 * (np.arange(_HEADS) + 1) / _HEADS)
    i = np.arange(_ABLK)[:, None]
    j = np.arange(_ABLK)[None, :]
    cur = np.where(i - j >= 0, -slopes[:, None, None] * ((i - j) * dilation), _NEG)
    prev = np.where(j >= i, -slopes[:, None, None] * ((i - j + _ABLK) * dilation), _NEG)
    both = np.concatenate([prev, cur], axis=2)
    return jnp.asarray(both.reshape(_HEADS // 2, 2 * _ABLK, 2 * _ABLK), F32)


def _strided(a, d):
    return a.reshape(a.shape[0] // d, d * a.shape[1])


def _head(h):
    return slice(h * _HDIM, (h + 1) * _HDIM)


def _pair(pr):
    return slice(pr * _LANES, (pr + 1) * _LANES)


def _low_lanes(shape):
    return lax.broadcasted_iota(jnp.int32, shape, 1) < _HDIM


def _halves(v, low):
    z = jnp.zeros_like(v)
    return jnp.where(low, v, z), jnp.where(low, z, v)


def _no_prev_mask(first):
    return jnp.logical_and(first, lax.broadcasted_iota(jnp.int32, (2 * _ABLK, 2 * _ABLK), 1) < _ABLK)


def _lane_spec(nb):
    return pl.BlockSpec((_ABLK, _LANES), lambda r, j: (jnp.minimum(j, nb - 1), r))


def _expand_heads(v):
    low = _low_lanes((v.shape[0], _LANES))
    return jnp.concatenate([jnp.where(low, v[:, 2 * pr:2 * pr + 1], v[:, 2 * pr + 1:2 * pr + 2]) for pr in range(_HEADS // 2)], axis=1)


def _attn_specs(nb, n_parts):
    def cur(p):
        return pl.BlockSpec((_ABLK, _AW), lambda r, j: (jnp.minimum(j, nb - 1), r * n_parts + p))

    def prev(p):
        return pl.BlockSpec((_ABLK, _AW), lambda r, j: (jnp.clip(j - 1, 0, nb - 1), r * n_parts + p))
    return cur, prev


def _attn_fwd(qkv, dilation, *, name):
    t = qkv.shape[0]
    nb = t // dilation // _ABLK
    bias = _alibi_bias(dilation)
    scale = _HDIM ** -0.5

    def body(q_ref, kc_ref, kp_ref, vc_ref, vp_ref, b_ref, o_ref, l_ref):
        no_prev = _no_prev_mask(pl.program_id(1) == 0)
        low = _low_lanes((_ABLK, _LANES))
        l_ref[...] = jnp.zeros_like(l_ref)
        for pr in range(_HEADS // 2):
            sl = _pair(pr)
            k2 = jnp.concatenate([kp_ref[:, sl], kc_ref[:, sl]], axis=0)
            v2 = jnp.concatenate([vp_ref[:, sl], vc_ref[:, sl]], axis=0)
            q2 = jnp.concatenate(_halves(q_ref[:, sl], low), axis=0)
            s = jnp.where(no_prev, _NEG, _dot(q2, k2, _NT) * scale + b_ref[pr])
            m = jnp.max(s, axis=-1, keepdims=True)
            p = jnp.exp(s - m)
            den = jnp.sum(p, axis=-1, keepdims=True)
            o = _dot(p, v2) / den
            lse = m + jnp.log(den)
            l_ref[:, 2 * pr:2 * pr + 1] = lse[:_ABLK]
            l_ref[:, 2 * pr + 1:2 * pr + 2] = lse[_ABLK:]
            o_ref[:, sl] = jnp.where(low, o[:_ABLK], o[_ABLK:]).astype(o_ref.dtype)

    cur, prev = _attn_specs(nb, 3)
    cur1, _ = _attn_specs(nb, 1)
    bspec = pl.BlockSpec((_HEADS // 2, 2 * _ABLK, 2 * _ABLK), lambda r, j: (0, 0, 0))
    sv = _hbm(_strided(qkv, dilation))
    o, l = pl.pallas_call(
        body, grid=(dilation, nb), in_specs=[cur(0), cur(1), prev(1), cur(2), prev(2), bspec],
        out_specs=[cur1(0), _lane_spec(nb)],
        out_shape=[_SDS((t // dilation, dilation * _AW), _BF), _SDS((t // dilation, dilation * _LANES), F32)], name=name,
        compiler_params=_params(("parallel", "arbitrary"), 16 << 20))(sv, sv, sv, sv, sv, bias)
    return o, l.reshape(t, _LANES)


def _attn_bwd(qkv, do, ld, dilation, *, name):
    t = qkv.shape[0]
    nb = t // dilation // _ABLK
    bias = _alibi_bias(dilation)
    scale = _HDIM ** -0.5

    def body(q_ref, kc_ref, kp_ref, vc_ref, vp_ref, do_ref, ld_ref, b_ref, dq_ref, dk_ref, dv_ref, ck, cv):
        n = pl.program_id(1)

        @pl.when(n == 0)
        def _():
            ck[...] = jnp.zeros_like(ck)
            cv[...] = jnp.zeros_like(cv)

        @pl.when(n < nb)
        def _():
            low = _low_lanes((_ABLK, _LANES))
            no_prev = _no_prev_mask(n == 0)
            for pr in range(_HEADS // 2):
                sl = _pair(pr)
                k2 = jnp.concatenate([kp_ref[:, sl], kc_ref[:, sl]], axis=0)
                v2 = jnp.concatenate([vp_ref[:, sl], vc_ref[:, sl]], axis=0)
                q2 = jnp.concatenate(_halves(q_ref[:, sl], low), axis=0)
                do2 = jnp.concatenate(_halves(do_ref[:, sl], low), axis=0)
                lrow = jnp.concatenate([ld_ref[:, 2 * pr:2 * pr + 1], ld_ref[:, 2 * pr + 1:2 * pr + 2]], axis=0)
                dsum = jnp.concatenate([ld_ref[:, _HEADS + 2 * pr:_HEADS + 2 * pr + 1],
                                        ld_ref[:, _HEADS + 2 * pr + 1:_HEADS + 2 * pr + 2]], axis=0)
                p = jnp.exp(jnp.where(no_prev, _NEG, _dot(q2, k2, _NT) * scale + b_ref[pr]) - lrow)
                ds = (p * (_dot(do2, v2, _NT) - dsum)).astype(_BF)
                dq = _dot(ds, k2)
                dk2, dv2 = _dot(ds, q2, _TN), _dot(p, do2, _TN)
                dq_ref[:, sl] = (jnp.where(low, dq[:_ABLK], dq[_ABLK:]) * scale).astype(dq_ref.dtype)
                dk_ref[:, sl] = (ck[:, sl] + dk2[:_ABLK] * scale).astype(dk_ref.dtype)
                dv_ref[:, sl] = (cv[:, sl] + dv2[:_ABLK]).astype(dv_ref.dtype)
                ck[:, sl] = dk2[_ABLK:] * scale
                cv[:, sl] = dv2[_ABLK:]

        @pl.when(n == nb)
        def _():
            dk_ref[...] = ck[...].astype(dk_ref.dtype)
            dv_ref[...] = cv[...].astype(dv_ref.dtype)

    cur, prev = _attn_specs(nb, 3)
    cur1, prev1 = _attn_specs(nb, 1)
    bspec = pl.BlockSpec((_HEADS // 2, 2 * _ABLK, 2 * _ABLK), lambda r, j: (0, 0, 0))
    sv, dov, ldv = _hbm(_strided(qkv, dilation)), _hbm(do), _hbm(_strided(ld, dilation))
    dqkv = pl.pallas_call(
        body, grid=(dilation, nb + 1),
        in_specs=[cur(0), cur(1), prev(1), cur(2), prev(2), cur1(0), _lane_spec(nb), bspec],
        out_specs=[cur1(0), prev1(0), prev1(0)], out_shape=[_SDS(dov.shape, _BF)] * 3, name=name,
        scratch_shapes=[pltpu.VMEM((_ABLK, _AW), F32)] * 2,
        compiler_params=_params(("parallel", "arbitrary"), 16 << 20))(sv, sv, sv, sv, sv, dov, ldv, bias)
    return dqkv


def _ssd_in_specs(ch):
    return dict(
        xs=pl.BlockSpec((_CHUNK, _AW), lambda c: (ch(c), 0)),
        bc=pl.BlockSpec((_CHUNK, 2 * _GROUPS * _NSTATE), lambda c: (ch(c), _AW // (2 * _GROUPS * _NSTATE))),
        lane=pl.BlockSpec((_CHUNK, _LANES), lambda c: (ch(c), 0)),
        arow=pl.BlockSpec((_HEADS, 1, _CHUNK), lambda c: (0, 0, ch(c))),
        st=pl.BlockSpec((1, _HEADS // 2, _NSTATE, _LANES), lambda c: (ch(c), 0, 0, 0)),
    )


def _decay(a_col, a_row):
    i0 = lax.broadcasted_iota(jnp.int32, (_CHUNK, _CHUNK), 0)
    i1 = lax.broadcasted_iota(jnp.int32, (_CHUNK, _CHUNK), 1)
    return jnp.where(i0 >= i1, jnp.exp(a_col - a_row), 0.0), jnp.where(i1 >= i0, jnp.exp(a_row - a_col), 0.0)


def _rsum(v):
    return jnp.sum(v, axis=-1, keepdims=True)


def _ssd_fwd(act, dt, acum, a_row, *, name):
    t = act.shape[0]
    nc = t // _CHUNK
    sp = _ssd_in_specs(lambda c: c)
    gw = _GROUPS * _NSTATE

    def body(xs_ref, bc_ref, dt_ref, ac_ref, ar_ref, y_ref, sall_ref, st):
        @pl.when(pl.program_id(0) == 0)
        def _():
            st[...] = jnp.zeros_like(st)

        low = _low_lanes((_CHUNK, _LANES))
        for g in range(_GROUPS):
            bg = bc_ref[:, g * _NSTATE:(g + 1) * _NSTATE]
            cg = bc_ref[:, gw + g * _NSTATE:gw + (g + 1) * _NSTATE].astype(_BF)
            cb = _dot(cg, bg, _NT)
            for pr in range(g * _HPG // 2, (g + 1) * _HPG // 2):
                ha, hb = 2 * pr, 2 * pr + 1
                a_a, a_b = ac_ref[:, ha:ha + 1], ac_ref[:, hb:hb + 1]
                x = (xs_ref[:, _pair(pr)] * jnp.where(low, dt_ref[:, ha:ha + 1], dt_ref[:, hb:hb + 1])).astype(_BF)
                lm_a, _ = _decay(a_a, ar_ref[ha])
                lm_b, _ = _decay(a_b, ar_ref[hb])
                sv = st[pr]
                sall_ref[0, pr] = sv
                yd = _dot(jnp.concatenate([cb * lm_a, cb * lm_b], axis=0), x)
                yd = jnp.where(low, yd[:_CHUNK], yd[_CHUNK:])
                y_ref[:, _pair(pr)] = yd + jnp.where(low, jnp.exp(a_a), jnp.exp(a_b)) * _dot(cg, sv)
                al_a, al_b = jnp.min(a_a, axis=0, keepdims=True), jnp.min(a_b, axis=0, keepdims=True)
                upd = _dot(jnp.concatenate([bg * jnp.exp(al_a - a_a), bg * jnp.exp(al_b - a_b)], axis=1), x, _TN)
                st[pr] = jnp.where(low, jnp.exp(al_a), jnp.exp(al_b)) * sv + jnp.where(low, upd[:_NSTATE], upd[_NSTATE:])

    return pl.pallas_call(
        body, grid=(nc,), in_specs=[sp['xs'], sp['bc'], sp['lane'], sp['lane'], sp['arow']],
        out_specs=[sp['xs'], sp['st']], out_shape=[_SDS((t, _AW), F32), _SDS((nc, _HEADS // 2, _NSTATE, _LANES), F32)],
        scratch_shapes=[pltpu.VMEM((_HEADS // 2, _NSTATE, _LANES), F32)], name=name,
        compiler_params=_params(("arbitrary",), 16 << 20))(*[_hbm(a) for a in (act, act, dt, acum, a_row)])


def _ssd_bwd(act, dt, acum, a_row, sall, dy, *, name):
    t = act.shape[0]
    nc = t // _CHUNK
    sp = _ssd_in_specs(lambda c: nc - 1 - c)
    gw = _GROUPS * _NSTATE

    def body(xs_ref, bc_ref, dt_ref, ac_ref, ar_ref, sall_ref, dy_ref, dxs_ref, dbc_ref, ddt_ref, da_ref, dst):
        @pl.when(pl.program_id(0) == 0)
        def _():
            dst[...] = jnp.zeros_like(dst)

        ddt_ref[...] = jnp.zeros_like(ddt_ref)
        da_ref[...] = jnp.zeros_like(da_ref)
        row = lax.broadcasted_iota(jnp.int32, (_CHUNK, 1), 0)
        low = _low_lanes((_CHUNK, _LANES))
        for g in range(_GROUPS):
            bg = bc_ref[:, g * _NSTATE:(g + 1) * _NSTATE]
            bgb = bg.astype(_BF)
            cg = bc_ref[:, gw + g * _NSTATE:gw + (g + 1) * _NSTATE].astype(_BF)
            cb, cbt = _dot(cg, bgb, _NT), _dot(bgb, cg, _NT)
            dcb = jnp.zeros((_CHUNK, _CHUNK), F32)
            dbg = jnp.zeros((_CHUNK, _NSTATE), F32)
            dcg = jnp.zeros((_CHUNK, _NSTATE), F32)
            for pr in range(g * _HPG // 2, (g + 1) * _HPG // 2):
                heads = (2 * pr, 2 * pr + 1)
                a_cols = [ac_ref[:, h:h + 1] for h in heads]
                dt_pair = jnp.where(low, dt_ref[:, heads[0]:heads[0] + 1], dt_ref[:, heads[1]:heads[1] + 1])
                xsv = xs_ref[:, _pair(pr)]
                x = xsv * dt_pair
                xb = x.astype(_BF)
                xhs = _halves(xb, low)
                dyv = dy_ref[:, _pair(pr)]
                dyb = dyv.astype(_BF)
                dyhs = _halves(dyb, low)
                sv, dsv = sall_ref[0, pr], dst[pr]
                svb, dsb = sv.astype(_BF), dsv.astype(_BF)
                a_lasts = [jnp.min(a, axis=0, keepdims=True) for a in a_cols]
                e_pair = jnp.where(low, jnp.exp(a_cols[0]), jnp.exp(a_cols[1]))
                el_pair = jnp.where(low, jnp.exp(a_lasts[0]), jnp.exp(a_lasts[1]))
                yo = e_pair * _dot(cg, svb)
                decays = [_decay(a_cols[i], ar_ref[h]) for i, h in enumerate(heads)]
                gms, gmts = [cb * lm for lm, _ in decays], [cbt * lmt for _, lmt in decays]
                w_cols = [jnp.exp(a_lasts[i] - a_cols[i]) for i in range(2)]
                x2, dy2 = jnp.concatenate(xhs, axis=0), jnp.concatenate(dyhs, axis=0)
                bwd = _dot(jnp.concatenate([bg * w_cols[0], bg * w_cols[1]], axis=0), dsb)
                dxg = _dot(jnp.concatenate(gms, axis=1), dyb, _TN)
                dg2, dgt2, xds2 = _dot(dy2, xb, _NT), _dot(x2, dyb, _NT), _dot(x2, dsb, _NT)
                das = []
                for i in range(2):
                    rows_i = slice(i * _CHUNK, (i + 1) * _CHUNK)
                    dcb = dcb + dg2[rows_i] * decays[i][0]
                    dbg = dbg + w_cols[i] * xds2[rows_i]
                    das.append(_rsum(dg2[rows_i] * gms[i]) - _rsum(dgt2[rows_i] * gmts[i]))
                bwd = jnp.where(low, bwd[:_CHUNK], bwd[_CHUNK:])
                dx = jnp.where(low, dxg[:_CHUNK], dxg[_CHUNK:]) + bwd
                edy = (e_pair * dyv).astype(_BF)
                dcg = dcg + _dot(edy, svb, _NT)
                zs, yos, sds, dts = (_halves(v, low) for v in (x * bwd, dyv * yo, sv * dsv, dx * xsv))
                for i, h in enumerate(heads):
                    z = _rsum(zs[i])
                    da_last = jnp.sum(z, axis=0, keepdims=True) + jnp.exp(a_lasts[i]) * jnp.sum(_rsum(sds[i]), axis=0, keepdims=True)
                    da_ref[:, h:h + 1] = das[i] + _rsum(yos[i]) - z + jnp.where(row == _CHUNK - 1, da_last, 0.0)
                    ddt_ref[:, h:h + 1] = _rsum(dts[i])
                dxs_ref[:, _pair(pr)] = dx * dt_pair
                dst[pr] = el_pair * dsv + _dot(cg, edy, _TN)
            dbc_ref[:, g * _NSTATE:(g + 1) * _NSTATE] = dbg + _dot(dcb, cg, _TN)
            dbc_ref[:, gw + g * _NSTATE:gw + (g + 1) * _NSTATE] = dcg + _dot(dcb, bgb)

    ch = lambda c: nc - 1 - c
    wide = pl.BlockSpec((_CHUNK, 2 * gw), lambda c: (ch(c), 0))
    return pl.pallas_call(
        body, grid=(nc,), in_specs=[sp['xs'], sp['bc'], sp['lane'], sp['lane'], sp['arow'], sp['st'], sp['xs']],
        out_specs=[sp['xs'], wide, sp['lane'], sp['lane']],
        out_shape=[_SDS((t, _AW), F32), _SDS((t, 2 * gw), F32), _SDS((t, _LANES), F32), _SDS((t, _LANES), F32)],
        scratch_shapes=[pltpu.VMEM((_HEADS // 2, _NSTATE, _LANES), F32)], name=name,
        compiler_params=_params(("arbitrary",), 16 << 20))(*[_hbm(a) for a in (act, act, dt, acum, a_row, sall, dy)])


def _scan_rows(v, reverse):
    r = lax.broadcasted_iota(jnp.int32, v.shape, 0)
    for s in (1, 2, 4, 8, 16, 32, 64):
        if reverse:
            v = v + jnp.where(r < _CHUNK - s, pltpu.roll(v, _CHUNK - s, 0), 0.0)
        else:
            v = v + jnp.where(r >= s, pltpu.roll(v, s, 0), 0.0)
    return v


def _softplus(x):
    return jnp.maximum(x, 0.0) + jnp.log(1.0 + jnp.exp(-jnp.abs(x)))


def _sigmoid(x):
    return 1.0 / (1.0 + jnp.exp(-x))


def _silu(x):
    return x * _sigmoid(x)


def _dsilu(x):
    s = _sigmoid(x)
    return s * (1.0 + x * (1.0 - s))


def _lanes(a):
    return jnp.pad(a, (0, _LANES - a.shape[0])).reshape(1, _LANES)


def _layer_fwd(x, p, l):
    cch = p['conv_w'].shape[1]
    sv = {}
    h1 = _rms_fwd(x, p['ln1_g'], name=f"ln1_fwd_{l}")
    qkv = _mm(h1, p['w_in'], b_cols=(0, 3 * _AW), outs=(_BF,), name=f"in_proj_qkv_{l}")
    xbc = _mm(h1, p['w_in'], b_cols=(3 * _AW, cch), name=f"in_proj_xbc_{l}")
    zdt = _mm(h1, p['w_in'], b_cols=(3 * _AW + cch, _AW + _LANES), name=f"in_proj_zdt_{l}")
    z, dt_raw = (zdt, _AW, 0), (zdt, _LANES, _AW // _LANES)

    outs = []
    for dil in _DILATIONS:
        outs += _attn_fwd(qkv, dil, name=f"attn_fwd_d{dil}_{l}")

    tile = 2 * _ABLK
    perms = [_perm(d, tile) for d in _DILATIONS[1:]]

    def combine(o1, l1, o2, l2, o3, l3, p2, p3):
        m = jnp.maximum(jnp.maximum(l1, l2), l3)
        e1, e2, e3 = jnp.exp(l1 - m), jnp.exp(l2 - m), jnp.exp(l3 - m)
        tot = e1 + e2 + e3
        mixed = sum(_expand_heads(e / tot) * o for e, o in ((e1, o1.astype(F32)), (e2, _unstride(o2, p2)), (e3, _unstride(o3, p3))))
        return mixed, m + jnp.log(tot)
    outs = [a if i % 2 or i == 0 else ("strided", a, _DILATIONS[i // 2]) for i, a in enumerate(outs)]
    attn, lse = _rows(combine, outs, perms, [(_AW, F32), (_LANES, F32)], tile=tile, name=f"attn_combine_{l}")
    attn_n = _rms_fwd(attn, p['attn_norm_g'], name=f"attn_norm_fwd_{l}")

    def conv(u0, before, w, b):
        u1, u2, u3 = _shifted(u0, before, True)
        return _silu(w[0:1] * u3 + w[1:2] * u2 + w[2:3] * u1 + w[3:4] * u0 + b)
    act = _rows(conv, [xbc], [p['conv_w'], p['conv_b'].reshape(1, cch)], [(cch, F32)], halos=[(xbc, -1)], tile=_tile_for(cch),
                name=f"conv_fwd_{l}")[0]

    def dtf(raw, bias, alog):
        dt = _softplus(raw + bias)
        return dt, _scan_rows(dt * -jnp.exp(alog), False)
    dt, acum = _rows(dtf, [dt_raw], [_lanes(p['dt_bias']), _lanes(p['a_log'])], [(_LANES, F32), (_LANES, F32)],
                     tile=_CHUNK, name=f"dt_fwd_{l}")
    a_row = acum[:, :_HEADS].T[:, None, :]
    y_ssd, sall = _ssd_fwd(act, dt, acum, a_row, name=f"ssd_fwd_{l}")
    dskip = jnp.repeat(p['d_skip'], _HDIM).reshape(1, _AW)
    xs = (act, _AW, 0)

    def gate(y, xs, z, dsk):
        return (y + dsk * xs) * _silu(z)
    y2 = _rows(gate, [y_ssd, xs, z], [dskip], [(_AW, F32)], tile=_tile_for(_AW), name=f"gate_fwd_{l}")[0]
    y_n = _rms_fwd(y2, p['ssd_norm_g'], groups=_GROUPS, name=f"ssd_norm_fwd_{l}")

    mix = jnp.concatenate([attn_n, y_n], axis=1)
    sv.update(x=x, h1=h1, qkv=qkv, zdt=zdt, xbc=xbc, attn=attn, lse=lse, act=act, dt=dt, acum=acum, a_row=a_row,
              sall=sall, y_ssd=y_ssd, dskip=dskip, y2=y2, mix=mix)
    return mix, sv


def _layer_fwd_mlp(p, sv, l):
    x2 = _mm(sv['mix'], p['w_out'], extra=(sv['x'],), epi=_add_to, name=f"out_proj_{l}")
    h2 = _rms_fwd(x2, p['ln2_g'], name=f"ln2_fwd_{l}")
    a = _mm(h2, p['w_mlp_in'], b_chips=_CHIPS, epi=lambda acc: (jnp.square(jnp.maximum(acc, 0.0)),), outs=(_BF,), name=f"mlp_in_{l}")
    x3 = _mm(a, p['w_mlp_out'], extra=(x2,), epi=_add_to, name=f"mlp_out_{l}")
    sv.update(x2=x2, h2=h2, a=a)
    return x3


def _layer_bwd(dx3, p, sv, l, send, after):
    cch = p['conv_w'].shape[1]
    g = {}
    dx3b = dx3.astype(_BF)
    du = _mm(dx3b, p['w_mlp_out'], tb=True, extra=(sv['a'],), outs=(_BF,), after=after,
             epi=lambda acc, a: (acc * 2.0 * jnp.sqrt(a.astype(F32)),), name=f"mlp_out_dx_{l}")
    g['w_mlp_out'] = _mm(sv['a'], dx3b, ta=True, outs=(_BF,), name=f"mlp_out_dw_{l}")
    g['w_mlp_in'] = _mm(sv['h2'], du, ta=True, out_chips=_CHIPS, outs=(_BF,), name=f"mlp_in_dw_{l}")
    sent = send(('w_mlp_out', 'w_mlp_in'), g)
    dh2 = _mm(du, p['w_mlp_in'], tb=True, b_chips=_CHIPS, after=sent, name=f"mlp_in_dx_{l}")
    dx2, g['ln2_g'] = _rms_bwd(sv['x2'], dh2, p['ln2_g'], dx3, name=f"ln2_bwd_{l}")
    dx2b = dx2.astype(_BF)
    dmix = _mm(dx2b, p['w_out'], tb=True, name=f"out_proj_dx_{l}")
    g['w_out'] = _mm(sv['mix'], dx2b, ta=True, outs=(_BF,), name=f"out_proj_dw_{l}")
    after_out = send(('w_out',), g)

    tile = 2 * _ABLK
    perms = [_perm(d, tile) for d in _DILATIONS[1:]]

    def norm_bwd(attn, dy, lse, gn, p2, p3):
        dattn, dgn = _rms_bwd_tile(attn, dy, gn, 1)
        prod, low = dattn * attn, _low_lanes((attn.shape[0], _LANES))
        lane = lax.broadcasted_iota(jnp.int32, lse.shape, 1)
        ld = jnp.where(lane < _HEADS, lse, 0.0)
        for pr in range(_HEADS // 2):
            for i, part in enumerate(_halves(prod[:, _pair(pr)], low)):
                ld = jnp.where(lane == _HEADS + 2 * pr + i, _rsum(part), ld)
        return dattn, _stride(dattn, p2, _DILATIONS[1]), _stride(dattn, p3, _DILATIONS[2]), ld, dgn
    *dos, ld, gn_sum = _rows(norm_bwd, [sv['attn'], (dmix, _AW, 0), sv['lse']], [p['attn_norm_g'].reshape(1, _AW)] + perms,
                             [(_AW, _BF)] + [(_AW, _BF, d) for d in _DILATIONS[1:]] + [(_LANES, F32)], [_AW], after=after_out,
                             tile=tile, name=f"attn_norm_bwd_{l}")
    g['attn_norm_g'] = gn_sum.sum(axis=0)
    parts = [_attn_bwd(sv['qkv'], do, ld, dil, name=f"attn_bwd_d{dil}_{l}") for do, dil in zip(dos, _DILATIONS)]

    def branch_sum(*t):
        parts_, (p2, p3) = t[:9], t[9:]
        t = [a.astype(F32) for a in parts_[:3]] + [_unstride(a, p2) for a in parts_[3:6]] + [_unstride(a, p3) for a in parts_[6:]]
        return jnp.concatenate([t[i] + t[3 + i] + t[6 + i] for i in range(3)], axis=1)
    branch_ins = list(parts[0]) + [("strided", a, d) for pr, d in zip(parts[1:], _DILATIONS[1:]) for a in pr]
    w_all = 3 * _AW + cch + _AW + _LANES
    dproj = _rows(branch_sum, branch_ins, perms, [(3 * _AW, _BF)], into=(None, w_all, 0), tile=tile, name=f"attn_bwd_sum_{l}")[0]

    xs, z, dt_raw = (sv['act'], _AW, 0), (sv['zdt'], _AW, 0), (sv['zdt'], _LANES, _AW // _LANES)

    def gate_bwd(y2, dy, y, xs, z, dsk, gn):
        dy2, dgn = _rms_bwd_tile(y2, dy, gn, _GROUPS)
        dy1 = dy2 * _silu(z)
        return dy1, dsk * dy1, dy2 * (y + dsk * xs) * _dsilu(z), dy1 * xs, dgn
    dy1, dxs_skip, dz, dsk_sum, gn_sum = _rows(
        gate_bwd, [sv['y2'], (dmix, _AW, 1), sv['y_ssd'], xs, z], [sv['dskip'], p['ssd_norm_g'].reshape(1, _AW)],
        [(_AW, F32), (_AW, F32), (_AW, _BF)], [_AW, _AW], tile=128, name=f"gate_bwd_{l}")
    g['ssd_norm_g'] = gn_sum.sum(axis=0)
    g['d_skip'] = dsk_sum.sum(axis=0).reshape(_HEADS, _HDIM).sum(axis=1)
    dxs, dbc, ddt, da = _ssd_bwd(sv['act'], sv['dt'], sv['acum'], sv['a_row'], sv['sall'], dy1, name=f"ssd_bwd_{l}")

    def dtb(da, ddtx, raw, dt, dz, bias, alog):
        a = -jnp.exp(alog)
        dda = _scan_rows(da, True)
        draw = (dda * a + ddtx) * _sigmoid(raw + bias)
        return jnp.concatenate([dz, draw.astype(dz.dtype)], axis=1), draw, dda * dt * a
    dproj, dbias, dalog = _rows(dtb, [da, ddt, dt_raw, sv['dt'], dz], [_lanes(p['dt_bias']), _lanes(p['a_log'])],
                                [(_AW + _LANES, _BF)], [_LANES, _LANES], into=(dproj, w_all, (3 * _AW + cch) // (_AW + _LANES)),
                                tile=_CHUNK, name=f"dt_bwd_{l}")
    g['dt_bias'], g['a_log'] = dbias.sum(axis=0)[:_HEADS], dalog.sum(axis=0)[:_HEADS]
    def conv_bwd1(u0, dxs, dbc, dxk, before, w, b):
        u1, u2, u3 = _shifted(u0, before, True)
        pre = w[0:1] * u3 + w[1:2] * u2 + w[2:3] * u1 + w[3:4] * u0 + b
        dp = jnp.concatenate([dxs + dxk, dbc], axis=1) * _dsilu(pre)
        return dp, dp * u3, dp * u2, dp * u1, dp * u0, dp
    dpre, *dws = _rows(conv_bwd1, [sv['xbc'], dxs, dbc, dxs_skip], [p['conv_w'], p['conv_b'].reshape(1, cch)], [(cch, F32)],
                       [cch] * 5, halos=[(sv['xbc'], -1)], tile=128, name=f"conv_bwd_pre_{l}")
    g['conv_w'] = jnp.stack([dws[i].sum(axis=0) for i in range(_CONV_K)])
    g['conv_b'] = dws[4].sum(axis=0)

    def conv_bwd2(p0, after_, w):
        p1, p2, p3 = _shifted(p0, after_, False)
        return w[3:4] * p0 + w[2:3] * p1 + w[1:2] * p2 + w[0:1] * p3
    dproj = _rows(conv_bwd2, [dpre], [p['conv_w']], [(cch, _BF)], halos=[(dpre, 1)], into=(dproj, w_all, 3 * _AW // cch),
                  tile=_tile_for(cch), name=f"conv_bwd_in_{l}")[0]
    g_all = _mm(sv['h1'], dproj, ta=True, outs=(_BF,), name=f"in_proj_dw_{l}")
    z0 = 3 * _AW + cch
    g['w_in'] = jnp.concatenate([g_all[:, :3 * _AW], g_all[:, z0:z0 + _AW], g_all[:, 3 * _AW:z0], g_all[:, z0 + _AW:z0 + _AW + _HEADS]], axis=1)
    sent = send(('w_in',), g)
    for n in _BIG:
        del g[n]
    dh1 = _mm(dproj, p['w_in'], tb=True, after=sent, name=f"in_proj_dx_{l}")
    dx, g['ln1_g'] = _rms_bwd(sv['x'], dh1, p['ln1_g'], dx2, name=f"ln1_bwd_{l}")
    return dx, g


def _loss_bwd(x, g, tgt):
    w = x.shape[1]
    tile = _tile_for(w)

    def fn(x, tgt, g):
        r = _rstd(x)
        xh = x * r
        e = xh * g - tgt
        gd = e * (g / w)
        dx = r * (gd - xh * jnp.mean(gd * xh, axis=-1, keepdims=True))
        rowloss = 0.5 * jnp.mean(e * e, axis=-1, keepdims=True)
        return dx, (e / w) * xh, jnp.broadcast_to(rowloss, (tile, _LANES))
    dx, dg, ls = _rows(fn, [x, tgt], [g.reshape(1, w)], [(w, F32)], [w, _LANES], tile=tile, name="loss_head")
    return dx, dg.sum(axis=0), ls[:, 0].sum()


def _adamw_math(w, g, m, v):
    m2 = _B1 * m + (1.0 - _B1) * g
    v2 = _B2 * v + (1.0 - _B2) * jnp.square(g)
    m_hat = m2 / (1.0 - _B1 ** _STEP)
    v_hat = v2 / (1.0 - _B2 ** _STEP)
    return -_LR * (m_hat / (jnp.sqrt(v_hat) + _AEPS) + _WD * w), m2, v2


def _adamw(w, g, m, v, *, name):
    width = w.shape[-1]
    flat = [a.reshape(-1, width) for a in (w, g, m, v)]
    tile = _pick(flat[0].shape[0], (_tile_for(width), 32, 8))
    res = _rows(_adamw_math, flat, [], [(width, F32)] * 3, tile=tile, name=name)
    return [r.reshape(w.shape) for r in res]


_HBM = pl.BlockSpec(memory_space=pltpu.HBM)


def _place():
    x, y, c = lax.axis_index("x"), lax.axis_index("y"), lax.axis_index("c")
    other_chips = [(1 - x, y), (x, 1 - y), (1 - x, 1 - y)]
    return x, y, c, other_chips


def _remote(src, dst, sems, i, dev):
    return pltpu.make_async_remote_copy(src_ref=src, dst_ref=dst, send_sem=sems[0].at[i], recv_sem=sems[1].at[i],
                                        device_id=dev, device_id_type=_MESH)


def _exchange8(v, *, reduce, after=None, name):
    r, w = v.shape
    behind = [] if after is None else [after]

    def body(v_ref, *rest):
        all_ref, rest = rest[len(behind)], rest[len(behind) + 1:]
        sems = rest[-2:]
        x, y, c, _ = _place()
        me = 4 * x + 2 * y + c
        all_ref[me] = v_ref[...]
        flips = [((d >> 2) & 1, (d >> 1) & 1, d & 1) for d in range(1, 8)]
        sends = [_remote(v_ref, all_ref.at[me], sems, i, (x ^ fx, y ^ fy, c ^ fc)) for i, (fx, fy, fc) in enumerate(flips)]
        for cp in sends:
            cp.start()
        for i, (fx, fy, fc) in enumerate(flips):
            _remote(v_ref, all_ref.at[me ^ (4 * fx + 2 * fy + fc)], sems, i, (x ^ fx, y ^ fy, c ^ fc)).wait_recv()
        for cp in sends:
            cp.wait_send()
        if reduce:
            acc = all_ref[0]
            for s in range(1, 8):
                acc = acc + all_ref[s]
            rest[0][...] = acc

    vm = pl.BlockSpec(memory_space=pltpu.VMEM)
    out_shape = [_SDS((8, r, w), v.dtype)] + ([_SDS((r, w), v.dtype)] if reduce else [])
    res = pl.pallas_call(body, in_specs=[vm] + [_ANY] * len(behind), out_specs=[vm] * len(out_shape), out_shape=out_shape, name=name,
                         scratch_shapes=[pltpu.SemaphoreType.DMA((7,)), pltpu.SemaphoreType.DMA((7,))],
                         compiler_params=pltpu.CompilerParams(vmem_limit_bytes=int(32 << 20)))(v, *behind)
    return res[1] if reduce else res[0]


_SEM = pl.BlockSpec(memory_space=pltpu.SEMAPHORE)
_ANY = pl.BlockSpec(memory_space=pl.ANY)
_EFFECT = pltpu.SideEffectType.DATAFLOW_SIDE_EFFECTING


def _send_start(name, srcs, land_shapes, plan, n_sends, after):
    ns, nl = len(srcs), len(land_shapes)
    zones = [_hbm(lax.empty(s.shape, s.dtype)) if isinstance(s, _SDS) else s for s in land_shapes]

    def body(*refs):
        ins, lands, sems = refs[:ns], refs[ns:ns + nl], refs[ns + nl + 1:ns + nl + 3]
        x, y, c, chips = _place()
        for i, (s, d, dev) in enumerate(plan(x, y, c, chips, ins, lands)[0]):
            _remote(s, d, sems, i, dev).start()
        refs[-1][...] = jnp.zeros_like(refs[-1])

    sem = pltpu.SemaphoreType.DMA((n_sends,))
    res = pl.pallas_call(
        body, name=name, in_specs=[_HBM] * (ns + nl) + [_ANY],
        out_shape=(sem, sem, *[pltpu.HBM(s.shape, s.dtype) for s in land_shapes], _SDS((8, _LANES), F32)),
        out_specs=(_SEM, _SEM, *[_HBM] * nl, pl.BlockSpec(memory_space=pltpu.VMEM)),
        input_output_aliases={ns + i: 2 + i for i in range(nl)},
        compiler_params=pltpu.CompilerParams(has_side_effects=_EFFECT))(
            *[_hbm(s) for s in srcs], *zones, after)
    return dict(sems=res[:2], srcs=srcs, lands=res[2:2 + nl], plan=plan), res[-1]


def _send_wait(name, h, after):
    ns, nl = len(h['srcs']), len(h['lands'])

    def body(*refs):
        ins, lands, sems = refs[:ns], refs[ns:ns + nl], refs[ns + nl:ns + nl + 2]
        x, y, c, chips = _place()
        sends, landings = h['plan'](x, y, c, chips, ins, lands)
        for i, (s, d, dev) in enumerate(sends):
            _remote(s, d, sems, i, dev).wait_send()
        for i, d in enumerate(landings):
            _remote(d, d, sems, i, sends[i][2]).wait_recv()

    return pl.pallas_call(
        body, name=name, in_specs=[_HBM] * (ns + nl) + [_SEM, _SEM, _ANY],
        out_shape=tuple(pltpu.HBM(a.shape, a.dtype) for a in h['lands']), out_specs=tuple([_HBM] * nl),
        input_output_aliases={ns + i: i for i in range(nl)},
        compiler_params=pltpu.CompilerParams(has_side_effects=_EFFECT))(
            *[_hbm(s) for s in h['srcs']], *h['lands'], *h['sems'], after)


def _gather_plan(items):
    def plan(x, y, c, chips, ins, lands):
        k = 2 * x + y
        sends = [(ins[si].at[l], lands[t].at[k], (px, py, c)) for t, (si, l) in enumerate(items) for px, py in chips]
        return sends, [lands[t].at[2 * px + py] for t in range(len(items)) for px, py in chips]
    return plan


_FLIPS = [((d >> 2) & 1, (d >> 1) & 1, d & 1) for d in range(1, 8)]


def _reduce_plan(halves):
    def plan(x, y, c, chips, ins, lands):
        sends, landings = [], []
        for t, hf in enumerate(halves):
            for i, (fx, fy, fc) in enumerate(_FLIPS):
                px, py, pc = x ^ fx, y ^ fy, c ^ fc
                sends.append((ins[t].at[2 * px + py, pl.ds(pc * hf, hf)], lands[t].at[i], (px, py, pc)))
                landings.append(lands[t].at[i])
        return sends, landings
    return plan


def _swap(name, srcs, out_shapes, plan, n_sends):
    n = len(srcs)

    def body(*refs):
        ins, outs, sems = refs[:n], refs[n:n + len(out_shapes)], refs[-2:]
        x, y, c, chips = _place()
        sends, landings = plan(x, y, c, chips, ins, outs)
        out = [_remote(s, d, sems, i, dev) for i, (s, d, dev) in enumerate(sends)]
        for cp in out:
            cp.start()
        for i, d in enumerate(landings):
            _remote(d, d, sems, i, sends[i][2]).wait_recv()
        for cp in out:
            cp.wait_send()

    return pl.pallas_call(
        body, in_specs=[_HBM] * n, out_specs=[_HBM] * len(out_shapes), out_shape=out_shapes, name=name,
        scratch_shapes=[pltpu.SemaphoreType.DMA((n_sends,)), pltpu.SemaphoreType.DMA((n_sends,))])(*srcs)


def _sum_owned(grads, landed, c, k, names):
    def sum8(*parts):
        acc = parts[0].astype(F32)
        for p in parts[1:]:
            acc = acc + p.astype(F32)
        return acc
    outs = []
    for g, got, name in zip(grads, landed, names):
        hf, b = got.shape[1:]
        own = lax.dynamic_slice_in_dim(lax.dynamic_index_in_dim(g, k, axis=0, keepdims=False), c * hf, hf, axis=0)
        outs.append(_rows(sum8, [own] + [got[i] for i in range(len(_FLIPS))], [], [(b, F32)], tile=_pick(hf, (_tile_for(b), 32)),
                          name=f"grad_sum_{name}")[0])
    return outs


def _share_halves(mine, *, name):
    n = len(mine)

    def plan(x, y, c_, chips, ins, outs):
        return [(ins[t], outs[t], (x, y, 1 - c_)) for t in range(n)], [outs[t] for t in range(n)]
    return _swap(name, mine, [_SDS(h.shape, F32) for h in mine], plan, n)


def _adamw_owned(w, mine, theirs, m, v, c, *, name):
    depth, a, b = w.shape
    half = a // 2
    tile = _pick(half, (_tile_for(b), 32, 8))
    nh = half // tile

    def blocks_of(l):
        return lambda i: (jnp.clip(i - 2 * nh * l, 0, 2 * nh - 1) % nh, 0)

    def fn(w, m, v, *rest):
        halves, cflag = rest[:-1], rest[-1]
        step = pl.program_id(0)
        is_mine = cflag[0:1, 0:1] == ((step // nh) % 2).astype(F32)
        g = jnp.where(is_mine, halves[0], halves[1])
        for l in range(1, depth):
            g = jnp.where(step >= 2 * nh * l, jnp.where(is_mine, halves[2 * l], halves[2 * l + 1]), g)
        return (g,) + _adamw_math(w, g, m, v)
    ins = [a_.reshape(depth * a, b) for a_ in (w, m, v)]
    ins += [(h, b, blocks_of(l)) for l in range(depth) for h in (mine[l], theirs[l])]
    res = _rows(fn, ins, [jnp.full((1, _LANES), c, F32)], [(b, F32)] * 4, tile=tile, name=name)
    return [r.reshape(w.shape) for r in res]


_BIG = ("w_in", "w_out", "w_mlp_in", "w_mlp_out")
_SMALL = ("ln1_g", "conv_b", "dt_bias", "a_log", "d_skip", "attn_norm_g", "ssd_norm_g", "ln2_g", "final_norm_g")
_ORDER = ("ln1_g", "w_in", "conv_w", "conv_b", "dt_bias", "a_log", "d_skip", "attn_norm_g", "ssd_norm_g", "w_out", "ln2_g",
          "w_mlp_in", "w_mlp_out", "final_norm_g")


def _pack(parts, rows):
    flat = jnp.concatenate([p.reshape(-1) for p in parts])
    return jnp.pad(flat, (0, rows * _LANES - flat.shape[0])).reshape(rows, _LANES)


def _unpack(buf, like):
    flat, out, o = buf.reshape(-1), [], 0
    for p in like:
        out.append(flat[o:o + p.size].reshape(p.shape))
        o += p.size
    return out


def kernel(x, ln1_g, w_in, conv_w, conv_b, dt_bias, a_log, d_skip, attn_norm_g, ssd_norm_g, w_out, ln2_g, w_mlp_in, w_mlp_out, final_norm_g, loss_target, m_ln1_g, m_w_in, m_conv_w, m_conv_b, m_dt_bias, m_a_log, m_d_skip, m_attn_norm_g, m_ssd_norm_g, m_w_out, m_ln2_g, m_w_mlp_in, m_w_mlp_out, m_final_norm_g, v_ln1_g, v_w_in, v_conv_w, v_conv_b, v_dt_bias, v_a_log, v_d_skip, v_attn_norm_g, v_ssd_norm_g, v_w_out, v_ln2_g, v_w_mlp_in, v_w_mlp_out, v_final_norm_g):
    w = dict(ln1_g=ln1_g, w_in=w_in, conv_w=conv_w, conv_b=conv_b, dt_bias=dt_bias, a_log=a_log, d_skip=d_skip,
             attn_norm_g=attn_norm_g, ssd_norm_g=ssd_norm_g, w_out=w_out, ln2_g=ln2_g, w_mlp_in=w_mlp_in, w_mlp_out=w_mlp_out,
             final_norm_g=final_norm_g)
    m = dict(ln1_g=m_ln1_g, w_in=m_w_in, conv_w=m_conv_w, conv_b=m_conv_b, dt_bias=m_dt_bias, a_log=m_a_log, d_skip=m_d_skip,
             attn_norm_g=m_attn_norm_g, ssd_norm_g=m_ssd_norm_g, w_out=m_w_out, ln2_g=m_ln2_g, w_mlp_in=m_w_mlp_in,
             w_mlp_out=m_w_mlp_out, final_norm_g=m_final_norm_g)
    v = dict(ln1_g=v_ln1_g, w_in=v_w_in, conv_w=v_conv_w, conv_b=v_conv_b, dt_bias=v_dt_bias, a_log=v_a_log, d_skip=v_d_skip,
             attn_norm_g=v_attn_norm_g, ssd_norm_g=v_ssd_norm_g, w_out=v_w_out, ln2_g=v_ln2_g, w_mlp_in=v_w_mlp_in,
             w_mlp_out=v_w_mlp_out, final_norm_g=v_final_norm_g)
    depth, d_model = ln1_g.shape
    n_chips = 4
    c = lax.axis_index("c")
    chip = 2 * lax.axis_index("x") + lax.axis_index("y")
    in_proj = w_in.shape[2] * n_chips
    cch = conv_w.shape[2] * n_chips
    zdt_pad = _LANES - _HEADS

    cw = _exchange8(conv_w.reshape(depth * _CONV_K, -1), reduce=False, name="gather_conv_w")[0::2]
    conv_full = cw.reshape(n_chips, depth, _CONV_K, -1).transpose(1, 2, 0, 3).reshape(depth, _CONV_K, cch)
    own = [w[n].astype(_BF) for n in _BIG]
    is_own = (jnp.arange(n_chips) == chip).reshape(n_chips, 1, 1)

    def start_gather(tag, items, after):
        lands = [_SDS((n_chips, *own[i].shape[1:]), _BF) for i, _ in items]
        return _send_start(f"gather_start_{tag}", own, lands, _gather_plan(items), 3 * len(items), after)

    def finish_gather(tag, handle, items, after):
        landed = _send_wait(f"gather_wait_{tag}", handle, after)
        return {_BIG[i]: jnp.where(is_own, own[i][l][None], g) for (i, l), g in zip(items, landed)}

    def layer_weights(l, blocks):
        p = {}
        if 'w_in' in blocks:
            full_in = blocks['w_in'].transpose(1, 0, 2).reshape(d_model, in_proj)
            p['w_in'] = jnp.concatenate([full_in[:, :3 * _AW], full_in[:, 4 * _AW:4 * _AW + cch], full_in[:, 3 * _AW:4 * _AW],
                                         full_in[:, 4 * _AW + cch:], jnp.zeros((d_model, zdt_pad), _BF)], axis=1)
        if 'w_out' in blocks:
            p['w_out'] = blocks['w_out'].reshape(-1, d_model)
            p['w_mlp_in'] = blocks['w_mlp_in']
            p['w_mlp_out'] = blocks['w_mlp_out'].reshape(-1, d_model)
        return p

    groups = dict(a=[(0, 0)], b=[(1, 0), (2, 0), (3, 0)], c=[(0, 1)], d=[(1, 1), (2, 1), (3, 1)])
    handles, token = {}, conv_full

    half_in = own[0].shape[1] // 2

    def rows_of(ref, who):
        return ref.at[pl.ds(who * half_in, half_in)]

    def plan_a(x_, y_, c_, chips, ins, lands):
        k = 2 * x_ + y_
        sends = [(rows_of(ins[0].at[0], c_), rows_of(lands[0].at[k], c_), (px, py, c_)) for px, py in chips]
        return sends, [rows_of(lands[0].at[2 * px + py], c_) for px, py in chips]

    def plan_pass(x_, y_, c_, chips, ins, lands):
        sends = [(rows_of(lands[0].at[2 * px + py], c_),) * 2 + ((x_, y_, 1 - c_),) for px, py in chips]
        return sends, [rows_of(lands[0].at[2 * px + py], 1 - c_) for px, py in chips]
    handles["a"], token = _send_start("gather_start_a", own[:1], [_SDS((n_chips, *own[0].shape[1:]), _BF)], plan_a, 3, token)
    for tag, items in list(groups.items())[1:]:
        handles[tag], token = start_gather(tag, items, token)
    landed = _send_wait("gather_land_a", handles["a"], token)
    handles["a"], token = _send_start("gather_pass_a", [], landed, plan_pass, 3, landed[0])
    layers = [{n: w[n][l] for n in _SMALL[:-1]} for l in range(depth)]
    for l in range(depth):
        layers[l]['conv_w'] = conv_full[l]

    layers[0].update(layer_weights(0, finish_gather("a", handles["a"], groups["a"], token)))
    mix, sv0 = _layer_fwd(x[0], layers[0], 0)
    layers[0].update(layer_weights(0, finish_gather("b", handles["b"], groups["b"], mix)))
    h = _layer_fwd_mlp(layers[0], sv0, 0)
    layers[1].update(layer_weights(1, finish_gather("c", handles["c"], groups["c"], h)))
    mix, sv1 = _layer_fwd(h, layers[1], 1)
    layers[1].update(layer_weights(1, finish_gather("d", handles["d"], groups["d"], mix)))
    h = _layer_fwd_mlp(layers[1], sv1, 1)
    saved = [sv0, sv1]

    def by_chip(g, name):
        if name == "w_mlp_in":
            return g
        if name == "w_in":
            return g.reshape(d_model, n_chips, -1).transpose(1, 0, 2)
        return g.reshape(n_chips, -1, d_model)

    pending = []

    def sender(l):
        def send(names, g):
            srcs = [by_chip(g[n], n) for n in names]
            halves = [s.shape[1] // 2 for s in srcs]
            lands = [_SDS((len(_FLIPS), hf, s.shape[2]), _BF) for s, hf in zip(srcs, halves)]
            handle, tok = _send_start(f"grad_start_{names[-1]}_{l}", srcs, lands, _reduce_plan(halves), len(_FLIPS) * len(srcs), srcs[0])
            pending.append((l, names, srcs, handle))
            return tok
        return send

    dx, g_final, loss_part = _loss_bwd(h, final_norm_g, loss_target[0])
    grads, after = [None] * depth, None
    for l in reversed(range(depth)):
        dx, grads[l] = _layer_bwd(dx, layers[l], saved[l], l, sender(l), after)
        after = dx
    landed_of = {}
    sent_in = {(n, l): (gi, j) for gi, (l, names, _, _) in enumerate(pending) for j, n in enumerate(names)}

    def landed_for(gi, after):
        if gi not in landed_of:
            l, names, _, handle = pending[gi]
            landed_of[gi] = _send_wait(f"grad_wait_{names[-1]}_{l}", handle, after)
        return landed_of[gi]

    red, delta, new_m, new_v = {}, {}, {}, {}
    after = dx
    for n in ("w_mlp_out", "w_mlp_in", "w_out", "w_in"):
        mine = []
        for l in range(depth):
            gi, j = sent_in[(n, l)]
            got = landed_for(gi, after)[j]
            mine.append(_sum_owned([pending[gi][2][j]], [got], c, chip, [f"{n}_{l}"])[0])
        theirs = _share_halves(mine, name=f"grad_share_{n}")
        red[n], delta[n], new_m[n], new_v[n] = _adamw_owned(w[n], mine, theirs, m[n], v[n], c, name=f"adamw_{n}")
        after = delta[n]

    small = {n: jnp.stack([grads[l][n] for l in range(depth)]) for n in _SMALL[:-1] + ("conv_w",)}
    small["final_norm_g"] = g_final
    parts = [loss_part.reshape(1)] + [small[n] for n in _SMALL + ("conv_w",)]
    rows = -(-sum(p.size for p in parts) // 1024) * 8
    tot = _unpack(_exchange8(_pack(parts, rows), reduce=True, after=red[_BIG[0]], name="allreduce_small"), parts)
    loss = tot[0][0]
    red.update(zip(_SMALL + ("conv_w",), tot[1:]))
    red["conv_w"] = lax.dynamic_index_in_dim(red["conv_w"].reshape(depth, _CONV_K, n_chips, -1), chip, axis=2, keepdims=False)

    names = _SMALL + ("conv_w",)
    like = [w[n] for n in names]
    srows = -(-sum(p.size for p in like) // 1024) * 8
    res = _adamw(*[_pack([d[n] for n in names], srows) for d in (w, red, m, v)], name="adamw_small")
    for dst, buf in zip((delta, new_m, new_v), res):
        dst.update(zip(names, _unpack(buf, like)))
    return (loss, dx[None], *[red[n] for n in _ORDER], *[delta[n] for n in _ORDER], *[new_m[n] for n in _ORDER],
            *[new_v[n] for n in _ORDER])
```

```python
import numpy as np
import jax
import jax.numpy as jnp
from jax import lax
from jax.experimental import pallas as pl
from jax.experimental.pallas import tpu as pltpu

F32 = jnp.float32
_BF = jnp.bfloat16
_NEG = -1e30
_EPS = 1e-5
_HEADS = 16
_HDIM = 64
_AW = _HEADS * _HDIM
_ABLK = 128
_DILATIONS = (1, 4, 16)
_CHUNK = 128
_NSTATE = 128
_GROUPS = 2
_HPG = _HEADS // _GROUPS
_CONV_K = 4
_LANES = 128
_CHIPS = 4
_LR, _B1, _B2, _AEPS, _WD, _STEP = 0.001, 0.9, 0.999, 1e-08, 0.01, 10
_VMEM_CAP = 56 * 1024 * 1024
_MESH = pl.DeviceIdType.MESH
_SDS = jax.ShapeDtypeStruct
_NT = (((1,), (1,)), ((), ()))
_TN = (((0,), (0,)), ((), ()))


def _params(sem, est_bytes):
    lim = int(min(max(2 * est_bytes + (4 << 20), 16 << 20), _VMEM_CAP))
    return pltpu.CompilerParams(dimension_semantics=sem, vmem_limit_bytes=lim)


def _nbytes(shape, dtype):
    return int(np.prod(shape)) * jnp.dtype(dtype).itemsize


def _hbm(a):
    return pltpu.with_memory_space_constraint(a, pltpu.HBM)


def _dot(a, b, dims=(((1,), (0,)), ((), ()))):
    return lax.dot_general(a.astype(_BF), b.astype(_BF), dims, preferred_element_type=F32)


_HALO = 8


def _rows(fn, ins, consts, outs, sums=(), *, halos=(), into=None, after=None, tile, name):
    rows = (ins[0][0] if isinstance(ins[0], tuple) else ins[0]).shape[0]
    n_steps = rows // tile

    def norm_in(a):
        if not isinstance(a, tuple):
            return a, tile, a.shape[1], lambda i: (i, 0)
        if isinstance(a[0], str) and a[0] == "slot":
            return a[1], (None, tile, a[1].shape[2]), a[1].shape[2], lambda i, s=a[2]: (s, i, 0)
        if isinstance(a[0], str):
            return a[1], tile // a[2], a[1].shape[1], lambda i: (i, 0)
        return a[0], tile, a[1], a[2] if callable(a[2]) else (lambda i, j=a[2]: (i, j))
    ins = [norm_in(a) for a in ins]
    outs = [(w, dt, d[0] if d else 1) for w, dt, *d in outs]
    n_in, n_h, n_c, n_o, n_s = len(ins), len(halos), len(consts), len(outs), len(sums)
    n_x = int(into is not None and into[0] is not None) + int(after is not None)

    def body(*refs):
        step = pl.program_id(0)
        vals = [r[...] for r in refs[:n_in]]
        for r, (_, side) in zip(refs[n_in:n_in + n_h], halos):
            vals.append(jnp.where(step == (0 if side < 0 else n_steps - 1), 0.0, r[...]))
        vals += [r[...] for r in refs[n_in + n_h:n_in + n_h + n_c]]
        refs = refs[:n_in] + refs[n_in + n_h:]
        res = fn(*vals)
        res = res if isinstance(res, tuple) else (res,)
        orefs = refs[n_in + n_c + n_x:n_in + n_c + n_x + n_o]
        srefs = refs[n_in + n_c + n_x + n_o:]
        for r, v in zip(orefs, res[:n_o]):
            r[...] = v.astype(r.dtype)
        if n_s:
            @pl.when(pl.program_id(0) == 0)
            def _():
                for r in srefs:
                    r[...] = jnp.zeros_like(r)
            for r, v in zip(srefs, res[n_o:]):
                r[...] += v.reshape(tile // 8, 8, v.shape[-1]).sum(axis=0)

    per = tile // _HALO
    in_specs = [pl.BlockSpec(r if isinstance(r, tuple) else (r, w), idx) for _, r, w, idx in ins]
    in_specs += [pl.BlockSpec((_HALO, a.shape[1]), (lambda i: (jnp.maximum(i * per - 1, 0), 0)) if side < 0
                              else (lambda i: (jnp.minimum((i + 1) * per, rows // _HALO - 1), 0))) for a, side in halos]
    in_specs += [pl.BlockSpec(c.shape, lambda i, nd=c.ndim: (0,) * nd) for c in consts]
    out_shape = [_SDS((rows // d, d * w), dt) for w, dt, d in outs] + [_SDS((8, w), F32) for w in sums]
    out_specs = [pl.BlockSpec((tile // d, d * w), lambda i: (i, 0)) for w, _, d in outs]
    out_specs += [pl.BlockSpec((8, w), lambda i: (0, 0)) for w in sums]
    est = (sum(_nbytes((tile if isinstance(r, tuple) else r, w), a.dtype) for a, r, w, _ in ins)
           + sum(_nbytes((tile, w), dt) for w, dt, _ in outs))
    shared, aliases = [], {}
    if into is not None:
        buf, total, j = into
        out_shape[0] = _SDS((rows, total), outs[0][1])
        out_specs[0] = pl.BlockSpec((tile, outs[0][0]), lambda i: (i, j))
        if buf is not None:
            shared, aliases = [buf], {n_in + n_h + n_c: 0}
    if after is not None:
        shared.append(after)
    in_specs += [pl.BlockSpec(memory_space=pl.ANY)] * len(shared)
    return pl.pallas_call(body, grid=(n_steps,), in_specs=in_specs, out_specs=out_specs, out_shape=out_shape, name=name,
                          input_output_aliases=aliases, compiler_params=_params(("arbitrary",), 3 * est))(
                              *[_hbm(a[0]) for a in ins], *[_hbm(a) for a, _ in halos], *consts, *shared)


def _perm(d, tile):
    p = np.zeros((tile, tile), np.float32)
    t = np.arange(tile)
    p[t, (t % d) * (tile // d) + t // d] = 1.0
    return jnp.asarray(p, _BF)


def _unstride(s, p):
    d = p.shape[0] // s.shape[0]
    w = s.shape[1] // d
    return _dot(p, jnp.concatenate([s[:, r * w:(r + 1) * w] for r in range(d)], axis=0))


def _stride(x, p, d):
    z = _dot(p, x, _TN)
    n = x.shape[0] // d
    return jnp.concatenate([z[r * n:(r + 1) * n] for r in range(d)], axis=1)


def _shifted(u, halo, back):
    n = u.shape[0] + _HALO
    if back:
        ext = jnp.concatenate([halo, u], axis=0)
        return [pltpu.roll(ext, j, 0)[_HALO:] for j in (1, 2, 3)]
    ext = jnp.concatenate([u, halo], axis=0)
    return [pltpu.roll(ext, n - j, 0)[:u.shape[0]] for j in (1, 2, 3)]


def _tile_for(width):
    return max(c for c in (256, 128, 64, 32) if c * width <= (1 << 18) or c == 32)


def _rstd(x):
    return lax.rsqrt(jnp.mean(x * x, axis=-1, keepdims=True) + _EPS)


def _split(x, groups):
    w = x.shape[-1] // groups
    return [x[:, g * w:(g + 1) * w] for g in range(groups)]


def _cat(parts):
    return parts[0] if len(parts) == 1 else jnp.concatenate(parts, axis=-1)


def _rms_bwd_tile(x, dy, g, groups):
    dxs, dgs = [], []
    for xs, ds, gs in zip(_split(x, groups), _split(dy.astype(F32), groups), _split(g, groups)):
        r = _rstd(xs)
        xh = xs * r
        gd = ds * gs
        dxs.append(r * (gd - xh * jnp.mean(gd * xh, axis=-1, keepdims=True)))
        dgs.append(ds * xh)
    return _cat(dxs), _cat(dgs)


def _rms_fwd(x, g, *, groups=1, name):
    def fn(x, g):
        return _cat([xs * _rstd(xs) * gs for xs, gs in zip(_split(x, groups), _split(g, groups))])
    w = x.shape[1]
    return _rows(fn, [x], [g.reshape(1, w)], [(w, _BF)], tile=_tile_for(w), name=name)[0]


def _rms_bwd(x, dy, g, res=None, *, name):
    def fn(x, dy, *rest):
        dx, dg = _rms_bwd_tile(x, dy, rest[-1], 1)
        return (dx + rest[0] if res is not None else dx), dg
    w = x.shape[1]
    ins = [x, dy] + ([res] if res is not None else [])
    dx, dg = _rows(fn, ins, [g.reshape(1, w)], [(w, F32)], [w], tile=_tile_for(w), name=name)
    return dx, dg.sum(axis=0)


def _pick(n, cands):
    for c in cands:
        if n % c == 0:
            return c
    raise ValueError(f"no block size for {n}")


_MM_BLOCKS = (1024, 1152, 512, 384)


def _mm(a, b, *, ta=False, tb=False, extra=(), epi=None, outs=(F32,), after=None, b_chips=0, out_chips=0, b_cols=None, name):
    m, k = (a.shape[1], a.shape[0]) if ta else a.shape
    b_shape = (b.shape[1], b.shape[2] * b_chips) if b_chips else b.shape
    if b_cols is not None:
        b_shape = (b.shape[0], b_cols[1])
    n = b_shape[0] if tb else b_shape[1]
    assert k == (b_shape[1] if tb else b_shape[0])
    n_cap = n // max(out_chips, 1 if tb else b_chips, 1)
    k_cap = k // (b_chips if (b_chips and tb) else 1)
    bm, bn = _pick(m, _MM_BLOCKS), _pick(n_cap, _MM_BLOCKS)
    bk = _pick(k_cap, (2048, 1920) + _MM_BLOCKS)
    nk = k // bk
    n_e, n_o = len(extra), len(outs)
    behind = [] if after is None else [after]
    dims = (((0 if ta else 1,), (1 if tb else 0,)), ((), ()))

    def body(a_ref, b_ref, *rest):
        ex, orefs, acc = rest[:n_e], rest[n_e + len(behind):n_e + len(behind) + n_o], rest[-1]
        kk = pl.program_id(2)

        @pl.when(kk == 0)
        def _():
            acc[...] = jnp.zeros_like(acc)

        acc[...] += _dot(a_ref[...], b_ref[...], dims)

        @pl.when(kk == nk - 1)
        def _():
            r = acc[...]
            res = epi(r, *[e[...] for e in ex]) if epi is not None else (r,)
            for o, v in zip(orefs, res):
                o[...] = v.astype(o.dtype)

    a_spec = pl.BlockSpec((bk, bm), lambda i, j, kk: (kk, i)) if ta else pl.BlockSpec((bm, bk), lambda i, j, kk: (i, kk))
    if b_chips and tb:
        per = k_cap // bk
        b_spec = pl.BlockSpec((None, bn, bk), lambda i, j, kk: (kk // per, j, kk % per))
    elif b_chips:
        per = n_cap // bn
        b_spec = pl.BlockSpec((None, bk, bn), lambda i, j, kk: (j // per, kk, j % per))
    else:
        first = 0 if b_cols is None else b_cols[0] // bn
        assert b_cols is None or (not tb and b_cols[0] % bn == 0)
        b_spec = pl.BlockSpec((bn, bk), lambda i, j, kk: (j, kk)) if tb else pl.BlockSpec((bk, bn), lambda i, j, kk: (kk, first + j))
    t_spec = pl.BlockSpec((bm, bn), lambda i, j, kk: (i, j))
    o_spec, o_shape = t_spec, (m, n)
    if out_chips:
        per_o = n_cap // bn
        o_spec, o_shape = pl.BlockSpec((None, bm, bn), lambda i, j, kk: (j // per_o, i, j % per_o)), (out_chips, m, n_cap)
    est = (_nbytes((bm, bk), a.dtype) + _nbytes((bk, bn), b.dtype) + sum(_nbytes((bm, bn), e.dtype) for e in extra)
           + sum(_nbytes((bm, bn), o) for o in outs)) * 2 + 2 * _nbytes((bm, bn), F32)
    res = pl.pallas_call(
        body, grid=(m // bm, n // bn, nk), in_specs=[a_spec, b_spec] + [t_spec] * n_e + [pl.BlockSpec(memory_space=pl.ANY)] * len(behind),
        out_specs=[o_spec] * n_o, out_shape=[_SDS(o_shape, o) for o in outs], scratch_shapes=[pltpu.VMEM((bm, bn), F32)], name=name,
        compiler_params=_params(("parallel", "parallel", "arbitrary"), est))(_hbm(a), _hbm(b), *[_hbm(e) for e in extra], *behind)
    return res[0] if n_o == 1 else res


def _add_to(acc, r):
    return (acc + r,)


def _alibi_bias(dilation):
    slopes = 2.0 ** (-8.0 * (np.arange(_HEADS) + 1) / _HEADS)
    i = np.arange(_ABLK)[:, None]
    j = np.arange(_ABLK)[None, :]
    cur = np.where(i - j >= 0, -slopes[:, None, None] * ((i - j) * dilation), _NEG)
    prev = np.where(j >= i, -slopes[:, None, None] * ((i - j + _ABLK) * dilation), _NEG)
    both = np.concatenate([prev, cur], axis=2)
    return jnp.asarray(both.reshape(_HEADS // 2, 2 * _ABLK, 2 * _ABLK), F32)


def _strided(a, d):
    return a.reshape(a.shape[0] // d, d * a.shape[1])


def _head(h):
    return slice(h * _HDIM, (h + 1) * _HDIM)


def _pair(pr):
    return slice(pr * _LANES, (pr + 1) * _LANES)


def _low_lanes(shape):
    return lax.broadcasted_iota(jnp.int32, shape, 1) < _HDIM


def _halves(v, low):
    z = jnp.zeros_like(v)
    return jnp.where(low, v, z), jnp.where(low, z, v)


def _no_prev_mask(first):
    return jnp.logical_and(first, lax.broadcasted_iota(jnp.int32, (2 * _ABLK, 2 * _ABLK), 1) < _ABLK)


def _lane_spec(nb):
    return pl.BlockSpec((_ABLK, _LANES), lambda r, j: (jnp.minimum(j, nb - 1), r))


def _expand_heads(v):
    low = _low_lanes((v.shape[0], _LANES))
    return jnp.concatenate([jnp.where(low, v[:, 2 * pr:2 * pr + 1], v[:, 2 * pr + 1:2 * pr + 2]) for pr in range(_HEADS // 2)], axis=1)


def _attn_specs(nb, n_parts):
    def cur(p):
        return pl.BlockSpec((_ABLK, _AW), lambda r, j: (jnp.minimum(j, nb - 1), r * n_parts + p))

    def prev(p):
        return pl.BlockSpec((_ABLK, _AW), lambda r, j: (jnp.clip(j - 1, 0, nb - 1), r * n_parts + p))
    return cur, prev


def _attn_fwd(qkv, dilation, *, name):
    t = qkv.shape[0]
    nb = t // dilation // _ABLK
    bias = _alibi_bias(dilation)
    scale = _HDIM ** -0.5

    def body(q_ref, kc_ref, kp_ref, vc_ref, vp_ref, b_ref, o_ref, l_ref):
        no_prev = _no_prev_mask(pl.program_id(1) == 0)
        low = _low_lanes((_ABLK, _LANES))
        l_ref[...] = jnp.zeros_like(l_ref)
        for pr in range(_HEADS // 2):
            sl = _pair(pr)
            k2 = jnp.concatenate([kp_ref[:, sl], kc_ref[:, sl]], axis=0)
            v2 = jnp.concatenate([vp_ref[:, sl], vc_ref[:, sl]], axis=0)
            q2 = jnp.concatenate(_halves(q_ref[:, sl], low), axis=0)
            s = jnp.where(no_prev, _NEG, _dot(q2, k2, _NT) * scale + b_ref[pr])
            m = jnp.max(s, axis=-1, keepdims=True)
            p = jnp.exp(s - m)
            den = jnp.sum(p, axis=-1, keepdims=True)
            o = _dot(p, v2) / den
            lse = m + jnp.log(den)
            l_ref[:, 2 * pr:2 * pr + 1] = lse[:_ABLK]
            l_ref[:, 2 * pr + 1:2 * pr + 2] = lse[_ABLK:]
            o_ref[:, sl] = jnp.where(low, o[:_ABLK], o[_ABLK:]).astype(o_ref.dtype)

    cur, prev = _attn_specs(nb, 3)
    cur1, _ = _attn_specs(nb, 1)
    bspec = pl.BlockSpec((_HEADS // 2, 2 * _ABLK, 2 * _ABLK), lambda r, j: (0, 0, 0))
    sv = _hbm(_strided(qkv, dilation))
    o, l = pl.pallas_call(
        body, grid=(dilation, nb), in_specs=[cur(0), cur(1), prev(1), cur(2), prev(2), bspec],
        out_specs=[cur1(0), _lane_spec(nb)],
        out_shape=[_SDS((t // dilation, dilation * _AW), _BF), _SDS((t // dilation, dilation * _LANES), F32)], name=name,
        compiler_params=_params(("parallel", "arbitrary"), 16 << 20))(sv, sv, sv, sv, sv, bias)
    return o, l.reshape(t, _LANES)


def _attn_bwd(qkv, do, ld, dilation, *, name):
    t = qkv.shape[0]
    nb = t // dilation // _ABLK
    bias = _alibi_bias(dilation)
    scale = _HDIM ** -0.5

    def body(q_ref, kc_ref, kp_ref, vc_ref, vp_ref, do_ref, ld_ref, b_ref, dq_ref, dk_ref, dv_ref, ck, cv):
        n = pl.program_id(1)

        @pl.when(n == 0)
        def _():
            ck[...] = jnp.zeros_like(ck)
            cv[...] = jnp.zeros_like(cv)

        @pl.when(n < nb)
        def _():
            low = _low_lanes((_ABLK, _LANES))
            no_prev = _no_prev_mask(n == 0)
            for pr in range(_HEADS // 2):
                sl = _pair(pr)
                k2 = jnp.concatenate([kp_ref[:, sl], kc_ref[:, sl]], axis=0)
                v2 = jnp.concatenate([vp_ref[:, sl], vc_ref[:, sl]], axis=0)
                q2 = jnp.concatenate(_halves(q_ref[:, sl], low), axis=0)
                do2 = jnp.concatenate(_halves(do_ref[:, sl], low), axis=0)
                lrow = jnp.concatenate([ld_ref[:, 2 * pr:2 * pr + 1], ld_ref[:, 2 * pr + 1:2 * pr + 2]], axis=0)
                dsum = jnp.concatenate([ld_ref[:, _HEADS + 2 * pr:_HEADS + 2 * pr + 1],
                                        ld_ref[:, _HEADS + 2 * pr + 1:_HEADS + 2 * pr + 2]], axis=0)
                p = jnp.exp(jnp.where(no_prev, _NEG, _dot(q2, k2, _NT) * scale + b_ref[pr]) - lrow)
                ds = (p * (_dot(do2, v2, _NT) - dsum)).astype(_BF)
                dq = _dot(ds, k2)
                dk2, dv2 = _dot(ds, q2, _TN), _dot(p, do2, _TN)
                dq_ref[:, sl] = (jnp.where(low, dq[:_ABLK], dq[_ABLK:]) * scale).astype(dq_ref.dtype)
                dk_ref[:, sl] = (ck[:, sl] + dk2[:_ABLK] * scale).astype(dk_ref.dtype)
                dv_ref[:, sl] = (cv[:, sl] + dv2[:_ABLK]).astype(dv_ref.dtype)
                ck[:, sl] = dk2[_ABLK:] * scale
                cv[:, sl] = dv2[_ABLK:]

        @pl.when(n == nb)
        def _():
            dk_ref[...] = ck[...].astype(dk_ref.dtype)
            dv_ref[...] = cv[...].astype(dv_ref.dtype)

    cur, prev = _attn_specs(nb, 3)
    cur1, prev1 = _attn_specs(nb, 1)
    bspec = pl.BlockSpec((_HEADS // 2, 2 * _ABLK, 2 * _ABLK), lambda r, j: (0, 0, 0))
    sv, dov, ldv = _hbm(_strided(qkv, dilation)), _hbm(do), _hbm(_strided(ld, dilation))
    dqkv = pl.pallas_call(
        body, grid=(dilation, nb + 1),
        in_specs=[cur(0), cur(1), prev(1), cur(2), prev(2), cur1(0), _lane_spec(nb), bspec],
        out_specs=[cur1(0), prev1(0), prev1(0)], out_shape=[_SDS(dov.shape, _BF)] * 3, name=name,
        scratch_shapes=[pltpu.VMEM((_ABLK, _AW), F32)] * 2,
        compiler_params=_params(("parallel", "arbitrary"), 16 << 20))(sv, sv, sv, sv, sv, dov, ldv, bias)
    return dqkv


def _ssd_in_specs(ch):
    return dict(
        xs=pl.BlockSpec((_CHUNK, _AW), lambda c: (ch(c), 0)),
        bc=pl.BlockSpec((_CHUNK, 2 * _GROUPS * _NSTATE), lambda c: (ch(c), _AW // (2 * _GROUPS * _NSTATE))),
        lane=pl.BlockSpec((_CHUNK, _LANES), lambda c: (ch(c), 0)),
        arow=pl.BlockSpec((_HEADS, 1, _CHUNK), lambda c: (0, 0, ch(c))),
        st=pl.BlockSpec((1, _HEADS // 2, _NSTATE, _LANES), lambda c: (ch(c), 0, 0, 0)),
    )


def _decay(a_col, a_row):
    i0 = lax.broadcasted_iota(jnp.int32, (_CHUNK, _CHUNK), 0)
    i1 = lax.broadcasted_iota(jnp.int32, (_CHUNK, _CHUNK), 1)
    return jnp.where(i0 >= i1, jnp.exp(a_col - a_row), 0.0), jnp.where(i1 >= i0, jnp.exp(a_row - a_col), 0.0)


def _rsum(v):
    return jnp.sum(v, axis=-1, keepdims=True)


def _ssd_fwd(act, dt, acum, a_row, *, name):
    t = act.shape[0]
    nc = t // _CHUNK
    sp = _ssd_in_specs(lambda c: c)
    gw = _GROUPS * _NSTATE

    def body(xs_ref, bc_ref, dt_ref, ac_ref, ar_ref, y_ref, sall_ref, st):
        @pl.when(pl.program_id(0) == 0)
        def _():
            st[...] = jnp.zeros_like(st)

        low = _low_lanes((_CHUNK, _LANES))
        for g in range(_GROUPS):
            bg = bc_ref[:, g * _NSTATE:(g + 1) * _NSTATE]
            cg = bc_ref[:, gw + g * _NSTATE:gw + (g + 1) * _NSTATE].astype(_BF)
            cb = _dot(cg, bg, _NT)
            for pr in range(g * _HPG // 2, (g + 1) * _HPG // 2):
                ha, hb = 2 * pr, 2 * pr + 1
                a_a, a_b = ac_ref[:, ha:ha + 1], ac_ref[:, hb:hb + 1]
                x = (xs_ref[:, _pair(pr)] * jnp.where(low, dt_ref[:, ha:ha + 1], dt_ref[:, hb:hb + 1])).astype(_BF)
                lm_a, _ = _decay(a_a, ar_ref[ha])
                lm_b, _ = _decay(a_b, ar_ref[hb])
                sv = st[pr]
                sall_ref[0, pr] = sv
                yd = _dot(jnp.concatenate([cb * lm_a, cb * lm_b], axis=0), x)
                yd = jnp.where(low, yd[:_CHUNK], yd[_CHUNK:])
                y_ref[:, _pair(pr)] = yd + jnp.where(low, jnp.exp(a_a), jnp.exp(a_b)) * _dot(cg, sv)
                al_a, al_b = jnp.min(a_a, axis=0, keepdims=True), jnp.min(a_b, axis=0, keepdims=True)
                upd = _dot(jnp.concatenate([bg * jnp.exp(al_a - a_a), bg * jnp.exp(al_b - a_b)], axis=1), x, _TN)
                st[pr] = jnp.where(low, jnp.exp(al_a), jnp.exp(al_b)) * sv + jnp.where(low, upd[:_NSTATE], upd[_NSTATE:])

    return pl.pallas_call(
        body, grid=(nc,), in_specs=[sp['xs'], sp['bc'], sp['lane'], sp['lane'], sp['arow']],
        out_specs=[sp['xs'], sp['st']], out_shape=[_SDS((t, _AW), F32), _SDS((nc, _HEADS // 2, _NSTATE, _LANES), F32)],
        scratch_shapes=[pltpu.VMEM((_HEADS // 2, _NSTATE, _LANES), F32)], name=name,
        compiler_params=_params(("arbitrary",), 16 << 20))(*[_hbm(a) for a in (act, act, dt, acum, a_row)])


def _ssd_bwd(act, dt, acum, a_row, sall, dy, *, name):
    t = act.shape[0]
    nc = t // _CHUNK
    sp = _ssd_in_specs(lambda c: nc - 1 - c)
    gw = _GROUPS * _NSTATE

    def body(xs_ref, bc_ref, dt_ref, ac_ref, ar_ref, sall_ref, dy_ref, dxs_ref, dbc_ref, ddt_ref, da_ref, dst):
        @pl.when(pl.program_id(0) == 0)
        def _():
            dst[...] = jnp.zeros_like(dst)

        ddt_ref[...] = jnp.zeros_like(ddt_ref)
        da_ref[...] = jnp.zeros_like(da_ref)
        row = lax.broadcasted_iota(jnp.int32, (_CHUNK, 1), 0)
        low = _low_lanes((_CHUNK, _LANES))
        for g in range(_GROUPS):
            bg = bc_ref[:, g * _NSTATE:(g + 1) * _NSTATE]
            bgb = bg.astype(_BF)
            cg = bc_ref[:, gw + g * _NSTATE:gw + (g + 1) * _NSTATE].astype(_BF)
            cb, cbt = _dot(cg, bgb, _NT), _dot(bgb, cg, _NT)
            dcb = jnp.zeros((_CHUNK, _CHUNK), F32)
            dbg = jnp.zeros((_CHUNK, _NSTATE), F32)
            dcg = jnp.zeros((_CHUNK, _NSTATE), F32)
            for pr in range(g * _HPG // 2, (g + 1) * _HPG // 2):
                heads = (2 * pr, 2 * pr + 1)
                a_cols = [ac_ref[:, h:h + 1] for h in heads]
                dt_pair = jnp.where(low, dt_ref[:, heads[0]:heads[0] + 1], dt_ref[:, heads[1]:heads[1] + 1])
                xsv = xs_ref[:, _pair(pr)]
                x = xsv * dt_pair
                xb = x.astype(_BF)
                xhs = _halves(xb, low)
                dyv = dy_ref[:, _pair(pr)]
                dyb = dyv.astype(_BF)
                dyhs = _halves(dyb, low)
                sv, dsv = sall_ref[0, pr], dst[pr]
                svb, dsb = sv.astype(_BF), dsv.astype(_BF)
                a_lasts = [jnp.min(a, axis=0, keepdims=True) for a in a_cols]
                e_pair = jnp.where(low, jnp.exp(a_cols[0]), jnp.exp(a_cols[1]))
                el_pair = jnp.where(low, jnp.exp(a_lasts[0]), jnp.exp(a_lasts[1]))
                yo = e_pair * _dot(cg, svb)
                decays = [_decay(a_cols[i], ar_ref[h]) for i, h in enumerate(heads)]
                gms, gmts = [cb * lm for lm, _ in decays], [cbt * lmt for _, lmt in decays]
                w_cols = [jnp.exp(a_lasts[i] - a_cols[i]) for i in range(2)]
                x2, dy2 = jnp.concatenate(xhs, axis=0), jnp.concatenate(dyhs, axis=0)
                bwd = _dot(jnp.concatenate([bg * w_cols[0], bg * w_cols[1]], axis=0), dsb)
                dxg = _dot(jnp.concatenate(gms, axis=1), dyb, _TN)
                dg2, dgt2, xds2 = _dot(dy2, xb, _NT), _dot(x2, dyb, _NT), _dot(x2, dsb, _NT)
                das = []
                for i in range(2):
                    rows_i = slice(i * _CHUNK, (i + 1) * _CHUNK)
                    dcb = dcb + dg2[rows_i] * decays[i][0]
                    dbg = dbg + w_cols[i] * xds2[rows_i]
                    das.append(_rsum(dg2[rows_i] * gms[i]) - _rsum(dgt2[rows_i] * gmts[i]))
                bwd = jnp.where(low, bwd[:_CHUNK], bwd[_CHUNK:])
                dx = jnp.where(low, dxg[:_CHUNK], dxg[_CHUNK:]) + bwd
                edy = (e_pair * dyv).astype(_BF)
                dcg = dcg + _dot(edy, svb, _NT)
                zs, yos, sds, dts = (_halves(v, low) for v in (x * bwd, dyv * yo, sv * dsv, dx * xsv))
                for i, h in enumerate(heads):
                    z = _rsum(zs[i])
                    da_last = jnp.sum(z, axis=0, keepdims=True) + jnp.exp(a_lasts[i]) * jnp.sum(_rsum(sds[i]), axis=0, keepdims=True)
                    da_ref[:, h:h + 1] = das[i] + _rsum(yos[i]) - z + jnp.where(row == _CHUNK - 1, da_last, 0.0)
                    ddt_ref[:, h:h + 1] = _rsum(dts[i])
                dxs_ref[:, _pair(pr)] = dx * dt_pair
                dst[pr] = el_pair * dsv + _dot(cg, edy, _TN)
            dbc_ref[:, g * _NSTATE:(g + 1) * _NSTATE] = dbg + _dot(dcb, cg, _TN)
            dbc_ref[:, gw + g * _NSTATE:gw + (g + 1) * _NSTATE] = dcg + _dot(dcb, bgb)

    ch = lambda c: nc - 1 - c
    wide = pl.BlockSpec((_CHUNK, 2 * gw), lambda c: (ch(c), 0))
    return pl.pallas_call(
        body, grid=(nc,), in_specs=[sp['xs'], sp['bc'], sp['lane'], sp['lane'], sp['arow'], sp['st'], sp['xs']],
        out_specs=[sp['xs'], wide, sp['lane'], sp['lane']],
        out_shape=[_SDS((t, _AW), F32), _SDS((t, 2 * gw), F32), _SDS((t, _LANES), F32), _SDS((t, _LANES), F32)],
        scratch_shapes=[pltpu.VMEM((_HEADS // 2, _NSTATE, _LANES), F32)], name=name,
        compiler_params=_params(("arbitrary",), 16 << 20))(*[_hbm(a) for a in (act, act, dt, acum, a_row, sall, dy)])


def _scan_rows(v, reverse):
    r = lax.broadcasted_iota(jnp.int32, v.shape, 0)
    for s in (1, 2, 4, 8, 16, 32, 64):
        if reverse:
            v = v + jnp.where(r < _CHUNK - s, pltpu.roll(v, _CHUNK - s, 0), 0.0)
        else:
            v = v + jnp.where(r >= s, pltpu.roll(v, s, 0), 0.0)
    return v


def _softplus(x):
    return jnp.maximum(x, 0.0) + jnp.log(1.0 + jnp.exp(-jnp.abs(x)))


def _sigmoid(x):
    return 1.0 / (1.0 + jnp.exp(-x))


def _silu(x):
    return x * _sigmoid(x)


def _dsilu(x):
    s = _sigmoid(x)
    return s * (1.0 + x * (1.0 - s))


def _lanes(a):
    return jnp.pad(a, (0, _LANES - a.shape[0])).reshape(1, _LANES)


def _layer_fwd(x, p, l):
    cch = p['conv_w'].shape[1]
    sv = {}
    h1 = _rms_fwd(x, p['ln1_g'], name=f"ln1_fwd_{l}")
    qkv = _mm(h1, p['w_in'], b_cols=(0, 3 * _AW), outs=(_BF,), name=f"in_proj_qkv_{l}")
    xbc = _mm(h1, p['w_in'], b_cols=(3 * _AW, cch), name=f"in_proj_xbc_{l}")
    zdt = _mm(h1, p['w_in'], b_cols=(3 * _AW + cch, _AW + _LANES), name=f"in_proj_zdt_{l}")
    z, dt_raw = (zdt, _AW, 0), (zdt, _LANES, _AW // _LANES)

    outs = []
    for dil in _DILATIONS:
        outs += _attn_fwd(qkv, dil, name=f"attn_fwd_d{dil}_{l}")

    tile = 2 * _ABLK
    perms = [_perm(d, tile) for d in _DILATIONS[1:]]

    def combine(o1, l1, o2, l2, o3, l3, p2, p3):
        m = jnp.maximum(jnp.maximum(l1, l2), l3)
        e1, e2, e3 = jnp.exp(l1 - m), jnp.exp(l2 - m), jnp.exp(l3 - m)
        tot = e1 + e2 + e3
        mixed = sum(_expand_heads(e / tot) * o for e, o in ((e1, o1.astype(F32)), (e2, _unstride(o2, p2)), (e3, _unstride(o3, p3))))
        return mixed, m + jnp.log(tot)
    outs = [a if i % 2 or i == 0 else ("strided", a, _DILATIONS[i // 2]) for i, a in enumerate(outs)]
    attn, lse = _rows(combine, outs, perms, [(_AW, F32), (_LANES, F32)], tile=tile, name=f"attn_combine_{l}")
    attn_n = _rms_fwd(attn, p['attn_norm_g'], name=f"attn_norm_fwd_{l}")

    def conv(u0, before, w, b):
        u1, u2, u3 = _shifted(u0, before, True)
        return _silu(w[0:1] * u3 + w[1:2] * u2 + w[2:3] * u1 + w[3:4] * u0 + b)
    act = _rows(conv, [xbc], [p['conv_w'], p['conv_b'].reshape(1, cch)], [(cch, F32)], halos=[(xbc, -1)], tile=_tile_for(cch),
                name=f"conv_fwd_{l}")[0]

    def dtf(raw, bias, alog):
        dt = _softplus(raw + bias)
        return dt, _scan_rows(dt * -jnp.exp(alog), False)
    dt, acum = _rows(dtf, [dt_raw], [_lanes(p['dt_bias']), _lanes(p['a_log'])], [(_LANES, F32), (_LANES, F32)],
                     tile=_CHUNK, name=f"dt_fwd_{l}")
    a_row = acum[:, :_HEADS].T[:, None, :]
    y_ssd, sall = _ssd_fwd(act, dt, acum, a_row, name=f"ssd_fwd_{l}")
    dskip = jnp.repeat(p['d_skip'], _HDIM).reshape(1, _AW)
    xs = (act, _AW, 0)

    def gate(y, xs, z, dsk):
        return (y + dsk * xs) * _silu(z)
    y2 = _rows(gate, [y_ssd, xs, z], [dskip], [(_AW, F32)], tile=_tile_for(_AW), name=f"gate_fwd_{l}")[0]
    y_n = _rms_fwd(y2, p['ssd_norm_g'], groups=_GROUPS, name=f"ssd_norm_fwd_{l}")

    mix = jnp.concatenate([attn_n, y_n], axis=1)
    sv.update(x=x, h1=h1, qkv=qkv, zdt=zdt, xbc=xbc, attn=attn, lse=lse, act=act, dt=dt, acum=acum, a_row=a_row,
              sall=sall, y_ssd=y_ssd, dskip=dskip, y2=y2, mix=mix)
    return mix, sv


def _layer_fwd_mlp(p, sv, l):
    x2 = _mm(sv['mix'], p['w_out'], extra=(sv['x'],), epi=_add_to, name=f"out_proj_{l}")
    h2 = _rms_fwd(x2, p['ln2_g'], name=f"ln2_fwd_{l}")
    a = _mm(h2, p['w_mlp_in'], b_chips=_CHIPS, epi=lambda acc: (jnp.square(jnp.maximum(acc, 0.0)),), outs=(_BF,), name=f"mlp_in_{l}")
    x3 = _mm(a, p['w_mlp_out'], extra=(x2,), epi=_add_to, name=f"mlp_out_{l}")
    sv.update(x2=x2, h2=h2, a=a)
    return x3


def _layer_bwd(dx3, p, sv, l, send, after):
    cch = p['conv_w'].shape[1]
    g = {}
    dx3b = dx3.astype(_BF)
    du = _mm(dx3b, p['w_mlp_out'], tb=True, extra=(sv['a'],), outs=(_BF,), after=after,
             epi=lambda acc, a: (acc * 2.0 * jnp.sqrt(a.astype(F32)),), name=f"mlp_out_dx_{l}")
    g['w_mlp_out'] = _mm(sv['a'], dx3b, ta=True, outs=(_BF,), name=f"mlp_out_dw_{l}")
    g['w_mlp_in'] = _mm(sv['h2'], du, ta=True, out_chips=_CHIPS, outs=(_BF,), name=f"mlp_in_dw_{l}")
    sent = send(('w_mlp_out', 'w_mlp_in'), g)
    dh2 = _mm(du, p['w_mlp_in'], tb=True, b_chips=_CHIPS, after=sent, name=f"mlp_in_dx_{l}")
    dx2, g['ln2_g'] = _rms_bwd(sv['x2'], dh2, p['ln2_g'], dx3, name=f"ln2_bwd_{l}")
    dx2b = dx2.astype(_BF)
    dmix = _mm(dx2b, p['w_out'], tb=True, name=f"out_proj_dx_{l}")
    g['w_out'] = _mm(sv['mix'], dx2b, ta=True, outs=(_BF,), name=f"out_proj_dw_{l}")
    after_out = send(('w_out',), g)

    tile = 2 * _ABLK
    perms = [_perm(d, tile) for d in _DILATIONS[1:]]

    def norm_bwd(attn, dy, lse, gn, p2, p3):
        dattn, dgn = _rms_bwd_tile(attn, dy, gn, 1)
        prod, low = dattn * attn, _low_lanes((attn.shape[0], _LANES))
        lane = lax.broadcasted_iota(jnp.int32, lse.shape, 1)
        ld = jnp.where(lane < _HEADS, lse, 0.0)
        for pr in range(_HEADS // 2):
            for i, part in enumerate(_halves(prod[:, _pair(pr)], low)):
                ld = jnp.where(lane == _HEADS + 2 * pr + i, _rsum(part), ld)
        return dattn, _stride(dattn, p2, _DILATIONS[1]), _stride(dattn, p3, _DILATIONS[2]), ld, dgn
    *dos, ld, gn_sum = _rows(norm_bwd, [sv['attn'], (dmix, _AW, 0), sv['lse']], [p['attn_norm_g'].reshape(1, _AW)] + perms,
                             [(_AW, _BF)] + [(_AW, _BF, d) for d in _DILATIONS[1:]] + [(_LANES, F32)], [_AW], after=after_out,
                             tile=tile, name=f"attn_norm_bwd_{l}")
    g['attn_norm_g'] = gn_sum.sum(axis=0)
    parts = [_attn_bwd(sv['qkv'], do, ld, dil, name=f"attn_bwd_d{dil}_{l}") for do, dil in zip(dos, _DILATIONS)]

    def branch_sum(*t):
        parts_, (p2, p3) = t[:9], t[9:]
        t = [a.astype(F32) for a in parts_[:3]] + [_unstride(a, p2) for a in parts_[3:6]] + [_unstride(a, p3) for a in parts_[6:]]
        return jnp.concatenate([t[i] + t[3 + i] + t[6 + i] for i in range(3)], axis=1)
    branch_ins = list(parts[0]) + [("strided", a, d) for pr, d in zip(parts[1:], _DILATIONS[1:]) for a in pr]
    w_all = 3 * _AW + cch + _AW + _LANES
    dproj = _rows(branch_sum, branch_ins, perms, [(3 * _AW, _BF)], into=(None, w_all, 0), tile=tile, name=f"attn_bwd_sum_{l}")[0]

    xs, z, dt_raw = (sv['act'], _AW, 0), (sv['zdt'], _AW, 0), (sv['zdt'], _LANES, _AW // _LANES)

    def gate_bwd(y2, dy, y, xs, z, dsk, gn):
        dy2, dgn = _rms_bwd_tile(y2, dy, gn, _GROUPS)
        dy1 = dy2 * _silu(z)
        return dy1, dsk * dy1, dy2 * (y + dsk * xs) * _dsilu(z), dy1 * xs, dgn
    dy1, dxs_skip, dz, dsk_sum, gn_sum = _rows(
        gate_bwd, [sv['y2'], (dmix, _AW, 1), sv['y_ssd'], xs, z], [sv['dskip'], p['ssd_norm_g'].reshape(1, _AW)],
        [(_AW, F32), (_AW, F32), (_AW, _BF)], [_AW, _AW], tile=128, name=f"gate_bwd_{l}")
    g['ssd_norm_g'] = gn_sum.sum(axis=0)
    g['d_skip'] = dsk_sum.sum(axis=0).reshape(_HEADS, _HDIM).sum(axis=1)
    dxs, dbc, ddt, da = _ssd_bwd(sv['act'], sv['dt'], sv['acum'], sv['a_row'], sv['sall'], dy1, name=f"ssd_bwd_{l}")

    def dtb(da, ddtx, raw, dt, dz, bias, alog):
        a = -jnp.exp(alog)
        dda = _scan_rows(da, True)
        draw = (dda * a + ddtx) * _sigmoid(raw + bias)
        return jnp.concatenate([dz, draw.astype(dz.dtype)], axis=1), draw, dda * dt * a
    dproj, dbias, dalog = _rows(dtb, [da, ddt, dt_raw, sv['dt'], dz], [_lanes(p['dt_bias']), _lanes(p['a_log'])],
                                [(_AW + _LANES, _BF)], [_LANES, _LANES], into=(dproj, w_all, (3 * _AW + cch) // (_AW + _LANES)),
                                tile=_CHUNK, name=f"dt_bwd_{l}")
    g['dt_bias'], g['a_log'] = dbias.sum(axis=0)[:_HEADS], dalog.sum(axis=0)[:_HEADS]
    def conv_bwd1(u0, dxs, dbc, dxk, before, w, b):
        u1, u2, u3 = _shifted(u0, before, True)
        pre = w[0:1] * u3 + w[1:2] * u2 + w[2:3] * u1 + w[3:4] * u0 + b
        dp = jnp.concatenate([dxs + dxk, dbc], axis=1) * _dsilu(pre)
        return dp, dp * u3, dp * u2, dp * u1, dp * u0, dp
    dpre, *dws = _rows(conv_bwd1, [sv['xbc'], dxs, dbc, dxs_skip], [p['conv_w'], p['conv_b'].reshape(1, cch)], [(cch, F32)],
                       [cch] * 5, halos=[(sv['xbc'], -1)], tile=128, name=f"conv_bwd_pre_{l}")
    g['conv_w'] = jnp.stack([dws[i].sum(axis=0) for i in range(_CONV_K)])
    g['conv_b'] = dws[4].sum(axis=0)

    def conv_bwd2(p0, after_, w):
        p1, p2, p3 = _shifted(p0, after_, False)
        return w[3:4] * p0 + w[2:3] * p1 + w[1:2] * p2 + w[0:1] * p3
    dproj = _rows(conv_bwd2, [dpre], [p['conv_w']], [(cch, _BF)], halos=[(dpre, 1)], into=(dproj, w_all, 3 * _AW // cch),
                  tile=_tile_for(cch), name=f"conv_bwd_in_{l}")[0]
    g_all = _mm(sv['h1'], dproj, ta=True, outs=(_BF,), name=f"in_proj_dw_{l}")
    z0 = 3 * _AW + cch
    g['w_in'] = jnp.concatenate([g_all[:, :3 * _AW], g_all[:, z0:z0 + _AW], g_all[:, 3 * _AW:z0], g_all[:, z0 + _AW:z0 + _AW + _HEADS]], axis=1)
    sent = send(('w_in',), g)
    for n in _BIG:
        del g[n]
    dh1 = _mm(dproj, p['w_in'], tb=True, after=sent, name=f"in_proj_dx_{l}")
    dx, g['ln1_g'] = _rms_bwd(sv['x'], dh1, p['ln1_g'], dx2, name=f"ln1_bwd_{l}")
    return dx, g


def _loss_bwd(x, g, tgt):
    w = x.shape[1]
    tile = _tile_for(w)

    def fn(x, tgt, g):
        r = _rstd(x)
        xh = x * r
        e = xh * g - tgt
        gd = e * (g / w)
        dx = r * (gd - xh * jnp.mean(gd * xh, axis=-1, keepdims=True))
        rowloss = 0.5 * jnp.mean(e * e, axis=-1, keepdims=True)
        return dx, (e / w) * xh, jnp.broadcast_to(rowloss, (tile, _LANES))
    dx, dg, ls = _rows(fn, [x, tgt], [g.reshape(1, w)], [(w, F32)], [w, _LANES], tile=tile, name="loss_head")
    return dx, dg.sum(axis=0), ls[:, 0].sum()


def _adamw_math(w, g, m, v):
    m2 = _B1 * m + (1.0 - _B1) * g
    v2 = _B2 * v + (1.0 - _B2) * jnp.square(g)
    m_hat = m2 / (1.0 - _B1 ** _STEP)
    v_hat = v2 / (1.0 - _B2 ** _STEP)
    return -_LR * (m_hat / (jnp.sqrt(v_hat) + _AEPS) + _WD * w), m2, v2


def _adamw(w, g, m, v, *, name):
    width = w.shape[-1]
    flat = [a.reshape(-1, width) for a in (w, g, m, v)]
    tile = _pick(flat[0].shape[0], (_tile_for(width), 32, 8))
    res = _rows(_adamw_math, flat, [], [(width, F32)] * 3, tile=tile, name=name)
    return [r.reshape(w.shape) for r in res]


_HBM = pl.BlockSpec(memory_space=pltpu.HBM)


def _place():
    x, y, c = lax.axis_index("x"), lax.axis_index("y"), lax.axis_index("c")
    other_chips = [(1 - x, y), (x, 1 - y), (1 - x, 1 - y)]
    return x, y, c, other_chips


def _remote(src, dst, sems, i, dev):
    return pltpu.make_async_remote_copy(src_ref=src, dst_ref=dst, send_sem=sems[0].at[i], recv_sem=sems[1].at[i],
                                        device_id=dev, device_id_type=_MESH)


def _exchange8(v, *, reduce, after=None, name):
    r, w = v.shape
    behind = [] if after is None else [after]

    def body(v_ref, *rest):
        all_ref, rest = rest[len(behind)], rest[len(behind) + 1:]
        sems = rest[-2:]
        x, y, c, _ = _place()
        me = 4 * x + 2 * y + c
        all_ref[me] = v_ref[...]
        flips = [((d >> 2) & 1, (d >> 1) & 1, d & 1) for d in range(1, 8)]
        sends = [_remote(v_ref, all_ref.at[me], sems, i, (x ^ fx, y ^ fy, c ^ fc)) for i, (fx, fy, fc) in enumerate(flips)]
        for cp in sends:
            cp.start()
        for i, (fx, fy, fc) in enumerate(flips):
            _remote(v_ref, all_ref.at[me ^ (4 * fx + 2 * fy + fc)], sems, i, (x ^ fx, y ^ fy, c ^ fc)).wait_recv()
        for cp in sends:
            cp.wait_send()
        if reduce:
            acc = all_ref[0]
            for s in range(1, 8):
                acc = acc + all_ref[s]
            rest[0][...] = acc

    vm = pl.BlockSpec(memory_space=pltpu.VMEM)
    out_shape = [_SDS((8, r, w), v.dtype)] + ([_SDS((r, w), v.dtype)] if reduce else [])
    res = pl.pallas_call(body, in_specs=[vm] + [_ANY] * len(behind), out_specs=[vm] * len(out_shape), out_shape=out_shape, name=name,
                         scratch_shapes=[pltpu.SemaphoreType.DMA((7,)), pltpu.SemaphoreType.DMA((7,))],
                         compiler_params=pltpu.CompilerParams(vmem_limit_bytes=int(32 << 20)))(v, *behind)
    return res[1] if reduce else res[0]


_SEM = pl.BlockSpec(memory_space=pltpu.SEMAPHORE)
_ANY = pl.BlockSpec(memory_space=pl.ANY)
_EFFECT = pltpu.SideEffectType.DATAFLOW_SIDE_EFFECTING


def _send_start(name, srcs, land_shapes, plan, n_sends, after):
    ns, nl = len(srcs), len(land_shapes)
    zones = [_hbm(lax.empty(s.shape, s.dtype)) if isinstance(s, _SDS) else s for s in land_shapes]

    def body(*refs):
        ins, lands, sems = refs[:ns], refs[ns:ns + nl], refs[ns + nl + 1:ns + nl + 3]
        x, y, c, chips = _place()
        for i, (s, d, dev) in enumerate(plan(x, y, c, chips, ins, lands)[0]):
            _remote(s, d, sems, i, dev).start()
        refs[-1][...] = jnp.zeros_like(refs[-1])

    sem = pltpu.SemaphoreType.DMA((n_sends,))
    res = pl.pallas_call(
        body, name=name, in_specs=[_HBM] * (ns + nl) + [_ANY],
        out_shape=(sem, sem, *[pltpu.HBM(s.shape, s.dtype) for s in land_shapes], _SDS((8, _LANES), F32)),
        out_specs=(_SEM, _SEM, *[_HBM] * nl, pl.BlockSpec(memory_space=pltpu.VMEM)),
        input_output_aliases={ns + i: 2 + i for i in range(nl)},
        compiler_params=pltpu.CompilerParams(has_side_effects=_EFFECT))(
            *[_hbm(s) for s in srcs], *zones, after)
    return dict(sems=res[:2], srcs=srcs, lands=res[2:2 + nl], plan=plan), res[-1]


def _send_wait(name, h, after):
    ns, nl = len(h['srcs']), len(h['lands'])

    def body(*refs):
        ins, lands, sems = refs[:ns], refs[ns:ns + nl], refs[ns + nl:ns + nl + 2]
        x, y, c, chips = _place()
        sends, landings = h['plan'](x, y, c, chips, ins, lands)
        for i, (s, d, dev) in enumerate(sends):
            _remote(s, d, sems, i, dev).wait_send()
        for i, d in enumerate(landings):
            _remote(d, d, sems, i, sends[i][2]).wait_recv()

    return pl.pallas_call(
        body, name=name, in_specs=[_HBM] * (ns + nl) + [_SEM, _SEM, _ANY],
        out_shape=tuple(pltpu.HBM(a.shape, a.dtype) for a in h['lands']), out_specs=tuple([_HBM] * nl),
        input_output_aliases={ns + i: i for i in range(nl)},
        compiler_params=pltpu.CompilerParams(has_side_effects=_EFFECT))(
            *[_hbm(s) for s in h['srcs']], *h['lands'], *h['sems'], after)


def _gather_plan(items):
    def plan(x, y, c, chips, ins, lands):
        k = 2 * x + y
        sends = [(ins[si].at[l], lands[t].at[k], (px, py, c)) for t, (si, l) in enumerate(items) for px, py in chips]
        return sends, [lands[t].at[2 * px + py] for t in range(len(items)) for px, py in chips]
    return plan


_FLIPS = [((d >> 2) & 1, (d >> 1) & 1, d & 1) for d in range(1, 8)]


def _reduce_plan(halves):
    def plan(x, y, c, chips, ins, lands):
        sends, landings = [], []
        for t, hf in enumerate(halves):
            for i, (fx, fy, fc) in enumerate(_FLIPS):
                px, py, pc = x ^ fx, y ^ fy, c ^ fc
                sends.append((ins[t].at[2 * px + py, pl.ds(pc * hf, hf)], lands[t].at[i], (px, py, pc)))
                landings.append(lands[t].at[i])
        return sends, landings
    return plan


def _swap(name, srcs, out_shapes, plan, n_sends):
    n = len(srcs)

    def body(*refs):
        ins, outs, sems = refs[:n], refs[n:n + len(out_shapes)], refs[-2:]
        x, y, c, chips = _place()
        sends, landings = plan(x, y, c, chips, ins, outs)
        out = [_remote(s, d, sems, i, dev) for i, (s, d, dev) in enumerate(sends)]
        for cp in out:
            cp.start()
        for i, d in enumerate(landings):
            _remote(d, d, sems, i, sends[i][2]).wait_recv()
        for cp in out:
            cp.wait_send()

    return pl.pallas_call(
        body, in_specs=[_HBM] * n, out_specs=[_HBM] * len(out_shapes), out_shape=out_shapes, name=name,
        scratch_shapes=[pltpu.SemaphoreType.DMA((n_sends,)), pltpu.SemaphoreType.DMA((n_sends,))])(*srcs)


def _sum_owned(grads, landed, c, k, names):
    def sum8(*parts):
        acc = parts[0].astype(F32)
        for p in parts[1:]:
            acc = acc + p.astype(F32)
        return acc
    outs = []
    for g, got, name in zip(grads, landed, names):
        hf, b = got.shape[1:]
        own = lax.dynamic_slice_in_dim(lax.dynamic_index_in_dim(g, k, axis=0, keepdims=False), c * hf, hf, axis=0)
        outs.append(_rows(sum8, [own] + [("slot", got, i) for i in range(len(_FLIPS))], [], [(b, F32)],
                          tile=_pick(hf, (_tile_for(b), 32)), name=f"grad_sum_{name}")[0])
    return outs


def _share_halves(mine, *, name):
    n = len(mine)

    def plan(x, y, c_, chips, ins, outs):
        return [(ins[t], outs[t], (x, y, 1 - c_)) for t in range(n)], [outs[t] for t in range(n)]
    return _swap(name, mine, [_SDS(h.shape, F32) for h in mine], plan, n)


def _adamw_owned(w, mine, theirs, m, v, c, *, name):
    depth, a, b = w.shape
    half = a // 2
    tile = _pick(half, (_tile_for(b), 32, 8))
    nh = half // tile

    def blocks_of(l):
        return lambda i: (jnp.clip(i - 2 * nh * l, 0, 2 * nh - 1) % nh, 0)

    def fn(w, m, v, *rest):
        halves, cflag = rest[:-1], rest[-1]
        step = pl.program_id(0)
        is_mine = cflag[0:1, 0:1] == ((step // nh) % 2).astype(F32)
        g = jnp.where(is_mine, halves[0], halves[1])
        for l in range(1, depth):
            g = jnp.where(step >= 2 * nh * l, jnp.where(is_mine, halves[2 * l], halves[2 * l + 1]), g)
        return (g,) + _adamw_math(w, g, m, v)
    ins = [a_.reshape(depth * a, b) for a_ in (w, m, v)]
    ins += [(h, b, blocks_of(l)) for l in range(depth) for h in (mine[l], theirs[l])]
    res = _rows(fn, ins, [jnp.full((1, _LANES), c, F32)], [(b, F32)] * 4, tile=tile, name=name)
    return [r.reshape(w.shape) for r in res]


_BIG = ("w_in", "w_out", "w_mlp_in", "w_mlp_out")
_SMALL = ("ln1_g", "conv_b", "dt_bias", "a_log", "d_skip", "attn_norm_g", "ssd_norm_g", "ln2_g", "final_norm_g")
_ORDER = ("ln1_g", "w_in", "conv_w", "conv_b", "dt_bias", "a_log", "d_skip", "attn_norm_g", "ssd_norm_g", "w_out", "ln2_g",
          "w_mlp_in", "w_mlp_out", "final_norm_g")


def _pack(parts, rows):
    flat = jnp.concatenate([p.reshape(-1) for p in parts])
    return jnp.pad(flat, (0, rows * _LANES - flat.shape[0])).reshape(rows, _LANES)


def _unpack(buf, like):
    flat, out, o = buf.reshape(-1), [], 0
    for p in like:
        out.append(flat[o:o + p.size].reshape(p.shape))
        o += p.size
    return out


def kernel(x, ln1_g, w_in, conv_w, conv_b, dt_bias, a_log, d_skip, attn_norm_g, ssd_norm_g, w_out, ln2_g, w_mlp_in, w_mlp_out, final_norm_g, loss_target, m_ln1_g, m_w_in, m_conv_w, m_conv_b, m_dt_bias, m_a_log, m_d_skip, m_attn_norm_g, m_ssd_norm_g, m_w_out, m_ln2_g, m_w_mlp_in, m_w_mlp_out, m_final_norm_g, v_ln1_g, v_w_in, v_conv_w, v_conv_b, v_dt_bias, v_a_log, v_d_skip, v_attn_norm_g, v_ssd_norm_g, v_w_out, v_ln2_g, v_w_mlp_in, v_w_mlp_out, v_final_norm_g):
    w = dict(ln1_g=ln1_g, w_in=w_in, conv_w=conv_w, conv_b=conv_b, dt_bias=dt_bias, a_log=a_log, d_skip=d_skip,
             attn_norm_g=attn_norm_g, ssd_norm_g=ssd_norm_g, w_out=w_out, ln2_g=ln2_g, w_mlp_in=w_mlp_in, w_mlp_out=w_mlp_out,
             final_norm_g=final_norm_g)
    m = dict(ln1_g=m_ln1_g, w_in=m_w_in, conv_w=m_conv_w, conv_b=m_conv_b, dt_bias=m_dt_bias, a_log=m_a_log, d_skip=m_d_skip,
             attn_norm_g=m_attn_norm_g, ssd_norm_g=m_ssd_norm_g, w_out=m_w_out, ln2_g=m_ln2_g, w_mlp_in=m_w_mlp_in,
             w_mlp_out=m_w_mlp_out, final_norm_g=m_final_norm_g)
    v = dict(ln1_g=v_ln1_g, w_in=v_w_in, conv_w=v_conv_w, conv_b=v_conv_b, dt_bias=v_dt_bias, a_log=v_a_log, d_skip=v_d_skip,
             attn_norm_g=v_attn_norm_g, ssd_norm_g=v_ssd_norm_g, w_out=v_w_out, ln2_g=v_ln2_g, w_mlp_in=v_w_mlp_in,
             w_mlp_out=v_w_mlp_out, final_norm_g=v_final_norm_g)
    depth, d_model = ln1_g.shape
    n_chips = 4
    c = lax.axis_index("c")
    chip = 2 * lax.axis_index("x") + lax.axis_index("y")
    in_proj = w_in.shape[2] * n_chips
    cch = conv_w.shape[2] * n_chips
    zdt_pad = _LANES - _HEADS

    cw = _exchange8(conv_w.reshape(depth * _CONV_K, -1), reduce=False, name="gather_conv_w")[0::2]
    conv_full = cw.reshape(n_chips, depth, _CONV_K, -1).transpose(1, 2, 0, 3).reshape(depth, _CONV_K, cch)
    own = [w[n].astype(_BF) for n in _BIG]
    is_own = (jnp.arange(n_chips) == chip).reshape(n_chips, 1, 1)

    def start_gather(tag, items, after):
        lands = [_SDS((n_chips, *own[i].shape[1:]), _BF) for i, _ in items]
        return _send_start(f"gather_start_{tag}", own, lands, _gather_plan(items), 3 * len(items), after)

    def finish_gather(tag, handle, items, after):
        landed = _send_wait(f"gather_wait_{tag}", handle, after)
        return {_BIG[i]: jnp.where(is_own, own[i][l][None], g) for (i, l), g in zip(items, landed)}

    def layer_weights(l, blocks):
        p = {}
        if 'w_in' in blocks:
            full_in = blocks['w_in'].transpose(1, 0, 2).reshape(d_model, in_proj)
            p['w_in'] = jnp.concatenate([full_in[:, :3 * _AW], full_in[:, 4 * _AW:4 * _AW + cch], full_in[:, 3 * _AW:4 * _AW],
                                         full_in[:, 4 * _AW + cch:], jnp.zeros((d_model, zdt_pad), _BF)], axis=1)
        if 'w_out' in blocks:
            p['w_out'] = blocks['w_out'].reshape(-1, d_model)
            p['w_mlp_in'] = blocks['w_mlp_in']
            p['w_mlp_out'] = blocks['w_mlp_out'].reshape(-1, d_model)
        return p

    groups = dict(a=[(0, 0)], b=[(1, 0), (2, 0), (3, 0)], c=[(0, 1)], d=[(1, 1), (2, 1), (3, 1)])
    handles, token = {}, conv_full

    half_in = own[0].shape[1] // 2

    def rows_of(ref, who):
        return ref.at[pl.ds(who * half_in, half_in)]

    def plan_a(x_, y_, c_, chips, ins, lands):
        k = 2 * x_ + y_
        sends = [(rows_of(ins[0].at[0], c_), rows_of(lands[0].at[k], c_), (px, py, c_)) for px, py in chips]
        return sends, [rows_of(lands[0].at[2 * px + py], c_) for px, py in chips]

    def plan_pass(x_, y_, c_, chips, ins, lands):
        sends = [(rows_of(lands[0].at[2 * px + py], c_),) * 2 + ((x_, y_, 1 - c_),) for px, py in chips]
        return sends, [rows_of(lands[0].at[2 * px + py], 1 - c_) for px, py in chips]
    handles["a"], token = _send_start("gather_start_a", own[:1], [_SDS((n_chips, *own[0].shape[1:]), _BF)], plan_a, 3, token)
    for tag, items in list(groups.items())[1:]:
        handles[tag], token = start_gather(tag, items, token)
    landed = _send_wait("gather_land_a", handles["a"], token)
    handles["a"], token = _send_start("gather_pass_a", [], landed, plan_pass, 3, landed[0])
    layers = [{n: w[n][l] for n in _SMALL[:-1]} for l in range(depth)]
    for l in range(depth):
        layers[l]['conv_w'] = conv_full[l]

    layers[0].update(layer_weights(0, finish_gather("a", handles["a"], groups["a"], token)))
    mix, sv0 = _layer_fwd(x[0], layers[0], 0)
    layers[0].update(layer_weights(0, finish_gather("b", handles["b"], groups["b"], mix)))
    h = _layer_fwd_mlp(layers[0], sv0, 0)
    layers[1].update(layer_weights(1, finish_gather("c", handles["c"], groups["c"], h)))
    mix, sv1 = _layer_fwd(h, layers[1], 1)
    layers[1].update(layer_weights(1, finish_gather("d", handles["d"], groups["d"], mix)))
    h = _layer_fwd_mlp(layers[1], sv1, 1)
    saved = [sv0, sv1]

    def by_chip(g, name):
        if name == "w_mlp_in":
            return g
        if name == "w_in":
            return g.reshape(d_model, n_chips, -1).transpose(1, 0, 2)
        return g.reshape(n_chips, -1, d_model)

    pending = []

    def sender(l):
        def send(names, g):
            srcs = [by_chip(g[n], n) for n in names]
            halves = [s.shape[1] // 2 for s in srcs]
            lands = [_SDS((len(_FLIPS), hf, s.shape[2]), _BF) for s, hf in zip(srcs, halves)]
            handle, tok = _send_start(f"grad_start_{names[-1]}_{l}", srcs, lands, _reduce_plan(halves), len(_FLIPS) * len(srcs), srcs[0])
            pending.append((l, names, srcs, handle))
            return tok
        return send

    dx, g_final, loss_part = _loss_bwd(h, final_norm_g, loss_target[0])
    grads, after = [None] * depth, None
    for l in reversed(range(depth)):
        dx, grads[l] = _layer_bwd(dx, layers[l], saved[l], l, sender(l), after)
        after = dx
    landed_of = {}
    sent_in = {(n, l): (gi, j) for gi, (l, names, _, _) in enumerate(pending) for j, n in enumerate(names)}

    def landed_for(gi, after):
        if gi not in landed_of:
            l, names, _, handle = pending[gi]
            landed_of[gi] = _send_wait(f"grad_wait_{names[-1]}_{l}", handle, after)
        return landed_of[gi]

    red, delta, new_m, new_v = {}, {}, {}, {}
    after = dx
    for n in ("w_mlp_out", "w_mlp_in", "w_out", "w_in"):
        mine = []
        for l in range(depth):
            gi, j = sent_in[(n, l)]
            got = landed_for(gi, after)[j]
            mine.append(_sum_owned([pending[gi][2][j]], [got], c, chip, [f"{n}_{l}"])[0])
        theirs = _share_halves(mine, name=f"grad_share_{n}")
        red[n], delta[n], new_m[n], new_v[n] = _adamw_owned(w[n], mine, theirs, m[n], v[n], c, name=f"adamw_{n}")
        after = delta[n]

    small = {n: jnp.stack([grads[l][n] for l in range(depth)]) for n in _SMALL[:-1] + ("conv_w",)}
    small["final_norm_g"] = g_final
    parts = [loss_part.reshape(1)] + [small[n] for n in _SMALL + ("conv_w",)]
    rows = -(-sum(p.size for p in parts) // 1024) * 8
    tot = _unpack(_exchange8(_pack(parts, rows), reduce=True, after=red[_BIG[0]], name="allreduce_small"), parts)
    loss = tot[0][0]
    red.update(zip(_SMALL + ("conv_w",), tot[1:]))
    red["conv_w"] = lax.dynamic_index_in_dim(red["conv_w"].reshape(depth, _CONV_K, n_chips, -1), chip, axis=2, keepdims=False)

    names = _SMALL + ("conv_w",)
    like = [w[n] for n in names]
    srows = -(-sum(p.size for p in like) // 1024) * 8
    res = _adamw(*[_pack([d[n] for n in names], srows) for d in (w, red, m, v)], name="adamw_small")
    for dst, buf in zip((delta, new_m, new_v), res):
        dst.update(zip(names, _unpack(buf, like)))
    return (loss, dx[None], *[red[n] for n in _ORDER], *[delta[n] for n in _ORDER], *[new_m[n] for n in _ORDER],
            *[new_v[n] for n in _ORDER])
```

```python
import numpy as np
import jax
import jax.numpy as jnp
from jax import lax
from jax.experimental import pallas as pl
from jax.experimental.pallas import tpu as pltpu

F32 = jnp.float32
_BF = jnp.bfloat16
_NEG = -1e30
_EPS = 1e-5
_HEADS = 16
_HDIM = 64
_AW = _HEADS * _HDIM
_ABLK = 128
_DILATIONS = (1, 4, 16)
_CHUNK = 128
_NSTATE = 128
_GROUPS = 2
_HPG = _HEADS // _GROUPS
_CONV_K = 4
_LANES = 128
_CHIPS = 4
_LR, _B1, _B2, _AEPS, _WD, _STEP = 0.001, 0.9, 0.999, 1e-08, 0.01, 10
_VMEM_CAP = 56 * 1024 * 1024
_MESH = pl.DeviceIdType.MESH
_SDS = jax.ShapeDtypeStruct
_NT = (((1,), (1,)), ((), ()))
_TN = (((0,), (0,)), ((), ()))


def _params(sem, est_bytes):
    lim = int(min(max(2 * est_bytes + (4 << 20), 16 << 20), _VMEM_CAP))
    return pltpu.CompilerParams(dimension_semantics=sem, vmem_limit_bytes=lim)


def _nbytes(shape, dtype):
    return int(np.prod(shape)) * jnp.dtype(dtype).itemsize


def _hbm(a):
    return pltpu.with_memory_space_constraint(a, pltpu.HBM)


def _dot(a, b, dims=(((1,), (0,)), ((), ()))):
    return lax.dot_general(a.astype(_BF), b.astype(_BF), dims, preferred_element_type=F32)


_HALO = 8


def _rows(fn, ins, consts, outs, sums=(), *, halos=(), into=None, after=None, tile, name):
    rows = (ins[0][0] if isinstance(ins[0], tuple) else ins[0]).shape[0]
    n_steps = rows // tile

    def norm_in(a):
        if not isinstance(a, tuple):
            return a, tile, a.shape[1], lambda i: (i, 0)
        if isinstance(a[0], str) and a[0] == "slot":
            return a[1], (None, tile, a[1].shape[2]), a[1].shape[2], lambda i, s=a[2]: (s, i, 0)
        if isinstance(a[0], str):
            return a[1], tile // a[2], a[1].shape[1], lambda i: (i, 0)
        return a[0], tile, a[1], a[2] if callable(a[2]) else (lambda i, j=a[2]: (i, j))
    ins = [norm_in(a) for a in ins]
    outs = [(w, dt, d[0] if d else 1) for w, dt, *d in outs]
    n_in, n_h, n_c, n_o, n_s = len(ins), len(halos), len(consts), len(outs), len(sums)
    n_x = int(into is not None and into[0] is not None) + int(after is not None)

    def body(*refs):
        step = pl.program_id(0)
        vals = [r[...] for r in refs[:n_in]]
        for r, (_, side) in zip(refs[n_in:n_in + n_h], halos):
            vals.append(jnp.where(step == (0 if side < 0 else n_steps - 1), 0.0, r[...]))
        vals += [r[...] for r in refs[n_in + n_h:n_in + n_h + n_c]]
        refs = refs[:n_in] + refs[n_in + n_h:]
        res = fn(*vals)
        res = res if isinstance(res, tuple) else (res,)
        orefs = refs[n_in + n_c + n_x:n_in + n_c + n_x + n_o]
        srefs = refs[n_in + n_c + n_x + n_o:]
        for r, v in zip(orefs, res[:n_o]):
            r[...] = v.astype(r.dtype)
        if n_s:
            @pl.when(pl.program_id(0) == 0)
            def _():
                for r in srefs:
                    r[...] = jnp.zeros_like(r)
            for r, v in zip(srefs, res[n_o:]):
                r[...] += v.reshape(tile // 8, 8, v.shape[-1]).sum(axis=0)

    per = tile // _HALO
    in_specs = [pl.BlockSpec(r if isinstance(r, tuple) else (r, w), idx) for _, r, w, idx in ins]
    in_specs += [pl.BlockSpec((_HALO, a.shape[1]), (lambda i: (jnp.maximum(i * per - 1, 0), 0)) if side < 0
                              else (lambda i: (jnp.minimum((i + 1) * per, rows // _HALO - 1), 0))) for a, side in halos]
    in_specs += [pl.BlockSpec(c.shape, lambda i, nd=c.ndim: (0,) * nd) for c in consts]
    out_shape = [_SDS((rows // d, d * w), dt) for w, dt, d in outs] + [_SDS((8, w), F32) for w in sums]
    out_specs = [pl.BlockSpec((tile // d, d * w), lambda i: (i, 0)) for w, _, d in outs]
    out_specs += [pl.BlockSpec((8, w), lambda i: (0, 0)) for w in sums]
    est = (sum(_nbytes((tile if isinstance(r, tuple) else r, w), a.dtype) for a, r, w, _ in ins)
           + sum(_nbytes((tile, w), dt) for w, dt, _ in outs))
    shared, aliases = [], {}
    if into is not None:
        buf, total, j = into
        out_shape[0] = _SDS((rows, total), outs[0][1])
        out_specs[0] = pl.BlockSpec((tile, outs[0][0]), lambda i: (i, j))
        if buf is not None:
            shared, aliases = [buf], {n_in + n_h + n_c: 0}
    if after is not None:
        shared.append(after)
    in_specs += [pl.BlockSpec(memory_space=pl.ANY)] * len(shared)
    return pl.pallas_call(body, grid=(n_steps,), in_specs=in_specs, out_specs=out_specs, out_shape=out_shape, name=name,
                          input_output_aliases=aliases, compiler_params=_params(("arbitrary",), 3 * est))(
                              *[_hbm(a[0]) for a in ins], *[_hbm(a) for a, _ in halos], *consts, *shared)


def _perm(d, tile):
    p = np.zeros((tile, tile), np.float32)
    t = np.arange(tile)
    p[t, (t % d) * (tile // d) + t // d] = 1.0
    return jnp.asarray(p, _BF)


def _unstride(s, p):
    d = p.shape[0] // s.shape[0]
    w = s.shape[1] // d
    return _dot(p, jnp.concatenate([s[:, r * w:(r + 1) * w] for r in range(d)], axis=0))


def _stride(x, p, d):
    z = _dot(p, x, _TN)
    n = x.shape[0] // d
    return jnp.concatenate([z[r * n:(r + 1) * n] for r in range(d)], axis=1)


def _shifted(u, halo, back):
    n = u.shape[0] + _HALO
    if back:
        ext = jnp.concatenate([halo, u], axis=0)
        return [pltpu.roll(ext, j, 0)[_HALO:] for j in (1, 2, 3)]
    ext = jnp.concatenate([u, halo], axis=0)
    return [pltpu.roll(ext, n - j, 0)[:u.shape[0]] for j in (1, 2, 3)]


def _tile_for(width):
    return max(c for c in (256, 128, 64, 32) if c * width <= (1 << 18) or c == 32)


def _rstd(x):
    return lax.rsqrt(jnp.mean(x * x, axis=-1, keepdims=True) + _EPS)


def _split(x, groups):
    w = x.shape[-1] // groups
    return [x[:, g * w:(g + 1) * w] for g in range(groups)]


def _cat(parts):
    return parts[0] if len(parts) == 1 else jnp.concatenate(parts, axis=-1)


def _rms_bwd_tile(x, dy, g, groups):
    dxs, dgs = [], []
    for xs, ds, gs in zip(_split(x, groups), _split(dy.astype(F32), groups), _split(g, groups)):
        r = _rstd(xs)
        xh = xs * r
        gd = ds * gs
        dxs.append(r * (gd - xh * jnp.mean(gd * xh, axis=-1, keepdims=True)))
        dgs.append(ds * xh)
    return _cat(dxs), _cat(dgs)


def _rms_fwd(x, g, *, groups=1, name):
    def fn(x, g):
        return _cat([xs * _rstd(xs) * gs for xs, gs in zip(_split(x, groups), _split(g, groups))])
    w = x.shape[1]
    return _rows(fn, [x], [g.reshape(1, w)], [(w, _BF)], tile=_tile_for(w), name=name)[0]


def _rms_bwd(x, dy, g, res=None, *, name):
    def fn(x, dy, *rest):
        dx, dg = _rms_bwd_tile(x, dy, rest[-1], 1)
        return (dx + rest[0] if res is not None else dx), dg
    w = x.shape[1]
    ins = [x, dy] + ([res] if res is not None else [])
    dx, dg = _rows(fn, ins, [g.reshape(1, w)], [(w, F32)], [w], tile=_tile_for(w), name=name)
    return dx, dg.sum(axis=0)


def _pick(n, cands):
    for c in cands:
        if n % c == 0:
            return c
    raise ValueError(f"no block size for {n}")


_MM_BLOCKS = (1024, 1152, 512, 384)


def _mm(a, b, *, ta=False, tb=False, extra=(), epi=None, outs=(F32,), after=None, b_chips=0, out_chips=0, b_cols=None, name):
    m, k = (a.shape[1], a.shape[0]) if ta else a.shape
    b_shape = (b.shape[1], b.shape[2] * b_chips) if b_chips else b.shape
    if b_cols is not None:
        b_shape = (b.shape[0], b_cols[1])
    n = b_shape[0] if tb else b_shape[1]
    assert k == (b_shape[1] if tb else b_shape[0])
    n_cap = n // max(out_chips, 1 if tb else b_chips, 1)
    k_cap = k // (b_chips if (b_chips and tb) else 1)
    bm, bn = _pick(m, _MM_BLOCKS), _pick(n_cap, _MM_BLOCKS)
    bk = _pick(k_cap, (2048, 1920) + _MM_BLOCKS)
    nk = k // bk
    n_e, n_o = len(extra), len(outs)
    behind = [] if after is None else [after]
    dims = (((0 if ta else 1,), (1 if tb else 0,)), ((), ()))

    def body(a_ref, b_ref, *rest):
        ex, orefs, acc = rest[:n_e], rest[n_e + len(behind):n_e + len(behind) + n_o], rest[-1]
        kk = pl.program_id(2)

        @pl.when(kk == 0)
        def _():
            acc[...] = jnp.zeros_like(acc)

        acc[...] += _dot(a_ref[...], b_ref[...], dims)

        @pl.when(kk == nk - 1)
        def _():
            r = acc[...]
            res = epi(r, *[e[...] for e in ex]) if epi is not None else (r,)
            for o, v in zip(orefs, res):
                o[...] = v.astype(o.dtype)

    a_spec = pl.BlockSpec((bk, bm), lambda i, j, kk: (kk, i)) if ta else pl.BlockSpec((bm, bk), lambda i, j, kk: (i, kk))
    if b_chips and tb:
        per = k_cap // bk
        b_spec = pl.BlockSpec((None, bn, bk), lambda i, j, kk: (kk // per, j, kk % per))
    elif b_chips:
        per = n_cap // bn
        b_spec = pl.BlockSpec((None, bk, bn), lambda i, j, kk: (j // per, kk, j % per))
    else:
        first = 0 if b_cols is None else b_cols[0] // bn
        assert b_cols is None or (not tb and b_cols[0] % bn == 0)
        b_spec = pl.BlockSpec((bn, bk), lambda i, j, kk: (j, kk)) if tb else pl.BlockSpec((bk, bn), lambda i, j, kk: (kk, first + j))
    t_spec = pl.BlockSpec((bm, bn), lambda i, j, kk: (i, j))
    o_spec, o_shape = t_spec, (m, n)
    if out_chips:
        per_o = n_cap // bn
        o_spec, o_shape = pl.BlockSpec((None, bm, bn), lambda i, j, kk: (j // per_o, i, j % per_o)), (out_chips, m, n_cap)
    est = (_nbytes((bm, bk), a.dtype) + _nbytes((bk, bn), b.dtype) + sum(_nbytes((bm, bn), e.dtype) for e in extra)
           + sum(_nbytes((bm, bn), o) for o in outs)) * 2 + 2 * _nbytes((bm, bn), F32)
    res = pl.pallas_call(
        body, grid=(m // bm, n // bn, nk), in_specs=[a_spec, b_spec] + [t_spec] * n_e + [pl.BlockSpec(memory_space=pl.ANY)] * len(behind),
        out_specs=[o_spec] * n_o, out_shape=[_SDS(o_shape, o) for o in outs], scratch_shapes=[pltpu.VMEM((bm, bn), F32)], name=name,
        compiler_params=_params(("parallel", "parallel", "arbitrary"), est))(_hbm(a), _hbm(b), *[_hbm(e) for e in extra], *behind)
    return res[0] if n_o == 1 else res


def _add_to(acc, r):
    return (acc + r,)


def _alibi_bias(dilation):
    slopes = 2.0 ** (-8.0 * (np.arange(_HEADS) + 1) / _HEADS)
    i = np.arange(_ABLK)[:, None]
    j = np.arange(_ABLK)[None, :]
    cur = np.where(i - j >= 0, -slopes[:, None, None] * ((i - j) * dilation), _NEG)
    prev = np.where(j >= i, -slopes[:, None, None] * ((i - j + _ABLK) * dilation), _NEG)
    both = np.stack([np.concatenate([np.full_like(prev, _NEG), cur], axis=2), np.concatenate([prev, cur], axis=2)])
    return jnp.asarray(both.reshape(2, _HEADS // 2, 2 * _ABLK, 2 * _ABLK), F32)


def _bias_spec():
    return pl.BlockSpec((None, _HEADS // 2, 2 * _ABLK, 2 * _ABLK), lambda r, j: (jnp.minimum(j, 1), 0, 0, 0))


def _strided(a, d):
    return a.reshape(a.shape[0] // d, d * a.shape[1])


def _head(h):
    return slice(h * _HDIM, (h + 1) * _HDIM)


def _pair(pr):
    return slice(pr * _LANES, (pr + 1) * _LANES)


def _low_lanes(shape):
    return lax.broadcasted_iota(jnp.int32, shape, 1) < _HDIM


def _halves(v, low):
    z = jnp.zeros_like(v)
    return jnp.where(low, v, z), jnp.where(low, z, v)


def _lane_spec(nb):
    return pl.BlockSpec((_ABLK, _LANES), lambda r, j: (jnp.minimum(j, nb - 1), r))


def _expand_heads(v):
    low = _low_lanes((v.shape[0], _LANES))
    return jnp.concatenate([jnp.where(low, v[:, 2 * pr:2 * pr + 1], v[:, 2 * pr + 1:2 * pr + 2]) for pr in range(_HEADS // 2)], axis=1)


def _attn_specs(nb, n_parts):
    def cur(p):
        return pl.BlockSpec((_ABLK, _AW), lambda r, j: (jnp.minimum(j, nb - 1), r * n_parts + p))

    def prev(p):
        return pl.BlockSpec((_ABLK, _AW), lambda r, j: (jnp.clip(j - 1, 0, nb - 1), r * n_parts + p))
    return cur, prev


def _attn_fwd(qkv, dilation, *, name):
    t = qkv.shape[0]
    nb = t // dilation // _ABLK
    bias = _alibi_bias(dilation)
    scale = _HDIM ** -0.5

    def body(q_ref, kc_ref, kp_ref, vc_ref, vp_ref, b_ref, o_ref, l_ref):
        low = _low_lanes((_ABLK, _LANES))
        l_ref[...] = jnp.zeros_like(l_ref)
        for pr in range(_HEADS // 2):
            sl = _pair(pr)
            k2 = jnp.concatenate([kp_ref[:, sl], kc_ref[:, sl]], axis=0)
            v2 = jnp.concatenate([vp_ref[:, sl], vc_ref[:, sl]], axis=0)
            q2 = jnp.concatenate(_halves(q_ref[:, sl] * scale, low), axis=0)
            s = _dot(q2, k2, _NT) + b_ref[pr]
            m = jnp.max(s, axis=-1, keepdims=True)
            p = jnp.exp(s - m)
            den = jnp.sum(p, axis=-1, keepdims=True)
            o = _dot(p, v2) / den
            lse = m + jnp.log(den)
            l_ref[:, 2 * pr:2 * pr + 1] = lse[:_ABLK]
            l_ref[:, 2 * pr + 1:2 * pr + 2] = lse[_ABLK:]
            o_ref[:, sl] = jnp.where(low, o[:_ABLK], o[_ABLK:]).astype(o_ref.dtype)

    cur, prev = _attn_specs(nb, 3)
    cur1, _ = _attn_specs(nb, 1)
    bspec = _bias_spec()
    sv = _hbm(_strided(qkv, dilation))
    o, l = pl.pallas_call(
        body, grid=(dilation, nb), in_specs=[cur(0), cur(1), prev(1), cur(2), prev(2), bspec],
        out_specs=[cur1(0), _lane_spec(nb)],
        out_shape=[_SDS((t // dilation, dilation * _AW), _BF), _SDS((t // dilation, dilation * _LANES), F32)], name=name,
        compiler_params=_params(("parallel", "arbitrary"), 16 << 20))(sv, sv, sv, sv, sv, bias)
    return o, l.reshape(t, _LANES)


def _attn_bwd(qkv, do, ld, dilation, *, name):
    t = qkv.shape[0]
    nb = t // dilation // _ABLK
    bias = _alibi_bias(dilation)
    scale = _HDIM ** -0.5

    def body(q_ref, kc_ref, kp_ref, vc_ref, vp_ref, do_ref, ld_ref, b_ref, dq_ref, dk_ref, dv_ref, ck, cv):
        n = pl.program_id(1)

        @pl.when(n == 0)
        def _():
            ck[...] = jnp.zeros_like(ck)
            cv[...] = jnp.zeros_like(cv)

        @pl.when(n < nb)
        def _():
            low = _low_lanes((_ABLK, _LANES))
            for pr in range(_HEADS // 2):
                sl = _pair(pr)
                k2 = jnp.concatenate([kp_ref[:, sl], kc_ref[:, sl]], axis=0)
                v2 = jnp.concatenate([vp_ref[:, sl], vc_ref[:, sl]], axis=0)
                q2 = jnp.concatenate(_halves(q_ref[:, sl] * scale, low), axis=0)
                do2 = jnp.concatenate(_halves(do_ref[:, sl], low), axis=0)
                lrow = jnp.concatenate([ld_ref[:, 2 * pr:2 * pr + 1], ld_ref[:, 2 * pr + 1:2 * pr + 2]], axis=0)
                dsum = jnp.concatenate([ld_ref[:, _HEADS + 2 * pr:_HEADS + 2 * pr + 1],
                                        ld_ref[:, _HEADS + 2 * pr + 1:_HEADS + 2 * pr + 2]], axis=0)
                p = jnp.exp(_dot(q2, k2, _NT) + b_ref[pr] - lrow)
                ds = (p * (_dot(do2, v2, _NT) - dsum)).astype(_BF)
                dq = _dot(ds, k2)
                dk2, dv2 = _dot(ds, q2, _TN), _dot(p, do2, _TN)
                dq_ref[:, sl] = (jnp.where(low, dq[:_ABLK], dq[_ABLK:]) * scale).astype(dq_ref.dtype)
                dk_ref[:, sl] = (ck[:, sl] + dk2[:_ABLK]).astype(dk_ref.dtype)
                dv_ref[:, sl] = (cv[:, sl] + dv2[:_ABLK]).astype(dv_ref.dtype)
                ck[:, sl] = dk2[_ABLK:]
                cv[:, sl] = dv2[_ABLK:]

        @pl.when(n == nb)
        def _():
            dk_ref[...] = ck[...].astype(dk_ref.dtype)
            dv_ref[...] = cv[...].astype(dv_ref.dtype)

    cur, prev = _attn_specs(nb, 3)
    cur1, prev1 = _attn_specs(nb, 1)
    bspec = _bias_spec()
    sv, dov, ldv = _hbm(_strided(qkv, dilation)), _hbm(do), _hbm(_strided(ld, dilation))
    dqkv = pl.pallas_call(
        body, grid=(dilation, nb + 1),
        in_specs=[cur(0), cur(1), prev(1), cur(2), prev(2), cur1(0), _lane_spec(nb), bspec],
        out_specs=[cur1(0), prev1(0), prev1(0)], out_shape=[_SDS(dov.shape, _BF)] * 3, name=name,
        scratch_shapes=[pltpu.VMEM((_ABLK, _AW), F32)] * 2,
        compiler_params=_params(("parallel", "arbitrary"), 16 << 20))(sv, sv, sv, sv, sv, dov, ldv, bias)
    return dqkv


def _ssd_in_specs(ch):
    return dict(
        xs=pl.BlockSpec((_CHUNK, _AW), lambda c: (ch(c), 0)),
        bc=pl.BlockSpec((_CHUNK, 2 * _GROUPS * _NSTATE), lambda c: (ch(c), _AW // (2 * _GROUPS * _NSTATE))),
        lane=pl.BlockSpec((_CHUNK, _LANES), lambda c: (ch(c), 0)),
        arow=pl.BlockSpec((_HEADS, 1, _CHUNK), lambda c: (0, 0, ch(c))),
        st=pl.BlockSpec((1, _HEADS // 2, _NSTATE, _LANES), lambda c: (ch(c), 0, 0, 0)),
    )


def _decay(a_col, a_row):
    i0 = lax.broadcasted_iota(jnp.int32, (_CHUNK, _CHUNK), 0)
    i1 = lax.broadcasted_iota(jnp.int32, (_CHUNK, _CHUNK), 1)
    return jnp.where(i0 >= i1, jnp.exp(a_col - a_row), 0.0), jnp.where(i1 >= i0, jnp.exp(a_row - a_col), 0.0)


def _rsum(v):
    return jnp.sum(v, axis=-1, keepdims=True)


def _ssd_fwd(act, dt, acum, a_row, *, name):
    t = act.shape[0]
    nc = t // _CHUNK
    sp = _ssd_in_specs(lambda c: c)
    gw = _GROUPS * _NSTATE

    def body(xs_ref, bc_ref, dt_ref, ac_ref, ar_ref, y_ref, sall_ref, st):
        @pl.when(pl.program_id(0) == 0)
        def _():
            st[...] = jnp.zeros_like(st)

        low = _low_lanes((_CHUNK, _LANES))
        for g in range(_GROUPS):
            bg = bc_ref[:, g * _NSTATE:(g + 1) * _NSTATE]
            cg = bc_ref[:, gw + g * _NSTATE:gw + (g + 1) * _NSTATE].astype(_BF)
            cb = _dot(cg, bg, _NT)
            for pr in range(g * _HPG // 2, (g + 1) * _HPG // 2):
                ha, hb = 2 * pr, 2 * pr + 1
                a_a, a_b = ac_ref[:, ha:ha + 1], ac_ref[:, hb:hb + 1]
                x = (xs_ref[:, _pair(pr)] * jnp.where(low, dt_ref[:, ha:ha + 1], dt_ref[:, hb:hb + 1])).astype(_BF)
                lm_a, _ = _decay(a_a, ar_ref[ha])
                lm_b, _ = _decay(a_b, ar_ref[hb])
                sv = st[pr]
                sall_ref[0, pr] = sv
                yd = _dot(jnp.concatenate([cb * lm_a, cb * lm_b], axis=0), x)
                yd = jnp.where(low, yd[:_CHUNK], yd[_CHUNK:])
                y_ref[:, _pair(pr)] = yd + jnp.where(low, jnp.exp(a_a), jnp.exp(a_b)) * _dot(cg, sv)
                al_a, al_b = jnp.min(a_a, axis=0, keepdims=True), jnp.min(a_b, axis=0, keepdims=True)
                upd = _dot(jnp.concatenate([bg * jnp.exp(al_a - a_a), bg * jnp.exp(al_b - a_b)], axis=1), x, _TN)
                st[pr] = jnp.where(low, jnp.exp(al_a), jnp.exp(al_b)) * sv + jnp.where(low, upd[:_NSTATE], upd[_NSTATE:])

    return pl.pallas_call(
        body, grid=(nc,), in_specs=[sp['xs'], sp['bc'], sp['lane'], sp['lane'], sp['arow']],
        out_specs=[sp['xs'], sp['st']], out_shape=[_SDS((t, _AW), F32), _SDS((nc, _HEADS // 2, _NSTATE, _LANES), F32)],
        scratch_shapes=[pltpu.VMEM((_HEADS // 2, _NSTATE, _LANES), F32)], name=name,
        compiler_params=_params(("arbitrary",), 16 << 20))(*[_hbm(a) for a in (act, act, dt, acum, a_row)])


def _ssd_bwd(act, dt, acum, a_row, sall, dy, *, name):
    t = act.shape[0]
    nc = t // _CHUNK
    sp = _ssd_in_specs(lambda c: nc - 1 - c)
    gw = _GROUPS * _NSTATE

    def body(xs_ref, bc_ref, dt_ref, ac_ref, ar_ref, sall_ref, dy_ref, dxs_ref, dbc_ref, ddt_ref, da_ref, dst):
        @pl.when(pl.program_id(0) == 0)
        def _():
            dst[...] = jnp.zeros_like(dst)

        ddt_ref[...] = jnp.zeros_like(ddt_ref)
        da_ref[...] = jnp.zeros_like(da_ref)
        row = lax.broadcasted_iota(jnp.int32, (_CHUNK, 1), 0)
        low = _low_lanes((_CHUNK, _LANES))
        for g in range(_GROUPS):
            bg = bc_ref[:, g * _NSTATE:(g + 1) * _NSTATE]
            bgb = bg.astype(_BF)
            cg = bc_ref[:, gw + g * _NSTATE:gw + (g + 1) * _NSTATE].astype(_BF)
            cb, cbt = _dot(cg, bgb, _NT), _dot(bgb, cg, _NT)
            dcb = jnp.zeros((_CHUNK, _CHUNK), F32)
            dbg = jnp.zeros((_CHUNK, _NSTATE), F32)
            dcg = jnp.zeros((_CHUNK, _NSTATE), F32)
            for pr in range(g * _HPG // 2, (g + 1) * _HPG // 2):
                heads = (2 * pr, 2 * pr + 1)
                a_cols = [ac_ref[:, h:h + 1] for h in heads]
                dt_pair = jnp.where(low, dt_ref[:, heads[0]:heads[0] + 1], dt_ref[:, heads[1]:heads[1] + 1])
                xsv = xs_ref[:, _pair(pr)]
                x = xsv * dt_pair
                xb = x.astype(_BF)
                xhs = _halves(xb, low)
                dyv = dy_ref[:, _pair(pr)]
                dyb = dyv.astype(_BF)
                dyhs = _halves(dyb, low)
                sv, dsv = sall_ref[0, pr], dst[pr]
                svb, dsb = sv.astype(_BF), dsv.astype(_BF)
                a_lasts = [jnp.min(a, axis=0, keepdims=True) for a in a_cols]
                e_pair = jnp.where(low, jnp.exp(a_cols[0]), jnp.exp(a_cols[1]))
                el_pair = jnp.where(low, jnp.exp(a_lasts[0]), jnp.exp(a_lasts[1]))
                yo = e_pair * _dot(cg, svb)
                decays = [_decay(a_cols[i], ar_ref[h]) for i, h in enumerate(heads)]
                gms, gmts = [cb * lm for lm, _ in decays], [cbt * lmt for _, lmt in decays]
                w_cols = [jnp.exp(a_lasts[i] - a_cols[i]) for i in range(2)]
                x2, dy2 = jnp.concatenate(xhs, axis=0), jnp.concatenate(dyhs, axis=0)
                bwd = _dot(jnp.concatenate([bg * w_cols[0], bg * w_cols[1]], axis=0), dsb)
                dxg = _dot(jnp.concatenate(gms, axis=1), dyb, _TN)
                dg2, dgt2, xds2 = _dot(dy2, xb, _NT), _dot(x2, dyb, _NT), _dot(x2, dsb, _NT)
                das = []
                for i in range(2):
                    rows_i = slice(i * _CHUNK, (i + 1) * _CHUNK)
                    dcb = dcb + dg2[rows_i] * decays[i][0]
                    dbg = dbg + w_cols[i] * xds2[rows_i]
                    das.append(_rsum(dg2[rows_i] * gms[i]) - _rsum(dgt2[rows_i] * gmts[i]))
                bwd = jnp.where(low, bwd[:_CHUNK], bwd[_CHUNK:])
                dx = jnp.where(low, dxg[:_CHUNK], dxg[_CHUNK:]) + bwd
                edy = (e_pair * dyv).astype(_BF)
                dcg = dcg + _dot(edy, svb, _NT)
                zs, yos, sds, dts = (_halves(v, low) for v in (x * bwd, dyv * yo, sv * dsv, dx * xsv))
                for i, h in enumerate(heads):
                    z = _rsum(zs[i])
                    da_last = jnp.sum(z, axis=0, keepdims=True) + jnp.exp(a_lasts[i]) * jnp.sum(_rsum(sds[i]), axis=0, keepdims=True)
                    da_ref[:, h:h + 1] = das[i] + _rsum(yos[i]) - z + jnp.where(row == _CHUNK - 1, da_last, 0.0)
                    ddt_ref[:, h:h + 1] = _rsum(dts[i])
                dxs_ref[:, _pair(pr)] = dx * dt_pair
                dst[pr] = el_pair * dsv + _dot(cg, edy, _TN)
            dbc_ref[:, g * _NSTATE:(g + 1) * _NSTATE] = dbg + _dot(dcb, cg, _TN)
            dbc_ref[:, gw + g * _NSTATE:gw + (g + 1) * _NSTATE] = dcg + _dot(dcb, bgb)

    ch = lambda c: nc - 1 - c
    wide = pl.BlockSpec((_CHUNK, 2 * gw), lambda c: (ch(c), 0))
    return pl.pallas_call(
        body, grid=(nc,), in_specs=[sp['xs'], sp['bc'], sp['lane'], sp['lane'], sp['arow'], sp['st'], sp['xs']],
        out_specs=[sp['xs'], wide, sp['lane'], sp['lane']],
        out_shape=[_SDS((t, _AW), F32), _SDS((t, 2 * gw), F32), _SDS((t, _LANES), F32), _SDS((t, _LANES), F32)],
        scratch_shapes=[pltpu.VMEM((_HEADS // 2, _NSTATE, _LANES), F32)], name=name,
        compiler_params=_params(("arbitrary",), 16 << 20))(*[_hbm(a) for a in (act, act, dt, acum, a_row, sall, dy)])


def _scan_rows(v, reverse):
    r = lax.broadcasted_iota(jnp.int32, v.shape, 0)
    for s in (1, 2, 4, 8, 16, 32, 64):
        if reverse:
            v = v + jnp.where(r < _CHUNK - s, pltpu.roll(v, _CHUNK - s, 0), 0.0)
        else:
            v = v + jnp.where(r >= s, pltpu.roll(v, s, 0), 0.0)
    return v


def _softplus(x):
    return jnp.maximum(x, 0.0) + jnp.log(1.0 + jnp.exp(-jnp.abs(x)))


def _sigmoid(x):
    return 1.0 / (1.0 + jnp.exp(-x))


def _silu(x):
    return x * _sigmoid(x)


def _dsilu(x):
    s = _sigmoid(x)
    return s * (1.0 + x * (1.0 - s))


def _lanes(a):
    return jnp.pad(a, (0, _LANES - a.shape[0])).reshape(1, _LANES)


def _layer_fwd(x, p, l):
    cch = p['conv_w'].shape[1]
    sv = {}
    h1 = _rms_fwd(x, p['ln1_g'], name=f"ln1_fwd_{l}")
    qkv = _mm(h1, p['w_in'], b_cols=(0, 3 * _AW), outs=(_BF,), name=f"in_proj_qkv_{l}")
    xbc = _mm(h1, p['w_in'], b_cols=(3 * _AW, cch), name=f"in_proj_xbc_{l}")
    zdt = _mm(h1, p['w_in'], b_cols=(3 * _AW + cch, _AW + _LANES), name=f"in_proj_zdt_{l}")
    z, dt_raw = (zdt, _AW, 0), (zdt, _LANES, _AW // _LANES)

    outs = []
    for dil in _DILATIONS:
        outs += _attn_fwd(qkv, dil, name=f"attn_fwd_d{dil}_{l}")

    tile = 2 * _ABLK
    perms = [_perm(d, tile) for d in _DILATIONS[1:]]

    def combine(o1, l1, o2, l2, o3, l3, p2, p3):
        m = jnp.maximum(jnp.maximum(l1, l2), l3)
        e1, e2, e3 = jnp.exp(l1 - m), jnp.exp(l2 - m), jnp.exp(l3 - m)
        tot = e1 + e2 + e3
        mixed = sum(_expand_heads(e / tot) * o for e, o in ((e1, o1.astype(F32)), (e2, _unstride(o2, p2)), (e3, _unstride(o3, p3))))
        return mixed, m + jnp.log(tot)
    outs = [a if i % 2 or i == 0 else ("strided", a, _DILATIONS[i // 2]) for i, a in enumerate(outs)]
    attn, lse = _rows(combine, outs, perms, [(_AW, F32), (_LANES, F32)], tile=tile, name=f"attn_combine_{l}")
    attn_n = _rms_fwd(attn, p['attn_norm_g'], name=f"attn_norm_fwd_{l}")

    def conv(u0, before, w, b):
        u1, u2, u3 = _shifted(u0, before, True)
        return _silu(w[0:1] * u3 + w[1:2] * u2 + w[2:3] * u1 + w[3:4] * u0 + b)
    act = _rows(conv, [xbc], [p['conv_w'], p['conv_b'].reshape(1, cch)], [(cch, F32)], halos=[(xbc, -1)], tile=_tile_for(cch),
                name=f"conv_fwd_{l}")[0]

    def dtf(raw, bias, alog):
        dt = _softplus(raw + bias)
        return dt, _scan_rows(dt * -jnp.exp(alog), False)
    dt, acum = _rows(dtf, [dt_raw], [_lanes(p['dt_bias']), _lanes(p['a_log'])], [(_LANES, F32), (_LANES, F32)],
                     tile=_CHUNK, name=f"dt_fwd_{l}")
    a_row = acum[:, :_HEADS].T[:, None, :]
    y_ssd, sall = _ssd_fwd(act, dt, acum, a_row, name=f"ssd_fwd_{l}")
    dskip = jnp.repeat(p['d_skip'], _HDIM).reshape(1, _AW)
    xs = (act, _AW, 0)

    def gate(y, xs, z, dsk):
        return (y + dsk * xs) * _silu(z)
    y2 = _rows(gate, [y_ssd, xs, z], [dskip], [(_AW, F32)], tile=_tile_for(_AW), name=f"gate_fwd_{l}")[0]
    y_n = _rms_fwd(y2, p['ssd_norm_g'], groups=_GROUPS, name=f"ssd_norm_fwd_{l}")

    mix = jnp.concatenate([attn_n, y_n], axis=1)
    sv.update(x=x, h1=h1, qkv=qkv, zdt=zdt, xbc=xbc, attn=attn, lse=lse, act=act, dt=dt, acum=acum, a_row=a_row,
              sall=sall, y_ssd=y_ssd, dskip=dskip, y2=y2, mix=mix)
    return mix, sv


def _layer_fwd_mlp(p, sv, l):
    x2 = _mm(sv['mix'], p['w_out'], extra=(sv['x'],), epi=_add_to, name=f"out_proj_{l}")
    h2 = _rms_fwd(x2, p['ln2_g'], name=f"ln2_fwd_{l}")
    a = _mm(h2, p['w_mlp_in'], b_chips=_CHIPS, epi=lambda acc: (jnp.square(jnp.maximum(acc, 0.0)),), outs=(_BF,), name=f"mlp_in_{l}")
    x3 = _mm(a, p['w_mlp_out'], extra=(x2,), epi=_add_to, name=f"mlp_out_{l}")
    sv.update(x2=x2, h2=h2, a=a)
    return x3


def _layer_bwd(dx3, p, sv, l, send, after):
    cch = p['conv_w'].shape[1]
    g = {}
    dx3b = dx3.astype(_BF)
    du = _mm(dx3b, p['w_mlp_out'], tb=True, extra=(sv['a'],), outs=(_BF,), after=after,
             epi=lambda acc, a: (acc * 2.0 * jnp.sqrt(a.astype(F32)),), name=f"mlp_out_dx_{l}")
    g['w_mlp_out'] = _mm(sv['a'], dx3b, ta=True, outs=(_BF,), name=f"mlp_out_dw_{l}")
    g['w_mlp_in'] = _mm(sv['h2'], du, ta=True, out_chips=_CHIPS, outs=(_BF,), name=f"mlp_in_dw_{l}")
    sent = send(('w_mlp_out', 'w_mlp_in'), g)
    dh2 = _mm(du, p['w_mlp_in'], tb=True, b_chips=_CHIPS, after=sent, name=f"mlp_in_dx_{l}")
    dx2, g['ln2_g'] = _rms_bwd(sv['x2'], dh2, p['ln2_g'], dx3, name=f"ln2_bwd_{l}")
    dx2b = dx2.astype(_BF)
    dmix = _mm(dx2b, p['w_out'], tb=True, name=f"out_proj_dx_{l}")
    g['w_out'] = _mm(sv['mix'], dx2b, ta=True, outs=(_BF,), name=f"out_proj_dw_{l}")
    after_out = send(('w_out',), g)

    tile = 2 * _ABLK
    perms = [_perm(d, tile) for d in _DILATIONS[1:]]

    def norm_bwd(attn, dy, lse, gn, p2, p3):
        dattn, dgn = _rms_bwd_tile(attn, dy, gn, 1)
        prod, low = dattn * attn, _low_lanes((attn.shape[0], _LANES))
        lane = lax.broadcasted_iota(jnp.int32, lse.shape, 1)
        ld = jnp.where(lane < _HEADS, lse, 0.0)
        for pr in range(_HEADS // 2):
            for i, part in enumerate(_halves(prod[:, _pair(pr)], low)):
                ld = jnp.where(lane == _HEADS + 2 * pr + i, _rsum(part), ld)
        return dattn, _stride(dattn, p2, _DILATIONS[1]), _stride(dattn, p3, _DILATIONS[2]), ld, dgn
    *dos, ld, gn_sum = _rows(norm_bwd, [sv['attn'], (dmix, _AW, 0), sv['lse']], [p['attn_norm_g'].reshape(1, _AW)] + perms,
                             [(_AW, _BF)] + [(_AW, _BF, d) for d in _DILATIONS[1:]] + [(_LANES, F32)], [_AW], after=after_out,
                             tile=tile, name=f"attn_norm_bwd_{l}")
    g['attn_norm_g'] = gn_sum.sum(axis=0)
    parts = [_attn_bwd(sv['qkv'], do, ld, dil, name=f"attn_bwd_d{dil}_{l}") for do, dil in zip(dos, _DILATIONS)]

    def branch_sum(*t):
        parts_, (p2, p3) = t[:9], t[9:]
        t = [a.astype(F32) for a in parts_[:3]] + [_unstride(a, p2) for a in parts_[3:6]] + [_unstride(a, p3) for a in parts_[6:]]
        return jnp.concatenate([t[i] + t[3 + i] + t[6 + i] for i in range(3)], axis=1)
    branch_ins = list(parts[0]) + [("strided", a, d) for pr, d in zip(parts[1:], _DILATIONS[1:]) for a in pr]
    w_all = 3 * _AW + cch + _AW + _LANES
    dproj = _rows(branch_sum, branch_ins, perms, [(3 * _AW, _BF)], into=(None, w_all, 0), tile=tile, name=f"attn_bwd_sum_{l}")[0]

    xs, z, dt_raw = (sv['act'], _AW, 0), (sv['zdt'], _AW, 0), (sv['zdt'], _LANES, _AW // _LANES)

    def gate_bwd(y2, dy, y, xs, z, dsk, gn):
        dy2, dgn = _rms_bwd_tile(y2, dy, gn, _GROUPS)
        dy1 = dy2 * _silu(z)
        return dy1, dsk * dy1, dy2 * (y + dsk * xs) * _dsilu(z), dy1 * xs, dgn
    dy1, dxs_skip, dz, dsk_sum, gn_sum = _rows(
        gate_bwd, [sv['y2'], (dmix, _AW, 1), sv['y_ssd'], xs, z], [sv['dskip'], p['ssd_norm_g'].reshape(1, _AW)],
        [(_AW, F32), (_AW, F32), (_AW, _BF)], [_AW, _AW], tile=128, name=f"gate_bwd_{l}")
    g['ssd_norm_g'] = gn_sum.sum(axis=0)
    g['d_skip'] = dsk_sum.sum(axis=0).reshape(_HEADS, _HDIM).sum(axis=1)
    dxs, dbc, ddt, da = _ssd_bwd(sv['act'], sv['dt'], sv['acum'], sv['a_row'], sv['sall'], dy1, name=f"ssd_bwd_{l}")

    def dtb(da, ddtx, raw, dt, dz, bias, alog):
        a = -jnp.exp(alog)
        dda = _scan_rows(da, True)
        draw = (dda * a + ddtx) * _sigmoid(raw + bias)
        return jnp.concatenate([dz, draw.astype(dz.dtype)], axis=1), draw, dda * dt * a
    dproj, dbias, dalog = _rows(dtb, [da, ddt, dt_raw, sv['dt'], dz], [_lanes(p['dt_bias']), _lanes(p['a_log'])],
                                [(_AW + _LANES, _BF)], [_LANES, _LANES], into=(dproj, w_all, (3 * _AW + cch) // (_AW + _LANES)),
                                tile=_CHUNK, name=f"dt_bwd_{l}")
    g['dt_bias'], g['a_log'] = dbias.sum(axis=0)[:_HEADS], dalog.sum(axis=0)[:_HEADS]
    def conv_bwd1(u0, dxs, dbc, dxk, before, w, b):
        u1, u2, u3 = _shifted(u0, before, True)
        pre = w[0:1] * u3 + w[1:2] * u2 + w[2:3] * u1 + w[3:4] * u0 + b
        dp = jnp.concatenate([dxs + dxk, dbc], axis=1) * _dsilu(pre)
        return dp, dp * u3, dp * u2, dp * u1, dp * u0, dp
    dpre, *dws = _rows(conv_bwd1, [sv['xbc'], dxs, dbc, dxs_skip], [p['conv_w'], p['conv_b'].reshape(1, cch)], [(cch, F32)],
                       [cch] * 5, halos=[(sv['xbc'], -1)], tile=128, name=f"conv_bwd_pre_{l}")
    g['conv_w'] = jnp.stack([dws[i].sum(axis=0) for i in range(_CONV_K)])
    g['conv_b'] = dws[4].sum(axis=0)

    def conv_bwd2(p0, after_, w):
        p1, p2, p3 = _shifted(p0, after_, False)
        return w[3:4] * p0 + w[2:3] * p1 + w[1:2] * p2 + w[0:1] * p3
    dproj = _rows(conv_bwd2, [dpre], [p['conv_w']], [(cch, _BF)], halos=[(dpre, 1)], into=(dproj, w_all, 3 * _AW // cch),
                  tile=_tile_for(cch), name=f"conv_bwd_in_{l}")[0]
    g_all = _mm(sv['h1'], dproj, ta=True, outs=(_BF,), name=f"in_proj_dw_{l}")
    z0 = 3 * _AW + cch
    g['w_in'] = jnp.concatenate([g_all[:, :3 * _AW], g_all[:, z0:z0 + _AW], g_all[:, 3 * _AW:z0], g_all[:, z0 + _AW:z0 + _AW + _HEADS]], axis=1)
    sent = send(('w_in',), g)
    for n in _BIG:
        del g[n]
    dh1 = _mm(dproj, p['w_in'], tb=True, after=sent, name=f"in_proj_dx_{l}")
    dx, g['ln1_g'] = _rms_bwd(sv['x'], dh1, p['ln1_g'], dx2, name=f"ln1_bwd_{l}")
    return dx, g


def _loss_bwd(x, g, tgt):
    w = x.shape[1]
    tile = _tile_for(w)

    def fn(x, tgt, g):
        r = _rstd(x)
        xh = x * r
        e = xh * g - tgt
        gd = e * (g / w)
        dx = r * (gd - xh * jnp.mean(gd * xh, axis=-1, keepdims=True))
        rowloss = 0.5 * jnp.mean(e * e, axis=-1, keepdims=True)
        return dx, (e / w) * xh, jnp.broadcast_to(rowloss, (tile, _LANES))
    dx, dg, ls = _rows(fn, [x, tgt], [g.reshape(1, w)], [(w, F32)], [w, _LANES], tile=tile, name="loss_head")
    return dx, dg.sum(axis=0), ls[:, 0].sum()


def _adamw_math(w, g, m, v):
    m2 = _B1 * m + (1.0 - _B1) * g
    v2 = _B2 * v + (1.0 - _B2) * jnp.square(g)
    m_hat = m2 / (1.0 - _B1 ** _STEP)
    v_hat = v2 / (1.0 - _B2 ** _STEP)
    return -_LR * (m_hat / (jnp.sqrt(v_hat) + _AEPS) + _WD * w), m2, v2


def _adamw(w, g, m, v, *, name):
    width = w.shape[-1]
    flat = [a.reshape(-1, width) for a in (w, g, m, v)]
    tile = _pick(flat[0].shape[0], (_tile_for(width), 32, 8))
    res = _rows(_adamw_math, flat, [], [(width, F32)] * 3, tile=tile, name=name)
    return [r.reshape(w.shape) for r in res]


_HBM = pl.BlockSpec(memory_space=pltpu.HBM)


def _place():
    x, y, c = lax.axis_index("x"), lax.axis_index("y"), lax.axis_index("c")
    other_chips = [(1 - x, y), (x, 1 - y), (1 - x, 1 - y)]
    return x, y, c, other_chips


def _remote(src, dst, sems, i, dev):
    return pltpu.make_async_remote_copy(src_ref=src, dst_ref=dst, send_sem=sems[0].at[i], recv_sem=sems[1].at[i],
                                        device_id=dev, device_id_type=_MESH)


def _exchange8(v, *, reduce, after=None, name):
    r, w = v.shape
    behind = [] if after is None else [after]

    def body(v_ref, *rest):
        all_ref, rest = rest[len(behind)], rest[len(behind) + 1:]
        sems = rest[-2:]
        x, y, c, _ = _place()
        me = 4 * x + 2 * y + c
        all_ref[me] = v_ref[...]
        flips = [((d >> 2) & 1, (d >> 1) & 1, d & 1) for d in range(1, 8)]
        sends = [_remote(v_ref, all_ref.at[me], sems, i, (x ^ fx, y ^ fy, c ^ fc)) for i, (fx, fy, fc) in enumerate(flips)]
        for cp in sends:
            cp.start()
        for i, (fx, fy, fc) in enumerate(flips):
            _remote(v_ref, all_ref.at[me ^ (4 * fx + 2 * fy + fc)], sems, i, (x ^ fx, y ^ fy, c ^ fc)).wait_recv()
        for cp in sends:
            cp.wait_send()
        if reduce:
            acc = all_ref[0]
            for s in range(1, 8):
                acc = acc + all_ref[s]
            rest[0][...] = acc

    vm = pl.BlockSpec(memory_space=pltpu.VMEM)
    out_shape = [_SDS((8, r, w), v.dtype)] + ([_SDS((r, w), v.dtype)] if reduce else [])
    res = pl.pallas_call(body, in_specs=[vm] + [_ANY] * len(behind), out_specs=[vm] * len(out_shape), out_shape=out_shape, name=name,
                         scratch_shapes=[pltpu.SemaphoreType.DMA((7,)), pltpu.SemaphoreType.DMA((7,))],
                         compiler_params=pltpu.CompilerParams(vmem_limit_bytes=int(32 << 20)))(v, *behind)
    return res[1] if reduce else res[0]


_SEM = pl.BlockSpec(memory_space=pltpu.SEMAPHORE)
_ANY = pl.BlockSpec(memory_space=pl.ANY)
_EFFECT = pltpu.SideEffectType.DATAFLOW_SIDE_EFFECTING


def _send_start(name, srcs, land_shapes, plan, n_sends, after):
    ns, nl = len(srcs), len(land_shapes)
    zones = [_hbm(lax.empty(s.shape, s.dtype)) if isinstance(s, _SDS) else s for s in land_shapes]

    def body(*refs):
        ins, lands, sems = refs[:ns], refs[ns:ns + nl], refs[ns + nl + 1:ns + nl + 3]
        x, y, c, chips = _place()
        for i, (s, d, dev) in enumerate(plan(x, y, c, chips, ins, lands)[0]):
            _remote(s, d, sems, i, dev).start()
        refs[-1][...] = jnp.zeros_like(refs[-1])

    sem = pltpu.SemaphoreType.DMA((n_sends,))
    res = pl.pallas_call(
        body, name=name, in_specs=[_HBM] * (ns + nl) + [_ANY],
        out_shape=(sem, sem, *[pltpu.HBM(s.shape, s.dtype) for s in land_shapes], _SDS((8, _LANES), F32)),
        out_specs=(_SEM, _SEM, *[_HBM] * nl, pl.BlockSpec(memory_space=pltpu.VMEM)),
        input_output_aliases={ns + i: 2 + i for i in range(nl)},
        compiler_params=pltpu.CompilerParams(has_side_effects=_EFFECT))(
            *[_hbm(s) for s in srcs], *zones, after)
    return dict(sems=res[:2], srcs=srcs, lands=res[2:2 + nl], plan=plan), res[-1]


def _send_wait(name, h, after):
    ns, nl = len(h['srcs']), len(h['lands'])

    def body(*refs):
        ins, lands, sems = refs[:ns], refs[ns:ns + nl], refs[ns + nl:ns + nl + 2]
        x, y, c, chips = _place()
        sends, landings = h['plan'](x, y, c, chips, ins, lands)
        for i, (s, d, dev) in enumerate(sends):
            _remote(s, d, sems, i, dev).wait_send()
        for i, d in enumerate(landings):
            _remote(d, d, sems, i, sends[i][2]).wait_recv()

    return pl.pallas_call(
        body, name=name, in_specs=[_HBM] * (ns + nl) + [_SEM, _SEM, _ANY],
        out_shape=tuple(pltpu.HBM(a.shape, a.dtype) for a in h['lands']), out_specs=tuple([_HBM] * nl),
        input_output_aliases={ns + i: i for i in range(nl)},
        compiler_params=pltpu.CompilerParams(has_side_effects=_EFFECT))(
            *[_hbm(s) for s in h['srcs']], *h['lands'], *h['sems'], after)


def _gather_plan(items):
    def plan(x, y, c, chips, ins, lands):
        k = 2 * x + y
        sends = [(ins[si].at[l], lands[t].at[k], (px, py, c)) for t, (si, l) in enumerate(items) for px, py in chips]
        return sends, [lands[t].at[2 * px + py] for t in range(len(items)) for px, py in chips]
    return plan


_FLIPS = [((d >> 2) & 1, (d >> 1) & 1, d & 1) for d in range(1, 8)]


def _reduce_plan(halves):
    def plan(x, y, c, chips, ins, lands):
        sends, landings = [], []
        for t, hf in enumerate(halves):
            for i, (fx, fy, fc) in enumerate(_FLIPS):
                px, py, pc = x ^ fx, y ^ fy, c ^ fc
                sends.append((ins[t].at[2 * px + py, pl.ds(pc * hf, hf)], lands[t].at[i], (px, py, pc)))
                landings.append(lands[t].at[i])
        return sends, landings
    return plan


def _swap(name, srcs, out_shapes, plan, n_sends):
    n = len(srcs)

    def body(*refs):
        ins, outs, sems = refs[:n], refs[n:n + len(out_shapes)], refs[-2:]
        x, y, c, chips = _place()
        sends, landings = plan(x, y, c, chips, ins, outs)
        out = [_remote(s, d, sems, i, dev) for i, (s, d, dev) in enumerate(sends)]
        for cp in out:
            cp.start()
        for i, d in enumerate(landings):
            _remote(d, d, sems, i, sends[i][2]).wait_recv()
        for cp in out:
            cp.wait_send()

    return pl.pallas_call(
        body, in_specs=[_HBM] * n, out_specs=[_HBM] * len(out_shapes), out_shape=out_shapes, name=name,
        scratch_shapes=[pltpu.SemaphoreType.DMA((n_sends,)), pltpu.SemaphoreType.DMA((n_sends,))])(*srcs)


def _sum_owned(grads, landed, c, k, names):
    def sum8(*parts):
        acc = parts[0].astype(F32)
        for p in parts[1:]:
            acc = acc + p.astype(F32)
        return acc
    outs = []
    for g, got, name in zip(grads, landed, names):
        hf, b = got.shape[1:]
        own = lax.dynamic_slice_in_dim(lax.dynamic_index_in_dim(g, k, axis=0, keepdims=False), c * hf, hf, axis=0)
        outs.append(_rows(sum8, [own] + [("slot", got, i) for i in range(len(_FLIPS))], [], [(b, F32)],
                          tile=_pick(hf, (_tile_for(b), 32)), name=f"grad_sum_{name}")[0])
    return outs


def _share_halves(mine, *, name):
    n = len(mine)

    def plan(x, y, c_, chips, ins, outs):
        return [(ins[t], outs[t], (x, y, 1 - c_)) for t in range(n)], [outs[t] for t in range(n)]
    return _swap(name, mine, [_SDS(h.shape, F32) for h in mine], plan, n)


def _adamw_owned(w, mine, theirs, m, v, c, *, name):
    depth, a, b = w.shape
    half = a // 2
    tile = _pick(half, (_tile_for(b), 32, 8))
    nh = half // tile

    def blocks_of(l):
        return lambda i: (jnp.clip(i - 2 * nh * l, 0, 2 * nh - 1) % nh, 0)

    def fn(w, m, v, *rest):
        halves, cflag = rest[:-1], rest[-1]
        step = pl.program_id(0)
        is_mine = cflag[0:1, 0:1] == ((step // nh) % 2).astype(F32)
        g = jnp.where(is_mine, halves[0], halves[1])
        for l in range(1, depth):
            g = jnp.where(step >= 2 * nh * l, jnp.where(is_mine, halves[2 * l], halves[2 * l + 1]), g)
        return (g,) + _adamw_math(w, g, m, v)
    ins = [a_.reshape(depth * a, b) for a_ in (w, m, v)]
    ins += [(h, b, blocks_of(l)) for l in range(depth) for h in (mine[l], theirs[l])]
    res = _rows(fn, ins, [jnp.full((1, _LANES), c, F32)], [(b, F32)] * 4, tile=tile, name=name)
    return [r.reshape(w.shape) for r in res]


_BIG = ("w_in", "w_out", "w_mlp_in", "w_mlp_out")
_SMALL = ("ln1_g", "conv_b", "dt_bias", "a_log", "d_skip", "attn_norm_g", "ssd_norm_g", "ln2_g", "final_norm_g")
_ORDER = ("ln1_g", "w_in", "conv_w", "conv_b", "dt_bias", "a_log", "d_skip", "attn_norm_g", "ssd_norm_g", "w_out", "ln2_g",
          "w_mlp_in", "w_mlp_out", "final_norm_g")


def _pack(parts, rows):
    flat = jnp.concatenate([p.reshape(-1) for p in parts])
    return jnp.pad(flat, (0, rows * _LANES - flat.shape[0])).reshape(rows, _LANES)


def _unpack(buf, like):
    flat, out, o = buf.reshape(-1), [], 0
    for p in like:
        out.append(flat[o:o + p.size].reshape(p.shape))
        o += p.size
    return out


def kernel(x, ln1_g, w_in, conv_w, conv_b, dt_bias, a_log, d_skip, attn_norm_g, ssd_norm_g, w_out, ln2_g, w_mlp_in, w_mlp_out, final_norm_g, loss_target, m_ln1_g, m_w_in, m_conv_w, m_conv_b, m_dt_bias, m_a_log, m_d_skip, m_attn_norm_g, m_ssd_norm_g, m_w_out, m_ln2_g, m_w_mlp_in, m_w_mlp_out, m_final_norm_g, v_ln1_g, v_w_in, v_conv_w, v_conv_b, v_dt_bias, v_a_log, v_d_skip, v_attn_norm_g, v_ssd_norm_g, v_w_out, v_ln2_g, v_w_mlp_in, v_w_mlp_out, v_final_norm_g):
    w = dict(ln1_g=ln1_g, w_in=w_in, conv_w=conv_w, conv_b=conv_b, dt_bias=dt_bias, a_log=a_log, d_skip=d_skip,
             attn_norm_g=attn_norm_g, ssd_norm_g=ssd_norm_g, w_out=w_out, ln2_g=ln2_g, w_mlp_in=w_mlp_in, w_mlp_out=w_mlp_out,
             final_norm_g=final_norm_g)
    m = dict(ln1_g=m_ln1_g, w_in=m_w_in, conv_w=m_conv_w, conv_b=m_conv_b, dt_bias=m_dt_bias, a_log=m_a_log, d_skip=m_d_skip,
             attn_norm_g=m_attn_norm_g, ssd_norm_g=m_ssd_norm_g, w_out=m_w_out, ln2_g=m_ln2_g, w_mlp_in=m_w_mlp_in,
             w_mlp_out=m_w_mlp_out, final_norm_g=m_final_norm_g)
    v = dict(ln1_g=v_ln1_g, w_in=v_w_in, conv_w=v_conv_w, conv_b=v_conv_b, dt_bias=v_dt_bias, a_log=v_a_log, d_skip=v_d_skip,
             attn_norm_g=v_attn_norm_g, ssd_norm_g=v_ssd_norm_g, w_out=v_w_out, ln2_g=v_ln2_g, w_mlp_in=v_w_mlp_in,
             w_mlp_out=v_w_mlp_out, final_norm_g=v_final_norm_g)
    depth, d_model = ln1_g.shape
    n_chips = 4
    c = lax.axis_index("c")
    chip = 2 * lax.axis_index("x") + lax.axis_index("y")
    in_proj = w_in.shape[2] * n_chips
    cch = conv_w.shape[2] * n_chips
    zdt_pad = _LANES - _HEADS

    cw = _exchange8(conv_w.reshape(depth * _CONV_K, -1), reduce=False, name="gather_conv_w")[0::2]
    conv_full = cw.reshape(n_chips, depth, _CONV_K, -1).transpose(1, 2, 0, 3).reshape(depth, _CONV_K, cch)
    own = [w[n].astype(_BF) for n in _BIG]
    is_own = (jnp.arange(n_chips) == chip).reshape(n_chips, 1, 1)

    def start_gather(tag, items, after):
        lands = [_SDS((n_chips, *own[i].shape[1:]), _BF) for i, _ in items]
        return _send_start(f"gather_start_{tag}", own, lands, _gather_plan(items), 3 * len(items), after)

    def finish_gather(tag, handle, items, after):
        landed = _send_wait(f"gather_wait_{tag}", handle, after)
        return {_BIG[i]: jnp.where(is_own, own[i][l][None], g) for (i, l), g in zip(items, landed)}

    def layer_weights(l, blocks):
        p = {}
        if 'w_in' in blocks:
            full_in = blocks['w_in'].transpose(1, 0, 2).reshape(d_model, in_proj)
            p['w_in'] = jnp.concatenate([full_in[:, :3 * _AW], full_in[:, 4 * _AW:4 * _AW + cch], full_in[:, 3 * _AW:4 * _AW],
                                         full_in[:, 4 * _AW + cch:], jnp.zeros((d_model, zdt_pad), _BF)], axis=1)
        if 'w_out' in blocks:
            p['w_out'] = blocks['w_out'].reshape(-1, d_model)
            p['w_mlp_in'] = blocks['w_mlp_in']
            p['w_mlp_out'] = blocks['w_mlp_out'].reshape(-1, d_model)
        return p

    groups = dict(a=[(0, 0)], b=[(1, 0), (2, 0), (3, 0)], c=[(0, 1)], d=[(1, 1), (2, 1), (3, 1)])
    handles, token = {}, conv_full

    half_in = own[0].shape[1] // 2

    def rows_of(ref, who):
        return ref.at[pl.ds(who * half_in, half_in)]

    def plan_a(x_, y_, c_, chips, ins, lands):
        k = 2 * x_ + y_
        sends = [(rows_of(ins[0].at[0], c_), rows_of(lands[0].at[k], c_), (px, py, c_)) for px, py in chips]
        return sends, [rows_of(lands[0].at[2 * px + py], c_) for px, py in chips]

    def plan_pass(x_, y_, c_, chips, ins, lands):
        sends = [(rows_of(lands[0].at[2 * px + py], c_),) * 2 + ((x_, y_, 1 - c_),) for px, py in chips]
        return sends, [rows_of(lands[0].at[2 * px + py], 1 - c_) for px, py in chips]
    handles["a"], token = _send_start("gather_start_a", own[:1], [_SDS((n_chips, *own[0].shape[1:]), _BF)], plan_a, 3, token)
    for tag, items in list(groups.items())[1:]:
        handles[tag], token = start_gather(tag, items, token)
    landed = _send_wait("gather_land_a", handles["a"], token)
    handles["a"], token = _send_start("gather_pass_a", [], landed, plan_pass, 3, landed[0])
    layers = [{n: w[n][l] for n in _SMALL[:-1]} for l in range(depth)]
    for l in range(depth):
        layers[l]['conv_w'] = conv_full[l]

    layers[0].update(layer_weights(0, finish_gather("a", handles["a"], groups["a"], token)))
    mix, sv0 = _layer_fwd(x[0], layers[0], 0)
    layers[0].update(layer_weights(0, finish_gather("b", handles["b"], groups["b"], mix)))
    h = _layer_fwd_mlp(layers[0], sv0, 0)
    layers[1].update(layer_weights(1, finish_gather("c", handles["c"], groups["c"], h)))
    mix, sv1 = _layer_fwd(h, layers[1], 1)
    layers[1].update(layer_weights(1, finish_gather("d", handles["d"], groups["d"], mix)))
    h = _layer_fwd_mlp(layers[1], sv1, 1)
    saved = [sv0, sv1]

    def by_chip(g, name):
        if name == "w_mlp_in":
            return g
        if name == "w_in":
            return g.reshape(d_model, n_chips, -1).transpose(1, 0, 2)
        return g.reshape(n_chips, -1, d_model)

    pending = []

    def sender(l):
        def send(names, g):
            srcs = [by_chip(g[n], n) for n in names]
            halves = [s.shape[1] // 2 for s in srcs]
            lands = [_SDS((len(_FLIPS), hf, s.shape[2]), _BF) for s, hf in zip(srcs, halves)]
            handle, tok = _send_start(f"grad_start_{names[-1]}_{l}", srcs, lands, _reduce_plan(halves), len(_FLIPS) * len(srcs), srcs[0])
            pending.append((l, names, srcs, handle))
            return tok
        return send

    dx, g_final, loss_part = _loss_bwd(h, final_norm_g, loss_target[0])
    grads, after = [None] * depth, None
    for l in reversed(range(depth)):
        dx, grads[l] = _layer_bwd(dx, layers[l], saved[l], l, sender(l), after)
        after = dx
    landed_of = {}
    sent_in = {(n, l): (gi, j) for gi, (l, names, _, _) in enumerate(pending) for j, n in enumerate(names)}

    def landed_for(gi, after):
        if gi not in landed_of:
            l, names, _, handle = pending[gi]
            landed_of[gi] = _send_wait(f"grad_wait_{names[-1]}_{l}", handle, after)
        return landed_of[gi]

    red, delta, new_m, new_v = {}, {}, {}, {}
    after = dx
    for n in ("w_mlp_out", "w_mlp_in", "w_out", "w_in"):
        mine = []
        for l in range(depth):
            gi, j = sent_in[(n, l)]
            got = landed_for(gi, after)[j]
            mine.append(_sum_owned([pending[gi][2][j]], [got], c, chip, [f"{n}_{l}"])[0])
        theirs = _share_halves(mine, name=f"grad_share_{n}")
        red[n], delta[n], new_m[n], new_v[n] = _adamw_owned(w[n], mine, theirs, m[n], v[n], c, name=f"adamw_{n}")
        after = delta[n]

    small = {n: jnp.stack([grads[l][n] for l in range(depth)]) for n in _SMALL[:-1] + ("conv_w",)}
    small["final_norm_g"] = g_final
    parts = [loss_part.reshape(1)] + [small[n] for n in _SMALL + ("conv_w",)]
    rows = -(-sum(p.size for p in parts) // 1024) * 8
    tot = _unpack(_exchange8(_pack(parts, rows), reduce=True, after=red[_BIG[0]], name="allreduce_small"), parts)
    loss = tot[0][0]
    red.update(zip(_SMALL + ("conv_w",), tot[1:]))
    red["conv_w"] = lax.dynamic_index_in_dim(red["conv_w"].reshape(depth, _CONV_K, n_chips, -1), chip, axis=2, keepdims=False)

    names = _SMALL + ("conv_w",)
    like = [w[n] for n in names]
    srows = -(-sum(p.size for p in like) // 1024) * 8
    res = _adamw(*[_pack([d[n] for n in names], srows) for d in (w, red, m, v)], name="adamw_small")
    for dst, buf in zip((delta, new_m, new_v), res):
        dst.update(zip(names, _unpack(buf, like)))
    return (loss, dx[None], *[red[n] for n in _ORDER], *[delta[n] for n in _ORDER], *[new_m[n] for n in _ORDER],
            *[new_v[n] for n in _ORDER])
```

```python
import numpy as np
import jax
import jax.numpy as jnp
from jax import lax
from jax.experimental import pallas as pl
from jax.experimental.pallas import tpu as pltpu

F32 = jnp.float32
_BF = jnp.bfloat16
_NEG = -1e30
_EPS = 1e-5
_HEADS = 16
_HDIM = 64
_AW = _HEADS * _HDIM
_ABLK = 128
_DILATIONS = (1, 4, 16)
_CHUNK = 128
_NSTATE = 128
_GROUPS = 2
_HPG = _HEADS // _GROUPS
_CONV_K = 4
_LANES = 128
_CHIPS = 4
_LR, _B1, _B2, _AEPS, _WD, _STEP = 0.001, 0.9, 0.999, 1e-08, 0.01, 10
_VMEM_CAP = 56 * 1024 * 1024
_MESH = pl.DeviceIdType.MESH
_SDS = jax.ShapeDtypeStruct
_NT = (((1,), (1,)), ((), ()))
_TN = (((0,), (0,)), ((), ()))


def _params(sem, est_bytes):
    lim = int(min(max(2 * est_bytes + (4 << 20), 16 << 20), _VMEM_CAP))
    return pltpu.CompilerParams(dimension_semantics=sem, vmem_limit_bytes=lim)


def _nbytes(shape, dtype):
    return int(np.prod(shape)) * jnp.dtype(dtype).itemsize


def _hbm(a):
    return pltpu.with_memory_space_constraint(a, pltpu.HBM)


def _dot(a, b, dims=(((1,), (0,)), ((), ()))):
    return lax.dot_general(a.astype(_BF), b.astype(_BF), dims, preferred_element_type=F32)


_HALO = 8


def _rows(fn, ins, consts, outs, sums=(), *, halos=(), into=None, after=None, tile, name):
    rows = (ins[0][0] if isinstance(ins[0], tuple) else ins[0]).shape[0]
    n_steps = rows // tile

    def norm_in(a):
        if not isinstance(a, tuple):
            return a, tile, a.shape[1], lambda i: (i, 0)
        if isinstance(a[0], str) and a[0] == "slot":
            return a[1], (None, tile, a[1].shape[2]), a[1].shape[2], lambda i, s=a[2]: (s, i, 0)
        if isinstance(a[0], str):
            return a[1], tile // a[2], a[1].shape[1], lambda i: (i, 0)
        return a[0], tile, a[1], a[2] if callable(a[2]) else (lambda i, j=a[2]: (i, j))
    ins = [norm_in(a) for a in ins]
    outs = [(w, dt, d[0] if d else 1) for w, dt, *d in outs]
    n_in, n_h, n_c, n_o, n_s = len(ins), len(halos), len(consts), len(outs), len(sums)
    n_x = int(into is not None and into[0] is not None) + int(after is not None)

    def body(*refs):
        step = pl.program_id(0)
        vals = [r[...] for r in refs[:n_in]]
        for r, (_, side) in zip(refs[n_in:n_in + n_h], halos):
            vals.append(jnp.where(step == (0 if side < 0 else n_steps - 1), 0.0, r[...]))
        vals += [r[...] for r in refs[n_in + n_h:n_in + n_h + n_c]]
        refs = refs[:n_in] + refs[n_in + n_h:]
        res = fn(*vals)
        res = res if isinstance(res, tuple) else (res,)
        orefs = refs[n_in + n_c + n_x:n_in + n_c + n_x + n_o]
        srefs = refs[n_in + n_c + n_x + n_o:]
        for r, v in zip(orefs, res[:n_o]):
            r[...] = v.astype(r.dtype)
        if n_s:
            @pl.when(pl.program_id(0) == 0)
            def _():
                for r in srefs:
                    r[...] = jnp.zeros_like(r)
            for r, v in zip(srefs, res[n_o:]):
                r[...] += v.reshape(tile // 8, 8, v.shape[-1]).sum(axis=0)

    per = tile // _HALO
    in_specs = [pl.BlockSpec(r if isinstance(r, tuple) else (r, w), idx) for _, r, w, idx in ins]
    in_specs += [pl.BlockSpec((_HALO, a.shape[1]), (lambda i: (jnp.maximum(i * per - 1, 0), 0)) if side < 0
                              else (lambda i: (jnp.minimum((i + 1) * per, rows // _HALO - 1), 0))) for a, side in halos]
    in_specs += [pl.BlockSpec(c.shape, lambda i, nd=c.ndim: (0,) * nd) for c in consts]
    out_shape = [_SDS((rows // d, d * w), dt) for w, dt, d in outs] + [_SDS((8, w), F32) for w in sums]
    out_specs = [pl.BlockSpec((tile // d, d * w), lambda i: (i, 0)) for w, _, d in outs]
    out_specs += [pl.BlockSpec((8, w), lambda i: (0, 0)) for w in sums]
    est = (sum(_nbytes((tile if isinstance(r, tuple) else r, w), a.dtype) for a, r, w, _ in ins)
           + sum(_nbytes((tile, w), dt) for w, dt, _ in outs))
    shared, aliases = [], {}
    if into is not None:
        buf, total, j = into
        out_shape[0] = _SDS((rows, total), outs[0][1])
        out_specs[0] = pl.BlockSpec((tile, outs[0][0]), lambda i: (i, j))
        if buf is not None:
            shared, aliases = [buf], {n_in + n_h + n_c: 0}
    if after is not None:
        shared.append(after)
    in_specs += [pl.BlockSpec(memory_space=pl.ANY)] * len(shared)
    return pl.pallas_call(body, grid=(n_steps,), in_specs=in_specs, out_specs=out_specs, out_shape=out_shape, name=name,
                          input_output_aliases=aliases, compiler_params=_params(("arbitrary",), 3 * est))(
                              *[_hbm(a[0]) for a in ins], *[_hbm(a) for a, _ in halos], *consts, *shared)


def _perm(d, tile):
    p = np.zeros((tile, tile), np.float32)
    t = np.arange(tile)
    p[t, (t % d) * (tile // d) + t // d] = 1.0
    return jnp.asarray(p, _BF)


def _unstride(s, p):
    d = p.shape[0] // s.shape[0]
    w = s.shape[1] // d
    return _dot(p, jnp.concatenate([s[:, r * w:(r + 1) * w] for r in range(d)], axis=0))


def _stride(x, p, d):
    z = _dot(p, x, _TN)
    n = x.shape[0] // d
    return jnp.concatenate([z[r * n:(r + 1) * n] for r in range(d)], axis=1)


def _shifted(u, halo, back):
    n = u.shape[0] + _HALO
    if back:
        ext = jnp.concatenate([halo, u], axis=0)
        return [pltpu.roll(ext, j, 0)[_HALO:] for j in (1, 2, 3)]
    ext = jnp.concatenate([u, halo], axis=0)
    return [pltpu.roll(ext, n - j, 0)[:u.shape[0]] for j in (1, 2, 3)]


def _tile_for(width):
    return max(c for c in (256, 128, 64, 32) if c * width <= (1 << 18) or c == 32)


def _rstd(x):
    return lax.rsqrt(jnp.mean(x * x, axis=-1, keepdims=True) + _EPS)


def _split(x, groups):
    w = x.shape[-1] // groups
    return [x[:, g * w:(g + 1) * w] for g in range(groups)]


def _cat(parts):
    return parts[0] if len(parts) == 1 else jnp.concatenate(parts, axis=-1)


def _rms_bwd_tile(x, dy, g, groups):
    dxs, dgs = [], []
    for xs, ds, gs in zip(_split(x, groups), _split(dy.astype(F32), groups), _split(g, groups)):
        r = _rstd(xs)
        xh = xs * r
        gd = ds * gs
        dxs.append(r * (gd - xh * jnp.mean(gd * xh, axis=-1, keepdims=True)))
        dgs.append(ds * xh)
    return _cat(dxs), _cat(dgs)


def _rms_fwd(x, g, *, groups=1, into=None, name):
    def fn(x, g):
        return _cat([xs * _rstd(xs) * gs for xs, gs in zip(_split(x, groups), _split(g, groups))])
    w = x.shape[1]
    return _rows(fn, [x], [g.reshape(1, w)], [(w, _BF)], into=into, tile=_tile_for(w), name=name)[0]


def _rms_bwd(x, dy, g, res, *, name):
    def fn(x, dy, res, g):
        dx, dg = _rms_bwd_tile(x, dy, g, 1)
        return dx + res, dx + res, dg
    w = x.shape[1]
    dx, dxb, dg = _rows(fn, [x, dy, res], [g.reshape(1, w)], [(w, F32), (w, _BF)], [w], tile=_tile_for(w), name=name)
    return dx, dxb, dg.sum(axis=0)


def _pick(n, cands):
    for c in cands:
        if n % c == 0:
            return c
    raise ValueError(f"no block size for {n}")


_MM_BLOCKS = (1024, 1152, 512, 384)


def _mm(a, b, *, ta=False, tb=False, extra=(), epi=None, outs=(F32,), after=None, b_chips=0, out_chips=0, b_cols=None, name):
    m, k = (a.shape[1], a.shape[0]) if ta else a.shape
    b_shape = (b.shape[1], b.shape[2] * b_chips) if b_chips else b.shape
    if b_cols is not None:
        b_shape = (b.shape[0], b_cols[1])
    n = b_shape[0] if tb else b_shape[1]
    assert k == (b_shape[1] if tb else b_shape[0])
    n_cap = n // max(out_chips, 1 if tb else b_chips, 1)
    k_cap = k // (b_chips if (b_chips and tb) else 1)
    bm, bn = _pick(m, _MM_BLOCKS), _pick(n_cap, _MM_BLOCKS)
    bk = _pick(k_cap, (2048, 1920) + _MM_BLOCKS)
    nk = k // bk
    n_e, n_o = len(extra), len(outs)
    behind = [] if after is None else [after]
    dims = (((0 if ta else 1,), (1 if tb else 0,)), ((), ()))

    def body(a_ref, b_ref, *rest):
        ex, orefs, acc = rest[:n_e], rest[n_e + len(behind):n_e + len(behind) + n_o], rest[-1]
        kk = pl.program_id(2)

        @pl.when(kk == 0)
        def _():
            acc[...] = jnp.zeros_like(acc)

        acc[...] += _dot(a_ref[...], b_ref[...], dims)

        @pl.when(kk == nk - 1)
        def _():
            r = acc[...]
            res = epi(r, *[e[...] for e in ex]) if epi is not None else (r,)
            for o, v in zip(orefs, res):
                o[...] = v.astype(o.dtype)

    a_spec = pl.BlockSpec((bk, bm), lambda i, j, kk: (kk, i)) if ta else pl.BlockSpec((bm, bk), lambda i, j, kk: (i, kk))
    if b_chips and tb:
        per = k_cap // bk
        b_spec = pl.BlockSpec((None, bn, bk), lambda i, j, kk: (kk // per, j, kk % per))
    elif b_chips:
        per = n_cap // bn
        b_spec = pl.BlockSpec((None, bk, bn), lambda i, j, kk: (j // per, kk, j % per))
    else:
        first = 0 if b_cols is None else b_cols[0] // bn
        assert b_cols is None or (not tb and b_cols[0] % bn == 0)
        b_spec = pl.BlockSpec((bn, bk), lambda i, j, kk: (j, kk)) if tb else pl.BlockSpec((bk, bn), lambda i, j, kk: (kk, first + j))
    t_spec = pl.BlockSpec((bm, bn), lambda i, j, kk: (i, j))
    o_spec, o_shape = t_spec, (m, n)
    if out_chips:
        per_o = n_cap // bn
        o_spec, o_shape = pl.BlockSpec((None, bm, bn), lambda i, j, kk: (j // per_o, i, j % per_o)), (out_chips, m, n_cap)
    est = (_nbytes((bm, bk), a.dtype) + _nbytes((bk, bn), b.dtype) + sum(_nbytes((bm, bn), e.dtype) for e in extra)
           + sum(_nbytes((bm, bn), o) for o in outs)) * 2 + 2 * _nbytes((bm, bn), F32)
    res = pl.pallas_call(
        body, grid=(m // bm, n // bn, nk), in_specs=[a_spec, b_spec] + [t_spec] * n_e + [pl.BlockSpec(memory_space=pl.ANY)] * len(behind),
        out_specs=[o_spec] * n_o, out_shape=[_SDS(o_shape, o) for o in outs], scratch_shapes=[pltpu.VMEM((bm, bn), F32)], name=name,
        compiler_params=_params(("parallel", "parallel", "arbitrary"), est))(_hbm(a), _hbm(b), *[_hbm(e) for e in extra], *behind)
    return res[0] if n_o == 1 else res


def _add_to(acc, r):
    return (acc + r,)


def _alibi_bias(dilation):
    slopes = 2.0 ** (-8.0 * (np.arange(_HEADS) + 1) / _HEADS)
    i = np.arange(_ABLK)[:, None]
    j = np.arange(_ABLK)[None, :]
    cur = np.where(i - j >= 0, -slopes[:, None, None] * ((i - j) * dilation), _NEG)
    prev = np.where(j >= i, -slopes[:, None, None] * ((i - j + _ABLK) * dilation), _NEG)
    both = np.stack([np.concatenate([np.full_like(prev, _NEG), cur], axis=2), np.concatenate([prev, cur], axis=2)])
    return jnp.asarray(both.reshape(2, _HEADS // 2, 2 * _ABLK, 2 * _ABLK), F32)


def _bias_spec():
    return pl.BlockSpec((None, _HEADS // 2, 2 * _ABLK, 2 * _ABLK), lambda r, j: (jnp.minimum(j, 1), 0, 0, 0))


def _strided(a, d):
    return a.reshape(a.shape[0] // d, d * a.shape[1])


def _pair(pr):
    return slice(pr * _LANES, (pr + 1) * _LANES)


def _low_lanes(shape):
    return lax.broadcasted_iota(jnp.int32, shape, 1) < _HDIM


def _halves(v, low):
    z = jnp.zeros_like(v)
    return jnp.where(low, v, z), jnp.where(low, z, v)


def _lane_spec(nb):
    return pl.BlockSpec((_ABLK, _LANES), lambda r, j: (jnp.minimum(j, nb - 1), r))


def _expand_heads(v):
    low = _low_lanes((v.shape[0], _LANES))
    return jnp.concatenate([jnp.where(low, v[:, 2 * pr:2 * pr + 1], v[:, 2 * pr + 1:2 * pr + 2]) for pr in range(_HEADS // 2)], axis=1)


def _attn_specs(nb, n_parts):
    def cur(p):
        return pl.BlockSpec((_ABLK, _AW), lambda r, j: (jnp.minimum(j, nb - 1), r * n_parts + p))

    def prev(p):
        return pl.BlockSpec((_ABLK, _AW), lambda r, j: (jnp.clip(j - 1, 0, nb - 1), r * n_parts + p))
    return cur, prev


def _attn_fwd(qkv, dilation, *, name):
    t = qkv.shape[0]
    nb = t // dilation // _ABLK
    bias = _alibi_bias(dilation)
    scale = _HDIM ** -0.5

    def body(q_ref, kc_ref, kp_ref, vc_ref, vp_ref, b_ref, o_ref, l_ref):
        low = _low_lanes((_ABLK, _LANES))
        l_ref[...] = jnp.zeros_like(l_ref)
        for pr in range(_HEADS // 2):
            sl = _pair(pr)
            k2 = jnp.concatenate([kp_ref[:, sl], kc_ref[:, sl]], axis=0)
            v2 = jnp.concatenate([vp_ref[:, sl], vc_ref[:, sl]], axis=0)
            q2 = jnp.concatenate(_halves(q_ref[:, sl] * scale, low), axis=0)
            s = _dot(q2, k2, _NT) + b_ref[pr]
            m = jnp.max(s, axis=-1, keepdims=True)
            p = jnp.exp(s - m)
            den = jnp.sum(p, axis=-1, keepdims=True)
            o = _dot(p, v2) / den
            lse = m + jnp.log(den)
            l_ref[:, 2 * pr:2 * pr + 1] = lse[:_ABLK]
            l_ref[:, 2 * pr + 1:2 * pr + 2] = lse[_ABLK:]
            o_ref[:, sl] = jnp.where(low, o[:_ABLK], o[_ABLK:]).astype(o_ref.dtype)

    cur, prev = _attn_specs(nb, 3)
    cur1, _ = _attn_specs(nb, 1)
    bspec = _bias_spec()
    sv = _hbm(_strided(qkv, dilation))
    o, l = pl.pallas_call(
        body, grid=(dilation, nb), in_specs=[cur(0), cur(1), prev(1), cur(2), prev(2), bspec],
        out_specs=[cur1(0), _lane_spec(nb)],
        out_shape=[_SDS((t // dilation, dilation * _AW), _BF), _SDS((t // dilation, dilation * _LANES), F32)], name=name,
        compiler_params=_params(("parallel", "arbitrary"), 16 << 20))(sv, sv, sv, sv, sv, bias)
    return o, l.reshape(t, _LANES)


def _attn_bwd(qkv, do, ld, dilation, *, name):
    t = qkv.shape[0]
    nb = t // dilation // _ABLK
    bias = _alibi_bias(dilation)
    scale = _HDIM ** -0.5

    def body(q_ref, kc_ref, kp_ref, vc_ref, vp_ref, do_ref, ld_ref, b_ref, dq_ref, dk_ref, dv_ref, ck, cv):
        n = pl.program_id(1)

        @pl.when(n == 0)
        def _():
            ck[...] = jnp.zeros_like(ck)
            cv[...] = jnp.zeros_like(cv)

        @pl.when(n < nb)
        def _():
            low = _low_lanes((_ABLK, _LANES))
            for pr in range(_HEADS // 2):
                sl = _pair(pr)
                k2 = jnp.concatenate([kp_ref[:, sl], kc_ref[:, sl]], axis=0)
                v2 = jnp.concatenate([vp_ref[:, sl], vc_ref[:, sl]], axis=0)
                q2 = jnp.concatenate(_halves(q_ref[:, sl] * scale, low), axis=0)
                do2 = jnp.concatenate(_halves(do_ref[:, sl], low), axis=0)
                lrow = jnp.concatenate([ld_ref[:, 2 * pr:2 * pr + 1], ld_ref[:, 2 * pr + 1:2 * pr + 2]], axis=0)
                dsum = jnp.concatenate([ld_ref[:, _HEADS + 2 * pr:_HEADS + 2 * pr + 1],
                                        ld_ref[:, _HEADS + 2 * pr + 1:_HEADS + 2 * pr + 2]], axis=0)
                p = jnp.exp(_dot(q2, k2, _NT) + b_ref[pr] - lrow)
                ds = (p * (_dot(do2, v2, _NT) - dsum)).astype(_BF)
                dq = _dot(ds, k2)
                dk2, dv2 = _dot(ds, q2, _TN), _dot(p, do2, _TN)
                dq_ref[:, sl] = (jnp.where(low, dq[:_ABLK], dq[_ABLK:]) * scale).astype(dq_ref.dtype)
                dk_ref[:, sl] = (ck[:, sl] + dk2[:_ABLK]).astype(dk_ref.dtype)
                dv_ref[:, sl] = (cv[:, sl] + dv2[:_ABLK]).astype(dv_ref.dtype)
                ck[:, sl] = dk2[_ABLK:]
                cv[:, sl] = dv2[_ABLK:]

        @pl.when(n == nb)
        def _():
            dk_ref[...] = ck[...].astype(dk_ref.dtype)
            dv_ref[...] = cv[...].astype(dv_ref.dtype)

    cur, prev = _attn_specs(nb, 3)
    cur1, prev1 = _attn_specs(nb, 1)
    bspec = _bias_spec()
    sv, dov, ldv = _hbm(_strided(qkv, dilation)), _hbm(do), _hbm(_strided(ld, dilation))
    dqkv = pl.pallas_call(
        body, grid=(dilation, nb + 1),
        in_specs=[cur(0), cur(1), prev(1), cur(2), prev(2), cur1(0), _lane_spec(nb), bspec],
        out_specs=[cur1(0), prev1(0), prev1(0)], out_shape=[_SDS(dov.shape, _BF)] * 3, name=name,
        scratch_shapes=[pltpu.VMEM((_ABLK, _AW), F32)] * 2,
        compiler_params=_params(("parallel", "arbitrary"), 16 << 20))(sv, sv, sv, sv, sv, dov, ldv, bias)
    return dqkv


def _ssd_in_specs(ch):
    return dict(
        xs=pl.BlockSpec((_CHUNK, _AW), lambda c: (ch(c), 0)),
        bc=pl.BlockSpec((_CHUNK, 2 * _GROUPS * _NSTATE), lambda c: (ch(c), _AW // (2 * _GROUPS * _NSTATE))),
        lane=pl.BlockSpec((_CHUNK, _LANES), lambda c: (ch(c), 0)),
        arow=pl.BlockSpec((_HEADS, 1, _CHUNK), lambda c: (0, 0, ch(c))),
        st=pl.BlockSpec((1, _HEADS // 2, _NSTATE, _LANES), lambda c: (ch(c), 0, 0, 0)),
    )


def _decay(a_col, a_row):
    i0 = lax.broadcasted_iota(jnp.int32, (_CHUNK, _CHUNK), 0)
    i1 = lax.broadcasted_iota(jnp.int32, (_CHUNK, _CHUNK), 1)
    return jnp.where(i0 >= i1, jnp.exp(a_col - a_row), 0.0), jnp.where(i1 >= i0, jnp.exp(a_row - a_col), 0.0)


def _rsum(v):
    return jnp.sum(v, axis=-1, keepdims=True)


def _ssd_fwd(act, dt, acum, a_row, *, name):
    t = act.shape[0]
    nc = t // _CHUNK
    sp = _ssd_in_specs(lambda c: c)
    gw = _GROUPS * _NSTATE

    def body(xs_ref, bc_ref, dt_ref, ac_ref, ar_ref, y_ref, sall_ref, st):
        @pl.when(pl.program_id(0) == 0)
        def _():
            st[...] = jnp.zeros_like(st)

        low = _low_lanes((_CHUNK, _LANES))
        for g in range(_GROUPS):
            bg = bc_ref[:, g * _NSTATE:(g + 1) * _NSTATE]
            cg = bc_ref[:, gw + g * _NSTATE:gw + (g + 1) * _NSTATE].astype(_BF)
            cb = _dot(cg, bg, _NT)
            for pr in range(g * _HPG // 2, (g + 1) * _HPG // 2):
                ha, hb = 2 * pr, 2 * pr + 1
                a_a, a_b = ac_ref[:, ha:ha + 1], ac_ref[:, hb:hb + 1]
                x = (xs_ref[:, _pair(pr)] * jnp.where(low, dt_ref[:, ha:ha + 1], dt_ref[:, hb:hb + 1])).astype(_BF)
                lm_a, _ = _decay(a_a, ar_ref[ha])
                lm_b, _ = _decay(a_b, ar_ref[hb])
                sv = st[pr]
                sall_ref[0, pr] = sv
                yd = _dot(jnp.concatenate([cb * lm_a, cb * lm_b], axis=0), x)
                yd = jnp.where(low, yd[:_CHUNK], yd[_CHUNK:])
                y_ref[:, _pair(pr)] = yd + jnp.where(low, jnp.exp(a_a), jnp.exp(a_b)) * _dot(cg, sv)
                al_a, al_b = jnp.min(a_a, axis=0, keepdims=True), jnp.min(a_b, axis=0, keepdims=True)
                upd = _dot(jnp.concatenate([bg * jnp.exp(al_a - a_a), bg * jnp.exp(al_b - a_b)], axis=1), x, _TN)
                st[pr] = jnp.where(low, jnp.exp(al_a), jnp.exp(al_b)) * sv + jnp.where(low, upd[:_NSTATE], upd[_NSTATE:])

    return pl.pallas_call(
        body, grid=(nc,), in_specs=[sp['xs'], sp['bc'], sp['lane'], sp['lane'], sp['arow']],
        out_specs=[sp['xs'], sp['st']], out_shape=[_SDS((t, _AW), F32), _SDS((nc, _HEADS // 2, _NSTATE, _LANES), F32)],
        scratch_shapes=[pltpu.VMEM((_HEADS // 2, _NSTATE, _LANES), F32)], name=name,
        compiler_params=_params(("arbitrary",), 16 << 20))(*[_hbm(a) for a in (act, act, dt, acum, a_row)])


def _ssd_bwd(act, dt, acum, a_row, sall, dy, *, name):
    t = act.shape[0]
    nc = t // _CHUNK
    sp = _ssd_in_specs(lambda c: nc - 1 - c)
    gw = _GROUPS * _NSTATE

    def body(xs_ref, bc_ref, dt_ref, ac_ref, ar_ref, sall_ref, dy_ref, dxs_ref, dbc_ref, ddt_ref, da_ref, dst):
        @pl.when(pl.program_id(0) == 0)
        def _():
            dst[...] = jnp.zeros_like(dst)

        ddt_ref[...] = jnp.zeros_like(ddt_ref)
        da_ref[...] = jnp.zeros_like(da_ref)
        row = lax.broadcasted_iota(jnp.int32, (_CHUNK, 1), 0)
        low = _low_lanes((_CHUNK, _LANES))
        for g in range(_GROUPS):
            bg = bc_ref[:, g * _NSTATE:(g + 1) * _NSTATE]
            bgb = bg.astype(_BF)
            cg = bc_ref[:, gw + g * _NSTATE:gw + (g + 1) * _NSTATE].astype(_BF)
            cb, cbt = _dot(cg, bgb, _NT), _dot(bgb, cg, _NT)
            dcb = jnp.zeros((_CHUNK, _CHUNK), F32)
            dbg = jnp.zeros((_CHUNK, _NSTATE), F32)
            dcg = jnp.zeros((_CHUNK, _NSTATE), F32)
            for pr in range(g * _HPG // 2, (g + 1) * _HPG // 2):
                heads = (2 * pr, 2 * pr + 1)
                a_cols = [ac_ref[:, h:h + 1] for h in heads]
                dt_pair = jnp.where(low, dt_ref[:, heads[0]:heads[0] + 1], dt_ref[:, heads[1]:heads[1] + 1])
                xsv = xs_ref[:, _pair(pr)]
                x = xsv * dt_pair
                xb = x.astype(_BF)
                xhs = _halves(xb, low)
                dyv = dy_ref[:, _pair(pr)]
                dyb = dyv.astype(_BF)
                dyhs = _halves(dyb, low)
                sv, dsv = sall_ref[0, pr], dst[pr]
                svb, dsb = sv.astype(_BF), dsv.astype(_BF)
                a_lasts = [jnp.min(a, axis=0, keepdims=True) for a in a_cols]
                e_pair = jnp.where(low, jnp.exp(a_cols[0]), jnp.exp(a_cols[1]))
                el_pair = jnp.where(low, jnp.exp(a_lasts[0]), jnp.exp(a_lasts[1]))
                yo = e_pair * _dot(cg, svb)
                decays = [_decay(a_cols[i], ar_ref[h]) for i, h in enumerate(heads)]
                gms, gmts = [cb * lm for lm, _ in decays], [cbt * lmt for _, lmt in decays]
                w_cols = [jnp.exp(a_lasts[i] - a_cols[i]) for i in range(2)]
                x2, dy2 = jnp.concatenate(xhs, axis=0), jnp.concatenate(dyhs, axis=0)
                bwd = _dot(jnp.concatenate([bg * w_cols[0], bg * w_cols[1]], axis=0), dsb)
                dxg = _dot(jnp.concatenate(gms, axis=1), dyb, _TN)
                dg2, dgt2, xds2 = _dot(dy2, xb, _NT), _dot(x2, dyb, _NT), _dot(x2, dsb, _NT)
                das = []
                for i in range(2):
                    rows_i = slice(i * _CHUNK, (i + 1) * _CHUNK)
                    dcb = dcb + dg2[rows_i] * decays[i][0]
                    dbg = dbg + w_cols[i] * xds2[rows_i]
                    das.append(_rsum(dg2[rows_i] * gms[i]) - _rsum(dgt2[rows_i] * gmts[i]))
                bwd = jnp.where(low, bwd[:_CHUNK], bwd[_CHUNK:])
                dx = jnp.where(low, dxg[:_CHUNK], dxg[_CHUNK:]) + bwd
                edy = (e_pair * dyv).astype(_BF)
                dcg = dcg + _dot(edy, svb, _NT)
                zs, yos, sds, dts = (_halves(v, low) for v in (x * bwd, dyv * yo, sv * dsv, dx * xsv))
                for i, h in enumerate(heads):
                    z = _rsum(zs[i])
                    da_last = jnp.sum(z, axis=0, keepdims=True) + jnp.exp(a_lasts[i]) * jnp.sum(_rsum(sds[i]), axis=0, keepdims=True)
                    da_ref[:, h:h + 1] = das[i] + _rsum(yos[i]) - z + jnp.where(row == _CHUNK - 1, da_last, 0.0)
                    ddt_ref[:, h:h + 1] = _rsum(dts[i])
                dxs_ref[:, _pair(pr)] = dx * dt_pair
                dst[pr] = el_pair * dsv + _dot(cg, edy, _TN)
            dbc_ref[:, g * _NSTATE:(g + 1) * _NSTATE] = dbg + _dot(dcb, cg, _TN)
            dbc_ref[:, gw + g * _NSTATE:gw + (g + 1) * _NSTATE] = dcg + _dot(dcb, bgb)

    ch = lambda c: nc - 1 - c
    wide = pl.BlockSpec((_CHUNK, 2 * gw), lambda c: (ch(c), 0))
    return pl.pallas_call(
        body, grid=(nc,), in_specs=[sp['xs'], sp['bc'], sp['lane'], sp['lane'], sp['arow'], sp['st'], sp['xs']],
        out_specs=[sp['xs'], wide, sp['lane'], sp['lane']],
        out_shape=[_SDS((t, _AW), F32), _SDS((t, 2 * gw), F32), _SDS((t, _LANES), F32), _SDS((t, _LANES), F32)],
        scratch_shapes=[pltpu.VMEM((_HEADS // 2, _NSTATE, _LANES), F32)], name=name,
        compiler_params=_params(("arbitrary",), 16 << 20))(*[_hbm(a) for a in (act, act, dt, acum, a_row, sall, dy)])


def _scan_rows(v, reverse):
    r = lax.broadcasted_iota(jnp.int32, v.shape, 0)
    for s in (1, 2, 4, 8, 16, 32, 64):
        if reverse:
            v = v + jnp.where(r < _CHUNK - s, pltpu.roll(v, _CHUNK - s, 0), 0.0)
        else:
            v = v + jnp.where(r >= s, pltpu.roll(v, s, 0), 0.0)
    return v


def _softplus(x):
    return jnp.maximum(x, 0.0) + jnp.log(1.0 + jnp.exp(-jnp.abs(x)))


def _sigmoid(x):
    return 1.0 / (1.0 + jnp.exp(-x))


def _silu(x):
    return x * _sigmoid(x)


def _dsilu(x):
    s = _sigmoid(x)
    return s * (1.0 + x * (1.0 - s))


def _lanes(a):
    return jnp.pad(a, (0, _LANES - a.shape[0])).reshape(1, _LANES)


def _layer_fwd(x, p, l):
    cch = p['conv_w'].shape[1]
    sv = {}
    h1 = _rms_fwd(x, p['ln1_g'], name=f"ln1_fwd_{l}")
    qkv = _mm(h1, p['w_in'], b_cols=(0, 3 * _AW), outs=(_BF,), name=f"in_proj_qkv_{l}")
    xbc = _mm(h1, p['w_in'], b_cols=(3 * _AW, cch), name=f"in_proj_xbc_{l}")
    zdt = _mm(h1, p['w_in'], b_cols=(3 * _AW + cch, _AW + _LANES), name=f"in_proj_zdt_{l}")
    z, dt_raw = (zdt, _AW, 0), (zdt, _LANES, _AW // _LANES)

    outs = []
    for dil in _DILATIONS:
        outs += _attn_fwd(qkv, dil, name=f"attn_fwd_d{dil}_{l}")

    tile = 2 * _ABLK
    perms = [_perm(d, tile) for d in _DILATIONS[1:]]

    def combine(o1, l1, o2, l2, o3, l3, p2, p3):
        m = jnp.maximum(jnp.maximum(l1, l2), l3)
        e1, e2, e3 = jnp.exp(l1 - m), jnp.exp(l2 - m), jnp.exp(l3 - m)
        tot = e1 + e2 + e3
        mixed = sum(_expand_heads(e / tot) * o for e, o in ((e1, o1.astype(F32)), (e2, _unstride(o2, p2)), (e3, _unstride(o3, p3))))
        return mixed, m + jnp.log(tot)
    outs = [a if i % 2 or i == 0 else ("strided", a, _DILATIONS[i // 2]) for i, a in enumerate(outs)]
    attn, lse = _rows(combine, outs, perms, [(_AW, F32), (_LANES, F32)], tile=tile, name=f"attn_combine_{l}")
    mix = _rms_fwd(attn, p['attn_norm_g'], into=(None, 2 * _AW, 0), name=f"attn_norm_fwd_{l}")

    def conv(u0, before, w, b):
        u1, u2, u3 = _shifted(u0, before, True)
        return _silu(w[0:1] * u3 + w[1:2] * u2 + w[2:3] * u1 + w[3:4] * u0 + b)
    act = _rows(conv, [xbc], [p['conv_w'], p['conv_b'].reshape(1, cch)], [(cch, F32)], halos=[(xbc, -1)], tile=_tile_for(cch),
                name=f"conv_fwd_{l}")[0]

    def dtf(raw, bias, alog):
        dt = _softplus(raw + bias)
        return dt, _scan_rows(dt * -jnp.exp(alog), False)
    dt, acum = _rows(dtf, [dt_raw], [_lanes(p['dt_bias']), _lanes(p['a_log'])], [(_LANES, F32), (_LANES, F32)],
                     tile=_CHUNK, name=f"dt_fwd_{l}")
    a_row = acum[:, :_HEADS].T[:, None, :]
    y_ssd, sall = _ssd_fwd(act, dt, acum, a_row, name=f"ssd_fwd_{l}")
    dskip = jnp.repeat(p['d_skip'], _HDIM).reshape(1, _AW)
    xs = (act, _AW, 0)

    def gate(y, xs, z, dsk):
        return (y + dsk * xs) * _silu(z)
    y2 = _rows(gate, [y_ssd, xs, z], [dskip], [(_AW, F32)], tile=_tile_for(_AW), name=f"gate_fwd_{l}")[0]
    mix = _rms_fwd(y2, p['ssd_norm_g'], groups=_GROUPS, into=(mix, 2 * _AW, 1), name=f"ssd_norm_fwd_{l}")
    sv.update(x=x, h1=h1, qkv=qkv, zdt=zdt, xbc=xbc, attn=attn, lse=lse, act=act, dt=dt, acum=acum, a_row=a_row,
              sall=sall, y_ssd=y_ssd, dskip=dskip, y2=y2, mix=mix)
    return mix, sv


def _layer_fwd_mlp(p, sv, l):
    x2 = _mm(sv['mix'], p['w_out'], extra=(sv['x'],), epi=_add_to, name=f"out_proj_{l}")
    h2 = _rms_fwd(x2, p['ln2_g'], name=f"ln2_fwd_{l}")
    a = _mm(h2, p['w_mlp_in'], b_chips=_CHIPS, epi=lambda acc: (jnp.square(jnp.maximum(acc, 0.0)),), outs=(_BF,), name=f"mlp_in_{l}")
    x3 = _mm(a, p['w_mlp_out'], extra=(x2,), epi=_add_to, name=f"mlp_out_{l}")
    sv.update(x2=x2, h2=h2, a=a)
    return x3


def _layer_bwd(dx3, dx3b, p, sv, l, send, after):
    cch = p['conv_w'].shape[1]
    g = {}
    du = _mm(dx3b, p['w_mlp_out'], tb=True, extra=(sv['a'],), outs=(_BF,), after=after,
             epi=lambda acc, a: (acc * 2.0 * jnp.sqrt(a.astype(F32)),), name=f"mlp_out_dx_{l}")
    g['w_mlp_out'] = _mm(sv['a'], dx3b, ta=True, outs=(_BF,), name=f"mlp_out_dw_{l}")
    g['w_mlp_in'] = _mm(sv['h2'], du, ta=True, out_chips=_CHIPS, outs=(_BF,), name=f"mlp_in_dw_{l}")
    sent = send(('w_mlp_out', 'w_mlp_in'), g)
    dh2 = _mm(du, p['w_mlp_in'], tb=True, b_chips=_CHIPS, after=sent, name=f"mlp_in_dx_{l}")
    dx2, dx2b, g['ln2_g'] = _rms_bwd(sv['x2'], dh2, p['ln2_g'], dx3, name=f"ln2_bwd_{l}")
    dmix = _mm(dx2b, p['w_out'], tb=True, name=f"out_proj_dx_{l}")
    g['w_out'] = _mm(sv['mix'], dx2b, ta=True, outs=(_BF,), name=f"out_proj_dw_{l}")
    after_out = send(('w_out',), g)

    tile = 2 * _ABLK
    perms = [_perm(d, tile) for d in _DILATIONS[1:]]

    def norm_bwd(attn, dy, lse, gn, p2, p3):
        dattn, dgn = _rms_bwd_tile(attn, dy, gn, 1)
        prod, low = dattn * attn, _low_lanes((attn.shape[0], _LANES))
        lane = lax.broadcasted_iota(jnp.int32, lse.shape, 1)
        ld = jnp.where(lane < _HEADS, lse, 0.0)
        for pr in range(_HEADS // 2):
            for i, part in enumerate(_halves(prod[:, _pair(pr)], low)):
                ld = jnp.where(lane == _HEADS + 2 * pr + i, _rsum(part), ld)
        return dattn, _stride(dattn, p2, _DILATIONS[1]), _stride(dattn, p3, _DILATIONS[2]), ld, dgn
    *dos, ld, gn_sum = _rows(norm_bwd, [sv['attn'], (dmix, _AW, 0), sv['lse']], [p['attn_norm_g'].reshape(1, _AW)] + perms,
                             [(_AW, _BF)] + [(_AW, _BF, d) for d in _DILATIONS[1:]] + [(_LANES, F32)], [_AW], after=after_out,
                             tile=tile, name=f"attn_norm_bwd_{l}")
    g['attn_norm_g'] = gn_sum.sum(axis=0)
    parts = [_attn_bwd(sv['qkv'], do, ld, dil, name=f"attn_bwd_d{dil}_{l}") for do, dil in zip(dos, _DILATIONS)]

    def branch_sum(*t):
        parts_, (p2, p3) = t[:9], t[9:]
        t = [a.astype(F32) for a in parts_[:3]] + [_unstride(a, p2) for a in parts_[3:6]] + [_unstride(a, p3) for a in parts_[6:]]
        return jnp.concatenate([t[i] + t[3 + i] + t[6 + i] for i in range(3)], axis=1)
    branch_ins = list(parts[0]) + [("strided", a, d) for pr, d in zip(parts[1:], _DILATIONS[1:]) for a in pr]
    w_all = 3 * _AW + cch + _AW + _LANES
    dproj = _rows(branch_sum, branch_ins, perms, [(3 * _AW, _BF)], into=(None, w_all, 0), tile=tile, name=f"attn_bwd_sum_{l}")[0]

    xs, z, dt_raw = (sv['act'], _AW, 0), (sv['zdt'], _AW, 0), (sv['zdt'], _LANES, _AW // _LANES)

    def gate_bwd(y2, dy, y, xs, z, dsk, gn):
        dy2, dgn = _rms_bwd_tile(y2, dy, gn, _GROUPS)
        dy1 = dy2 * _silu(z)
        return dy1, dsk * dy1, dy2 * (y + dsk * xs) * _dsilu(z), dy1 * xs, dgn
    dy1, dxs_skip, dz, dsk_sum, gn_sum = _rows(
        gate_bwd, [sv['y2'], (dmix, _AW, 1), sv['y_ssd'], xs, z], [sv['dskip'], p['ssd_norm_g'].reshape(1, _AW)],
        [(_AW, F32), (_AW, F32), (_AW, _BF)], [_AW, _AW], tile=128, name=f"gate_bwd_{l}")
    g['ssd_norm_g'] = gn_sum.sum(axis=0)
    g['d_skip'] = dsk_sum.sum(axis=0).reshape(_HEADS, _HDIM).sum(axis=1)
    dxs, dbc, ddt, da = _ssd_bwd(sv['act'], sv['dt'], sv['acum'], sv['a_row'], sv['sall'], dy1, name=f"ssd_bwd_{l}")

    def dtb(da, ddtx, raw, dt, dz, bias, alog):
        a = -jnp.exp(alog)
        dda = _scan_rows(da, True)
        draw = (dda * a + ddtx) * _sigmoid(raw + bias)
        return jnp.concatenate([dz, draw.astype(dz.dtype)], axis=1), draw, dda * dt * a
    dproj, dbias, dalog = _rows(dtb, [da, ddt, dt_raw, sv['dt'], dz], [_lanes(p['dt_bias']), _lanes(p['a_log'])],
                                [(_AW + _LANES, _BF)], [_LANES, _LANES], into=(dproj, w_all, (3 * _AW + cch) // (_AW + _LANES)),
                                tile=_CHUNK, name=f"dt_bwd_{l}")
    g['dt_bias'], g['a_log'] = dbias.sum(axis=0)[:_HEADS], dalog.sum(axis=0)[:_HEADS]
    def conv_bwd1(u0, dxs, dbc, dxk, before, w, b):
        u1, u2, u3 = _shifted(u0, before, True)
        pre = w[0:1] * u3 + w[1:2] * u2 + w[2:3] * u1 + w[3:4] * u0 + b
        dp = jnp.concatenate([dxs + dxk, dbc], axis=1) * _dsilu(pre)
        return dp, dp * u3, dp * u2, dp * u1, dp * u0, dp
    dpre, *dws = _rows(conv_bwd1, [sv['xbc'], dxs, dbc, dxs_skip], [p['conv_w'], p['conv_b'].reshape(1, cch)], [(cch, F32)],
                       [cch] * 5, halos=[(sv['xbc'], -1)], tile=128, name=f"conv_bwd_pre_{l}")
    g['conv_w'] = jnp.stack([dws[i].sum(axis=0) for i in range(_CONV_K)])
    g['conv_b'] = dws[4].sum(axis=0)

    def conv_bwd2(p0, after_, w):
        p1, p2, p3 = _shifted(p0, after_, False)
        return w[3:4] * p0 + w[2:3] * p1 + w[1:2] * p2 + w[0:1] * p3
    dproj = _rows(conv_bwd2, [dpre], [p['conv_w']], [(cch, _BF)], halos=[(dpre, 1)], into=(dproj, w_all, 3 * _AW // cch),
                  tile=_tile_for(cch), name=f"conv_bwd_in_{l}")[0]
    g_all = _mm(sv['h1'], dproj, ta=True, outs=(_BF,), name=f"in_proj_dw_{l}")
    z0 = 3 * _AW + cch
    g['w_in'] = jnp.concatenate([g_all[:, :3 * _AW], g_all[:, z0:z0 + _AW], g_all[:, 3 * _AW:z0], g_all[:, z0 + _AW:z0 + _AW + _HEADS]], axis=1)
    sent = send(('w_in',), g)
    for n in _BIG:
        del g[n]
    dh1 = _mm(dproj, p['w_in'], tb=True, after=sent, name=f"in_proj_dx_{l}")
    dx, dxb, g['ln1_g'] = _rms_bwd(sv['x'], dh1, p['ln1_g'], dx2, name=f"ln1_bwd_{l}")
    return dx, dxb, g


def _loss_bwd(x, g, tgt):
    w = x.shape[1]
    tile = _tile_for(w)

    def fn(x, tgt, g):
        r = _rstd(x)
        xh = x * r
        e = xh * g - tgt
        gd = e * (g / w)
        dx = r * (gd - xh * jnp.mean(gd * xh, axis=-1, keepdims=True))
        rowloss = 0.5 * jnp.mean(e * e, axis=-1, keepdims=True)
        return dx, dx, (e / w) * xh, jnp.broadcast_to(rowloss, (tile, _LANES))
    dx, dxb, dg, ls = _rows(fn, [x, tgt], [g.reshape(1, w)], [(w, F32), (w, _BF)], [w, _LANES], tile=tile, name="loss_head")
    return dx, dxb, dg.sum(axis=0), ls[:, 0].sum()


def _adamw_math(w, g, m, v):
    m2 = _B1 * m + (1.0 - _B1) * g
    v2 = _B2 * v + (1.0 - _B2) * jnp.square(g)
    m_hat = m2 / (1.0 - _B1 ** _STEP)
    v_hat = v2 / (1.0 - _B2 ** _STEP)
    return -_LR * (m_hat / (jnp.sqrt(v_hat) + _AEPS) + _WD * w), m2, v2


def _adamw(w, g, m, v, *, name):
    width = w.shape[-1]
    flat = [a.reshape(-1, width) for a in (w, g, m, v)]
    tile = _pick(flat[0].shape[0], (_tile_for(width), 32, 8))
    res = _rows(_adamw_math, flat, [], [(width, F32)] * 3, tile=tile, name=name)
    return [r.reshape(w.shape) for r in res]


_HBM = pl.BlockSpec(memory_space=pltpu.HBM)


def _place():
    x, y, c = lax.axis_index("x"), lax.axis_index("y"), lax.axis_index("c")
    other_chips = [(1 - x, y), (x, 1 - y), (1 - x, 1 - y)]
    return x, y, c, other_chips


def _remote(src, dst, sems, i, dev):
    return pltpu.make_async_remote_copy(src_ref=src, dst_ref=dst, send_sem=sems[0].at[i], recv_sem=sems[1].at[i],
                                        device_id=dev, device_id_type=_MESH)


def _exchange8(v, *, reduce, after=None, name):
    r, w = v.shape
    behind = [] if after is None else [after]

    def body(v_ref, *rest):
        all_ref, rest = rest[len(behind)], rest[len(behind) + 1:]
        sems = rest[-2:]
        x, y, c, _ = _place()
        me = 4 * x + 2 * y + c
        all_ref[me] = v_ref[...]
        flips = [((d >> 2) & 1, (d >> 1) & 1, d & 1) for d in range(1, 8)]
        sends = [_remote(v_ref, all_ref.at[me], sems, i, (x ^ fx, y ^ fy, c ^ fc)) for i, (fx, fy, fc) in enumerate(flips)]
        for cp in sends:
            cp.start()
        for i, (fx, fy, fc) in enumerate(flips):
            _remote(v_ref, all_ref.at[me ^ (4 * fx + 2 * fy + fc)], sems, i, (x ^ fx, y ^ fy, c ^ fc)).wait_recv()
        for cp in sends:
            cp.wait_send()
        if reduce:
            acc = all_ref[0]
            for s in range(1, 8):
                acc = acc + all_ref[s]
            rest[0][...] = acc

    vm = pl.BlockSpec(memory_space=pltpu.VMEM)
    out_shape = [_SDS((8, r, w), v.dtype)] + ([_SDS((r, w), v.dtype)] if reduce else [])
    res = pl.pallas_call(body, in_specs=[vm] + [_ANY] * len(behind), out_specs=[vm] * len(out_shape), out_shape=out_shape, name=name,
                         scratch_shapes=[pltpu.SemaphoreType.DMA((7,)), pltpu.SemaphoreType.DMA((7,))],
                         compiler_params=pltpu.CompilerParams(vmem_limit_bytes=int(32 << 20)))(v, *behind)
    return res[1] if reduce else res[0]


_SEM = pl.BlockSpec(memory_space=pltpu.SEMAPHORE)
_ANY = pl.BlockSpec(memory_space=pl.ANY)
_EFFECT = pltpu.SideEffectType.DATAFLOW_SIDE_EFFECTING


def _send_start(name, srcs, land_shapes, plan, n_sends, after):
    ns, nl = len(srcs), len(land_shapes)
    zones = [_hbm(lax.empty(s.shape, s.dtype)) if isinstance(s, _SDS) else s for s in land_shapes]

    def body(*refs):
        ins, lands, sems = refs[:ns], refs[ns:ns + nl], refs[ns + nl + 1:ns + nl + 3]
        x, y, c, chips = _place()
        for i, (s, d, dev) in enumerate(plan(x, y, c, chips, ins, lands)[0]):
            _remote(s, d, sems, i, dev).start()
        refs[-1][...] = jnp.zeros_like(refs[-1])

    sem = pltpu.SemaphoreType.DMA((n_sends,))
    res = pl.pallas_call(
        body, name=name, in_specs=[_HBM] * (ns + nl) + [_ANY],
        out_shape=(sem, sem, *[pltpu.HBM(s.shape, s.dtype) for s in land_shapes], _SDS((8, _LANES), F32)),
        out_specs=(_SEM, _SEM, *[_HBM] * nl, pl.BlockSpec(memory_space=pltpu.VMEM)),
        input_output_aliases={ns + i: 2 + i for i in range(nl)},
        compiler_params=pltpu.CompilerParams(has_side_effects=_EFFECT))(
            *[_hbm(s) for s in srcs], *zones, after)
    return dict(sems=res[:2], srcs=srcs, lands=res[2:2 + nl], plan=plan), res[-1]


def _send_wait(name, h, after):
    ns, nl = len(h['srcs']), len(h['lands'])

    def body(*refs):
        ins, lands, sems = refs[:ns], refs[ns:ns + nl], refs[ns + nl:ns + nl + 2]
        x, y, c, chips = _place()
        sends, landings = h['plan'](x, y, c, chips, ins, lands)
        for i, (s, d, dev) in enumerate(sends):
            _remote(s, d, sems, i, dev).wait_send()
        for i, d in enumerate(landings):
            _remote(d, d, sems, i, sends[i][2]).wait_recv()

    return pl.pallas_call(
        body, name=name, in_specs=[_HBM] * (ns + nl) + [_SEM, _SEM, _ANY],
        out_shape=tuple(pltpu.HBM(a.shape, a.dtype) for a in h['lands']), out_specs=tuple([_HBM] * nl),
        input_output_aliases={ns + i: i for i in range(nl)},
        compiler_params=pltpu.CompilerParams(has_side_effects=_EFFECT))(
            *[_hbm(s) for s in h['srcs']], *h['lands'], *h['sems'], after)


def _gather_plan(items):
    def plan(x, y, c, chips, ins, lands):
        k = 2 * x + y
        sends = [(ins[si].at[l], lands[t].at[k], (px, py, c)) for t, (si, l) in enumerate(items) for px, py in chips]
        return sends, [lands[t].at[2 * px + py] for t in range(len(items)) for px, py in chips]
    return plan


_FLIPS = [((d >> 2) & 1, (d >> 1) & 1, d & 1) for d in range(1, 8)]


def _reduce_plan(halves):
    def plan(x, y, c, chips, ins, lands):
        sends, landings = [], []
        for t, hf in enumerate(halves):
            for i, (fx, fy, fc) in enumerate(_FLIPS):
                px, py, pc = x ^ fx, y ^ fy, c ^ fc
                sends.append((ins[t].at[2 * px + py, pl.ds(pc * hf, hf)], lands[t].at[i], (px, py, pc)))
                landings.append(lands[t].at[i])
        return sends, landings
    return plan


def _swap(name, srcs, out_shapes, plan, n_sends):
    n = len(srcs)

    def body(*refs):
        ins, outs, sems = refs[:n], refs[n:n + len(out_shapes)], refs[-2:]
        x, y, c, chips = _place()
        sends, landings = plan(x, y, c, chips, ins, outs)
        out = [_remote(s, d, sems, i, dev) for i, (s, d, dev) in enumerate(sends)]
        for cp in out:
            cp.start()
        for i, d in enumerate(landings):
            _remote(d, d, sems, i, sends[i][2]).wait_recv()
        for cp in out:
            cp.wait_send()

    return pl.pallas_call(
        body, in_specs=[_HBM] * n, out_specs=[_HBM] * len(out_shapes), out_shape=out_shapes, name=name,
        scratch_shapes=[pltpu.SemaphoreType.DMA((n_sends,)), pltpu.SemaphoreType.DMA((n_sends,))])(*srcs)


def _sum_owned(grads, landed, c, k, names):
    def sum8(*parts):
        acc = parts[0].astype(F32)
        for p in parts[1:]:
            acc = acc + p.astype(F32)
        return acc
    outs = []
    for g, got, name in zip(grads, landed, names):
        hf, b = got.shape[1:]
        own = lax.dynamic_slice_in_dim(lax.dynamic_index_in_dim(g, k, axis=0, keepdims=False), c * hf, hf, axis=0)
        outs.append(_rows(sum8, [own] + [("slot", got, i) for i in range(len(_FLIPS))], [], [(b, F32)],
                          tile=_pick(hf, (_tile_for(b), 32)), name=f"grad_sum_{name}")[0])
    return outs


def _share_halves(mine, *, name):
    n = len(mine)

    def plan(x, y, c_, chips, ins, outs):
        return [(ins[t], outs[t], (x, y, 1 - c_)) for t in range(n)], [outs[t] for t in range(n)]
    return _swap(name, mine, [_SDS(h.shape, F32) for h in mine], plan, n)


def _adamw_owned(w, mine, theirs, m, v, c, *, name):
    depth, a, b = w.shape
    half = a // 2
    tile = _pick(half, (_tile_for(b), 32, 8))
    nh = half // tile

    def blocks_of(l):
        return lambda i: (jnp.clip(i - 2 * nh * l, 0, 2 * nh - 1) % nh, 0)

    def fn(w, m, v, *rest):
        halves, cflag = rest[:-1], rest[-1]
        step = pl.program_id(0)
        is_mine = cflag[0:1, 0:1] == ((step // nh) % 2).astype(F32)
        g = jnp.where(is_mine, halves[0], halves[1])
        for l in range(1, depth):
            g = jnp.where(step >= 2 * nh * l, jnp.where(is_mine, halves[2 * l], halves[2 * l + 1]), g)
        return (g,) + _adamw_math(w, g, m, v)
    ins = [a_.reshape(depth * a, b) for a_ in (w, m, v)]
    ins += [(h, b, blocks_of(l)) for l in range(depth) for h in (mine[l], theirs[l])]
    res = _rows(fn, ins, [jnp.full((1, _LANES), c, F32)], [(b, F32)] * 4, tile=tile, name=name)
    return [r.reshape(w.shape) for r in res]


_BIG = ("w_in", "w_out", "w_mlp_in", "w_mlp_out")
_SMALL = ("ln1_g", "conv_b", "dt_bias", "a_log", "d_skip", "attn_norm_g", "ssd_norm_g", "ln2_g", "final_norm_g")
_ORDER = ("ln1_g", "w_in", "conv_w", "conv_b", "dt_bias", "a_log", "d_skip", "attn_norm_g", "ssd_norm_g", "w_out", "ln2_g",
          "w_mlp_in", "w_mlp_out", "final_norm_g")


def _pack(parts, rows):
    flat = jnp.concatenate([p.reshape(-1) for p in parts])
    return jnp.pad(flat, (0, rows * _LANES - flat.shape[0])).reshape(rows, _LANES)


def _unpack(buf, like):
    flat, out, o = buf.reshape(-1), [], 0
    for p in like:
        out.append(flat[o:o + p.size].reshape(p.shape))
        o += p.size
    return out


def kernel(x, ln1_g, w_in, conv_w, conv_b, dt_bias, a_log, d_skip, attn_norm_g, ssd_norm_g, w_out, ln2_g, w_mlp_in, w_mlp_out, final_norm_g, loss_target, m_ln1_g, m_w_in, m_conv_w, m_conv_b, m_dt_bias, m_a_log, m_d_skip, m_attn_norm_g, m_ssd_norm_g, m_w_out, m_ln2_g, m_w_mlp_in, m_w_mlp_out, m_final_norm_g, v_ln1_g, v_w_in, v_conv_w, v_conv_b, v_dt_bias, v_a_log, v_d_skip, v_attn_norm_g, v_ssd_norm_g, v_w_out, v_ln2_g, v_w_mlp_in, v_w_mlp_out, v_final_norm_g):
    w = dict(ln1_g=ln1_g, w_in=w_in, conv_w=conv_w, conv_b=conv_b, dt_bias=dt_bias, a_log=a_log, d_skip=d_skip,
             attn_norm_g=attn_norm_g, ssd_norm_g=ssd_norm_g, w_out=w_out, ln2_g=ln2_g, w_mlp_in=w_mlp_in, w_mlp_out=w_mlp_out,
             final_norm_g=final_norm_g)
    m = dict(ln1_g=m_ln1_g, w_in=m_w_in, conv_w=m_conv_w, conv_b=m_conv_b, dt_bias=m_dt_bias, a_log=m_a_log, d_skip=m_d_skip,
             attn_norm_g=m_attn_norm_g, ssd_norm_g=m_ssd_norm_g, w_out=m_w_out, ln2_g=m_ln2_g, w_mlp_in=m_w_mlp_in,
             w_mlp_out=m_w_mlp_out, final_norm_g=m_final_norm_g)
    v = dict(ln1_g=v_ln1_g, w_in=v_w_in, conv_w=v_conv_w, conv_b=v_conv_b, dt_bias=v_dt_bias, a_log=v_a_log, d_skip=v_d_skip,
             attn_norm_g=v_attn_norm_g, ssd_norm_g=v_ssd_norm_g, w_out=v_w_out, ln2_g=v_ln2_g, w_mlp_in=v_w_mlp_in,
             w_mlp_out=v_w_mlp_out, final_norm_g=v_final_norm_g)
    depth, d_model = ln1_g.shape
    n_chips = _CHIPS
    c = lax.axis_index("c")
    chip = 2 * lax.axis_index("x") + lax.axis_index("y")
    in_proj = w_in.shape[2] * n_chips
    cch = conv_w.shape[2] * n_chips
    zdt_pad = _LANES - _HEADS

    cw = _exchange8(conv_w.reshape(depth * _CONV_K, -1), reduce=False, name="gather_conv_w")[0::2]
    conv_full = cw.reshape(n_chips, depth, _CONV_K, -1).transpose(1, 2, 0, 3).reshape(depth, _CONV_K, cch)
    own = [w[n].astype(_BF) for n in _BIG]
    is_own = (jnp.arange(n_chips) == chip).reshape(n_chips, 1, 1)

    def start_gather(tag, items, after):
        lands = [_SDS((n_chips, *own[i].shape[1:]), _BF) for i, _ in items]
        return _send_start(f"gather_start_{tag}", own, lands, _gather_plan(items), 3 * len(items), after)

    def finish_gather(tag, handle, items, after):
        landed = _send_wait(f"gather_wait_{tag}", handle, after)
        return {_BIG[i]: jnp.where(is_own, own[i][l][None], g) for (i, l), g in zip(items, landed)}

    def layer_weights(l, blocks):
        p = {}
        if 'w_in' in blocks:
            full_in = blocks['w_in'].transpose(1, 0, 2).reshape(d_model, in_proj)
            p['w_in'] = jnp.concatenate([full_in[:, :3 * _AW], full_in[:, 4 * _AW:4 * _AW + cch], full_in[:, 3 * _AW:4 * _AW],
                                         full_in[:, 4 * _AW + cch:], jnp.zeros((d_model, zdt_pad), _BF)], axis=1)
        if 'w_out' in blocks:
            p['w_out'] = blocks['w_out'].reshape(-1, d_model)
            p['w_mlp_in'] = blocks['w_mlp_in']
            p['w_mlp_out'] = blocks['w_mlp_out'].reshape(-1, d_model)
        return p

    groups = dict(a=[(0, 0)], b=[(1, 0), (2, 0), (3, 0)], c=[(0, 1)], d=[(1, 1), (2, 1), (3, 1)])
    handles, token = {}, conv_full

    half_in = own[0].shape[1] // 2

    def rows_of(ref, who):
        return ref.at[pl.ds(who * half_in, half_in)]

    def plan_a(x_, y_, c_, chips, ins, lands):
        k = 2 * x_ + y_
        sends = [(rows_of(ins[0].at[0], c_), rows_of(lands[0].at[k], c_), (px, py, c_)) for px, py in chips]
        return sends, [rows_of(lands[0].at[2 * px + py], c_) for px, py in chips]

    def plan_pass(x_, y_, c_, chips, ins, lands):
        sends = [(rows_of(lands[0].at[2 * px + py], c_),) * 2 + ((x_, y_, 1 - c_),) for px, py in chips]
        return sends, [rows_of(lands[0].at[2 * px + py], 1 - c_) for px, py in chips]
    handles["a"], token = _send_start("gather_start_a", own[:1], [_SDS((n_chips, *own[0].shape[1:]), _BF)], plan_a, 3, token)
    for tag, items in list(groups.items())[1:]:
        handles[tag], token = start_gather(tag, items, token)
    landed = _send_wait("gather_land_a", handles["a"], token)
    handles["a"], token = _send_start("gather_pass_a", [], landed, plan_pass, 3, landed[0])
    layers = [{n: w[n][l] for n in _SMALL[:-1]} for l in range(depth)]
    for l in range(depth):
        layers[l]['conv_w'] = conv_full[l]

    layers[0].update(layer_weights(0, finish_gather("a", handles["a"], groups["a"], token)))
    mix, sv0 = _layer_fwd(x[0], layers[0], 0)
    layers[0].update(layer_weights(0, finish_gather("b", handles["b"], groups["b"], mix)))
    h = _layer_fwd_mlp(layers[0], sv0, 0)
    layers[1].update(layer_weights(1, finish_gather("c", handles["c"], groups["c"], h)))
    mix, sv1 = _layer_fwd(h, layers[1], 1)
    layers[1].update(layer_weights(1, finish_gather("d", handles["d"], groups["d"], mix)))
    h = _layer_fwd_mlp(layers[1], sv1, 1)
    saved = [sv0, sv1]

    def by_chip(g, name):
        if name == "w_mlp_in":
            return g
        if name == "w_in":
            return g.reshape(d_model, n_chips, -1).transpose(1, 0, 2)
        return g.reshape(n_chips, -1, d_model)

    pending = []

    def sender(l):
        def send(names, g):
            srcs = [by_chip(g[n], n) for n in names]
            halves = [s.shape[1] // 2 for s in srcs]
            lands = [_SDS((len(_FLIPS), hf, s.shape[2]), _BF) for s, hf in zip(srcs, halves)]
            handle, tok = _send_start(f"grad_start_{names[-1]}_{l}", srcs, lands, _reduce_plan(halves), len(_FLIPS) * len(srcs), srcs[0])
            pending.append((l, names, srcs, handle))
            return tok
        return send

    dx, dxb, g_final, loss_part = _loss_bwd(h, final_norm_g, loss_target[0])
    grads, after = [None] * depth, None
    for l in reversed(range(depth)):
        dx, dxb, grads[l] = _layer_bwd(dx, dxb, layers[l], saved[l], l, sender(l), after)
        after = dx
    landed_of = {}
    sent_in = {(n, l): (gi, j) for gi, (l, names, _, _) in enumerate(pending) for j, n in enumerate(names)}

    def landed_for(gi, after):
        if gi not in landed_of:
            l, names, _, handle = pending[gi]
            landed_of[gi] = _send_wait(f"grad_wait_{names[-1]}_{l}", handle, after)
        return landed_of[gi]

    red, delta, new_m, new_v = {}, {}, {}, {}
    after = dx
    for n in ("w_mlp_out", "w_mlp_in", "w_out", "w_in"):
        mine = []
        for l in range(depth):
            gi, j = sent_in[(n, l)]
            got = landed_for(gi, after)[j]
            mine.append(_sum_owned([pending[gi][2][j]], [got], c, chip, [f"{n}_{l}"])[0])
        theirs = _share_halves(mine, name=f"grad_share_{n}")
        red[n], delta[n], new_m[n], new_v[n] = _adamw_owned(w[n], mine, theirs, m[n], v[n], c, name=f"adamw_{n}")
        after = delta[n]

    small = {n: jnp.stack([grads[l][n] for l in range(depth)]) for n in _SMALL[:-1] + ("conv_w",)}
    small["final_norm_g"] = g_final
    parts = [loss_part.reshape(1)] + [small[n] for n in _SMALL + ("conv_w",)]
    rows = -(-sum(p.size for p in parts) // 1024) * 8
    tot = _unpack(_exchange8(_pack(parts, rows), reduce=True, after=red[_BIG[0]], name="allreduce_small"), parts)
    loss = tot[0][0]
    red.update(zip(_SMALL + ("conv_w",), tot[1:]))
    red["conv_w"] = lax.dynamic_index_in_dim(red["conv_w"].reshape(depth, _CONV_K, n_chips, -1), chip, axis=2, keepdims=False)

    names = _SMALL + ("conv_w",)
    like = [w[n] for n in names]
    srows = -(-sum(p.size for p in like) // 1024) * 8
    res = _adamw(*[_pack([d[n] for n in names], srows) for d in (w, red, m, v)], name="adamw_small")
    for dst, buf in zip((delta, new_m, new_v), res):
        dst.update(zip(names, _unpack(buf, like)))
    return (loss, dx[None], *[red[n] for n in _ORDER], *[delta[n] for n in _ORDER], *[new_m[n] for n in _ORDER],
            *[new_v[n] for n in _ORDER])
```

```python
import numpy as np
import jax
import jax.numpy as jnp
from jax import lax
from jax.experimental import pallas as pl
from jax.experimental.pallas import tpu as pltpu

F32 = jnp.float32
_BF = jnp.bfloat16
_NEG = -1e30
_EPS = 1e-5
_HEADS = 16
_HDIM = 64
_AW = _HEADS * _HDIM
_ABLK = 128
_DILATIONS = (1, 4, 16)
_CHUNK = 128
_NSTATE = 128
_GROUPS = 2
_HPG = _HEADS // _GROUPS
_CONV_K = 4
_LANES = 128
_CHIPS = 4
_LR, _B1, _B2, _AEPS, _WD, _STEP = 0.001, 0.9, 0.999, 1e-08, 0.01, 10
_VMEM_CAP = 56 * 1024 * 1024
_MESH = pl.DeviceIdType.MESH
_SDS = jax.ShapeDtypeStruct
_NT = (((1,), (1,)), ((), ()))
_TN = (((0,), (0,)), ((), ()))


def _params(sem, est_bytes):
    lim = int(min(max(2 * est_bytes + (4 << 20), 16 << 20), _VMEM_CAP))
    return pltpu.CompilerParams(dimension_semantics=sem, vmem_limit_bytes=lim)


def _nbytes(shape, dtype):
    return int(np.prod(shape)) * jnp.dtype(dtype).itemsize


def _hbm(a):
    return pltpu.with_memory_space_constraint(a, pltpu.HBM)


def _dot(a, b, dims=(((1,), (0,)), ((), ()))):
    return lax.dot_general(a.astype(_BF), b.astype(_BF), dims, preferred_element_type=F32)


_HALO = 8


def _rows(fn, ins, consts, outs, sums=(), *, halos=(), into=None, after=None, tile, name):
    rows = (ins[0][0] if isinstance(ins[0], tuple) else ins[0]).shape[0]
    n_steps = rows // tile

    def norm_in(a):
        if not isinstance(a, tuple):
            return a, tile, a.shape[1], lambda i: (i, 0)
        if isinstance(a[0], str) and a[0] == "slot":
            return a[1], (None, tile, a[1].shape[2]), a[1].shape[2], lambda i, s=a[2]: (s, i, 0)
        if isinstance(a[0], str):
            return a[1], tile // a[2], a[1].shape[1], lambda i: (i, 0)
        return a[0], tile, a[1], a[2] if callable(a[2]) else (lambda i, j=a[2]: (i, j))
    ins = [norm_in(a) for a in ins]
    outs = [(w, dt, d[0] if d else 1) for w, dt, *d in outs]
    n_in, n_h, n_c, n_o, n_s = len(ins), len(halos), len(consts), len(outs), len(sums)
    n_x = int(into is not None and into[0] is not None) + int(after is not None)

    def body(*refs):
        step = pl.program_id(0)
        vals = [r[...] for r in refs[:n_in]]
        for r, (_, side) in zip(refs[n_in:n_in + n_h], halos):
            vals.append(jnp.where(step == (0 if side < 0 else n_steps - 1), 0.0, r[...]))
        vals += [r[...] for r in refs[n_in + n_h:n_in + n_h + n_c]]
        refs = refs[:n_in] + refs[n_in + n_h:]
        res = fn(*vals)
        res = res if isinstance(res, tuple) else (res,)
        orefs = refs[n_in + n_c + n_x:n_in + n_c + n_x + n_o]
        srefs = refs[n_in + n_c + n_x + n_o:]
        for r, v in zip(orefs, res[:n_o]):
            r[...] = v.astype(r.dtype)
        if n_s:
            @pl.when(pl.program_id(0) == 0)
            def _():
                for r in srefs:
                    r[...] = jnp.zeros_like(r)
            for r, v in zip(srefs, res[n_o:]):
                r[...] += v.reshape(tile // 8, 8, v.shape[-1]).sum(axis=0)

    per = tile // _HALO
    in_specs = [pl.BlockSpec(r if isinstance(r, tuple) else (r, w), idx) for _, r, w, idx in ins]
    in_specs += [pl.BlockSpec((_HALO, a.shape[1]), (lambda i: (jnp.maximum(i * per - 1, 0), 0)) if side < 0
                              else (lambda i: (jnp.minimum((i + 1) * per, rows // _HALO - 1), 0))) for a, side in halos]
    in_specs += [pl.BlockSpec(c.shape, lambda i, nd=c.ndim: (0,) * nd) for c in consts]
    out_shape = [_SDS((rows // d, d * w), dt) for w, dt, d in outs] + [_SDS((8, w), F32) for w in sums]
    out_specs = [pl.BlockSpec((tile // d, d * w), lambda i: (i, 0)) for w, _, d in outs]
    out_specs += [pl.BlockSpec((8, w), lambda i: (0, 0)) for w in sums]
    est = (sum(_nbytes((tile if isinstance(r, tuple) else r, w), a.dtype) for a, r, w, _ in ins)
           + sum(_nbytes((tile, w), dt) for w, dt, _ in outs))
    shared, aliases = [], {}
    if into is not None:
        buf, total, j = into
        out_shape[0] = _SDS((rows, total), outs[0][1])
        out_specs[0] = pl.BlockSpec((tile, outs[0][0]), lambda i: (i, j))
        if buf is not None:
            shared, aliases = [buf], {n_in + n_h + n_c: 0}
    if after is not None:
        shared.append(after)
    in_specs += [pl.BlockSpec(memory_space=pl.ANY)] * len(shared)
    return pl.pallas_call(body, grid=(n_steps,), in_specs=in_specs, out_specs=out_specs, out_shape=out_shape, name=name,
                          input_output_aliases=aliases, compiler_params=_params(("arbitrary",), 3 * est))(
                              *[_hbm(a[0]) for a in ins], *[_hbm(a) for a, _ in halos], *consts, *shared)


def _perm(d, tile):
    p = np.zeros((tile, tile), np.float32)
    t = np.arange(tile)
    p[t, (t % d) * (tile // d) + t // d] = 1.0
    return jnp.asarray(p, _BF)


def _unstride(s, p):
    d = p.shape[0] // s.shape[0]
    w = s.shape[1] // d
    return _dot(p, jnp.concatenate([s[:, r * w:(r + 1) * w] for r in range(d)], axis=0))


def _stride(x, p, d):
    z = _dot(p, x, _TN)
    n = x.shape[0] // d
    return jnp.concatenate([z[r * n:(r + 1) * n] for r in range(d)], axis=1)


def _shifted(u, halo, back):
    n = u.shape[0] + _HALO
    if back:
        ext = jnp.concatenate([halo, u], axis=0)
        return [pltpu.roll(ext, j, 0)[_HALO:] for j in (1, 2, 3)]
    ext = jnp.concatenate([u, halo], axis=0)
    return [pltpu.roll(ext, n - j, 0)[:u.shape[0]] for j in (1, 2, 3)]


def _tile_for(width):
    return max(c for c in (256, 128, 64, 32) if c * width <= (1 << 18) or c == 32)


def _rstd(x):
    return lax.rsqrt(jnp.mean(x * x, axis=-1, keepdims=True) + _EPS)


def _split(x, groups):
    w = x.shape[-1] // groups
    return [x[:, g * w:(g + 1) * w] for g in range(groups)]


def _cat(parts):
    return parts[0] if len(parts) == 1 else jnp.concatenate(parts, axis=-1)


def _rms_bwd_tile(x, dy, g, groups):
    dxs, dgs = [], []
    for xs, ds, gs in zip(_split(x, groups), _split(dy.astype(F32), groups), _split(g, groups)):
        r = _rstd(xs)
        xh = xs * r
        gd = ds * gs
        dxs.append(r * (gd - xh * jnp.mean(gd * xh, axis=-1, keepdims=True)))
        dgs.append(ds * xh)
    return _cat(dxs), _cat(dgs)


def _rms_fwd(x, g, *, groups=1, into=None, name):
    def fn(x, g):
        return _cat([xs * _rstd(xs) * gs for xs, gs in zip(_split(x, groups), _split(g, groups))])
    w = x.shape[1]
    return _rows(fn, [x], [g.reshape(1, w)], [(w, _BF)], into=into, tile=_tile_for(w), name=name)[0]


def _rms_bwd(x, dy, g, res, *, name):
    def fn(x, dy, res, g):
        dx, dg = _rms_bwd_tile(x, dy, g, 1)
        return dx + res, dx + res, dg
    w = x.shape[1]
    dx, dxb, dg = _rows(fn, [x, dy, res], [g.reshape(1, w)], [(w, F32), (w, _BF)], [w], tile=_tile_for(w), name=name)
    return dx, dxb, dg.sum(axis=0)


def _pick(n, cands):
    for c in cands:
        if n % c == 0:
            return c
    raise ValueError(f"no block size for {n}")


_MM_BLOCKS = (1024, 1152, 512, 384)


def _mm(a, b, *, ta=False, tb=False, extra=(), epi=None, outs=(F32,), after=None, b_chips=0, out_chips=0, b_cols=None, name):
    m, k = (a.shape[1], a.shape[0]) if ta else a.shape
    b_shape = (b.shape[1], b.shape[2] * b_chips) if b_chips else b.shape
    if b_cols is not None:
        b_shape = (b.shape[0], b_cols[1])
    n = b_shape[0] if tb else b_shape[1]
    assert k == (b_shape[1] if tb else b_shape[0])
    n_cap = n // max(out_chips, 1 if tb else b_chips, 1)
    k_cap = k // (b_chips if (b_chips and tb) else 1)
    bm, bn = _pick(m, _MM_BLOCKS), _pick(n_cap, _MM_BLOCKS)
    bk = _pick(k_cap, (2048, 1920) + _MM_BLOCKS)
    nk = k // bk
    n_e, n_o = len(extra), len(outs)
    behind = [] if after is None else [after]
    dims = (((0 if ta else 1,), (1 if tb else 0,)), ((), ()))

    def body(a_ref, b_ref, *rest):
        ex, orefs, acc = rest[:n_e], rest[n_e + len(behind):n_e + len(behind) + n_o], rest[-1]
        kk = pl.program_id(2)

        @pl.when(kk == 0)
        def _():
            acc[...] = jnp.zeros_like(acc)

        acc[...] += _dot(a_ref[...], b_ref[...], dims)

        @pl.when(kk == nk - 1)
        def _():
            r = acc[...]
            res = epi(r, *[e[...] for e in ex]) if epi is not None else (r,)
            for o, v in zip(orefs, res):
                o[...] = v.astype(o.dtype)

    a_spec = pl.BlockSpec((bk, bm), lambda i, j, kk: (kk, i)) if ta else pl.BlockSpec((bm, bk), lambda i, j, kk: (i, kk))
    if b_chips and tb:
        per = k_cap // bk
        b_spec = pl.BlockSpec((None, bn, bk), lambda i, j, kk: (kk // per, j, kk % per))
    elif b_chips:
        per = n_cap // bn
        b_spec = pl.BlockSpec((None, bk, bn), lambda i, j, kk: (j // per, kk, j % per))
    else:
        first = 0 if b_cols is None else b_cols[0] // bn
        assert b_cols is None or (not tb and b_cols[0] % bn == 0)
        b_spec = pl.BlockSpec((bn, bk), lambda i, j, kk: (j, kk)) if tb else pl.BlockSpec((bk, bn), lambda i, j, kk: (kk, first + j))
    t_spec = pl.BlockSpec((bm, bn), lambda i, j, kk: (i, j))
    o_spec, o_shape = t_spec, (m, n)
    if out_chips:
        per_o = n_cap // bn
        o_spec, o_shape = pl.BlockSpec((None, bm, bn), lambda i, j, kk: (j // per_o, i, j % per_o)), (out_chips, m, n_cap)
    est = (_nbytes((bm, bk), a.dtype) + _nbytes((bk, bn), b.dtype) + sum(_nbytes((bm, bn), e.dtype) for e in extra)
           + sum(_nbytes((bm, bn), o) for o in outs)) * 2 + 2 * _nbytes((bm, bn), F32)
    res = pl.pallas_call(
        body, grid=(m // bm, n // bn, nk), in_specs=[a_spec, b_spec] + [t_spec] * n_e + [pl.BlockSpec(memory_space=pl.ANY)] * len(behind),
        out_specs=[o_spec] * n_o, out_shape=[_SDS(o_shape, o) for o in outs], scratch_shapes=[pltpu.VMEM((bm, bn), F32)], name=name,
        compiler_params=_params(("parallel", "parallel", "arbitrary"), est))(_hbm(a), _hbm(b), *[_hbm(e) for e in extra], *behind)
    return res[0] if n_o == 1 else res


def _add_to(acc, r):
    return (acc + r,)


def _alibi_bias(dilation):
    slopes = 2.0 ** (-8.0 * (np.arange(_HEADS) + 1) / _HEADS)
    i = np.arange(_ABLK)[:, None]
    j = np.arange(_ABLK)[None, :]
    cur = np.where(i - j >= 0, -slopes[:, None, None] * ((i - j) * dilation), _NEG)
    prev = np.where(j >= i, -slopes[:, None, None] * ((i - j + _ABLK) * dilation), _NEG)
    both = np.stack([np.concatenate([np.full_like(prev, _NEG), cur], axis=2), np.concatenate([prev, cur], axis=2)])
    return jnp.asarray(both.reshape(2, _HEADS // 2, 2 * _ABLK, 2 * _ABLK), F32)


def _bias_spec():
    return pl.BlockSpec((None, _HEADS // 2, 2 * _ABLK, 2 * _ABLK), lambda r, j: (jnp.minimum(j, 1), 0, 0, 0))


def _strided(a, d):
    return a.reshape(a.shape[0] // d, d * a.shape[1])


def _pair(pr):
    return slice(pr * _LANES, (pr + 1) * _LANES)


def _low_lanes(shape):
    return lax.broadcasted_iota(jnp.int32, shape, 1) < _HDIM


def _halves(v, low):
    z = jnp.zeros_like(v)
    return jnp.where(low, v, z), jnp.where(low, z, v)


def _lane_spec(nb):
    return pl.BlockSpec((_ABLK, _LANES), lambda r, j: (jnp.minimum(j, nb - 1), r))


def _expand_heads(v):
    low = _low_lanes((v.shape[0], _LANES))
    return jnp.concatenate([jnp.where(low, v[:, 2 * pr:2 * pr + 1], v[:, 2 * pr + 1:2 * pr + 2]) for pr in range(_HEADS // 2)], axis=1)


def _attn_specs(nb, n_parts):
    def cur(p):
        return pl.BlockSpec((_ABLK, _AW), lambda r, j: (jnp.minimum(j, nb - 1), r * n_parts + p))

    def prev(p):
        return pl.BlockSpec((_ABLK, _AW), lambda r, j: (jnp.clip(j - 1, 0, nb - 1), r * n_parts + p))
    return cur, prev


def _attn_fwd(qkv, dilation, *, name):
    t = qkv.shape[0]
    nb = t // dilation // _ABLK
    bias = _alibi_bias(dilation)
    scale = _HDIM ** -0.5

    def body(q_ref, kc_ref, kp_ref, vc_ref, vp_ref, b_ref, o_ref, l_ref):
        low = _low_lanes((_ABLK, _LANES))
        l_ref[...] = jnp.zeros_like(l_ref)
        for pr in range(_HEADS // 2):
            sl = _pair(pr)
            k2 = jnp.concatenate([kp_ref[:, sl], kc_ref[:, sl]], axis=0)
            v2 = jnp.concatenate([vp_ref[:, sl], vc_ref[:, sl]], axis=0)
            q2 = jnp.concatenate(_halves(q_ref[:, sl] * scale, low), axis=0)
            s = _dot(q2, k2, _NT) + b_ref[pr]
            m = jnp.max(s, axis=-1, keepdims=True)
            p = jnp.exp(s - m)
            den = jnp.sum(p, axis=-1, keepdims=True)
            o = _dot(p, v2) / den
            lse = m + jnp.log(den)
            l_ref[:, 2 * pr:2 * pr + 1] = lse[:_ABLK]
            l_ref[:, 2 * pr + 1:2 * pr + 2] = lse[_ABLK:]
            o_ref[:, sl] = jnp.where(low, o[:_ABLK], o[_ABLK:]).astype(o_ref.dtype)

    cur, prev = _attn_specs(nb, 3)
    cur1, _ = _attn_specs(nb, 1)
    bspec = _bias_spec()
    sv = _hbm(_strided(qkv, dilation))
    o, l = pl.pallas_call(
        body, grid=(dilation, nb), in_specs=[cur(0), cur(1), prev(1), cur(2), prev(2), bspec],
        out_specs=[cur1(0), _lane_spec(nb)],
        out_shape=[_SDS((t // dilation, dilation * _AW), _BF), _SDS((t // dilation, dilation * _LANES), F32)], name=name,
        compiler_params=_params(("parallel", "arbitrary"), 16 << 20))(sv, sv, sv, sv, sv, bias)
    return o, l.reshape(t, _LANES)


def _attn_bwd(qkv, do, ld, dilation, *, name):
    t = qkv.shape[0]
    nb = t // dilation // _ABLK
    bias = _alibi_bias(dilation)
    scale = _HDIM ** -0.5

    def body(q_ref, kc_ref, kp_ref, vc_ref, vp_ref, do_ref, ld_ref, b_ref, dq_ref, dk_ref, dv_ref, ck, cv):
        n = pl.program_id(1)

        @pl.when(n == 0)
        def _():
            ck[...] = jnp.zeros_like(ck)
            cv[...] = jnp.zeros_like(cv)

        @pl.when(n < nb)
        def _():
            low = _low_lanes((_ABLK, _LANES))
            for pr in range(_HEADS // 2):
                sl = _pair(pr)
                k2 = jnp.concatenate([kp_ref[:, sl], kc_ref[:, sl]], axis=0)
                v2 = jnp.concatenate([vp_ref[:, sl], vc_ref[:, sl]], axis=0)
                q2 = jnp.concatenate(_halves(q_ref[:, sl] * scale, low), axis=0)
                do2 = jnp.concatenate(_halves(do_ref[:, sl], low), axis=0)
                lrow = jnp.concatenate([ld_ref[:, 2 * pr:2 * pr + 1], ld_ref[:, 2 * pr + 1:2 * pr + 2]], axis=0)
                dsum = jnp.concatenate([ld_ref[:, _HEADS + 2 * pr:_HEADS + 2 * pr + 1],
                                        ld_ref[:, _HEADS + 2 * pr + 1:_HEADS + 2 * pr + 2]], axis=0)
                p = jnp.exp(_dot(q2, k2, _NT) + b_ref[pr] - lrow)
                ds = (p * (_dot(do2, v2, _NT) - dsum)).astype(_BF)
                dq = _dot(ds, k2)
                dk2, dv2 = _dot(ds, q2, _TN), _dot(p, do2, _TN)
                dq_ref[:, sl] = (jnp.where(low, dq[:_ABLK], dq[_ABLK:]) * scale).astype(dq_ref.dtype)
                dk_ref[:, sl] = (ck[:, sl] + dk2[:_ABLK]).astype(dk_ref.dtype)
                dv_ref[:, sl] = (cv[:, sl] + dv2[:_ABLK]).astype(dv_ref.dtype)
                ck[:, sl] = dk2[_ABLK:]
                cv[:, sl] = dv2[_ABLK:]

        @pl.when(n == nb)
        def _():
            dk_ref[...] = ck[...].astype(dk_ref.dtype)
            dv_ref[...] = cv[...].astype(dv_ref.dtype)

    cur, prev = _attn_specs(nb, 3)
    cur1, prev1 = _attn_specs(nb, 1)
    bspec = _bias_spec()
    sv, dov, ldv = _hbm(_strided(qkv, dilation)), _hbm(do), _hbm(_strided(ld, dilation))
    dqkv = pl.pallas_call(
        body, grid=(dilation, nb + 1),
        in_specs=[cur(0), cur(1), prev(1), cur(2), prev(2), cur1(0), _lane_spec(nb), bspec],
        out_specs=[cur1(0), prev1(0), prev1(0)], out_shape=[_SDS(dov.shape, _BF)] * 3, name=name,
        scratch_shapes=[pltpu.VMEM((_ABLK, _AW), F32)] * 2,
        compiler_params=_params(("parallel", "arbitrary"), 16 << 20))(sv, sv, sv, sv, sv, dov, ldv, bias)
    return dqkv


def _ssd_in_specs(ch):
    return dict(
        xs=pl.BlockSpec((_CHUNK, _AW), lambda c: (ch(c), 0)),
        bc=pl.BlockSpec((_CHUNK, 2 * _GROUPS * _NSTATE), lambda c: (ch(c), _AW // (2 * _GROUPS * _NSTATE))),
        lane=pl.BlockSpec((_CHUNK, _LANES), lambda c: (ch(c), 0)),
        arow=pl.BlockSpec((_HEADS, 1, _CHUNK), lambda c: (0, 0, ch(c))),
        st=pl.BlockSpec((1, _HEADS // 2, _NSTATE, _LANES), lambda c: (ch(c), 0, 0, 0)),
    )


def _decay(a_col, a_row):
    i0 = lax.broadcasted_iota(jnp.int32, (_CHUNK, _CHUNK), 0)
    i1 = lax.broadcasted_iota(jnp.int32, (_CHUNK, _CHUNK), 1)
    return jnp.where(i0 >= i1, jnp.exp(a_col - a_row), 0.0), jnp.where(i1 >= i0, jnp.exp(a_row - a_col), 0.0)


def _rsum(v):
    return jnp.sum(v, axis=-1, keepdims=True)


def _ssd_fwd(act, dt, acum, a_row, *, name):
    t = act.shape[0]
    nc = t // _CHUNK
    sp = _ssd_in_specs(lambda c: c)
    gw = _GROUPS * _NSTATE

    def body(xs_ref, bc_ref, dt_ref, ac_ref, ar_ref, y_ref, sall_ref, st):
        @pl.when(pl.program_id(0) == 0)
        def _():
            st[...] = jnp.zeros_like(st)

        low = _low_lanes((_CHUNK, _LANES))
        for g in range(_GROUPS):
            bg = bc_ref[:, g * _NSTATE:(g + 1) * _NSTATE]
            cg = bc_ref[:, gw + g * _NSTATE:gw + (g + 1) * _NSTATE].astype(_BF)
            cb = _dot(cg, bg, _NT)
            for pr in range(g * _HPG // 2, (g + 1) * _HPG // 2):
                ha, hb = 2 * pr, 2 * pr + 1
                a_a, a_b = ac_ref[:, ha:ha + 1], ac_ref[:, hb:hb + 1]
                x = (xs_ref[:, _pair(pr)] * jnp.where(low, dt_ref[:, ha:ha + 1], dt_ref[:, hb:hb + 1])).astype(_BF)
                lm_a, _ = _decay(a_a, ar_ref[ha])
                lm_b, _ = _decay(a_b, ar_ref[hb])
                sv = st[pr]
                sall_ref[0, pr] = sv
                yd = _dot(jnp.concatenate([cb * lm_a, cb * lm_b], axis=0), x)
                yd = jnp.where(low, yd[:_CHUNK], yd[_CHUNK:])
                y_ref[:, _pair(pr)] = yd + jnp.where(low, jnp.exp(a_a), jnp.exp(a_b)) * _dot(cg, sv)
                al_a, al_b = jnp.min(a_a, axis=0, keepdims=True), jnp.min(a_b, axis=0, keepdims=True)
                upd = _dot(jnp.concatenate([bg * jnp.exp(al_a - a_a), bg * jnp.exp(al_b - a_b)], axis=1), x, _TN)
                st[pr] = jnp.where(low, jnp.exp(al_a), jnp.exp(al_b)) * sv + jnp.where(low, upd[:_NSTATE], upd[_NSTATE:])

    return pl.pallas_call(
        body, grid=(nc,), in_specs=[sp['xs'], sp['bc'], sp['lane'], sp['lane'], sp['arow']],
        out_specs=[sp['xs'], sp['st']], out_shape=[_SDS((t, _AW), F32), _SDS((nc, _HEADS // 2, _NSTATE, _LANES), F32)],
        scratch_shapes=[pltpu.VMEM((_HEADS // 2, _NSTATE, _LANES), F32)], name=name,
        compiler_params=_params(("arbitrary",), 16 << 20))(*[_hbm(a) for a in (act, act, dt, acum, a_row)])


def _ssd_bwd(act, dt, acum, a_row, sall, dy, *, name):
    t = act.shape[0]
    nc = t // _CHUNK
    sp = _ssd_in_specs(lambda c: nc - 1 - c)
    gw = _GROUPS * _NSTATE

    def body(xs_ref, bc_ref, dt_ref, ac_ref, ar_ref, sall_ref, dy_ref, dxs_ref, dbc_ref, ddt_ref, da_ref, dst):
        @pl.when(pl.program_id(0) == 0)
        def _():
            dst[...] = jnp.zeros_like(dst)

        ddt_ref[...] = jnp.zeros_like(ddt_ref)
        da_ref[...] = jnp.zeros_like(da_ref)
        row = lax.broadcasted_iota(jnp.int32, (_CHUNK, 1), 0)
        low = _low_lanes((_CHUNK, _LANES))
        for g in range(_GROUPS):
            bg = bc_ref[:, g * _NSTATE:(g + 1) * _NSTATE]
            bgb = bg.astype(_BF)
            cg = bc_ref[:, gw + g * _NSTATE:gw + (g + 1) * _NSTATE].astype(_BF)
            cb, cbt = _dot(cg, bgb, _NT), _dot(bgb, cg, _NT)
            dcb = jnp.zeros((_CHUNK, _CHUNK), F32)
            dbg = jnp.zeros((_CHUNK, _NSTATE), F32)
            dcg = jnp.zeros((_CHUNK, _NSTATE), F32)
            for pr in range(g * _HPG // 2, (g + 1) * _HPG // 2):
                heads = (2 * pr, 2 * pr + 1)
                a_cols = [ac_ref[:, h:h + 1] for h in heads]
                dt_pair = jnp.where(low, dt_ref[:, heads[0]:heads[0] + 1], dt_ref[:, heads[1]:heads[1] + 1])
                xsv = xs_ref[:, _pair(pr)]
                x = xsv * dt_pair
                xb = x.astype(_BF)
                xhs = _halves(xb, low)
                dyv = dy_ref[:, _pair(pr)]
                dyb = dyv.astype(_BF)
                dyhs = _halves(dyb, low)
                sv, dsv = sall_ref[0, pr], dst[pr]
                svb, dsb = sv.astype(_BF), dsv.astype(_BF)
                a_lasts = [jnp.min(a, axis=0, keepdims=True) for a in a_cols]
                e_pair = jnp.where(low, jnp.exp(a_cols[0]), jnp.exp(a_cols[1]))
                el_pair = jnp.where(low, jnp.exp(a_lasts[0]), jnp.exp(a_lasts[1]))
                yo = e_pair * _dot(cg, svb)
                decays = [_decay(a_cols[i], ar_ref[h]) for i, h in enumerate(heads)]
                gms, gmts = [cb * lm for lm, _ in decays], [cbt * lmt for _, lmt in decays]
                w_cols = [jnp.exp(a_lasts[i] - a_cols[i]) for i in range(2)]
                x2, dy2 = jnp.concatenate(xhs, axis=0), jnp.concatenate(dyhs, axis=0)
                bwd = _dot(jnp.concatenate([bg * w_cols[0], bg * w_cols[1]], axis=0), dsb)
                dxg = _dot(jnp.concatenate(gms, axis=1), dyb, _TN)
                dg2, dgt2, xds2 = _dot(dy2, xb, _NT), _dot(x2, dyb, _NT), _dot(x2, dsb, _NT)
                das = []
                for i in range(2):
                    rows_i = slice(i * _CHUNK, (i + 1) * _CHUNK)
                    dcb = dcb + dg2[rows_i] * decays[i][0]
                    dbg = dbg + w_cols[i] * xds2[rows_i]
                    das.append(_rsum(dg2[rows_i] * gms[i]) - _rsum(dgt2[rows_i] * gmts[i]))
                bwd = jnp.where(low, bwd[:_CHUNK], bwd[_CHUNK:])
                dx = jnp.where(low, dxg[:_CHUNK], dxg[_CHUNK:]) + bwd
                edy = (e_pair * dyv).astype(_BF)
                dcg = dcg + _dot(edy, svb, _NT)
                zs, yos, sds, dts = (_halves(v, low) for v in (x * bwd, dyv * yo, sv * dsv, dx * xsv))
                for i, h in enumerate(heads):
                    z = _rsum(zs[i])
                    da_last = jnp.sum(z, axis=0, keepdims=True) + jnp.exp(a_lasts[i]) * jnp.sum(_rsum(sds[i]), axis=0, keepdims=True)
                    da_ref[:, h:h + 1] = das[i] + _rsum(yos[i]) - z + jnp.where(row == _CHUNK - 1, da_last, 0.0)
                    ddt_ref[:, h:h + 1] = _rsum(dts[i])
                dxs_ref[:, _pair(pr)] = dx * dt_pair
                dst[pr] = el_pair * dsv + _dot(cg, edy, _TN)
            dbc_ref[:, g * _NSTATE:(g + 1) * _NSTATE] = dbg + _dot(dcb, cg, _TN)
            dbc_ref[:, gw + g * _NSTATE:gw + (g + 1) * _NSTATE] = dcg + _dot(dcb, bgb)

    ch = lambda c: nc - 1 - c
    wide = pl.BlockSpec((_CHUNK, 2 * gw), lambda c: (ch(c), 0))
    return pl.pallas_call(
        body, grid=(nc,), in_specs=[sp['xs'], sp['bc'], sp['lane'], sp['lane'], sp['arow'], sp['st'], sp['xs']],
        out_specs=[sp['xs'], wide, sp['lane'], sp['lane']],
        out_shape=[_SDS((t, _AW), F32), _SDS((t, 2 * gw), F32), _SDS((t, _LANES), F32), _SDS((t, _LANES), F32)],
        scratch_shapes=[pltpu.VMEM((_HEADS // 2, _NSTATE, _LANES), F32)], name=name,
        compiler_params=_params(("arbitrary",), 16 << 20))(*[_hbm(a) for a in (act, act, dt, acum, a_row, sall, dy)])


def _scan_rows(v, reverse):
    r = lax.broadcasted_iota(jnp.int32, v.shape, 0)
    for s in (1, 2, 4, 8, 16, 32, 64):
        if reverse:
            v = v + jnp.where(r < _CHUNK - s, pltpu.roll(v, _CHUNK - s, 0), 0.0)
        else:
            v = v + jnp.where(r >= s, pltpu.roll(v, s, 0), 0.0)
    return v


def _softplus(x):
    return jnp.maximum(x, 0.0) + jnp.log(1.0 + jnp.exp(-jnp.abs(x)))


def _sigmoid(x):
    return 1.0 / (1.0 + jnp.exp(-x))


def _silu(x):
    return x * _sigmoid(x)


def _dsilu(x):
    s = _sigmoid(x)
    return s * (1.0 + x * (1.0 - s))


def _lanes(a):
    return jnp.pad(a, (0, _LANES - a.shape[0])).reshape(1, _LANES)


def _layer_fwd(x, p, l):
    cch = p['conv_w'].shape[1]
    sv = {}
    h1 = _rms_fwd(x, p['ln1_g'], name=f"ln1_fwd_{l}")
    qkv = _mm(h1, p['w_in'], b_cols=(0, 3 * _AW), outs=(_BF,), name=f"in_proj_qkv_{l}")
    xbc = _mm(h1, p['w_in'], b_cols=(3 * _AW, cch), name=f"in_proj_xbc_{l}")
    zdt = _mm(h1, p['w_in'], b_cols=(3 * _AW + cch, _AW + _LANES), name=f"in_proj_zdt_{l}")
    z, dt_raw = (zdt, _AW, 0), (zdt, _LANES, _AW // _LANES)

    outs = []
    for dil in _DILATIONS:
        outs += _attn_fwd(qkv, dil, name=f"attn_fwd_d{dil}_{l}")

    tile = 2 * _ABLK
    perms = [_perm(d, tile) for d in _DILATIONS[1:]]

    def combine(o1, l1, o2, l2, o3, l3, p2, p3):
        m = jnp.maximum(jnp.maximum(l1, l2), l3)
        e1, e2, e3 = jnp.exp(l1 - m), jnp.exp(l2 - m), jnp.exp(l3 - m)
        tot = e1 + e2 + e3
        mixed = sum(_expand_heads(e / tot) * o for e, o in ((e1, o1.astype(F32)), (e2, _unstride(o2, p2)), (e3, _unstride(o3, p3))))
        return mixed, m + jnp.log(tot)
    outs = [a if i % 2 or i == 0 else ("strided", a, _DILATIONS[i // 2]) for i, a in enumerate(outs)]
    attn, lse = _rows(combine, outs, perms, [(_AW, F32), (_LANES, F32)], tile=tile, name=f"attn_combine_{l}")
    mix = _rms_fwd(attn, p['attn_norm_g'], into=(None, 2 * _AW, 0), name=f"attn_norm_fwd_{l}")

    def conv(u0, before, w, b):
        u1, u2, u3 = _shifted(u0, before, True)
        return _silu(w[0:1] * u3 + w[1:2] * u2 + w[2:3] * u1 + w[3:4] * u0 + b)
    act = _rows(conv, [xbc], [p['conv_w'], p['conv_b'].reshape(1, cch)], [(cch, F32)], halos=[(xbc, -1)], tile=_tile_for(cch),
                name=f"conv_fwd_{l}")[0]

    def dtf(raw, bias, alog):
        dt = _softplus(raw + bias)
        return dt, _scan_rows(dt * -jnp.exp(alog), False)
    dt, acum = _rows(dtf, [dt_raw], [_lanes(p['dt_bias']), _lanes(p['a_log'])], [(_LANES, F32), (_LANES, F32)],
                     tile=_CHUNK, name=f"dt_fwd_{l}")
    a_row = acum[:, :_HEADS].T[:, None, :]
    y_ssd, sall = _ssd_fwd(act, dt, acum, a_row, name=f"ssd_fwd_{l}")
    dskip = jnp.repeat(p['d_skip'], _HDIM).reshape(1, _AW)
    xs = (act, _AW, 0)

    def gate(y, xs, z, dsk):
        return (y + dsk * xs) * _silu(z)
    y2 = _rows(gate, [y_ssd, xs, z], [dskip], [(_AW, F32)], tile=_tile_for(_AW), name=f"gate_fwd_{l}")[0]
    mix = _rms_fwd(y2, p['ssd_norm_g'], groups=_GROUPS, into=(mix, 2 * _AW, 1), name=f"ssd_norm_fwd_{l}")
    sv.update(x=x, h1=h1, qkv=qkv, zdt=zdt, xbc=xbc, attn=attn, lse=lse, act=act, dt=dt, acum=acum, a_row=a_row,
              sall=sall, y_ssd=y_ssd, dskip=dskip, y2=y2, mix=mix)
    return mix, sv


def _layer_fwd_mlp(p, sv, l):
    x2 = _mm(sv['mix'], p['w_out'], extra=(sv['x'],), epi=_add_to, name=f"out_proj_{l}")
    h2 = _rms_fwd(x2, p['ln2_g'], name=f"ln2_fwd_{l}")
    a = _mm(h2, p['w_mlp_in'], b_chips=_CHIPS, epi=lambda acc: (jnp.square(jnp.maximum(acc, 0.0)),), outs=(_BF,), name=f"mlp_in_{l}")
    x3 = _mm(a, p['w_mlp_out'], extra=(x2,), epi=_add_to, name=f"mlp_out_{l}")
    sv.update(x2=x2, h2=h2, a=a)
    return x3


def _layer_bwd(dx3, dx3b, p, sv, l, send, after):
    cch = p['conv_w'].shape[1]
    g = {}
    du = _mm(dx3b, p['w_mlp_out'], tb=True, extra=(sv['a'],), outs=(_BF,), after=after,
             epi=lambda acc, a: (acc * 2.0 * jnp.sqrt(a.astype(F32)),), name=f"mlp_out_dx_{l}")
    g['w_mlp_out'] = _mm(sv['a'], dx3b, ta=True, outs=(_BF,), name=f"mlp_out_dw_{l}")
    g['w_mlp_in'] = _mm(sv['h2'], du, ta=True, out_chips=_CHIPS, outs=(_BF,), name=f"mlp_in_dw_{l}")
    sent = send(('w_mlp_out', 'w_mlp_in'), g)
    dh2 = _mm(du, p['w_mlp_in'], tb=True, b_chips=_CHIPS, after=sent, name=f"mlp_in_dx_{l}")
    dx2, dx2b, g['ln2_g'] = _rms_bwd(sv['x2'], dh2, p['ln2_g'], dx3, name=f"ln2_bwd_{l}")
    dmix = _mm(dx2b, p['w_out'], tb=True, name=f"out_proj_dx_{l}")
    g['w_out'] = _mm(sv['mix'], dx2b, ta=True, outs=(_BF,), name=f"out_proj_dw_{l}")
    after_out = send(('w_out',), g)

    tile = 2 * _ABLK
    perms = [_perm(d, tile) for d in _DILATIONS[1:]]

    def norm_bwd(attn, dy, lse, gn, p2, p3):
        dattn, dgn = _rms_bwd_tile(attn, dy, gn, 1)
        prod, low = dattn * attn, _low_lanes((attn.shape[0], _LANES))
        lane = lax.broadcasted_iota(jnp.int32, lse.shape, 1)
        ld = jnp.where(lane < _HEADS, lse, 0.0)
        for pr in range(_HEADS // 2):
            for i, part in enumerate(_halves(prod[:, _pair(pr)], low)):
                ld = jnp.where(lane == _HEADS + 2 * pr + i, _rsum(part), ld)
        return dattn, _stride(dattn, p2, _DILATIONS[1]), _stride(dattn, p3, _DILATIONS[2]), ld, dgn
    *dos, ld, gn_sum = _rows(norm_bwd, [sv['attn'], (dmix, _AW, 0), sv['lse']], [p['attn_norm_g'].reshape(1, _AW)] + perms,
                             [(_AW, _BF)] + [(_AW, _BF, d) for d in _DILATIONS[1:]] + [(_LANES, F32)], [_AW], after=after_out,
                             tile=tile, name=f"attn_norm_bwd_{l}")
    g['attn_norm_g'] = gn_sum.sum(axis=0)
    parts = [_attn_bwd(sv['qkv'], do, ld, dil, name=f"attn_bwd_d{dil}_{l}") for do, dil in zip(dos, _DILATIONS)]

    def branch_sum(*t):
        parts_, (p2, p3) = t[:9], t[9:]
        t = [a.astype(F32) for a in parts_[:3]] + [_unstride(a, p2) for a in parts_[3:6]] + [_unstride(a, p3) for a in parts_[6:]]
        return jnp.concatenate([t[i] + t[3 + i] + t[6 + i] for i in range(3)], axis=1)
    branch_ins = list(parts[0]) + [("strided", a, d) for pr, d in zip(parts[1:], _DILATIONS[1:]) for a in pr]
    w_all = 3 * _AW + cch + _AW + _LANES
    dproj = _rows(branch_sum, branch_ins, perms, [(3 * _AW, _BF)], into=(None, w_all, 0), tile=tile, name=f"attn_bwd_sum_{l}")[0]

    xs, z, dt_raw = (sv['act'], _AW, 0), (sv['zdt'], _AW, 0), (sv['zdt'], _LANES, _AW // _LANES)

    def gate_bwd(y2, dy, y, xs, z, dsk, gn):
        dy2, dgn = _rms_bwd_tile(y2, dy, gn, _GROUPS)
        dy1 = dy2 * _silu(z)
        return dy1, dsk * dy1, dy2 * (y + dsk * xs) * _dsilu(z), dy1 * xs, dgn
    dy1, dxs_skip, dz, dsk_sum, gn_sum = _rows(
        gate_bwd, [sv['y2'], (dmix, _AW, 1), sv['y_ssd'], xs, z], [sv['dskip'], p['ssd_norm_g'].reshape(1, _AW)],
        [(_AW, F32), (_AW, F32), (_AW, _BF)], [_AW, _AW], tile=128, name=f"gate_bwd_{l}")
    g['ssd_norm_g'] = gn_sum.sum(axis=0)
    g['d_skip'] = dsk_sum.sum(axis=0).reshape(_HEADS, _HDIM).sum(axis=1)
    dxs, dbc, ddt, da = _ssd_bwd(sv['act'], sv['dt'], sv['acum'], sv['a_row'], sv['sall'], dy1, name=f"ssd_bwd_{l}")

    def dtb(da, ddtx, raw, dt, dz, bias, alog):
        a = -jnp.exp(alog)
        dda = _scan_rows(da, True)
        draw = (dda * a + ddtx) * _sigmoid(raw + bias)
        return jnp.concatenate([dz, draw.astype(dz.dtype)], axis=1), draw, dda * dt * a
    dproj, dbias, dalog = _rows(dtb, [da, ddt, dt_raw, sv['dt'], dz], [_lanes(p['dt_bias']), _lanes(p['a_log'])],
                                [(_AW + _LANES, _BF)], [_LANES, _LANES], into=(dproj, w_all, (3 * _AW + cch) // (_AW + _LANES)),
                                tile=_CHUNK, name=f"dt_bwd_{l}")
    g['dt_bias'], g['a_log'] = dbias.sum(axis=0)[:_HEADS], dalog.sum(axis=0)[:_HEADS]
    def conv_bwd1(u0, dxs, dbc, dxk, before, w, b):
        u1, u2, u3 = _shifted(u0, before, True)
        pre = w[0:1] * u3 + w[1:2] * u2 + w[2:3] * u1 + w[3:4] * u0 + b
        dp = jnp.concatenate([dxs + dxk, dbc], axis=1) * _dsilu(pre)
        return dp, dp * u3, dp * u2, dp * u1, dp * u0, dp
    dpre, *dws = _rows(conv_bwd1, [sv['xbc'], dxs, dbc, dxs_skip], [p['conv_w'], p['conv_b'].reshape(1, cch)], [(cch, F32)],
                       [cch] * 5, halos=[(sv['xbc'], -1)], tile=128, name=f"conv_bwd_pre_{l}")
    g['conv_w'] = jnp.stack([dws[i].sum(axis=0) for i in range(_CONV_K)])
    g['conv_b'] = dws[4].sum(axis=0)

    def conv_bwd2(p0, after_, w):
        p1, p2, p3 = _shifted(p0, after_, False)
        return w[3:4] * p0 + w[2:3] * p1 + w[1:2] * p2 + w[0:1] * p3
    dproj = _rows(conv_bwd2, [dpre], [p['conv_w']], [(cch, _BF)], halos=[(dpre, 1)], into=(dproj, w_all, 3 * _AW // cch),
                  tile=_tile_for(cch), name=f"conv_bwd_in_{l}")[0]
    g_all = _mm(sv['h1'], dproj, ta=True, outs=(_BF,), name=f"in_proj_dw_{l}")
    z0 = 3 * _AW + cch
    g['w_in'] = jnp.concatenate([g_all[:, :3 * _AW], g_all[:, z0:z0 + _AW], g_all[:, 3 * _AW:z0], g_all[:, z0 + _AW:z0 + _AW + _HEADS]], axis=1)
    sent = send(('w_in',), g)
    for n in _BIG:
        del g[n]
    dh1 = _mm(dproj, p['w_in'], tb=True, after=sent, name=f"in_proj_dx_{l}")
    dx, dxb, g['ln1_g'] = _rms_bwd(sv['x'], dh1, p['ln1_g'], dx2, name=f"ln1_bwd_{l}")
    return dx, dxb, g


def _loss_bwd(x, g, tgt):
    w = x.shape[1]
    tile = _tile_for(w)

    def fn(x, tgt, g):
        r = _rstd(x)
        xh = x * r
        e = xh * g - tgt
        gd = e * (g / w)
        dx = r * (gd - xh * jnp.mean(gd * xh, axis=-1, keepdims=True))
        rowloss = 0.5 * jnp.mean(e * e, axis=-1, keepdims=True)
        return dx, dx, (e / w) * xh, jnp.broadcast_to(rowloss, (tile, _LANES))
    dx, dxb, dg, ls = _rows(fn, [x, tgt], [g.reshape(1, w)], [(w, F32), (w, _BF)], [w, _LANES], tile=tile, name="loss_head")
    return dx, dxb, dg.sum(axis=0), ls[:, 0].sum()


def _adamw_math(w, g, m, v):
    m2 = _B1 * m + (1.0 - _B1) * g
    v2 = _B2 * v + (1.0 - _B2) * jnp.square(g)
    m_hat = m2 / (1.0 - _B1 ** _STEP)
    v_hat = v2 / (1.0 - _B2 ** _STEP)
    return -_LR * (m_hat / (jnp.sqrt(v_hat) + _AEPS) + _WD * w), m2, v2


def _adamw(w, g, m, v, *, name):
    width = w.shape[-1]
    flat = [a.reshape(-1, width) for a in (w, g, m, v)]
    tile = _pick(flat[0].shape[0], (_tile_for(width), 32, 8))
    res = _rows(_adamw_math, flat, [], [(width, F32)] * 3, tile=tile, name=name)
    return [r.reshape(w.shape) for r in res]


_HBM = pl.BlockSpec(memory_space=pltpu.HBM)


def _place():
    x, y, c = lax.axis_index("x"), lax.axis_index("y"), lax.axis_index("c")
    other_chips = [(1 - x, y), (x, 1 - y), (1 - x, 1 - y)]
    return x, y, c, other_chips


def _remote(src, dst, sems, i, dev):
    return pltpu.make_async_remote_copy(src_ref=src, dst_ref=dst, send_sem=sems[0].at[i], recv_sem=sems[1].at[i],
                                        device_id=dev, device_id_type=_MESH)


def _exchange8(v, *, reduce, after=None, name):
    r, w = v.shape
    behind = [] if after is None else [after]

    def body(v_ref, *rest):
        all_ref, rest = rest[len(behind)], rest[len(behind) + 1:]
        sems = rest[-2:]
        x, y, c, _ = _place()
        me = 4 * x + 2 * y + c
        all_ref[me] = v_ref[...]
        flips = [((d >> 2) & 1, (d >> 1) & 1, d & 1) for d in range(1, 8)]
        sends = [_remote(v_ref, all_ref.at[me], sems, i, (x ^ fx, y ^ fy, c ^ fc)) for i, (fx, fy, fc) in enumerate(flips)]
        for cp in sends:
            cp.start()
        for i, (fx, fy, fc) in enumerate(flips):
            _remote(v_ref, all_ref.at[me ^ (4 * fx + 2 * fy + fc)], sems, i, (x ^ fx, y ^ fy, c ^ fc)).wait_recv()
        for cp in sends:
            cp.wait_send()
        if reduce:
            acc = all_ref[0]
            for s in range(1, 8):
                acc = acc + all_ref[s]
            rest[0][...] = acc

    vm = pl.BlockSpec(memory_space=pltpu.VMEM)
    out_shape = [_SDS((8, r, w), v.dtype)] + ([_SDS((r, w), v.dtype)] if reduce else [])
    res = pl.pallas_call(body, in_specs=[vm] + [_ANY] * len(behind), out_specs=[vm] * len(out_shape), out_shape=out_shape, name=name,
                         scratch_shapes=[pltpu.SemaphoreType.DMA((7,)), pltpu.SemaphoreType.DMA((7,))],
                         compiler_params=pltpu.CompilerParams(vmem_limit_bytes=int(32 << 20)))(v, *behind)
    return res[1] if reduce else res[0]


_SEM = pl.BlockSpec(memory_space=pltpu.SEMAPHORE)
_ANY = pl.BlockSpec(memory_space=pl.ANY)
_EFFECT = pltpu.SideEffectType.DATAFLOW_SIDE_EFFECTING


def _send_start(name, srcs, land_shapes, plan, n_sends, after):
    ns, nl = len(srcs), len(land_shapes)
    zones = [_hbm(lax.empty(s.shape, s.dtype)) if isinstance(s, _SDS) else s for s in land_shapes]

    def body(*refs):
        ins, lands, sems = refs[:ns], refs[ns:ns + nl], refs[ns + nl + 1:ns + nl + 3]
        x, y, c, chips = _place()
        for i, (s, d, dev) in enumerate(plan(x, y, c, chips, ins, lands)[0]):
            _remote(s, d, sems, i, dev).start()
        refs[-1][...] = jnp.zeros_like(refs[-1])

    sem = pltpu.SemaphoreType.DMA((n_sends,))
    res = pl.pallas_call(
        body, name=name, in_specs=[_HBM] * (ns + nl) + [_ANY],
        out_shape=(sem, sem, *[pltpu.HBM(s.shape, s.dtype) for s in land_shapes], _SDS((8, _LANES), F32)),
        out_specs=(_SEM, _SEM, *[_HBM] * nl, pl.BlockSpec(memory_space=pltpu.VMEM)),
        input_output_aliases={ns + i: 2 + i for i in range(nl)},
        compiler_params=pltpu.CompilerParams(has_side_effects=_EFFECT))(
            *[_hbm(s) for s in srcs], *zones, after)
    return dict(sems=res[:2], srcs=srcs, lands=res[2:2 + nl], plan=plan), res[-1]


def _send_wait(name, h, after):
    ns, nl = len(h['srcs']), len(h['lands'])

    def body(*refs):
        ins, lands, sems = refs[:ns], refs[ns:ns + nl], refs[ns + nl:ns + nl + 2]
        x, y, c, chips = _place()
        sends, landings = h['plan'](x, y, c, chips, ins, lands)
        for i, (s, d, dev) in enumerate(sends):
            _remote(s, d, sems, i, dev).wait_send()
        for i, d in enumerate(landings):
            _remote(d, d, sems, i, sends[i][2]).wait_recv()

    return pl.pallas_call(
        body, name=name, in_specs=[_HBM] * (ns + nl) + [_SEM, _SEM, _ANY],
        out_shape=tuple(pltpu.HBM(a.shape, a.dtype) for a in h['lands']), out_specs=tuple([_HBM] * nl),
        input_output_aliases={ns + i: i for i in range(nl)},
        compiler_params=pltpu.CompilerParams(has_side_effects=_EFFECT))(
            *[_hbm(s) for s in h['srcs']], *h['lands'], *h['sems'], after)


def _gather_plan(items):
    def plan(x, y, c, chips, ins, lands):
        k = 2 * x + y
        to = [(px, py, c) for px, py in chips] + [(x, y, 1 - c)]
        sends = [(ins[si].at[l], lands[t].at[k], dev) for t, (si, l) in enumerate(items) for dev in to]
        return sends, [lands[t].at[2 * px + py] for t in range(len(items)) for px, py in chips + [(x, y)]]
    return plan


_FLIPS = [((d >> 2) & 1, (d >> 1) & 1, d & 1) for d in range(1, 8)]


def _reduce_plan(halves):
    def plan(x, y, c, chips, ins, lands):
        sends, landings = [], []
        for t, hf in enumerate(halves):
            for i, (fx, fy, fc) in enumerate(_FLIPS):
                px, py, pc = x ^ fx, y ^ fy, c ^ fc
                sends.append((ins[t].at[2 * px + py, pl.ds(pc * hf, hf)], lands[t].at[i], (px, py, pc)))
                landings.append(lands[t].at[i])
        return sends, landings
    return plan


def _swap(name, srcs, out_shapes, plan, n_sends):
    n = len(srcs)

    def body(*refs):
        ins, outs, sems = refs[:n], refs[n:n + len(out_shapes)], refs[-2:]
        x, y, c, chips = _place()
        sends, landings = plan(x, y, c, chips, ins, outs)
        out = [_remote(s, d, sems, i, dev) for i, (s, d, dev) in enumerate(sends)]
        for cp in out:
            cp.start()
        for i, d in enumerate(landings):
            _remote(d, d, sems, i, sends[i][2]).wait_recv()
        for cp in out:
            cp.wait_send()

    return pl.pallas_call(
        body, in_specs=[_HBM] * n, out_specs=[_HBM] * len(out_shapes), out_shape=out_shapes, name=name,
        scratch_shapes=[pltpu.SemaphoreType.DMA((n_sends,)), pltpu.SemaphoreType.DMA((n_sends,))])(*srcs)


def _sum_owned(grads, landed, c, k, names):
    def sum8(*parts):
        acc = parts[0].astype(F32)
        for p in parts[1:]:
            acc = acc + p.astype(F32)
        return acc
    outs = []
    for g, got, name in zip(grads, landed, names):
        hf, b = got.shape[1:]
        own = lax.dynamic_slice_in_dim(lax.dynamic_index_in_dim(g, k, axis=0, keepdims=False), c * hf, hf, axis=0)
        outs.append(_rows(sum8, [own] + [("slot", got, i) for i in range(len(_FLIPS))], [], [(b, F32)],
                          tile=_pick(hf, (_tile_for(b), 32)), name=f"grad_sum_{name}")[0])
    return outs


def _share_halves(mine, *, name):
    n = len(mine)

    def plan(x, y, c_, chips, ins, outs):
        return [(ins[t], outs[t], (x, y, 1 - c_)) for t in range(n)], [outs[t] for t in range(n)]
    return _swap(name, mine, [_SDS(h.shape, F32) for h in mine], plan, n)


def _adamw_owned(w, mine, theirs, m, v, c, *, name):
    depth, a, b = w.shape
    half = a // 2
    tile = _pick(half, (_tile_for(b), 32, 8))
    nh = half // tile

    def blocks_of(l):
        return lambda i: (jnp.clip(i - 2 * nh * l, 0, 2 * nh - 1) % nh, 0)

    def fn(w, m, v, *rest):
        halves, cflag = rest[:-1], rest[-1]
        step = pl.program_id(0)
        is_mine = cflag[0:1, 0:1] == ((step // nh) % 2).astype(F32)
        g = jnp.where(is_mine, halves[0], halves[1])
        for l in range(1, depth):
            g = jnp.where(step >= 2 * nh * l, jnp.where(is_mine, halves[2 * l], halves[2 * l + 1]), g)
        return (g,) + _adamw_math(w, g, m, v)
    ins = [a_.reshape(depth * a, b) for a_ in (w, m, v)]
    ins += [(h, b, blocks_of(l)) for l in range(depth) for h in (mine[l], theirs[l])]
    res = _rows(fn, ins, [jnp.full((1, _LANES), c, F32)], [(b, F32)] * 4, tile=tile, name=name)
    return [r.reshape(w.shape) for r in res]


_BIG = ("w_in", "w_out", "w_mlp_in", "w_mlp_out")
_SMALL = ("ln1_g", "conv_b", "dt_bias", "a_log", "d_skip", "attn_norm_g", "ssd_norm_g", "ln2_g", "final_norm_g")
_ORDER = ("ln1_g", "w_in", "conv_w", "conv_b", "dt_bias", "a_log", "d_skip", "attn_norm_g", "ssd_norm_g", "w_out", "ln2_g",
          "w_mlp_in", "w_mlp_out", "final_norm_g")


def _pack(parts, rows):
    flat = jnp.concatenate([p.reshape(-1) for p in parts])
    return jnp.pad(flat, (0, rows * _LANES - flat.shape[0])).reshape(rows, _LANES)


def _unpack(buf, like):
    flat, out, o = buf.reshape(-1), [], 0
    for p in like:
        out.append(flat[o:o + p.size].reshape(p.shape))
        o += p.size
    return out


def kernel(x, ln1_g, w_in, conv_w, conv_b, dt_bias, a_log, d_skip, attn_norm_g, ssd_norm_g, w_out, ln2_g, w_mlp_in, w_mlp_out, final_norm_g, loss_target, m_ln1_g, m_w_in, m_conv_w, m_conv_b, m_dt_bias, m_a_log, m_d_skip, m_attn_norm_g, m_ssd_norm_g, m_w_out, m_ln2_g, m_w_mlp_in, m_w_mlp_out, m_final_norm_g, v_ln1_g, v_w_in, v_conv_w, v_conv_b, v_dt_bias, v_a_log, v_d_skip, v_attn_norm_g, v_ssd_norm_g, v_w_out, v_ln2_g, v_w_mlp_in, v_w_mlp_out, v_final_norm_g):
    w = dict(ln1_g=ln1_g, w_in=w_in, conv_w=conv_w, conv_b=conv_b, dt_bias=dt_bias, a_log=a_log, d_skip=d_skip,
             attn_norm_g=attn_norm_g, ssd_norm_g=ssd_norm_g, w_out=w_out, ln2_g=ln2_g, w_mlp_in=w_mlp_in, w_mlp_out=w_mlp_out,
             final_norm_g=final_norm_g)
    m = dict(ln1_g=m_ln1_g, w_in=m_w_in, conv_w=m_conv_w, conv_b=m_conv_b, dt_bias=m_dt_bias, a_log=m_a_log, d_skip=m_d_skip,
             attn_norm_g=m_attn_norm_g, ssd_norm_g=m_ssd_norm_g, w_out=m_w_out, ln2_g=m_ln2_g, w_mlp_in=m_w_mlp_in,
             w_mlp_out=m_w_mlp_out, final_norm_g=m_final_norm_g)
    v = dict(ln1_g=v_ln1_g, w_in=v_w_in, conv_w=v_conv_w, conv_b=v_conv_b, dt_bias=v_dt_bias, a_log=v_a_log, d_skip=v_d_skip,
             attn_norm_g=v_attn_norm_g, ssd_norm_g=v_ssd_norm_g, w_out=v_w_out, ln2_g=v_ln2_g, w_mlp_in=v_w_mlp_in,
             w_mlp_out=v_w_mlp_out, final_norm_g=v_final_norm_g)
    depth, d_model = ln1_g.shape
    n_chips = _CHIPS
    c = lax.axis_index("c")
    chip = 2 * lax.axis_index("x") + lax.axis_index("y")
    in_proj = w_in.shape[2] * n_chips
    cch = conv_w.shape[2] * n_chips
    zdt_pad = _LANES - _HEADS

    cw = _exchange8(conv_w.reshape(depth * _CONV_K, -1), reduce=False, name="gather_conv_w")[0::2]
    conv_full = cw.reshape(n_chips, depth, _CONV_K, -1).transpose(1, 2, 0, 3).reshape(depth, _CONV_K, cch)
    own = [w[n].astype(_BF) for n in _BIG]

    def start_gather(tag, items, after):
        lands = [_SDS((n_chips, *own[i].shape[1:]), _BF) for i, _ in items]
        return _send_start(f"gather_start_{tag}", own, lands, _gather_plan(items), n_chips * len(items), after)

    def finish_gather(tag, handle, items, after):
        landed = _send_wait(f"gather_wait_{tag}", handle, after)
        return {_BIG[i]: g for (i, _), g in zip(items, landed)}

    def layer_weights(l, blocks):
        p = {}
        if 'w_in' in blocks:
            full_in = blocks['w_in'].transpose(1, 0, 2).reshape(d_model, in_proj)
            p['w_in'] = jnp.concatenate([full_in[:, :3 * _AW], full_in[:, 4 * _AW:4 * _AW + cch], full_in[:, 3 * _AW:4 * _AW],
                                         full_in[:, 4 * _AW + cch:], jnp.zeros((d_model, zdt_pad), _BF)], axis=1)
        if 'w_out' in blocks:
            p['w_out'] = blocks['w_out'].reshape(-1, d_model)
            p['w_mlp_in'] = blocks['w_mlp_in']
            p['w_mlp_out'] = blocks['w_mlp_out'].reshape(-1, d_model)
        return p

    groups = dict(a=[(0, 0)], b=[(1, 0), (2, 0), (3, 0)], c=[(0, 1)], d=[(1, 1), (2, 1), (3, 1)])
    handles, token = {}, conv_full

    half_in = own[0].shape[1] // 2

    def rows_of(ref, who):
        return ref.at[pl.ds(who * half_in, half_in)]

    def plan_a(x_, y_, c_, chips, ins, lands):
        k = 2 * x_ + y_
        sends = [(rows_of(ins[0].at[0], c_), rows_of(lands[0].at[k], c_), (px, py, c_)) for px, py in chips]
        sends.append((ins[0].at[0], lands[0].at[k], (x_, y_, 1 - c_)))
        return sends, [rows_of(lands[0].at[2 * px + py], c_) for px, py in chips] + [lands[0].at[k]]

    def plan_pass(x_, y_, c_, chips, ins, lands):
        sends = [(rows_of(lands[0].at[2 * px + py], c_),) * 2 + ((x_, y_, 1 - c_),) for px, py in chips]
        return sends, [rows_of(lands[0].at[2 * px + py], 1 - c_) for px, py in chips]
    handles["a"], token = _send_start("gather_start_a", own[:1], [_SDS((n_chips, *own[0].shape[1:]), _BF)], plan_a, n_chips, token)
    for tag, items in list(groups.items())[1:]:
        handles[tag], token = start_gather(tag, items, token)
    landed = _send_wait("gather_land_a", handles["a"], token)
    handles["a"], token = _send_start("gather_pass_a", [], landed, plan_pass, 3, landed[0])
    layers = [{n: w[n][l] for n in _SMALL[:-1]} for l in range(depth)]
    for l in range(depth):
        layers[l]['conv_w'] = conv_full[l]

    layers[0].update(layer_weights(0, finish_gather("a", handles["a"], groups["a"], token)))
    mix, sv0 = _layer_fwd(x[0], layers[0], 0)
    layers[0].update(layer_weights(0, finish_gather("b", handles["b"], groups["b"], mix)))
    h = _layer_fwd_mlp(layers[0], sv0, 0)
    layers[1].update(layer_weights(1, finish_gather("c", handles["c"], groups["c"], h)))
    mix, sv1 = _layer_fwd(h, layers[1], 1)
    layers[1].update(layer_weights(1, finish_gather("d", handles["d"], groups["d"], mix)))
    h = _layer_fwd_mlp(layers[1], sv1, 1)
    saved = [sv0, sv1]

    def by_chip(g, name):
        if name == "w_mlp_in":
            return g
        if name == "w_in":
            return g.reshape(d_model, n_chips, -1).transpose(1, 0, 2)
        return g.reshape(n_chips, -1, d_model)

    pending = []

    def sender(l):
        def send(names, g):
            srcs = [by_chip(g[n], n) for n in names]
            halves = [s.shape[1] // 2 for s in srcs]
            lands = [_SDS((len(_FLIPS), hf, s.shape[2]), _BF) for s, hf in zip(srcs, halves)]
            handle, tok = _send_start(f"grad_start_{names[-1]}_{l}", srcs, lands, _reduce_plan(halves), len(_FLIPS) * len(srcs), srcs[0])
            pending.append((l, names, srcs, handle))
            return tok
        return send

    dx, dxb, g_final, loss_part = _loss_bwd(h, final_norm_g, loss_target[0])
    grads, after = [None] * depth, None
    for l in reversed(range(depth)):
        dx, dxb, grads[l] = _layer_bwd(dx, dxb, layers[l], saved[l], l, sender(l), after)
        after = dx
    landed_of = {}
    sent_in = {(n, l): (gi, j) for gi, (l, names, _, _) in enumerate(pending) for j, n in enumerate(names)}

    def landed_for(gi, after):
        if gi not in landed_of:
            l, names, _, handle = pending[gi]
            landed_of[gi] = _send_wait(f"grad_wait_{names[-1]}_{l}", handle, after)
        return landed_of[gi]

    red, delta, new_m, new_v = {}, {}, {}, {}
    after = dx
    for n in ("w_mlp_out", "w_mlp_in", "w_out", "w_in"):
        mine = []
        for l in range(depth):
            gi, j = sent_in[(n, l)]
            got = landed_for(gi, after)[j]
            mine.append(_sum_owned([pending[gi][2][j]], [got], c, chip, [f"{n}_{l}"])[0])
        theirs = _share_halves(mine, name=f"grad_share_{n}")
        red[n], delta[n], new_m[n], new_v[n] = _adamw_owned(w[n], mine, theirs, m[n], v[n], c, name=f"adamw_{n}")
        after = delta[n]

    small = {n: jnp.stack([grads[l][n] for l in range(depth)]) for n in _SMALL[:-1] + ("conv_w",)}
    small["final_norm_g"] = g_final
    parts = [loss_part.reshape(1)] + [small[n] for n in _SMALL + ("conv_w",)]
    rows = -(-sum(p.size for p in parts) // 1024) * 8
    tot = _unpack(_exchange8(_pack(parts, rows), reduce=True, after=red[_BIG[0]], name="allreduce_small"), parts)
    loss = tot[0][0]
    red.update(zip(_SMALL + ("conv_w",), tot[1:]))
    red["conv_w"] = lax.dynamic_index_in_dim(red["conv_w"].reshape(depth, _CONV_K, n_chips, -1), chip, axis=2, keepdims=False)

    names = _SMALL + ("conv_w",)
    like = [w[n] for n in names]
    srows = -(-sum(p.size for p in like) // 1024) * 8
    res = _adamw(*[_pack([d[n] for n in names], srows) for d in (w, red, m, v)], name="adamw_small")
    for dst, buf in zip((delta, new_m, new_v), res):
        dst.update(zip(names, _unpack(buf, like)))
    return (loss, dx[None], *[red[n] for n in _ORDER], *[delta[n] for n in _ORDER], *[new_m[n] for n in _ORDER],
            *[new_v[n] for n in _ORDER])
```

```python
import numpy as np
import jax
import jax.numpy as jnp
from jax import lax
from jax.experimental import pallas as pl
from jax.experimental.pallas import tpu as pltpu

F32 = jnp.float32
_BF = jnp.bfloat16
_NEG = -1e30
_EPS = 1e-5
_HEADS = 16
_HDIM = 64
_AW = _HEADS * _HDIM
_ABLK = 128
_DILATIONS = (1, 4, 16)
_CHUNK = 128
_NSTATE = 128
_GROUPS = 2
_HPG = _HEADS // _GROUPS
_CONV_K = 4
_LANES = 128
_CHIPS = 4
_LR, _B1, _B2, _AEPS, _WD, _STEP = 0.001, 0.9, 0.999, 1e-08, 0.01, 10
_VMEM_CAP = 56 * 1024 * 1024
_MESH = pl.DeviceIdType.MESH
_SDS = jax.ShapeDtypeStruct
_NT = (((1,), (1,)), ((), ()))
_TN = (((0,), (0,)), ((), ()))


def _params(sem, est_bytes):
    lim = int(min(max(2 * est_bytes + (4 << 20), 16 << 20), _VMEM_CAP))
    return pltpu.CompilerParams(dimension_semantics=sem, vmem_limit_bytes=lim)


def _nbytes(shape, dtype):
    return int(np.prod(shape)) * jnp.dtype(dtype).itemsize


def _hbm(a):
    return pltpu.with_memory_space_constraint(a, pltpu.HBM)


def _dot(a, b, dims=(((1,), (0,)), ((), ()))):
    return lax.dot_general(a.astype(_BF), b.astype(_BF), dims, preferred_element_type=F32)


_HALO = 8


def _rows(fn, ins, consts, outs, sums=(), *, halos=(), into=None, after=None, tile, name):
    rows = (ins[0][0] if isinstance(ins[0], tuple) else ins[0]).shape[0]
    n_steps = rows // tile

    def norm_in(a):
        if not isinstance(a, tuple):
            return a, tile, a.shape[1], lambda i: (i, 0)
        if isinstance(a[0], str) and a[0] == "slot":
            return a[1], (None, tile, a[1].shape[2]), a[1].shape[2], lambda i, s=a[2]: (s, i, 0)
        if isinstance(a[0], str):
            return a[1], tile // a[2], a[1].shape[1], lambda i: (i, 0)
        return a[0], tile, a[1], a[2] if callable(a[2]) else (lambda i, j=a[2]: (i, j))
    ins = [norm_in(a) for a in ins]
    outs = [(w, dt, d[0] if d else 1) for w, dt, *d in outs]
    n_in, n_h, n_c, n_o, n_s = len(ins), len(halos), len(consts), len(outs), len(sums)
    n_x = int(into is not None and into[0] is not None) + int(after is not None)

    def body(*refs):
        step = pl.program_id(0)
        vals = [r[...] for r in refs[:n_in]]
        for r, (_, side) in zip(refs[n_in:n_in + n_h], halos):
            vals.append(jnp.where(step == (0 if side < 0 else n_steps - 1), 0.0, r[...]))
        vals += [r[...] for r in refs[n_in + n_h:n_in + n_h + n_c]]
        refs = refs[:n_in] + refs[n_in + n_h:]
        res = fn(*vals)
        res = res if isinstance(res, tuple) else (res,)
        orefs = refs[n_in + n_c + n_x:n_in + n_c + n_x + n_o]
        srefs = refs[n_in + n_c + n_x + n_o:]
        for r, v in zip(orefs, res[:n_o]):
            r[...] = v.astype(r.dtype)
        if n_s:
            @pl.when(pl.program_id(0) == 0)
            def _():
                for r in srefs:
                    r[...] = jnp.zeros_like(r)
            for r, v in zip(srefs, res[n_o:]):
                r[...] += v.reshape(tile // 8, 8, v.shape[-1]).sum(axis=0)

    per = tile // _HALO
    in_specs = [pl.BlockSpec(r if isinstance(r, tuple) else (r, w), idx) for _, r, w, idx in ins]
    in_specs += [pl.BlockSpec((_HALO, a.shape[1]), (lambda i: (jnp.maximum(i * per - 1, 0), 0)) if side < 0
                              else (lambda i: (jnp.minimum((i + 1) * per, rows // _HALO - 1), 0))) for a, side in halos]
    in_specs += [pl.BlockSpec(c.shape, lambda i, nd=c.ndim: (0,) * nd) for c in consts]
    out_shape = [_SDS((rows // d, d * w), dt) for w, dt, d in outs] + [_SDS((8, w), F32) for w in sums]
    out_specs = [pl.BlockSpec((tile // d, d * w), lambda i: (i, 0)) for w, _, d in outs]
    out_specs += [pl.BlockSpec((8, w), lambda i: (0, 0)) for w in sums]
    est = (sum(_nbytes((tile if isinstance(r, tuple) else r, w), a.dtype) for a, r, w, _ in ins)
           + sum(_nbytes((tile, w), dt) for w, dt, _ in outs))
    shared, aliases = [], {}
    if into is not None:
        buf, total, j = into
        out_shape[0] = _SDS((rows, total), outs[0][1])
        out_specs[0] = pl.BlockSpec((tile, outs[0][0]), lambda i: (i, j))
        if buf is not None:
            shared, aliases = [buf], {n_in + n_h + n_c: 0}
    if after is not None:
        shared.append(after)
    in_specs += [pl.BlockSpec(memory_space=pl.ANY)] * len(shared)
    return pl.pallas_call(body, grid=(n_steps,), in_specs=in_specs, out_specs=out_specs, out_shape=out_shape, name=name,
                          input_output_aliases=aliases, compiler_params=_params(("arbitrary",), 3 * est))(
                              *[_hbm(a[0]) for a in ins], *[_hbm(a) for a, _ in halos], *consts, *shared)


def _perm(d, tile):
    p = np.zeros((tile, tile), np.float32)
    t = np.arange(tile)
    p[t, (t % d) * (tile // d) + t // d] = 1.0
    return jnp.asarray(p, _BF)


def _unstride(s, p):
    d = p.shape[0] // s.shape[0]
    w = s.shape[1] // d
    return _dot(p, jnp.concatenate([s[:, r * w:(r + 1) * w] for r in range(d)], axis=0))


def _stride(x, p, d):
    z = _dot(p, x, _TN)
    n = x.shape[0] // d
    return jnp.concatenate([z[r * n:(r + 1) * n] for r in range(d)], axis=1)


def _shifted(u, halo, back):
    n = u.shape[0] + _HALO
    if back:
        ext = jnp.concatenate([halo, u], axis=0)
        return [pltpu.roll(ext, j, 0)[_HALO:] for j in (1, 2, 3)]
    ext = jnp.concatenate([u, halo], axis=0)
    return [pltpu.roll(ext, n - j, 0)[:u.shape[0]] for j in (1, 2, 3)]


def _tile_for(width):
    return max(c for c in (256, 128, 64, 32) if c * width <= (1 << 18) or c == 32)


def _rstd(x):
    return lax.rsqrt(jnp.mean(x * x, axis=-1, keepdims=True) + _EPS)


def _split(x, groups):
    w = x.shape[-1] // groups
    return [x[:, g * w:(g + 1) * w] for g in range(groups)]


def _cat(parts):
    return parts[0] if len(parts) == 1 else jnp.concatenate(parts, axis=-1)


def _rms_bwd_tile(x, dy, g, groups):
    dxs, dgs = [], []
    for xs, ds, gs in zip(_split(x, groups), _split(dy.astype(F32), groups), _split(g, groups)):
        r = _rstd(xs)
        xh = xs * r
        gd = ds * gs
        dxs.append(r * (gd - xh * jnp.mean(gd * xh, axis=-1, keepdims=True)))
        dgs.append(ds * xh)
    return _cat(dxs), _cat(dgs)


def _rms_fwd(x, g, *, groups=1, into=None, name):
    def fn(x, g):
        return _cat([xs * _rstd(xs) * gs for xs, gs in zip(_split(x, groups), _split(g, groups))])
    w = x.shape[1]
    return _rows(fn, [x], [g.reshape(1, w)], [(w, _BF)], into=into, tile=_tile_for(w), name=name)[0]


def _rms_bwd(x, dy, g, res, *, name):
    def fn(x, dy, res, g):
        dx, dg = _rms_bwd_tile(x, dy, g, 1)
        return dx + res, dx + res, dg
    w = x.shape[1]
    dx, dxb, dg = _rows(fn, [x, dy, res], [g.reshape(1, w)], [(w, F32), (w, _BF)], [w], tile=_tile_for(w), name=name)
    return dx, dxb, dg.sum(axis=0)


def _pick(n, cands):
    for c in cands:
        if n % c == 0:
            return c
    raise ValueError(f"no block size for {n}")


_MM_BLOCKS = (1024, 1152, 512, 384)


def _mm(a, b, *, ta=False, tb=False, extra=(), epi=None, outs=(F32,), after=None, b_chips=0, out_chips=0, b_cols=None, name):
    m, k = (a.shape[1], a.shape[0]) if ta else a.shape
    b_shape = (b.shape[1], b.shape[2] * b_chips) if b_chips else b.shape
    if b_cols is not None:
        b_shape = (b.shape[0], b_cols[1])
    n = b_shape[0] if tb else b_shape[1]
    assert k == (b_shape[1] if tb else b_shape[0])
    n_cap = n // max(out_chips, 1 if tb else b_chips, 1)
    k_cap = k // (b_chips if (b_chips and tb) else 1)
    bm, bn = _pick(m, _MM_BLOCKS), _pick(n_cap, _MM_BLOCKS)
    bk = _pick(k_cap, (2048, 1920) + _MM_BLOCKS)
    nk = k // bk
    n_e, n_o = len(extra), len(outs)
    behind = [] if after is None else [after]
    dims = (((0 if ta else 1,), (1 if tb else 0,)), ((), ()))

    def body(a_ref, b_ref, *rest):
        ex, orefs, acc = rest[:n_e], rest[n_e + len(behind):n_e + len(behind) + n_o], rest[-1]
        kk = pl.program_id(2)

        @pl.when(kk == 0)
        def _():
            acc[...] = jnp.zeros_like(acc)

        acc[...] += _dot(a_ref[...], b_ref[...], dims)

        @pl.when(kk == nk - 1)
        def _():
            r = acc[...]
            res = epi(r, *[e[...] for e in ex]) if epi is not None else (r,)
            for o, v in zip(orefs, res):
                o[...] = v.astype(o.dtype)

    a_spec = pl.BlockSpec((bk, bm), lambda i, j, kk: (kk, i)) if ta else pl.BlockSpec((bm, bk), lambda i, j, kk: (i, kk))
    if b_chips and tb:
        per = k_cap // bk
        b_spec = pl.BlockSpec((None, bn, bk), lambda i, j, kk: (kk // per, j, kk % per))
    elif b_chips:
        per = n_cap // bn
        b_spec = pl.BlockSpec((None, bk, bn), lambda i, j, kk: (j // per, kk, j % per))
    else:
        first = 0 if b_cols is None else b_cols[0] // bn
        assert b_cols is None or (not tb and b_cols[0] % bn == 0)
        b_spec = pl.BlockSpec((bn, bk), lambda i, j, kk: (j, kk)) if tb else pl.BlockSpec((bk, bn), lambda i, j, kk: (kk, first + j))
    t_spec = pl.BlockSpec((bm, bn), lambda i, j, kk: (i, j))
    o_spec, o_shape = t_spec, (m, n)
    if out_chips:
        per_o = n_cap // bn
        o_spec, o_shape = pl.BlockSpec((None, bm, bn), lambda i, j, kk: (j // per_o, i, j % per_o)), (out_chips, m, n_cap)
    est = (_nbytes((bm, bk), a.dtype) + _nbytes((bk, bn), b.dtype) + sum(_nbytes((bm, bn), e.dtype) for e in extra)
           + sum(_nbytes((bm, bn), o) for o in outs)) * 2 + 2 * _nbytes((bm, bn), F32)
    res = pl.pallas_call(
        body, grid=(m // bm, n // bn, nk), in_specs=[a_spec, b_spec] + [t_spec] * n_e + [pl.BlockSpec(memory_space=pl.ANY)] * len(behind),
        out_specs=[o_spec] * n_o, out_shape=[_SDS(o_shape, o) for o in outs], scratch_shapes=[pltpu.VMEM((bm, bn), F32)], name=name,
        compiler_params=_params(("parallel", "parallel", "arbitrary"), est))(_hbm(a), _hbm(b), *[_hbm(e) for e in extra], *behind)
    return res[0] if n_o == 1 else res


def _add_to(acc, r):
    return (acc + r,)


def _alibi_bias(dilation):
    slopes = 2.0 ** (-8.0 * (np.arange(_HEADS) + 1) / _HEADS)
    i = np.arange(_ABLK)[:, None]
    j = np.arange(_ABLK)[None, :]
    cur = np.where(i - j >= 0, -slopes[:, None, None] * ((i - j) * dilation), _NEG)
    prev = np.where(j >= i, -slopes[:, None, None] * ((i - j + _ABLK) * dilation), _NEG)
    both = np.stack([np.concatenate([np.full_like(prev, _NEG), cur], axis=2), np.concatenate([prev, cur], axis=2)])
    return jnp.asarray(both.reshape(2, _HEADS // 2, 2 * _ABLK, 2 * _ABLK), F32)


def _bias_spec():
    return pl.BlockSpec((None, _HEADS // 2, 2 * _ABLK, 2 * _ABLK), lambda r, j: (jnp.minimum(j, 1), 0, 0, 0))


def _strided(a, d):
    return a.reshape(a.shape[0] // d, d * a.shape[1])


def _pair(pr):
    return slice(pr * _LANES, (pr + 1) * _LANES)


def _low_lanes(shape):
    return lax.broadcasted_iota(jnp.int32, shape, 1) < _HDIM


def _halves(v, low):
    z = jnp.zeros_like(v)
    return jnp.where(low, v, z), jnp.where(low, z, v)


def _lane_spec(nb):
    return pl.BlockSpec((_ABLK, _LANES), lambda r, j: (jnp.minimum(j, nb - 1), r))


def _expand_heads(v):
    low = _low_lanes((v.shape[0], _LANES))
    return jnp.concatenate([jnp.where(low, v[:, 2 * pr:2 * pr + 1], v[:, 2 * pr + 1:2 * pr + 2]) for pr in range(_HEADS // 2)], axis=1)


def _attn_specs(nb, n_parts):
    def cur(p):
        return pl.BlockSpec((_ABLK, _AW), lambda r, j: (jnp.minimum(j, nb - 1), r * n_parts + p))

    def prev(p):
        return pl.BlockSpec((_ABLK, _AW), lambda r, j: (jnp.clip(j - 1, 0, nb - 1), r * n_parts + p))
    return cur, prev


def _attn_fwd(qkv, dilation, *, name):
    t = qkv.shape[0] * dilation
    nb = t // dilation // _ABLK
    bias = _alibi_bias(dilation)
    scale = _HDIM ** -0.5

    def body(q_ref, kc_ref, kp_ref, vc_ref, vp_ref, b_ref, o_ref, l_ref):
        low = _low_lanes((_ABLK, _LANES))
        l_ref[...] = jnp.zeros_like(l_ref)
        for pr in range(_HEADS // 2):
            sl = _pair(pr)
            k2 = jnp.concatenate([kp_ref[:, sl], kc_ref[:, sl]], axis=0)
            v2 = jnp.concatenate([vp_ref[:, sl], vc_ref[:, sl]], axis=0)
            q2 = jnp.concatenate(_halves(q_ref[:, sl] * scale, low), axis=0)
            s = _dot(q2, k2, _NT) + b_ref[pr]
            m = jnp.max(s, axis=-1, keepdims=True)
            p = jnp.exp(s - m)
            den = jnp.sum(p, axis=-1, keepdims=True)
            o = _dot(p, v2) / den
            lse = m + jnp.log(den)
            l_ref[:, 2 * pr:2 * pr + 1] = lse[:_ABLK]
            l_ref[:, 2 * pr + 1:2 * pr + 2] = lse[_ABLK:]
            o_ref[:, sl] = jnp.where(low, o[:_ABLK], o[_ABLK:]).astype(o_ref.dtype)

    cur, prev = _attn_specs(nb, 3)
    cur1, _ = _attn_specs(nb, 1)
    bspec = _bias_spec()
    sv = _hbm(qkv)
    o, l = pl.pallas_call(
        body, grid=(dilation, nb), in_specs=[cur(0), cur(1), prev(1), cur(2), prev(2), bspec],
        out_specs=[cur1(0), _lane_spec(nb)],
        out_shape=[_SDS((t // dilation, dilation * _AW), _BF), _SDS((t // dilation, dilation * _LANES), F32)], name=name,
        compiler_params=_params(("parallel", "arbitrary"), 16 << 20))(sv, sv, sv, sv, sv, bias)
    return o, l.reshape(t, _LANES)


def _attn_bwd(qkv, do, ld, dilation, *, name):
    t = qkv.shape[0] * dilation
    nb = t // dilation // _ABLK
    bias = _alibi_bias(dilation)
    scale = _HDIM ** -0.5

    def body(q_ref, kc_ref, kp_ref, vc_ref, vp_ref, do_ref, ld_ref, b_ref, dq_ref, dk_ref, dv_ref, ck, cv):
        n = pl.program_id(1)

        @pl.when(n == 0)
        def _():
            ck[...] = jnp.zeros_like(ck)
            cv[...] = jnp.zeros_like(cv)

        @pl.when(n < nb)
        def _():
            low = _low_lanes((_ABLK, _LANES))
            for pr in range(_HEADS // 2):
                sl = _pair(pr)
                k2 = jnp.concatenate([kp_ref[:, sl], kc_ref[:, sl]], axis=0)
                v2 = jnp.concatenate([vp_ref[:, sl], vc_ref[:, sl]], axis=0)
                q2 = jnp.concatenate(_halves(q_ref[:, sl] * scale, low), axis=0)
                do2 = jnp.concatenate(_halves(do_ref[:, sl], low), axis=0)
                lrow = jnp.concatenate([ld_ref[:, 2 * pr:2 * pr + 1], ld_ref[:, 2 * pr + 1:2 * pr + 2]], axis=0)
                dsum = jnp.concatenate([ld_ref[:, _HEADS + 2 * pr:_HEADS + 2 * pr + 1],
                                        ld_ref[:, _HEADS + 2 * pr + 1:_HEADS + 2 * pr + 2]], axis=0)
                p = jnp.exp(_dot(q2, k2, _NT) + b_ref[pr] - lrow)
                ds = (p * (_dot(do2, v2, _NT) - dsum)).astype(_BF)
                dq = _dot(ds, k2)
                dk2, dv2 = _dot(ds, q2, _TN), _dot(p, do2, _TN)
                dq_ref[:, sl] = (jnp.where(low, dq[:_ABLK], dq[_ABLK:]) * scale).astype(dq_ref.dtype)
                dk_ref[:, sl] = (ck[:, sl] + dk2[:_ABLK]).astype(dk_ref.dtype)
                dv_ref[:, sl] = (cv[:, sl] + dv2[:_ABLK]).astype(dv_ref.dtype)
                ck[:, sl] = dk2[_ABLK:]
                cv[:, sl] = dv2[_ABLK:]

        @pl.when(n == nb)
        def _():
            dk_ref[...] = ck[...].astype(dk_ref.dtype)
            dv_ref[...] = cv[...].astype(dv_ref.dtype)

    cur, prev = _attn_specs(nb, 3)
    cur1, prev1 = _attn_specs(nb, 1)
    bspec = _bias_spec()
    sv, dov, ldv = _hbm(qkv), _hbm(do), _hbm(_strided(ld, dilation))
    dqkv = pl.pallas_call(
        body, grid=(dilation, nb + 1),
        in_specs=[cur(0), cur(1), prev(1), cur(2), prev(2), cur1(0), _lane_spec(nb), bspec],
        out_specs=[cur1(0), prev1(0), prev1(0)], out_shape=[_SDS(dov.shape, _BF)] * 3, name=name,
        scratch_shapes=[pltpu.VMEM((_ABLK, _AW), F32)] * 2,
        compiler_params=_params(("parallel", "arbitrary"), 16 << 20))(sv, sv, sv, sv, sv, dov, ldv, bias)
    return dqkv


def _ssd_in_specs(ch):
    return dict(
        xs=pl.BlockSpec((_CHUNK, _AW), lambda c: (ch(c), 0)),
        bc=pl.BlockSpec((_CHUNK, 2 * _GROUPS * _NSTATE), lambda c: (ch(c), _AW // (2 * _GROUPS * _NSTATE))),
        lane=pl.BlockSpec((_CHUNK, _LANES), lambda c: (ch(c), 0)),
        arow=pl.BlockSpec((_HEADS, 1, _CHUNK), lambda c: (0, 0, ch(c))),
        st=pl.BlockSpec((1, _HEADS // 2, _NSTATE, _LANES), lambda c: (ch(c), 0, 0, 0)),
    )


def _decay(a_col, a_row):
    i0 = lax.broadcasted_iota(jnp.int32, (_CHUNK, _CHUNK), 0)
    i1 = lax.broadcasted_iota(jnp.int32, (_CHUNK, _CHUNK), 1)
    return jnp.where(i0 >= i1, jnp.exp(a_col - a_row), 0.0), jnp.where(i1 >= i0, jnp.exp(a_row - a_col), 0.0)


def _rsum(v):
    return jnp.sum(v, axis=-1, keepdims=True)


def _ssd_fwd(act, dt, acum, a_row, *, name):
    t = act.shape[0]
    nc = t // _CHUNK
    sp = _ssd_in_specs(lambda c: c)
    gw = _GROUPS * _NSTATE

    def body(xs_ref, bc_ref, dt_ref, ac_ref, ar_ref, y_ref, sall_ref, st):
        @pl.when(pl.program_id(0) == 0)
        def _():
            st[...] = jnp.zeros_like(st)

        low = _low_lanes((_CHUNK, _LANES))
        for g in range(_GROUPS):
            bg = bc_ref[:, g * _NSTATE:(g + 1) * _NSTATE]
            cg = bc_ref[:, gw + g * _NSTATE:gw + (g + 1) * _NSTATE].astype(_BF)
            cb = _dot(cg, bg, _NT)
            for pr in range(g * _HPG // 2, (g + 1) * _HPG // 2):
                ha, hb = 2 * pr, 2 * pr + 1
                a_a, a_b = ac_ref[:, ha:ha + 1], ac_ref[:, hb:hb + 1]
                x = (xs_ref[:, _pair(pr)] * jnp.where(low, dt_ref[:, ha:ha + 1], dt_ref[:, hb:hb + 1])).astype(_BF)
                lm_a, _ = _decay(a_a, ar_ref[ha])
                lm_b, _ = _decay(a_b, ar_ref[hb])
                sv = st[pr]
                sall_ref[0, pr] = sv
                yd = _dot(jnp.concatenate([cb * lm_a, cb * lm_b], axis=0), x)
                yd = jnp.where(low, yd[:_CHUNK], yd[_CHUNK:])
                y_ref[:, _pair(pr)] = yd + jnp.where(low, jnp.exp(a_a), jnp.exp(a_b)) * _dot(cg, sv)
                al_a, al_b = jnp.min(a_a, axis=0, keepdims=True), jnp.min(a_b, axis=0, keepdims=True)
                upd = _dot(jnp.concatenate([bg * jnp.exp(al_a - a_a), bg * jnp.exp(al_b - a_b)], axis=1), x, _TN)
                st[pr] = jnp.where(low, jnp.exp(al_a), jnp.exp(al_b)) * sv + jnp.where(low, upd[:_NSTATE], upd[_NSTATE:])

    return pl.pallas_call(
        body, grid=(nc,), in_specs=[sp['xs'], sp['bc'], sp['lane'], sp['lane'], sp['arow']],
        out_specs=[sp['xs'], sp['st']], out_shape=[_SDS((t, _AW), F32), _SDS((nc, _HEADS // 2, _NSTATE, _LANES), F32)],
        scratch_shapes=[pltpu.VMEM((_HEADS // 2, _NSTATE, _LANES), F32)], name=name,
        compiler_params=_params(("arbitrary",), 16 << 20))(*[_hbm(a) for a in (act, act, dt, acum, a_row)])


def _ssd_bwd(act, dt, acum, a_row, sall, dy, *, name):
    t = act.shape[0]
    nc = t // _CHUNK
    sp = _ssd_in_specs(lambda c: nc - 1 - c)
    gw = _GROUPS * _NSTATE

    def body(xs_ref, bc_ref, dt_ref, ac_ref, ar_ref, sall_ref, dy_ref, dxs_ref, dbc_ref, ddt_ref, da_ref, dst):
        @pl.when(pl.program_id(0) == 0)
        def _():
            dst[...] = jnp.zeros_like(dst)

        ddt_ref[...] = jnp.zeros_like(ddt_ref)
        da_ref[...] = jnp.zeros_like(da_ref)
        row = lax.broadcasted_iota(jnp.int32, (_CHUNK, 1), 0)
        low = _low_lanes((_CHUNK, _LANES))
        for g in range(_GROUPS):
            bg = bc_ref[:, g * _NSTATE:(g + 1) * _NSTATE]
            bgb = bg.astype(_BF)
            cg = bc_ref[:, gw + g * _NSTATE:gw + (g + 1) * _NSTATE].astype(_BF)
            cb, cbt = _dot(cg, bgb, _NT), _dot(bgb, cg, _NT)
            dcb = jnp.zeros((_CHUNK, _CHUNK), F32)
            dbg = jnp.zeros((_CHUNK, _NSTATE), F32)
            dcg = jnp.zeros((_CHUNK, _NSTATE), F32)
            for pr in range(g * _HPG // 2, (g + 1) * _HPG // 2):
                heads = (2 * pr, 2 * pr + 1)
                a_cols = [ac_ref[:, h:h + 1] for h in heads]
                dt_pair = jnp.where(low, dt_ref[:, heads[0]:heads[0] + 1], dt_ref[:, heads[1]:heads[1] + 1])
                xsv = xs_ref[:, _pair(pr)]
                x = xsv * dt_pair
                xb = x.astype(_BF)
                xhs = _halves(xb, low)
                dyv = dy_ref[:, _pair(pr)]
                dyb = dyv.astype(_BF)
                dyhs = _halves(dyb, low)
                sv, dsv = sall_ref[0, pr], dst[pr]
                svb, dsb = sv.astype(_BF), dsv.astype(_BF)
                a_lasts = [jnp.min(a, axis=0, keepdims=True) for a in a_cols]
                e_pair = jnp.where(low, jnp.exp(a_cols[0]), jnp.exp(a_cols[1]))
                el_pair = jnp.where(low, jnp.exp(a_lasts[0]), jnp.exp(a_lasts[1]))
                yo = e_pair * _dot(cg, svb)
                decays = [_decay(a_cols[i], ar_ref[h]) for i, h in enumerate(heads)]
                gms, gmts = [cb * lm for lm, _ in decays], [cbt * lmt for _, lmt in decays]
                w_cols = [jnp.exp(a_lasts[i] - a_cols[i]) for i in range(2)]
                x2, dy2 = jnp.concatenate(xhs, axis=0), jnp.concatenate(dyhs, axis=0)
                bwd = _dot(jnp.concatenate([bg * w_cols[0], bg * w_cols[1]], axis=0), dsb)
                dxg = _dot(jnp.concatenate(gms, axis=1), dyb, _TN)
                dg2, dgt2, xds2 = _dot(dy2, xb, _NT), _dot(x2, dyb, _NT), _dot(x2, dsb, _NT)
                das = []
                for i in range(2):
                    rows_i = slice(i * _CHUNK, (i + 1) * _CHUNK)
                    dcb = dcb + dg2[rows_i] * decays[i][0]
                    dbg = dbg + w_cols[i] * xds2[rows_i]
                    das.append(_rsum(dg2[rows_i] * gms[i]) - _rsum(dgt2[rows_i] * gmts[i]))
                bwd = jnp.where(low, bwd[:_CHUNK], bwd[_CHUNK:])
                dx = jnp.where(low, dxg[:_CHUNK], dxg[_CHUNK:]) + bwd
                edy = (e_pair * dyv).astype(_BF)
                dcg = dcg + _dot(edy, svb, _NT)
                zs, yos, sds, dts = (_halves(v, low) for v in (x * bwd, dyv * yo, sv * dsv, dx * xsv))
                for i, h in enumerate(heads):
                    z = _rsum(zs[i])
                    da_last = jnp.sum(z, axis=0, keepdims=True) + jnp.exp(a_lasts[i]) * jnp.sum(_rsum(sds[i]), axis=0, keepdims=True)
                    da_ref[:, h:h + 1] = das[i] + _rsum(yos[i]) - z + jnp.where(row == _CHUNK - 1, da_last, 0.0)
                    ddt_ref[:, h:h + 1] = _rsum(dts[i])
                dxs_ref[:, _pair(pr)] = dx * dt_pair
                dst[pr] = el_pair * dsv + _dot(cg, edy, _TN)
            dbc_ref[:, g * _NSTATE:(g + 1) * _NSTATE] = dbg + _dot(dcb, cg, _TN)
            dbc_ref[:, gw + g * _NSTATE:gw + (g + 1) * _NSTATE] = dcg + _dot(dcb, bgb)

    ch = lambda c: nc - 1 - c
    wide = pl.BlockSpec((_CHUNK, 2 * gw), lambda c: (ch(c), 0))
    return pl.pallas_call(
        body, grid=(nc,), in_specs=[sp['xs'], sp['bc'], sp['lane'], sp['lane'], sp['arow'], sp['st'], sp['xs']],
        out_specs=[sp['xs'], wide, sp['lane'], sp['lane']],
        out_shape=[_SDS((t, _AW), F32), _SDS((t, 2 * gw), F32), _SDS((t, _LANES), F32), _SDS((t, _LANES), F32)],
        scratch_shapes=[pltpu.VMEM((_HEADS // 2, _NSTATE, _LANES), F32)], name=name,
        compiler_params=_params(("arbitrary",), 16 << 20))(*[_hbm(a) for a in (act, act, dt, acum, a_row, sall, dy)])


def _scan_rows(v, reverse):
    r = lax.broadcasted_iota(jnp.int32, v.shape, 0)
    for s in (1, 2, 4, 8, 16, 32, 64):
        if reverse:
            v = v + jnp.where(r < _CHUNK - s, pltpu.roll(v, _CHUNK - s, 0), 0.0)
        else:
            v = v + jnp.where(r >= s, pltpu.roll(v, s, 0), 0.0)
    return v


def _softplus(x):
    return jnp.maximum(x, 0.0) + jnp.log(1.0 + jnp.exp(-jnp.abs(x)))


def _sigmoid(x):
    return 1.0 / (1.0 + jnp.exp(-x))


def _silu(x):
    return x * _sigmoid(x)


def _dsilu(x):
    s = _sigmoid(x)
    return s * (1.0 + x * (1.0 - s))


def _lanes(a):
    return jnp.pad(a, (0, _LANES - a.shape[0])).reshape(1, _LANES)


def _layer_fwd(x, p, l):
    cch = p['conv_w'].shape[1]
    sv = {}
    h1 = _rms_fwd(x, p['ln1_g'], name=f"ln1_fwd_{l}")
    qkv = _mm(h1, p['w_in'], b_cols=(0, 3 * _AW), outs=(_BF,), name=f"in_proj_qkv_{l}")
    xbc = _mm(h1, p['w_in'], b_cols=(3 * _AW, cch), name=f"in_proj_xbc_{l}")
    zdt = _mm(h1, p['w_in'], b_cols=(3 * _AW + cch, _AW + _LANES), name=f"in_proj_zdt_{l}")
    z, dt_raw = (zdt, _AW, 0), (zdt, _LANES, _AW // _LANES)

    tile = 2 * _ABLK
    perms = [_perm(d, tile) for d in _DILATIONS[1:]]
    views = [qkv] + list(_rows(lambda a, p2, p3: (_stride(a, p2, _DILATIONS[1]), _stride(a, p3, _DILATIONS[2])), [qkv], perms,
                               [(3 * _AW, _BF, d) for d in _DILATIONS[1:]], tile=tile, name=f"qkv_strided_{l}"))
    outs = []
    for dil, view in zip(_DILATIONS, views):
        outs += _attn_fwd(view, dil, name=f"attn_fwd_d{dil}_{l}")

    def combine(o1, l1, o2, l2, o3, l3, p2, p3):
        m = jnp.maximum(jnp.maximum(l1, l2), l3)
        e1, e2, e3 = jnp.exp(l1 - m), jnp.exp(l2 - m), jnp.exp(l3 - m)
        tot = e1 + e2 + e3
        mixed = sum(_expand_heads(e / tot) * o for e, o in ((e1, o1.astype(F32)), (e2, _unstride(o2, p2)), (e3, _unstride(o3, p3))))
        return mixed, m + jnp.log(tot)
    outs = [a if i % 2 or i == 0 else ("strided", a, _DILATIONS[i // 2]) for i, a in enumerate(outs)]
    attn, lse = _rows(combine, outs, perms, [(_AW, F32), (_LANES, F32)], tile=tile, name=f"attn_combine_{l}")
    mix = _rms_fwd(attn, p['attn_norm_g'], into=(None, 2 * _AW, 0), name=f"attn_norm_fwd_{l}")

    def conv(u0, before, w, b):
        u1, u2, u3 = _shifted(u0, before, True)
        return _silu(w[0:1] * u3 + w[1:2] * u2 + w[2:3] * u1 + w[3:4] * u0 + b)
    act = _rows(conv, [xbc], [p['conv_w'], p['conv_b'].reshape(1, cch)], [(cch, F32)], halos=[(xbc, -1)], tile=_tile_for(cch),
                name=f"conv_fwd_{l}")[0]

    def dtf(raw, bias, alog):
        dt = _softplus(raw + bias)
        return dt, _scan_rows(dt * -jnp.exp(alog), False)
    dt, acum = _rows(dtf, [dt_raw], [_lanes(p['dt_bias']), _lanes(p['a_log'])], [(_LANES, F32), (_LANES, F32)],
                     tile=_CHUNK, name=f"dt_fwd_{l}")
    a_row = acum[:, :_HEADS].T[:, None, :]
    y_ssd, sall = _ssd_fwd(act, dt, acum, a_row, name=f"ssd_fwd_{l}")
    dskip = jnp.repeat(p['d_skip'], _HDIM).reshape(1, _AW)
    xs = (act, _AW, 0)

    def gate(y, xs, z, dsk):
        return (y + dsk * xs) * _silu(z)
    y2 = _rows(gate, [y_ssd, xs, z], [dskip], [(_AW, F32)], tile=_tile_for(_AW), name=f"gate_fwd_{l}")[0]
    mix = _rms_fwd(y2, p['ssd_norm_g'], groups=_GROUPS, into=(mix, 2 * _AW, 1), name=f"ssd_norm_fwd_{l}")
    sv.update(x=x, h1=h1, qkv=views, zdt=zdt, xbc=xbc, attn=attn, lse=lse, act=act, dt=dt, acum=acum, a_row=a_row,
              sall=sall, y_ssd=y_ssd, dskip=dskip, y2=y2, mix=mix)
    return mix, sv


def _layer_fwd_mlp(p, sv, l):
    x2 = _mm(sv['mix'], p['w_out'], extra=(sv['x'],), epi=_add_to, name=f"out_proj_{l}")
    h2 = _rms_fwd(x2, p['ln2_g'], name=f"ln2_fwd_{l}")
    a = _mm(h2, p['w_mlp_in'], b_chips=_CHIPS, epi=lambda acc: (jnp.square(jnp.maximum(acc, 0.0)),), outs=(_BF,), name=f"mlp_in_{l}")
    x3 = _mm(a, p['w_mlp_out'], extra=(x2,), epi=_add_to, name=f"mlp_out_{l}")
    sv.update(x2=x2, h2=h2, a=a)
    return x3


def _layer_bwd(dx3, dx3b, p, sv, l, send, after):
    cch = p['conv_w'].shape[1]
    g = {}
    du = _mm(dx3b, p['w_mlp_out'], tb=True, extra=(sv['a'],), outs=(_BF,), after=after,
             epi=lambda acc, a: (acc * 2.0 * jnp.sqrt(a.astype(F32)),), name=f"mlp_out_dx_{l}")
    g['w_mlp_out'] = _mm(sv['a'], dx3b, ta=True, outs=(_BF,), name=f"mlp_out_dw_{l}")
    g['w_mlp_in'] = _mm(sv['h2'], du, ta=True, out_chips=_CHIPS, outs=(_BF,), name=f"mlp_in_dw_{l}")
    sent = send(('w_mlp_out', 'w_mlp_in'), g)
    dh2 = _mm(du, p['w_mlp_in'], tb=True, b_chips=_CHIPS, after=sent, name=f"mlp_in_dx_{l}")
    dx2, dx2b, g['ln2_g'] = _rms_bwd(sv['x2'], dh2, p['ln2_g'], dx3, name=f"ln2_bwd_{l}")
    dmix = _mm(dx2b, p['w_out'], tb=True, name=f"out_proj_dx_{l}")
    g['w_out'] = _mm(sv['mix'], dx2b, ta=True, outs=(_BF,), name=f"out_proj_dw_{l}")
    after_out = send(('w_out',), g)

    tile = 2 * _ABLK
    perms = [_perm(d, tile) for d in _DILATIONS[1:]]

    def norm_bwd(attn, dy, lse, gn, p2, p3):
        dattn, dgn = _rms_bwd_tile(attn, dy, gn, 1)
        prod, low = dattn * attn, _low_lanes((attn.shape[0], _LANES))
        lane = lax.broadcasted_iota(jnp.int32, lse.shape, 1)
        ld = jnp.where(lane < _HEADS, lse, 0.0)
        for pr in range(_HEADS // 2):
            for i, part in enumerate(_halves(prod[:, _pair(pr)], low)):
                ld = jnp.where(lane == _HEADS + 2 * pr + i, _rsum(part), ld)
        return dattn, _stride(dattn, p2, _DILATIONS[1]), _stride(dattn, p3, _DILATIONS[2]), ld, dgn
    *dos, ld, gn_sum = _rows(norm_bwd, [sv['attn'], (dmix, _AW, 0), sv['lse']], [p['attn_norm_g'].reshape(1, _AW)] + perms,
                             [(_AW, _BF)] + [(_AW, _BF, d) for d in _DILATIONS[1:]] + [(_LANES, F32)], [_AW], after=after_out,
                             tile=tile, name=f"attn_norm_bwd_{l}")
    g['attn_norm_g'] = gn_sum.sum(axis=0)
    parts = [_attn_bwd(view, do, ld, dil, name=f"attn_bwd_d{dil}_{l}") for view, do, dil in zip(sv['qkv'], dos, _DILATIONS)]

    def branch_sum(*t):
        parts_, (p2, p3) = t[:9], t[9:]
        t = [a.astype(F32) for a in parts_[:3]] + [_unstride(a, p2) for a in parts_[3:6]] + [_unstride(a, p3) for a in parts_[6:]]
        return jnp.concatenate([t[i] + t[3 + i] + t[6 + i] for i in range(3)], axis=1)
    branch_ins = list(parts[0]) + [("strided", a, d) for pr, d in zip(parts[1:], _DILATIONS[1:]) for a in pr]
    w_all = 3 * _AW + cch + _AW + _LANES
    dproj = _rows(branch_sum, branch_ins, perms, [(3 * _AW, _BF)], into=(None, w_all, 0), tile=tile, name=f"attn_bwd_sum_{l}")[0]

    xs, z, dt_raw = (sv['act'], _AW, 0), (sv['zdt'], _AW, 0), (sv['zdt'], _LANES, _AW // _LANES)

    def gate_bwd(y2, dy, y, xs, z, dsk, gn):
        dy2, dgn = _rms_bwd_tile(y2, dy, gn, _GROUPS)
        dy1 = dy2 * _silu(z)
        return dy1, dsk * dy1, dy2 * (y + dsk * xs) * _dsilu(z), dy1 * xs, dgn
    dy1, dxs_skip, dz, dsk_sum, gn_sum = _rows(
        gate_bwd, [sv['y2'], (dmix, _AW, 1), sv['y_ssd'], xs, z], [sv['dskip'], p['ssd_norm_g'].reshape(1, _AW)],
        [(_AW, F32), (_AW, F32), (_AW, _BF)], [_AW, _AW], tile=128, name=f"gate_bwd_{l}")
    g['ssd_norm_g'] = gn_sum.sum(axis=0)
    g['d_skip'] = dsk_sum.sum(axis=0).reshape(_HEADS, _HDIM).sum(axis=1)
    dxs, dbc, ddt, da = _ssd_bwd(sv['act'], sv['dt'], sv['acum'], sv['a_row'], sv['sall'], dy1, name=f"ssd_bwd_{l}")

    def dtb(da, ddtx, raw, dt, dz, bias, alog):
        a = -jnp.exp(alog)
        dda = _scan_rows(da, True)
        draw = (dda * a + ddtx) * _sigmoid(raw + bias)
        return jnp.concatenate([dz, draw.astype(dz.dtype)], axis=1), draw, dda * dt * a
    dproj, dbias, dalog = _rows(dtb, [da, ddt, dt_raw, sv['dt'], dz], [_lanes(p['dt_bias']), _lanes(p['a_log'])],
                                [(_AW + _LANES, _BF)], [_LANES, _LANES], into=(dproj, w_all, (3 * _AW + cch) // (_AW + _LANES)),
                                tile=_CHUNK, name=f"dt_bwd_{l}")
    g['dt_bias'], g['a_log'] = dbias.sum(axis=0)[:_HEADS], dalog.sum(axis=0)[:_HEADS]
    def conv_bwd1(u0, dxs, dbc, dxk, before, w, b):
        u1, u2, u3 = _shifted(u0, before, True)
        pre = w[0:1] * u3 + w[1:2] * u2 + w[2:3] * u1 + w[3:4] * u0 + b
        dp = jnp.concatenate([dxs + dxk, dbc], axis=1) * _dsilu(pre)
        return dp, dp * u3, dp * u2, dp * u1, dp * u0, dp
    dpre, *dws = _rows(conv_bwd1, [sv['xbc'], dxs, dbc, dxs_skip], [p['conv_w'], p['conv_b'].reshape(1, cch)], [(cch, F32)],
                       [cch] * 5, halos=[(sv['xbc'], -1)], tile=128, name=f"conv_bwd_pre_{l}")
    g['conv_w'] = jnp.stack([dws[i].sum(axis=0) for i in range(_CONV_K)])
    g['conv_b'] = dws[4].sum(axis=0)

    def conv_bwd2(p0, after_, w):
        p1, p2, p3 = _shifted(p0, after_, False)
        return w[3:4] * p0 + w[2:3] * p1 + w[1:2] * p2 + w[0:1] * p3
    dproj = _rows(conv_bwd2, [dpre], [p['conv_w']], [(cch, _BF)], halos=[(dpre, 1)], into=(dproj, w_all, 3 * _AW // cch),
                  tile=_tile_for(cch), name=f"conv_bwd_in_{l}")[0]
    g_all = _mm(sv['h1'], dproj, ta=True, outs=(_BF,), name=f"in_proj_dw_{l}")
    z0 = 3 * _AW + cch
    g['w_in'] = jnp.concatenate([g_all[:, :3 * _AW], g_all[:, z0:z0 + _AW], g_all[:, 3 * _AW:z0], g_all[:, z0 + _AW:z0 + _AW + _HEADS]], axis=1)
    sent = send(('w_in',), g)
    for n in _BIG:
        del g[n]
    dh1 = _mm(dproj, p['w_in'], tb=True, after=sent, name=f"in_proj_dx_{l}")
    dx, dxb, g['ln1_g'] = _rms_bwd(sv['x'], dh1, p['ln1_g'], dx2, name=f"ln1_bwd_{l}")
    return dx, dxb, g


def _loss_bwd(x, g, tgt):
    w = x.shape[1]
    tile = _tile_for(w)

    def fn(x, tgt, g):
        r = _rstd(x)
        xh = x * r
        e = xh * g - tgt
        gd = e * (g / w)
        dx = r * (gd - xh * jnp.mean(gd * xh, axis=-1, keepdims=True))
        rowloss = 0.5 * jnp.mean(e * e, axis=-1, keepdims=True)
        return dx, dx, (e / w) * xh, jnp.broadcast_to(rowloss, (tile, _LANES))
    dx, dxb, dg, ls = _rows(fn, [x, tgt], [g.reshape(1, w)], [(w, F32), (w, _BF)], [w, _LANES], tile=tile, name="loss_head")
    return dx, dxb, dg.sum(axis=0), ls[:, 0].sum()


def _adamw_math(w, g, m, v):
    m2 = _B1 * m + (1.0 - _B1) * g
    v2 = _B2 * v + (1.0 - _B2) * jnp.square(g)
    m_hat = m2 / (1.0 - _B1 ** _STEP)
    v_hat = v2 / (1.0 - _B2 ** _STEP)
    return -_LR * (m_hat / (jnp.sqrt(v_hat) + _AEPS) + _WD * w), m2, v2


def _adamw(w, g, m, v, *, name):
    width = w.shape[-1]
    flat = [a.reshape(-1, width) for a in (w, g, m, v)]
    tile = _pick(flat[0].shape[0], (_tile_for(width), 32, 8))
    res = _rows(_adamw_math, flat, [], [(width, F32)] * 3, tile=tile, name=name)
    return [r.reshape(w.shape) for r in res]


_HBM = pl.BlockSpec(memory_space=pltpu.HBM)


def _place():
    x, y, c = lax.axis_index("x"), lax.axis_index("y"), lax.axis_index("c")
    other_chips = [(1 - x, y), (x, 1 - y), (1 - x, 1 - y)]
    return x, y, c, other_chips


def _remote(src, dst, sems, i, dev):
    return pltpu.make_async_remote_copy(src_ref=src, dst_ref=dst, send_sem=sems[0].at[i], recv_sem=sems[1].at[i],
                                        device_id=dev, device_id_type=_MESH)


def _exchange8(v, *, reduce, after=None, name):
    r, w = v.shape
    behind = [] if after is None else [after]

    def body(v_ref, *rest):
        all_ref, rest = rest[len(behind)], rest[len(behind) + 1:]
        sems = rest[-2:]
        x, y, c, _ = _place()
        me = 4 * x + 2 * y + c
        all_ref[me] = v_ref[...]
        flips = [((d >> 2) & 1, (d >> 1) & 1, d & 1) for d in range(1, 8)]
        sends = [_remote(v_ref, all_ref.at[me], sems, i, (x ^ fx, y ^ fy, c ^ fc)) for i, (fx, fy, fc) in enumerate(flips)]
        for cp in sends:
            cp.start()
        for i, (fx, fy, fc) in enumerate(flips):
            _remote(v_ref, all_ref.at[me ^ (4 * fx + 2 * fy + fc)], sems, i, (x ^ fx, y ^ fy, c ^ fc)).wait_recv()
        for cp in sends:
            cp.wait_send()
        if reduce:
            acc = all_ref[0]
            for s in range(1, 8):
                acc = acc + all_ref[s]
            rest[0][...] = acc

    vm = pl.BlockSpec(memory_space=pltpu.VMEM)
    out_shape = [_SDS((8, r, w), v.dtype)] + ([_SDS((r, w), v.dtype)] if reduce else [])
    res = pl.pallas_call(body, in_specs=[vm] + [_ANY] * len(behind), out_specs=[vm] * len(out_shape), out_shape=out_shape, name=name,
                         scratch_shapes=[pltpu.SemaphoreType.DMA((7,)), pltpu.SemaphoreType.DMA((7,))],
                         compiler_params=pltpu.CompilerParams(vmem_limit_bytes=int(32 << 20)))(v, *behind)
    return res[1] if reduce else res[0]


_SEM = pl.BlockSpec(memory_space=pltpu.SEMAPHORE)
_ANY = pl.BlockSpec(memory_space=pl.ANY)
_EFFECT = pltpu.SideEffectType.DATAFLOW_SIDE_EFFECTING


def _send_start(name, srcs, land_shapes, plan, n_sends, after):
    ns, nl = len(srcs), len(land_shapes)
    zones = [_hbm(lax.empty(s.shape, s.dtype)) if isinstance(s, _SDS) else s for s in land_shapes]

    def body(*refs):
        ins, lands, sems = refs[:ns], refs[ns:ns + nl], refs[ns + nl + 1:ns + nl + 3]
        x, y, c, chips = _place()
        for i, (s, d, dev) in enumerate(plan(x, y, c, chips, ins, lands)[0]):
            _remote(s, d, sems, i, dev).start()
        refs[-1][...] = jnp.zeros_like(refs[-1])

    sem = pltpu.SemaphoreType.DMA((n_sends,))
    res = pl.pallas_call(
        body, name=name, in_specs=[_HBM] * (ns + nl) + [_ANY],
        out_shape=(sem, sem, *[pltpu.HBM(s.shape, s.dtype) for s in land_shapes], _SDS((8, _LANES), F32)),
        out_specs=(_SEM, _SEM, *[_HBM] * nl, pl.BlockSpec(memory_space=pltpu.VMEM)),
        input_output_aliases={ns + i: 2 + i for i in range(nl)},
        compiler_params=pltpu.CompilerParams(has_side_effects=_EFFECT))(
            *[_hbm(s) for s in srcs], *zones, after)
    return dict(sems=res[:2], srcs=srcs, lands=res[2:2 + nl], plan=plan), res[-1]


def _send_wait(name, h, after):
    ns, nl = len(h['srcs']), len(h['lands'])

    def body(*refs):
        ins, lands, sems = refs[:ns], refs[ns:ns + nl], refs[ns + nl:ns + nl + 2]
        x, y, c, chips = _place()
        sends, landings = h['plan'](x, y, c, chips, ins, lands)
        for i, (s, d, dev) in enumerate(sends):
            _remote(s, d, sems, i, dev).wait_send()
        for i, d in enumerate(landings):
            _remote(d, d, sems, i, sends[i][2]).wait_recv()

    return pl.pallas_call(
        body, name=name, in_specs=[_HBM] * (ns + nl) + [_SEM, _SEM, _ANY],
        out_shape=tuple(pltpu.HBM(a.shape, a.dtype) for a in h['lands']), out_specs=tuple([_HBM] * nl),
        input_output_aliases={ns + i: i for i in range(nl)},
        compiler_params=pltpu.CompilerParams(has_side_effects=_EFFECT))(
            *[_hbm(s) for s in h['srcs']], *h['lands'], *h['sems'], after)


def _gather_plan(items):
    def plan(x, y, c, chips, ins, lands):
        k = 2 * x + y
        to = [(px, py, c) for px, py in chips] + [(x, y, 1 - c)]
        sends = [(ins[si].at[l], lands[t].at[k], dev) for t, (si, l) in enumerate(items) for dev in to]
        return sends, [lands[t].at[2 * px + py] for t in range(len(items)) for px, py in chips + [(x, y)]]
    return plan


_FLIPS = [((d >> 2) & 1, (d >> 1) & 1, d & 1) for d in range(1, 8)]


def _reduce_plan(halves):
    def plan(x, y, c, chips, ins, lands):
        sends, landings = [], []
        for t, hf in enumerate(halves):
            for i, (fx, fy, fc) in enumerate(_FLIPS):
                px, py, pc = x ^ fx, y ^ fy, c ^ fc
                sends.append((ins[t].at[2 * px + py, pl.ds(pc * hf, hf)], lands[t].at[i], (px, py, pc)))
                landings.append(lands[t].at[i])
        return sends, landings
    return plan


def _swap(name, srcs, out_shapes, plan, n_sends):
    n = len(srcs)

    def body(*refs):
        ins, outs, sems = refs[:n], refs[n:n + len(out_shapes)], refs[-2:]
        x, y, c, chips = _place()
        sends, landings = plan(x, y, c, chips, ins, outs)
        out = [_remote(s, d, sems, i, dev) for i, (s, d, dev) in enumerate(sends)]
        for cp in out:
            cp.start()
        for i, d in enumerate(landings):
            _remote(d, d, sems, i, sends[i][2]).wait_recv()
        for cp in out:
            cp.wait_send()

    return pl.pallas_call(
        body, in_specs=[_HBM] * n, out_specs=[_HBM] * len(out_shapes), out_shape=out_shapes, name=name,
        scratch_shapes=[pltpu.SemaphoreType.DMA((n_sends,)), pltpu.SemaphoreType.DMA((n_sends,))])(*srcs)


def _sum_owned(grads, landed, c, k, names):
    def sum8(*parts):
        acc = parts[0].astype(F32)
        for p in parts[1:]:
            acc = acc + p.astype(F32)
        return acc
    outs = []
    for g, got, name in zip(grads, landed, names):
        hf, b = got.shape[1:]
        own = lax.dynamic_slice_in_dim(lax.dynamic_index_in_dim(g, k, axis=0, keepdims=False), c * hf, hf, axis=0)
        outs.append(_rows(sum8, [own] + [("slot", got, i) for i in range(len(_FLIPS))], [], [(b, F32)],
                          tile=_pick(hf, (_tile_for(b), 32)), name=f"grad_sum_{name}")[0])
    return outs


def _share_halves(mine, *, name):
    n = len(mine)

    def plan(x, y, c_, chips, ins, outs):
        return [(ins[t], outs[t], (x, y, 1 - c_)) for t in range(n)], [outs[t] for t in range(n)]
    return _swap(name, mine, [_SDS(h.shape, F32) for h in mine], plan, n)


def _adamw_owned(w, mine, theirs, m, v, c, *, name):
    depth, a, b = w.shape
    half = a // 2
    tile = _pick(half, (_tile_for(b), 32, 8))
    nh = half // tile

    def blocks_of(l):
        return lambda i: (jnp.clip(i - 2 * nh * l, 0, 2 * nh - 1) % nh, 0)

    def fn(w, m, v, *rest):
        halves, cflag = rest[:-1], rest[-1]
        step = pl.program_id(0)
        is_mine = cflag[0:1, 0:1] == ((step // nh) % 2).astype(F32)
        g = jnp.where(is_mine, halves[0], halves[1])
        for l in range(1, depth):
            g = jnp.where(step >= 2 * nh * l, jnp.where(is_mine, halves[2 * l], halves[2 * l + 1]), g)
        return (g,) + _adamw_math(w, g, m, v)
    ins = [a_.reshape(depth * a, b) for a_ in (w, m, v)]
    ins += [(h, b, blocks_of(l)) for l in range(depth) for h in (mine[l], theirs[l])]
    res = _rows(fn, ins, [jnp.full((1, _LANES), c, F32)], [(b, F32)] * 4, tile=tile, name=name)
    return [r.reshape(w.shape) for r in res]


_BIG = ("w_in", "w_out", "w_mlp_in", "w_mlp_out")
_SMALL = ("ln1_g", "conv_b", "dt_bias", "a_log", "d_skip", "attn_norm_g", "ssd_norm_g", "ln2_g", "final_norm_g")
_ORDER = ("ln1_g", "w_in", "conv_w", "conv_b", "dt_bias", "a_log", "d_skip", "attn_norm_g", "ssd_norm_g", "w_out", "ln2_g",
          "w_mlp_in", "w_mlp_out", "final_norm_g")


def _pack(parts, rows):
    flat = jnp.concatenate([p.reshape(-1) for p in parts])
    return jnp.pad(flat, (0, rows * _LANES - flat.shape[0])).reshape(rows, _LANES)


def _unpack(buf, like):
    flat, out, o = buf.reshape(-1), [], 0
    for p in like:
        out.append(flat[o:o + p.size].reshape(p.shape))
        o += p.size
    return out


def kernel(x, ln1_g, w_in, conv_w, conv_b, dt_bias, a_log, d_skip, attn_norm_g, ssd_norm_g, w_out, ln2_g, w_mlp_in, w_mlp_out, final_norm_g, loss_target, m_ln1_g, m_w_in, m_conv_w, m_conv_b, m_dt_bias, m_a_log, m_d_skip, m_attn_norm_g, m_ssd_norm_g, m_w_out, m_ln2_g, m_w_mlp_in, m_w_mlp_out, m_final_norm_g, v_ln1_g, v_w_in, v_conv_w, v_conv_b, v_dt_bias, v_a_log, v_d_skip, v_attn_norm_g, v_ssd_norm_g, v_w_out, v_ln2_g, v_w_mlp_in, v_w_mlp_out, v_final_norm_g):
    w = dict(ln1_g=ln1_g, w_in=w_in, conv_w=conv_w, conv_b=conv_b, dt_bias=dt_bias, a_log=a_log, d_skip=d_skip,
             attn_norm_g=attn_norm_g, ssd_norm_g=ssd_norm_g, w_out=w_out, ln2_g=ln2_g, w_mlp_in=w_mlp_in, w_mlp_out=w_mlp_out,
             final_norm_g=final_norm_g)
    m = dict(ln1_g=m_ln1_g, w_in=m_w_in, conv_w=m_conv_w, conv_b=m_conv_b, dt_bias=m_dt_bias, a_log=m_a_log, d_skip=m_d_skip,
             attn_norm_g=m_attn_norm_g, ssd_norm_g=m_ssd_norm_g, w_out=m_w_out, ln2_g=m_ln2_g, w_mlp_in=m_w_mlp_in,
             w_mlp_out=m_w_mlp_out, final_norm_g=m_final_norm_g)
    v = dict(ln1_g=v_ln1_g, w_in=v_w_in, conv_w=v_conv_w, conv_b=v_conv_b, dt_bias=v_dt_bias, a_log=v_a_log, d_skip=v_d_skip,
             attn_norm_g=v_attn_norm_g, ssd_norm_g=v_ssd_norm_g, w_out=v_w_out, ln2_g=v_ln2_g, w_mlp_in=v_w_mlp_in,
             w_mlp_out=v_w_mlp_out, final_norm_g=v_final_norm_g)
    depth, d_model = ln1_g.shape
    n_chips = _CHIPS
    c = lax.axis_index("c")
    chip = 2 * lax.axis_index("x") + lax.axis_index("y")
    in_proj = w_in.shape[2] * n_chips
    cch = conv_w.shape[2] * n_chips
    zdt_pad = _LANES - _HEADS

    cw = _exchange8(conv_w.reshape(depth * _CONV_K, -1), reduce=False, name="gather_conv_w")[0::2]
    conv_full = cw.reshape(n_chips, depth, _CONV_K, -1).transpose(1, 2, 0, 3).reshape(depth, _CONV_K, cch)
    own = [w[n].astype(_BF) for n in _BIG]

    def start_gather(tag, items, after):
        lands = [_SDS((n_chips, *own[i].shape[1:]), _BF) for i, _ in items]
        return _send_start(f"gather_start_{tag}", own, lands, _gather_plan(items), n_chips * len(items), after)

    def finish_gather(tag, handle, items, after):
        landed = _send_wait(f"gather_wait_{tag}", handle, after)
        return {_BIG[i]: g for (i, _), g in zip(items, landed)}

    def layer_weights(l, blocks):
        p = {}
        if 'w_in' in blocks:
            full_in = blocks['w_in'].transpose(1, 0, 2).reshape(d_model, in_proj)
            p['w_in'] = jnp.concatenate([full_in[:, :3 * _AW], full_in[:, 4 * _AW:4 * _AW + cch], full_in[:, 3 * _AW:4 * _AW],
                                         full_in[:, 4 * _AW + cch:], jnp.zeros((d_model, zdt_pad), _BF)], axis=1)
        if 'w_out' in blocks:
            p['w_out'] = blocks['w_out'].reshape(-1, d_model)
            p['w_mlp_in'] = blocks['w_mlp_in']
            p['w_mlp_out'] = blocks['w_mlp_out'].reshape(-1, d_model)
        return p

    groups = dict(a=[(0, 0)], b=[(1, 0), (2, 0), (3, 0)], c=[(0, 1)], d=[(1, 1), (2, 1), (3, 1)])
    handles, token = {}, conv_full

    half_in = own[0].shape[1] // 2

    def rows_of(ref, who):
        return ref.at[pl.ds(who * half_in, half_in)]

    def plan_a(x_, y_, c_, chips, ins, lands):
        k = 2 * x_ + y_
        sends = [(rows_of(ins[0].at[0], c_), rows_of(lands[0].at[k], c_), (px, py, c_)) for px, py in chips]
        sends.append((ins[0].at[0], lands[0].at[k], (x_, y_, 1 - c_)))
        return sends, [rows_of(lands[0].at[2 * px + py], c_) for px, py in chips] + [lands[0].at[k]]

    def plan_pass(x_, y_, c_, chips, ins, lands):
        sends = [(rows_of(lands[0].at[2 * px + py], c_),) * 2 + ((x_, y_, 1 - c_),) for px, py in chips]
        return sends, [rows_of(lands[0].at[2 * px + py], 1 - c_) for px, py in chips]
    handles["a"], token = _send_start("gather_start_a", own[:1], [_SDS((n_chips, *own[0].shape[1:]), _BF)], plan_a, n_chips, token)
    for tag, items in list(groups.items())[1:]:
        handles[tag], token = start_gather(tag, items, token)
    landed = _send_wait("gather_land_a", handles["a"], token)
    handles["a"], token = _send_start("gather_pass_a", [], landed, plan_pass, 3, landed[0])
    layers = [{n: w[n][l] for n in _SMALL[:-1]} for l in range(depth)]
    for l in range(depth):
        layers[l]['conv_w'] = conv_full[l]

    layers[0].update(layer_weights(0, finish_gather("a", handles["a"], groups["a"], token)))
    mix, sv0 = _layer_fwd(x[0], layers[0], 0)
    layers[0].update(layer_weights(0, finish_gather("b", handles["b"], groups["b"], mix)))
    h = _layer_fwd_mlp(layers[0], sv0, 0)
    layers[1].update(layer_weights(1, finish_gather("c", handles["c"], groups["c"], h)))
    mix, sv1 = _layer_fwd(h, layers[1], 1)
    layers[1].update(layer_weights(1, finish_gather("d", handles["d"], groups["d"], mix)))
    h = _layer_fwd_mlp(layers[1], sv1, 1)
    saved = [sv0, sv1]

    def by_chip(g, name):
        if name == "w_mlp_in":
            return g
        if name == "w_in":
            return g.reshape(d_model, n_chips, -1).transpose(1, 0, 2)
        return g.reshape(n_chips, -1, d_model)

    pending = []

    def sender(l):
        def send(names, g):
            srcs = [by_chip(g[n], n) for n in names]
            halves = [s.shape[1] // 2 for s in srcs]
            lands = [_SDS((len(_FLIPS), hf, s.shape[2]), _BF) for s, hf in zip(srcs, halves)]
            handle, tok = _send_start(f"grad_start_{names[-1]}_{l}", srcs, lands, _reduce_plan(halves), len(_FLIPS) * len(srcs), srcs[0])
            pending.append((l, names, srcs, handle))
            return tok
        return send

    dx, dxb, g_final, loss_part = _loss_bwd(h, final_norm_g, loss_target[0])
    grads, after = [None] * depth, None
    for l in reversed(range(depth)):
        dx, dxb, grads[l] = _layer_bwd(dx, dxb, layers[l], saved[l], l, sender(l), after)
        after = dx
    landed_of = {}
    sent_in = {(n, l): (gi, j) for gi, (l, names, _, _) in enumerate(pending) for j, n in enumerate(names)}

    def landed_for(gi, after):
        if gi not in landed_of:
            l, names, _, handle = pending[gi]
            landed_of[gi] = _send_wait(f"grad_wait_{names[-1]}_{l}", handle, after)
        return landed_of[gi]

    red, delta, new_m, new_v = {}, {}, {}, {}
    after = dx
    for n in ("w_mlp_out", "w_mlp_in", "w_out", "w_in"):
        mine = []
        for l in range(depth):
            gi, j = sent_in[(n, l)]
            got = landed_for(gi, after)[j]
            mine.append(_sum_owned([pending[gi][2][j]], [got], c, chip, [f"{n}_{l}"])[0])
        theirs = _share_halves(mine, name=f"grad_share_{n}")
        red[n], delta[n], new_m[n], new_v[n] = _adamw_owned(w[n], mine, theirs, m[n], v[n], c, name=f"adamw_{n}")
        after = delta[n]

    small = {n: jnp.stack([grads[l][n] for l in range(depth)]) for n in _SMALL[:-1] + ("conv_w",)}
    small["final_norm_g"] = g_final
    parts = [loss_part.reshape(1)] + [small[n] for n in _SMALL + ("conv_w",)]
    rows = -(-sum(p.size for p in parts) // 1024) * 8
    tot = _unpack(_exchange8(_pack(parts, rows), reduce=True, after=red[_BIG[0]], name="allreduce_small"), parts)
    loss = tot[0][0]
    red.update(zip(_SMALL + ("conv_w",), tot[1:]))
    red["conv_w"] = lax.dynamic_index_in_dim(red["conv_w"].reshape(depth, _CONV_K, n_chips, -1), chip, axis=2, keepdims=False)

    names = _SMALL + ("conv_w",)
    like = [w[n] for n in names]
    srows = -(-sum(p.size for p in like) // 1024) * 8
    res = _adamw(*[_pack([d[n] for n in names], srows) for d in (w, red, m, v)], name="adamw_small")
    for dst, buf in zip((delta, new_m, new_v), res):
        dst.update(zip(names, _unpack(buf, like)))
    return (loss, dx[None], *[red[n] for n in _ORDER], *[delta[n] for n in _ORDER], *[new_m[n] for n in _ORDER],
            *[new_v[n] for n in _ORDER])
```

```python
import numpy as np
import jax
import jax.numpy as jnp
from jax import lax
from jax.experimental import pallas as pl
from jax.experimental.pallas import tpu as pltpu

F32 = jnp.float32
_BF = jnp.bfloat16
_NEG = -1e30
_EPS = 1e-5
_HEADS = 16
_HDIM = 64
_AW = _HEADS * _HDIM
_ABLK = 128
_DILATIONS = (1, 4, 16)
_CHUNK = 128
_NSTATE = 128
_GROUPS = 2
_HPG = _HEADS // _GROUPS
_CONV_K = 4
_LANES = 128
_CHIPS = 4
_LR, _B1, _B2, _AEPS, _WD, _STEP = 0.001, 0.9, 0.999, 1e-08, 0.01, 10
_VMEM_CAP = 56 * 1024 * 1024
_MESH = pl.DeviceIdType.MESH
_SDS = jax.ShapeDtypeStruct
_NT = (((1,), (1,)), ((), ()))
_TN = (((0,), (0,)), ((), ()))


def _params(sem, est_bytes):
    lim = int(min(max(2 * est_bytes + (4 << 20), 16 << 20), _VMEM_CAP))
    return pltpu.CompilerParams(dimension_semantics=sem, vmem_limit_bytes=lim)


def _nbytes(shape, dtype):
    return int(np.prod(shape)) * jnp.dtype(dtype).itemsize


def _hbm(a):
    return pltpu.with_memory_space_constraint(a, pltpu.HBM)


def _dot(a, b, dims=(((1,), (0,)), ((), ()))):
    return lax.dot_general(a.astype(_BF), b.astype(_BF), dims, preferred_element_type=F32)


_HALO = 8


def _rows(fn, ins, consts, outs, sums=(), *, halos=(), into=None, after=None, tile, name):
    rows = (ins[0][0] if isinstance(ins[0], tuple) else ins[0]).shape[0]
    n_steps = rows // tile

    def norm_in(a):
        if not isinstance(a, tuple):
            return a, tile, a.shape[1], lambda i: (i, 0)
        if isinstance(a[0], str) and a[0] == "slot":
            return a[1], (None, tile, a[1].shape[2]), a[1].shape[2], lambda i, s=a[2]: (s, i, 0)
        if isinstance(a[0], str):
            return a[1], tile // a[2], a[1].shape[1], lambda i: (i, 0)
        return a[0], tile, a[1], a[2] if callable(a[2]) else (lambda i, j=a[2]: (i, j))
    ins = [norm_in(a) for a in ins]
    outs = [(w, dt, d[0] if d else 1) for w, dt, *d in outs]
    n_in, n_h, n_c, n_o, n_s = len(ins), len(halos), len(consts), len(outs), len(sums)
    n_x = int(into is not None and into[0] is not None) + int(after is not None)

    def body(*refs):
        step = pl.program_id(0)
        vals = [r[...] for r in refs[:n_in]]
        for r, (_, side) in zip(refs[n_in:n_in + n_h], halos):
            vals.append(jnp.where(step == (0 if side < 0 else n_steps - 1), 0.0, r[...]))
        vals += [r[...] for r in refs[n_in + n_h:n_in + n_h + n_c]]
        refs = refs[:n_in] + refs[n_in + n_h:]
        res = fn(*vals)
        res = res if isinstance(res, tuple) else (res,)
        orefs = refs[n_in + n_c + n_x:n_in + n_c + n_x + n_o]
        srefs = refs[n_in + n_c + n_x + n_o:]
        for r, v in zip(orefs, res[:n_o]):
            r[...] = v.astype(r.dtype)
        if n_s:
            @pl.when(pl.program_id(0) == 0)
            def _():
                for r in srefs:
                    r[...] = jnp.zeros_like(r)
            for r, v in zip(srefs, res[n_o:]):
                r[...] += v.reshape(tile // 8, 8, v.shape[-1]).sum(axis=0)

    per = tile // _HALO
    in_specs = [pl.BlockSpec(r if isinstance(r, tuple) else (r, w), idx) for _, r, w, idx in ins]
    in_specs += [pl.BlockSpec((_HALO, a.shape[1]), (lambda i: (jnp.maximum(i * per - 1, 0), 0)) if side < 0
                              else (lambda i: (jnp.minimum((i + 1) * per, rows // _HALO - 1), 0))) for a, side in halos]
    in_specs += [pl.BlockSpec(c.shape, lambda i, nd=c.ndim: (0,) * nd) for c in consts]
    out_shape = [_SDS((rows // d, d * w), dt) for w, dt, d in outs] + [_SDS((8, w), F32) for w in sums]
    out_specs = [pl.BlockSpec((tile // d, d * w), lambda i: (i, 0)) for w, _, d in outs]
    out_specs += [pl.BlockSpec((8, w), lambda i: (0, 0)) for w in sums]
    est = (sum(_nbytes((tile if isinstance(r, tuple) else r, w), a.dtype) for a, r, w, _ in ins)
           + sum(_nbytes((tile, w), dt) for w, dt, _ in outs))
    shared, aliases = [], {}
    if into is not None:
        buf, total, j = into
        out_shape[0] = _SDS((rows, total), outs[0][1])
        out_specs[0] = pl.BlockSpec((tile, outs[0][0]), lambda i: (i, j))
        if buf is not None:
            shared, aliases = [buf], {n_in + n_h + n_c: 0}
    if after is not None:
        shared.append(after)
    in_specs += [pl.BlockSpec(memory_space=pl.ANY)] * len(shared)
    return pl.pallas_call(body, grid=(n_steps,), in_specs=in_specs, out_specs=out_specs, out_shape=out_shape, name=name,
                          input_output_aliases=aliases, compiler_params=_params(("arbitrary",), 3 * est))(
                              *[_hbm(a[0]) for a in ins], *[_hbm(a) for a, _ in halos], *consts, *shared)


def _perm(d, tile):
    p = np.zeros((tile, tile), np.float32)
    t = np.arange(tile)
    p[t, (t % d) * (tile // d) + t // d] = 1.0
    return jnp.asarray(p, _BF)


def _unstride(s, p):
    d = p.shape[0] // s.shape[0]
    w = s.shape[1] // d
    return _dot(p, jnp.concatenate([s[:, r * w:(r + 1) * w] for r in range(d)], axis=0))


def _stride(x, p, d):
    z = _dot(p, x, _TN)
    n = x.shape[0] // d
    return jnp.concatenate([z[r * n:(r + 1) * n] for r in range(d)], axis=1)


def _shifted(u, halo, back):
    n = u.shape[0] + _HALO
    if back:
        ext = jnp.concatenate([halo, u], axis=0)
        return [pltpu.roll(ext, j, 0)[_HALO:] for j in (1, 2, 3)]
    ext = jnp.concatenate([u, halo], axis=0)
    return [pltpu.roll(ext, n - j, 0)[:u.shape[0]] for j in (1, 2, 3)]


def _tile_for(width):
    return max(c for c in (256, 128, 64, 32) if c * width <= (1 << 18) or c == 32)


def _rstd(x):
    return lax.rsqrt(jnp.mean(x * x, axis=-1, keepdims=True) + _EPS)


def _split(x, groups):
    w = x.shape[-1] // groups
    return [x[:, g * w:(g + 1) * w] for g in range(groups)]


def _cat(parts):
    return parts[0] if len(parts) == 1 else jnp.concatenate(parts, axis=-1)


def _rms_bwd_tile(x, dy, g, groups):
    dxs, dgs = [], []
    for xs, ds, gs in zip(_split(x, groups), _split(dy.astype(F32), groups), _split(g, groups)):
        r = _rstd(xs)
        xh = xs * r
        gd = ds * gs
        dxs.append(r * (gd - xh * jnp.mean(gd * xh, axis=-1, keepdims=True)))
        dgs.append(ds * xh)
    return _cat(dxs), _cat(dgs)


def _rms_fwd(x, g, *, groups=1, into=None, name):
    def fn(x, g):
        return _cat([xs * _rstd(xs) * gs for xs, gs in zip(_split(x, groups), _split(g, groups))])
    w = x.shape[1]
    return _rows(fn, [x], [g.reshape(1, w)], [(w, _BF)], into=into, tile=_tile_for(w), name=name)[0]


def _rms_bwd(x, dy, g, res, *, name):
    def fn(x, dy, res, g):
        dx, dg = _rms_bwd_tile(x, dy, g, 1)
        return dx + res, dx + res, dg
    w = x.shape[1]
    dx, dxb, dg = _rows(fn, [x, dy, res], [g.reshape(1, w)], [(w, F32), (w, _BF)], [w], tile=_tile_for(w), name=name)
    return dx, dxb, dg.sum(axis=0)


def _pick(n, cands):
    for c in cands:
        if n % c == 0:
            return c
    raise ValueError(f"no block size for {n}")


_MM_BLOCKS = (1024, 1152, 512, 384)


def _mm(a, b, *, ta=False, tb=False, extra=(), epi=None, outs=(F32,), after=None, b_chips=0, out_chips=0, b_cols=None, name):
    m, k = (a.shape[1], a.shape[0]) if ta else a.shape
    b_shape = (b.shape[1], b.shape[2] * b_chips) if b_chips else b.shape
    if b_cols is not None:
        b_shape = (b.shape[0], b_cols[1])
    n = b_shape[0] if tb else b_shape[1]
    assert k == (b_shape[1] if tb else b_shape[0])
    n_cap = n // max(out_chips, 1 if tb else b_chips, 1)
    k_cap = k // (b_chips if (b_chips and tb) else 1)
    bm, bn = _pick(m, _MM_BLOCKS), _pick(n_cap, _MM_BLOCKS)
    bk = _pick(k_cap, (2048, 1920) + _MM_BLOCKS)
    nk = k // bk
    n_e, n_o = len(extra), len(outs)
    behind = [] if after is None else [after]
    dims = (((0 if ta else 1,), (1 if tb else 0,)), ((), ()))

    def body(a_ref, b_ref, *rest):
        ex, orefs, acc = rest[:n_e], rest[n_e + len(behind):n_e + len(behind) + n_o], rest[-1]
        kk = pl.program_id(2)

        @pl.when(kk == 0)
        def _():
            acc[...] = jnp.zeros_like(acc)

        acc[...] += _dot(a_ref[...], b_ref[...], dims)

        @pl.when(kk == nk - 1)
        def _():
            r = acc[...]
            res = epi(r, *[e[...] for e in ex]) if epi is not None else (r,)
            for o, v in zip(orefs, res):
                o[...] = v.astype(o.dtype)

    a_spec = pl.BlockSpec((bk, bm), lambda i, j, kk: (kk, i)) if ta else pl.BlockSpec((bm, bk), lambda i, j, kk: (i, kk))
    if b_chips and tb:
        per = k_cap // bk
        b_spec = pl.BlockSpec((None, bn, bk), lambda i, j, kk: (kk // per, j, kk % per))
    elif b_chips:
        per = n_cap // bn
        b_spec = pl.BlockSpec((None, bk, bn), lambda i, j, kk: (j // per, kk, j % per))
    else:
        first = 0 if b_cols is None else b_cols[0] // bn
        assert b_cols is None or (not tb and b_cols[0] % bn == 0)
        b_spec = pl.BlockSpec((bn, bk), lambda i, j, kk: (j, kk)) if tb else pl.BlockSpec((bk, bn), lambda i, j, kk: (kk, first + j))
    t_spec = pl.BlockSpec((bm, bn), lambda i, j, kk: (i, j))
    o_spec, o_shape = t_spec, (m, n)
    if out_chips:
        per_o = n_cap // bn
        o_spec, o_shape = pl.BlockSpec((None, bm, bn), lambda i, j, kk: (j // per_o, i, j % per_o)), (out_chips, m, n_cap)
    est = (_nbytes((bm, bk), a.dtype) + _nbytes((bk, bn), b.dtype) + sum(_nbytes((bm, bn), e.dtype) for e in extra)
           + sum(_nbytes((bm, bn), o) for o in outs)) * 2 + 2 * _nbytes((bm, bn), F32)
    res = pl.pallas_call(
        body, grid=(m // bm, n // bn, nk), in_specs=[a_spec, b_spec] + [t_spec] * n_e + [pl.BlockSpec(memory_space=pl.ANY)] * len(behind),
        out_specs=[o_spec] * n_o, out_shape=[_SDS(o_shape, o) for o in outs], scratch_shapes=[pltpu.VMEM((bm, bn), F32)], name=name,
        compiler_params=_params(("parallel", "parallel", "arbitrary"), est))(_hbm(a), _hbm(b), *[_hbm(e) for e in extra], *behind)
    return res[0] if n_o == 1 else res


def _w_in_groups(cch):
    return (0, 3 * _AW), (3 * _AW, _AW), (4 * _AW, cch), (4 * _AW + cch, _HEADS)


def _w_in_from_chips(blocks, cch, *, name):
    chips, d, _ = blocks.shape
    qkv, z, xbc, dt = _w_in_groups(cch)
    order = (qkv, xbc, z, dt)
    total = 3 * _AW + cch + _AW + _LANES
    tile = 2 * _ABLK

    def body(b_ref, o_ref):
        full = jnp.concatenate([b_ref[k] for k in range(chips)], axis=1)
        parts = [full[:, s:s + w] for s, w in order]
        o_ref[...] = jnp.concatenate(parts + [jnp.zeros((tile, total - sum(w for _, w in order)), full.dtype)], axis=1)

    return pl.pallas_call(
        body, grid=(d // tile,), in_specs=[pl.BlockSpec((chips, tile, blocks.shape[2]), lambda i: (0, i, 0))],
        out_specs=pl.BlockSpec((tile, total), lambda i: (i, 0)), out_shape=_SDS((d, total), blocks.dtype), name=name,
        compiler_params=_params(("arbitrary",), 16 << 20))(_hbm(blocks))


def _w_in_to_chips(g_all, cch, chips, *, name):
    d = g_all.shape[0]
    qkv, z, xbc, dt = _w_in_groups(cch)
    n = dt[0] + dt[1]
    z0 = 3 * _AW + cch
    tile = 2 * _ABLK

    def body(g_ref, o_ref):
        v = g_ref[...]
        full = jnp.concatenate([v[:, :3 * _AW], v[:, z0:z0 + _AW], v[:, 3 * _AW:z0], v[:, z0 + _AW:z0 + _AW + _HEADS]], axis=1)
        for k in range(chips):
            o_ref[k] = full[:, k * (n // chips):(k + 1) * (n // chips)]

    return pl.pallas_call(
        body, grid=(d // tile,), in_specs=[pl.BlockSpec((tile, g_all.shape[1]), lambda i: (i, 0))],
        out_specs=pl.BlockSpec((chips, tile, n // chips), lambda i: (0, i, 0)), out_shape=_SDS((chips, d, n // chips), g_all.dtype),
        name=name, compiler_params=_params(("arbitrary",), 16 << 20))(_hbm(g_all))


def _add_to(acc, r):
    return (acc + r,)


def _alibi_bias(dilation):
    slopes = 2.0 ** (-8.0 * (np.arange(_HEADS) + 1) / _HEADS)
    i = np.arange(_ABLK)[:, None]
    j = np.arange(_ABLK)[None, :]
    cur = np.where(i - j >= 0, -slopes[:, None, None] * ((i - j) * dilation), _NEG)
    prev = np.where(j >= i, -slopes[:, None, None] * ((i - j + _ABLK) * dilation), _NEG)
    both = np.stack([np.concatenate([np.full_like(prev, _NEG), cur], axis=2), np.concatenate([prev, cur], axis=2)])
    return jnp.asarray(both.reshape(2, _HEADS // 2, 2 * _ABLK, 2 * _ABLK), F32)


def _bias_spec():
    return pl.BlockSpec((None, _HEADS // 2, 2 * _ABLK, 2 * _ABLK), lambda r, j: (jnp.minimum(j, 1), 0, 0, 0))


def _strided(a, d):
    return a.reshape(a.shape[0] // d, d * a.shape[1])


def _pair(pr):
    return slice(pr * _LANES, (pr + 1) * _LANES)


def _low_lanes(shape):
    return lax.broadcasted_iota(jnp.int32, shape, 1) < _HDIM


def _halves(v, low):
    z = jnp.zeros_like(v)
    return jnp.where(low, v, z), jnp.where(low, z, v)


def _lane_spec(nb):
    return pl.BlockSpec((_ABLK, _LANES), lambda r, j: (jnp.minimum(j, nb - 1), r))


def _expand_heads(v):
    low = _low_lanes((v.shape[0], _LANES))
    return jnp.concatenate([jnp.where(low, v[:, 2 * pr:2 * pr + 1], v[:, 2 * pr + 1:2 * pr + 2]) for pr in range(_HEADS // 2)], axis=1)


def _attn_specs(nb, n_parts):
    def cur(p):
        return pl.BlockSpec((_ABLK, _AW), lambda r, j: (jnp.minimum(j, nb - 1), r * n_parts + p))

    def prev(p):
        return pl.BlockSpec((_ABLK, _AW), lambda r, j: (jnp.clip(j - 1, 0, nb - 1), r * n_parts + p))
    return cur, prev


def _attn_fwd(qkv, dilation, *, name):
    t = qkv.shape[0] * dilation
    nb = t // dilation // _ABLK
    bias = _alibi_bias(dilation)
    scale = _HDIM ** -0.5

    def body(q_ref, kc_ref, kp_ref, vc_ref, vp_ref, b_ref, o_ref, l_ref):
        low = _low_lanes((_ABLK, _LANES))
        l_ref[...] = jnp.zeros_like(l_ref)
        for pr in range(_HEADS // 2):
            sl = _pair(pr)
            k2 = jnp.concatenate([kp_ref[:, sl], kc_ref[:, sl]], axis=0)
            v2 = jnp.concatenate([vp_ref[:, sl], vc_ref[:, sl]], axis=0)
            q2 = jnp.concatenate(_halves(q_ref[:, sl] * scale, low), axis=0)
            s = _dot(q2, k2, _NT) + b_ref[pr]
            m = jnp.max(s, axis=-1, keepdims=True)
            p = jnp.exp(s - m)
            den = jnp.sum(p, axis=-1, keepdims=True)
            o = _dot(p, v2) / den
            lse = m + jnp.log(den)
            l_ref[:, 2 * pr:2 * pr + 1] = lse[:_ABLK]
            l_ref[:, 2 * pr + 1:2 * pr + 2] = lse[_ABLK:]
            o_ref[:, sl] = jnp.where(low, o[:_ABLK], o[_ABLK:]).astype(o_ref.dtype)

    cur, prev = _attn_specs(nb, 3)
    cur1, _ = _attn_specs(nb, 1)
    bspec = _bias_spec()
    sv = _hbm(qkv)
    o, l = pl.pallas_call(
        body, grid=(dilation, nb), in_specs=[cur(0), cur(1), prev(1), cur(2), prev(2), bspec],
        out_specs=[cur1(0), _lane_spec(nb)],
        out_shape=[_SDS((t // dilation, dilation * _AW), _BF), _SDS((t // dilation, dilation * _LANES), F32)], name=name,
        compiler_params=_params(("parallel", "arbitrary"), 16 << 20))(sv, sv, sv, sv, sv, bias)
    return o, l.reshape(t, _LANES)


def _attn_bwd(qkv, do, ld, dilation, *, name):
    t = qkv.shape[0] * dilation
    nb = t // dilation // _ABLK
    bias = _alibi_bias(dilation)
    scale = _HDIM ** -0.5

    def body(q_ref, kc_ref, kp_ref, vc_ref, vp_ref, do_ref, ld_ref, b_ref, dq_ref, dk_ref, dv_ref, ck, cv):
        n = pl.program_id(1)

        @pl.when(n == 0)
        def _():
            ck[...] = jnp.zeros_like(ck)
            cv[...] = jnp.zeros_like(cv)

        @pl.when(n < nb)
        def _():
            low = _low_lanes((_ABLK, _LANES))
            for pr in range(_HEADS // 2):
                sl = _pair(pr)
                k2 = jnp.concatenate([kp_ref[:, sl], kc_ref[:, sl]], axis=0)
                v2 = jnp.concatenate([vp_ref[:, sl], vc_ref[:, sl]], axis=0)
                q2 = jnp.concatenate(_halves(q_ref[:, sl] * scale, low), axis=0)
                do2 = jnp.concatenate(_halves(do_ref[:, sl], low), axis=0)
                lrow = jnp.concatenate([ld_ref[:, 2 * pr:2 * pr + 1], ld_ref[:, 2 * pr + 1:2 * pr + 2]], axis=0)
                dsum = jnp.concatenate([ld_ref[:, _HEADS + 2 * pr:_HEADS + 2 * pr + 1],
                                        ld_ref[:, _HEADS + 2 * pr + 1:_HEADS + 2 * pr + 2]], axis=0)
                p = jnp.exp(_dot(q2, k2, _NT) + b_ref[pr] - lrow)
                ds = (p * (_dot(do2, v2, _NT) - dsum)).astype(_BF)
                dq = _dot(ds, k2)
                dk2, dv2 = _dot(ds, q2, _TN), _dot(p, do2, _TN)
                dq_ref[:, sl] = (jnp.where(low, dq[:_ABLK], dq[_ABLK:]) * scale).astype(dq_ref.dtype)
                dk_ref[:, sl] = (ck[:, sl] + dk2[:_ABLK]).astype(dk_ref.dtype)
                dv_ref[:, sl] = (cv[:, sl] + dv2[:_ABLK]).astype(dv_ref.dtype)
                ck[:, sl] = dk2[_ABLK:]
                cv[:, sl] = dv2[_ABLK:]

        @pl.when(n == nb)
        def _():
            dk_ref[...] = ck[...].astype(dk_ref.dtype)
            dv_ref[...] = cv[...].astype(dv_ref.dtype)

    cur, prev = _attn_specs(nb, 3)
    cur1, prev1 = _attn_specs(nb, 1)
    bspec = _bias_spec()
    sv, dov, ldv = _hbm(qkv), _hbm(do), _hbm(_strided(ld, dilation))
    dqkv = pl.pallas_call(
        body, grid=(dilation, nb + 1),
        in_specs=[cur(0), cur(1), prev(1), cur(2), prev(2), cur1(0), _lane_spec(nb), bspec],
        out_specs=[cur1(0), prev1(0), prev1(0)], out_shape=[_SDS(dov.shape, _BF)] * 3, name=name,
        scratch_shapes=[pltpu.VMEM((_ABLK, _AW), F32)] * 2,
        compiler_params=_params(("parallel", "arbitrary"), 16 << 20))(sv, sv, sv, sv, sv, dov, ldv, bias)
    return dqkv


def _ssd_in_specs(ch):
    return dict(
        xs=pl.BlockSpec((_CHUNK, _AW), lambda c: (ch(c), 0)),
        bc=pl.BlockSpec((_CHUNK, 2 * _GROUPS * _NSTATE), lambda c: (ch(c), _AW // (2 * _GROUPS * _NSTATE))),
        lane=pl.BlockSpec((_CHUNK, _LANES), lambda c: (ch(c), 0)),
        arow=pl.BlockSpec((_HEADS, 1, _CHUNK), lambda c: (0, 0, ch(c))),
        st=pl.BlockSpec((1, _HEADS // 2, _NSTATE, _LANES), lambda c: (ch(c), 0, 0, 0)),
    )


def _decay(a_col, a_row):
    i0 = lax.broadcasted_iota(jnp.int32, (_CHUNK, _CHUNK), 0)
    i1 = lax.broadcasted_iota(jnp.int32, (_CHUNK, _CHUNK), 1)
    return jnp.where(i0 >= i1, jnp.exp(a_col - a_row), 0.0), jnp.where(i1 >= i0, jnp.exp(a_row - a_col), 0.0)


def _rsum(v):
    return jnp.sum(v, axis=-1, keepdims=True)


def _ssd_fwd(act, dt, acum, a_row, *, name):
    t = act.shape[0]
    nc = t // _CHUNK
    sp = _ssd_in_specs(lambda c: c)
    gw = _GROUPS * _NSTATE

    def body(xs_ref, bc_ref, dt_ref, ac_ref, ar_ref, y_ref, sall_ref, st):
        @pl.when(pl.program_id(0) == 0)
        def _():
            st[...] = jnp.zeros_like(st)

        low = _low_lanes((_CHUNK, _LANES))
        for g in range(_GROUPS):
            bg = bc_ref[:, g * _NSTATE:(g + 1) * _NSTATE]
            cg = bc_ref[:, gw + g * _NSTATE:gw + (g + 1) * _NSTATE].astype(_BF)
            cb = _dot(cg, bg, _NT)
            for pr in range(g * _HPG // 2, (g + 1) * _HPG // 2):
                ha, hb = 2 * pr, 2 * pr + 1
                a_a, a_b = ac_ref[:, ha:ha + 1], ac_ref[:, hb:hb + 1]
                x = (xs_ref[:, _pair(pr)] * jnp.where(low, dt_ref[:, ha:ha + 1], dt_ref[:, hb:hb + 1])).astype(_BF)
                lm_a, _ = _decay(a_a, ar_ref[ha])
                lm_b, _ = _decay(a_b, ar_ref[hb])
                sv = st[pr]
                sall_ref[0, pr] = sv
                yd = _dot(jnp.concatenate([cb * lm_a, cb * lm_b], axis=0), x)
                yd = jnp.where(low, yd[:_CHUNK], yd[_CHUNK:])
                y_ref[:, _pair(pr)] = yd + jnp.where(low, jnp.exp(a_a), jnp.exp(a_b)) * _dot(cg, sv)
                al_a, al_b = jnp.min(a_a, axis=0, keepdims=True), jnp.min(a_b, axis=0, keepdims=True)
                upd = _dot(jnp.concatenate([bg * jnp.exp(al_a - a_a), bg * jnp.exp(al_b - a_b)], axis=1), x, _TN)
                st[pr] = jnp.where(low, jnp.exp(al_a), jnp.exp(al_b)) * sv + jnp.where(low, upd[:_NSTATE], upd[_NSTATE:])

    return pl.pallas_call(
        body, grid=(nc,), in_specs=[sp['xs'], sp['bc'], sp['lane'], sp['lane'], sp['arow']],
        out_specs=[sp['xs'], sp['st']], out_shape=[_SDS((t, _AW), F32), _SDS((nc, _HEADS // 2, _NSTATE, _LANES), F32)],
        scratch_shapes=[pltpu.VMEM((_HEADS // 2, _NSTATE, _LANES), F32)], name=name,
        compiler_params=_params(("arbitrary",), 16 << 20))(*[_hbm(a) for a in (act, act, dt, acum, a_row)])


def _ssd_bwd(act, dt, acum, a_row, sall, dy, *, name):
    t = act.shape[0]
    nc = t // _CHUNK
    sp = _ssd_in_specs(lambda c: nc - 1 - c)
    gw = _GROUPS * _NSTATE

    def body(xs_ref, bc_ref, dt_ref, ac_ref, ar_ref, sall_ref, dy_ref, dxs_ref, dbc_ref, ddt_ref, da_ref, dst):
        @pl.when(pl.program_id(0) == 0)
        def _():
            dst[...] = jnp.zeros_like(dst)

        ddt_ref[...] = jnp.zeros_like(ddt_ref)
        da_ref[...] = jnp.zeros_like(da_ref)
        row = lax.broadcasted_iota(jnp.int32, (_CHUNK, 1), 0)
        low = _low_lanes((_CHUNK, _LANES))
        for g in range(_GROUPS):
            bg = bc_ref[:, g * _NSTATE:(g + 1) * _NSTATE]
            bgb = bg.astype(_BF)
            cg = bc_ref[:, gw + g * _NSTATE:gw + (g + 1) * _NSTATE].astype(_BF)
            cb, cbt = _dot(cg, bgb, _NT), _dot(bgb, cg, _NT)
            dcb = jnp.zeros((_CHUNK, _CHUNK), F32)
            dbg = jnp.zeros((_CHUNK, _NSTATE), F32)
            dcg = jnp.zeros((_CHUNK, _NSTATE), F32)
            for pr in range(g * _HPG // 2, (g + 1) * _HPG // 2):
                heads = (2 * pr, 2 * pr + 1)
                a_cols = [ac_ref[:, h:h + 1] for h in heads]
                dt_pair = jnp.where(low, dt_ref[:, heads[0]:heads[0] + 1], dt_ref[:, heads[1]:heads[1] + 1])
                xsv = xs_ref[:, _pair(pr)]
                x = xsv * dt_pair
                xb = x.astype(_BF)
                xhs = _halves(xb, low)
                dyv = dy_ref[:, _pair(pr)]
                dyb = dyv.astype(_BF)
                dyhs = _halves(dyb, low)
                sv, dsv = sall_ref[0, pr], dst[pr]
                svb, dsb = sv.astype(_BF), dsv.astype(_BF)
                a_lasts = [jnp.min(a, axis=0, keepdims=True) for a in a_cols]
                e_pair = jnp.where(low, jnp.exp(a_cols[0]), jnp.exp(a_cols[1]))
                el_pair = jnp.where(low, jnp.exp(a_lasts[0]), jnp.exp(a_lasts[1]))
                yo = e_pair * _dot(cg, svb)
                decays = [_decay(a_cols[i], ar_ref[h]) for i, h in enumerate(heads)]
                gms, gmts = [cb * lm for lm, _ in decays], [cbt * lmt for _, lmt in decays]
                w_cols = [jnp.exp(a_lasts[i] - a_cols[i]) for i in range(2)]
                x2, dy2 = jnp.concatenate(xhs, axis=0), jnp.concatenate(dyhs, axis=0)
                bwd = _dot(jnp.concatenate([bg * w_cols[0], bg * w_cols[1]], axis=0), dsb)
                dxg = _dot(jnp.concatenate(gms, axis=1), dyb, _TN)
                dg2, dgt2, xds2 = _dot(dy2, xb, _NT), _dot(x2, dyb, _NT), _dot(x2, dsb, _NT)
                das = []
                for i in range(2):
                    rows_i = slice(i * _CHUNK, (i + 1) * _CHUNK)
                    dcb = dcb + dg2[rows_i] * decays[i][0]
                    dbg = dbg + w_cols[i] * xds2[rows_i]
                    das.append(_rsum(dg2[rows_i] * gms[i]) - _rsum(dgt2[rows_i] * gmts[i]))
                bwd = jnp.where(low, bwd[:_CHUNK], bwd[_CHUNK:])
                dx = jnp.where(low, dxg[:_CHUNK], dxg[_CHUNK:]) + bwd
                edy = (e_pair * dyv).astype(_BF)
                dcg = dcg + _dot(edy, svb, _NT)
                zs, yos, sds, dts = (_halves(v, low) for v in (x * bwd, dyv * yo, sv * dsv, dx * xsv))
                for i, h in enumerate(heads):
                    z = _rsum(zs[i])
                    da_last = jnp.sum(z, axis=0, keepdims=True) + jnp.exp(a_lasts[i]) * jnp.sum(_rsum(sds[i]), axis=0, keepdims=True)
                    da_ref[:, h:h + 1] = das[i] + _rsum(yos[i]) - z + jnp.where(row == _CHUNK - 1, da_last, 0.0)
                    ddt_ref[:, h:h + 1] = _rsum(dts[i])
                dxs_ref[:, _pair(pr)] = dx * dt_pair
                dst[pr] = el_pair * dsv + _dot(cg, edy, _TN)
            dbc_ref[:, g * _NSTATE:(g + 1) * _NSTATE] = dbg + _dot(dcb, cg, _TN)
            dbc_ref[:, gw + g * _NSTATE:gw + (g + 1) * _NSTATE] = dcg + _dot(dcb, bgb)

    ch = lambda c: nc - 1 - c
    wide = pl.BlockSpec((_CHUNK, 2 * gw), lambda c: (ch(c), 0))
    return pl.pallas_call(
        body, grid=(nc,), in_specs=[sp['xs'], sp['bc'], sp['lane'], sp['lane'], sp['arow'], sp['st'], sp['xs']],
        out_specs=[sp['xs'], wide, sp['lane'], sp['lane']],
        out_shape=[_SDS((t, _AW), F32), _SDS((t, 2 * gw), F32), _SDS((t, _LANES), F32), _SDS((t, _LANES), F32)],
        scratch_shapes=[pltpu.VMEM((_HEADS // 2, _NSTATE, _LANES), F32)], name=name,
        compiler_params=_params(("arbitrary",), 16 << 20))(*[_hbm(a) for a in (act, act, dt, acum, a_row, sall, dy)])


def _scan_rows(v, reverse):
    r = lax.broadcasted_iota(jnp.int32, v.shape, 0)
    for s in (1, 2, 4, 8, 16, 32, 64):
        if reverse:
            v = v + jnp.where(r < _CHUNK - s, pltpu.roll(v, _CHUNK - s, 0), 0.0)
        else:
            v = v + jnp.where(r >= s, pltpu.roll(v, s, 0), 0.0)
    return v


def _softplus(x):
    return jnp.maximum(x, 0.0) + jnp.log(1.0 + jnp.exp(-jnp.abs(x)))


def _sigmoid(x):
    return 1.0 / (1.0 + jnp.exp(-x))


def _silu(x):
    return x * _sigmoid(x)


def _dsilu(x):
    s = _sigmoid(x)
    return s * (1.0 + x * (1.0 - s))


def _lanes(a):
    return jnp.pad(a, (0, _LANES - a.shape[0])).reshape(1, _LANES)


def _layer_fwd(x, p, l):
    cch = p['conv_w'].shape[1]
    sv = {}
    h1 = _rms_fwd(x, p['ln1_g'], name=f"ln1_fwd_{l}")
    qkv = _mm(h1, p['w_in'], b_cols=(0, 3 * _AW), outs=(_BF,), name=f"in_proj_qkv_{l}")
    xbc = _mm(h1, p['w_in'], b_cols=(3 * _AW, cch), name=f"in_proj_xbc_{l}")
    zdt = _mm(h1, p['w_in'], b_cols=(3 * _AW + cch, _AW + _LANES), name=f"in_proj_zdt_{l}")
    z, dt_raw = (zdt, _AW, 0), (zdt, _LANES, _AW // _LANES)

    tile = 2 * _ABLK
    perms = [_perm(d, tile) for d in _DILATIONS[1:]]
    views = [qkv] + list(_rows(lambda a, p2, p3: (_stride(a, p2, _DILATIONS[1]), _stride(a, p3, _DILATIONS[2])), [qkv], perms,
                               [(3 * _AW, _BF, d) for d in _DILATIONS[1:]], tile=tile, name=f"qkv_strided_{l}"))
    outs = []
    for dil, view in zip(_DILATIONS, views):
        outs += _attn_fwd(view, dil, name=f"attn_fwd_d{dil}_{l}")

    def combine(o1, l1, o2, l2, o3, l3, p2, p3):
        m = jnp.maximum(jnp.maximum(l1, l2), l3)
        e1, e2, e3 = jnp.exp(l1 - m), jnp.exp(l2 - m), jnp.exp(l3 - m)
        tot = e1 + e2 + e3
        mixed = sum(_expand_heads(e / tot) * o for e, o in ((e1, o1.astype(F32)), (e2, _unstride(o2, p2)), (e3, _unstride(o3, p3))))
        return mixed, m + jnp.log(tot)
    outs = [a if i % 2 or i == 0 else ("strided", a, _DILATIONS[i // 2]) for i, a in enumerate(outs)]
    attn, lse = _rows(combine, outs, perms, [(_AW, F32), (_LANES, F32)], tile=tile, name=f"attn_combine_{l}")
    mix = _rms_fwd(attn, p['attn_norm_g'], into=(None, 2 * _AW, 0), name=f"attn_norm_fwd_{l}")

    def conv(u0, before, w, b):
        u1, u2, u3 = _shifted(u0, before, True)
        return _silu(w[0:1] * u3 + w[1:2] * u2 + w[2:3] * u1 + w[3:4] * u0 + b)
    act = _rows(conv, [xbc], [p['conv_w'], p['conv_b'].reshape(1, cch)], [(cch, F32)], halos=[(xbc, -1)], tile=_tile_for(cch),
                name=f"conv_fwd_{l}")[0]

    def dtf(raw, bias, alog):
        dt = _softplus(raw + bias)
        return dt, _scan_rows(dt * -jnp.exp(alog), False)
    dt, acum = _rows(dtf, [dt_raw], [_lanes(p['dt_bias']), _lanes(p['a_log'])], [(_LANES, F32), (_LANES, F32)],
                     tile=_CHUNK, name=f"dt_fwd_{l}")
    a_row = acum[:, :_HEADS].T[:, None, :]
    y_ssd, sall = _ssd_fwd(act, dt, acum, a_row, name=f"ssd_fwd_{l}")
    dskip = jnp.repeat(p['d_skip'], _HDIM).reshape(1, _AW)
    xs = (act, _AW, 0)

    def gate(y, xs, z, dsk):
        return (y + dsk * xs) * _silu(z)
    y2 = _rows(gate, [y_ssd, xs, z], [dskip], [(_AW, F32)], tile=_tile_for(_AW), name=f"gate_fwd_{l}")[0]
    mix = _rms_fwd(y2, p['ssd_norm_g'], groups=_GROUPS, into=(mix, 2 * _AW, 1), name=f"ssd_norm_fwd_{l}")
    sv.update(x=x, h1=h1, qkv=views, zdt=zdt, xbc=xbc, attn=attn, lse=lse, act=act, dt=dt, acum=acum, a_row=a_row,
              sall=sall, y_ssd=y_ssd, dskip=dskip, y2=y2, mix=mix)
    return mix, sv


def _layer_fwd_mlp(p, sv, l):
    x2 = _mm(sv['mix'], p['w_out'], extra=(sv['x'],), epi=_add_to, name=f"out_proj_{l}")
    h2 = _rms_fwd(x2, p['ln2_g'], name=f"ln2_fwd_{l}")
    a = _mm(h2, p['w_mlp_in'], b_chips=_CHIPS, epi=lambda acc: (jnp.square(jnp.maximum(acc, 0.0)),), outs=(_BF,), name=f"mlp_in_{l}")
    x3 = _mm(a, p['w_mlp_out'], extra=(x2,), epi=_add_to, name=f"mlp_out_{l}")
    sv.update(x2=x2, h2=h2, a=a)
    return x3


def _layer_bwd(dx3, dx3b, p, sv, l, send, after):
    cch = p['conv_w'].shape[1]
    g = {}
    du = _mm(dx3b, p['w_mlp_out'], tb=True, extra=(sv['a'],), outs=(_BF,), after=after,
             epi=lambda acc, a: (acc * 2.0 * jnp.sqrt(a.astype(F32)),), name=f"mlp_out_dx_{l}")
    g['w_mlp_out'] = _mm(sv['a'], dx3b, ta=True, outs=(_BF,), name=f"mlp_out_dw_{l}")
    g['w_mlp_in'] = _mm(sv['h2'], du, ta=True, out_chips=_CHIPS, outs=(_BF,), name=f"mlp_in_dw_{l}")
    sent = send(('w_mlp_out', 'w_mlp_in'), g)
    dh2 = _mm(du, p['w_mlp_in'], tb=True, b_chips=_CHIPS, after=sent, name=f"mlp_in_dx_{l}")
    dx2, dx2b, g['ln2_g'] = _rms_bwd(sv['x2'], dh2, p['ln2_g'], dx3, name=f"ln2_bwd_{l}")
    dmix = _mm(dx2b, p['w_out'], tb=True, name=f"out_proj_dx_{l}")
    g['w_out'] = _mm(sv['mix'], dx2b, ta=True, outs=(_BF,), name=f"out_proj_dw_{l}")
    after_out = send(('w_out',), g)

    tile = 2 * _ABLK
    perms = [_perm(d, tile) for d in _DILATIONS[1:]]

    def norm_bwd(attn, dy, lse, gn, p2, p3):
        dattn, dgn = _rms_bwd_tile(attn, dy, gn, 1)
        prod, low = dattn * attn, _low_lanes((attn.shape[0], _LANES))
        lane = lax.broadcasted_iota(jnp.int32, lse.shape, 1)
        ld = jnp.where(lane < _HEADS, lse, 0.0)
        for pr in range(_HEADS // 2):
            for i, part in enumerate(_halves(prod[:, _pair(pr)], low)):
                ld = jnp.where(lane == _HEADS + 2 * pr + i, _rsum(part), ld)
        return dattn, _stride(dattn, p2, _DILATIONS[1]), _stride(dattn, p3, _DILATIONS[2]), ld, dgn
    *dos, ld, gn_sum = _rows(norm_bwd, [sv['attn'], (dmix, _AW, 0), sv['lse']], [p['attn_norm_g'].reshape(1, _AW)] + perms,
                             [(_AW, _BF)] + [(_AW, _BF, d) for d in _DILATIONS[1:]] + [(_LANES, F32)], [_AW], after=after_out,
                             tile=tile, name=f"attn_norm_bwd_{l}")
    g['attn_norm_g'] = gn_sum.sum(axis=0)
    parts = [_attn_bwd(view, do, ld, dil, name=f"attn_bwd_d{dil}_{l}") for view, do, dil in zip(sv['qkv'], dos, _DILATIONS)]

    def branch_sum(*t):
        parts_, (p2, p3) = t[:9], t[9:]
        t = [a.astype(F32) for a in parts_[:3]] + [_unstride(a, p2) for a in parts_[3:6]] + [_unstride(a, p3) for a in parts_[6:]]
        return jnp.concatenate([t[i] + t[3 + i] + t[6 + i] for i in range(3)], axis=1)
    branch_ins = list(parts[0]) + [("strided", a, d) for pr, d in zip(parts[1:], _DILATIONS[1:]) for a in pr]
    w_all = 3 * _AW + cch + _AW + _LANES
    dproj = _rows(branch_sum, branch_ins, perms, [(3 * _AW, _BF)], into=(None, w_all, 0), tile=tile, name=f"attn_bwd_sum_{l}")[0]

    xs, z, dt_raw = (sv['act'], _AW, 0), (sv['zdt'], _AW, 0), (sv['zdt'], _LANES, _AW // _LANES)

    def gate_bwd(y2, dy, y, xs, z, dsk, gn):
        dy2, dgn = _rms_bwd_tile(y2, dy, gn, _GROUPS)
        dy1 = dy2 * _silu(z)
        return dy1, dsk * dy1, dy2 * (y + dsk * xs) * _dsilu(z), dy1 * xs, dgn
    dy1, dxs_skip, dz, dsk_sum, gn_sum = _rows(
        gate_bwd, [sv['y2'], (dmix, _AW, 1), sv['y_ssd'], xs, z], [sv['dskip'], p['ssd_norm_g'].reshape(1, _AW)],
        [(_AW, F32), (_AW, F32), (_AW, _BF)], [_AW, _AW], tile=128, name=f"gate_bwd_{l}")
    g['ssd_norm_g'] = gn_sum.sum(axis=0)
    g['d_skip'] = dsk_sum.sum(axis=0).reshape(_HEADS, _HDIM).sum(axis=1)
    dxs, dbc, ddt, da = _ssd_bwd(sv['act'], sv['dt'], sv['acum'], sv['a_row'], sv['sall'], dy1, name=f"ssd_bwd_{l}")

    def dtb(da, ddtx, raw, dt, dz, bias, alog):
        a = -jnp.exp(alog)
        dda = _scan_rows(da, True)
        draw = (dda * a + ddtx) * _sigmoid(raw + bias)
        return jnp.concatenate([dz, draw.astype(dz.dtype)], axis=1), draw, dda * dt * a
    dproj, dbias, dalog = _rows(dtb, [da, ddt, dt_raw, sv['dt'], dz], [_lanes(p['dt_bias']), _lanes(p['a_log'])],
                                [(_AW + _LANES, _BF)], [_LANES, _LANES], into=(dproj, w_all, (3 * _AW + cch) // (_AW + _LANES)),
                                tile=_CHUNK, name=f"dt_bwd_{l}")
    g['dt_bias'], g['a_log'] = dbias.sum(axis=0)[:_HEADS], dalog.sum(axis=0)[:_HEADS]
    def conv_bwd1(u0, dxs, dbc, dxk, before, w, b):
        u1, u2, u3 = _shifted(u0, before, True)
        pre = w[0:1] * u3 + w[1:2] * u2 + w[2:3] * u1 + w[3:4] * u0 + b
        dp = jnp.concatenate([dxs + dxk, dbc], axis=1) * _dsilu(pre)
        return dp, dp * u3, dp * u2, dp * u1, dp * u0, dp
    dpre, *dws = _rows(conv_bwd1, [sv['xbc'], dxs, dbc, dxs_skip], [p['conv_w'], p['conv_b'].reshape(1, cch)], [(cch, F32)],
                       [cch] * 5, halos=[(sv['xbc'], -1)], tile=128, name=f"conv_bwd_pre_{l}")
    g['conv_w'] = jnp.stack([dws[i].sum(axis=0) for i in range(_CONV_K)])
    g['conv_b'] = dws[4].sum(axis=0)

    def conv_bwd2(p0, after_, w):
        p1, p2, p3 = _shifted(p0, after_, False)
        return w[3:4] * p0 + w[2:3] * p1 + w[1:2] * p2 + w[0:1] * p3
    dproj = _rows(conv_bwd2, [dpre], [p['conv_w']], [(cch, _BF)], halos=[(dpre, 1)], into=(dproj, w_all, 3 * _AW // cch),
                  tile=_tile_for(cch), name=f"conv_bwd_in_{l}")[0]
    g_all = _mm(sv['h1'], dproj, ta=True, outs=(_BF,), name=f"in_proj_dw_{l}")
    g['w_in'] = _w_in_to_chips(g_all, cch, _CHIPS, name=f"w_in_by_chip_{l}")
    sent = send(('w_in',), g)
    for n in _BIG:
        del g[n]
    dh1 = _mm(dproj, p['w_in'], tb=True, after=sent, name=f"in_proj_dx_{l}")
    dx, dxb, g['ln1_g'] = _rms_bwd(sv['x'], dh1, p['ln1_g'], dx2, name=f"ln1_bwd_{l}")
    return dx, dxb, g


def _loss_bwd(x, g, tgt):
    w = x.shape[1]
    tile = _tile_for(w)

    def fn(x, tgt, g):
        r = _rstd(x)
        xh = x * r
        e = xh * g - tgt
        gd = e * (g / w)
        dx = r * (gd - xh * jnp.mean(gd * xh, axis=-1, keepdims=True))
        rowloss = 0.5 * jnp.mean(e * e, axis=-1, keepdims=True)
        return dx, dx, (e / w) * xh, jnp.broadcast_to(rowloss, (tile, _LANES))
    dx, dxb, dg, ls = _rows(fn, [x, tgt], [g.reshape(1, w)], [(w, F32), (w, _BF)], [w, _LANES], tile=tile, name="loss_head")
    return dx, dxb, dg.sum(axis=0), ls[:, 0].sum()


def _adamw_math(w, g, m, v):
    m2 = _B1 * m + (1.0 - _B1) * g
    v2 = _B2 * v + (1.0 - _B2) * jnp.square(g)
    m_hat = m2 / (1.0 - _B1 ** _STEP)
    v_hat = v2 / (1.0 - _B2 ** _STEP)
    return -_LR * (m_hat / (jnp.sqrt(v_hat) + _AEPS) + _WD * w), m2, v2


def _adamw(w, g, m, v, *, name):
    width = w.shape[-1]
    flat = [a.reshape(-1, width) for a in (w, g, m, v)]
    tile = _pick(flat[0].shape[0], (_tile_for(width), 32, 8))
    res = _rows(_adamw_math, flat, [], [(width, F32)] * 3, tile=tile, name=name)
    return [r.reshape(w.shape) for r in res]


_HBM = pl.BlockSpec(memory_space=pltpu.HBM)


def _place():
    x, y, c = lax.axis_index("x"), lax.axis_index("y"), lax.axis_index("c")
    other_chips = [(1 - x, y), (x, 1 - y), (1 - x, 1 - y)]
    return x, y, c, other_chips


def _remote(src, dst, sems, i, dev):
    return pltpu.make_async_remote_copy(src_ref=src, dst_ref=dst, send_sem=sems[0].at[i], recv_sem=sems[1].at[i],
                                        device_id=dev, device_id_type=_MESH)


def _exchange8(v, *, reduce, after=None, name):
    r, w = v.shape
    behind = [] if after is None else [after]

    def body(v_ref, *rest):
        all_ref, rest = rest[len(behind)], rest[len(behind) + 1:]
        sems = rest[-2:]
        x, y, c, _ = _place()
        me = 4 * x + 2 * y + c
        all_ref[me] = v_ref[...]
        flips = [((d >> 2) & 1, (d >> 1) & 1, d & 1) for d in range(1, 8)]
        sends = [_remote(v_ref, all_ref.at[me], sems, i, (x ^ fx, y ^ fy, c ^ fc)) for i, (fx, fy, fc) in enumerate(flips)]
        for cp in sends:
            cp.start()
        for i, (fx, fy, fc) in enumerate(flips):
            _remote(v_ref, all_ref.at[me ^ (4 * fx + 2 * fy + fc)], sems, i, (x ^ fx, y ^ fy, c ^ fc)).wait_recv()
        for cp in sends:
            cp.wait_send()
        if reduce:
            acc = all_ref[0]
            for s in range(1, 8):
                acc = acc + all_ref[s]
            rest[0][...] = acc

    vm = pl.BlockSpec(memory_space=pltpu.VMEM)
    out_shape = [_SDS((8, r, w), v.dtype)] + ([_SDS((r, w), v.dtype)] if reduce else [])
    res = pl.pallas_call(body, in_specs=[vm] + [_ANY] * len(behind), out_specs=[vm] * len(out_shape), out_shape=out_shape, name=name,
                         scratch_shapes=[pltpu.SemaphoreType.DMA((7,)), pltpu.SemaphoreType.DMA((7,))],
                         compiler_params=pltpu.CompilerParams(vmem_limit_bytes=int(32 << 20)))(v, *behind)
    return res[1] if reduce else res[0]


_SEM = pl.BlockSpec(memory_space=pltpu.SEMAPHORE)
_ANY = pl.BlockSpec(memory_space=pl.ANY)
_EFFECT = pltpu.SideEffectType.DATAFLOW_SIDE_EFFECTING


def _send_start(name, srcs, land_shapes, plan, n_sends, after):
    ns, nl = len(srcs), len(land_shapes)
    zones = [_hbm(lax.empty(s.shape, s.dtype)) if isinstance(s, _SDS) else s for s in land_shapes]

    def body(*refs):
        ins, lands, sems = refs[:ns], refs[ns:ns + nl], refs[ns + nl + 1:ns + nl + 3]
        x, y, c, chips = _place()
        for i, (s, d, dev) in enumerate(plan(x, y, c, chips, ins, lands)[0]):
            _remote(s, d, sems, i, dev).start()
        refs[-1][...] = jnp.zeros_like(refs[-1])

    sem = pltpu.SemaphoreType.DMA((n_sends,))
    res = pl.pallas_call(
        body, name=name, in_specs=[_HBM] * (ns + nl) + [_ANY],
        out_shape=(sem, sem, *[pltpu.HBM(s.shape, s.dtype) for s in land_shapes], _SDS((8, _LANES), F32)),
        out_specs=(_SEM, _SEM, *[_HBM] * nl, pl.BlockSpec(memory_space=pltpu.VMEM)),
        input_output_aliases={ns + i: 2 + i for i in range(nl)},
        compiler_params=pltpu.CompilerParams(has_side_effects=_EFFECT))(
            *[_hbm(s) for s in srcs], *zones, after)
    return dict(sems=res[:2], srcs=srcs, lands=res[2:2 + nl], plan=plan), res[-1]


def _send_wait(name, h, after):
    ns, nl = len(h['srcs']), len(h['lands'])

    def body(*refs):
        ins, lands, sems = refs[:ns], refs[ns:ns + nl], refs[ns + nl:ns + nl + 2]
        x, y, c, chips = _place()
        sends, landings = h['plan'](x, y, c, chips, ins, lands)
        for i, (s, d, dev) in enumerate(sends):
            _remote(s, d, sems, i, dev).wait_send()
        for i, d in enumerate(landings):
            _remote(d, d, sems, i, sends[i][2]).wait_recv()

    return pl.pallas_call(
        body, name=name, in_specs=[_HBM] * (ns + nl) + [_SEM, _SEM, _ANY],
        out_shape=tuple(pltpu.HBM(a.shape, a.dtype) for a in h['lands']), out_specs=tuple([_HBM] * nl),
        input_output_aliases={ns + i: i for i in range(nl)},
        compiler_params=pltpu.CompilerParams(has_side_effects=_EFFECT))(
            *[_hbm(s) for s in h['srcs']], *h['lands'], *h['sems'], after)


def _gather_plan(items):
    def plan(x, y, c, chips, ins, lands):
        k = 2 * x + y
        to = [(px, py, c) for px, py in chips] + [(x, y, 1 - c)]
        sends = [(ins[si].at[l], lands[t].at[k], dev) for t, (si, l) in enumerate(items) for dev in to]
        return sends, [lands[t].at[2 * px + py] for t in range(len(items)) for px, py in chips + [(x, y)]]
    return plan


_FLIPS = [((d >> 2) & 1, (d >> 1) & 1, d & 1) for d in range(1, 8)]


def _reduce_plan(halves):
    def plan(x, y, c, chips, ins, lands):
        sends, landings = [], []
        for t, hf in enumerate(halves):
            for i, (fx, fy, fc) in enumerate(_FLIPS):
                px, py, pc = x ^ fx, y ^ fy, c ^ fc
                sends.append((ins[t].at[2 * px + py, pl.ds(pc * hf, hf)], lands[t].at[i], (px, py, pc)))
                landings.append(lands[t].at[i])
        return sends, landings
    return plan


def _swap(name, srcs, out_shapes, plan, n_sends):
    n = len(srcs)

    def body(*refs):
        ins, outs, sems = refs[:n], refs[n:n + len(out_shapes)], refs[-2:]
        x, y, c, chips = _place()
        sends, landings = plan(x, y, c, chips, ins, outs)
        out = [_remote(s, d, sems, i, dev) for i, (s, d, dev) in enumerate(sends)]
        for cp in out:
            cp.start()
        for i, d in enumerate(landings):
            _remote(d, d, sems, i, sends[i][2]).wait_recv()
        for cp in out:
            cp.wait_send()

    return pl.pallas_call(
        body, in_specs=[_HBM] * n, out_specs=[_HBM] * len(out_shapes), out_shape=out_shapes, name=name,
        scratch_shapes=[pltpu.SemaphoreType.DMA((n_sends,)), pltpu.SemaphoreType.DMA((n_sends,))])(*srcs)


def _sum_owned(grads, landed, c, k, names):
    def sum8(*parts):
        acc = parts[0].astype(F32)
        for p in parts[1:]:
            acc = acc + p.astype(F32)
        return acc
    outs = []
    for g, got, name in zip(grads, landed, names):
        hf, b = got.shape[1:]
        own = lax.dynamic_slice_in_dim(lax.dynamic_index_in_dim(g, k, axis=0, keepdims=False), c * hf, hf, axis=0)
        outs.append(_rows(sum8, [own] + [("slot", got, i) for i in range(len(_FLIPS))], [], [(b, F32)],
                          tile=_pick(hf, (_tile_for(b), 32)), name=f"grad_sum_{name}")[0])
    return outs


def _share_halves(mine, *, name):
    n = len(mine)

    def plan(x, y, c_, chips, ins, outs):
        return [(ins[t], outs[t], (x, y, 1 - c_)) for t in range(n)], [outs[t] for t in range(n)]
    return _swap(name, mine, [_SDS(h.shape, F32) for h in mine], plan, n)


def _adamw_owned(w, mine, theirs, m, v, c, *, name):
    depth, a, b = w.shape
    half = a // 2
    tile = _pick(half, (_tile_for(b), 32, 8))
    nh = half // tile

    def blocks_of(l):
        return lambda i: (jnp.clip(i - 2 * nh * l, 0, 2 * nh - 1) % nh, 0)

    def fn(w, m, v, *rest):
        halves, cflag = rest[:-1], rest[-1]
        step = pl.program_id(0)
        is_mine = cflag[0:1, 0:1] == ((step // nh) % 2).astype(F32)
        g = jnp.where(is_mine, halves[0], halves[1])
        for l in range(1, depth):
            g = jnp.where(step >= 2 * nh * l, jnp.where(is_mine, halves[2 * l], halves[2 * l + 1]), g)
        return (g,) + _adamw_math(w, g, m, v)
    ins = [a_.reshape(depth * a, b) for a_ in (w, m, v)]
    ins += [(h, b, blocks_of(l)) for l in range(depth) for h in (mine[l], theirs[l])]
    res = _rows(fn, ins, [jnp.full((1, _LANES), c, F32)], [(b, F32)] * 4, tile=tile, name=name)
    return [r.reshape(w.shape) for r in res]


_BIG = ("w_in", "w_out", "w_mlp_in", "w_mlp_out")
_SMALL = ("ln1_g", "conv_b", "dt_bias", "a_log", "d_skip", "attn_norm_g", "ssd_norm_g", "ln2_g", "final_norm_g")
_ORDER = ("ln1_g", "w_in", "conv_w", "conv_b", "dt_bias", "a_log", "d_skip", "attn_norm_g", "ssd_norm_g", "w_out", "ln2_g",
          "w_mlp_in", "w_mlp_out", "final_norm_g")


def _pack(parts, rows):
    flat = jnp.concatenate([p.reshape(-1) for p in parts])
    return jnp.pad(flat, (0, rows * _LANES - flat.shape[0])).reshape(rows, _LANES)


def _unpack(buf, like):
    flat, out, o = buf.reshape(-1), [], 0
    for p in like:
        out.append(flat[o:o + p.size].reshape(p.shape))
        o += p.size
    return out


def kernel(x, ln1_g, w_in, conv_w, conv_b, dt_bias, a_log, d_skip, attn_norm_g, ssd_norm_g, w_out, ln2_g, w_mlp_in, w_mlp_out, final_norm_g, loss_target, m_ln1_g, m_w_in, m_conv_w, m_conv_b, m_dt_bias, m_a_log, m_d_skip, m_attn_norm_g, m_ssd_norm_g, m_w_out, m_ln2_g, m_w_mlp_in, m_w_mlp_out, m_final_norm_g, v_ln1_g, v_w_in, v_conv_w, v_conv_b, v_dt_bias, v_a_log, v_d_skip, v_attn_norm_g, v_ssd_norm_g, v_w_out, v_ln2_g, v_w_mlp_in, v_w_mlp_out, v_final_norm_g):
    w = dict(ln1_g=ln1_g, w_in=w_in, conv_w=conv_w, conv_b=conv_b, dt_bias=dt_bias, a_log=a_log, d_skip=d_skip,
             attn_norm_g=attn_norm_g, ssd_norm_g=ssd_norm_g, w_out=w_out, ln2_g=ln2_g, w_mlp_in=w_mlp_in, w_mlp_out=w_mlp_out,
             final_norm_g=final_norm_g)
    m = dict(ln1_g=m_ln1_g, w_in=m_w_in, conv_w=m_conv_w, conv_b=m_conv_b, dt_bias=m_dt_bias, a_log=m_a_log, d_skip=m_d_skip,
             attn_norm_g=m_attn_norm_g, ssd_norm_g=m_ssd_norm_g, w_out=m_w_out, ln2_g=m_ln2_g, w_mlp_in=m_w_mlp_in,
             w_mlp_out=m_w_mlp_out, final_norm_g=m_final_norm_g)
    v = dict(ln1_g=v_ln1_g, w_in=v_w_in, conv_w=v_conv_w, conv_b=v_conv_b, dt_bias=v_dt_bias, a_log=v_a_log, d_skip=v_d_skip,
             attn_norm_g=v_attn_norm_g, ssd_norm_g=v_ssd_norm_g, w_out=v_w_out, ln2_g=v_ln2_g, w_mlp_in=v_w_mlp_in,
             w_mlp_out=v_w_mlp_out, final_norm_g=v_final_norm_g)
    depth, d_model = ln1_g.shape
    n_chips = _CHIPS
    c = lax.axis_index("c")
    chip = 2 * lax.axis_index("x") + lax.axis_index("y")
    cch = conv_w.shape[2] * n_chips

    cw = _exchange8(conv_w.reshape(depth * _CONV_K, -1), reduce=False, name="gather_conv_w")[0::2]
    conv_full = cw.reshape(n_chips, depth, _CONV_K, -1).transpose(1, 2, 0, 3).reshape(depth, _CONV_K, cch)
    own = [w[n].astype(_BF) for n in _BIG]

    def start_gather(tag, items, after):
        lands = [_SDS((n_chips, *own[i].shape[1:]), _BF) for i, _ in items]
        return _send_start(f"gather_start_{tag}", own, lands, _gather_plan(items), n_chips * len(items), after)

    def finish_gather(tag, handle, items, after):
        landed = _send_wait(f"gather_wait_{tag}", handle, after)
        return {_BIG[i]: g for (i, _), g in zip(items, landed)}

    def layer_weights(l, blocks):
        p = {}
        if 'w_in' in blocks:
            p['w_in'] = _w_in_from_chips(blocks['w_in'], cch, name=f"w_in_regroup_{l}")
        if 'w_out' in blocks:
            p['w_out'] = blocks['w_out'].reshape(-1, d_model)
            p['w_mlp_in'] = blocks['w_mlp_in']
            p['w_mlp_out'] = blocks['w_mlp_out'].reshape(-1, d_model)
        return p

    groups = dict(a=[(0, 0)], b=[(1, 0), (2, 0), (3, 0)], c=[(0, 1)], d=[(1, 1), (2, 1), (3, 1)])
    handles, token = {}, conv_full

    half_in = own[0].shape[1] // 2

    def rows_of(ref, who):
        return ref.at[pl.ds(who * half_in, half_in)]

    def plan_a(x_, y_, c_, chips, ins, lands):
        k = 2 * x_ + y_
        sends = [(rows_of(ins[0].at[0], c_), rows_of(lands[0].at[k], c_), (px, py, c_)) for px, py in chips]
        sends.append((ins[0].at[0], lands[0].at[k], (x_, y_, 1 - c_)))
        return sends, [rows_of(lands[0].at[2 * px + py], c_) for px, py in chips] + [lands[0].at[k]]

    def plan_pass(x_, y_, c_, chips, ins, lands):
        sends = [(rows_of(lands[0].at[2 * px + py], c_),) * 2 + ((x_, y_, 1 - c_),) for px, py in chips]
        return sends, [rows_of(lands[0].at[2 * px + py], 1 - c_) for px, py in chips]
    handles["a"], token = _send_start("gather_start_a", own[:1], [_SDS((n_chips, *own[0].shape[1:]), _BF)], plan_a, n_chips, token)
    for tag, items in list(groups.items())[1:]:
        handles[tag], token = start_gather(tag, items, token)
    landed = _send_wait("gather_land_a", handles["a"], token)
    handles["a"], token = _send_start("gather_pass_a", [], landed, plan_pass, 3, landed[0])
    layers = [{n: w[n][l] for n in _SMALL[:-1]} for l in range(depth)]
    for l in range(depth):
        layers[l]['conv_w'] = conv_full[l]

    layers[0].update(layer_weights(0, finish_gather("a", handles["a"], groups["a"], token)))
    mix, sv0 = _layer_fwd(x[0], layers[0], 0)
    layers[0].update(layer_weights(0, finish_gather("b", handles["b"], groups["b"], mix)))
    h = _layer_fwd_mlp(layers[0], sv0, 0)
    layers[1].update(layer_weights(1, finish_gather("c", handles["c"], groups["c"], h)))
    mix, sv1 = _layer_fwd(h, layers[1], 1)
    layers[1].update(layer_weights(1, finish_gather("d", handles["d"], groups["d"], mix)))
    h = _layer_fwd_mlp(layers[1], sv1, 1)
    saved = [sv0, sv1]

    def by_chip(g, name):
        if name in ("w_mlp_in", "w_in"):
            return g
        return g.reshape(n_chips, -1, d_model)

    pending = []

    def sender(l):
        def send(names, g):
            srcs = [by_chip(g[n], n) for n in names]
            halves = [s.shape[1] // 2 for s in srcs]
            lands = [_SDS((len(_FLIPS), hf, s.shape[2]), _BF) for s, hf in zip(srcs, halves)]
            handle, tok = _send_start(f"grad_start_{names[-1]}_{l}", srcs, lands, _reduce_plan(halves), len(_FLIPS) * len(srcs), srcs[0])
            pending.append((l, names, srcs, handle))
            return tok
        return send

    dx, dxb, g_final, loss_part = _loss_bwd(h, final_norm_g, loss_target[0])
    grads, after = [None] * depth, None
    for l in reversed(range(depth)):
        dx, dxb, grads[l] = _layer_bwd(dx, dxb, layers[l], saved[l], l, sender(l), after)
        after = dx
    landed_of = {}
    sent_in = {(n, l): (gi, j) for gi, (l, names, _, _) in enumerate(pending) for j, n in enumerate(names)}

    def landed_for(gi, after):
        if gi not in landed_of:
            l, names, _, handle = pending[gi]
            landed_of[gi] = _send_wait(f"grad_wait_{names[-1]}_{l}", handle, after)
        return landed_of[gi]

    red, delta, new_m, new_v = {}, {}, {}, {}
    after = dx
    for n in ("w_mlp_out", "w_mlp_in", "w_out", "w_in"):
        mine = []
        for l in range(depth):
            gi, j = sent_in[(n, l)]
            got = landed_for(gi, after)[j]
            mine.append(_sum_owned([pending[gi][2][j]], [got], c, chip, [f"{n}_{l}"])[0])
        theirs = _share_halves(mine, name=f"grad_share_{n}")
        red[n], delta[n], new_m[n], new_v[n] = _adamw_owned(w[n], mine, theirs, m[n], v[n], c, name=f"adamw_{n}")
        after = delta[n]

    small = {n: jnp.stack([grads[l][n] for l in range(depth)]) for n in _SMALL[:-1] + ("conv_w",)}
    small["final_norm_g"] = g_final
    parts = [loss_part.reshape(1)] + [small[n] for n in _SMALL + ("conv_w",)]
    rows = -(-sum(p.size for p in parts) // 1024) * 8
    tot = _unpack(_exchange8(_pack(parts, rows), reduce=True, after=red[_BIG[0]], name="allreduce_small"), parts)
    loss = tot[0][0]
    red.update(zip(_SMALL + ("conv_w",), tot[1:]))
    red["conv_w"] = lax.dynamic_index_in_dim(red["conv_w"].reshape(depth, _CONV_K, n_chips, -1), chip, axis=2, keepdims=False)

    names = _SMALL + ("conv_w",)
    like = [w[n] for n in names]
    srows = -(-sum(p.size for p in like) // 1024) * 8
    res = _adamw(*[_pack([d[n] for n in names], srows) for d in (w, red, m, v)], name="adamw_small")
    for dst, buf in zip((delta, new_m, new_v), res):
        dst.update(zip(names, _unpack(buf, like)))
    return (loss, dx[None], *[red[n] for n in _ORDER], *[delta[n] for n in _ORDER], *[new_m[n] for n in _ORDER],
            *[new_v[n] for n in _ORDER])
```

```python
import numpy as np
import jax
import jax.numpy as jnp
from jax import lax
from jax.experimental import pallas as pl
from jax.experimental.pallas import tpu as pltpu

F32 = jnp.float32
_BF = jnp.bfloat16
_NEG = -1e30
_EPS = 1e-5
_HEADS = 16
_HDIM = 64
_AW = _HEADS * _HDIM
_ABLK = 128
_DILATIONS = (1, 4, 16)
_CHUNK = 128
_NSTATE = 128
_GROUPS = 2
_HPG = _HEADS // _GROUPS
_CONV_K = 4
_LANES = 128
_CHIPS = 4
_LR, _B1, _B2, _AEPS, _WD, _STEP = 0.001, 0.9, 0.999, 1e-08, 0.01, 10
_VMEM_CAP = 56 * 1024 * 1024
_MESH = pl.DeviceIdType.MESH
_SDS = jax.ShapeDtypeStruct
_NT = (((1,), (1,)), ((), ()))
_TN = (((0,), (0,)), ((), ()))


def _params(sem, est_bytes):
    lim = int(min(max(2 * est_bytes + (4 << 20), 16 << 20), _VMEM_CAP))
    return pltpu.CompilerParams(dimension_semantics=sem, vmem_limit_bytes=lim)


def _nbytes(shape, dtype):
    return int(np.prod(shape)) * jnp.dtype(dtype).itemsize


def _hbm(a):
    return pltpu.with_memory_space_constraint(a, pltpu.HBM)


def _dot(a, b, dims=(((1,), (0,)), ((), ()))):
    return lax.dot_general(a.astype(_BF), b.astype(_BF), dims, preferred_element_type=F32)


_HALO = 8


def _rows(fn, ins, consts, outs, sums=(), *, halos=(), into=None, after=None, tile, name):
    rows = (ins[0][0] if isinstance(ins[0], tuple) else ins[0]).shape[0]
    n_steps = rows // tile

    def norm_in(a):
        if not isinstance(a, tuple):
            return a, tile, a.shape[1], lambda i: (i, 0)
        if isinstance(a[0], str) and a[0] == "slot":
            return a[1], (None, tile, a[1].shape[2]), a[1].shape[2], lambda i, s=a[2]: (s, i, 0)
        if isinstance(a[0], str):
            return a[1], tile // a[2], a[1].shape[1], lambda i: (i, 0)
        return a[0], tile, a[1], a[2] if callable(a[2]) else (lambda i, j=a[2]: (i, j))
    ins = [norm_in(a) for a in ins]
    outs = [(w, dt, d[0] if d else 1) for w, dt, *d in outs]
    n_in, n_h, n_c, n_o, n_s = len(ins), len(halos), len(consts), len(outs), len(sums)
    n_x = int(into is not None and into[0] is not None) + int(after is not None)

    def body(*refs):
        step = pl.program_id(0)
        vals = [r[...] for r in refs[:n_in]]
        for r, (_, side) in zip(refs[n_in:n_in + n_h], halos):
            vals.append(jnp.where(step == (0 if side < 0 else n_steps - 1), 0.0, r[...]))
        vals += [r[...] for r in refs[n_in + n_h:n_in + n_h + n_c]]
        refs = refs[:n_in] + refs[n_in + n_h:]
        res = fn(*vals)
        res = res if isinstance(res, tuple) else (res,)
        orefs = refs[n_in + n_c + n_x:n_in + n_c + n_x + n_o]
        srefs = refs[n_in + n_c + n_x + n_o:]
        for r, v in zip(orefs, res[:n_o]):
            r[...] = v.astype(r.dtype)
        if n_s:
            @pl.when(pl.program_id(0) == 0)
            def _():
                for r in srefs:
                    r[...] = jnp.zeros_like(r)
            for r, v in zip(srefs, res[n_o:]):
                r[...] += v.reshape(tile // 8, 8, v.shape[-1]).sum(axis=0)

    per = tile // _HALO
    in_specs = [pl.BlockSpec(r if isinstance(r, tuple) else (r, w), idx) for _, r, w, idx in ins]
    in_specs += [pl.BlockSpec((_HALO, a.shape[1]), (lambda i: (jnp.maximum(i * per - 1, 0), 0)) if side < 0
                              else (lambda i: (jnp.minimum((i + 1) * per, rows // _HALO - 1), 0))) for a, side in halos]
    in_specs += [pl.BlockSpec(c.shape, lambda i, nd=c.ndim: (0,) * nd) for c in consts]
    out_shape = [_SDS((rows // d, d * w), dt) for w, dt, d in outs] + [_SDS((8, w), F32) for w in sums]
    out_specs = [pl.BlockSpec((tile // d, d * w), lambda i: (i, 0)) for w, _, d in outs]
    out_specs += [pl.BlockSpec((8, w), lambda i: (0, 0)) for w in sums]
    est = (sum(_nbytes((tile if isinstance(r, tuple) else r, w), a.dtype) for a, r, w, _ in ins)
           + sum(_nbytes((tile, w), dt) for w, dt, _ in outs))
    shared, aliases = [], {}
    if into is not None:
        buf, total, j = into
        out_shape[0] = _SDS((rows, total), outs[0][1])
        out_specs[0] = pl.BlockSpec((tile, outs[0][0]), lambda i: (i, j))
        if buf is not None:
            shared, aliases = [buf], {n_in + n_h + n_c: 0}
    if after is not None:
        shared.append(after)
    in_specs += [pl.BlockSpec(memory_space=pl.ANY)] * len(shared)
    return pl.pallas_call(body, grid=(n_steps,), in_specs=in_specs, out_specs=out_specs, out_shape=out_shape, name=name,
                          input_output_aliases=aliases, compiler_params=_params(("arbitrary",), 3 * est))(
                              *[_hbm(a[0]) for a in ins], *[_hbm(a) for a, _ in halos], *consts, *shared)


def _perm(d, tile):
    p = np.zeros((tile, tile), np.float32)
    t = np.arange(tile)
    p[t, (t % d) * (tile // d) + t // d] = 1.0
    return jnp.asarray(p, _BF)


def _unstride(s, p):
    d = p.shape[0] // s.shape[0]
    w = s.shape[1] // d
    return _dot(p, jnp.concatenate([s[:, r * w:(r + 1) * w] for r in range(d)], axis=0))


def _stride(x, p, d):
    z = _dot(p, x, _TN)
    n = x.shape[0] // d
    return jnp.concatenate([z[r * n:(r + 1) * n] for r in range(d)], axis=1)


def _shifted(u, halo, back):
    n = u.shape[0] + _HALO
    if back:
        ext = jnp.concatenate([halo, u], axis=0)
        return [pltpu.roll(ext, j, 0)[_HALO:] for j in (1, 2, 3)]
    ext = jnp.concatenate([u, halo], axis=0)
    return [pltpu.roll(ext, n - j, 0)[:u.shape[0]] for j in (1, 2, 3)]


def _tile_for(width):
    return max(c for c in (256, 128, 64, 32) if c * width <= (1 << 18) or c == 32)


_NORM_TILE = 256


def _rstd(x):
    return lax.rsqrt(jnp.mean(x * x, axis=-1, keepdims=True) + _EPS)


def _split(x, groups):
    w = x.shape[-1] // groups
    return [x[:, g * w:(g + 1) * w] for g in range(groups)]


def _cat(parts):
    return parts[0] if len(parts) == 1 else jnp.concatenate(parts, axis=-1)


def _rms_bwd_tile(x, dy, g, groups):
    dxs, dgs = [], []
    for xs, ds, gs in zip(_split(x, groups), _split(dy.astype(F32), groups), _split(g, groups)):
        r = _rstd(xs)
        xh = xs * r
        gd = ds * gs
        dxs.append(r * (gd - xh * jnp.mean(gd * xh, axis=-1, keepdims=True)))
        dgs.append(ds * xh)
    return _cat(dxs), _cat(dgs)


def _rms_fwd(x, g, *, groups=1, into=None, name):
    def fn(x, g):
        return _cat([xs * _rstd(xs) * gs for xs, gs in zip(_split(x, groups), _split(g, groups))])
    w = x.shape[1]
    return _rows(fn, [x], [g.reshape(1, w)], [(w, _BF)], into=into, tile=_NORM_TILE, name=name)[0]


def _rms_bwd(x, dy, g, res, *, name):
    def fn(x, dy, res, g):
        dx, dg = _rms_bwd_tile(x, dy, g, 1)
        return dx + res, dx + res, dg
    w = x.shape[1]
    dx, dxb, dg = _rows(fn, [x, dy, res], [g.reshape(1, w)], [(w, F32), (w, _BF)], [w], tile=_NORM_TILE, name=name)
    return dx, dxb, dg.sum(axis=0)


def _pick(n, cands):
    for c in cands:
        if n % c == 0:
            return c
    raise ValueError(f"no block size for {n}")


_MM_BLOCKS = (1024, 1152, 512, 384)


def _mm(a, b, *, ta=False, tb=False, extra=(), epi=None, outs=(F32,), after=None, b_chips=0, out_chips=0, b_cols=None, name):
    m, k = (a.shape[1], a.shape[0]) if ta else a.shape
    b_shape = (b.shape[1], b.shape[2] * b_chips) if b_chips else b.shape
    if b_cols is not None:
        b_shape = (b.shape[0], b_cols[1])
    n = b_shape[0] if tb else b_shape[1]
    assert k == (b_shape[1] if tb else b_shape[0])
    n_cap = n // max(out_chips, 1 if tb else b_chips, 1)
    k_cap = k // (b_chips if (b_chips and tb) else 1)
    bm, bn = _pick(m, _MM_BLOCKS), _pick(n_cap, _MM_BLOCKS)
    bk = _pick(k_cap, (2048, 1920) + _MM_BLOCKS)
    nk = k // bk
    n_e, n_o = len(extra), len(outs)
    behind = [] if after is None else [after]
    dims = (((0 if ta else 1,), (1 if tb else 0,)), ((), ()))

    def body(a_ref, b_ref, *rest):
        ex, orefs, acc = rest[:n_e], rest[n_e + len(behind):n_e + len(behind) + n_o], rest[-1]
        kk = pl.program_id(2)

        @pl.when(kk == 0)
        def _():
            acc[...] = jnp.zeros_like(acc)

        acc[...] += _dot(a_ref[...], b_ref[...], dims)

        @pl.when(kk == nk - 1)
        def _():
            r = acc[...]
            res = epi(r, *[e[...] for e in ex]) if epi is not None else (r,)
            for o, v in zip(orefs, res):
                o[...] = v.astype(o.dtype)

    a_spec = pl.BlockSpec((bk, bm), lambda i, j, kk: (kk, i)) if ta else pl.BlockSpec((bm, bk), lambda i, j, kk: (i, kk))
    if b_chips and tb:
        per = k_cap // bk
        b_spec = pl.BlockSpec((None, bn, bk), lambda i, j, kk: (kk // per, j, kk % per))
    elif b_chips:
        per = n_cap // bn
        b_spec = pl.BlockSpec((None, bk, bn), lambda i, j, kk: (j // per, kk, j % per))
    else:
        first = 0 if b_cols is None else b_cols[0] // bn
        assert b_cols is None or (not tb and b_cols[0] % bn == 0)
        b_spec = pl.BlockSpec((bn, bk), lambda i, j, kk: (j, kk)) if tb else pl.BlockSpec((bk, bn), lambda i, j, kk: (kk, first + j))
    t_spec = pl.BlockSpec((bm, bn), lambda i, j, kk: (i, j))
    o_spec, o_shape = t_spec, (m, n)
    if out_chips:
        per_o = n_cap // bn
        o_spec, o_shape = pl.BlockSpec((None, bm, bn), lambda i, j, kk: (j // per_o, i, j % per_o)), (out_chips, m, n_cap)
    est = (_nbytes((bm, bk), a.dtype) + _nbytes((bk, bn), b.dtype) + sum(_nbytes((bm, bn), e.dtype) for e in extra)
           + sum(_nbytes((bm, bn), o) for o in outs)) * 2 + 2 * _nbytes((bm, bn), F32)
    res = pl.pallas_call(
        body, grid=(m // bm, n // bn, nk), in_specs=[a_spec, b_spec] + [t_spec] * n_e + [pl.BlockSpec(memory_space=pl.ANY)] * len(behind),
        out_specs=[o_spec] * n_o, out_shape=[_SDS(o_shape, o) for o in outs], scratch_shapes=[pltpu.VMEM((bm, bn), F32)], name=name,
        compiler_params=_params(("parallel", "parallel", "arbitrary"), est))(_hbm(a), _hbm(b), *[_hbm(e) for e in extra], *behind)
    return res[0] if n_o == 1 else res


def _w_in_groups(cch):
    return (0, 3 * _AW), (3 * _AW, _AW), (4 * _AW, cch), (4 * _AW + cch, _HEADS)


def _w_in_from_chips(blocks, cch, *, name):
    chips, d, _ = blocks.shape
    qkv, z, xbc, dt = _w_in_groups(cch)
    order = (qkv, xbc, z, dt)
    total = 3 * _AW + cch + _AW + _LANES
    tile = 2 * _ABLK

    def body(b_ref, o_ref):
        full = jnp.concatenate([b_ref[k] for k in range(chips)], axis=1)
        parts = [full[:, s:s + w] for s, w in order]
        o_ref[...] = jnp.concatenate(parts + [jnp.zeros((tile, total - sum(w for _, w in order)), full.dtype)], axis=1)

    return pl.pallas_call(
        body, grid=(d // tile,), in_specs=[pl.BlockSpec((chips, tile, blocks.shape[2]), lambda i: (0, i, 0))],
        out_specs=pl.BlockSpec((tile, total), lambda i: (i, 0)), out_shape=_SDS((d, total), blocks.dtype), name=name,
        compiler_params=_params(("arbitrary",), 16 << 20))(_hbm(blocks))


def _w_in_to_chips(g_all, cch, chips, *, name):
    d = g_all.shape[0]
    qkv, z, xbc, dt = _w_in_groups(cch)
    n = dt[0] + dt[1]
    z0 = 3 * _AW + cch
    tile = 2 * _ABLK

    def body(g_ref, o_ref):
        v = g_ref[...]
        full = jnp.concatenate([v[:, :3 * _AW], v[:, z0:z0 + _AW], v[:, 3 * _AW:z0], v[:, z0 + _AW:z0 + _AW + _HEADS]], axis=1)
        for k in range(chips):
            o_ref[k] = full[:, k * (n // chips):(k + 1) * (n // chips)]

    return pl.pallas_call(
        body, grid=(d // tile,), in_specs=[pl.BlockSpec((tile, g_all.shape[1]), lambda i: (i, 0))],
        out_specs=pl.BlockSpec((chips, tile, n // chips), lambda i: (0, i, 0)), out_shape=_SDS((chips, d, n // chips), g_all.dtype),
        name=name, compiler_params=_params(("arbitrary",), 16 << 20))(_hbm(g_all))


def _add_to(acc, r):
    return (acc + r,)


def _alibi_bias(dilation):
    slopes = 2.0 ** (-8.0 * (np.arange(_HEADS) + 1) / _HEADS)
    i = np.arange(_ABLK)[:, None]
    j = np.arange(_ABLK)[None, :]
    cur = np.where(i - j >= 0, -slopes[:, None, None] * ((i - j) * dilation), _NEG)
    prev = np.where(j >= i, -slopes[:, None, None] * ((i - j + _ABLK) * dilation), _NEG)
    both = np.stack([np.concatenate([np.full_like(prev, _NEG), cur], axis=2), np.concatenate([prev, cur], axis=2)])
    return jnp.asarray(both.reshape(2, _HEADS // 2, 2 * _ABLK, 2 * _ABLK), F32)


def _bias_spec():
    return pl.BlockSpec((None, _HEADS // 2, 2 * _ABLK, 2 * _ABLK), lambda r, j: (jnp.minimum(j, 1), 0, 0, 0))


def _strided(a, d):
    return a.reshape(a.shape[0] // d, d * a.shape[1])


def _pair(pr):
    return slice(pr * _LANES, (pr + 1) * _LANES)


def _low_lanes(shape):
    return lax.broadcasted_iota(jnp.int32, shape, 1) < _HDIM


def _halves(v, low):
    z = jnp.zeros_like(v)
    return jnp.where(low, v, z), jnp.where(low, z, v)


def _lane_spec(nb):
    return pl.BlockSpec((_ABLK, _LANES), lambda r, j: (jnp.minimum(j, nb - 1), r))


def _expand_heads(v):
    low = _low_lanes((v.shape[0], _LANES))
    return jnp.concatenate([jnp.where(low, v[:, 2 * pr:2 * pr + 1], v[:, 2 * pr + 1:2 * pr + 2]) for pr in range(_HEADS // 2)], axis=1)


def _attn_specs(nb, n_parts):
    def cur(p):
        return pl.BlockSpec((_ABLK, _AW), lambda r, j: (jnp.minimum(j, nb - 1), r * n_parts + p))

    def prev(p):
        return pl.BlockSpec((_ABLK, _AW), lambda r, j: (jnp.clip(j - 1, 0, nb - 1), r * n_parts + p))
    return cur, prev


def _attn_fwd(qkv, dilation, *, name):
    t = qkv.shape[0] * dilation
    nb = t // dilation // _ABLK
    bias = _alibi_bias(dilation)
    scale = _HDIM ** -0.5

    def body(q_ref, kc_ref, kp_ref, vc_ref, vp_ref, b_ref, o_ref, l_ref):
        low = _low_lanes((_ABLK, _LANES))
        l_ref[...] = jnp.zeros_like(l_ref)
        for pr in range(_HEADS // 2):
            sl = _pair(pr)
            k2 = jnp.concatenate([kp_ref[:, sl], kc_ref[:, sl]], axis=0)
            v2 = jnp.concatenate([vp_ref[:, sl], vc_ref[:, sl]], axis=0)
            q2 = jnp.concatenate(_halves(q_ref[:, sl] * scale, low), axis=0)
            s = _dot(q2, k2, _NT) + b_ref[pr]
            m = jnp.max(s, axis=-1, keepdims=True)
            p = jnp.exp(s - m)
            den = jnp.sum(p, axis=-1, keepdims=True)
            o = _dot(p, v2) / den
            lse = m + jnp.log(den)
            l_ref[:, 2 * pr:2 * pr + 1] = lse[:_ABLK]
            l_ref[:, 2 * pr + 1:2 * pr + 2] = lse[_ABLK:]
            o_ref[:, sl] = jnp.where(low, o[:_ABLK], o[_ABLK:]).astype(o_ref.dtype)

    cur, prev = _attn_specs(nb, 3)
    cur1, _ = _attn_specs(nb, 1)
    bspec = _bias_spec()
    sv = _hbm(qkv)
    o, l = pl.pallas_call(
        body, grid=(dilation, nb), in_specs=[cur(0), cur(1), prev(1), cur(2), prev(2), bspec],
        out_specs=[cur1(0), _lane_spec(nb)],
        out_shape=[_SDS((t // dilation, dilation * _AW), _BF), _SDS((t // dilation, dilation * _LANES), F32)], name=name,
        compiler_params=_params(("parallel", "arbitrary"), 16 << 20))(sv, sv, sv, sv, sv, bias)
    return o, l.reshape(t, _LANES)


def _attn_bwd(qkv, do, ld, dilation, *, name):
    t = qkv.shape[0] * dilation
    nb = t // dilation // _ABLK
    bias = _alibi_bias(dilation)
    scale = _HDIM ** -0.5

    def body(q_ref, kc_ref, kp_ref, vc_ref, vp_ref, do_ref, ld_ref, b_ref, dq_ref, dk_ref, dv_ref, ck, cv):
        n = pl.program_id(1)

        @pl.when(n == 0)
        def _():
            ck[...] = jnp.zeros_like(ck)
            cv[...] = jnp.zeros_like(cv)

        @pl.when(n < nb)
        def _():
            low = _low_lanes((_ABLK, _LANES))
            for pr in range(_HEADS // 2):
                sl = _pair(pr)
                k2 = jnp.concatenate([kp_ref[:, sl], kc_ref[:, sl]], axis=0)
                v2 = jnp.concatenate([vp_ref[:, sl], vc_ref[:, sl]], axis=0)
                q2 = jnp.concatenate(_halves(q_ref[:, sl] * scale, low), axis=0)
                do2 = jnp.concatenate(_halves(do_ref[:, sl], low), axis=0)
                lrow = jnp.concatenate([ld_ref[:, 2 * pr:2 * pr + 1], ld_ref[:, 2 * pr + 1:2 * pr + 2]], axis=0)
                dsum = jnp.concatenate([ld_ref[:, _HEADS + 2 * pr:_HEADS + 2 * pr + 1],
                                        ld_ref[:, _HEADS + 2 * pr + 1:_HEADS + 2 * pr + 2]], axis=0)
                p = jnp.exp(_dot(q2, k2, _NT) + b_ref[pr] - lrow)
                ds = (p * (_dot(do2, v2, _NT) - dsum)).astype(_BF)
                dq = _dot(ds, k2)
                dk2, dv2 = _dot(ds, q2, _TN), _dot(p, do2, _TN)
                dq_ref[:, sl] = (jnp.where(low, dq[:_ABLK], dq[_ABLK:]) * scale).astype(dq_ref.dtype)
                dk_ref[:, sl] = (ck[:, sl] + dk2[:_ABLK]).astype(dk_ref.dtype)
                dv_ref[:, sl] = (cv[:, sl] + dv2[:_ABLK]).astype(dv_ref.dtype)
                ck[:, sl] = dk2[_ABLK:]
                cv[:, sl] = dv2[_ABLK:]

        @pl.when(n == nb)
        def _():
            dk_ref[...] = ck[...].astype(dk_ref.dtype)
            dv_ref[...] = cv[...].astype(dv_ref.dtype)

    cur, prev = _attn_specs(nb, 3)
    cur1, prev1 = _attn_specs(nb, 1)
    bspec = _bias_spec()
    sv, dov, ldv = _hbm(qkv), _hbm(do), _hbm(_strided(ld, dilation))
    dqkv = pl.pallas_call(
        body, grid=(dilation, nb + 1),
        in_specs=[cur(0), cur(1), prev(1), cur(2), prev(2), cur1(0), _lane_spec(nb), bspec],
        out_specs=[cur1(0), prev1(0), prev1(0)], out_shape=[_SDS(dov.shape, _BF)] * 3, name=name,
        scratch_shapes=[pltpu.VMEM((_ABLK, _AW), F32)] * 2,
        compiler_params=_params(("parallel", "arbitrary"), 16 << 20))(sv, sv, sv, sv, sv, dov, ldv, bias)
    return dqkv


def _ssd_in_specs(ch):
    return dict(
        xs=pl.BlockSpec((_CHUNK, _AW), lambda c: (ch(c), 0)),
        bc=pl.BlockSpec((_CHUNK, 2 * _GROUPS * _NSTATE), lambda c: (ch(c), _AW // (2 * _GROUPS * _NSTATE))),
        lane=pl.BlockSpec((_CHUNK, _LANES), lambda c: (ch(c), 0)),
        arow=pl.BlockSpec((_HEADS, 1, _CHUNK), lambda c: (0, 0, ch(c))),
        st=pl.BlockSpec((1, _HEADS // 2, _NSTATE, _LANES), lambda c: (ch(c), 0, 0, 0)),
    )


def _decay(a_col, a_row):
    i0 = lax.broadcasted_iota(jnp.int32, (_CHUNK, _CHUNK), 0)
    i1 = lax.broadcasted_iota(jnp.int32, (_CHUNK, _CHUNK), 1)
    return jnp.where(i0 >= i1, jnp.exp(a_col - a_row), 0.0), jnp.where(i1 >= i0, jnp.exp(a_row - a_col), 0.0)


def _rsum(v):
    return jnp.sum(v, axis=-1, keepdims=True)


def _ssd_fwd(act, dt, acum, a_row, *, name):
    t = act.shape[0]
    nc = t // _CHUNK
    sp = _ssd_in_specs(lambda c: c)
    gw = _GROUPS * _NSTATE

    def body(xs_ref, bc_ref, dt_ref, ac_ref, ar_ref, y_ref, sall_ref, st):
        @pl.when(pl.program_id(0) == 0)
        def _():
            st[...] = jnp.zeros_like(st)

        low = _low_lanes((_CHUNK, _LANES))
        for g in range(_GROUPS):
            bg = bc_ref[:, g * _NSTATE:(g + 1) * _NSTATE]
            cg = bc_ref[:, gw + g * _NSTATE:gw + (g + 1) * _NSTATE].astype(_BF)
            cb = _dot(cg, bg, _NT)
            for pr in range(g * _HPG // 2, (g + 1) * _HPG // 2):
                ha, hb = 2 * pr, 2 * pr + 1
                a_a, a_b = ac_ref[:, ha:ha + 1], ac_ref[:, hb:hb + 1]
                x = (xs_ref[:, _pair(pr)] * jnp.where(low, dt_ref[:, ha:ha + 1], dt_ref[:, hb:hb + 1])).astype(_BF)
                lm_a, _ = _decay(a_a, ar_ref[ha])
                lm_b, _ = _decay(a_b, ar_ref[hb])
                sv = st[pr]
                sall_ref[0, pr] = sv
                yd = _dot(jnp.concatenate([cb * lm_a, cb * lm_b], axis=0), x)
                yd = jnp.where(low, yd[:_CHUNK], yd[_CHUNK:])
                y_ref[:, _pair(pr)] = yd + jnp.where(low, jnp.exp(a_a), jnp.exp(a_b)) * _dot(cg, sv)
                al_a, al_b = jnp.min(a_a, axis=0, keepdims=True), jnp.min(a_b, axis=0, keepdims=True)
                upd = _dot(jnp.concatenate([bg * jnp.exp(al_a - a_a), bg * jnp.exp(al_b - a_b)], axis=1), x, _TN)
                st[pr] = jnp.where(low, jnp.exp(al_a), jnp.exp(al_b)) * sv + jnp.where(low, upd[:_NSTATE], upd[_NSTATE:])

    return pl.pallas_call(
        body, grid=(nc,), in_specs=[sp['xs'], sp['bc'], sp['lane'], sp['lane'], sp['arow']],
        out_specs=[sp['xs'], sp['st']], out_shape=[_SDS((t, _AW), F32), _SDS((nc, _HEADS // 2, _NSTATE, _LANES), F32)],
        scratch_shapes=[pltpu.VMEM((_HEADS // 2, _NSTATE, _LANES), F32)], name=name,
        compiler_params=_params(("arbitrary",), 16 << 20))(*[_hbm(a) for a in (act, act, dt, acum, a_row)])


def _ssd_bwd(act, dt, acum, a_row, sall, dy, *, name):
    t = act.shape[0]
    nc = t // _CHUNK
    sp = _ssd_in_specs(lambda c: nc - 1 - c)
    gw = _GROUPS * _NSTATE

    def body(xs_ref, bc_ref, dt_ref, ac_ref, ar_ref, sall_ref, dy_ref, dxs_ref, dbc_ref, ddt_ref, da_ref, dst):
        @pl.when(pl.program_id(0) == 0)
        def _():
            dst[...] = jnp.zeros_like(dst)

        ddt_ref[...] = jnp.zeros_like(ddt_ref)
        da_ref[...] = jnp.zeros_like(da_ref)
        row = lax.broadcasted_iota(jnp.int32, (_CHUNK, 1), 0)
        low = _low_lanes((_CHUNK, _LANES))
        for g in range(_GROUPS):
            bg = bc_ref[:, g * _NSTATE:(g + 1) * _NSTATE]
            bgb = bg.astype(_BF)
            cg = bc_ref[:, gw + g * _NSTATE:gw + (g + 1) * _NSTATE].astype(_BF)
            cb, cbt = _dot(cg, bgb, _NT), _dot(bgb, cg, _NT)
            dcb = jnp.zeros((_CHUNK, _CHUNK), F32)
            dbg = jnp.zeros((_CHUNK, _NSTATE), F32)
            dcg = jnp.zeros((_CHUNK, _NSTATE), F32)
            for pr in range(g * _HPG // 2, (g + 1) * _HPG // 2):
                heads = (2 * pr, 2 * pr + 1)
                a_cols = [ac_ref[:, h:h + 1] for h in heads]
                dt_pair = jnp.where(low, dt_ref[:, heads[0]:heads[0] + 1], dt_ref[:, heads[1]:heads[1] + 1])
                xsv = xs_ref[:, _pair(pr)]
                x = xsv * dt_pair
                xb = x.astype(_BF)
                xhs = _halves(xb, low)
                dyv = dy_ref[:, _pair(pr)]
                dyb = dyv.astype(_BF)
                dyhs = _halves(dyb, low)
                sv, dsv = sall_ref[0, pr], dst[pr]
                svb, dsb = sv.astype(_BF), dsv.astype(_BF)
                a_lasts = [jnp.min(a, axis=0, keepdims=True) for a in a_cols]
                e_pair = jnp.where(low, jnp.exp(a_cols[0]), jnp.exp(a_cols[1]))
                el_pair = jnp.where(low, jnp.exp(a_lasts[0]), jnp.exp(a_lasts[1]))
                yo = e_pair * _dot(cg, svb)
                decays = [_decay(a_cols[i], ar_ref[h]) for i, h in enumerate(heads)]
                gms, gmts = [cb * lm for lm, _ in decays], [cbt * lmt for _, lmt in decays]
                w_cols = [jnp.exp(a_lasts[i] - a_cols[i]) for i in range(2)]
                x2, dy2 = jnp.concatenate(xhs, axis=0), jnp.concatenate(dyhs, axis=0)
                bwd = _dot(jnp.concatenate([bg * w_cols[0], bg * w_cols[1]], axis=0), dsb)
                dxg = _dot(jnp.concatenate(gms, axis=1), dyb, _TN)
                dg2, dgt2, xds2 = _dot(dy2, xb, _NT), _dot(x2, dyb, _NT), _dot(x2, dsb, _NT)
                das = []
                for i in range(2):
                    rows_i = slice(i * _CHUNK, (i + 1) * _CHUNK)
                    dcb = dcb + dg2[rows_i] * decays[i][0]
                    dbg = dbg + w_cols[i] * xds2[rows_i]
                    das.append(_rsum(dg2[rows_i] * gms[i]) - _rsum(dgt2[rows_i] * gmts[i]))
                bwd = jnp.where(low, bwd[:_CHUNK], bwd[_CHUNK:])
                dx = jnp.where(low, dxg[:_CHUNK], dxg[_CHUNK:]) + bwd
                edy = (e_pair * dyv).astype(_BF)
                dcg = dcg + _dot(edy, svb, _NT)
                zs, yos, sds, dts = (_halves(v, low) for v in (x * bwd, dyv * yo, sv * dsv, dx * xsv))
                for i, h in enumerate(heads):
                    z = _rsum(zs[i])
                    da_last = jnp.sum(z, axis=0, keepdims=True) + jnp.exp(a_lasts[i]) * jnp.sum(_rsum(sds[i]), axis=0, keepdims=True)
                    da_ref[:, h:h + 1] = das[i] + _rsum(yos[i]) - z + jnp.where(row == _CHUNK - 1, da_last, 0.0)
                    ddt_ref[:, h:h + 1] = _rsum(dts[i])
                dxs_ref[:, _pair(pr)] = dx * dt_pair
                dst[pr] = el_pair * dsv + _dot(cg, edy, _TN)
            dbc_ref[:, g * _NSTATE:(g + 1) * _NSTATE] = dbg + _dot(dcb, cg, _TN)
            dbc_ref[:, gw + g * _NSTATE:gw + (g + 1) * _NSTATE] = dcg + _dot(dcb, bgb)

    ch = lambda c: nc - 1 - c
    wide = pl.BlockSpec((_CHUNK, 2 * gw), lambda c: (ch(c), 0))
    return pl.pallas_call(
        body, grid=(nc,), in_specs=[sp['xs'], sp['bc'], sp['lane'], sp['lane'], sp['arow'], sp['st'], sp['xs']],
        out_specs=[sp['xs'], wide, sp['lane'], sp['lane']],
        out_shape=[_SDS((t, _AW), F32), _SDS((t, 2 * gw), F32), _SDS((t, _LANES), F32), _SDS((t, _LANES), F32)],
        scratch_shapes=[pltpu.VMEM((_HEADS // 2, _NSTATE, _LANES), F32)], name=name,
        compiler_params=_params(("arbitrary",), 16 << 20))(*[_hbm(a) for a in (act, act, dt, acum, a_row, sall, dy)])


def _scan_rows(v, reverse):
    r = lax.broadcasted_iota(jnp.int32, v.shape, 0)
    for s in (1, 2, 4, 8, 16, 32, 64):
        if reverse:
            v = v + jnp.where(r < _CHUNK - s, pltpu.roll(v, _CHUNK - s, 0), 0.0)
        else:
            v = v + jnp.where(r >= s, pltpu.roll(v, s, 0), 0.0)
    return v


def _softplus(x):
    return jnp.maximum(x, 0.0) + jnp.log(1.0 + jnp.exp(-jnp.abs(x)))


def _sigmoid(x):
    return 1.0 / (1.0 + jnp.exp(-x))


def _silu(x):
    return x * _sigmoid(x)


def _dsilu(x):
    s = _sigmoid(x)
    return s * (1.0 + x * (1.0 - s))


def _lanes(a):
    return jnp.pad(a, (0, _LANES - a.shape[0])).reshape(1, _LANES)


def _layer_fwd(x, p, l):
    cch = p['conv_w'].shape[1]
    sv = {}
    h1 = _rms_fwd(x, p['ln1_g'], name=f"ln1_fwd_{l}")
    qkv = _mm(h1, p['w_in'], b_cols=(0, 3 * _AW), outs=(_BF,), name=f"in_proj_qkv_{l}")
    xbc = _mm(h1, p['w_in'], b_cols=(3 * _AW, cch), name=f"in_proj_xbc_{l}")
    zdt = _mm(h1, p['w_in'], b_cols=(3 * _AW + cch, _AW + _LANES), name=f"in_proj_zdt_{l}")
    z, dt_raw = (zdt, _AW, 0), (zdt, _LANES, _AW // _LANES)

    tile = 2 * _ABLK
    perms = [_perm(d, tile) for d in _DILATIONS[1:]]
    views = [qkv] + list(_rows(lambda a, p2, p3: (_stride(a, p2, _DILATIONS[1]), _stride(a, p3, _DILATIONS[2])), [qkv], perms,
                               [(3 * _AW, _BF, d) for d in _DILATIONS[1:]], tile=tile, name=f"qkv_strided_{l}"))
    outs = []
    for dil, view in zip(_DILATIONS, views):
        outs += _attn_fwd(view, dil, name=f"attn_fwd_d{dil}_{l}")

    def combine(o1, l1, o2, l2, o3, l3, p2, p3):
        m = jnp.maximum(jnp.maximum(l1, l2), l3)
        e1, e2, e3 = jnp.exp(l1 - m), jnp.exp(l2 - m), jnp.exp(l3 - m)
        tot = e1 + e2 + e3
        mixed = sum(_expand_heads(e / tot) * o for e, o in ((e1, o1.astype(F32)), (e2, _unstride(o2, p2)), (e3, _unstride(o3, p3))))
        return mixed, m + jnp.log(tot)
    outs = [a if i % 2 or i == 0 else ("strided", a, _DILATIONS[i // 2]) for i, a in enumerate(outs)]
    attn, lse = _rows(combine, outs, perms, [(_AW, F32), (_LANES, F32)], tile=tile, name=f"attn_combine_{l}")
    mix = _rms_fwd(attn, p['attn_norm_g'], into=(None, 2 * _AW, 0), name=f"attn_norm_fwd_{l}")

    def conv(u0, before, w, b):
        u1, u2, u3 = _shifted(u0, before, True)
        return _silu(w[0:1] * u3 + w[1:2] * u2 + w[2:3] * u1 + w[3:4] * u0 + b)
    act = _rows(conv, [xbc], [p['conv_w'], p['conv_b'].reshape(1, cch)], [(cch, F32)], halos=[(xbc, -1)], tile=_tile_for(cch),
                name=f"conv_fwd_{l}")[0]

    def dtf(raw, bias, alog):
        dt = _softplus(raw + bias)
        return dt, _scan_rows(dt * -jnp.exp(alog), False)
    dt, acum = _rows(dtf, [dt_raw], [_lanes(p['dt_bias']), _lanes(p['a_log'])], [(_LANES, F32), (_LANES, F32)],
                     tile=_CHUNK, name=f"dt_fwd_{l}")
    a_row = acum[:, :_HEADS].T[:, None, :]
    y_ssd, sall = _ssd_fwd(act, dt, acum, a_row, name=f"ssd_fwd_{l}")
    dskip = jnp.repeat(p['d_skip'], _HDIM).reshape(1, _AW)
    xs = (act, _AW, 0)

    def gate(y, xs, z, dsk):
        return (y + dsk * xs) * _silu(z)
    y2 = _rows(gate, [y_ssd, xs, z], [dskip], [(_AW, F32)], tile=_tile_for(_AW), name=f"gate_fwd_{l}")[0]
    mix = _rms_fwd(y2, p['ssd_norm_g'], groups=_GROUPS, into=(mix, 2 * _AW, 1), name=f"ssd_norm_fwd_{l}")
    sv.update(x=x, h1=h1, qkv=views, zdt=zdt, xbc=xbc, attn=attn, lse=lse, act=act, dt=dt, acum=acum, a_row=a_row,
              sall=sall, y_ssd=y_ssd, dskip=dskip, y2=y2, mix=mix)
    return mix, sv


def _layer_fwd_mlp(p, sv, l):
    x2 = _mm(sv['mix'], p['w_out'], extra=(sv['x'],), epi=_add_to, name=f"out_proj_{l}")
    h2 = _rms_fwd(x2, p['ln2_g'], name=f"ln2_fwd_{l}")
    a = _mm(h2, p['w_mlp_in'], b_chips=_CHIPS, epi=lambda acc: (jnp.square(jnp.maximum(acc, 0.0)),), outs=(_BF,), name=f"mlp_in_{l}")
    x3 = _mm(a, p['w_mlp_out'], extra=(x2,), epi=_add_to, name=f"mlp_out_{l}")
    sv.update(x2=x2, h2=h2, a=a)
    return x3


def _layer_bwd(dx3, dx3b, p, sv, l, send, after):
    cch = p['conv_w'].shape[1]
    g = {}
    du = _mm(dx3b, p['w_mlp_out'], tb=True, extra=(sv['a'],), outs=(_BF,), after=after,
             epi=lambda acc, a: (acc * 2.0 * jnp.sqrt(a.astype(F32)),), name=f"mlp_out_dx_{l}")
    g['w_mlp_out'] = _mm(sv['a'], dx3b, ta=True, outs=(_BF,), name=f"mlp_out_dw_{l}")
    g['w_mlp_in'] = _mm(sv['h2'], du, ta=True, out_chips=_CHIPS, outs=(_BF,), name=f"mlp_in_dw_{l}")
    sent = send(('w_mlp_out', 'w_mlp_in'), g)
    dh2 = _mm(du, p['w_mlp_in'], tb=True, b_chips=_CHIPS, after=sent, name=f"mlp_in_dx_{l}")
    dx2, dx2b, g['ln2_g'] = _rms_bwd(sv['x2'], dh2, p['ln2_g'], dx3, name=f"ln2_bwd_{l}")
    dmix = _mm(dx2b, p['w_out'], tb=True, name=f"out_proj_dx_{l}")
    g['w_out'] = _mm(sv['mix'], dx2b, ta=True, outs=(_BF,), name=f"out_proj_dw_{l}")
    after_out = send(('w_out',), g)

    tile = 2 * _ABLK
    perms = [_perm(d, tile) for d in _DILATIONS[1:]]

    def norm_bwd(attn, dy, lse, gn, p2, p3):
        dattn, dgn = _rms_bwd_tile(attn, dy, gn, 1)
        prod, low = dattn * attn, _low_lanes((attn.shape[0], _LANES))
        lane = lax.broadcasted_iota(jnp.int32, lse.shape, 1)
        ld = jnp.where(lane < _HEADS, lse, 0.0)
        for pr in range(_HEADS // 2):
            for i, part in enumerate(_halves(prod[:, _pair(pr)], low)):
                ld = jnp.where(lane == _HEADS + 2 * pr + i, _rsum(part), ld)
        return dattn, _stride(dattn, p2, _DILATIONS[1]), _stride(dattn, p3, _DILATIONS[2]), ld, dgn
    *dos, ld, gn_sum = _rows(norm_bwd, [sv['attn'], (dmix, _AW, 0), sv['lse']], [p['attn_norm_g'].reshape(1, _AW)] + perms,
                             [(_AW, _BF)] + [(_AW, _BF, d) for d in _DILATIONS[1:]] + [(_LANES, F32)], [_AW], after=after_out,
                             tile=tile, name=f"attn_norm_bwd_{l}")
    g['attn_norm_g'] = gn_sum.sum(axis=0)
    parts = [_attn_bwd(view, do, ld, dil, name=f"attn_bwd_d{dil}_{l}") for view, do, dil in zip(sv['qkv'], dos, _DILATIONS)]

    def branch_sum(*t):
        parts_, (p2, p3) = t[:9], t[9:]
        t = [a.astype(F32) for a in parts_[:3]] + [_unstride(a, p2) for a in parts_[3:6]] + [_unstride(a, p3) for a in parts_[6:]]
        return jnp.concatenate([t[i] + t[3 + i] + t[6 + i] for i in range(3)], axis=1)
    branch_ins = list(parts[0]) + [("strided", a, d) for pr, d in zip(parts[1:], _DILATIONS[1:]) for a in pr]
    w_all = 3 * _AW + cch + _AW + _LANES
    dproj = _rows(branch_sum, branch_ins, perms, [(3 * _AW, _BF)], into=(None, w_all, 0), tile=tile, name=f"attn_bwd_sum_{l}")[0]

    xs, z, dt_raw = (sv['act'], _AW, 0), (sv['zdt'], _AW, 0), (sv['zdt'], _LANES, _AW // _LANES)

    def gate_bwd(y2, dy, y, xs, z, dsk, gn):
        dy2, dgn = _rms_bwd_tile(y2, dy, gn, _GROUPS)
        dy1 = dy2 * _silu(z)
        return dy1, dsk * dy1, dy2 * (y + dsk * xs) * _dsilu(z), dy1 * xs, dgn
    dy1, dxs_skip, dz, dsk_sum, gn_sum = _rows(
        gate_bwd, [sv['y2'], (dmix, _AW, 1), sv['y_ssd'], xs, z], [sv['dskip'], p['ssd_norm_g'].reshape(1, _AW)],
        [(_AW, F32), (_AW, F32), (_AW, _BF)], [_AW, _AW], tile=128, name=f"gate_bwd_{l}")
    g['ssd_norm_g'] = gn_sum.sum(axis=0)
    g['d_skip'] = dsk_sum.sum(axis=0).reshape(_HEADS, _HDIM).sum(axis=1)
    dxs, dbc, ddt, da = _ssd_bwd(sv['act'], sv['dt'], sv['acum'], sv['a_row'], sv['sall'], dy1, name=f"ssd_bwd_{l}")

    def dtb(da, ddtx, raw, dt, dz, bias, alog):
        a = -jnp.exp(alog)
        dda = _scan_rows(da, True)
        draw = (dda * a + ddtx) * _sigmoid(raw + bias)
        return jnp.concatenate([dz, draw.astype(dz.dtype)], axis=1), draw, dda * dt * a
    dproj, dbias, dalog = _rows(dtb, [da, ddt, dt_raw, sv['dt'], dz], [_lanes(p['dt_bias']), _lanes(p['a_log'])],
                                [(_AW + _LANES, _BF)], [_LANES, _LANES], into=(dproj, w_all, (3 * _AW + cch) // (_AW + _LANES)),
                                tile=_CHUNK, name=f"dt_bwd_{l}")
    g['dt_bias'], g['a_log'] = dbias.sum(axis=0)[:_HEADS], dalog.sum(axis=0)[:_HEADS]
    def conv_bwd1(u0, dxs, dbc, dxk, before, w, b):
        u1, u2, u3 = _shifted(u0, before, True)
        pre = w[0:1] * u3 + w[1:2] * u2 + w[2:3] * u1 + w[3:4] * u0 + b
        dp = jnp.concatenate([dxs + dxk, dbc], axis=1) * _dsilu(pre)
        return dp, dp * u3, dp * u2, dp * u1, dp * u0, dp
    dpre, *dws = _rows(conv_bwd1, [sv['xbc'], dxs, dbc, dxs_skip], [p['conv_w'], p['conv_b'].reshape(1, cch)], [(cch, F32)],
                       [cch] * 5, halos=[(sv['xbc'], -1)], tile=128, name=f"conv_bwd_pre_{l}")
    g['conv_w'] = jnp.stack([dws[i].sum(axis=0) for i in range(_CONV_K)])
    g['conv_b'] = dws[4].sum(axis=0)

    def conv_bwd2(p0, after_, w):
        p1, p2, p3 = _shifted(p0, after_, False)
        return w[3:4] * p0 + w[2:3] * p1 + w[1:2] * p2 + w[0:1] * p3
    dproj = _rows(conv_bwd2, [dpre], [p['conv_w']], [(cch, _BF)], halos=[(dpre, 1)], into=(dproj, w_all, 3 * _AW // cch),
                  tile=_tile_for(cch), name=f"conv_bwd_in_{l}")[0]
    g_all = _mm(sv['h1'], dproj, ta=True, outs=(_BF,), name=f"in_proj_dw_{l}")
    g['w_in'] = _w_in_to_chips(g_all, cch, _CHIPS, name=f"w_in_by_chip_{l}")
    sent = send(('w_in',), g)
    for n in _BIG:
        del g[n]
    dh1 = _mm(dproj, p['w_in'], tb=True, after=sent, name=f"in_proj_dx_{l}")
    dx, dxb, g['ln1_g'] = _rms_bwd(sv['x'], dh1, p['ln1_g'], dx2, name=f"ln1_bwd_{l}")
    return dx, dxb, g


def _loss_bwd(x, g, tgt):
    w = x.shape[1]
    tile = _NORM_TILE

    def fn(x, tgt, g):
        r = _rstd(x)
        xh = x * r
        e = xh * g - tgt
        gd = e * (g / w)
        dx = r * (gd - xh * jnp.mean(gd * xh, axis=-1, keepdims=True))
        rowloss = 0.5 * jnp.mean(e * e, axis=-1, keepdims=True)
        return dx, dx, (e / w) * xh, jnp.broadcast_to(rowloss, (tile, _LANES))
    dx, dxb, dg, ls = _rows(fn, [x, tgt], [g.reshape(1, w)], [(w, F32), (w, _BF)], [w, _LANES], tile=tile, name="loss_head")
    return dx, dxb, dg.sum(axis=0), ls[:, 0].sum()


def _adamw_math(w, g, m, v):
    m2 = _B1 * m + (1.0 - _B1) * g
    v2 = _B2 * v + (1.0 - _B2) * jnp.square(g)
    m_hat = m2 / (1.0 - _B1 ** _STEP)
    v_hat = v2 / (1.0 - _B2 ** _STEP)
    return -_LR * (m_hat / (jnp.sqrt(v_hat) + _AEPS) + _WD * w), m2, v2


def _adamw(w, g, m, v, *, name):
    width = w.shape[-1]
    flat = [a.reshape(-1, width) for a in (w, g, m, v)]
    tile = _pick(flat[0].shape[0], (_tile_for(width), 32, 8))
    res = _rows(_adamw_math, flat, [], [(width, F32)] * 3, tile=tile, name=name)
    return [r.reshape(w.shape) for r in res]


_HBM = pl.BlockSpec(memory_space=pltpu.HBM)


def _place():
    x, y, c = lax.axis_index("x"), lax.axis_index("y"), lax.axis_index("c")
    other_chips = [(1 - x, y), (x, 1 - y), (1 - x, 1 - y)]
    return x, y, c, other_chips


def _remote(src, dst, sems, i, dev):
    return pltpu.make_async_remote_copy(src_ref=src, dst_ref=dst, send_sem=sems[0].at[i], recv_sem=sems[1].at[i],
                                        device_id=dev, device_id_type=_MESH)


def _exchange8(v, *, reduce, after=None, name):
    r, w = v.shape
    behind = [] if after is None else [after]

    def body(v_ref, *rest):
        all_ref, rest = rest[len(behind)], rest[len(behind) + 1:]
        sems = rest[-2:]
        x, y, c, _ = _place()
        me = 4 * x + 2 * y + c
        all_ref[me] = v_ref[...]
        flips = [((d >> 2) & 1, (d >> 1) & 1, d & 1) for d in range(1, 8)]
        sends = [_remote(v_ref, all_ref.at[me], sems, i, (x ^ fx, y ^ fy, c ^ fc)) for i, (fx, fy, fc) in enumerate(flips)]
        for cp in sends:
            cp.start()
        for i, (fx, fy, fc) in enumerate(flips):
            _remote(v_ref, all_ref.at[me ^ (4 * fx + 2 * fy + fc)], sems, i, (x ^ fx, y ^ fy, c ^ fc)).wait_recv()
        for cp in sends:
            cp.wait_send()
        if reduce:
            acc = all_ref[0]
            for s in range(1, 8):
                acc = acc + all_ref[s]
            rest[0][...] = acc

    vm = pl.BlockSpec(memory_space=pltpu.VMEM)
    out_shape = [_SDS((8, r, w), v.dtype)] + ([_SDS((r, w), v.dtype)] if reduce else [])
    res = pl.pallas_call(body, in_specs=[vm] + [_ANY] * len(behind), out_specs=[vm] * len(out_shape), out_shape=out_shape, name=name,
                         scratch_shapes=[pltpu.SemaphoreType.DMA((7,)), pltpu.SemaphoreType.DMA((7,))],
                         compiler_params=pltpu.CompilerParams(vmem_limit_bytes=int(32 << 20)))(v, *behind)
    return res[1] if reduce else res[0]


_SEM = pl.BlockSpec(memory_space=pltpu.SEMAPHORE)
_ANY = pl.BlockSpec(memory_space=pl.ANY)
_EFFECT = pltpu.SideEffectType.DATAFLOW_SIDE_EFFECTING


def _send_start(name, srcs, land_shapes, plan, n_sends, after):
    ns, nl = len(srcs), len(land_shapes)
    zones = [_hbm(lax.empty(s.shape, s.dtype)) if isinstance(s, _SDS) else s for s in land_shapes]

    def body(*refs):
        ins, lands, sems = refs[:ns], refs[ns:ns + nl], refs[ns + nl + 1:ns + nl + 3]
        x, y, c, chips = _place()
        for i, (s, d, dev) in enumerate(plan(x, y, c, chips, ins, lands)[0]):
            _remote(s, d, sems, i, dev).start()
        refs[-1][...] = jnp.zeros_like(refs[-1])

    sem = pltpu.SemaphoreType.DMA((n_sends,))
    res = pl.pallas_call(
        body, name=name, in_specs=[_HBM] * (ns + nl) + [_ANY],
        out_shape=(sem, sem, *[pltpu.HBM(s.shape, s.dtype) for s in land_shapes], _SDS((8, _LANES), F32)),
        out_specs=(_SEM, _SEM, *[_HBM] * nl, pl.BlockSpec(memory_space=pltpu.VMEM)),
        input_output_aliases={ns + i: 2 + i for i in range(nl)},
        compiler_params=pltpu.CompilerParams(has_side_effects=_EFFECT))(
            *[_hbm(s) for s in srcs], *zones, after)
    return dict(sems=res[:2], srcs=srcs, lands=res[2:2 + nl], plan=plan), res[-1]


def _send_wait(name, h, after):
    ns, nl = len(h['srcs']), len(h['lands'])

    def body(*refs):
        ins, lands, sems = refs[:ns], refs[ns:ns + nl], refs[ns + nl:ns + nl + 2]
        x, y, c, chips = _place()
        sends, landings = h['plan'](x, y, c, chips, ins, lands)
        for i, (s, d, dev) in enumerate(sends):
            _remote(s, d, sems, i, dev).wait_send()
        for i, d in enumerate(landings):
            _remote(d, d, sems, i, sends[i][2]).wait_recv()

    return pl.pallas_call(
        body, name=name, in_specs=[_HBM] * (ns + nl) + [_SEM, _SEM, _ANY],
        out_shape=tuple(pltpu.HBM(a.shape, a.dtype) for a in h['lands']), out_specs=tuple([_HBM] * nl),
        input_output_aliases={ns + i: i for i in range(nl)},
        compiler_params=pltpu.CompilerParams(has_side_effects=_EFFECT))(
            *[_hbm(s) for s in h['srcs']], *h['lands'], *h['sems'], after)


def _gather_plan(items):
    def plan(x, y, c, chips, ins, lands):
        k = 2 * x + y
        to = [(px, py, c) for px, py in chips] + [(x, y, 1 - c)]
        sends = [(ins[si].at[l], lands[t].at[k], dev) for t, (si, l) in enumerate(items) for dev in to]
        return sends, [lands[t].at[2 * px + py] for t in range(len(items)) for px, py in chips + [(x, y)]]
    return plan


_FLIPS = [((d >> 2) & 1, (d >> 1) & 1, d & 1) for d in range(1, 8)]


def _reduce_plan(halves):
    def plan(x, y, c, chips, ins, lands):
        sends, landings = [], []
        for t, hf in enumerate(halves):
            for i, (fx, fy, fc) in enumerate(_FLIPS):
                px, py, pc = x ^ fx, y ^ fy, c ^ fc
                sends.append((ins[t].at[2 * px + py, pl.ds(pc * hf, hf)], lands[t].at[i], (px, py, pc)))
                landings.append(lands[t].at[i])
        return sends, landings
    return plan


def _swap(name, srcs, out_shapes, plan, n_sends):
    n = len(srcs)

    def body(*refs):
        ins, outs, sems = refs[:n], refs[n:n + len(out_shapes)], refs[-2:]
        x, y, c, chips = _place()
        sends, landings = plan(x, y, c, chips, ins, outs)
        out = [_remote(s, d, sems, i, dev) for i, (s, d, dev) in enumerate(sends)]
        for cp in out:
            cp.start()
        for i, d in enumerate(landings):
            _remote(d, d, sems, i, sends[i][2]).wait_recv()
        for cp in out:
            cp.wait_send()

    return pl.pallas_call(
        body, in_specs=[_HBM] * n, out_specs=[_HBM] * len(out_shapes), out_shape=out_shapes, name=name,
        scratch_shapes=[pltpu.SemaphoreType.DMA((n_sends,)), pltpu.SemaphoreType.DMA((n_sends,))])(*srcs)


def _sum_owned(grads, landed, c, k, names):
    def sum8(*parts):
        acc = parts[0].astype(F32)
        for p in parts[1:]:
            acc = acc + p.astype(F32)
        return acc
    outs = []
    for g, got, name in zip(grads, landed, names):
        hf, b = got.shape[1:]
        own = lax.dynamic_slice_in_dim(lax.dynamic_index_in_dim(g, k, axis=0, keepdims=False), c * hf, hf, axis=0)
        outs.append(_rows(sum8, [own] + [("slot", got, i) for i in range(len(_FLIPS))], [], [(b, F32)],
                          tile=_pick(hf, (_tile_for(b), 32)), name=f"grad_sum_{name}")[0])
    return outs


def _share_halves(mine, *, name):
    n = len(mine)

    def plan(x, y, c_, chips, ins, outs):
        return [(ins[t], outs[t], (x, y, 1 - c_)) for t in range(n)], [outs[t] for t in range(n)]
    return _swap(name, mine, [_SDS(h.shape, F32) for h in mine], plan, n)


def _adamw_owned(w, mine, theirs, m, v, c, *, name):
    depth, a, b = w.shape
    half = a // 2
    tile = _pick(half, (_tile_for(b), 32, 8))
    nh = half // tile

    def blocks_of(l):
        return lambda i: (jnp.clip(i - 2 * nh * l, 0, 2 * nh - 1) % nh, 0)

    def fn(w, m, v, *rest):
        halves, cflag = rest[:-1], rest[-1]
        step = pl.program_id(0)
        is_mine = cflag[0:1, 0:1] == ((step // nh) % 2).astype(F32)
        g = jnp.where(is_mine, halves[0], halves[1])
        for l in range(1, depth):
            g = jnp.where(step >= 2 * nh * l, jnp.where(is_mine, halves[2 * l], halves[2 * l + 1]), g)
        return (g,) + _adamw_math(w, g, m, v)
    ins = [a_.reshape(depth * a, b) for a_ in (w, m, v)]
    ins += [(h, b, blocks_of(l)) for l in range(depth) for h in (mine[l], theirs[l])]
    res = _rows(fn, ins, [jnp.full((1, _LANES), c, F32)], [(b, F32)] * 4, tile=tile, name=name)
    return [r.reshape(w.shape) for r in res]


_BIG = ("w_in", "w_out", "w_mlp_in", "w_mlp_out")
_SMALL = ("ln1_g", "conv_b", "dt_bias", "a_log", "d_skip", "attn_norm_g", "ssd_norm_g", "ln2_g", "final_norm_g")
_ORDER = ("ln1_g", "w_in", "conv_w", "conv_b", "dt_bias", "a_log", "d_skip", "attn_norm_g", "ssd_norm_g", "w_out", "ln2_g",
          "w_mlp_in", "w_mlp_out", "final_norm_g")


def _pack(parts, rows):
    flat = jnp.concatenate([p.reshape(-1) for p in parts])
    return jnp.pad(flat, (0, rows * _LANES - flat.shape[0])).reshape(rows, _LANES)


def _unpack(buf, like):
    flat, out, o = buf.reshape(-1), [], 0
    for p in like:
        out.append(flat[o:o + p.size].reshape(p.shape))
        o += p.size
    return out


def kernel(x, ln1_g, w_in, conv_w, conv_b, dt_bias, a_log, d_skip, attn_norm_g, ssd_norm_g, w_out, ln2_g, w_mlp_in, w_mlp_out, final_norm_g, loss_target, m_ln1_g, m_w_in, m_conv_w, m_conv_b, m_dt_bias, m_a_log, m_d_skip, m_attn_norm_g, m_ssd_norm_g, m_w_out, m_ln2_g, m_w_mlp_in, m_w_mlp_out, m_final_norm_g, v_ln1_g, v_w_in, v_conv_w, v_conv_b, v_dt_bias, v_a_log, v_d_skip, v_attn_norm_g, v_ssd_norm_g, v_w_out, v_ln2_g, v_w_mlp_in, v_w_mlp_out, v_final_norm_g):
    w = dict(ln1_g=ln1_g, w_in=w_in, conv_w=conv_w, conv_b=conv_b, dt_bias=dt_bias, a_log=a_log, d_skip=d_skip,
             attn_norm_g=attn_norm_g, ssd_norm_g=ssd_norm_g, w_out=w_out, ln2_g=ln2_g, w_mlp_in=w_mlp_in, w_mlp_out=w_mlp_out,
             final_norm_g=final_norm_g)
    m = dict(ln1_g=m_ln1_g, w_in=m_w_in, conv_w=m_conv_w, conv_b=m_conv_b, dt_bias=m_dt_bias, a_log=m_a_log, d_skip=m_d_skip,
             attn_norm_g=m_attn_norm_g, ssd_norm_g=m_ssd_norm_g, w_out=m_w_out, ln2_g=m_ln2_g, w_mlp_in=m_w_mlp_in,
             w_mlp_out=m_w_mlp_out, final_norm_g=m_final_norm_g)
    v = dict(ln1_g=v_ln1_g, w_in=v_w_in, conv_w=v_conv_w, conv_b=v_conv_b, dt_bias=v_dt_bias, a_log=v_a_log, d_skip=v_d_skip,
             attn_norm_g=v_attn_norm_g, ssd_norm_g=v_ssd_norm_g, w_out=v_w_out, ln2_g=v_ln2_g, w_mlp_in=v_w_mlp_in,
             w_mlp_out=v_w_mlp_out, final_norm_g=v_final_norm_g)
    depth, d_model = ln1_g.shape
    n_chips = _CHIPS
    c = lax.axis_index("c")
    chip = 2 * lax.axis_index("x") + lax.axis_index("y")
    cch = conv_w.shape[2] * n_chips

    cw = _exchange8(conv_w.reshape(depth * _CONV_K, -1), reduce=False, name="gather_conv_w")[0::2]
    conv_full = cw.reshape(n_chips, depth, _CONV_K, -1).transpose(1, 2, 0, 3).reshape(depth, _CONV_K, cch)
    own = [w[n].astype(_BF) for n in _BIG]

    def start_gather(tag, items, after):
        lands = [_SDS((n_chips, *own[i].shape[1:]), _BF) for i, _ in items]
        return _send_start(f"gather_start_{tag}", own, lands, _gather_plan(items), n_chips * len(items), after)

    def finish_gather(tag, handle, items, after):
        landed = _send_wait(f"gather_wait_{tag}", handle, after)
        return {_BIG[i]: g for (i, _), g in zip(items, landed)}

    def layer_weights(l, blocks):
        p = {}
        if 'w_in' in blocks:
            p['w_in'] = _w_in_from_chips(blocks['w_in'], cch, name=f"w_in_regroup_{l}")
        if 'w_out' in blocks:
            p['w_out'] = blocks['w_out'].reshape(-1, d_model)
            p['w_mlp_in'] = blocks['w_mlp_in']
            p['w_mlp_out'] = blocks['w_mlp_out'].reshape(-1, d_model)
        return p

    groups = dict(a=[(0, 0)], b=[(1, 0), (2, 0), (3, 0)], c=[(0, 1)], d=[(1, 1), (2, 1), (3, 1)])
    handles, token = {}, conv_full

    half_in = own[0].shape[1] // 2

    def rows_of(ref, who):
        return ref.at[pl.ds(who * half_in, half_in)]

    def plan_a(x_, y_, c_, chips, ins, lands):
        k = 2 * x_ + y_
        sends = [(rows_of(ins[0].at[0], c_), rows_of(lands[0].at[k], c_), (px, py, c_)) for px, py in chips]
        sends.append((ins[0].at[0], lands[0].at[k], (x_, y_, 1 - c_)))
        return sends, [rows_of(lands[0].at[2 * px + py], c_) for px, py in chips] + [lands[0].at[k]]

    def plan_pass(x_, y_, c_, chips, ins, lands):
        sends = [(rows_of(lands[0].at[2 * px + py], c_),) * 2 + ((x_, y_, 1 - c_),) for px, py in chips]
        return sends, [rows_of(lands[0].at[2 * px + py], 1 - c_) for px, py in chips]
    handles["a"], token = _send_start("gather_start_a", own[:1], [_SDS((n_chips, *own[0].shape[1:]), _BF)], plan_a, n_chips, token)
    for tag, items in list(groups.items())[1:]:
        handles[tag], token = start_gather(tag, items, token)
    landed = _send_wait("gather_land_a", handles["a"], token)
    handles["a"], token = _send_start("gather_pass_a", [], landed, plan_pass, 3, landed[0])
    layers = [{n: w[n][l] for n in _SMALL[:-1]} for l in range(depth)]
    for l in range(depth):
        layers[l]['conv_w'] = conv_full[l]

    layers[0].update(layer_weights(0, finish_gather("a", handles["a"], groups["a"], token)))
    mix, sv0 = _layer_fwd(x[0], layers[0], 0)
    layers[0].update(layer_weights(0, finish_gather("b", handles["b"], groups["b"], mix)))
    h = _layer_fwd_mlp(layers[0], sv0, 0)
    layers[1].update(layer_weights(1, finish_gather("c", handles["c"], groups["c"], h)))
    mix, sv1 = _layer_fwd(h, layers[1], 1)
    layers[1].update(layer_weights(1, finish_gather("d", handles["d"], groups["d"], mix)))
    h = _layer_fwd_mlp(layers[1], sv1, 1)
    saved = [sv0, sv1]

    def by_chip(g, name):
        if name in ("w_mlp_in", "w_in"):
            return g
        return g.reshape(n_chips, -1, d_model)

    pending = []

    def sender(l):
        def send(names, g):
            srcs = [by_chip(g[n], n) for n in names]
            halves = [s.shape[1] // 2 for s in srcs]
            lands = [_SDS((len(_FLIPS), hf, s.shape[2]), _BF) for s, hf in zip(srcs, halves)]
            handle, tok = _send_start(f"grad_start_{names[-1]}_{l}", srcs, lands, _reduce_plan(halves), len(_FLIPS) * len(srcs), srcs[0])
            pending.append((l, names, srcs, handle))
            return tok
        return send

    dx, dxb, g_final, loss_part = _loss_bwd(h, final_norm_g, loss_target[0])
    grads, after = [None] * depth, None
    for l in reversed(range(depth)):
        dx, dxb, grads[l] = _layer_bwd(dx, dxb, layers[l], saved[l], l, sender(l), after)
        after = dx
    landed_of = {}
    sent_in = {(n, l): (gi, j) for gi, (l, names, _, _) in enumerate(pending) for j, n in enumerate(names)}

    def landed_for(gi, after):
        if gi not in landed_of:
            l, names, _, handle = pending[gi]
            landed_of[gi] = _send_wait(f"grad_wait_{names[-1]}_{l}", handle, after)
        return landed_of[gi]

    red, delta, new_m, new_v = {}, {}, {}, {}
    after = dx
    for n in ("w_mlp_out", "w_mlp_in", "w_out", "w_in"):
        mine = []
        for l in range(depth):
            gi, j = sent_in[(n, l)]
            got = landed_for(gi, after)[j]
            mine.append(_sum_owned([pending[gi][2][j]], [got], c, chip, [f"{n}_{l}"])[0])
        theirs = _share_halves(mine, name=f"grad_share_{n}")
        red[n], delta[n], new_m[n], new_v[n] = _adamw_owned(w[n], mine, theirs, m[n], v[n], c, name=f"adamw_{n}")
        after = delta[n]

    small = {n: jnp.stack([grads[l][n] for l in range(depth)]) for n in _SMALL[:-1] + ("conv_w",)}
    small["final_norm_g"] = g_final
    parts = [loss_part.reshape(1)] + [small[n] for n in _SMALL + ("conv_w",)]
    rows = -(-sum(p.size for p in parts) // 1024) * 8
    tot = _unpack(_exchange8(_pack(parts, rows), reduce=True, after=red[_BIG[0]], name="allreduce_small"), parts)
    loss = tot[0][0]
    red.update(zip(_SMALL + ("conv_w",), tot[1:]))
    red["conv_w"] = lax.dynamic_index_in_dim(red["conv_w"].reshape(depth, _CONV_K, n_chips, -1), chip, axis=2, keepdims=False)

    names = _SMALL + ("conv_w",)
    like = [w[n] for n in names]
    srows = -(-sum(p.size for p in like) // 1024) * 8
    res = _adamw(*[_pack([d[n] for n in names], srows) for d in (w, red, m, v)], name="adamw_small")
    for dst, buf in zip((delta, new_m, new_v), res):
        dst.update(zip(names, _unpack(buf, like)))
    return (loss, dx[None], *[red[n] for n in _ORDER], *[delta[n] for n in _ORDER], *[new_m[n] for n in _ORDER],
            *[new_v[n] for n in _ORDER])
```

```python
import numpy as np
import jax
import jax.numpy as jnp
from jax import lax
from jax.experimental import pallas as pl
from jax.experimental.pallas import tpu as pltpu

F32 = jnp.float32
_BF = jnp.bfloat16
_NEG = -1e30
_EPS = 1e-5
_HEADS = 16
_HDIM = 64
_AW = _HEADS * _HDIM
_ABLK = 128
_DILATIONS = (1, 4, 16)
_CHUNK = 128
_NSTATE = 128
_GROUPS = 2
_HPG = _HEADS // _GROUPS
_CONV_K = 4
_LANES = 128
_CHIPS = 4
_LR, _B1, _B2, _AEPS, _WD, _STEP = 0.001, 0.9, 0.999, 1e-08, 0.01, 10
_VMEM_CAP = 56 * 1024 * 1024
_MESH = pl.DeviceIdType.MESH
_SDS = jax.ShapeDtypeStruct
_NT = (((1,), (1,)), ((), ()))
_TN = (((0,), (0,)), ((), ()))


def _params(sem, est_bytes):
    lim = int(min(max(2 * est_bytes + (4 << 20), 16 << 20), _VMEM_CAP))
    return pltpu.CompilerParams(dimension_semantics=sem, vmem_limit_bytes=lim)


def _nbytes(shape, dtype):
    return int(np.prod(shape)) * jnp.dtype(dtype).itemsize


def _hbm(a):
    return pltpu.with_memory_space_constraint(a, pltpu.HBM)


def _dot(a, b, dims=(((1,), (0,)), ((), ()))):
    return lax.dot_general(a.astype(_BF), b.astype(_BF), dims, preferred_element_type=F32)


_HALO = 8


def _rows(fn, ins, consts, outs, sums=(), *, halos=(), into=None, after=None, tile, name):
    first = ins[0]
    if isinstance(first, tuple) and isinstance(first[0], str) and first[0] == "layers":
        rows = first[1].shape[0] * first[1].shape[1]
    else:
        rows = (first[0] if isinstance(first, tuple) else first).shape[0]
    n_steps = rows // tile

    def norm_in(a):
        if not isinstance(a, tuple):
            return a, tile, a.shape[1], lambda i: (i, 0)
        if isinstance(a[0], str) and a[0] == "layers":
            return a[1], (None, tile, a[1].shape[2]), a[1].shape[2], lambda i, n=a[1].shape[1] // tile: (i // n, i % n, 0)
        if isinstance(a[0], str) and a[0] == "slot":
            return a[1], (None, tile, a[1].shape[2]), a[1].shape[2], lambda i, s=a[2]: (s, i, 0)
        if isinstance(a[0], str):
            return a[1], tile // a[2], a[1].shape[1], lambda i: (i, 0)
        return a[0], tile, a[1], a[2] if callable(a[2]) else (lambda i, j=a[2]: (i, j))
    ins = [norm_in(a) for a in ins]
    outs = [(w, dt, d[0] if d else 1) for w, dt, *d in outs]
    n_in, n_h, n_c, n_o, n_s = len(ins), len(halos), len(consts), len(outs), len(sums)
    n_x = int(into is not None and into[0] is not None) + int(after is not None)

    def body(*refs):
        step = pl.program_id(0)
        vals = [r[...] for r in refs[:n_in]]
        for r, (_, side) in zip(refs[n_in:n_in + n_h], halos):
            vals.append(jnp.where(step == (0 if side < 0 else n_steps - 1), 0.0, r[...]))
        vals += [r[...] for r in refs[n_in + n_h:n_in + n_h + n_c]]
        refs = refs[:n_in] + refs[n_in + n_h:]
        res = fn(*vals)
        res = res if isinstance(res, tuple) else (res,)
        orefs = refs[n_in + n_c + n_x:n_in + n_c + n_x + n_o]
        srefs = refs[n_in + n_c + n_x + n_o:]
        for r, v in zip(orefs, res[:n_o]):
            r[...] = v.astype(r.dtype)
        if n_s:
            @pl.when(pl.program_id(0) == 0)
            def _():
                for r in srefs:
                    r[...] = jnp.zeros_like(r)
            for r, v in zip(srefs, res[n_o:]):
                r[...] += v.reshape(tile // 8, 8, v.shape[-1]).sum(axis=0)

    per = tile // _HALO
    in_specs = [pl.BlockSpec(r if isinstance(r, tuple) else (r, w), idx) for _, r, w, idx in ins]
    in_specs += [pl.BlockSpec((_HALO, a.shape[1]), (lambda i: (jnp.maximum(i * per - 1, 0), 0)) if side < 0
                              else (lambda i: (jnp.minimum((i + 1) * per, rows // _HALO - 1), 0))) for a, side in halos]
    in_specs += [pl.BlockSpec(c.shape, lambda i, nd=c.ndim: (0,) * nd) for c in consts]
    out_shape = [_SDS((rows // d, d * w), dt) for w, dt, d in outs] + [_SDS((8, w), F32) for w in sums]
    out_specs = [pl.BlockSpec((tile // d, d * w), lambda i: (i, 0)) for w, _, d in outs]
    out_specs += [pl.BlockSpec((8, w), lambda i: (0, 0)) for w in sums]
    est = (sum(_nbytes((tile if isinstance(r, tuple) else r, w), a.dtype) for a, r, w, _ in ins)
           + sum(_nbytes((tile, w), dt) for w, dt, _ in outs))
    shared, aliases = [], {}
    if into is not None:
        buf, total, j = into
        out_shape[0] = _SDS((rows, total), outs[0][1])
        out_specs[0] = pl.BlockSpec((tile, outs[0][0]), lambda i: (i, j))
        if buf is not None:
            shared, aliases = [buf], {n_in + n_h + n_c: 0}
    if after is not None:
        shared.append(after)
    in_specs += [pl.BlockSpec(memory_space=pl.ANY)] * len(shared)
    return pl.pallas_call(body, grid=(n_steps,), in_specs=in_specs, out_specs=out_specs, out_shape=out_shape, name=name,
                          input_output_aliases=aliases, compiler_params=_params(("arbitrary",), 3 * est))(
                              *[_hbm(a[0]) for a in ins], *[_hbm(a) for a, _ in halos], *consts, *shared)


def _perm(d, tile):
    p = np.zeros((tile, tile), np.float32)
    t = np.arange(tile)
    p[t, (t % d) * (tile // d) + t // d] = 1.0
    return jnp.asarray(p, _BF)


def _unstride(s, p):
    d = p.shape[0] // s.shape[0]
    w = s.shape[1] // d
    return _dot(p, jnp.concatenate([s[:, r * w:(r + 1) * w] for r in range(d)], axis=0))


def _stride(x, p, d):
    z = _dot(p, x, _TN)
    n = x.shape[0] // d
    return jnp.concatenate([z[r * n:(r + 1) * n] for r in range(d)], axis=1)


def _shifted(u, halo, back):
    n = u.shape[0] + _HALO
    if back:
        ext = jnp.concatenate([halo, u], axis=0)
        return [pltpu.roll(ext, j, 0)[_HALO:] for j in (1, 2, 3)]
    ext = jnp.concatenate([u, halo], axis=0)
    return [pltpu.roll(ext, n - j, 0)[:u.shape[0]] for j in (1, 2, 3)]


def _tile_for(width):
    return max(c for c in (256, 128, 64, 32) if c * width <= (1 << 18) or c == 32)


_NORM_TILE = 256


def _rstd(x):
    return lax.rsqrt(jnp.mean(x * x, axis=-1, keepdims=True) + _EPS)


def _split(x, groups):
    w = x.shape[-1] // groups
    return [x[:, g * w:(g + 1) * w] for g in range(groups)]


def _cat(parts):
    return parts[0] if len(parts) == 1 else jnp.concatenate(parts, axis=-1)


def _rms_bwd_tile(x, dy, g, groups):
    dxs, dgs = [], []
    for xs, ds, gs in zip(_split(x, groups), _split(dy.astype(F32), groups), _split(g, groups)):
        r = _rstd(xs)
        xh = xs * r
        gd = ds * gs
        dxs.append(r * (gd - xh * jnp.mean(gd * xh, axis=-1, keepdims=True)))
        dgs.append(ds * xh)
    return _cat(dxs), _cat(dgs)


def _rms_fwd(x, g, *, groups=1, into=None, name):
    def fn(x, g):
        return _cat([xs * _rstd(xs) * gs for xs, gs in zip(_split(x, groups), _split(g, groups))])
    w = x.shape[1]
    return _rows(fn, [x], [g.reshape(1, w)], [(w, _BF)], into=into, tile=_NORM_TILE, name=name)[0]


def _rms_bwd(x, dy, g, res, *, name):
    def fn(x, dy, res, g):
        dx, dg = _rms_bwd_tile(x, dy, g, 1)
        return dx + res, dx + res, dg
    w = x.shape[1]
    dx, dxb, dg = _rows(fn, [x, dy, res], [g.reshape(1, w)], [(w, F32), (w, _BF)], [w], tile=_NORM_TILE, name=name)
    return dx, dxb, dg.sum(axis=0)


def _pick(n, cands):
    for c in cands:
        if n % c == 0:
            return c
    raise ValueError(f"no block size for {n}")


_MM_BLOCKS = (1024, 1152, 512, 384)


def _mm(a, b, *, ta=False, tb=False, extra=(), epi=None, outs=(F32,), after=None, b_chips=0, out_chips=0, b_cols=None, name):
    m, k = (a.shape[1], a.shape[0]) if ta else a.shape
    b_shape = (b.shape[1], b.shape[2] * b_chips) if b_chips else b.shape
    if b_cols is not None:
        b_shape = (b.shape[0], b_cols[1])
    n = b_shape[0] if tb else b_shape[1]
    assert k == (b_shape[1] if tb else b_shape[0])
    n_cap = n // max(out_chips, 1 if tb else b_chips, 1)
    k_cap = k // (b_chips if (b_chips and tb) else 1)
    bm, bn = _pick(m, _MM_BLOCKS), _pick(n_cap, _MM_BLOCKS)
    bk = _pick(k_cap, (2048, 1920) + _MM_BLOCKS)
    nk = k // bk
    n_e, n_o = len(extra), len(outs)
    behind = [] if after is None else [after]
    dims = (((0 if ta else 1,), (1 if tb else 0,)), ((), ()))

    def body(a_ref, b_ref, *rest):
        ex, orefs, acc = rest[:n_e], rest[n_e + len(behind):n_e + len(behind) + n_o], rest[-1]
        kk = pl.program_id(2)

        @pl.when(kk == 0)
        def _():
            acc[...] = jnp.zeros_like(acc)

        acc[...] += _dot(a_ref[...], b_ref[...], dims)

        @pl.when(kk == nk - 1)
        def _():
            r = acc[...]
            res = epi(r, *[e[...] for e in ex]) if epi is not None else (r,)
            for o, v in zip(orefs, res):
                o[...] = v.astype(o.dtype)

    a_spec = pl.BlockSpec((bk, bm), lambda i, j, kk: (kk, i)) if ta else pl.BlockSpec((bm, bk), lambda i, j, kk: (i, kk))
    if b_chips and tb:
        per = k_cap // bk
        b_spec = pl.BlockSpec((None, bn, bk), lambda i, j, kk: (kk // per, j, kk % per))
    elif b_chips:
        per = n_cap // bn
        b_spec = pl.BlockSpec((None, bk, bn), lambda i, j, kk: (j // per, kk, j % per))
    else:
        first = 0 if b_cols is None else b_cols[0] // bn
        assert b_cols is None or (not tb and b_cols[0] % bn == 0)
        b_spec = pl.BlockSpec((bn, bk), lambda i, j, kk: (j, kk)) if tb else pl.BlockSpec((bk, bn), lambda i, j, kk: (kk, first + j))
    t_spec = pl.BlockSpec((bm, bn), lambda i, j, kk: (i, j))
    o_spec, o_shape = t_spec, (m, n)
    if out_chips:
        per_o = n_cap // bn
        o_spec, o_shape = pl.BlockSpec((None, bm, bn), lambda i, j, kk: (j // per_o, i, j % per_o)), (out_chips, m, n_cap)
    est = (_nbytes((bm, bk), a.dtype) + _nbytes((bk, bn), b.dtype) + sum(_nbytes((bm, bn), e.dtype) for e in extra)
           + sum(_nbytes((bm, bn), o) for o in outs)) * 2 + 2 * _nbytes((bm, bn), F32)
    res = pl.pallas_call(
        body, grid=(m // bm, n // bn, nk), in_specs=[a_spec, b_spec] + [t_spec] * n_e + [pl.BlockSpec(memory_space=pl.ANY)] * len(behind),
        out_specs=[o_spec] * n_o, out_shape=[_SDS(o_shape, o) for o in outs], scratch_shapes=[pltpu.VMEM((bm, bn), F32)], name=name,
        compiler_params=_params(("parallel", "parallel", "arbitrary"), est))(_hbm(a), _hbm(b), *[_hbm(e) for e in extra], *behind)
    return res[0] if n_o == 1 else res


def _w_in_groups(cch):
    return (0, 3 * _AW), (3 * _AW, _AW), (4 * _AW, cch), (4 * _AW + cch, _HEADS)


def _w_in_from_chips(blocks, cch, *, name):
    chips, d, _ = blocks.shape
    qkv, z, xbc, dt = _w_in_groups(cch)
    order = (qkv, xbc, z, dt)
    total = 3 * _AW + cch + _AW + _LANES
    tile = 2 * _ABLK

    def body(b_ref, o_ref):
        full = jnp.concatenate([b_ref[k] for k in range(chips)], axis=1)
        parts = [full[:, s:s + w] for s, w in order]
        o_ref[...] = jnp.concatenate(parts + [jnp.zeros((tile, total - sum(w for _, w in order)), full.dtype)], axis=1)

    return pl.pallas_call(
        body, grid=(d // tile,), in_specs=[pl.BlockSpec((chips, tile, blocks.shape[2]), lambda i: (0, i, 0))],
        out_specs=pl.BlockSpec((tile, total), lambda i: (i, 0)), out_shape=_SDS((d, total), blocks.dtype), name=name,
        compiler_params=_params(("arbitrary",), 16 << 20))(_hbm(blocks))


def _w_in_to_chips(g_all, cch, chips, *, name):
    d = g_all.shape[0]
    qkv, z, xbc, dt = _w_in_groups(cch)
    n = dt[0] + dt[1]
    z0 = 3 * _AW + cch
    tile = 2 * _ABLK

    def body(g_ref, o_ref):
        v = g_ref[...]
        full = jnp.concatenate([v[:, :3 * _AW], v[:, z0:z0 + _AW], v[:, 3 * _AW:z0], v[:, z0 + _AW:z0 + _AW + _HEADS]], axis=1)
        for k in range(chips):
            o_ref[k] = full[:, k * (n // chips):(k + 1) * (n // chips)]

    return pl.pallas_call(
        body, grid=(d // tile,), in_specs=[pl.BlockSpec((tile, g_all.shape[1]), lambda i: (i, 0))],
        out_specs=pl.BlockSpec((chips, tile, n // chips), lambda i: (0, i, 0)), out_shape=_SDS((chips, d, n // chips), g_all.dtype),
        name=name, compiler_params=_params(("arbitrary",), 16 << 20))(_hbm(g_all))


def _add_to(acc, r):
    return (acc + r,)


def _alibi_bias(dilation):
    slopes = 2.0 ** (-8.0 * (np.arange(_HEADS) + 1) / _HEADS)
    i = np.arange(_ABLK)[:, None]
    j = np.arange(_ABLK)[None, :]
    cur = np.where(i - j >= 0, -slopes[:, None, None] * ((i - j) * dilation), _NEG)
    prev = np.where(j >= i, -slopes[:, None, None] * ((i - j + _ABLK) * dilation), _NEG)
    both = np.stack([np.concatenate([np.full_like(prev, _NEG), cur], axis=2), np.concatenate([prev, cur], axis=2)])
    return jnp.asarray(both.reshape(2, _HEADS // 2, 2 * _ABLK, 2 * _ABLK), F32)


def _bias_spec():
    return pl.BlockSpec((None, _HEADS // 2, 2 * _ABLK, 2 * _ABLK), lambda r, j: (jnp.minimum(j, 1), 0, 0, 0))


def _strided(a, d):
    return a.reshape(a.shape[0] // d, d * a.shape[1])


def _pair(pr):
    return slice(pr * _LANES, (pr + 1) * _LANES)


def _low_lanes(shape):
    return lax.broadcasted_iota(jnp.int32, shape, 1) < _HDIM


def _halves(v, low):
    z = jnp.zeros_like(v)
    return jnp.where(low, v, z), jnp.where(low, z, v)


def _lane_spec(nb):
    return pl.BlockSpec((_ABLK, _LANES), lambda r, j: (jnp.minimum(j, nb - 1), r))


def _expand_heads(v):
    low = _low_lanes((v.shape[0], _LANES))
    return jnp.concatenate([jnp.where(low, v[:, 2 * pr:2 * pr + 1], v[:, 2 * pr + 1:2 * pr + 2]) for pr in range(_HEADS // 2)], axis=1)


def _attn_specs(nb, n_parts):
    def cur(p):
        return pl.BlockSpec((_ABLK, _AW), lambda r, j: (jnp.minimum(j, nb - 1), r * n_parts + p))

    def prev(p):
        return pl.BlockSpec((_ABLK, _AW), lambda r, j: (jnp.clip(j - 1, 0, nb - 1), r * n_parts + p))
    return cur, prev


def _attn_fwd(qkv, dilation, *, name):
    t = qkv.shape[0] * dilation
    nb = t // dilation // _ABLK
    bias = _alibi_bias(dilation)
    scale = _HDIM ** -0.5

    def body(q_ref, kc_ref, kp_ref, vc_ref, vp_ref, b_ref, o_ref, l_ref):
        low = _low_lanes((_ABLK, _LANES))
        l_ref[...] = jnp.zeros_like(l_ref)
        for pr in range(_HEADS // 2):
            sl = _pair(pr)
            k2 = jnp.concatenate([kp_ref[:, sl], kc_ref[:, sl]], axis=0)
            v2 = jnp.concatenate([vp_ref[:, sl], vc_ref[:, sl]], axis=0)
            q2 = jnp.concatenate(_halves(q_ref[:, sl] * scale, low), axis=0)
            s = _dot(q2, k2, _NT) + b_ref[pr]
            m = jnp.max(s, axis=-1, keepdims=True)
            p = jnp.exp(s - m)
            den = jnp.sum(p, axis=-1, keepdims=True)
            o = _dot(p, v2) / den
            lse = m + jnp.log(den)
            l_ref[:, 2 * pr:2 * pr + 1] = lse[:_ABLK]
            l_ref[:, 2 * pr + 1:2 * pr + 2] = lse[_ABLK:]
            o_ref[:, sl] = jnp.where(low, o[:_ABLK], o[_ABLK:]).astype(o_ref.dtype)

    cur, prev = _attn_specs(nb, 3)
    cur1, _ = _attn_specs(nb, 1)
    bspec = _bias_spec()
    sv = _hbm(qkv)
    o, l = pl.pallas_call(
        body, grid=(dilation, nb), in_specs=[cur(0), cur(1), prev(1), cur(2), prev(2), bspec],
        out_specs=[cur1(0), _lane_spec(nb)],
        out_shape=[_SDS((t // dilation, dilation * _AW), _BF), _SDS((t // dilation, dilation * _LANES), F32)], name=name,
        compiler_params=_params(("parallel", "arbitrary"), 16 << 20))(sv, sv, sv, sv, sv, bias)
    return o, l.reshape(t, _LANES)


def _attn_bwd(qkv, do, ld, dilation, *, name):
    t = qkv.shape[0] * dilation
    nb = t // dilation // _ABLK
    bias = _alibi_bias(dilation)
    scale = _HDIM ** -0.5

    def body(q_ref, kc_ref, kp_ref, vc_ref, vp_ref, do_ref, ld_ref, b_ref, dq_ref, dk_ref, dv_ref, ck, cv):
        n = pl.program_id(1)

        @pl.when(n == 0)
        def _():
            ck[...] = jnp.zeros_like(ck)
            cv[...] = jnp.zeros_like(cv)

        @pl.when(n < nb)
        def _():
            low = _low_lanes((_ABLK, _LANES))
            for pr in range(_HEADS // 2):
                sl = _pair(pr)
                k2 = jnp.concatenate([kp_ref[:, sl], kc_ref[:, sl]], axis=0)
                v2 = jnp.concatenate([vp_ref[:, sl], vc_ref[:, sl]], axis=0)
                q2 = jnp.concatenate(_halves(q_ref[:, sl] * scale, low), axis=0)
                do2 = jnp.concatenate(_halves(do_ref[:, sl], low), axis=0)
                lrow = jnp.concatenate([ld_ref[:, 2 * pr:2 * pr + 1], ld_ref[:, 2 * pr + 1:2 * pr + 2]], axis=0)
                dsum = jnp.concatenate([ld_ref[:, _HEADS + 2 * pr:_HEADS + 2 * pr + 1],
                                        ld_ref[:, _HEADS + 2 * pr + 1:_HEADS + 2 * pr + 2]], axis=0)
                p = jnp.exp(_dot(q2, k2, _NT) + b_ref[pr] - lrow)
                ds = (p * (_dot(do2, v2, _NT) - dsum)).astype(_BF)
                dq = _dot(ds, k2)
                dk2, dv2 = _dot(ds, q2, _TN), _dot(p, do2, _TN)
                dq_ref[:, sl] = (jnp.where(low, dq[:_ABLK], dq[_ABLK:]) * scale).astype(dq_ref.dtype)
                dk_ref[:, sl] = (ck[:, sl] + dk2[:_ABLK]).astype(dk_ref.dtype)
                dv_ref[:, sl] = (cv[:, sl] + dv2[:_ABLK]).astype(dv_ref.dtype)
                ck[:, sl] = dk2[_ABLK:]
                cv[:, sl] = dv2[_ABLK:]

        @pl.when(n == nb)
        def _():
            dk_ref[...] = ck[...].astype(dk_ref.dtype)
            dv_ref[...] = cv[...].astype(dv_ref.dtype)

    cur, prev = _attn_specs(nb, 3)
    cur1, prev1 = _attn_specs(nb, 1)
    bspec = _bias_spec()
    sv, dov, ldv = _hbm(qkv), _hbm(do), _hbm(_strided(ld, dilation))
    dqkv = pl.pallas_call(
        body, grid=(dilation, nb + 1),
        in_specs=[cur(0), cur(1), prev(1), cur(2), prev(2), cur1(0), _lane_spec(nb), bspec],
        out_specs=[cur1(0), prev1(0), prev1(0)], out_shape=[_SDS(dov.shape, _BF)] * 3, name=name,
        scratch_shapes=[pltpu.VMEM((_ABLK, _AW), F32)] * 2,
        compiler_params=_params(("parallel", "arbitrary"), 16 << 20))(sv, sv, sv, sv, sv, dov, ldv, bias)
    return dqkv


def _ssd_in_specs(ch):
    return dict(
        xs=pl.BlockSpec((_CHUNK, _AW), lambda c: (ch(c), 0)),
        bc=pl.BlockSpec((_CHUNK, 2 * _GROUPS * _NSTATE), lambda c: (ch(c), _AW // (2 * _GROUPS * _NSTATE))),
        lane=pl.BlockSpec((_CHUNK, _LANES), lambda c: (ch(c), 0)),
        arow=pl.BlockSpec((_HEADS, 1, _CHUNK), lambda c: (0, 0, ch(c))),
        st=pl.BlockSpec((1, _HEADS // 2, _NSTATE, _LANES), lambda c: (ch(c), 0, 0, 0)),
    )


def _decay(a_col, a_row):
    i0 = lax.broadcasted_iota(jnp.int32, (_CHUNK, _CHUNK), 0)
    i1 = lax.broadcasted_iota(jnp.int32, (_CHUNK, _CHUNK), 1)
    return jnp.where(i0 >= i1, jnp.exp(a_col - a_row), 0.0), jnp.where(i1 >= i0, jnp.exp(a_row - a_col), 0.0)


def _rsum(v):
    return jnp.sum(v, axis=-1, keepdims=True)


def _ssd_fwd(act, dt, acum, a_row, *, name):
    t = act.shape[0]
    nc = t // _CHUNK
    sp = _ssd_in_specs(lambda c: c)
    gw = _GROUPS * _NSTATE

    def body(xs_ref, bc_ref, dt_ref, ac_ref, ar_ref, y_ref, sall_ref, st):
        @pl.when(pl.program_id(0) == 0)
        def _():
            st[...] = jnp.zeros_like(st)

        low = _low_lanes((_CHUNK, _LANES))
        for g in range(_GROUPS):
            bg = bc_ref[:, g * _NSTATE:(g + 1) * _NSTATE]
            cg = bc_ref[:, gw + g * _NSTATE:gw + (g + 1) * _NSTATE].astype(_BF)
            cb = _dot(cg, bg, _NT)
            for pr in range(g * _HPG // 2, (g + 1) * _HPG // 2):
                ha, hb = 2 * pr, 2 * pr + 1
                a_a, a_b = ac_ref[:, ha:ha + 1], ac_ref[:, hb:hb + 1]
                x = (xs_ref[:, _pair(pr)] * jnp.where(low, dt_ref[:, ha:ha + 1], dt_ref[:, hb:hb + 1])).astype(_BF)
                lm_a, _ = _decay(a_a, ar_ref[ha])
                lm_b, _ = _decay(a_b, ar_ref[hb])
                sv = st[pr]
                sall_ref[0, pr] = sv
                yd = _dot(jnp.concatenate([cb * lm_a, cb * lm_b], axis=0), x)
                yd = jnp.where(low, yd[:_CHUNK], yd[_CHUNK:])
                y_ref[:, _pair(pr)] = yd + jnp.where(low, jnp.exp(a_a), jnp.exp(a_b)) * _dot(cg, sv)
                al_a, al_b = jnp.min(a_a, axis=0, keepdims=True), jnp.min(a_b, axis=0, keepdims=True)
                upd = _dot(jnp.concatenate([bg * jnp.exp(al_a - a_a), bg * jnp.exp(al_b - a_b)], axis=1), x, _TN)
                st[pr] = jnp.where(low, jnp.exp(al_a), jnp.exp(al_b)) * sv + jnp.where(low, upd[:_NSTATE], upd[_NSTATE:])

    return pl.pallas_call(
        body, grid=(nc,), in_specs=[sp['xs'], sp['bc'], sp['lane'], sp['lane'], sp['arow']],
        out_specs=[sp['xs'], sp['st']], out_shape=[_SDS((t, _AW), F32), _SDS((nc, _HEADS // 2, _NSTATE, _LANES), F32)],
        scratch_shapes=[pltpu.VMEM((_HEADS // 2, _NSTATE, _LANES), F32)], name=name,
        compiler_params=_params(("arbitrary",), 16 << 20))(*[_hbm(a) for a in (act, act, dt, acum, a_row)])


def _ssd_bwd(act, dt, acum, a_row, sall, dy, *, name):
    t = act.shape[0]
    nc = t // _CHUNK
    sp = _ssd_in_specs(lambda c: nc - 1 - c)
    gw = _GROUPS * _NSTATE

    def body(xs_ref, bc_ref, dt_ref, ac_ref, ar_ref, sall_ref, dy_ref, dxs_ref, dbc_ref, ddt_ref, da_ref, dst):
        @pl.when(pl.program_id(0) == 0)
        def _():
            dst[...] = jnp.zeros_like(dst)

        ddt_ref[...] = jnp.zeros_like(ddt_ref)
        da_ref[...] = jnp.zeros_like(da_ref)
        row = lax.broadcasted_iota(jnp.int32, (_CHUNK, 1), 0)
        low = _low_lanes((_CHUNK, _LANES))
        for g in range(_GROUPS):
            bg = bc_ref[:, g * _NSTATE:(g + 1) * _NSTATE]
            bgb = bg.astype(_BF)
            cg = bc_ref[:, gw + g * _NSTATE:gw + (g + 1) * _NSTATE].astype(_BF)
            cb, cbt = _dot(cg, bgb, _NT), _dot(bgb, cg, _NT)
            dcb = jnp.zeros((_CHUNK, _CHUNK), F32)
            dbg = jnp.zeros((_CHUNK, _NSTATE), F32)
            dcg = jnp.zeros((_CHUNK, _NSTATE), F32)
            for pr in range(g * _HPG // 2, (g + 1) * _HPG // 2):
                heads = (2 * pr, 2 * pr + 1)
                a_cols = [ac_ref[:, h:h + 1] for h in heads]
                dt_pair = jnp.where(low, dt_ref[:, heads[0]:heads[0] + 1], dt_ref[:, heads[1]:heads[1] + 1])
                xsv = xs_ref[:, _pair(pr)]
                x = xsv * dt_pair
                xb = x.astype(_BF)
                xhs = _halves(xb, low)
                dyv = dy_ref[:, _pair(pr)]
                dyb = dyv.astype(_BF)
                dyhs = _halves(dyb, low)
                sv, dsv = sall_ref[0, pr], dst[pr]
                svb, dsb = sv.astype(_BF), dsv.astype(_BF)
                a_lasts = [jnp.min(a, axis=0, keepdims=True) for a in a_cols]
                e_pair = jnp.where(low, jnp.exp(a_cols[0]), jnp.exp(a_cols[1]))
                el_pair = jnp.where(low, jnp.exp(a_lasts[0]), jnp.exp(a_lasts[1]))
                yo = e_pair * _dot(cg, svb)
                decays = [_decay(a_cols[i], ar_ref[h]) for i, h in enumerate(heads)]
                gms, gmts = [cb * lm for lm, _ in decays], [cbt * lmt for _, lmt in decays]
                w_cols = [jnp.exp(a_lasts[i] - a_cols[i]) for i in range(2)]
                x2, dy2 = jnp.concatenate(xhs, axis=0), jnp.concatenate(dyhs, axis=0)
                bwd = _dot(jnp.concatenate([bg * w_cols[0], bg * w_cols[1]], axis=0), dsb)
                dxg = _dot(jnp.concatenate(gms, axis=1), dyb, _TN)
                dg2, dgt2, xds2 = _dot(dy2, xb, _NT), _dot(x2, dyb, _NT), _dot(x2, dsb, _NT)
                das = []
                for i in range(2):
                    rows_i = slice(i * _CHUNK, (i + 1) * _CHUNK)
                    dcb = dcb + dg2[rows_i] * decays[i][0]
                    dbg = dbg + w_cols[i] * xds2[rows_i]
                    das.append(_rsum(dg2[rows_i] * gms[i]) - _rsum(dgt2[rows_i] * gmts[i]))
                bwd = jnp.where(low, bwd[:_CHUNK], bwd[_CHUNK:])
                dx = jnp.where(low, dxg[:_CHUNK], dxg[_CHUNK:]) + bwd
                edy = (e_pair * dyv).astype(_BF)
                dcg = dcg + _dot(edy, svb, _NT)
                zs, yos, sds, dts = (_halves(v, low) for v in (x * bwd, dyv * yo, sv * dsv, dx * xsv))
                for i, h in enumerate(heads):
                    z = _rsum(zs[i])
                    da_last = jnp.sum(z, axis=0, keepdims=True) + jnp.exp(a_lasts[i]) * jnp.sum(_rsum(sds[i]), axis=0, keepdims=True)
                    da_ref[:, h:h + 1] = das[i] + _rsum(yos[i]) - z + jnp.where(row == _CHUNK - 1, da_last, 0.0)
                    ddt_ref[:, h:h + 1] = _rsum(dts[i])
                dxs_ref[:, _pair(pr)] = dx * dt_pair
                dst[pr] = el_pair * dsv + _dot(cg, edy, _TN)
            dbc_ref[:, g * _NSTATE:(g + 1) * _NSTATE] = dbg + _dot(dcb, cg, _TN)
            dbc_ref[:, gw + g * _NSTATE:gw + (g + 1) * _NSTATE] = dcg + _dot(dcb, bgb)

    ch = lambda c: nc - 1 - c
    wide = pl.BlockSpec((_CHUNK, 2 * gw), lambda c: (ch(c), 0))
    return pl.pallas_call(
        body, grid=(nc,), in_specs=[sp['xs'], sp['bc'], sp['lane'], sp['lane'], sp['arow'], sp['st'], sp['xs']],
        out_specs=[sp['xs'], wide, sp['lane'], sp['lane']],
        out_shape=[_SDS((t, _AW), F32), _SDS((t, 2 * gw), F32), _SDS((t, _LANES), F32), _SDS((t, _LANES), F32)],
        scratch_shapes=[pltpu.VMEM((_HEADS // 2, _NSTATE, _LANES), F32)], name=name,
        compiler_params=_params(("arbitrary",), 16 << 20))(*[_hbm(a) for a in (act, act, dt, acum, a_row, sall, dy)])


def _scan_rows(v, reverse):
    r = lax.broadcasted_iota(jnp.int32, v.shape, 0)
    for s in (1, 2, 4, 8, 16, 32, 64):
        if reverse:
            v = v + jnp.where(r < _CHUNK - s, pltpu.roll(v, _CHUNK - s, 0), 0.0)
        else:
            v = v + jnp.where(r >= s, pltpu.roll(v, s, 0), 0.0)
    return v


def _softplus(x):
    return jnp.maximum(x, 0.0) + jnp.log(1.0 + jnp.exp(-jnp.abs(x)))


def _sigmoid(x):
    return 1.0 / (1.0 + jnp.exp(-x))


def _silu(x):
    return x * _sigmoid(x)


def _dsilu(x):
    s = _sigmoid(x)
    return s * (1.0 + x * (1.0 - s))


def _lanes(a):
    return jnp.pad(a, (0, _LANES - a.shape[0])).reshape(1, _LANES)


def _layer_fwd(x, p, l):
    cch = p['conv_w'].shape[1]
    sv = {}
    h1 = _rms_fwd(x, p['ln1_g'], name=f"ln1_fwd_{l}")
    qkv = _mm(h1, p['w_in'], b_cols=(0, 3 * _AW), outs=(_BF,), name=f"in_proj_qkv_{l}")
    xbc = _mm(h1, p['w_in'], b_cols=(3 * _AW, cch), name=f"in_proj_xbc_{l}")
    zdt = _mm(h1, p['w_in'], b_cols=(3 * _AW + cch, _AW + _LANES), name=f"in_proj_zdt_{l}")
    z, dt_raw = (zdt, _AW, 0), (zdt, _LANES, _AW // _LANES)

    tile = 2 * _ABLK
    perms = [_perm(d, tile) for d in _DILATIONS[1:]]
    views = [qkv] + list(_rows(lambda a, p2, p3: (_stride(a, p2, _DILATIONS[1]), _stride(a, p3, _DILATIONS[2])), [qkv], perms,
                               [(3 * _AW, _BF, d) for d in _DILATIONS[1:]], tile=tile, name=f"qkv_strided_{l}"))
    outs = []
    for dil, view in zip(_DILATIONS, views):
        outs += _attn_fwd(view, dil, name=f"attn_fwd_d{dil}_{l}")

    def combine(o1, l1, o2, l2, o3, l3, p2, p3):
        m = jnp.maximum(jnp.maximum(l1, l2), l3)
        e1, e2, e3 = jnp.exp(l1 - m), jnp.exp(l2 - m), jnp.exp(l3 - m)
        tot = e1 + e2 + e3
        mixed = sum(_expand_heads(e / tot) * o for e, o in ((e1, o1.astype(F32)), (e2, _unstride(o2, p2)), (e3, _unstride(o3, p3))))
        return mixed, m + jnp.log(tot)
    outs = [a if i % 2 or i == 0 else ("strided", a, _DILATIONS[i // 2]) for i, a in enumerate(outs)]
    attn, lse = _rows(combine, outs, perms, [(_AW, F32), (_LANES, F32)], tile=tile, name=f"attn_combine_{l}")
    mix = _rms_fwd(attn, p['attn_norm_g'], into=(None, 2 * _AW, 0), name=f"attn_norm_fwd_{l}")

    def conv(u0, before, w, b):
        u1, u2, u3 = _shifted(u0, before, True)
        return _silu(w[0:1] * u3 + w[1:2] * u2 + w[2:3] * u1 + w[3:4] * u0 + b)
    act = _rows(conv, [xbc], [p['conv_w'], p['conv_b'].reshape(1, cch)], [(cch, F32)], halos=[(xbc, -1)], tile=_tile_for(cch),
                name=f"conv_fwd_{l}")[0]

    def dtf(raw, bias, alog):
        dt = _softplus(raw + bias)
        return dt, _scan_rows(dt * -jnp.exp(alog), False)
    dt, acum = _rows(dtf, [dt_raw], [_lanes(p['dt_bias']), _lanes(p['a_log'])], [(_LANES, F32), (_LANES, F32)],
                     tile=_CHUNK, name=f"dt_fwd_{l}")
    a_row = acum[:, :_HEADS].T[:, None, :]
    y_ssd, sall = _ssd_fwd(act, dt, acum, a_row, name=f"ssd_fwd_{l}")
    dskip = jnp.repeat(p['d_skip'], _HDIM).reshape(1, _AW)
    xs = (act, _AW, 0)

    def gate(y, xs, z, dsk):
        return (y + dsk * xs) * _silu(z)
    y2 = _rows(gate, [y_ssd, xs, z], [dskip], [(_AW, F32)], tile=_tile_for(_AW), name=f"gate_fwd_{l}")[0]
    mix = _rms_fwd(y2, p['ssd_norm_g'], groups=_GROUPS, into=(mix, 2 * _AW, 1), name=f"ssd_norm_fwd_{l}")
    sv.update(x=x, h1=h1, qkv=views, zdt=zdt, xbc=xbc, attn=attn, lse=lse, act=act, dt=dt, acum=acum, a_row=a_row,
              sall=sall, y_ssd=y_ssd, dskip=dskip, y2=y2, mix=mix)
    return mix, sv


def _layer_fwd_mlp(p, sv, l):
    x2 = _mm(sv['mix'], p['w_out'], extra=(sv['x'],), epi=_add_to, name=f"out_proj_{l}")
    h2 = _rms_fwd(x2, p['ln2_g'], name=f"ln2_fwd_{l}")
    a = _mm(h2, p['w_mlp_in'], b_chips=_CHIPS, epi=lambda acc: (jnp.square(jnp.maximum(acc, 0.0)),), outs=(_BF,), name=f"mlp_in_{l}")
    x3 = _mm(a, p['w_mlp_out'], extra=(x2,), epi=_add_to, name=f"mlp_out_{l}")
    sv.update(x2=x2, h2=h2, a=a)
    return x3


def _layer_bwd(dx3, dx3b, p, sv, l, send, after):
    cch = p['conv_w'].shape[1]
    g = {}
    du = _mm(dx3b, p['w_mlp_out'], tb=True, extra=(sv['a'],), outs=(_BF,), after=after,
             epi=lambda acc, a: (acc * 2.0 * jnp.sqrt(a.astype(F32)),), name=f"mlp_out_dx_{l}")
    g['w_mlp_out'] = _mm(sv['a'], dx3b, ta=True, outs=(_BF,), name=f"mlp_out_dw_{l}")
    g['w_mlp_in'] = _mm(sv['h2'], du, ta=True, out_chips=_CHIPS, outs=(_BF,), name=f"mlp_in_dw_{l}")
    sent = send(('w_mlp_out', 'w_mlp_in'), g)
    dh2 = _mm(du, p['w_mlp_in'], tb=True, b_chips=_CHIPS, after=sent, name=f"mlp_in_dx_{l}")
    dx2, dx2b, g['ln2_g'] = _rms_bwd(sv['x2'], dh2, p['ln2_g'], dx3, name=f"ln2_bwd_{l}")
    dmix = _mm(dx2b, p['w_out'], tb=True, name=f"out_proj_dx_{l}")
    g['w_out'] = _mm(sv['mix'], dx2b, ta=True, outs=(_BF,), name=f"out_proj_dw_{l}")
    after_out = send(('w_out',), g)

    tile = 2 * _ABLK
    perms = [_perm(d, tile) for d in _DILATIONS[1:]]

    def norm_bwd(attn, dy, lse, gn, p2, p3):
        dattn, dgn = _rms_bwd_tile(attn, dy, gn, 1)
        prod, low = dattn * attn, _low_lanes((attn.shape[0], _LANES))
        lane = lax.broadcasted_iota(jnp.int32, lse.shape, 1)
        ld = jnp.where(lane < _HEADS, lse, 0.0)
        for pr in range(_HEADS // 2):
            for i, part in enumerate(_halves(prod[:, _pair(pr)], low)):
                ld = jnp.where(lane == _HEADS + 2 * pr + i, _rsum(part), ld)
        return dattn, _stride(dattn, p2, _DILATIONS[1]), _stride(dattn, p3, _DILATIONS[2]), ld, dgn
    *dos, ld, gn_sum = _rows(norm_bwd, [sv['attn'], (dmix, _AW, 0), sv['lse']], [p['attn_norm_g'].reshape(1, _AW)] + perms,
                             [(_AW, _BF)] + [(_AW, _BF, d) for d in _DILATIONS[1:]] + [(_LANES, F32)], [_AW], after=after_out,
                             tile=tile, name=f"attn_norm_bwd_{l}")
    g['attn_norm_g'] = gn_sum.sum(axis=0)
    parts = [_attn_bwd(view, do, ld, dil, name=f"attn_bwd_d{dil}_{l}") for view, do, dil in zip(sv['qkv'], dos, _DILATIONS)]

    def branch_sum(*t):
        parts_, (p2, p3) = t[:9], t[9:]
        t = [a.astype(F32) for a in parts_[:3]] + [_unstride(a, p2) for a in parts_[3:6]] + [_unstride(a, p3) for a in parts_[6:]]
        return jnp.concatenate([t[i] + t[3 + i] + t[6 + i] for i in range(3)], axis=1)
    branch_ins = list(parts[0]) + [("strided", a, d) for pr, d in zip(parts[1:], _DILATIONS[1:]) for a in pr]
    w_all = 3 * _AW + cch + _AW + _LANES
    dproj = _rows(branch_sum, branch_ins, perms, [(3 * _AW, _BF)], into=(None, w_all, 0), tile=tile, name=f"attn_bwd_sum_{l}")[0]

    xs, z, dt_raw = (sv['act'], _AW, 0), (sv['zdt'], _AW, 0), (sv['zdt'], _LANES, _AW // _LANES)

    def gate_bwd(y2, dy, y, xs, z, dsk, gn):
        dy2, dgn = _rms_bwd_tile(y2, dy, gn, _GROUPS)
        dy1 = dy2 * _silu(z)
        return dy1, dsk * dy1, dy2 * (y + dsk * xs) * _dsilu(z), dy1 * xs, dgn
    dy1, dxs_skip, dz, dsk_sum, gn_sum = _rows(
        gate_bwd, [sv['y2'], (dmix, _AW, 1), sv['y_ssd'], xs, z], [sv['dskip'], p['ssd_norm_g'].reshape(1, _AW)],
        [(_AW, F32), (_AW, F32), (_AW, _BF)], [_AW, _AW], tile=128, name=f"gate_bwd_{l}")
    g['ssd_norm_g'] = gn_sum.sum(axis=0)
    g['d_skip'] = dsk_sum.sum(axis=0).reshape(_HEADS, _HDIM).sum(axis=1)
    dxs, dbc, ddt, da = _ssd_bwd(sv['act'], sv['dt'], sv['acum'], sv['a_row'], sv['sall'], dy1, name=f"ssd_bwd_{l}")

    def dtb(da, ddtx, raw, dt, dz, bias, alog):
        a = -jnp.exp(alog)
        dda = _scan_rows(da, True)
        draw = (dda * a + ddtx) * _sigmoid(raw + bias)
        return jnp.concatenate([dz, draw.astype(dz.dtype)], axis=1), draw, dda * dt * a
    dproj, dbias, dalog = _rows(dtb, [da, ddt, dt_raw, sv['dt'], dz], [_lanes(p['dt_bias']), _lanes(p['a_log'])],
                                [(_AW + _LANES, _BF)], [_LANES, _LANES], into=(dproj, w_all, (3 * _AW + cch) // (_AW + _LANES)),
                                tile=_CHUNK, name=f"dt_bwd_{l}")
    g['dt_bias'], g['a_log'] = dbias.sum(axis=0)[:_HEADS], dalog.sum(axis=0)[:_HEADS]
    def conv_bwd1(u0, dxs, dbc, dxk, before, w, b):
        u1, u2, u3 = _shifted(u0, before, True)
        pre = w[0:1] * u3 + w[1:2] * u2 + w[2:3] * u1 + w[3:4] * u0 + b
        dp = jnp.concatenate([dxs + dxk, dbc], axis=1) * _dsilu(pre)
        return dp, dp * u3, dp * u2, dp * u1, dp * u0, dp
    dpre, *dws = _rows(conv_bwd1, [sv['xbc'], dxs, dbc, dxs_skip], [p['conv_w'], p['conv_b'].reshape(1, cch)], [(cch, F32)],
                       [cch] * 5, halos=[(sv['xbc'], -1)], tile=128, name=f"conv_bwd_pre_{l}")
    g['conv_w'] = jnp.stack([dws[i].sum(axis=0) for i in range(_CONV_K)])
    g['conv_b'] = dws[4].sum(axis=0)

    def conv_bwd2(p0, after_, w):
        p1, p2, p3 = _shifted(p0, after_, False)
        return w[3:4] * p0 + w[2:3] * p1 + w[1:2] * p2 + w[0:1] * p3
    dproj = _rows(conv_bwd2, [dpre], [p['conv_w']], [(cch, _BF)], halos=[(dpre, 1)], into=(dproj, w_all, 3 * _AW // cch),
                  tile=_tile_for(cch), name=f"conv_bwd_in_{l}")[0]
    g_all = _mm(sv['h1'], dproj, ta=True, outs=(_BF,), name=f"in_proj_dw_{l}")
    g['w_in'] = _w_in_to_chips(g_all, cch, _CHIPS, name=f"w_in_by_chip_{l}")
    sent = send(('w_in',), g)
    for n in _BIG:
        del g[n]
    dh1 = _mm(dproj, p['w_in'], tb=True, after=sent, name=f"in_proj_dx_{l}")
    dx, dxb, g['ln1_g'] = _rms_bwd(sv['x'], dh1, p['ln1_g'], dx2, name=f"ln1_bwd_{l}")
    return dx, dxb, g


def _loss_bwd(x, g, tgt):
    w = x.shape[1]
    tile = _NORM_TILE

    def fn(x, tgt, g):
        r = _rstd(x)
        xh = x * r
        e = xh * g - tgt
        gd = e * (g / w)
        dx = r * (gd - xh * jnp.mean(gd * xh, axis=-1, keepdims=True))
        rowloss = 0.5 * jnp.mean(e * e, axis=-1, keepdims=True)
        return dx, dx, (e / w) * xh, jnp.broadcast_to(rowloss, (tile, _LANES))
    dx, dxb, dg, ls = _rows(fn, [x, tgt], [g.reshape(1, w)], [(w, F32), (w, _BF)], [w, _LANES], tile=tile, name="loss_head")
    return dx, dxb, dg.sum(axis=0), ls[:, 0].sum()


def _adamw_math(w, g, m, v):
    m2 = _B1 * m + (1.0 - _B1) * g
    v2 = _B2 * v + (1.0 - _B2) * jnp.square(g)
    m_hat = m2 / (1.0 - _B1 ** _STEP)
    v_hat = v2 / (1.0 - _B2 ** _STEP)
    return -_LR * (m_hat / (jnp.sqrt(v_hat) + _AEPS) + _WD * w), m2, v2


def _adamw(w, g, m, v, *, name):
    width = w.shape[-1]
    flat = [a.reshape(-1, width) for a in (w, g, m, v)]
    tile = _pick(flat[0].shape[0], (_tile_for(width), 32, 8))
    res = _rows(_adamw_math, flat, [], [(width, F32)] * 3, tile=tile, name=name)
    return [r.reshape(w.shape) for r in res]


_HBM = pl.BlockSpec(memory_space=pltpu.HBM)


def _place():
    x, y, c = lax.axis_index("x"), lax.axis_index("y"), lax.axis_index("c")
    other_chips = [(1 - x, y), (x, 1 - y), (1 - x, 1 - y)]
    return x, y, c, other_chips


def _remote(src, dst, sems, i, dev):
    return pltpu.make_async_remote_copy(src_ref=src, dst_ref=dst, send_sem=sems[0].at[i], recv_sem=sems[1].at[i],
                                        device_id=dev, device_id_type=_MESH)


def _exchange8(v, *, reduce, after=None, name):
    r, w = v.shape
    behind = [] if after is None else [after]

    def body(v_ref, *rest):
        all_ref, rest = rest[len(behind)], rest[len(behind) + 1:]
        sems = rest[-2:]
        x, y, c, _ = _place()
        me = 4 * x + 2 * y + c
        all_ref[me] = v_ref[...]
        flips = [((d >> 2) & 1, (d >> 1) & 1, d & 1) for d in range(1, 8)]
        sends = [_remote(v_ref, all_ref.at[me], sems, i, (x ^ fx, y ^ fy, c ^ fc)) for i, (fx, fy, fc) in enumerate(flips)]
        for cp in sends:
            cp.start()
        for i, (fx, fy, fc) in enumerate(flips):
            _remote(v_ref, all_ref.at[me ^ (4 * fx + 2 * fy + fc)], sems, i, (x ^ fx, y ^ fy, c ^ fc)).wait_recv()
        for cp in sends:
            cp.wait_send()
        if reduce:
            acc = all_ref[0]
            for s in range(1, 8):
                acc = acc + all_ref[s]
            rest[0][...] = acc

    vm = pl.BlockSpec(memory_space=pltpu.VMEM)
    out_shape = [_SDS((8, r, w), v.dtype)] + ([_SDS((r, w), v.dtype)] if reduce else [])
    res = pl.pallas_call(body, in_specs=[vm] + [_ANY] * len(behind), out_specs=[vm] * len(out_shape), out_shape=out_shape, name=name,
                         scratch_shapes=[pltpu.SemaphoreType.DMA((7,)), pltpu.SemaphoreType.DMA((7,))],
                         compiler_params=pltpu.CompilerParams(vmem_limit_bytes=int(32 << 20)))(v, *behind)
    return res[1] if reduce else res[0]


_SEM = pl.BlockSpec(memory_space=pltpu.SEMAPHORE)
_ANY = pl.BlockSpec(memory_space=pl.ANY)
_EFFECT = pltpu.SideEffectType.DATAFLOW_SIDE_EFFECTING


def _send_start(name, srcs, land_shapes, plan, n_sends, after):
    ns, nl = len(srcs), len(land_shapes)
    zones = [_hbm(lax.empty(s.shape, s.dtype)) if isinstance(s, _SDS) else s for s in land_shapes]

    def body(*refs):
        ins, lands, sems = refs[:ns], refs[ns:ns + nl], refs[ns + nl + 1:ns + nl + 3]
        x, y, c, chips = _place()
        for i, (s, d, dev) in enumerate(plan(x, y, c, chips, ins, lands)[0]):
            _remote(s, d, sems, i, dev).start()
        refs[-1][...] = jnp.zeros_like(refs[-1])

    sem = pltpu.SemaphoreType.DMA((n_sends,))
    res = pl.pallas_call(
        body, name=name, in_specs=[_HBM] * (ns + nl) + [_ANY],
        out_shape=(sem, sem, *[pltpu.HBM(s.shape, s.dtype) for s in land_shapes], _SDS((8, _LANES), F32)),
        out_specs=(_SEM, _SEM, *[_HBM] * nl, pl.BlockSpec(memory_space=pltpu.VMEM)),
        input_output_aliases={ns + i: 2 + i for i in range(nl)},
        compiler_params=pltpu.CompilerParams(has_side_effects=_EFFECT))(
            *[_hbm(s) for s in srcs], *zones, after)
    return dict(sems=res[:2], srcs=srcs, lands=res[2:2 + nl], plan=plan), res[-1]


def _send_wait(name, h, after):
    ns, nl = len(h['srcs']), len(h['lands'])

    def body(*refs):
        ins, lands, sems = refs[:ns], refs[ns:ns + nl], refs[ns + nl:ns + nl + 2]
        x, y, c, chips = _place()
        sends, landings = h['plan'](x, y, c, chips, ins, lands)
        for i, (s, d, dev) in enumerate(sends):
            _remote(s, d, sems, i, dev).wait_send()
        for i, d in enumerate(landings):
            _remote(d, d, sems, i, sends[i][2]).wait_recv()

    return pl.pallas_call(
        body, name=name, in_specs=[_HBM] * (ns + nl) + [_SEM, _SEM, _ANY],
        out_shape=tuple(pltpu.HBM(a.shape, a.dtype) for a in h['lands']), out_specs=tuple([_HBM] * nl),
        input_output_aliases={ns + i: i for i in range(nl)},
        compiler_params=pltpu.CompilerParams(has_side_effects=_EFFECT))(
            *[_hbm(s) for s in h['srcs']], *h['lands'], *h['sems'], after)


def _gather_plan(items):
    def plan(x, y, c, chips, ins, lands):
        k = 2 * x + y
        to = [(px, py, c) for px, py in chips] + [(x, y, 1 - c)]
        sends = [(ins[si].at[l], lands[t].at[k], dev) for t, (si, l) in enumerate(items) for dev in to]
        return sends, [lands[t].at[2 * px + py] for t in range(len(items)) for px, py in chips + [(x, y)]]
    return plan


_FLIPS = [((d >> 2) & 1, (d >> 1) & 1, d & 1) for d in range(1, 8)]


def _reduce_plan(halves):
    def plan(x, y, c, chips, ins, lands):
        sends, landings = [], []
        for t, hf in enumerate(halves):
            for i, (fx, fy, fc) in enumerate(_FLIPS):
                px, py, pc = x ^ fx, y ^ fy, c ^ fc
                sends.append((ins[t].at[2 * px + py, pl.ds(pc * hf, hf)], lands[t].at[i], (px, py, pc)))
                landings.append(lands[t].at[i])
        return sends, landings
    return plan


def _swap(name, srcs, out_shapes, plan, n_sends):
    n = len(srcs)

    def body(*refs):
        ins, outs, sems = refs[:n], refs[n:n + len(out_shapes)], refs[-2:]
        x, y, c, chips = _place()
        sends, landings = plan(x, y, c, chips, ins, outs)
        out = [_remote(s, d, sems, i, dev) for i, (s, d, dev) in enumerate(sends)]
        for cp in out:
            cp.start()
        for i, d in enumerate(landings):
            _remote(d, d, sems, i, sends[i][2]).wait_recv()
        for cp in out:
            cp.wait_send()

    return pl.pallas_call(
        body, in_specs=[_HBM] * n, out_specs=[_HBM] * len(out_shapes), out_shape=out_shapes, name=name,
        scratch_shapes=[pltpu.SemaphoreType.DMA((n_sends,)), pltpu.SemaphoreType.DMA((n_sends,))])(*srcs)


def _sum_owned(grads, landed, c, k, names):
    def sum8(*parts):
        acc = parts[0].astype(F32)
        for p in parts[1:]:
            acc = acc + p.astype(F32)
        return acc
    outs = []
    for g, got, name in zip(grads, landed, names):
        hf, b = got.shape[1:]
        own = lax.dynamic_slice_in_dim(lax.dynamic_index_in_dim(g, k, axis=0, keepdims=False), c * hf, hf, axis=0)
        outs.append(_rows(sum8, [own] + [("slot", got, i) for i in range(len(_FLIPS))], [], [(b, F32)],
                          tile=_pick(hf, (_tile_for(b), 32)), name=f"grad_sum_{name}")[0])
    return outs


def _share_halves(mine, *, name):
    n = len(mine)

    def plan(x, y, c_, chips, ins, outs):
        return [(ins[t], outs[t], (x, y, 1 - c_)) for t in range(n)], [outs[t] for t in range(n)]
    return _swap(name, mine, [_SDS(h.shape, F32) for h in mine], plan, n)


def _adamw_owned(w, mine, theirs, m, v, c, *, name):
    depth, a, b = w.shape
    half = a // 2
    tile = _pick(half, (_tile_for(b), 32, 8))
    nh = half // tile

    def blocks_of(l):
        return lambda i: (jnp.clip(i - 2 * nh * l, 0, 2 * nh - 1) % nh, 0)

    def fn(w, m, v, *rest):
        halves, cflag = rest[:-1], rest[-1]
        step = pl.program_id(0)
        is_mine = cflag[0:1, 0:1] == ((step // nh) % 2).astype(F32)
        g = jnp.where(is_mine, halves[0], halves[1])
        for l in range(1, depth):
            g = jnp.where(step >= 2 * nh * l, jnp.where(is_mine, halves[2 * l], halves[2 * l + 1]), g)
        return (g,) + _adamw_math(w, g, m, v)
    ins = [("layers", a_) for a_ in (w, m, v)]
    ins += [(h, b, blocks_of(l)) for l in range(depth) for h in (mine[l], theirs[l])]
    res = _rows(fn, ins, [jnp.full((1, _LANES), c, F32)], [(b, F32)] * 4, tile=tile, name=name)
    return [r.reshape(w.shape) for r in res]


_BIG = ("w_in", "w_out", "w_mlp_in", "w_mlp_out")
_SMALL = ("ln1_g", "conv_b", "dt_bias", "a_log", "d_skip", "attn_norm_g", "ssd_norm_g", "ln2_g", "final_norm_g")
_ORDER = ("ln1_g", "w_in", "conv_w", "conv_b", "dt_bias", "a_log", "d_skip", "attn_norm_g", "ssd_norm_g", "w_out", "ln2_g",
          "w_mlp_in", "w_mlp_out", "final_norm_g")


def _pack(parts, rows):
    flat = jnp.concatenate([p.reshape(-1) for p in parts])
    return jnp.pad(flat, (0, rows * _LANES - flat.shape[0])).reshape(rows, _LANES)


def _unpack(buf, like):
    flat, out, o = buf.reshape(-1), [], 0
    for p in like:
        out.append(flat[o:o + p.size].reshape(p.shape))
        o += p.size
    return out


def kernel(x, ln1_g, w_in, conv_w, conv_b, dt_bias, a_log, d_skip, attn_norm_g, ssd_norm_g, w_out, ln2_g, w_mlp_in, w_mlp_out, final_norm_g, loss_target, m_ln1_g, m_w_in, m_conv_w, m_conv_b, m_dt_bias, m_a_log, m_d_skip, m_attn_norm_g, m_ssd_norm_g, m_w_out, m_ln2_g, m_w_mlp_in, m_w_mlp_out, m_final_norm_g, v_ln1_g, v_w_in, v_conv_w, v_conv_b, v_dt_bias, v_a_log, v_d_skip, v_attn_norm_g, v_ssd_norm_g, v_w_out, v_ln2_g, v_w_mlp_in, v_w_mlp_out, v_final_norm_g):
    w = dict(ln1_g=ln1_g, w_in=w_in, conv_w=conv_w, conv_b=conv_b, dt_bias=dt_bias, a_log=a_log, d_skip=d_skip,
             attn_norm_g=attn_norm_g, ssd_norm_g=ssd_norm_g, w_out=w_out, ln2_g=ln2_g, w_mlp_in=w_mlp_in, w_mlp_out=w_mlp_out,
             final_norm_g=final_norm_g)
    m = dict(ln1_g=m_ln1_g, w_in=m_w_in, conv_w=m_conv_w, conv_b=m_conv_b, dt_bias=m_dt_bias, a_log=m_a_log, d_skip=m_d_skip,
             attn_norm_g=m_attn_norm_g, ssd_norm_g=m_ssd_norm_g, w_out=m_w_out, ln2_g=m_ln2_g, w_mlp_in=m_w_mlp_in,
             w_mlp_out=m_w_mlp_out, final_norm_g=m_final_norm_g)
    v = dict(ln1_g=v_ln1_g, w_in=v_w_in, conv_w=v_conv_w, conv_b=v_conv_b, dt_bias=v_dt_bias, a_log=v_a_log, d_skip=v_d_skip,
             attn_norm_g=v_attn_norm_g, ssd_norm_g=v_ssd_norm_g, w_out=v_w_out, ln2_g=v_ln2_g, w_mlp_in=v_w_mlp_in,
             w_mlp_out=v_w_mlp_out, final_norm_g=v_final_norm_g)
    depth, d_model = ln1_g.shape
    n_chips = _CHIPS
    c = lax.axis_index("c")
    chip = 2 * lax.axis_index("x") + lax.axis_index("y")
    cch = conv_w.shape[2] * n_chips

    cw = _exchange8(conv_w.reshape(depth * _CONV_K, -1), reduce=False, name="gather_conv_w")[0::2]
    conv_full = cw.reshape(n_chips, depth, _CONV_K, -1).transpose(1, 2, 0, 3).reshape(depth, _CONV_K, cch)
    own = [w[n].astype(_BF) for n in _BIG]

    def start_gather(tag, items, after):
        lands = [_SDS((n_chips, *own[i].shape[1:]), _BF) for i, _ in items]
        return _send_start(f"gather_start_{tag}", own, lands, _gather_plan(items), n_chips * len(items), after)

    def finish_gather(tag, handle, items, after):
        landed = _send_wait(f"gather_wait_{tag}", handle, after)
        return {_BIG[i]: g for (i, _), g in zip(items, landed)}

    def layer_weights(l, blocks):
        p = {}
        if 'w_in' in blocks:
            p['w_in'] = _w_in_from_chips(blocks['w_in'], cch, name=f"w_in_regroup_{l}")
        if 'w_out' in blocks:
            p['w_out'] = blocks['w_out'].reshape(-1, d_model)
            p['w_mlp_in'] = blocks['w_mlp_in']
            p['w_mlp_out'] = blocks['w_mlp_out'].reshape(-1, d_model)
        return p

    groups = dict(a=[(0, 0)], b=[(1, 0), (2, 0), (3, 0)], c=[(0, 1)], d=[(1, 1), (2, 1), (3, 1)])
    handles, token = {}, conv_full

    half_in = own[0].shape[1] // 2

    def rows_of(ref, who):
        return ref.at[pl.ds(who * half_in, half_in)]

    def plan_a(x_, y_, c_, chips, ins, lands):
        k = 2 * x_ + y_
        sends = [(rows_of(ins[0].at[0], c_), rows_of(lands[0].at[k], c_), (px, py, c_)) for px, py in chips]
        sends.append((ins[0].at[0], lands[0].at[k], (x_, y_, 1 - c_)))
        return sends, [rows_of(lands[0].at[2 * px + py], c_) for px, py in chips] + [lands[0].at[k]]

    def plan_pass(x_, y_, c_, chips, ins, lands):
        sends = [(rows_of(lands[0].at[2 * px + py], c_),) * 2 + ((x_, y_, 1 - c_),) for px, py in chips]
        return sends, [rows_of(lands[0].at[2 * px + py], 1 - c_) for px, py in chips]
    handles["a"], token = _send_start("gather_start_a", own[:1], [_SDS((n_chips, *own[0].shape[1:]), _BF)], plan_a, n_chips, token)
    for tag, items in list(groups.items())[1:]:
        handles[tag], token = start_gather(tag, items, token)
    landed = _send_wait("gather_land_a", handles["a"], token)
    handles["a"], token = _send_start("gather_pass_a", [], landed, plan_pass, 3, landed[0])
    layers = [{n: w[n][l] for n in _SMALL[:-1]} for l in range(depth)]
    for l in range(depth):
        layers[l]['conv_w'] = conv_full[l]

    layers[0].update(layer_weights(0, finish_gather("a", handles["a"], groups["a"], token)))
    mix, sv0 = _layer_fwd(x[0], layers[0], 0)
    layers[0].update(layer_weights(0, finish_gather("b", handles["b"], groups["b"], mix)))
    h = _layer_fwd_mlp(layers[0], sv0, 0)
    layers[1].update(layer_weights(1, finish_gather("c", handles["c"], groups["c"], h)))
    mix, sv1 = _layer_fwd(h, layers[1], 1)
    layers[1].update(layer_weights(1, finish_gather("d", handles["d"], groups["d"], mix)))
    h = _layer_fwd_mlp(layers[1], sv1, 1)
    saved = [sv0, sv1]

    def by_chip(g, name):
        if name in ("w_mlp_in", "w_in"):
            return g
        return g.reshape(n_chips, -1, d_model)

    pending = []

    def sender(l):
        def send(names, g):
            srcs = [by_chip(g[n], n) for n in names]
            halves = [s.shape[1] // 2 for s in srcs]
            lands = [_SDS((len(_FLIPS), hf, s.shape[2]), _BF) for s, hf in zip(srcs, halves)]
            handle, tok = _send_start(f"grad_start_{names[-1]}_{l}", srcs, lands, _reduce_plan(halves), len(_FLIPS) * len(srcs), srcs[0])
            pending.append((l, names, srcs, handle))
            return tok
        return send

    dx, dxb, g_final, loss_part = _loss_bwd(h, final_norm_g, loss_target[0])
    grads, after = [None] * depth, None
    for l in reversed(range(depth)):
        dx, dxb, grads[l] = _layer_bwd(dx, dxb, layers[l], saved[l], l, sender(l), after)
        after = dx
    landed_of = {}
    sent_in = {(n, l): (gi, j) for gi, (l, names, _, _) in enumerate(pending) for j, n in enumerate(names)}

    def landed_for(gi, after):
        if gi not in landed_of:
            l, names, _, handle = pending[gi]
            landed_of[gi] = _send_wait(f"grad_wait_{names[-1]}_{l}", handle, after)
        return landed_of[gi]

    red, delta, new_m, new_v = {}, {}, {}, {}
    after = dx
    for n in ("w_mlp_out", "w_mlp_in", "w_out", "w_in"):
        mine = []
        for l in range(depth):
            gi, j = sent_in[(n, l)]
            got = landed_for(gi, after)[j]
            mine.append(_sum_owned([pending[gi][2][j]], [got], c, chip, [f"{n}_{l}"])[0])
        theirs = _share_halves(mine, name=f"grad_share_{n}")
        red[n], delta[n], new_m[n], new_v[n] = _adamw_owned(w[n], mine, theirs, m[n], v[n], c, name=f"adamw_{n}")
        after = delta[n]

    small = {n: jnp.stack([grads[l][n] for l in range(depth)]) for n in _SMALL[:-1] + ("conv_w",)}
    small["final_norm_g"] = g_final
    parts = [loss_part.reshape(1)] + [small[n] for n in _SMALL + ("conv_w",)]
    rows = -(-sum(p.size for p in parts) // 1024) * 8
    tot = _unpack(_exchange8(_pack(parts, rows), reduce=True, after=red[_BIG[0]], name="allreduce_small"), parts)
    loss = tot[0][0]
    red.update(zip(_SMALL + ("conv_w",), tot[1:]))
    red["conv_w"] = lax.dynamic_index_in_dim(red["conv_w"].reshape(depth, _CONV_K, n_chips, -1), chip, axis=2, keepdims=False)

    names = _SMALL + ("conv_w",)
    like = [w[n] for n in names]
    srows = -(-sum(p.size for p in like) // 1024) * 8
    res = _adamw(*[_pack([d[n] for n in names], srows) for d in (w, red, m, v)], name="adamw_small")
    for dst, buf in zip((delta, new_m, new_v), res):
        dst.update(zip(names, _unpack(buf, like)))
    return (loss, dx[None], *[red[n] for n in _ORDER], *[delta[n] for n in _ORDER], *[new_m[n] for n in _ORDER],
            *[new_v[n] for n in _ORDER])
```

```python
import numpy as np
import jax
import jax.numpy as jnp
from jax import lax
from jax.experimental import pallas as pl
from jax.experimental.pallas import tpu as pltpu

F32 = jnp.float32
_BF = jnp.bfloat16
_NEG = -1e30
_EPS = 1e-5
_HEADS = 16
_HDIM = 64
_AW = _HEADS * _HDIM
_ABLK = 128
_DILATIONS = (1, 4, 16)
_CHUNK = 128
_NSTATE = 128
_GROUPS = 2
_HPG = _HEADS // _GROUPS
_CONV_K = 4
_LANES = 128
_CHIPS = 4
_LR, _B1, _B2, _AEPS, _WD, _STEP = 0.001, 0.9, 0.999, 1e-08, 0.01, 10
_VMEM_CAP = 56 * 1024 * 1024
_MESH = pl.DeviceIdType.MESH
_SDS = jax.ShapeDtypeStruct
_NT = (((1,), (1,)), ((), ()))
_TN = (((0,), (0,)), ((), ()))


def _params(sem, est_bytes):
    lim = int(min(max(2 * est_bytes + (4 << 20), 16 << 20), _VMEM_CAP))
    return pltpu.CompilerParams(dimension_semantics=sem, vmem_limit_bytes=lim)


def _nbytes(shape, dtype):
    return int(np.prod(shape)) * jnp.dtype(dtype).itemsize


def _hbm(a):
    return pltpu.with_memory_space_constraint(a, pltpu.HBM)


def _dot(a, b, dims=(((1,), (0,)), ((), ()))):
    return lax.dot_general(a.astype(_BF), b.astype(_BF), dims, preferred_element_type=F32)


_HALO = 8


def _rows(fn, ins, consts, outs, sums=(), *, halos=(), into=None, after=None, tile, name):
    first = ins[0]
    if isinstance(first, tuple) and isinstance(first[0], str) and first[0] == "layers":
        rows = first[1].shape[0] * first[1].shape[1]
    else:
        rows = (first[0] if isinstance(first, tuple) else first).shape[0]
    n_steps = rows // tile

    def norm_in(a):
        if not isinstance(a, tuple):
            return a, tile, a.shape[1], lambda i: (i, 0)
        if isinstance(a[0], str) and a[0] == "layers":
            return a[1], (None, tile, a[1].shape[2]), a[1].shape[2], lambda i, n=a[1].shape[1] // tile: (i // n, i % n, 0)
        if isinstance(a[0], str) and a[0] == "slot":
            return a[1], (None, tile, a[1].shape[2]), a[1].shape[2], lambda i, s=a[2]: (s, i, 0)
        if isinstance(a[0], str):
            return a[1], tile // a[2], a[1].shape[1], lambda i: (i, 0)
        return a[0], tile, a[1], a[2] if callable(a[2]) else (lambda i, j=a[2]: (i, j))
    ins = [norm_in(a) for a in ins]
    outs = [(w, dt, d[0] if d else 1) for w, dt, *d in outs]
    n_in, n_h, n_c, n_o, n_s = len(ins), len(halos), len(consts), len(outs), len(sums)
    n_x = int(into is not None and into[0] is not None) + int(after is not None)

    def body(*refs):
        step = pl.program_id(0)
        vals = [r[...] for r in refs[:n_in]]
        for r, (_, side) in zip(refs[n_in:n_in + n_h], halos):
            vals.append(jnp.where(step == (0 if side < 0 else n_steps - 1), 0.0, r[...]))
        vals += [r[...] for r in refs[n_in + n_h:n_in + n_h + n_c]]
        refs = refs[:n_in] + refs[n_in + n_h:]
        res = fn(*vals)
        res = res if isinstance(res, tuple) else (res,)
        orefs = refs[n_in + n_c + n_x:n_in + n_c + n_x + n_o]
        srefs = refs[n_in + n_c + n_x + n_o:]
        for r, v in zip(orefs, res[:n_o]):
            r[...] = v.astype(r.dtype)
        if n_s:
            @pl.when(pl.program_id(0) == 0)
            def _():
                for r in srefs:
                    r[...] = jnp.zeros_like(r)
            for r, v in zip(srefs, res[n_o:]):
                r[...] += v.reshape(tile // 8, 8, v.shape[-1]).sum(axis=0)

    per = tile // _HALO
    in_specs = [pl.BlockSpec(r if isinstance(r, tuple) else (r, w), idx) for _, r, w, idx in ins]
    in_specs += [pl.BlockSpec((_HALO, a.shape[1]), (lambda i: (jnp.maximum(i * per - 1, 0), 0)) if side < 0
                              else (lambda i: (jnp.minimum((i + 1) * per, rows // _HALO - 1), 0))) for a, side in halos]
    in_specs += [pl.BlockSpec(c.shape, lambda i, nd=c.ndim: (0,) * nd) for c in consts]
    out_shape = [_SDS((rows // d, d * w), dt) for w, dt, d in outs] + [_SDS((8, w), F32) for w in sums]
    out_specs = [pl.BlockSpec((tile // d, d * w), lambda i: (i, 0)) for w, _, d in outs]
    out_specs += [pl.BlockSpec((8, w), lambda i: (0, 0)) for w in sums]
    est = (sum(_nbytes((tile if isinstance(r, tuple) else r, w), a.dtype) for a, r, w, _ in ins)
           + sum(_nbytes((tile, w), dt) for w, dt, _ in outs))
    shared, aliases = [], {}
    if into is not None:
        buf, total, j = into
        out_shape[0] = _SDS((rows, total), outs[0][1])
        out_specs[0] = pl.BlockSpec((tile, outs[0][0]), lambda i: (i, j))
        if buf is not None:
            shared, aliases = [buf], {n_in + n_h + n_c: 0}
    if after is not None:
        shared.append(after)
    in_specs += [pl.BlockSpec(memory_space=pl.ANY)] * len(shared)
    return pl.pallas_call(body, grid=(n_steps,), in_specs=in_specs, out_specs=out_specs, out_shape=out_shape, name=name,
                          input_output_aliases=aliases, compiler_params=_params(("arbitrary",), 3 * est))(
                              *[_hbm(a[0]) for a in ins], *[_hbm(a) for a, _ in halos], *consts, *shared)


def _perm(d, tile):
    p = np.zeros((tile, tile), np.float32)
    t = np.arange(tile)
    p[t, (t % d) * (tile // d) + t // d] = 1.0
    return jnp.asarray(p, _BF)


def _unstride(s, p):
    d = p.shape[0] // s.shape[0]
    w = s.shape[1] // d
    return _dot(p, jnp.concatenate([s[:, r * w:(r + 1) * w] for r in range(d)], axis=0))


def _stride(x, p, d):
    z = _dot(p, x, _TN)
    n = x.shape[0] // d
    return jnp.concatenate([z[r * n:(r + 1) * n] for r in range(d)], axis=1)


def _shifted(u, halo, back):
    n = u.shape[0] + _HALO
    if back:
        ext = jnp.concatenate([halo, u], axis=0)
        return [pltpu.roll(ext, j, 0)[_HALO:] for j in (1, 2, 3)]
    ext = jnp.concatenate([u, halo], axis=0)
    return [pltpu.roll(ext, n - j, 0)[:u.shape[0]] for j in (1, 2, 3)]


def _tile_for(width):
    return max(c for c in (256, 128, 64, 32) if c * width <= (1 << 18) or c == 32)


_NORM_TILE = 256


def _rstd(x):
    return lax.rsqrt(jnp.mean(x * x, axis=-1, keepdims=True) + _EPS)


def _split(x, groups):
    w = x.shape[-1] // groups
    return [x[:, g * w:(g + 1) * w] for g in range(groups)]


def _cat(parts):
    return parts[0] if len(parts) == 1 else jnp.concatenate(parts, axis=-1)


def _rms_bwd_tile(x, dy, g, groups):
    dxs, dgs = [], []
    for xs, ds, gs in zip(_split(x, groups), _split(dy.astype(F32), groups), _split(g, groups)):
        r = _rstd(xs)
        xh = xs * r
        gd = ds * gs
        dxs.append(r * (gd - xh * jnp.mean(gd * xh, axis=-1, keepdims=True)))
        dgs.append(ds * xh)
    return _cat(dxs), _cat(dgs)


def _rms_fwd(x, g, *, groups=1, into=None, name):
    def fn(x, g):
        return _cat([xs * _rstd(xs) * gs for xs, gs in zip(_split(x, groups), _split(g, groups))])
    w = x.shape[1]
    return _rows(fn, [x], [g.reshape(1, w)], [(w, _BF)], into=into, tile=_NORM_TILE, name=name)[0]


def _rms_bwd(x, dy, g, res, *, name):
    def fn(x, dy, res, g):
        dx, dg = _rms_bwd_tile(x, dy, g, 1)
        return dx + res, dx + res, dg
    w = x.shape[1]
    dx, dxb, dg = _rows(fn, [x, dy, res], [g.reshape(1, w)], [(w, F32), (w, _BF)], [w], tile=_NORM_TILE, name=name)
    return dx, dxb, dg.sum(axis=0)


def _pick(n, cands):
    for c in cands:
        if n % c == 0:
            return c
    raise ValueError(f"no block size for {n}")


_MM_BLOCKS = (1024, 1152, 512, 384)


def _mm(a, b, *, ta=False, tb=False, extra=(), epi=None, outs=(F32,), after=None, b_chips=0, out_chips=0, b_cols=None, name):
    m, k = (a.shape[1], a.shape[0]) if ta else a.shape
    b_shape = (b.shape[1], b.shape[2] * b_chips) if b_chips else b.shape
    if b_cols is not None:
        b_shape = (b.shape[0], b_cols[1])
    n = b_shape[0] if tb else b_shape[1]
    assert k == (b_shape[1] if tb else b_shape[0])
    n_cap = n // max(out_chips, 1 if tb else b_chips, 1)
    k_cap = k // (b_chips if (b_chips and tb) else 1)
    bm, bn = _pick(m, _MM_BLOCKS), _pick(n_cap, _MM_BLOCKS)
    bk = _pick(k_cap, (2048, 1920) + _MM_BLOCKS)
    nk = k // bk
    n_e, n_o = len(extra), len(outs)
    behind = [] if after is None else [after]
    dims = (((0 if ta else 1,), (1 if tb else 0,)), ((), ()))

    def body(a_ref, b_ref, *rest):
        ex, orefs, acc = rest[:n_e], rest[n_e + len(behind):n_e + len(behind) + n_o], rest[-1]
        kk = pl.program_id(2)

        @pl.when(kk == 0)
        def _():
            acc[...] = jnp.zeros_like(acc)

        acc[...] += _dot(a_ref[...], b_ref[...], dims)

        @pl.when(kk == nk - 1)
        def _():
            r = acc[...]
            res = epi(r, *[e[...] for e in ex]) if epi is not None else (r,)
            for o, v in zip(orefs, res):
                o[...] = v.astype(o.dtype)

    a_spec = pl.BlockSpec((bk, bm), lambda i, j, kk: (kk, i)) if ta else pl.BlockSpec((bm, bk), lambda i, j, kk: (i, kk))
    if b_chips and tb:
        per = k_cap // bk
        b_spec = pl.BlockSpec((None, bn, bk), lambda i, j, kk: (kk // per, j, kk % per))
    elif b_chips:
        per = n_cap // bn
        b_spec = pl.BlockSpec((None, bk, bn), lambda i, j, kk: (j // per, kk, j % per))
    else:
        first = 0 if b_cols is None else b_cols[0] // bn
        assert b_cols is None or (not tb and b_cols[0] % bn == 0)
        b_spec = pl.BlockSpec((bn, bk), lambda i, j, kk: (j, kk)) if tb else pl.BlockSpec((bk, bn), lambda i, j, kk: (kk, first + j))
    t_spec = pl.BlockSpec((bm, bn), lambda i, j, kk: (i, j))
    o_spec, o_shape = t_spec, (m, n)
    if out_chips:
        per_o = n_cap // bn
        o_spec, o_shape = pl.BlockSpec((None, bm, bn), lambda i, j, kk: (j // per_o, i, j % per_o)), (out_chips, m, n_cap)
    est = (_nbytes((bm, bk), a.dtype) + _nbytes((bk, bn), b.dtype) + sum(_nbytes((bm, bn), e.dtype) for e in extra)
           + sum(_nbytes((bm, bn), o) for o in outs)) * 2 + 2 * _nbytes((bm, bn), F32)
    res = pl.pallas_call(
        body, grid=(m // bm, n // bn, nk), in_specs=[a_spec, b_spec] + [t_spec] * n_e + [pl.BlockSpec(memory_space=pl.ANY)] * len(behind),
        out_specs=[o_spec] * n_o, out_shape=[_SDS(o_shape, o) for o in outs], scratch_shapes=[pltpu.VMEM((bm, bn), F32)], name=name,
        compiler_params=_params(("parallel", "parallel", "arbitrary"), est))(_hbm(a), _hbm(b), *[_hbm(e) for e in extra], *behind)
    return res[0] if n_o == 1 else res


def _w_in_groups(cch):
    return (0, 3 * _AW), (3 * _AW, _AW), (4 * _AW, cch), (4 * _AW + cch, _HEADS)


def _w_in_from_chips(blocks, cch, *, name):
    chips, d, _ = blocks.shape
    qkv, z, xbc, dt = _w_in_groups(cch)
    order = (qkv, xbc, z, dt)
    total = 3 * _AW + cch + _AW + _LANES
    tile = 2 * _ABLK

    def body(b_ref, o_ref):
        full = jnp.concatenate([b_ref[k] for k in range(chips)], axis=1)
        parts = [full[:, s:s + w] for s, w in order]
        o_ref[...] = jnp.concatenate(parts + [jnp.zeros((tile, total - sum(w for _, w in order)), full.dtype)], axis=1)

    return pl.pallas_call(
        body, grid=(d // tile,), in_specs=[pl.BlockSpec((chips, tile, blocks.shape[2]), lambda i: (0, i, 0))],
        out_specs=pl.BlockSpec((tile, total), lambda i: (i, 0)), out_shape=_SDS((d, total), blocks.dtype), name=name,
        compiler_params=_params(("arbitrary",), 16 << 20))(_hbm(blocks))


def _w_in_to_chips(g_all, cch, chips, *, name):
    d = g_all.shape[0]
    qkv, z, xbc, dt = _w_in_groups(cch)
    n = dt[0] + dt[1]
    z0 = 3 * _AW + cch
    tile = 2 * _ABLK

    def body(g_ref, o_ref):
        v = g_ref[...]
        full = jnp.concatenate([v[:, :3 * _AW], v[:, z0:z0 + _AW], v[:, 3 * _AW:z0], v[:, z0 + _AW:z0 + _AW + _HEADS]], axis=1)
        for k in range(chips):
            o_ref[k] = full[:, k * (n // chips):(k + 1) * (n // chips)]

    return pl.pallas_call(
        body, grid=(d // tile,), in_specs=[pl.BlockSpec((tile, g_all.shape[1]), lambda i: (i, 0))],
        out_specs=pl.BlockSpec((chips, tile, n // chips), lambda i: (0, i, 0)), out_shape=_SDS((chips, d, n // chips), g_all.dtype),
        name=name, compiler_params=_params(("arbitrary",), 16 << 20))(_hbm(g_all))


def _add_to(acc, r):
    return (acc + r,)


def _alibi_bias(dilation):
    slopes = 2.0 ** (-8.0 * (np.arange(_HEADS) + 1) / _HEADS)
    i = np.arange(_ABLK)[:, None]
    j = np.arange(_ABLK)[None, :]
    cur = np.where(i - j >= 0, -slopes[:, None, None] * ((i - j) * dilation), _NEG)
    prev = np.where(j >= i, -slopes[:, None, None] * ((i - j + _ABLK) * dilation), _NEG)
    both = np.stack([np.concatenate([np.full_like(prev, _NEG), cur], axis=2), np.concatenate([prev, cur], axis=2)])
    return jnp.asarray(both.reshape(2, _HEADS // 2, 2 * _ABLK, 2 * _ABLK), F32)


def _bias_spec():
    return pl.BlockSpec((None, _HEADS // 2, 2 * _ABLK, 2 * _ABLK), lambda r, j: (jnp.minimum(j, 1), 0, 0, 0))


def _strided(a, d):
    return a.reshape(a.shape[0] // d, d * a.shape[1])


def _pair(pr):
    return slice(pr * _LANES, (pr + 1) * _LANES)


def _low_lanes(shape):
    return lax.broadcasted_iota(jnp.int32, shape, 1) < _HDIM


def _halves(v, low):
    z = jnp.zeros_like(v)
    return jnp.where(low, v, z), jnp.where(low, z, v)


def _lane_spec(nb):
    return pl.BlockSpec((_ABLK, _LANES), lambda r, j: (jnp.minimum(j, nb - 1), r))


def _expand_heads(v):
    low = _low_lanes((v.shape[0], _LANES))
    return jnp.concatenate([jnp.where(low, v[:, 2 * pr:2 * pr + 1], v[:, 2 * pr + 1:2 * pr + 2]) for pr in range(_HEADS // 2)], axis=1)


def _attn_specs(nb, n_parts):
    def cur(p):
        return pl.BlockSpec((_ABLK, _AW), lambda r, j: (jnp.minimum(j, nb - 1), r * n_parts + p))

    def prev(p):
        return pl.BlockSpec((_ABLK, _AW), lambda r, j: (jnp.clip(j - 1, 0, nb - 1), r * n_parts + p))
    return cur, prev


def _attn_fwd(qkv, dilation, *, name):
    t = qkv.shape[0] * dilation
    nb = t // dilation // _ABLK
    bias = _alibi_bias(dilation)
    scale = _HDIM ** -0.5

    def body(q_ref, kc_ref, kp_ref, vc_ref, vp_ref, b_ref, o_ref, l_ref):
        low = _low_lanes((_ABLK, _LANES))
        l_ref[...] = jnp.zeros_like(l_ref)
        for pr in range(_HEADS // 2):
            sl = _pair(pr)
            k2 = jnp.concatenate([kp_ref[:, sl], kc_ref[:, sl]], axis=0)
            v2 = jnp.concatenate([vp_ref[:, sl], vc_ref[:, sl]], axis=0)
            q2 = jnp.concatenate(_halves(q_ref[:, sl] * scale, low), axis=0)
            s = _dot(q2, k2, _NT) + b_ref[pr]
            m = jnp.max(s, axis=-1, keepdims=True)
            p = jnp.exp(s - m)
            den = jnp.sum(p, axis=-1, keepdims=True)
            o = _dot(p, v2) / den
            lse = m + jnp.log(den)
            l_ref[:, 2 * pr:2 * pr + 1] = lse[:_ABLK]
            l_ref[:, 2 * pr + 1:2 * pr + 2] = lse[_ABLK:]
            o_ref[:, sl] = jnp.where(low, o[:_ABLK], o[_ABLK:]).astype(o_ref.dtype)

    cur, prev = _attn_specs(nb, 3)
    cur1, _ = _attn_specs(nb, 1)
    bspec = _bias_spec()
    sv = _hbm(qkv)
    o, l = pl.pallas_call(
        body, grid=(dilation, nb), in_specs=[cur(0), cur(1), prev(1), cur(2), prev(2), bspec],
        out_specs=[cur1(0), _lane_spec(nb)],
        out_shape=[_SDS((t // dilation, dilation * _AW), _BF), _SDS((t // dilation, dilation * _LANES), F32)], name=name,
        compiler_params=_params(("parallel", "arbitrary"), 16 << 20))(sv, sv, sv, sv, sv, bias)
    return o, l.reshape(t, _LANES)


def _attn_bwd(qkv, do, ld, dilation, *, name):
    t = qkv.shape[0] * dilation
    nb = t // dilation // _ABLK
    bias = _alibi_bias(dilation)
    scale = _HDIM ** -0.5

    def body(q_ref, kc_ref, kp_ref, vc_ref, vp_ref, do_ref, ld_ref, b_ref, dq_ref, dk_ref, dv_ref, ck, cv):
        n = pl.program_id(1)

        @pl.when(n == 0)
        def _():
            ck[...] = jnp.zeros_like(ck)
            cv[...] = jnp.zeros_like(cv)

        @pl.when(n < nb)
        def _():
            low = _low_lanes((_ABLK, _LANES))
            for pr in range(_HEADS // 2):
                sl = _pair(pr)
                k2 = jnp.concatenate([kp_ref[:, sl], kc_ref[:, sl]], axis=0)
                v2 = jnp.concatenate([vp_ref[:, sl], vc_ref[:, sl]], axis=0)
                q2 = jnp.concatenate(_halves(q_ref[:, sl] * scale, low), axis=0)
                do2 = jnp.concatenate(_halves(do_ref[:, sl], low), axis=0)
                lrow = jnp.concatenate([ld_ref[:, 2 * pr:2 * pr + 1], ld_ref[:, 2 * pr + 1:2 * pr + 2]], axis=0)
                dsum = jnp.concatenate([ld_ref[:, _HEADS + 2 * pr:_HEADS + 2 * pr + 1],
                                        ld_ref[:, _HEADS + 2 * pr + 1:_HEADS + 2 * pr + 2]], axis=0)
                p = jnp.exp(_dot(q2, k2, _NT) + b_ref[pr] - lrow)
                ds = (p * (_dot(do2, v2, _NT) - dsum)).astype(_BF)
                dq = _dot(ds, k2)
                dk2, dv2 = _dot(ds, q2, _TN), _dot(p, do2, _TN)
                dq_ref[:, sl] = (jnp.where(low, dq[:_ABLK], dq[_ABLK:]) * scale).astype(dq_ref.dtype)
                dk_ref[:, sl] = (ck[:, sl] + dk2[:_ABLK]).astype(dk_ref.dtype)
                dv_ref[:, sl] = (cv[:, sl] + dv2[:_ABLK]).astype(dv_ref.dtype)
                ck[:, sl] = dk2[_ABLK:]
                cv[:, sl] = dv2[_ABLK:]

        @pl.when(n == nb)
        def _():
            dk_ref[...] = ck[...].astype(dk_ref.dtype)
            dv_ref[...] = cv[...].astype(dv_ref.dtype)

    cur, prev = _attn_specs(nb, 3)
    cur1, prev1 = _attn_specs(nb, 1)
    bspec = _bias_spec()
    sv, dov, ldv = _hbm(qkv), _hbm(do), _hbm(_strided(ld, dilation))
    dqkv = pl.pallas_call(
        body, grid=(dilation, nb + 1),
        in_specs=[cur(0), cur(1), prev(1), cur(2), prev(2), cur1(0), _lane_spec(nb), bspec],
        out_specs=[cur1(0), prev1(0), prev1(0)], out_shape=[_SDS(dov.shape, _BF)] * 3, name=name,
        scratch_shapes=[pltpu.VMEM((_ABLK, _AW), F32)] * 2,
        compiler_params=_params(("parallel", "arbitrary"), 16 << 20))(sv, sv, sv, sv, sv, dov, ldv, bias)
    return dqkv


def _ssd_in_specs(ch):
    return dict(
        xs=pl.BlockSpec((_CHUNK, _AW), lambda c: (ch(c), 0)),
        bc=pl.BlockSpec((_CHUNK, 2 * _GROUPS * _NSTATE), lambda c: (ch(c), _AW // (2 * _GROUPS * _NSTATE))),
        lane=pl.BlockSpec((_CHUNK, _LANES), lambda c: (ch(c), 0)),
        arow=pl.BlockSpec((_HEADS, 1, _CHUNK), lambda c: (0, 0, ch(c))),
        st=pl.BlockSpec((1, _HEADS // 2, _NSTATE, _LANES), lambda c: (ch(c), 0, 0, 0)),
    )


def _decay(a_col, a_row):
    i0 = lax.broadcasted_iota(jnp.int32, (_CHUNK, _CHUNK), 0)
    i1 = lax.broadcasted_iota(jnp.int32, (_CHUNK, _CHUNK), 1)
    return jnp.where(i0 >= i1, jnp.exp(a_col - a_row), 0.0), jnp.where(i1 >= i0, jnp.exp(a_row - a_col), 0.0)


def _rsum(v):
    return jnp.sum(v, axis=-1, keepdims=True)


def _ssd_fwd(act, dt, acum, a_row, *, name):
    t = act.shape[0]
    nc = t // _CHUNK
    sp = _ssd_in_specs(lambda c: c)
    gw = _GROUPS * _NSTATE

    def body(xs_ref, bc_ref, dt_ref, ac_ref, ar_ref, y_ref, sall_ref, st):
        @pl.when(pl.program_id(0) == 0)
        def _():
            st[...] = jnp.zeros_like(st)

        low = _low_lanes((_CHUNK, _LANES))
        for g in range(_GROUPS):
            bg = bc_ref[:, g * _NSTATE:(g + 1) * _NSTATE]
            cg = bc_ref[:, gw + g * _NSTATE:gw + (g + 1) * _NSTATE].astype(_BF)
            cb = _dot(cg, bg, _NT)
            for pr in range(g * _HPG // 2, (g + 1) * _HPG // 2):
                ha, hb = 2 * pr, 2 * pr + 1
                a_a, a_b = ac_ref[:, ha:ha + 1], ac_ref[:, hb:hb + 1]
                x = (xs_ref[:, _pair(pr)] * jnp.where(low, dt_ref[:, ha:ha + 1], dt_ref[:, hb:hb + 1])).astype(_BF)
                lm_a, _ = _decay(a_a, ar_ref[ha])
                lm_b, _ = _decay(a_b, ar_ref[hb])
                sv = st[pr]
                sall_ref[0, pr] = sv
                yd = _dot(jnp.concatenate([cb * lm_a, cb * lm_b], axis=0), x)
                yd = jnp.where(low, yd[:_CHUNK], yd[_CHUNK:])
                y_ref[:, _pair(pr)] = yd + jnp.where(low, jnp.exp(a_a), jnp.exp(a_b)) * _dot(cg, sv)
                al_a, al_b = jnp.min(a_a, axis=0, keepdims=True), jnp.min(a_b, axis=0, keepdims=True)
                upd = _dot(jnp.concatenate([bg * jnp.exp(al_a - a_a), bg * jnp.exp(al_b - a_b)], axis=1), x, _TN)
                st[pr] = jnp.where(low, jnp.exp(al_a), jnp.exp(al_b)) * sv + jnp.where(low, upd[:_NSTATE], upd[_NSTATE:])

    return pl.pallas_call(
        body, grid=(nc,), in_specs=[sp['xs'], sp['bc'], sp['lane'], sp['lane'], sp['arow']],
        out_specs=[sp['xs'], sp['st']], out_shape=[_SDS((t, _AW), F32), _SDS((nc, _HEADS // 2, _NSTATE, _LANES), F32)],
        scratch_shapes=[pltpu.VMEM((_HEADS // 2, _NSTATE, _LANES), F32)], name=name,
        compiler_params=_params(("arbitrary",), 16 << 20))(*[_hbm(a) for a in (act, act, dt, acum, a_row)])


def _ssd_bwd(act, dt, acum, a_row, sall, dy, *, name):
    t = act.shape[0]
    nc = t // _CHUNK
    sp = _ssd_in_specs(lambda c: nc - 1 - c)
    gw = _GROUPS * _NSTATE

    def body(xs_ref, bc_ref, dt_ref, ac_ref, ar_ref, sall_ref, dy_ref, dxs_ref, dbc_ref, ddt_ref, da_ref, dst):
        @pl.when(pl.program_id(0) == 0)
        def _():
            dst[...] = jnp.zeros_like(dst)

        ddt_ref[...] = jnp.zeros_like(ddt_ref)
        da_ref[...] = jnp.zeros_like(da_ref)
        row = lax.broadcasted_iota(jnp.int32, (_CHUNK, 1), 0)
        low = _low_lanes((_CHUNK, _LANES))
        for g in range(_GROUPS):
            bg = bc_ref[:, g * _NSTATE:(g + 1) * _NSTATE]
            bgb = bg.astype(_BF)
            cg = bc_ref[:, gw + g * _NSTATE:gw + (g + 1) * _NSTATE].astype(_BF)
            cb, cbt = _dot(cg, bgb, _NT), _dot(bgb, cg, _NT)
            dcb = jnp.zeros((_CHUNK, _CHUNK), F32)
            dbg = jnp.zeros((_CHUNK, _NSTATE), F32)
            dcg = jnp.zeros((_CHUNK, _NSTATE), F32)
            for pr in range(g * _HPG // 2, (g + 1) * _HPG // 2):
                heads = (2 * pr, 2 * pr + 1)
                a_cols = [ac_ref[:, h:h + 1] for h in heads]
                dt_pair = jnp.where(low, dt_ref[:, heads[0]:heads[0] + 1], dt_ref[:, heads[1]:heads[1] + 1])
                xsv = xs_ref[:, _pair(pr)]
                x = xsv * dt_pair
                xb = x.astype(_BF)
                xhs = _halves(xb, low)
                dyv = dy_ref[:, _pair(pr)]
                dyb = dyv.astype(_BF)
                dyhs = _halves(dyb, low)
                sv, dsv = sall_ref[0, pr], dst[pr]
                svb, dsb = sv.astype(_BF), dsv.astype(_BF)
                a_lasts = [jnp.min(a, axis=0, keepdims=True) for a in a_cols]
                e_pair = jnp.where(low, jnp.exp(a_cols[0]), jnp.exp(a_cols[1]))
                el_pair = jnp.where(low, jnp.exp(a_lasts[0]), jnp.exp(a_lasts[1]))
                yo = e_pair * _dot(cg, svb)
                decays = [_decay(a_cols[i], ar_ref[h]) for i, h in enumerate(heads)]
                gms, gmts = [cb * lm for lm, _ in decays], [cbt * lmt for _, lmt in decays]
                w_cols = [jnp.exp(a_lasts[i] - a_cols[i]) for i in range(2)]
                x2, dy2 = jnp.concatenate(xhs, axis=0), jnp.concatenate(dyhs, axis=0)
                bwd = _dot(jnp.concatenate([bg * w_cols[0], bg * w_cols[1]], axis=0), dsb)
                dxg = _dot(jnp.concatenate(gms, axis=1), dyb, _TN)
                dg2, dgt2, xds2 = _dot(dy2, xb, _NT), _dot(x2, dyb, _NT), _dot(x2, dsb, _NT)
                das = []
                for i in range(2):
                    rows_i = slice(i * _CHUNK, (i + 1) * _CHUNK)
                    dcb = dcb + dg2[rows_i] * decays[i][0]
                    dbg = dbg + w_cols[i] * xds2[rows_i]
                    das.append(_rsum(dg2[rows_i] * gms[i]) - _rsum(dgt2[rows_i] * gmts[i]))
                bwd = jnp.where(low, bwd[:_CHUNK], bwd[_CHUNK:])
                dx = jnp.where(low, dxg[:_CHUNK], dxg[_CHUNK:]) + bwd
                edy = (e_pair * dyv).astype(_BF)
                dcg = dcg + _dot(edy, svb, _NT)
                zs, yos, sds, dts = (_halves(v, low) for v in (x * bwd, dyv * yo, sv * dsv, dx * xsv))
                for i, h in enumerate(heads):
                    z = _rsum(zs[i])
                    da_last = jnp.sum(z, axis=0, keepdims=True) + jnp.exp(a_lasts[i]) * jnp.sum(_rsum(sds[i]), axis=0, keepdims=True)
                    da_ref[:, h:h + 1] = das[i] + _rsum(yos[i]) - z + jnp.where(row == _CHUNK - 1, da_last, 0.0)
                    ddt_ref[:, h:h + 1] = _rsum(dts[i])
                dxs_ref[:, _pair(pr)] = dx * dt_pair
                dst[pr] = el_pair * dsv + _dot(cg, edy, _TN)
            dbc_ref[:, g * _NSTATE:(g + 1) * _NSTATE] = dbg + _dot(dcb, cg, _TN)
            dbc_ref[:, gw + g * _NSTATE:gw + (g + 1) * _NSTATE] = dcg + _dot(dcb, bgb)

    ch = lambda c: nc - 1 - c
    wide = pl.BlockSpec((_CHUNK, 2 * gw), lambda c: (ch(c), 0))
    return pl.pallas_call(
        body, grid=(nc,), in_specs=[sp['xs'], sp['bc'], sp['lane'], sp['lane'], sp['arow'], sp['st'], sp['xs']],
        out_specs=[sp['xs'], wide, sp['lane'], sp['lane']],
        out_shape=[_SDS((t, _AW), F32), _SDS((t, 2 * gw), F32), _SDS((t, _LANES), F32), _SDS((t, _LANES), F32)],
        scratch_shapes=[pltpu.VMEM((_HEADS // 2, _NSTATE, _LANES), F32)], name=name,
        compiler_params=_params(("arbitrary",), 16 << 20))(*[_hbm(a) for a in (act, act, dt, acum, a_row, sall, dy)])


def _scan_rows(v, reverse):
    r = lax.broadcasted_iota(jnp.int32, v.shape, 0)
    for s in (1, 2, 4, 8, 16, 32, 64):
        if reverse:
            v = v + jnp.where(r < _CHUNK - s, pltpu.roll(v, _CHUNK - s, 0), 0.0)
        else:
            v = v + jnp.where(r >= s, pltpu.roll(v, s, 0), 0.0)
    return v


def _softplus(x):
    return jnp.maximum(x, 0.0) + jnp.log(1.0 + jnp.exp(-jnp.abs(x)))


def _sigmoid(x):
    return 1.0 / (1.0 + jnp.exp(-x))


def _silu(x):
    return x * _sigmoid(x)


def _dsilu(x):
    s = _sigmoid(x)
    return s * (1.0 + x * (1.0 - s))


def _lanes(a):
    return jnp.pad(a, (0, _LANES - a.shape[0])).reshape(1, _LANES)


def _layer_fwd(x, p, l):
    cch = p['conv_w'].shape[1]
    sv = {}
    h1 = _rms_fwd(x, p['ln1_g'], name=f"ln1_fwd_{l}")
    qkv = _mm(h1, p['w_in'], b_cols=(0, 3 * _AW), outs=(_BF,), name=f"in_proj_qkv_{l}")
    xbc = _mm(h1, p['w_in'], b_cols=(3 * _AW, cch), name=f"in_proj_xbc_{l}")
    zdt = _mm(h1, p['w_in'], b_cols=(3 * _AW + cch, _AW + _LANES), name=f"in_proj_zdt_{l}")
    z, dt_raw = (zdt, _AW, 0), (zdt, _LANES, _AW // _LANES)

    tile = 2 * _ABLK
    perms = [_perm(d, tile) for d in _DILATIONS[1:]]
    views = [qkv] + list(_rows(lambda a, p2, p3: (_stride(a, p2, _DILATIONS[1]), _stride(a, p3, _DILATIONS[2])), [qkv], perms,
                               [(3 * _AW, _BF, d) for d in _DILATIONS[1:]], tile=tile, name=f"qkv_strided_{l}"))
    outs = []
    for dil, view in zip(_DILATIONS, views):
        outs += _attn_fwd(view, dil, name=f"attn_fwd_d{dil}_{l}")

    def combine(o1, l1, o2, l2, o3, l3, p2, p3):
        m = jnp.maximum(jnp.maximum(l1, l2), l3)
        e1, e2, e3 = jnp.exp(l1 - m), jnp.exp(l2 - m), jnp.exp(l3 - m)
        tot = e1 + e2 + e3
        mixed = sum(_expand_heads(e / tot) * o for e, o in ((e1, o1.astype(F32)), (e2, _unstride(o2, p2)), (e3, _unstride(o3, p3))))
        return mixed, m + jnp.log(tot)
    outs = [a if i % 2 or i == 0 else ("strided", a, _DILATIONS[i // 2]) for i, a in enumerate(outs)]
    attn, lse = _rows(combine, outs, perms, [(_AW, F32), (_LANES, F32)], tile=tile, name=f"attn_combine_{l}")
    mix = _rms_fwd(attn, p['attn_norm_g'], into=(None, 2 * _AW, 0), name=f"attn_norm_fwd_{l}")

    def conv(u0, before, w, b):
        u1, u2, u3 = _shifted(u0, before, True)
        return _silu(w[0:1] * u3 + w[1:2] * u2 + w[2:3] * u1 + w[3:4] * u0 + b)
    act = _rows(conv, [xbc], [p['conv_w'], p['conv_b'].reshape(1, cch)], [(cch, F32)], halos=[(xbc, -1)], tile=_tile_for(cch),
                name=f"conv_fwd_{l}")[0]

    def dtf(raw, bias, alog):
        dt = _softplus(raw + bias)
        return dt, _scan_rows(dt * -jnp.exp(alog), False)
    dt, acum = _rows(dtf, [dt_raw], [_lanes(p['dt_bias']), _lanes(p['a_log'])], [(_LANES, F32), (_LANES, F32)],
                     tile=_CHUNK, name=f"dt_fwd_{l}")
    a_row = acum[:, :_HEADS].T[:, None, :]
    y_ssd, sall = _ssd_fwd(act, dt, acum, a_row, name=f"ssd_fwd_{l}")
    dskip = jnp.repeat(p['d_skip'], _HDIM).reshape(1, _AW)
    xs = (act, _AW, 0)

    def gate(y, xs, z, dsk):
        return (y + dsk * xs) * _silu(z)
    y2 = _rows(gate, [y_ssd, xs, z], [dskip], [(_AW, F32)], tile=_tile_for(_AW), name=f"gate_fwd_{l}")[0]
    mix = _rms_fwd(y2, p['ssd_norm_g'], groups=_GROUPS, into=(mix, 2 * _AW, 1), name=f"ssd_norm_fwd_{l}")
    sv.update(x=x, h1=h1, qkv=views, zdt=zdt, xbc=xbc, attn=attn, lse=lse, act=act, dt=dt, acum=acum, a_row=a_row,
              sall=sall, y_ssd=y_ssd, dskip=dskip, y2=y2, mix=mix)
    return mix, sv


def _layer_fwd_mlp(p, sv, l):
    x2 = _mm(sv['mix'], p['w_out'], extra=(sv['x'],), epi=_add_to, name=f"out_proj_{l}")
    h2 = _rms_fwd(x2, p['ln2_g'], name=f"ln2_fwd_{l}")
    a = _mm(h2, p['w_mlp_in'], b_chips=_CHIPS, epi=lambda acc: (jnp.square(jnp.maximum(acc, 0.0)),), outs=(_BF,), name=f"mlp_in_{l}")
    x3 = _mm(a, p['w_mlp_out'], extra=(x2,), epi=_add_to, name=f"mlp_out_{l}")
    sv.update(x2=x2, h2=h2, a=a)
    return x3


def _layer_bwd(dx3, dx3b, p, sv, l, send, after):
    cch = p['conv_w'].shape[1]
    g = {}
    du = _mm(dx3b, p['w_mlp_out'], tb=True, extra=(sv['a'],), outs=(_BF,), after=after,
             epi=lambda acc, a: (acc * 2.0 * jnp.sqrt(a.astype(F32)),), name=f"mlp_out_dx_{l}")
    g['w_mlp_out'] = _mm(sv['a'], dx3b, ta=True, outs=(_BF,), name=f"mlp_out_dw_{l}")
    g['w_mlp_in'] = _mm(sv['h2'], du, ta=True, out_chips=_CHIPS, outs=(_BF,), name=f"mlp_in_dw_{l}")
    sent = send(('w_mlp_out', 'w_mlp_in'), g)
    dh2 = _mm(du, p['w_mlp_in'], tb=True, b_chips=_CHIPS, after=sent, name=f"mlp_in_dx_{l}")
    dx2, dx2b, g['ln2_g'] = _rms_bwd(sv['x2'], dh2, p['ln2_g'], dx3, name=f"ln2_bwd_{l}")
    dmix = _mm(dx2b, p['w_out'], tb=True, name=f"out_proj_dx_{l}")
    g['w_out'] = _mm(sv['mix'], dx2b, ta=True, outs=(_BF,), name=f"out_proj_dw_{l}")
    after_out = send(('w_out',), g)

    tile = 2 * _ABLK
    perms = [_perm(d, tile) for d in _DILATIONS[1:]]

    def norm_bwd(attn, dy, lse, gn, p2, p3):
        dattn, dgn = _rms_bwd_tile(attn, dy, gn, 1)
        prod, low = dattn * attn, _low_lanes((attn.shape[0], _LANES))
        lane = lax.broadcasted_iota(jnp.int32, lse.shape, 1)
        ld = jnp.where(lane < _HEADS, lse, 0.0)
        for pr in range(_HEADS // 2):
            for i, part in enumerate(_halves(prod[:, _pair(pr)], low)):
                ld = jnp.where(lane == _HEADS + 2 * pr + i, _rsum(part), ld)
        return dattn, _stride(dattn, p2, _DILATIONS[1]), _stride(dattn, p3, _DILATIONS[2]), ld, dgn
    *dos, ld, gn_sum = _rows(norm_bwd, [sv['attn'], (dmix, _AW, 0), sv['lse']], [p['attn_norm_g'].reshape(1, _AW)] + perms,
                             [(_AW, _BF)] + [(_AW, _BF, d) for d in _DILATIONS[1:]] + [(_LANES, F32)], [_AW], after=after_out,
                             tile=tile, name=f"attn_norm_bwd_{l}")
    g['attn_norm_g'] = gn_sum.sum(axis=0)
    parts = [_attn_bwd(view, do, ld, dil, name=f"attn_bwd_d{dil}_{l}") for view, do, dil in zip(sv['qkv'], dos, _DILATIONS)]

    def branch_sum(*t):
        parts_, (p2, p3) = t[:9], t[9:]
        t = [a.astype(F32) for a in parts_[:3]] + [_unstride(a, p2) for a in parts_[3:6]] + [_unstride(a, p3) for a in parts_[6:]]
        return jnp.concatenate([t[i] + t[3 + i] + t[6 + i] for i in range(3)], axis=1)
    branch_ins = list(parts[0]) + [("strided", a, d) for pr, d in zip(parts[1:], _DILATIONS[1:]) for a in pr]
    w_all = 3 * _AW + cch + _AW + _LANES
    dproj = _rows(branch_sum, branch_ins, perms, [(3 * _AW, _BF)], into=(None, w_all, 0), tile=tile, name=f"attn_bwd_sum_{l}")[0]

    xs, z, dt_raw = (sv['act'], _AW, 0), (sv['zdt'], _AW, 0), (sv['zdt'], _LANES, _AW // _LANES)

    def gate_bwd(y2, dy, y, xs, z, dsk, gn):
        dy2, dgn = _rms_bwd_tile(y2, dy, gn, _GROUPS)
        dy1 = dy2 * _silu(z)
        return dy1, dsk * dy1, dy2 * (y + dsk * xs) * _dsilu(z), dy1 * xs, dgn
    dy1, dxs_skip, dz, dsk_sum, gn_sum = _rows(
        gate_bwd, [sv['y2'], (dmix, _AW, 1), sv['y_ssd'], xs, z], [sv['dskip'], p['ssd_norm_g'].reshape(1, _AW)],
        [(_AW, F32), (_AW, F32), (_AW, _BF)], [_AW, _AW], tile=_NORM_TILE, name=f"gate_bwd_{l}")
    g['ssd_norm_g'] = gn_sum.sum(axis=0)
    g['d_skip'] = dsk_sum.sum(axis=0).reshape(_HEADS, _HDIM).sum(axis=1)
    dxs, dbc, ddt, da = _ssd_bwd(sv['act'], sv['dt'], sv['acum'], sv['a_row'], sv['sall'], dy1, name=f"ssd_bwd_{l}")

    def dtb(da, ddtx, raw, dt, dz, bias, alog):
        a = -jnp.exp(alog)
        dda = _scan_rows(da, True)
        draw = (dda * a + ddtx) * _sigmoid(raw + bias)
        return jnp.concatenate([dz, draw.astype(dz.dtype)], axis=1), draw, dda * dt * a
    dproj, dbias, dalog = _rows(dtb, [da, ddt, dt_raw, sv['dt'], dz], [_lanes(p['dt_bias']), _lanes(p['a_log'])],
                                [(_AW + _LANES, _BF)], [_LANES, _LANES], into=(dproj, w_all, (3 * _AW + cch) // (_AW + _LANES)),
                                tile=_CHUNK, name=f"dt_bwd_{l}")
    g['dt_bias'], g['a_log'] = dbias.sum(axis=0)[:_HEADS], dalog.sum(axis=0)[:_HEADS]
    def conv_bwd1(u0, dxs, dbc, dxk, before, w, b):
        u1, u2, u3 = _shifted(u0, before, True)
        pre = w[0:1] * u3 + w[1:2] * u2 + w[2:3] * u1 + w[3:4] * u0 + b
        dp = jnp.concatenate([dxs + dxk, dbc], axis=1) * _dsilu(pre)
        return dp, dp * u3, dp * u2, dp * u1, dp * u0, dp
    dpre, *dws = _rows(conv_bwd1, [sv['xbc'], dxs, dbc, dxs_skip], [p['conv_w'], p['conv_b'].reshape(1, cch)], [(cch, F32)],
                       [cch] * 5, halos=[(sv['xbc'], -1)], tile=_NORM_TILE, name=f"conv_bwd_pre_{l}")
    g['conv_w'] = jnp.stack([dws[i].sum(axis=0) for i in range(_CONV_K)])
    g['conv_b'] = dws[4].sum(axis=0)

    def conv_bwd2(p0, after_, w):
        p1, p2, p3 = _shifted(p0, after_, False)
        return w[3:4] * p0 + w[2:3] * p1 + w[1:2] * p2 + w[0:1] * p3
    dproj = _rows(conv_bwd2, [dpre], [p['conv_w']], [(cch, _BF)], halos=[(dpre, 1)], into=(dproj, w_all, 3 * _AW // cch),
                  tile=_tile_for(cch), name=f"conv_bwd_in_{l}")[0]
    g_all = _mm(sv['h1'], dproj, ta=True, outs=(_BF,), name=f"in_proj_dw_{l}")
    g['w_in'] = _w_in_to_chips(g_all, cch, _CHIPS, name=f"w_in_by_chip_{l}")
    sent = send(('w_in',), g)
    for n in _BIG:
        del g[n]
    dh1 = _mm(dproj, p['w_in'], tb=True, after=sent, name=f"in_proj_dx_{l}")
    dx, dxb, g['ln1_g'] = _rms_bwd(sv['x'], dh1, p['ln1_g'], dx2, name=f"ln1_bwd_{l}")
    return dx, dxb, g


def _loss_bwd(x, g, tgt):
    w = x.shape[1]
    tile = _NORM_TILE

    def fn(x, tgt, g):
        r = _rstd(x)
        xh = x * r
        e = xh * g - tgt
        gd = e * (g / w)
        dx = r * (gd - xh * jnp.mean(gd * xh, axis=-1, keepdims=True))
        rowloss = 0.5 * jnp.mean(e * e, axis=-1, keepdims=True)
        return dx, dx, (e / w) * xh, jnp.broadcast_to(rowloss, (tile, _LANES))
    dx, dxb, dg, ls = _rows(fn, [x, tgt], [g.reshape(1, w)], [(w, F32), (w, _BF)], [w, _LANES], tile=tile, name="loss_head")
    return dx, dxb, dg.sum(axis=0), ls[:, 0].sum()


def _adamw_math(w, g, m, v):
    m2 = _B1 * m + (1.0 - _B1) * g
    v2 = _B2 * v + (1.0 - _B2) * jnp.square(g)
    m_hat = m2 / (1.0 - _B1 ** _STEP)
    v_hat = v2 / (1.0 - _B2 ** _STEP)
    return -_LR * (m_hat / (jnp.sqrt(v_hat) + _AEPS) + _WD * w), m2, v2


def _adamw(w, g, m, v, *, name):
    width = w.shape[-1]
    flat = [a.reshape(-1, width) for a in (w, g, m, v)]
    tile = _pick(flat[0].shape[0], (_tile_for(width), 32, 8))
    res = _rows(_adamw_math, flat, [], [(width, F32)] * 3, tile=tile, name=name)
    return [r.reshape(w.shape) for r in res]


_HBM = pl.BlockSpec(memory_space=pltpu.HBM)


def _place():
    x, y, c = lax.axis_index("x"), lax.axis_index("y"), lax.axis_index("c")
    other_chips = [(1 - x, y), (x, 1 - y), (1 - x, 1 - y)]
    return x, y, c, other_chips


def _remote(src, dst, sems, i, dev):
    return pltpu.make_async_remote_copy(src_ref=src, dst_ref=dst, send_sem=sems[0].at[i], recv_sem=sems[1].at[i],
                                        device_id=dev, device_id_type=_MESH)


def _exchange8(v, *, reduce, after=None, name):
    r, w = v.shape
    behind = [] if after is None else [after]

    def body(v_ref, *rest):
        all_ref, rest = rest[len(behind)], rest[len(behind) + 1:]
        sems = rest[-2:]
        x, y, c, _ = _place()
        me = 4 * x + 2 * y + c
        all_ref[me] = v_ref[...]
        flips = [((d >> 2) & 1, (d >> 1) & 1, d & 1) for d in range(1, 8)]
        sends = [_remote(v_ref, all_ref.at[me], sems, i, (x ^ fx, y ^ fy, c ^ fc)) for i, (fx, fy, fc) in enumerate(flips)]
        for cp in sends:
            cp.start()
        for i, (fx, fy, fc) in enumerate(flips):
            _remote(v_ref, all_ref.at[me ^ (4 * fx + 2 * fy + fc)], sems, i, (x ^ fx, y ^ fy, c ^ fc)).wait_recv()
        for cp in sends:
            cp.wait_send()
        if reduce:
            acc = all_ref[0]
            for s in range(1, 8):
                acc = acc + all_ref[s]
            rest[0][...] = acc

    vm = pl.BlockSpec(memory_space=pltpu.VMEM)
    out_shape = [_SDS((8, r, w), v.dtype)] + ([_SDS((r, w), v.dtype)] if reduce else [])
    res = pl.pallas_call(body, in_specs=[vm] + [_ANY] * len(behind), out_specs=[vm] * len(out_shape), out_shape=out_shape, name=name,
                         scratch_shapes=[pltpu.SemaphoreType.DMA((7,)), pltpu.SemaphoreType.DMA((7,))],
                         compiler_params=pltpu.CompilerParams(vmem_limit_bytes=int(32 << 20)))(v, *behind)
    return res[1] if reduce else res[0]


_SEM = pl.BlockSpec(memory_space=pltpu.SEMAPHORE)
_ANY = pl.BlockSpec(memory_space=pl.ANY)
_EFFECT = pltpu.SideEffectType.DATAFLOW_SIDE_EFFECTING


def _send_start(name, srcs, land_shapes, plan, n_sends, after):
    ns, nl = len(srcs), len(land_shapes)
    zones = [_hbm(lax.empty(s.shape, s.dtype)) if isinstance(s, _SDS) else s for s in land_shapes]

    def body(*refs):
        ins, lands, sems = refs[:ns], refs[ns:ns + nl], refs[ns + nl + 1:ns + nl + 3]
        x, y, c, chips = _place()
        for i, (s, d, dev) in enumerate(plan(x, y, c, chips, ins, lands)[0]):
            _remote(s, d, sems, i, dev).start()
        refs[-1][...] = jnp.zeros_like(refs[-1])

    sem = pltpu.SemaphoreType.DMA((n_sends,))
    res = pl.pallas_call(
        body, name=name, in_specs=[_HBM] * (ns + nl) + [_ANY],
        out_shape=(sem, sem, *[pltpu.HBM(s.shape, s.dtype) for s in land_shapes], _SDS((8, _LANES), F32)),
        out_specs=(_SEM, _SEM, *[_HBM] * nl, pl.BlockSpec(memory_space=pltpu.VMEM)),
        input_output_aliases={ns + i: 2 + i for i in range(nl)},
        compiler_params=pltpu.CompilerParams(has_side_effects=_EFFECT))(
            *[_hbm(s) for s in srcs], *zones, after)
    return dict(sems=res[:2], srcs=srcs, lands=res[2:2 + nl], plan=plan), res[-1]


def _send_wait(name, h, after):
    ns, nl = len(h['srcs']), len(h['lands'])

    def body(*refs):
        ins, lands, sems = refs[:ns], refs[ns:ns + nl], refs[ns + nl:ns + nl + 2]
        x, y, c, chips = _place()
        sends, landings = h['plan'](x, y, c, chips, ins, lands)
        for i, (s, d, dev) in enumerate(sends):
            _remote(s, d, sems, i, dev).wait_send()
        for i, d in enumerate(landings):
            _remote(d, d, sems, i, sends[i][2]).wait_recv()

    return pl.pallas_call(
        body, name=name, in_specs=[_HBM] * (ns + nl) + [_SEM, _SEM, _ANY],
        out_shape=tuple(pltpu.HBM(a.shape, a.dtype) for a in h['lands']), out_specs=tuple([_HBM] * nl),
        input_output_aliases={ns + i: i for i in range(nl)},
        compiler_params=pltpu.CompilerParams(has_side_effects=_EFFECT))(
            *[_hbm(s) for s in h['srcs']], *h['lands'], *h['sems'], after)


def _gather_plan(items):
    def plan(x, y, c, chips, ins, lands):
        k = 2 * x + y
        to = [(px, py, c) for px, py in chips] + [(x, y, 1 - c)]
        sends = [(ins[si].at[l], lands[t].at[k], dev) for t, (si, l) in enumerate(items) for dev in to]
        return sends, [lands[t].at[2 * px + py] for t in range(len(items)) for px, py in chips + [(x, y)]]
    return plan


_FLIPS = [((d >> 2) & 1, (d >> 1) & 1, d & 1) for d in range(1, 8)]


def _reduce_plan(halves):
    def plan(x, y, c, chips, ins, lands):
        sends, landings = [], []
        for t, hf in enumerate(halves):
            for i, (fx, fy, fc) in enumerate(_FLIPS):
                px, py, pc = x ^ fx, y ^ fy, c ^ fc
                sends.append((ins[t].at[2 * px + py, pl.ds(pc * hf, hf)], lands[t].at[i], (px, py, pc)))
                landings.append(lands[t].at[i])
        return sends, landings
    return plan


def _swap(name, srcs, out_shapes, plan, n_sends):
    n = len(srcs)

    def body(*refs):
        ins, outs, sems = refs[:n], refs[n:n + len(out_shapes)], refs[-2:]
        x, y, c, chips = _place()
        sends, landings = plan(x, y, c, chips, ins, outs)
        out = [_remote(s, d, sems, i, dev) for i, (s, d, dev) in enumerate(sends)]
        for cp in out:
            cp.start()
        for i, d in enumerate(landings):
            _remote(d, d, sems, i, sends[i][2]).wait_recv()
        for cp in out:
            cp.wait_send()

    return pl.pallas_call(
        body, in_specs=[_HBM] * n, out_specs=[_HBM] * len(out_shapes), out_shape=out_shapes, name=name,
        scratch_shapes=[pltpu.SemaphoreType.DMA((n_sends,)), pltpu.SemaphoreType.DMA((n_sends,))])(*srcs)


def _sum_owned(grads, landed, c, k, names):
    def sum8(*parts):
        acc = parts[0].astype(F32)
        for p in parts[1:]:
            acc = acc + p.astype(F32)
        return acc
    outs = []
    for g, got, name in zip(grads, landed, names):
        hf, b = got.shape[1:]
        own = lax.dynamic_slice_in_dim(lax.dynamic_index_in_dim(g, k, axis=0, keepdims=False), c * hf, hf, axis=0)
        outs.append(_rows(sum8, [own] + [("slot", got, i) for i in range(len(_FLIPS))], [], [(b, F32)],
                          tile=_pick(hf, (_tile_for(b), 32)), name=f"grad_sum_{name}")[0])
    return outs


def _share_halves(mine, *, name):
    n = len(mine)

    def plan(x, y, c_, chips, ins, outs):
        return [(ins[t], outs[t], (x, y, 1 - c_)) for t in range(n)], [outs[t] for t in range(n)]
    return _swap(name, mine, [_SDS(h.shape, F32) for h in mine], plan, n)


def _adamw_owned(w, mine, theirs, m, v, c, *, name):
    depth, a, b = w.shape
    half = a // 2
    tile = _pick(half, (_tile_for(b), 32, 8))
    nh = half // tile

    def blocks_of(l):
        return lambda i: (jnp.clip(i - 2 * nh * l, 0, 2 * nh - 1) % nh, 0)

    def fn(w, m, v, *rest):
        halves, cflag = rest[:-1], rest[-1]
        step = pl.program_id(0)
        is_mine = cflag[0:1, 0:1] == ((step // nh) % 2).astype(F32)
        g = jnp.where(is_mine, halves[0], halves[1])
        for l in range(1, depth):
            g = jnp.where(step >= 2 * nh * l, jnp.where(is_mine, halves[2 * l], halves[2 * l + 1]), g)
        return (g,) + _adamw_math(w, g, m, v)
    ins = [("layers", a_) for a_ in (w, m, v)]
    ins += [(h, b, blocks_of(l)) for l in range(depth) for h in (mine[l], theirs[l])]
    res = _rows(fn, ins, [jnp.full((1, _LANES), c, F32)], [(b, F32)] * 4, tile=tile, name=name)
    return [r.reshape(w.shape) for r in res]


_BIG = ("w_in", "w_out", "w_mlp_in", "w_mlp_out")
_SMALL = ("ln1_g", "conv_b", "dt_bias", "a_log", "d_skip", "attn_norm_g", "ssd_norm_g", "ln2_g", "final_norm_g")
_ORDER = ("ln1_g", "w_in", "conv_w", "conv_b", "dt_bias", "a_log", "d_skip", "attn_norm_g", "ssd_norm_g", "w_out", "ln2_g",
          "w_mlp_in", "w_mlp_out", "final_norm_g")


def _pack(parts, rows):
    flat = jnp.concatenate([p.reshape(-1) for p in parts])
    return jnp.pad(flat, (0, rows * _LANES - flat.shape[0])).reshape(rows, _LANES)


def _unpack(buf, like):
    flat, out, o = buf.reshape(-1), [], 0
    for p in like:
        out.append(flat[o:o + p.size].reshape(p.shape))
        o += p.size
    return out


def kernel(x, ln1_g, w_in, conv_w, conv_b, dt_bias, a_log, d_skip, attn_norm_g, ssd_norm_g, w_out, ln2_g, w_mlp_in, w_mlp_out, final_norm_g, loss_target, m_ln1_g, m_w_in, m_conv_w, m_conv_b, m_dt_bias, m_a_log, m_d_skip, m_attn_norm_g, m_ssd_norm_g, m_w_out, m_ln2_g, m_w_mlp_in, m_w_mlp_out, m_final_norm_g, v_ln1_g, v_w_in, v_conv_w, v_conv_b, v_dt_bias, v_a_log, v_d_skip, v_attn_norm_g, v_ssd_norm_g, v_w_out, v_ln2_g, v_w_mlp_in, v_w_mlp_out, v_final_norm_g):
    w = dict(ln1_g=ln1_g, w_in=w_in, conv_w=conv_w, conv_b=conv_b, dt_bias=dt_bias, a_log=a_log, d_skip=d_skip,
             attn_norm_g=attn_norm_g, ssd_norm_g=ssd_norm_g, w_out=w_out, ln2_g=ln2_g, w_mlp_in=w_mlp_in, w_mlp_out=w_mlp_out,
             final_norm_g=final_norm_g)
    m = dict(ln1_g=m_ln1_g, w_in=m_w_in, conv_w=m_conv_w, conv_b=m_conv_b, dt_bias=m_dt_bias, a_log=m_a_log, d_skip=m_d_skip,
             attn_norm_g=m_attn_norm_g, ssd_norm_g=m_ssd_norm_g, w_out=m_w_out, ln2_g=m_ln2_g, w_mlp_in=m_w_mlp_in,
             w_mlp_out=m_w_mlp_out, final_norm_g=m_final_norm_g)
    v = dict(ln1_g=v_ln1_g, w_in=v_w_in, conv_w=v_conv_w, conv_b=v_conv_b, dt_bias=v_dt_bias, a_log=v_a_log, d_skip=v_d_skip,
             attn_norm_g=v_attn_norm_g, ssd_norm_g=v_ssd_norm_g, w_out=v_w_out, ln2_g=v_ln2_g, w_mlp_in=v_w_mlp_in,
             w_mlp_out=v_w_mlp_out, final_norm_g=v_final_norm_g)
    depth, d_model = ln1_g.shape
    n_chips = _CHIPS
    c = lax.axis_index("c")
    chip = 2 * lax.axis_index("x") + lax.axis_index("y")
    cch = conv_w.shape[2] * n_chips

    cw = _exchange8(conv_w.reshape(depth * _CONV_K, -1), reduce=False, name="gather_conv_w")[0::2]
    conv_full = cw.reshape(n_chips, depth, _CONV_K, -1).transpose(1, 2, 0, 3).reshape(depth, _CONV_K, cch)
    own = [w[n].astype(_BF) for n in _BIG]

    def start_gather(tag, items, after):
        lands = [_SDS((n_chips, *own[i].shape[1:]), _BF) for i, _ in items]
        return _send_start(f"gather_start_{tag}", own, lands, _gather_plan(items), n_chips * len(items), after)

    def finish_gather(tag, handle, items, after):
        landed = _send_wait(f"gather_wait_{tag}", handle, after)
        return {_BIG[i]: g for (i, _), g in zip(items, landed)}

    def layer_weights(l, blocks):
        p = {}
        if 'w_in' in blocks:
            p['w_in'] = _w_in_from_chips(blocks['w_in'], cch, name=f"w_in_regroup_{l}")
        if 'w_out' in blocks:
            p['w_out'] = blocks['w_out'].reshape(-1, d_model)
            p['w_mlp_in'] = blocks['w_mlp_in']
            p['w_mlp_out'] = blocks['w_mlp_out'].reshape(-1, d_model)
        return p

    groups = dict(a=[(0, 0)], b=[(1, 0), (2, 0), (3, 0)], c=[(0, 1)], d=[(1, 1), (2, 1), (3, 1)])
    handles, token = {}, conv_full

    half_in = own[0].shape[1] // 2

    def rows_of(ref, who):
        return ref.at[pl.ds(who * half_in, half_in)]

    def plan_a(x_, y_, c_, chips, ins, lands):
        k = 2 * x_ + y_
        sends = [(rows_of(ins[0].at[0], c_), rows_of(lands[0].at[k], c_), (px, py, c_)) for px, py in chips]
        sends.append((ins[0].at[0], lands[0].at[k], (x_, y_, 1 - c_)))
        return sends, [rows_of(lands[0].at[2 * px + py], c_) for px, py in chips] + [lands[0].at[k]]

    def plan_pass(x_, y_, c_, chips, ins, lands):
        sends = [(rows_of(lands[0].at[2 * px + py], c_),) * 2 + ((x_, y_, 1 - c_),) for px, py in chips]
        return sends, [rows_of(lands[0].at[2 * px + py], 1 - c_) for px, py in chips]
    handles["a"], token = _send_start("gather_start_a", own[:1], [_SDS((n_chips, *own[0].shape[1:]), _BF)], plan_a, n_chips, token)
    for tag, items in list(groups.items())[1:]:
        handles[tag], token = start_gather(tag, items, token)
    landed = _send_wait("gather_land_a", handles["a"], token)
    handles["a"], token = _send_start("gather_pass_a", [], landed, plan_pass, 3, landed[0])
    layers = [{n: w[n][l] for n in _SMALL[:-1]} for l in range(depth)]
    for l in range(depth):
        layers[l]['conv_w'] = conv_full[l]

    layers[0].update(layer_weights(0, finish_gather("a", handles["a"], groups["a"], token)))
    mix, sv0 = _layer_fwd(x[0], layers[0], 0)
    layers[0].update(layer_weights(0, finish_gather("b", handles["b"], groups["b"], mix)))
    h = _layer_fwd_mlp(layers[0], sv0, 0)
    layers[1].update(layer_weights(1, finish_gather("c", handles["c"], groups["c"], h)))
    mix, sv1 = _layer_fwd(h, layers[1], 1)
    layers[1].update(layer_weights(1, finish_gather("d", handles["d"], groups["d"], mix)))
    h = _layer_fwd_mlp(layers[1], sv1, 1)
    saved = [sv0, sv1]

    def by_chip(g, name):
        if name in ("w_mlp_in", "w_in"):
            return g
        return g.reshape(n_chips, -1, d_model)

    pending = []

    def sender(l):
        def send(names, g):
            srcs = [by_chip(g[n], n) for n in names]
            halves = [s.shape[1] // 2 for s in srcs]
            lands = [_SDS((len(_FLIPS), hf, s.shape[2]), _BF) for s, hf in zip(srcs, halves)]
            handle, tok = _send_start(f"grad_start_{names[-1]}_{l}", srcs, lands, _reduce_plan(halves), len(_FLIPS) * len(srcs), srcs[0])
            pending.append((l, names, srcs, handle))
            return tok
        return send

    dx, dxb, g_final, loss_part = _loss_bwd(h, final_norm_g, loss_target[0])
    grads, after = [None] * depth, None
    for l in reversed(range(depth)):
        dx, dxb, grads[l] = _layer_bwd(dx, dxb, layers[l], saved[l], l, sender(l), after)
        after = dx
    landed_of = {}
    sent_in = {(n, l): (gi, j) for gi, (l, names, _, _) in enumerate(pending) for j, n in enumerate(names)}

    def landed_for(gi, after):
        if gi not in landed_of:
            l, names, _, handle = pending[gi]
            landed_of[gi] = _send_wait(f"grad_wait_{names[-1]}_{l}", handle, after)
        return landed_of[gi]

    red, delta, new_m, new_v = {}, {}, {}, {}
    after = dx
    for n in ("w_mlp_out", "w_mlp_in", "w_out", "w_in"):
        mine = []
        for l in range(depth):
            gi, j = sent_in[(n, l)]
            got = landed_for(gi, after)[j]
            mine.append(_sum_owned([pending[gi][2][j]], [got], c, chip, [f"{n}_{l}"])[0])
        theirs = _share_halves(mine, name=f"grad_share_{n}")
        red[n], delta[n], new_m[n], new_v[n] = _adamw_owned(w[n], mine, theirs, m[n], v[n], c, name=f"adamw_{n}")
        after = delta[n]

    small = {n: jnp.stack([grads[l][n] for l in range(depth)]) for n in _SMALL[:-1] + ("conv_w",)}
    small["final_norm_g"] = g_final
    parts = [loss_part.reshape(1)] + [small[n] for n in _SMALL + ("conv_w",)]
    rows = -(-sum(p.size for p in parts) // 1024) * 8
    tot = _unpack(_exchange8(_pack(parts, rows), reduce=True, after=red[_BIG[0]], name="allreduce_small"), parts)
    loss = tot[0][0]
    red.update(zip(_SMALL + ("conv_w",), tot[1:]))
    red["conv_w"] = lax.dynamic_index_in_dim(red["conv_w"].reshape(depth, _CONV_K, n_chips, -1), chip, axis=2, keepdims=False)

    names = _SMALL + ("conv_w",)
    like = [w[n] for n in names]
    srows = -(-sum(p.size for p in like) // 1024) * 8
    res = _adamw(*[_pack([d[n] for n in names], srows) for d in (w, red, m, v)], name="adamw_small")
    for dst, buf in zip((delta, new_m, new_v), res):
        dst.update(zip(names, _unpack(buf, like)))
    return (loss, dx[None], *[red[n] for n in _ORDER], *[delta[n] for n in _ORDER], *[new_m[n] for n in _ORDER],
            *[new_v[n] for n in _ORDER])
```
